```python
import math
import jax, jax.numpy as jnp
from jax import lax
import numpy as np

D_MODEL = 1024
BATCH = 8
SEQ = 8192
DEPTH = 1

N_MEM = 256
MAX_POS_OFFSET = 1024
BLOCK = 128
RMS_EPS = 1e-6
NEG_INF = -1e30

MLA_HEADS = 8
MLA_NOPE = 64
MLA_ROPE = 32
MLA_V = 64
MLA_QK_DIM = MLA_NOPE + MLA_ROPE
MLA_Q_RANK = 384
MLA_KV_RANK = 256
ROPE_THETA = 10000.0
MLA_WIDTH = MLA_HEADS * MLA_V

DIL_PAIRS = ((128, 1), (512, 4), (2048, 16))
DIL_GROUPS = 3
DIL_HEADS_PER_GROUP = 4
DIL_HEADS = DIL_GROUPS * DIL_HEADS_PER_GROUP
DIL_HEAD_DIM = 128
DIL_WIDTH = DIL_HEADS_PER_GROUP * DIL_HEAD_DIM

MEM_HEADS = 4
MEM_HEAD_DIM = 128
MEM_WIDTH = MEM_HEADS * MEM_HEAD_DIM

N_BRANCH = 3

D_FF = 2816
CONV_WIDTH = 3

OFF_Q = MLA_Q_RANK
OFF_KV = OFF_Q + MLA_KV_RANK
OFF_KR = OFF_KV + MLA_ROPE
OFF_DIL = OFF_KR + 3 * DIL_HEADS * DIL_HEAD_DIM
OFF_MEMQ = OFF_DIL + MEM_WIDTH
D_IN = OFF_MEMQ + N_BRANCH * D_MODEL

kernel_name = "hybrid_mla_dilated_memory_convffn"


def _rms_norm(x, g):
    xf = x.astype(jnp.float32)
    y = xf * lax.rsqrt(jnp.mean(xf * xf, axis=-1, keepdims=True) + RMS_EPS)
    return (y * g.astype(jnp.float32)).astype(x.dtype)


def _rope(t, positions):
    half = t.shape[-1] // 2
    inv_freq = ROPE_THETA ** (-jnp.arange(half, dtype=jnp.float32) / half)
    ang = positions.astype(jnp.float32)[:, :, None, None] * inv_freq
    cos, sin = jnp.cos(ang), jnp.sin(ang)
    t1 = t[..., :half].astype(jnp.float32)
    t2 = t[..., half:].astype(jnp.float32)
    return jnp.concatenate([t1 * cos - t2 * sin, t2 * cos + t1 * sin], axis=-1).astype(t.dtype)


def _alibi_slopes(n):
    return jnp.exp2(-8.0 * jnp.arange(1, n + 1, dtype=jnp.float32) / n)


def _mla(c_q, c_kv, k_rope, positions, q_norm, w_uq, kv_norm, w_ukv):
    B, S, _ = c_q.shape
    q = (_rms_norm(c_q, q_norm) @ w_uq).reshape(B, S, MLA_HEADS, MLA_QK_DIM)
    q = jnp.concatenate([q[..., :MLA_NOPE], _rope(q[..., MLA_NOPE:], positions)], axis=-1)
    kv = (_rms_norm(c_kv, kv_norm) @ w_ukv).reshape(B, S, MLA_HEADS, MLA_NOPE + MLA_V)
    k_pe = jnp.broadcast_to(_rope(k_rope[:, :, None, :], positions), (B, S, MLA_HEADS, MLA_ROPE))
    k = jnp.concatenate([kv[..., :MLA_NOPE], k_pe], axis=-1)
    v = kv[..., MLA_NOPE:]
    n_blk = S // BLOCK
    q_blocks = (q * MLA_QK_DIM ** -0.5).reshape(B, n_blk, BLOCK, MLA_HEADS, MLA_QK_DIM).transpose(1, 0, 2, 3, 4)
    key_idx = jnp.arange(S)

    def attend(args):
        q_blk, blk = args
        s = jnp.einsum('bqhd,bkhd->bhqk', q_blk, k).astype(jnp.float32)
        q_idx = blk * BLOCK + jnp.arange(BLOCK)
        s = jnp.where(key_idx[None, :] <= q_idx[:, None], s, NEG_INF)
        p = jax.nn.softmax(s, axis=-1).astype(v.dtype)
        return jnp.einsum('bhqk,bkhd->bqhd', p, v)

    o = lax.map(attend, (q_blocks, jnp.arange(n_blk)))
    return o.transpose(1, 0, 2, 3, 4).reshape(B, S, MLA_WIDTH)


def _dilated_group(q, k, v, window, dilation, slopes):
    B, S, H, dh = q.shape
    span = window // dilation
    L = S // dilation
    n_blk = -(-L // BLOCK)
    Lp = n_blk * BLOCK

    def to_blocks(t):
        t = t.reshape(B, L, dilation, H, dh).transpose(0, 2, 3, 1, 4)
        t = jnp.pad(t, ((0, 0), (0, 0), (0, 0), (0, Lp - L), (0, 0)))
        return t.reshape(B, dilation, H, n_blk, BLOCK, dh)

    def band(t):
        prev = jnp.pad(t, ((0, 0), (0, 0), (0, 0), (1, 0), (0, 0), (0, 0)))[:, :, :, :-1]
        return jnp.concatenate([prev, t], axis=4)

    qb = to_blocks(q) * dh ** -0.5
    kb = band(to_blocks(k))
    vb = band(to_blocks(v))
    s = jnp.einsum('bdhnqe,bdhnke->bdhnqk', qb, kb).astype(jnp.float32)
    dist = jnp.arange(BLOCK)[:, None] + BLOCK - jnp.arange(2 * BLOCK)[None, :]
    key_sub = jnp.arange(n_blk)[:, None, None] * BLOCK - BLOCK + jnp.arange(2 * BLOCK)[None, None, :]
    valid = (dist >= 0) & (dist <= span) & (key_sub >= 0)
    alibi = -slopes.astype(jnp.float32)[:, None, None, None] * (dist * dilation).astype(jnp.float32)
    s = jnp.where(valid, s + alibi, NEG_INF)
    m = jnp.max(s, axis=-1, keepdims=True)
    e = jnp.exp(s - m)
    den = jnp.sum(e, axis=-1, keepdims=True)
    o = jnp.einsum('bdhnqk,bdhnke->bdhnqe', (e / den).astype(v.dtype), vb)
    lse = (m + jnp.log(den))[..., 0]
    o = o.reshape(B, dilation, H, Lp, dh)[:, :, :, :L].transpose(0, 3, 1, 2, 4).reshape(B, S, H, dh)
    lse = lse.reshape(B, dilation, H, Lp)[..., :L].transpose(0, 3, 1, 2).reshape(B, S, H)
    return o, lse


def _dilated_mixture(dil_qkv):
    B, S, _ = dil_qkv.shape
    qkv = dil_qkv.reshape(B, S, 3, DIL_GROUPS, DIL_HEADS_PER_GROUP, DIL_HEAD_DIM)
    slopes = _alibi_slopes(DIL_HEADS).reshape(DIL_HEADS_PER_GROUP, DIL_GROUPS).T
    outs, lses = [], []
    for g, (window, dilation) in enumerate(DIL_PAIRS):
        o, lse = _dilated_group(qkv[:, :, 0, g], qkv[:, :, 1, g], qkv[:, :, 2, g], window, dilation, slopes[g])
        outs.append(o)
        lses.append(lse)
    w = jax.nn.softmax(jnp.stack(lses, axis=0), axis=0)
    o_stack = jnp.stack(outs, axis=0)
    o = jnp.sum(w[..., None].astype(o_stack.dtype) * o_stack, axis=0)
    return o.reshape(B, S, DIL_WIDTH)


def _mem_attention(q, mem, g_mem, w_mem_kv):
    B, S, _ = q.shape
    M = mem.shape[1]
    kv = (_rms_norm(mem, g_mem) @ w_mem_kv).reshape(B, M, 2, MEM_HEADS, MEM_HEAD_DIM)
    qh = q.reshape(B, S, MEM_HEADS, MEM_HEAD_DIM) * MEM_HEAD_DIM ** -0.5
    s = jnp.einsum('bshd,bmhd->bhsm', qh, kv[:, :, 0]).astype(jnp.float32)
    p = jax.nn.softmax(s, axis=-1).astype(q.dtype)
    return jnp.einsum('bhsm,bmhd->bshd', p, kv[:, :, 1]).reshape(B, S, MEM_WIDTH)


def _conv_ffn(h, w_up, conv_w, conv_b, w_down):
    S = h.shape[1]
    u = h @ w_up
    u_pad = jnp.pad(u, ((0, 0), (CONV_WIDTH - 1, 0), (0, 0)))
    z = conv_b + conv_w[0] * u_pad[:, 0:S]
    for j in range(1, CONV_WIDTH):
        z = z + conv_w[j] * u_pad[:, j:j + S]
    gate, val = z[..., :D_FF], z[..., D_FF:]
    return (jax.nn.silu(gate) * val) @ w_down


def _layer(x, mem, positions, g_pre_mix, w_in, b_gate, mla_q_norm, w_uq, mla_kv_norm, w_ukv, g_mem, w_mem_kv,
           w_br_mla, w_br_dil, w_br_mem, w_o, g_post_mix, g_pre_ffn, w_ffn_up, conv_w, conv_b, w_ffn_down, g_post_ffn):
    B, S, _ = x.shape
    h = _rms_norm(x, g_pre_mix)
    proj = h @ w_in
    y_mla = _mla(proj[..., :OFF_Q], proj[..., OFF_Q:OFF_KV], proj[..., OFF_KV:OFF_KR], positions,
                 mla_q_norm, w_uq, mla_kv_norm, w_ukv)
    y_dil = _dilated_mixture(proj[..., OFF_KR:OFF_DIL])
    y_mem = _mem_attention(proj[..., OFF_DIL:OFF_MEMQ], mem, g_mem, w_mem_kv)
    gates = jax.nn.sigmoid((proj[..., OFF_MEMQ:] + b_gate).astype(jnp.float32)).astype(x.dtype)
    gates = gates.reshape(B, S, N_BRANCH, D_MODEL)
    merged = (gates[:, :, 0] * (y_mla @ w_br_mla)
              + gates[:, :, 1] * (y_dil @ w_br_dil)
              + gates[:, :, 2] * (y_mem @ w_br_mem))
    x = x + _rms_norm(merged @ w_o, g_post_mix)
    h2 = _rms_norm(x, g_pre_ffn)
    x = x + _rms_norm(_conv_ffn(h2, w_ffn_up, conv_w, conv_b, w_ffn_down), g_post_ffn)
    return x


def _fwd_setup_inputs(seed: int = 0) -> dict:
    key = jax.random.key(seed)
    ks = jax.random.split(key, 24)
    f32 = jnp.float32

    def dense(k, fan_in, fan_out):
        return jax.random.normal(k, (DEPTH, fan_in, fan_out), f32) * fan_in ** -0.5

    def gain(k, n):
        return 1.0 + 0.05 * jax.random.normal(k, (DEPTH, n), f32)

    x = jax.random.normal(ks[0], (BATCH, SEQ, D_MODEL), f32)
    mem = jax.random.normal(ks[1], (BATCH, N_MEM, D_MODEL), f32)
    offset = jax.random.randint(ks[2], (BATCH, 1), 0, MAX_POS_OFFSET, dtype=jnp.int32)
    positions = (offset + jnp.arange(SEQ, dtype=jnp.int32)[None, :]).astype(jnp.int32)
    return {
        "x": x,
        "mem": mem,
        "positions": positions,
        "g_pre_mix": gain(ks[3], D_MODEL),
        "w_in": dense(ks[4], D_MODEL, D_IN),
        "b_gate": 0.1 * jax.random.normal(ks[5], (DEPTH, N_BRANCH * D_MODEL), f32),
        "mla_q_norm": gain(ks[6], MLA_Q_RANK),
        "w_uq": dense(ks[7], MLA_Q_RANK, MLA_HEADS * MLA_QK_DIM),
        "mla_kv_norm": gain(ks[8], MLA_KV_RANK),
        "w_ukv": dense(ks[9], MLA_KV_RANK, MLA_HEADS * (MLA_NOPE + MLA_V)),
        "g_mem": gain(ks[10], D_MODEL),
        "w_mem_kv": dense(ks[11], D_MODEL, 2 * MEM_WIDTH),
        "w_br_mla": dense(ks[12], MLA_WIDTH, D_MODEL),
        "w_br_dil": dense(ks[13], DIL_WIDTH, D_MODEL),
        "w_br_mem": dense(ks[14], MEM_WIDTH, D_MODEL),
        "w_o": dense(ks[15], D_MODEL, D_MODEL),
        "g_post_mix": gain(ks[16], D_MODEL),
        "g_pre_ffn": gain(ks[17], D_MODEL),
        "w_ffn_up": dense(ks[18], D_MODEL, 2 * D_FF),
        "conv_w": jax.random.normal(ks[19], (DEPTH, CONV_WIDTH, 2 * D_FF), f32) * CONV_WIDTH ** -0.5,
        "conv_b": 0.01 * jax.random.normal(ks[20], (DEPTH, 2 * D_FF), f32),
        "w_ffn_down": dense(ks[21], D_FF, D_MODEL),
        "g_post_ffn": gain(ks[22], D_MODEL),
    }


def _fwd_reference(x, mem, positions, g_pre_mix, w_in, b_gate, mla_q_norm, w_uq, mla_kv_norm, w_ukv, g_mem, w_mem_kv,
              w_br_mla, w_br_dil, w_br_mem, w_o, g_post_mix, g_pre_ffn, w_ffn_up, conv_w, conv_b, w_ffn_down,
              g_post_ffn):
    for l in range(DEPTH):
        x = _layer(x, mem, positions, g_pre_mix[l], w_in[l], b_gate[l], mla_q_norm[l], w_uq[l], mla_kv_norm[l],
                   w_ukv[l], g_mem[l], w_mem_kv[l], w_br_mla[l], w_br_dil[l], w_br_mem[l], w_o[l], g_post_mix[l],
                   g_pre_ffn[l], w_ffn_up[l], conv_w[l], conv_b[l], w_ffn_down[l], g_post_ffn[l])
    return x


import jax as _jax
import jax.numpy as _jnp

TWIN_FORMAT = 'train_step'
FWD_PARAMS = ['x', 'mem', 'positions', 'g_pre_mix', 'w_in', 'b_gate', 'mla_q_norm', 'w_uq', 'mla_kv_norm', 'w_ukv', 'g_mem', 'w_mem_kv', 'w_br_mla', 'w_br_dil', 'w_br_mem', 'w_o', 'g_post_mix', 'g_pre_ffn', 'w_ffn_up', 'conv_w', 'conv_b', 'w_ffn_down', 'g_post_ffn']
TWIN_WEIGHTS = ['g_pre_mix', 'w_in', 'b_gate', 'mla_q_norm', 'w_uq', 'mla_kv_norm', 'w_ukv', 'g_mem', 'w_mem_kv', 'w_br_mla', 'w_br_dil', 'w_br_mem', 'w_o', 'g_post_mix', 'g_pre_ffn', 'w_ffn_up', 'conv_w', 'conv_b', 'w_ffn_down', 'g_post_ffn']
TWIN_DIFF_INPUT = 'x'
TWIN_INPUTS = ['x', 'mem', 'positions', 'g_pre_mix', 'w_in', 'b_gate', 'mla_q_norm', 'w_uq', 'mla_kv_norm', 'w_ukv', 'g_mem', 'w_mem_kv', 'w_br_mla', 'w_br_dil', 'w_br_mem', 'w_o', 'g_post_mix', 'g_pre_ffn', 'w_ffn_up', 'conv_w', 'conv_b', 'w_ffn_down', 'g_post_ffn', 'loss_target', 'm_g_pre_mix', 'm_w_in', 'm_b_gate', 'm_mla_q_norm', 'm_w_uq', 'm_mla_kv_norm', 'm_w_ukv', 'm_g_mem', 'm_w_mem_kv', 'm_w_br_mla', 'm_w_br_dil', 'm_w_br_mem', 'm_w_o', 'm_g_post_mix', 'm_g_pre_ffn', 'm_w_ffn_up', 'm_conv_w', 'm_conv_b', 'm_w_ffn_down', 'm_g_post_ffn', 'v_g_pre_mix', 'v_w_in', 'v_b_gate', 'v_mla_q_norm', 'v_w_uq', 'v_mla_kv_norm', 'v_w_ukv', 'v_g_mem', 'v_w_mem_kv', 'v_w_br_mla', 'v_w_br_dil', 'v_w_br_mem', 'v_w_o', 'v_g_post_mix', 'v_g_pre_ffn', 'v_w_ffn_up', 'v_conv_w', 'v_conv_b', 'v_w_ffn_down', 'v_g_post_ffn']
TWIN_OUTPUTS = ['loss', 'grad_x', 'grad_g_pre_mix', 'grad_w_in', 'grad_b_gate', 'grad_mla_q_norm', 'grad_w_uq', 'grad_mla_kv_norm', 'grad_w_ukv', 'grad_g_mem', 'grad_w_mem_kv', 'grad_w_br_mla', 'grad_w_br_dil', 'grad_w_br_mem', 'grad_w_o', 'grad_g_post_mix', 'grad_g_pre_ffn', 'grad_w_ffn_up', 'grad_conv_w', 'grad_conv_b', 'grad_w_ffn_down', 'grad_g_post_ffn', 'delta_g_pre_mix', 'delta_w_in', 'delta_b_gate', 'delta_mla_q_norm', 'delta_w_uq', 'delta_mla_kv_norm', 'delta_w_ukv', 'delta_g_mem', 'delta_w_mem_kv', 'delta_w_br_mla', 'delta_w_br_dil', 'delta_w_br_mem', 'delta_w_o', 'delta_g_post_mix', 'delta_g_pre_ffn', 'delta_w_ffn_up', 'delta_conv_w', 'delta_conv_b', 'delta_w_ffn_down', 'delta_g_post_ffn', 'new_m_g_pre_mix', 'new_m_w_in', 'new_m_b_gate', 'new_m_mla_q_norm', 'new_m_w_uq', 'new_m_mla_kv_norm', 'new_m_w_ukv', 'new_m_g_mem', 'new_m_w_mem_kv', 'new_m_w_br_mla', 'new_m_w_br_dil', 'new_m_w_br_mem', 'new_m_w_o', 'new_m_g_post_mix', 'new_m_g_pre_ffn', 'new_m_w_ffn_up', 'new_m_conv_w', 'new_m_conv_b', 'new_m_w_ffn_down', 'new_m_g_post_ffn', 'new_v_g_pre_mix', 'new_v_w_in', 'new_v_b_gate', 'new_v_mla_q_norm', 'new_v_w_uq', 'new_v_mla_kv_norm', 'new_v_w_ukv', 'new_v_g_mem', 'new_v_w_mem_kv', 'new_v_w_br_mla', 'new_v_w_br_dil', 'new_v_w_br_mem', 'new_v_w_o', 'new_v_g_post_mix', 'new_v_g_pre_ffn', 'new_v_w_ffn_up', 'new_v_conv_w', 'new_v_conv_b', 'new_v_w_ffn_down', 'new_v_g_post_ffn']
TWIN_LEAF_KINDS = {'loss': 'loss', 'grad_x': 'grad_x', 'grad_g_pre_mix': 'grad_w', 'grad_w_in': 'grad_w', 'grad_b_gate': 'grad_w', 'grad_mla_q_norm': 'grad_w', 'grad_w_uq': 'grad_w', 'grad_mla_kv_norm': 'grad_w', 'grad_w_ukv': 'grad_w', 'grad_g_mem': 'grad_w', 'grad_w_mem_kv': 'grad_w', 'grad_w_br_mla': 'grad_w', 'grad_w_br_dil': 'grad_w', 'grad_w_br_mem': 'grad_w', 'grad_w_o': 'grad_w', 'grad_g_post_mix': 'grad_w', 'grad_g_pre_ffn': 'grad_w', 'grad_w_ffn_up': 'grad_w', 'grad_conv_w': 'grad_w', 'grad_conv_b': 'grad_w', 'grad_w_ffn_down': 'grad_w', 'grad_g_post_ffn': 'grad_w', 'delta_g_pre_mix': 'delta_w', 'delta_w_in': 'delta_w', 'delta_b_gate': 'delta_w', 'delta_mla_q_norm': 'delta_w', 'delta_w_uq': 'delta_w', 'delta_mla_kv_norm': 'delta_w', 'delta_w_ukv': 'delta_w', 'delta_g_mem': 'delta_w', 'delta_w_mem_kv': 'delta_w', 'delta_w_br_mla': 'delta_w', 'delta_w_br_dil': 'delta_w', 'delta_w_br_mem': 'delta_w', 'delta_w_o': 'delta_w', 'delta_g_post_mix': 'delta_w', 'delta_g_pre_ffn': 'delta_w', 'delta_w_ffn_up': 'delta_w', 'delta_conv_w': 'delta_w', 'delta_conv_b': 'delta_w', 'delta_w_ffn_down': 'delta_w', 'delta_g_post_ffn': 'delta_w', 'new_m_g_pre_mix': 'new_m', 'new_m_w_in': 'new_m', 'new_m_b_gate': 'new_m', 'new_m_mla_q_norm': 'new_m', 'new_m_w_uq': 'new_m', 'new_m_mla_kv_norm': 'new_m', 'new_m_w_ukv': 'new_m', 'new_m_g_mem': 'new_m', 'new_m_w_mem_kv': 'new_m', 'new_m_w_br_mla': 'new_m', 'new_m_w_br_dil': 'new_m', 'new_m_w_br_mem': 'new_m', 'new_m_w_o': 'new_m', 'new_m_g_post_mix': 'new_m', 'new_m_g_pre_ffn': 'new_m', 'new_m_w_ffn_up': 'new_m', 'new_m_conv_w': 'new_m', 'new_m_conv_b': 'new_m', 'new_m_w_ffn_down': 'new_m', 'new_m_g_post_ffn': 'new_m', 'new_v_g_pre_mix': 'new_v', 'new_v_w_in': 'new_v', 'new_v_b_gate': 'new_v', 'new_v_mla_q_norm': 'new_v', 'new_v_w_uq': 'new_v', 'new_v_mla_kv_norm': 'new_v', 'new_v_w_ukv': 'new_v', 'new_v_g_mem': 'new_v', 'new_v_w_mem_kv': 'new_v', 'new_v_w_br_mla': 'new_v', 'new_v_w_br_dil': 'new_v', 'new_v_w_br_mem': 'new_v', 'new_v_w_o': 'new_v', 'new_v_g_post_mix': 'new_v', 'new_v_g_pre_ffn': 'new_v', 'new_v_w_ffn_up': 'new_v', 'new_v_conv_w': 'new_v', 'new_v_conv_b': 'new_v', 'new_v_w_ffn_down': 'new_v', 'new_v_g_post_ffn': 'new_v'}


def _forward(args):
    return _fwd_reference(*[args[k] for k in FWD_PARAMS])


def _output_shape():
    def fwd():
        inp = _fwd_setup_inputs(0)
        return _fwd_reference(*[inp[k] for k in FWD_PARAMS])
    out = _jax.eval_shape(fwd)
    return out.shape, out.dtype

N_MICROBATCH = 1
ADAM_LR = 0.001
ADAM_B1 = 0.9
ADAM_B2 = 0.999
ADAM_EPS = 1e-08
ADAM_WD = 0.01
ADAM_STEP = 10
PER_EXAMPLE_BATCH_AXIS = {'x': 0, 'mem': 0, 'positions': 0, 'loss_target': 0}
SHARED_INPUTS = []
_WEIGHT_DTYPES = {'g_pre_mix': _jnp.float32, 'w_in': _jnp.float32, 'b_gate': _jnp.float32, 'mla_q_norm': _jnp.float32, 'w_uq': _jnp.float32, 'mla_kv_norm': _jnp.float32, 'w_ukv': _jnp.float32, 'g_mem': _jnp.float32, 'w_mem_kv': _jnp.float32, 'w_br_mla': _jnp.float32, 'w_br_dil': _jnp.float32, 'w_br_mem': _jnp.float32, 'w_o': _jnp.float32, 'g_post_mix': _jnp.float32, 'g_pre_ffn': _jnp.float32, 'w_ffn_up': _jnp.float32, 'conv_w': _jnp.float32, 'conv_b': _jnp.float32, 'w_ffn_down': _jnp.float32, 'g_post_ffn': _jnp.float32}
MOMENT_SCALE = {'g_pre_mix': 1.196374e+00, 'w_in': 3.799088e-01, 'b_gate': 2.613168e-01, 'mla_q_norm': 4.306314e-01, 'w_uq': 3.106307e-01, 'mla_kv_norm': 8.406294e-01, 'w_ukv': 3.898504e-01, 'g_mem': 4.073663e-01, 'w_mem_kv': 4.151835e-01, 'w_br_mla': 3.241253e-01, 'w_br_dil': 1.033860e+00, 'w_br_mem': 3.425361e-01, 'w_o': 1.074297e+00, 'g_post_mix': 6.400611e+01, 'g_pre_ffn': 7.796651e-01, 'w_ffn_up': 3.472905e-01, 'conv_w': 3.834207e-01, 'conv_b': 9.993223e-01, 'w_ffn_down': 6.958392e-01, 'g_post_ffn': 6.398433e+01}


def _to_microbatches(a, axis):
    t = _jnp.moveaxis(a, axis, 0)
    t = t.reshape((N_MICROBATCH, t.shape[0] // N_MICROBATCH) + t.shape[1:])
    return _jnp.moveaxis(t, 1, axis + 1)


def setup_inputs(seed: int = 0) -> dict:
    inp = _fwd_setup_inputs(seed)
    key = _jax.random.fold_in(_jax.random.key(seed), 7919)
    shape, _ = _output_shape()
    out = dict(inp)
    out["loss_target"] = _jax.random.normal(_jax.random.fold_in(key, 0), shape, _jnp.float32)
    for i, name in enumerate(TWIN_WEIGHTS):
        w = inp[name].astype(_jnp.float32)
        if MOMENT_SCALE is None:
            s = _jnp.sqrt(_jnp.mean(_jnp.square(w)) + 1e-30)
        else:
            s = MOMENT_SCALE[name]
        km, kv = _jax.random.split(_jax.random.fold_in(key, i + 1))
        out[name] = w
        out["m_" + name] = s * _jax.random.normal(km, w.shape, _jnp.float32)
        out["v_" + name] = (s * s) * _jax.random.uniform(kv, w.shape, _jnp.float32, 0.5, 1.5)
    if N_MICROBATCH > 1:
        for name, axis in PER_EXAMPLE_BATCH_AXIS.items():
            out[name] = _to_microbatches(out[name], axis)
    return {'x': out['x'], 'mem': out['mem'], 'positions': out['positions'], 'g_pre_mix': out['g_pre_mix'], 'w_in': out['w_in'], 'b_gate': out['b_gate'], 'mla_q_norm': out['mla_q_norm'], 'w_uq': out['w_uq'], 'mla_kv_norm': out['mla_kv_norm'], 'w_ukv': out['w_ukv'], 'g_mem': out['g_mem'], 'w_mem_kv': out['w_mem_kv'], 'w_br_mla': out['w_br_mla'], 'w_br_dil': out['w_br_dil'], 'w_br_mem': out['w_br_mem'], 'w_o': out['w_o'], 'g_post_mix': out['g_post_mix'], 'g_pre_ffn': out['g_pre_ffn'], 'w_ffn_up': out['w_ffn_up'], 'conv_w': out['conv_w'], 'conv_b': out['conv_b'], 'w_ffn_down': out['w_ffn_down'], 'g_post_ffn': out['g_post_ffn'], 'loss_target': out['loss_target'], 'm_g_pre_mix': out['m_g_pre_mix'], 'm_w_in': out['m_w_in'], 'm_b_gate': out['m_b_gate'], 'm_mla_q_norm': out['m_mla_q_norm'], 'm_w_uq': out['m_w_uq'], 'm_mla_kv_norm': out['m_mla_kv_norm'], 'm_w_ukv': out['m_w_ukv'], 'm_g_mem': out['m_g_mem'], 'm_w_mem_kv': out['m_w_mem_kv'], 'm_w_br_mla': out['m_w_br_mla'], 'm_w_br_dil': out['m_w_br_dil'], 'm_w_br_mem': out['m_w_br_mem'], 'm_w_o': out['m_w_o'], 'm_g_post_mix': out['m_g_post_mix'], 'm_g_pre_ffn': out['m_g_pre_ffn'], 'm_w_ffn_up': out['m_w_ffn_up'], 'm_conv_w': out['m_conv_w'], 'm_conv_b': out['m_conv_b'], 'm_w_ffn_down': out['m_w_ffn_down'], 'm_g_post_ffn': out['m_g_post_ffn'], 'v_g_pre_mix': out['v_g_pre_mix'], 'v_w_in': out['v_w_in'], 'v_b_gate': out['v_b_gate'], 'v_mla_q_norm': out['v_mla_q_norm'], 'v_w_uq': out['v_w_uq'], 'v_mla_kv_norm': out['v_mla_kv_norm'], 'v_w_ukv': out['v_w_ukv'], 'v_g_mem': out['v_g_mem'], 'v_w_mem_kv': out['v_w_mem_kv'], 'v_w_br_mla': out['v_w_br_mla'], 'v_w_br_dil': out['v_w_br_dil'], 'v_w_br_mem': out['v_w_br_mem'], 'v_w_o': out['v_w_o'], 'v_g_post_mix': out['v_g_post_mix'], 'v_g_pre_ffn': out['v_g_pre_ffn'], 'v_w_ffn_up': out['v_w_ffn_up'], 'v_conv_w': out['v_conv_w'], 'v_conv_b': out['v_conv_b'], 'v_w_ffn_down': out['v_w_ffn_down'], 'v_g_post_ffn': out['v_g_post_ffn']}


def _loss(weights, diff, rest, loss_target):
    with _jax.named_scope("forward"):
        args = {**rest, TWIN_DIFF_INPUT: diff, **{k: w.astype(_WEIGHT_DTYPES[k]) for k, w in weights.items()}}
        y = _forward(args)
    with _jax.named_scope("loss_head"):
        err = _jnp.square(y.astype(_jnp.float32) - loss_target)
        return 0.5 * _jnp.sum(_jnp.mean(err, axis=-1)) if err.ndim else 0.5 * err


def _adamw(w, g, m, v):
    m = ADAM_B1 * m + (1.0 - ADAM_B1) * g
    v = ADAM_B2 * v + (1.0 - ADAM_B2) * _jnp.square(g)
    m_hat = m / (1.0 - ADAM_B1 ** ADAM_STEP)
    v_hat = v / (1.0 - ADAM_B2 ** ADAM_STEP)
    delta = -ADAM_LR * (m_hat / (_jnp.sqrt(v_hat) + ADAM_EPS) + ADAM_WD * w)
    return delta, m, v


def reference(x, mem, positions, g_pre_mix, w_in, b_gate, mla_q_norm, w_uq, mla_kv_norm, w_ukv, g_mem, w_mem_kv, w_br_mla, w_br_dil, w_br_mem, w_o, g_post_mix, g_pre_ffn, w_ffn_up, conv_w, conv_b, w_ffn_down, g_post_ffn, loss_target, m_g_pre_mix, m_w_in, m_b_gate, m_mla_q_norm, m_w_uq, m_mla_kv_norm, m_w_ukv, m_g_mem, m_w_mem_kv, m_w_br_mla, m_w_br_dil, m_w_br_mem, m_w_o, m_g_post_mix, m_g_pre_ffn, m_w_ffn_up, m_conv_w, m_conv_b, m_w_ffn_down, m_g_post_ffn, v_g_pre_mix, v_w_in, v_b_gate, v_mla_q_norm, v_w_uq, v_mla_kv_norm, v_w_ukv, v_g_mem, v_w_mem_kv, v_w_br_mla, v_w_br_dil, v_w_br_mem, v_w_o, v_g_post_mix, v_g_pre_ffn, v_w_ffn_up, v_conv_w, v_conv_b, v_w_ffn_down, v_g_post_ffn):
    given = dict(x=x, mem=mem, positions=positions, g_pre_mix=g_pre_mix, w_in=w_in, b_gate=b_gate, mla_q_norm=mla_q_norm, w_uq=w_uq, mla_kv_norm=mla_kv_norm, w_ukv=w_ukv, g_mem=g_mem, w_mem_kv=w_mem_kv, w_br_mla=w_br_mla, w_br_dil=w_br_dil, w_br_mem=w_br_mem, w_o=w_o, g_post_mix=g_post_mix, g_pre_ffn=g_pre_ffn, w_ffn_up=w_ffn_up, conv_w=conv_w, conv_b=conv_b, w_ffn_down=w_ffn_down, g_post_ffn=g_post_ffn, loss_target=loss_target, m_g_pre_mix=m_g_pre_mix, m_w_in=m_w_in, m_b_gate=m_b_gate, m_mla_q_norm=m_mla_q_norm, m_w_uq=m_w_uq, m_mla_kv_norm=m_mla_kv_norm, m_w_ukv=m_w_ukv, m_g_mem=m_g_mem, m_w_mem_kv=m_w_mem_kv, m_w_br_mla=m_w_br_mla, m_w_br_dil=m_w_br_dil, m_w_br_mem=m_w_br_mem, m_w_o=m_w_o, m_g_post_mix=m_g_post_mix, m_g_pre_ffn=m_g_pre_ffn, m_w_ffn_up=m_w_ffn_up, m_conv_w=m_conv_w, m_conv_b=m_conv_b, m_w_ffn_down=m_w_ffn_down, m_g_post_ffn=m_g_post_ffn, v_g_pre_mix=v_g_pre_mix, v_w_in=v_w_in, v_b_gate=v_b_gate, v_mla_q_norm=v_mla_q_norm, v_w_uq=v_w_uq, v_mla_kv_norm=v_mla_kv_norm, v_w_ukv=v_w_ukv, v_g_mem=v_g_mem, v_w_mem_kv=v_w_mem_kv, v_w_br_mla=v_w_br_mla, v_w_br_dil=v_w_br_dil, v_w_br_mem=v_w_br_mem, v_w_o=v_w_o, v_g_post_mix=v_g_post_mix, v_g_pre_ffn=v_g_pre_ffn, v_w_ffn_up=v_w_ffn_up, v_conv_w=v_conv_w, v_conv_b=v_conv_b, v_w_ffn_down=v_w_ffn_down, v_g_post_ffn=v_g_post_ffn)
    weights = {n: given[n] for n in TWIN_WEIGHTS}
    shared = {n: given[n] for n in SHARED_INPUTS}
    per_example = {n: given[n] for n in ['x', 'mem', 'positions']}
    grad_fn = _jax.value_and_grad(_loss, argnums=(0, 1))

    def one_microbatch(ex, loss_target):
        ex = dict(ex)
        diff = ex.pop(TWIN_DIFF_INPUT)
        return grad_fn(weights, diff, {**shared, **ex}, loss_target)

    if N_MICROBATCH == 1:
        loss, (grad_w, grad_x) = one_microbatch(per_example, given["loss_target"])
    else:
        def body(carry, xs):
            loss_sum, grad_sum = carry
            l_k, (gw_k, gx_k) = one_microbatch(xs[0], xs[1])
            with _jax.named_scope("update"):
                return (loss_sum + l_k, _jax.tree.map(_jnp.add, grad_sum, gw_k)), gx_k

        init = (_jnp.zeros((), _jnp.float32), _jax.tree.map(_jnp.zeros_like, weights))
        (loss, grad_w), grad_x = _jax.lax.scan(body, init, (per_example, given["loss_target"]))
    with _jax.named_scope("update"):
        delta_w, new_m, new_v = {}, {}, {}
        for n in TWIN_WEIGHTS:
            delta_w[n], new_m[n], new_v[n] = _adamw(weights[n], grad_w[n], given["m_" + n], given["v_" + n])
    return (loss, grad_x, *[grad_w[n] for n in TWIN_WEIGHTS], *[delta_w[n] for n in TWIN_WEIGHTS],
            *[new_m[n] for n in TWIN_WEIGHTS], *[new_v[n] for n in TWIN_WEIGHTS])
```

```python
import functools

import numpy as np
import jax
import jax.numpy as jnp
from jax import lax
from jax.experimental import pallas as pl
from jax.experimental.pallas import tpu as pltpu

F32 = jnp.float32
BF16 = jnp.bfloat16

N_DEV = 8
D_MODEL = 1024
RMS_EPS = 1e-6
NEG_INF = -1e30
LANES = 128
SUBLANES = 8
BLOCK = 128

MLA_HEADS = 8
MLA_NOPE = 64
MLA_ROPE = 32
MLA_V = 64
MLA_QK_DIM = MLA_NOPE + MLA_ROPE
MLA_Q_RANK = 384
MLA_KV_RANK = 256
ROPE_THETA = 10000.0
DIL_PAIRS = ((128, 1), (512, 4), (2048, 16))
DIL_HEADS_PER_GROUP = 4
DIL_HEADS = 12
MEM_HEADS = 4
D_FF = 2816
OFF_Q = MLA_Q_RANK
OFF_KV = OFF_Q + MLA_KV_RANK
OFF_KR = OFF_KV + MLA_ROPE
OFF_DIL = OFF_KR + 3 * DIL_HEADS * LANES
OFF_MEMQ = OFF_DIL + MEM_HEADS * LANES
D_IN = OFF_MEMQ + 3 * D_MODEL
N_A = OFF_KV + LANES
N_DM = 3 * DIL_HEADS * LANES + MEM_HEADS * LANES

ADAM_LR = 0.001
ADAM_B1 = 0.9
ADAM_B2 = 0.999
ADAM_EPS = 1e-08
ADAM_WD = 0.01
ADAM_STEP = 10

VMEM_LIMIT = 48 * 1024 * 1024
VMEM_LIMIT_BIG = 58 * 1024 * 1024
PACK_W = 1024

_ALIBI_BASE = np.exp2(-8.0 * np.arange(1, DIL_HEADS + 1) / DIL_HEADS)
DIL_SLOPES = [[float(_ALIBI_BASE[hh * 3 + g]) for hh in range(DIL_HEADS_PER_GROUP)] for g in range(3)]

PARAMS = (
    ("g_pre_mix", (1024,), None), ("w_in", (1024, D_IN), 1), ("b_gate", (3072,), None),
    ("mla_q_norm", (384,), None), ("w_uq", (384, 768), 1), ("mla_kv_norm", (256,), None),
    ("w_ukv", (256, 1024), 1), ("g_mem", (1024,), None), ("w_mem_kv", (1024, 1024), 0),
    ("w_br_mla", (512, 1024), 1), ("w_br_dil", (512, 1024), 1), ("w_br_mem", (512, 1024), 1),
    ("w_o", (1024, 1024), 0), ("g_post_mix", (1024,), None), ("g_pre_ffn", (1024,), None),
    ("w_ffn_up", (1024, 2 * D_FF), 1), ("conv_w", (3, 2 * D_FF), 1), ("conv_b", (2 * D_FF,), None),
    ("w_ffn_down", (D_FF, 1024), 0), ("g_post_ffn", (1024,), None),
)
PARAM_NAMES = tuple(p[0] for p in PARAMS)


def _shard_shape(shape, axis):
    if axis is None:
        return shape
    return tuple(s // N_DEV if a == axis else s for a, s in enumerate(shape))


def _layout():
    off, table = 0, {}
    for name, shape, axis in sorted(PARAMS, key=lambda p: p[2] is None):
        n = int(np.prod(_shard_shape(shape, axis)))
        table[name] = (off, n)
        off += n
    rows = -(-off // PACK_W)
    rows = -(-rows // 208) * 208
    return table, off, rows


PACK_TABLE, PACK_USED, PACK_ROWS = _layout()


def _pick(n, cap):
    best = None
    for t in range(LANES, min(n, cap) + 1, LANES):
        if n % t == 0:
            best = t
    return best if best is not None else n


def _rows(n, cap):
    best = None
    for t in range(SUBLANES, min(n, cap) + 1, SUBLANES):
        if n % t == 0:
            best = t
    return best if best is not None else n


def _cparams(sem, vmem=VMEM_LIMIT):
    return pltpu.CompilerParams(dimension_semantics=sem, vmem_limit_bytes=vmem)


def _matmul(a, b, *, name, out_dtype=F32, trans_a=False, add=None, tm=1024, tn=1408, tk=640):
    if trans_a:
        kc, m = a.shape
    else:
        m, kc = a.shape
    n = b.shape[1]
    assert b.shape[0] == kc
    tm, tn, tk = _pick(m, tm), _pick(n, tn), _pick(kc, tk)
    nk = kc // tk

    def body(*refs):
        if add is None:
            a_ref, b_ref, o_ref, acc = refs
        else:
            a_ref, b_ref, c_ref, o_ref, acc = refs
        k = pl.program_id(2)

        @pl.when(k == 0)
        def _():
            if add is None:
                acc[...] = jnp.zeros_like(acc)
            else:
                acc[...] = c_ref[...].astype(F32)

        av = a_ref[...].astype(BF16)
        bv = b_ref[...].astype(BF16)
        if trans_a:
            acc[...] += lax.dot_general(av, bv, (((0,), (0,)), ((), ())), preferred_element_type=F32)
        else:
            acc[...] += jnp.dot(av, bv, preferred_element_type=F32)

        @pl.when(k == nk - 1)
        def _():
            o_ref[...] = acc[...].astype(out_dtype)

    if trans_a:
        a_spec = pl.BlockSpec((tk, tm), lambda i, j, k: (k, i))
    else:
        a_spec = pl.BlockSpec((tm, tk), lambda i, j, k: (i, k))
    in_specs = [a_spec, pl.BlockSpec((tk, tn), lambda i, j, k: (k, j))]
    args = [a, b]
    if add is not None:
        in_specs.append(pl.BlockSpec((tm, tn), lambda i, j, k: (i, j)))
        args.append(add)
    return pl.pallas_call(
        body, name=name, grid=(m // tm, n // tn, nk),
        in_specs=in_specs, out_specs=pl.BlockSpec((tm, tn), lambda i, j, k: (i, j)),
        out_shape=jax.ShapeDtypeStruct((m, n), out_dtype),
        scratch_shapes=[pltpu.VMEM((tm, tn), F32)],
        compiler_params=_cparams(("parallel", "parallel", "arbitrary")),
    )(*args)


def _rms_fwd(x, g, *, name, out_dtype, add=None):
    s, n = x.shape
    ts = _rows(s, 512)

    def body(*refs):
        if add is None:
            x_ref, g_ref, o_ref = refs
        else:
            x_ref, g_ref, a_ref, o_ref = refs
        xv = x_ref[...]
        r = lax.rsqrt(jnp.mean(xv * xv, axis=-1, keepdims=True) + RMS_EPS)
        y = xv * r * g_ref[...]
        if add is not None:
            y = a_ref[...] + y
        o_ref[...] = y.astype(out_dtype)

    row = pl.BlockSpec((ts, n), lambda i: (i, 0))
    in_specs = [row, pl.BlockSpec((1, n), lambda i: (0, 0))]
    args = [x, g]
    if add is not None:
        in_specs.append(row)
        args.append(add)
    return pl.pallas_call(
        body, name=name, grid=(s // ts,), in_specs=in_specs, out_specs=row,
        out_shape=jax.ShapeDtypeStruct((s, n), out_dtype),
        compiler_params=_cparams(("parallel",)),
    )(*args)


def _rms_bwd(x, g, dy, *, name, out_dtype, add=None):
    s, n = x.shape
    ts = _rows(s, 512)

    def body(*refs):
        if add is None:
            x_ref, g_ref, dy_ref, dx_ref, dg_ref = refs
        else:
            x_ref, g_ref, dy_ref, a_ref, dx_ref, dg_ref = refs
        i = pl.program_id(0)
        xv = x_ref[...]
        dyv = dy_ref[...].astype(F32)
        r = lax.rsqrt(jnp.mean(xv * xv, axis=-1, keepdims=True) + RMS_EPS)
        nx = xv * r
        gdy = dyv * g_ref[...]
        dx = r * (gdy - nx * jnp.mean(nx * gdy, axis=-1, keepdims=True))
        if add is not None:
            dx = a_ref[...] + dx
        dx_ref[...] = dx.astype(out_dtype)

        @pl.when(i == 0)
        def _():
            dg_ref[...] = jnp.zeros_like(dg_ref)

        dg_ref[...] += jnp.sum(dyv * nx, axis=0, keepdims=True)

    row = pl.BlockSpec((ts, n), lambda i: (i, 0))
    vec = pl.BlockSpec((1, n), lambda i: (0, 0))
    in_specs = [row, vec, row]
    args = [x, g, dy]
    if add is not None:
        in_specs.append(row)
        args.append(add)
    return pl.pallas_call(
        body, name=name, grid=(s // ts,), in_specs=in_specs, out_specs=[row, vec],
        out_shape=[jax.ShapeDtypeStruct((s, n), out_dtype), jax.ShapeDtypeStruct((1, n), F32)],
        compiler_params=_cparams(("arbitrary",)),
    )(*args)


def _loss_fwd(x1, f, target, g, *, name):
    s, n = x1.shape
    ts = _rows(s, 512)

    def body(x_ref, f_ref, t_ref, g_ref, dy_ref, sq_ref):
        i = pl.program_id(0)
        fv = f_ref[...]
        r = lax.rsqrt(jnp.mean(fv * fv, axis=-1, keepdims=True) + RMS_EPS)
        err = x_ref[...] + fv * r * g_ref[...] - t_ref[...]
        dy_ref[...] = err * (1.0 / n)

        @pl.when(i == 0)
        def _():
            sq_ref[...] = jnp.zeros_like(sq_ref)

        sq_ref[...] += jnp.sum(err * err, axis=0, keepdims=True)

    row = pl.BlockSpec((ts, n), lambda i: (i, 0))
    vec = pl.BlockSpec((1, n), lambda i: (0, 0))
    return pl.pallas_call(
        body, name=name, grid=(s // ts,), in_specs=[row, row, row, vec], out_specs=[row, vec],
        out_shape=[jax.ShapeDtypeStruct((s, n), F32), jax.ShapeDtypeStruct((1, n), F32)],
        compiler_params=_cparams(("arbitrary",)),
    )(x1, f, target, g)


def _rope_tables(positions):
    half = MLA_ROPE // 2
    inv_freq = ROPE_THETA ** (-jnp.arange(half, dtype=F32) / half)
    ang = positions.astype(F32)[:, None] * inv_freq[None, :]
    cos, sin = jnp.cos(ang), jnp.sin(ang)
    s = positions.shape[0]
    one = jnp.ones((s, MLA_NOPE), F32)
    zero = jnp.zeros((s, MLA_NOPE), F32)
    pad1 = jnp.ones((s, LANES - MLA_QK_DIM), F32)
    pad0 = jnp.zeros((s, LANES - MLA_QK_DIM), F32)
    zh = jnp.zeros((s, half), F32)
    c_tab = jnp.concatenate([one, cos, cos, pad1], axis=1)
    s1_tab = jnp.concatenate([zero, -sin, zh, pad0], axis=1)
    s2_tab = jnp.concatenate([zero, zh, sin, pad0], axis=1)
    return c_tab, s1_tab, s2_tab


def _rope_fwd(x, tabs, *, name, scale, add=None):
    s, n = x.shape
    nh = n // LANES
    ts = _rows(s, 512)
    half = MLA_ROPE // 2

    def body(*refs):
        if add is None:
            x_ref, c_ref, s1_ref, s2_ref, o_ref = refs
        else:
            x_ref, a_ref, c_ref, s1_ref, s2_ref, o_ref = refs
        c, s1, s2 = c_ref[...], s1_ref[...], s2_ref[...]
        for h in range(nh):
            xh = x_ref[:, h * LANES:(h + 1) * LANES]
            if add is not None:
                xh = xh + a_ref[...]
            y = xh * c + pltpu.roll(xh, LANES - half, 1) * s1 + pltpu.roll(xh, half, 1) * s2
            o_ref[:, h * LANES:(h + 1) * LANES] = (y * scale).astype(BF16)

    row = pl.BlockSpec((ts, n), lambda i: (i, 0))
    tab = pl.BlockSpec((ts, LANES), lambda i: (i, 0))
    in_specs = [row] + ([tab] if add is not None else []) + [tab, tab, tab]
    args = [x] + ([add] if add is not None else []) + list(tabs)
    return pl.pallas_call(
        body, name=name, grid=(s // ts,), in_specs=in_specs, out_specs=row,
        out_shape=jax.ShapeDtypeStruct((s, n), BF16),
        compiler_params=_cparams(("parallel",)),
    )(*args)


def _rope_bwd(dy, tabs, *, name, scale, with_add):
    s, n = dy.shape
    nh = n // LANES
    ts = _rows(s, 512)
    half = MLA_ROPE // 2

    def body(*refs):
        if with_add:
            dy_ref, c_ref, s1_ref, s2_ref, dx_ref, da_ref = refs
        else:
            dy_ref, c_ref, s1_ref, s2_ref, dx_ref = refs
        c, s1, s2 = c_ref[...], s1_ref[...], s2_ref[...]
        tot = None
        for h in range(nh):
            g = dy_ref[:, h * LANES:(h + 1) * LANES].astype(F32)
            dx = (g * c + pltpu.roll(g * s1, half, 1) + pltpu.roll(g * s2, LANES - half, 1)) * scale
            dx_ref[:, h * LANES:(h + 1) * LANES] = dx.astype(BF16)
            tot = dx if tot is None else tot + dx
        if with_add:
            da_ref[...] = tot

    row = pl.BlockSpec((ts, n), lambda i: (i, 0))
    tab = pl.BlockSpec((ts, LANES), lambda i: (i, 0))
    out_specs = [row, tab] if with_add else row
    out_shape = [jax.ShapeDtypeStruct((s, n), BF16)]
    if with_add:
        out_shape.append(jax.ShapeDtypeStruct((s, LANES), F32))
    else:
        out_shape = out_shape[0]
    return pl.pallas_call(
        body, name=name, grid=(s // ts,), in_specs=[row, tab, tab, tab], out_specs=out_specs,
        out_shape=out_shape, compiler_params=_cparams(("parallel",)),
    )(dy, *tabs)


def _scores(q, k, scale, diag):
    s = lax.dot_general(q, k, (((1,), (1,)), ((), ())), preferred_element_type=F32)
    if scale != 1.0:
        s = s * scale
    if diag:
        rows = lax.broadcasted_iota(jnp.int32, s.shape, 0)
        cols = lax.broadcasted_iota(jnp.int32, s.shape, 1)
        s = jnp.where(cols <= rows, s, NEG_INF)
    return s


def _flash_fwd(q, k, v, *, name, heads, qoff, koff, voff, causal, scale, tq, tk):
    s_q, s_kv = q.shape[0], k.shape[0]
    tq, tk = min(tq, s_q), min(tk, s_kv)
    nq, nk = s_q // tq, s_kv // tk
    if causal:
        assert tq == tk and s_q == s_kv

    def body(q_ref, k_ref, v_ref, o_ref, lse_ref, m_s, l_s, acc):
        i, j = pl.program_id(1), pl.program_id(2)

        @pl.when(j == 0)
        def _():
            m_s[...] = jnp.full_like(m_s, NEG_INF)
            l_s[...] = jnp.zeros_like(l_s)
            acc[...] = jnp.zeros_like(acc)

        def step(diag):
            s = _scores(q_ref[...], k_ref[...], scale, diag)
            m_prev = m_s[...]
            m_cur = jnp.maximum(m_prev, jnp.max(s, axis=1, keepdims=True))
            alpha = jnp.exp(m_prev - m_cur)
            p = jnp.exp(s - m_cur[:, :1])
            l_s[...] = alpha * l_s[...] + jnp.sum(p, axis=1, keepdims=True)
            acc[...] = alpha * acc[...] + jnp.dot(p.astype(BF16), v_ref[...], preferred_element_type=F32)
            m_s[...] = m_cur

        def finish():
            o_ref[...] = (acc[...] / l_s[...]).astype(o_ref.dtype)
            lse_ref[...] = m_s[...] + jnp.log(l_s[...])

        if causal:
            @pl.when(j < i)
            def _():
                step(False)

            @pl.when(j == i)
            def _():
                step(True)
                finish()
        else:
            step(False)

            @pl.when(j == nk - 1)
            def _():
                finish()

    def kv_idx(off):
        if causal:
            return lambda h, i, j: (jnp.minimum(j, i), off + h)
        return lambda h, i, j: (j, off + h)

    blk_q = pl.BlockSpec((tq, LANES), lambda h, i, j: (i, qoff + h))
    out_q = pl.BlockSpec((tq, LANES), lambda h, i, j: (i, h))
    return pl.pallas_call(
        body, name=name, grid=(heads, nq, nk),
        in_specs=[blk_q, pl.BlockSpec((tk, LANES), kv_idx(koff)), pl.BlockSpec((tk, LANES), kv_idx(voff))],
        out_specs=[out_q, out_q],
        out_shape=[jax.ShapeDtypeStruct((s_q, heads * LANES), BF16),
                   jax.ShapeDtypeStruct((s_q, heads * LANES), F32)],
        scratch_shapes=[pltpu.VMEM((tq, LANES), F32)] * 3,
        compiler_params=_cparams(("parallel", "parallel", "arbitrary")),
    )(q, k, v)


def _flash_bwd_dq(q, k, v, do, lse, delta, *, name, heads, qoff, koff, voff, causal, scale, tq, tk, out_dtype):
    s_q, s_kv = q.shape[0], k.shape[0]
    tq, tk = min(tq, s_q), min(tk, s_kv)
    nq, nk = s_q // tq, s_kv // tk

    def body(q_ref, k_ref, v_ref, do_ref, lse_ref, dl_ref, dq_ref, acc):
        i, j = pl.program_id(1), pl.program_id(2)

        @pl.when(j == 0)
        def _():
            acc[...] = jnp.zeros_like(acc)

        def step(diag):
            s = _scores(q_ref[...], k_ref[...], scale, diag)
            p = jnp.exp(s - lse_ref[:, :1])
            dp = lax.dot_general(do_ref[...], v_ref[...], (((1,), (1,)), ((), ())), preferred_element_type=F32)
            ds = p * (dp - dl_ref[:, :1])
            acc[...] += jnp.dot(ds.astype(BF16), k_ref[...], preferred_element_type=F32)

        def finish():
            dq_ref[...] = (acc[...] * scale).astype(out_dtype)

        if causal:
            @pl.when(j < i)
            def _():
                step(False)

            @pl.when(j == i)
            def _():
                step(True)
                finish()
        else:
            step(False)

            @pl.when(j == nk - 1)
            def _():
                finish()

    def kv_idx(off):
        if causal:
            return lambda h, i, j: (jnp.minimum(j, i), off + h)
        return lambda h, i, j: (j, off + h)

    blk_q = pl.BlockSpec((tq, LANES), lambda h, i, j: (i, qoff + h))
    blk_h = pl.BlockSpec((tq, LANES), lambda h, i, j: (i, h))
    return pl.pallas_call(
        body, name=name, grid=(heads, nq, nk),
        in_specs=[blk_q, pl.BlockSpec((tk, LANES), kv_idx(koff)), pl.BlockSpec((tk, LANES), kv_idx(voff)),
                  blk_h, blk_h, blk_h],
        out_specs=blk_h,
        out_shape=jax.ShapeDtypeStruct((s_q, heads * LANES), out_dtype),
        scratch_shapes=[pltpu.VMEM((tq, LANES), F32)],
        compiler_params=_cparams(("parallel", "parallel", "arbitrary")),
    )(q, k, v, do, lse, delta)


def _flash_bwd_dkv(q, k, v, do, lse, delta, *, name, heads, qoff, koff, voff, causal, scale, tq, tk,
                   dk_dtype, dv_dtype):
    s_q, s_kv = q.shape[0], k.shape[0]
    tq, tk = min(tq, s_q), min(tk, s_kv)
    nq, nk = s_q // tq, s_kv // tk

    def body(q_ref, k_ref, v_ref, do_ref, lse_ref, dl_ref, dk_ref, dv_ref, dk_acc, dv_acc):
        j, i = pl.program_id(1), pl.program_id(2)

        @pl.when(i == 0)
        def _():
            dk_acc[...] = jnp.zeros_like(dk_acc)
            dv_acc[...] = jnp.zeros_like(dv_acc)

        def step(diag):
            s = _scores(q_ref[...], k_ref[...], scale, diag)
            p = jnp.exp(s - lse_ref[:, :1])
            dov = do_ref[...]
            dp = lax.dot_general(dov, v_ref[...], (((1,), (1,)), ((), ())), preferred_element_type=F32)
            ds = p * (dp - dl_ref[:, :1])
            dv_acc[...] += lax.dot_general(p.astype(BF16), dov, (((0,), (0,)), ((), ())),
                                           preferred_element_type=F32)
            dk_acc[...] += lax.dot_general(ds.astype(BF16), q_ref[...], (((0,), (0,)), ((), ())),
                                           preferred_element_type=F32)

        if causal:
            @pl.when(i > j)
            def _():
                step(False)

            @pl.when(i == j)
            def _():
                step(True)
        else:
            step(False)

        @pl.when(i == nq - 1)
        def _():
            dk_ref[...] = (dk_acc[...] * scale).astype(dk_dtype)
            dv_ref[...] = dv_acc[...].astype(dv_dtype)

    def q_idx(off):
        if causal:
            return lambda h, j, i: (jnp.maximum(i, j), off + h)
        return lambda h, j, i: (i, off + h)

    blk_h = pl.BlockSpec((tq, LANES), q_idx(0))
    out_k = pl.BlockSpec((tk, LANES), lambda h, j, i: (j, h))
    return pl.pallas_call(
        body, name=name, grid=(heads, nk, nq),
        in_specs=[pl.BlockSpec((tq, LANES), q_idx(qoff)),
                  pl.BlockSpec((tk, LANES), lambda h, j, i: (j, koff + h)),
                  pl.BlockSpec((tk, LANES), lambda h, j, i: (j, voff + h)),
                  blk_h, blk_h, blk_h],
        out_specs=[out_k, out_k],
        out_shape=[jax.ShapeDtypeStruct((s_kv, heads * LANES), dk_dtype),
                   jax.ShapeDtypeStruct((s_kv, heads * LANES), dv_dtype)],
        scratch_shapes=[pltpu.VMEM((tk, LANES), F32)] * 2,
        compiler_params=_cparams(("parallel", "parallel", "arbitrary")),
    )(q, k, v, do, lse, delta)


def _row_dot(a, b, *, name):
    s, n = a.shape
    nh = n // LANES
    ts = _rows(s, 512)

    def body(a_ref, b_ref, o_ref):
        for h in range(nh):
            sl = slice(h * LANES, (h + 1) * LANES)
            d = jnp.sum(a_ref[:, sl].astype(F32) * b_ref[:, sl].astype(F32), axis=1, keepdims=True)
            o_ref[:, sl] = jnp.broadcast_to(d, (ts, LANES))

    row = pl.BlockSpec((ts, n), lambda i: (i, 0))
    return pl.pallas_call(
        body, name=name, grid=(s // ts,), in_specs=[row, row], out_specs=row,
        out_shape=jax.ShapeDtypeStruct((s, n), F32), compiler_params=_cparams(("parallel",)),
    )(a, b)


def _band_masks(dilation, slope):
    qi = lax.broadcasted_iota(jnp.int32, (BLOCK, 2 * BLOCK), 0)
    kj = lax.broadcasted_iota(jnp.int32, (BLOCK, 2 * BLOCK), 1)
    dist = qi + BLOCK - kj
    valid = (dist >= 0) & (dist <= BLOCK)
    bias = -slope * (dist * dilation).astype(F32)
    return valid, bias


def _band_fwd(q, k, v, slopes, *, name, dilation, qoff, koff, voff):
    s = q.shape[0]
    sub = s // dilation
    nb = sub // BLOCK
    assert nb * BLOCK == sub
    scale = LANES ** -0.5

    def body(sl_ref, q_ref, k_ref, v_ref, o_ref, lse_ref):
        slope = sl_ref[pl.program_id(0)]
        valid2, bias2 = _band_masks(dilation, slope)
        valid1, bias1 = valid2[:, BLOCK:], bias2[:, BLOCK:]

        def block(start_q, kk, vv, valid, bias):
            qb = q_ref[pl.ds(start_q, BLOCK), :]
            sc = lax.dot_general(qb, kk, (((1,), (1,)), ((), ())), preferred_element_type=F32) * scale
            sc = jnp.where(valid, sc + bias, NEG_INF)
            m = jnp.max(sc, axis=1, keepdims=True)
            e = jnp.exp(sc - m)
            den = jnp.sum(e, axis=1, keepdims=True)
            p = (e / den).astype(BF16)
            o_ref[pl.ds(start_q, BLOCK), :] = jnp.dot(p, vv, preferred_element_type=F32)
            lse_ref[pl.ds(start_q, BLOCK), :] = jnp.broadcast_to(m + jnp.log(den), (BLOCK, LANES))

        block(0, k_ref[0:BLOCK, :], v_ref[0:BLOCK, :], valid1, bias1)

        def loop(jj, carry):
            start_q = pl.multiple_of(jj * BLOCK, BLOCK)
            start_k = pl.multiple_of((jj - 1) * BLOCK, BLOCK)
            block(start_q, k_ref[pl.ds(start_k, 2 * BLOCK), :], v_ref[pl.ds(start_k, 2 * BLOCK), :], valid2, bias2)
            return carry

        lax.fori_loop(1, nb, loop, 0)

    def spec(off):
        return pl.BlockSpec((sub, LANES), lambda h, r: (r, off + h))

    out = pl.BlockSpec((sub, LANES), lambda h, r: (r, h))
    return pl.pallas_call(
        body, name=name, grid=(DIL_HEADS_PER_GROUP, dilation),
        in_specs=[pl.BlockSpec(memory_space=pltpu.SMEM), spec(qoff), spec(koff), spec(voff)],
        out_specs=[out, out],
        out_shape=[jax.ShapeDtypeStruct((s, DIL_HEADS_PER_GROUP * LANES), F32)] * 2,
        compiler_params=_cparams(("parallel", "parallel"), VMEM_LIMIT_BIG),
    )(slopes, q, k, v)


def _band_bwd(q, k, v, do, lse, delta, slopes, *, name, dilation, qoff, koff, voff):
    s = q.shape[0]
    sub = s // dilation
    nb = sub // BLOCK
    scale = LANES ** -0.5

    def body(sl_ref, q_ref, k_ref, v_ref, do_ref, lse_ref, dl_ref, dq_ref, dk_ref, dv_ref):
        slope = sl_ref[pl.program_id(0)]
        valid2, bias2 = _band_masks(dilation, slope)
        valid1, bias1 = valid2[:, BLOCK:], bias2[:, BLOCK:]

        def block(start_q, kk, vv, valid, bias):
            qb = q_ref[pl.ds(start_q, BLOCK), :]
            dob = do_ref[pl.ds(start_q, BLOCK), :]
            sc = lax.dot_general(qb, kk, (((1,), (1,)), ((), ())), preferred_element_type=F32) * scale
            sc = jnp.where(valid, sc + bias, NEG_INF)
            p = jnp.exp(sc - lse_ref[pl.ds(start_q, BLOCK), :][:, :1])
            dp = lax.dot_general(dob, vv, (((1,), (1,)), ((), ())), preferred_element_type=F32)
            ds = (p * (dp - dl_ref[pl.ds(start_q, BLOCK), :][:, :1])).astype(BF16)
            dq = jnp.dot(ds, kk, preferred_element_type=F32) * scale
            dq_ref[pl.ds(start_q, BLOCK), :] = dq.astype(BF16)
            dkk = lax.dot_general(ds, qb, (((0,), (0,)), ((), ())), preferred_element_type=F32) * scale
            dvv = lax.dot_general(p.astype(BF16), dob, (((0,), (0,)), ((), ())), preferred_element_type=F32)
            return dkk, dvv

        carry0 = block(0, k_ref[0:BLOCK, :], v_ref[0:BLOCK, :], valid1, bias1)

        def loop(jj, carry):
            dk_part, dv_part = carry
            start_q = pl.multiple_of(jj * BLOCK, BLOCK)
            start_k = pl.multiple_of((jj - 1) * BLOCK, BLOCK)
            dkk, dvv = block(start_q, k_ref[pl.ds(start_k, 2 * BLOCK), :], v_ref[pl.ds(start_k, 2 * BLOCK), :],
                             valid2, bias2)
            dk_ref[pl.ds(start_k, BLOCK), :] = (dk_part + dkk[:BLOCK]).astype(BF16)
            dv_ref[pl.ds(start_k, BLOCK), :] = (dv_part + dvv[:BLOCK]).astype(BF16)
            return dkk[BLOCK:], dvv[BLOCK:]

        dk_last, dv_last = lax.fori_loop(1, nb, loop, carry0)
        dk_ref[(nb - 1) * BLOCK:nb * BLOCK, :] = dk_last.astype(BF16)
        dv_ref[(nb - 1) * BLOCK:nb * BLOCK, :] = dv_last.astype(BF16)

    def spec(off):
        return pl.BlockSpec((sub, LANES), lambda h, r: (r, off + h))

    out = spec(0)
    return pl.pallas_call(
        body, name=name, grid=(DIL_HEADS_PER_GROUP, dilation),
        in_specs=[pl.BlockSpec(memory_space=pltpu.SMEM), spec(qoff), spec(koff), spec(voff), out, out, out],
        out_specs=[out, out, out],
        out_shape=[jax.ShapeDtypeStruct((s, DIL_HEADS_PER_GROUP * LANES), BF16)] * 3,
        compiler_params=_cparams(("parallel", "parallel"), VMEM_LIMIT_BIG),
    )(slopes, q, k, v, do, lse, delta)


def _mix_fwd(outs, lses, *, name):
    s, n = outs[0].shape
    ts = _rows(s, 512)

    def body(o0, o1, o2, l0, l1, l2, y_ref):
        la, lb, lc = l0[...], l1[...], l2[...]
        m = jnp.maximum(jnp.maximum(la, lb), lc)
        ea, eb, ec = jnp.exp(la - m), jnp.exp(lb - m), jnp.exp(lc - m)
        den = ea + eb + ec
        y = (ea / den) * o0[...] + (eb / den) * o1[...] + (ec / den) * o2[...]
        y_ref[...] = y.astype(BF16)

    row = pl.BlockSpec((ts, n), lambda i: (i, 0))
    return pl.pallas_call(
        body, name=name, grid=(s // ts,), in_specs=[row] * 6, out_specs=row,
        out_shape=jax.ShapeDtypeStruct((s, n), BF16), compiler_params=_cparams(("parallel",)),
    )(*outs, *lses)


def _mix_bwd(dy, outs, lses, *, name):
    s, n = dy.shape
    nh = n // LANES
    ts = _rows(s, 256)

    def body(dy_ref, o0, o1, o2, l0, l1, l2, d0, d1, d2, e0, e1, e2):
        la, lb, lc = l0[...], l1[...], l2[...]
        m = jnp.maximum(jnp.maximum(la, lb), lc)
        ea, eb, ec = jnp.exp(la - m), jnp.exp(lb - m), jnp.exp(lc - m)
        den = ea + eb + ec
        wa, wb, wc = ea / den, eb / den, ec / den
        dyv = dy_ref[...]
        y = wa * o0[...] + wb * o1[...] + wc * o2[...]
        prod = dyv * y
        d0[...] = (wa * dyv).astype(BF16)
        d1[...] = (wb * dyv).astype(BF16)
        d2[...] = (wc * dyv).astype(BF16)
        for h in range(nh):
            sl = slice(h * LANES, (h + 1) * LANES)
            t = jnp.sum(prod[:, sl], axis=1, keepdims=True)
            e0[:, sl] = wa[:, sl] * t
            e1[:, sl] = wb[:, sl] * t
            e2[:, sl] = wc[:, sl] * t

    row = pl.BlockSpec((ts, n), lambda i: (i, 0))
    return pl.pallas_call(
        body, name=name, grid=(s // ts,), in_specs=[row] * 7, out_specs=[row] * 6,
        out_shape=[jax.ShapeDtypeStruct((s, n), BF16)] * 3 + [jax.ShapeDtypeStruct((s, n), F32)] * 3,
        compiler_params=_cparams(("parallel",)),
    )(dy, *outs, *lses)


def _gate_fwd(gp, b_gate, branches, *, name):
    s = gp.shape[0]
    ts = _rows(s, 256)

    def body(gp_ref, b_ref, b0, b1, b2, o_ref):
        tot = None
        for i, br in enumerate((b0, b1, b2)):
            sl = slice(i * D_MODEL, (i + 1) * D_MODEL)
            t = jax.nn.sigmoid(gp_ref[:, sl] + b_ref[:, sl]) * br[...]
            tot = t if tot is None else tot + t
        o_ref[...] = tot.astype(BF16)

    row = pl.BlockSpec((ts, D_MODEL), lambda i: (i, 0))
    return pl.pallas_call(
        body, name=name, grid=(s // ts,),
        in_specs=[pl.BlockSpec((ts, 3 * D_MODEL), lambda i: (i, 0)), pl.BlockSpec((1, 3 * D_MODEL), lambda i: (0, 0)),
                  row, row, row],
        out_specs=row, out_shape=jax.ShapeDtypeStruct((s, D_MODEL), BF16),
        compiler_params=_cparams(("parallel",)),
    )(gp, b_gate, *branches)


def _gate_bwd(dm, gp, b_gate, branches, *, name):
    s = gp.shape[0]
    ts = _rows(s, 256)

    def body(dm_ref, gp_ref, b_ref, b0, b1, b2, d0, d1, d2, dgp_ref, db_ref):
        i = pl.program_id(0)

        @pl.when(i == 0)
        def _():
            db_ref[...] = jnp.zeros_like(db_ref)

        dmv = dm_ref[...]
        for k, (br, dbr) in enumerate(((b0, d0), (b1, d1), (b2, d2))):
            sl = slice(k * D_MODEL, (k + 1) * D_MODEL)
            sg = jax.nn.sigmoid(gp_ref[:, sl] + b_ref[:, sl])
            dbr[...] = (dmv * sg).astype(BF16)
            dg = dmv * br[...] * sg * (1.0 - sg)
            dgp_ref[:, sl] = dg.astype(BF16)
            db_ref[:, sl] += jnp.sum(dg, axis=0, keepdims=True)

    row = pl.BlockSpec((ts, D_MODEL), lambda i: (i, 0))
    wide = pl.BlockSpec((ts, 3 * D_MODEL), lambda i: (i, 0))
    vec = pl.BlockSpec((1, 3 * D_MODEL), lambda i: (0, 0))
    return pl.pallas_call(
        body, name=name, grid=(s // ts,),
        in_specs=[row, wide, vec, row, row, row], out_specs=[row, row, row, wide, vec],
        out_shape=[jax.ShapeDtypeStruct((s, D_MODEL), BF16)] * 3
        + [jax.ShapeDtypeStruct((s, 3 * D_MODEL), BF16), jax.ShapeDtypeStruct((1, 3 * D_MODEL), F32)],
        compiler_params=_cparams(("arbitrary",)),
    )(dm, gp, b_gate, *branches)


CONV_TC = 1408


def _shift_down(x, halo, k):
    rolled = pltpu.roll(x, k, 0)
    r8 = lax.broadcasted_iota(jnp.int32, halo.shape, 0)
    top = jnp.where(r8 < k, pltpu.roll(halo, k, 0), rolled[:SUBLANES])
    return jnp.concatenate([top, rolled[SUBLANES:]], axis=0)


def _shift_up(x, halo, k):
    n = x.shape[0]
    rolled = pltpu.roll(x, n - k, 0)
    r8 = lax.broadcasted_iota(jnp.int32, halo.shape, 0)
    bot = jnp.where(r8 >= SUBLANES - k, pltpu.roll(halo, SUBLANES - k, 0), rolled[n - SUBLANES:])
    return jnp.concatenate([rolled[:n - SUBLANES], bot], axis=0)


def _conv_fwd(u, conv_w, conv_b, *, name):
    s = u.shape[0]
    ts = _rows(s, 256)
    nct = D_FF // CONV_TC
    per8 = ts // SUBLANES

    def body(ug, uv, hg, hv, wg, wv, bg, bv, zg_ref, zv_ref, a_ref):
        first = pl.program_id(1) == 0

        def conv(u_ref, h_ref, w_ref, b_ref):
            x = u_ref[...]
            halo = jnp.where(first, 0.0, h_ref[...])
            z = b_ref[...] + w_ref[0:1, :] * _shift_down(x, halo, 2)
            z = z + w_ref[1:2, :] * _shift_down(x, halo, 1)
            return z + w_ref[2:3, :] * x

        zg = conv(ug, hg, wg, bg)
        zv = conv(uv, hv, wv, bv)
        zg_ref[...] = zg
        zv_ref[...] = zv
        a_ref[...] = (zg * jax.nn.sigmoid(zg) * zv).astype(BF16)

    def col(off):
        return pl.BlockSpec((ts, CONV_TC), lambda c, i: (i, c + off))

    def halo(off):
        return pl.BlockSpec((SUBLANES, CONV_TC), lambda c, i: (jnp.maximum(i * per8 - 1, 0), c + off))

    def wspec(rows, off):
        return pl.BlockSpec((rows, CONV_TC), lambda c, i: (0, c + off))

    zg, zv, a = pl.pallas_call(
        body, name=name, grid=(nct, s // ts),
        in_specs=[col(0), col(nct), halo(0), halo(nct), wspec(3, 0), wspec(3, nct), wspec(1, 0), wspec(1, nct)],
        out_specs=[col(0), col(0), col(0)],
        out_shape=[jax.ShapeDtypeStruct((s, D_FF), F32)] * 2 + [jax.ShapeDtypeStruct((s, D_FF), BF16)],
        compiler_params=_cparams(("parallel", "parallel")),
    )(u, u, u, u, conv_w, conv_w, conv_b, conv_b)
    return zg, zv, a


def _silu_bwd(da, zg, zv, *, name):
    s = da.shape[0]
    ts = _rows(s, 256)
    nct = D_FF // CONV_TC

    def body(da_ref, zg_ref, zv_ref, dz_g, dz_v):
        g = zg_ref[...]
        sg = jax.nn.sigmoid(g)
        dav = da_ref[...]
        dz_g[...] = dav * zv_ref[...] * sg * (1.0 + g * (1.0 - sg))
        dz_v[...] = dav * g * sg

    col = pl.BlockSpec((ts, CONV_TC), lambda c, i: (i, c))
    return pl.pallas_call(
        body, name=name, grid=(nct, s // ts), in_specs=[col, col, col], out_specs=[col, col],
        out_shape=[jax.ShapeDtypeStruct((s, D_FF), F32)] * 2,
        compiler_params=_cparams(("parallel", "parallel")),
    )(da, zg, zv)


def _conv_bwd(dz, u, conv_w, *, name, off):
    s = dz.shape[0]
    ts = _rows(s, 256)
    nct = D_FF // CONV_TC
    per8 = ts // SUBLANES
    nrow = s // ts
    last8 = s // SUBLANES - 1

    def body(dz_ref, nx_ref, u_ref, pv_ref, w_ref, du_ref, acc_ref):
        i = pl.program_id(1)
        dzv = dz_ref[...]
        nxt = jnp.where(i == nrow - 1, 0.0, nx_ref[...])
        du = w_ref[2:3, :] * dzv + w_ref[1:2, :] * _shift_up(dzv, nxt, 1) + w_ref[0:1, :] * _shift_up(dzv, nxt, 2)
        du_ref[...] = du.astype(BF16)

        @pl.when(i == 0)
        def _():
            acc_ref[...] = jnp.zeros_like(acc_ref)

        x = u_ref[...]
        prev = jnp.where(i == 0, 0.0, pv_ref[...])
        acc_ref[0:1, :] += jnp.sum(dzv * _shift_down(x, prev, 2), axis=0, keepdims=True)
        acc_ref[1:2, :] += jnp.sum(dzv * _shift_down(x, prev, 1), axis=0, keepdims=True)
        acc_ref[2:3, :] += jnp.sum(dzv * x, axis=0, keepdims=True)
        acc_ref[3:4, :] += jnp.sum(dzv, axis=0, keepdims=True)

    return pl.pallas_call(
        body, name=name, grid=(nct, nrow),
        in_specs=[pl.BlockSpec((ts, CONV_TC), lambda c, i: (i, c)),
                  pl.BlockSpec((SUBLANES, CONV_TC), lambda c, i: (jnp.minimum((i + 1) * per8, last8), c)),
                  pl.BlockSpec((ts, CONV_TC), lambda c, i: (i, c + off)),
                  pl.BlockSpec((SUBLANES, CONV_TC), lambda c, i: (jnp.maximum(i * per8 - 1, 0), c + off)),
                  pl.BlockSpec((3, CONV_TC), lambda c, i: (0, c + off))],
        out_specs=[pl.BlockSpec((ts, CONV_TC), lambda c, i: (i, c)),
                   pl.BlockSpec((SUBLANES, CONV_TC), lambda c, i: (0, c))],
        out_shape=[jax.ShapeDtypeStruct((s, D_FF), BF16), jax.ShapeDtypeStruct((SUBLANES, D_FF), F32)],
        compiler_params=_cparams(("parallel", "arbitrary")),
    )(dz, dz, u, u, conv_w)


def _peer(k):
    x, y, c = lax.axis_index("x"), lax.axis_index("y"), lax.axis_index("c")
    px = 1 - x if k & 4 else x
    py = 1 - y if k & 2 else y
    pc = 1 - c if k & 1 else c
    return (px, py, pc), 4 * px + 2 * py + pc


def _exchange(buf, *, name, gather):
    rows, width = buf.shape[-2:]

    def body(src, out, send_sems, recv_sems, local_sem):
        _, me = _peer(0)
        mine = src if gather else src.at[me]
        local = pltpu.make_async_copy(mine, out.at[me], local_sem)
        local.start()
        sends = []
        for k in range(1, N_DEV):
            dev, idx = _peer(k)
            cp = pltpu.make_async_remote_copy(
                src_ref=src if gather else src.at[idx], dst_ref=out.at[me],
                send_sem=send_sems.at[k - 1], recv_sem=recv_sems.at[k - 1],
                device_id=dev, device_id_type=pl.DeviceIdType.MESH)
            cp.start()
            sends.append(cp)
        for k in range(1, N_DEV):
            dev, idx = _peer(k)
            pltpu.make_async_remote_copy(
                src_ref=mine, dst_ref=out.at[idx],
                send_sem=send_sems.at[k - 1], recv_sem=recv_sems.at[k - 1],
                device_id=dev, device_id_type=pl.DeviceIdType.MESH).wait_recv()
        for cp in sends:
            cp.wait_send()
        local.wait()

    return pl.pallas_call(
        body, name=name,
        in_specs=[pl.BlockSpec(memory_space=pl.ANY)], out_specs=pl.BlockSpec(memory_space=pl.ANY),
        out_shape=jax.ShapeDtypeStruct((N_DEV, rows, width), buf.dtype),
        scratch_shapes=[pltpu.SemaphoreType.DMA((N_DEV - 1,)), pltpu.SemaphoreType.DMA((N_DEV - 1,)),
                        pltpu.SemaphoreType.DMA],
    )(buf)


def _adamw(parts, w, m, v, *, name):
    rows, width = w.shape
    tr = 208
    assert rows % tr == 0

    def body(p_ref, w_ref, m_ref, v_ref, g_ref, d_ref, nm_ref, nv_ref):
        g = p_ref[0]
        for k in range(1, N_DEV):
            g = g + p_ref[k]
        mn = ADAM_B1 * m_ref[...] + (1.0 - ADAM_B1) * g
        vn = ADAM_B2 * v_ref[...] + (1.0 - ADAM_B2) * jnp.square(g)
        m_hat = mn / (1.0 - ADAM_B1 ** ADAM_STEP)
        v_hat = vn / (1.0 - ADAM_B2 ** ADAM_STEP)
        g_ref[...] = g
        d_ref[...] = -ADAM_LR * (m_hat / (jnp.sqrt(v_hat) + ADAM_EPS) + ADAM_WD * w_ref[...])
        nm_ref[...] = mn
        nv_ref[...] = vn

    row = pl.BlockSpec((tr, width), lambda i: (i, 0))
    return pl.pallas_call(
        body, name=name, grid=(rows // tr,),
        in_specs=[pl.BlockSpec((N_DEV, tr, width), lambda i: (0, i, 0)), row, row, row],
        out_specs=[row] * 4, out_shape=[jax.ShapeDtypeStruct((rows, width), F32)] * 4,
        compiler_params=_cparams(("parallel",)),
    )(parts, w, m, v)


def _pack_local(blocks):
    order = sorted(PARAMS, key=lambda p: p[2] is None)
    flat = jnp.concatenate([blocks[name].reshape(-1).astype(F32) for name, _, _ in order])
    flat = jnp.pad(flat, (0, PACK_ROWS * PACK_W - PACK_USED))
    return flat.reshape(PACK_ROWS, PACK_W)


def _unpack_local(buf):
    flat = buf.reshape(-1)
    out = {}
    for name, shape, axis in PARAMS:
        off, n = PACK_TABLE[name]
        out[name] = flat[off:off + n].reshape((1,) + _shard_shape(shape, axis))
    return out


def _unpack_gathered(buf):
    flat = buf.reshape(N_DEV, -1)
    out = {}
    for name, shape, axis in PARAMS:
        if axis is None:
            continue
        off, n = PACK_TABLE[name]
        seg = flat[:, off:off + n].reshape((N_DEV,) + _shard_shape(shape, axis))
        if axis == 0:
            out[name] = seg.reshape(shape)
        else:
            out[name] = seg.transpose(1, 0, 2).reshape(shape)
    return out


def _pack_by_destination(grads):
    order = sorted(PARAMS, key=lambda p: p[2] is None)
    cols = []
    for name, shape, axis in order:
        g = grads[name].astype(F32)
        if axis is None:
            cols.append(jnp.broadcast_to(g.reshape(1, -1), (N_DEV, g.size)))
        elif axis == 0:
            cols.append(g.reshape(N_DEV, -1))
        else:
            r, c = shape
            cols.append(g.reshape(r, N_DEV, c // N_DEV).transpose(1, 0, 2).reshape(N_DEV, -1))
    flat = jnp.concatenate(cols, axis=1)
    flat = jnp.pad(flat, ((0, 0), (0, PACK_ROWS * PACK_W - PACK_USED)))
    return flat.reshape(N_DEV, PACK_ROWS, PACK_W)


def _to_residues(a, d):
    s, c = a.shape
    return a.reshape(s // d, d, c).transpose(1, 0, 2).reshape(s, c)


def _from_residues(a, d):
    s, c = a.shape
    return a.reshape(d, s // d, c).transpose(1, 0, 2).reshape(s, c)


def _pad_heads(w, heads, width, lo, hi):
    r = w.shape[0]
    w = w.reshape(r, heads, width)[:, :, lo:hi]
    w = jnp.pad(w, ((0, 0), (0, 0), (0, LANES - (hi - lo))))
    return w.reshape(r, heads * LANES)


def _local_step(x, mem, positions, target, w):
    s = x.shape[0]
    bf = lambda a: a.astype(BF16)

    w_in = w["w_in"]
    kr_cols = jnp.pad(w_in[:, OFF_KV:OFF_KR], ((0, 0), (MLA_NOPE, LANES - MLA_QK_DIM)))
    w_a = bf(jnp.concatenate([w_in[:, :OFF_KV], kr_cols], axis=1))
    w_dm = bf(w_in[:, OFF_KR:OFF_MEMQ])
    w_g = bf(w_in[:, OFF_MEMQ:])
    w_in_t = jnp.concatenate([w_a, w_dm, w_g], axis=1).T
    wq = bf(_pad_heads(w["w_uq"], MLA_HEADS, MLA_QK_DIM, 0, MLA_QK_DIM))
    wk = bf(_pad_heads(w["w_ukv"], MLA_HEADS, MLA_NOPE + MLA_V, 0, MLA_NOPE))
    wv = bf(_pad_heads(w["w_ukv"], MLA_HEADS, MLA_NOPE + MLA_V, MLA_NOPE, MLA_NOPE + MLA_V))
    w_mkv = bf(w["w_mem_kv"])
    wb_mla = bf(jnp.pad(w["w_br_mla"].reshape(MLA_HEADS, MLA_V, D_MODEL),
                        ((0, 0), (0, LANES - MLA_V), (0, 0))).reshape(MLA_HEADS * LANES, D_MODEL))
    wb_dil, wb_mem, w_o = bf(w["w_br_dil"]), bf(w["w_br_mem"]), bf(w["w_o"])
    w_up, w_down = bf(w["w_ffn_up"]), bf(w["w_ffn_down"])
    tabs = _rope_tables(positions)
    slopes = [jnp.asarray(sl, F32) for sl in DIL_SLOPES]
    mla_scale = MLA_QK_DIM ** -0.5
    mem_scale = LANES ** -0.5
    MQ = 3 * DIL_HEADS

    h = _rms_fwd(x, w["g_pre_mix"], name="rms_pre_mix", out_dtype=BF16)
    p_a = _matmul(h, w_a, name="proj_a")
    p_dm = _matmul(h, w_dm, name="proj_dm", out_dtype=BF16)
    p_g = _matmul(h, w_g, name="proj_gate")
    c_q, c_kv, kr = p_a[:, :OFF_Q], p_a[:, OFF_Q:OFF_KV], p_a[:, OFF_KV:]

    qn = _rms_fwd(c_q, w["mla_q_norm"], name="rms_q", out_dtype=BF16)
    kvn = _rms_fwd(c_kv, w["mla_kv_norm"], name="rms_kv", out_dtype=BF16)
    q_raw = _matmul(qn, wq, name="mla_q_up")
    k_raw = _matmul(kvn, wk, name="mla_k_up")
    v_f = _matmul(kvn, wv, name="mla_v_up", out_dtype=BF16)
    q_f = _rope_fwd(q_raw, tabs, name="rope_q", scale=mla_scale)
    k_f = _rope_fwd(k_raw, tabs, name="rope_k", scale=1.0, add=kr)
    mla = dict(heads=MLA_HEADS, qoff=0, koff=0, voff=0, causal=True, scale=1.0, tq=512, tk=512)
    o_mla, lse_mla = _flash_fwd(q_f, k_f, v_f, name="mla_fwd", **mla)

    dil_in, dil_o, dil_lse = [], [], []
    for g, (_, d) in enumerate(DIL_PAIRS):
        if d == 1:
            arrs, offs = (p_dm, p_dm, p_dm), (4 * g, DIL_HEADS + 4 * g, 2 * DIL_HEADS + 4 * g)
        else:
            arrs = tuple(_to_residues(p_dm[:, (t * DIL_HEADS + 4 * g) * LANES:(t * DIL_HEADS + 4 * g + 4) * LANES], d)
                         for t in range(3))
            offs = (0, 0, 0)
        o_g, lse_g = _band_fwd(*arrs, slopes[g], name=f"dil_fwd_{g}", dilation=d,
                               qoff=offs[0], koff=offs[1], voff=offs[2])
        dil_in.append((arrs, offs))
        dil_o.append(_from_residues(o_g, d))
        dil_lse.append(_from_residues(lse_g, d))
    y_dil = _mix_fwd(dil_o, dil_lse, name="dil_mix")

    memn = _rms_fwd(mem, w["g_mem"], name="rms_mem", out_dtype=BF16)
    kv_m = _matmul(memn, w_mkv, name="mem_kv", out_dtype=BF16)
    memat = dict(heads=MEM_HEADS, qoff=MQ, koff=0, voff=MEM_HEADS, causal=False, scale=mem_scale, tq=512, tk=256)
    o_mem, lse_mem = _flash_fwd(p_dm, kv_m, kv_m, name="mem_fwd", **memat)

    b_mla = _matmul(o_mla, wb_mla, name="br_mla")
    b_dil = _matmul(y_dil, wb_dil, name="br_dil")
    b_mem = _matmul(o_mem, wb_mem, name="br_mem")
    merged = _gate_fwd(p_g, w["b_gate"], (b_mla, b_dil, b_mem), name="gate_fwd")
    z1 = _matmul(merged, w_o, name="out_proj")
    x1 = _rms_fwd(z1, w["g_post_mix"], name="rms_post_mix", out_dtype=F32, add=x)
    h2 = _rms_fwd(x1, w["g_pre_ffn"], name="rms_pre_ffn", out_dtype=BF16)
    u = _matmul(h2, w_up, name="ffn_up")
    zg, zv, act = _conv_fwd(u, w["conv_w"], w["conv_b"], name="conv_fwd")
    f = _matmul(act, w_down, name="ffn_down")
    dy, sq = _loss_fwd(x1, f, target, w["g_post_ffn"], name="loss")
    loss = 0.5 * jnp.sum(sq) / D_MODEL

    grads = {}
    df, grads["g_post_ffn"] = _rms_bwd(f, w["g_post_ffn"], dy, name="rms_post_ffn_bwd", out_dtype=BF16)
    da = _matmul(df, w_down.T, name="ffn_down_dx")
    grads["w_ffn_down"] = _matmul(act, df, name="ffn_down_dw", trans_a=True)
    dzg, dzv = _silu_bwd(da, zg, zv, name="silu_bwd")
    nct = D_FF // CONV_TC
    du_g, cacc_g = _conv_bwd(dzg, u, w["conv_w"], name="conv_bwd_gate", off=0)
    du_v, cacc_v = _conv_bwd(dzv, u, w["conv_w"], name="conv_bwd_val", off=nct)
    grads["conv_w"] = jnp.concatenate([cacc_g[0:3], cacc_v[0:3]], axis=1)
    grads["conv_b"] = jnp.concatenate([cacc_g[3:4], cacc_v[3:4]], axis=1)
    w_up_t = w_up.T
    dh2 = _matmul(du_g, w_up_t[:D_FF], name="ffn_up_dx_gate")
    dh2 = _matmul(du_v, w_up_t[D_FF:], name="ffn_up_dx_val", add=dh2)
    grads["w_ffn_up"] = jnp.concatenate([_matmul(h2, du_g, name="ffn_up_dw_gate", trans_a=True),
                                         _matmul(h2, du_v, name="ffn_up_dw_val", trans_a=True)], axis=1)
    dx1, grads["g_pre_ffn"] = _rms_bwd(x1, w["g_pre_ffn"], dh2, name="rms_pre_ffn_bwd", out_dtype=F32, add=dy)
    dz1, grads["g_post_mix"] = _rms_bwd(z1, w["g_post_mix"], dx1, name="rms_post_mix_bwd", out_dtype=BF16)
    dmerged = _matmul(dz1, w_o.T, name="out_proj_dx")
    grads["w_o"] = _matmul(merged, dz1, name="out_proj_dw", trans_a=True)
    db_mla, db_dil, db_mem, dgp, grads["b_gate"] = _gate_bwd(
        dmerged, p_g, w["b_gate"], (b_mla, b_dil, b_mem), name="gate_bwd")

    do_mla = _matmul(db_mla, wb_mla.T, name="br_mla_dx", out_dtype=BF16)
    g_wb_mla = _matmul(o_mla, db_mla, name="br_mla_dw", trans_a=True)
    grads["w_br_mla"] = g_wb_mla.reshape(MLA_HEADS, LANES, D_MODEL)[:, :MLA_V].reshape(MLA_HEADS * MLA_V, D_MODEL)
    delta_mla = _row_dot(do_mla, o_mla, name="mla_delta")
    dq_f = _flash_bwd_dq(q_f, k_f, v_f, do_mla, lse_mla, delta_mla, name="mla_bwd_dq", out_dtype=F32, **mla)
    dk_f, dv_f = _flash_bwd_dkv(q_f, k_f, v_f, do_mla, lse_mla, delta_mla, name="mla_bwd_dkv",
                                dk_dtype=F32, dv_dtype=BF16, **mla)
    dq_raw = _rope_bwd(dq_f, tabs, name="rope_q_bwd", scale=mla_scale, with_add=False)
    dk_raw, dkr = _rope_bwd(dk_f, tabs, name="rope_k_bwd", scale=1.0, with_add=True)
    dqn = _matmul(dq_raw, wq.T, name="mla_q_up_dx")
    g_wq = _matmul(qn, dq_raw, name="mla_q_up_dw", trans_a=True)
    grads["w_uq"] = g_wq.reshape(MLA_Q_RANK, MLA_HEADS, LANES)[:, :, :MLA_QK_DIM].reshape(MLA_Q_RANK, -1)
    dkvn = _matmul(dk_raw, wk.T, name="mla_k_up_dx")
    dkvn = _matmul(dv_f, wv.T, name="mla_v_up_dx", add=dkvn)
    g_wk = _matmul(kvn, dk_raw, name="mla_k_up_dw", trans_a=True).reshape(MLA_KV_RANK, MLA_HEADS, LANES)
    g_wv = _matmul(kvn, dv_f, name="mla_v_up_dw", trans_a=True).reshape(MLA_KV_RANK, MLA_HEADS, LANES)
    grads["w_ukv"] = jnp.concatenate([g_wk[:, :, :MLA_NOPE], g_wv[:, :, :MLA_V]], axis=2).reshape(MLA_KV_RANK, -1)
    dc_q, grads["mla_q_norm"] = _rms_bwd(c_q, w["mla_q_norm"], dqn, name="rms_q_bwd", out_dtype=BF16)
    dc_kv, grads["mla_kv_norm"] = _rms_bwd(c_kv, w["mla_kv_norm"], dkvn, name="rms_kv_bwd", out_dtype=BF16)

    dy_dil = _matmul(db_dil, wb_dil.T, name="br_dil_dx")
    grads["w_br_dil"] = _matmul(y_dil, db_dil, name="br_dil_dw", trans_a=True)
    mix = _mix_bwd(dy_dil, dil_o, dil_lse, name="dil_mix_bwd")
    d_dil = [[None] * 3 for _ in range(3)]
    for g, (_, d) in enumerate(DIL_PAIRS):
        arrs, offs = dil_in[g]
        do_g, dl_g, lse_g = mix[g], mix[3 + g], dil_lse[g]
        if d != 1:
            do_g, dl_g, lse_g = _to_residues(do_g, d), _to_residues(dl_g, d), _to_residues(lse_g, d)
        dq_g, dk_g, dv_g = _band_bwd(*arrs, do_g, lse_g, dl_g, slopes[g], name=f"dil_bwd_{g}", dilation=d,
                                     qoff=offs[0], koff=offs[1], voff=offs[2])
        for t, a in enumerate((dq_g, dk_g, dv_g)):
            d_dil[t][g] = a if d == 1 else _from_residues(a, d)

    do_mem = _matmul(db_mem, wb_mem.T, name="br_mem_dx", out_dtype=BF16)
    grads["w_br_mem"] = _matmul(o_mem, db_mem, name="br_mem_dw", trans_a=True)
    delta_mem = _row_dot(do_mem, o_mem, name="mem_delta")
    dq_mem = _flash_bwd_dq(p_dm, kv_m, kv_m, do_mem, lse_mem, delta_mem, name="mem_bwd_dq", out_dtype=BF16, **memat)
    dk_mem, dv_mem = _flash_bwd_dkv(p_dm, kv_m, kv_m, do_mem, lse_mem, delta_mem, name="mem_bwd_dkv",
                                    dk_dtype=BF16, dv_dtype=BF16, **memat)
    dkv_m = jnp.concatenate([dk_mem, dv_mem], axis=1)
    dmemn = _matmul(dkv_m, w_mkv.T, name="mem_kv_dx")
    grads["w_mem_kv"] = _matmul(memn, dkv_m, name="mem_kv_dw", trans_a=True)
    _, grads["g_mem"] = _rms_bwd(mem, w["g_mem"], dmemn, name="rms_mem_bwd", out_dtype=BF16)

    dp_all = jnp.concatenate([dc_q, dc_kv, bf(dkr)] + d_dil[0] + d_dil[1] + d_dil[2] + [dq_mem, dgp], axis=1)
    dh = _matmul(dp_all, w_in_t, name="proj_dx")
    g_in = _matmul(h, dp_all, name="proj_dw", trans_a=True)
    grads["w_in"] = jnp.concatenate(
        [g_in[:, :OFF_KV], g_in[:, OFF_KV + MLA_NOPE:OFF_KV + MLA_QK_DIM], g_in[:, N_A:]], axis=1)
    dx, grads["g_pre_mix"] = _rms_bwd(x, w["g_pre_mix"], dh, name="rms_pre_mix_bwd", out_dtype=F32, add=dx1)
    return loss, dx, grads


def kernel(x, mem, positions, g_pre_mix, w_in, b_gate, mla_q_norm, w_uq, mla_kv_norm, w_ukv, g_mem, w_mem_kv, w_br_mla, w_br_dil, w_br_mem, w_o, g_post_mix, g_pre_ffn, w_ffn_up, conv_w, conv_b, w_ffn_down, g_post_ffn, loss_target, m_g_pre_mix, m_w_in, m_b_gate, m_mla_q_norm, m_w_uq, m_mla_kv_norm, m_w_ukv, m_g_mem, m_w_mem_kv, m_w_br_mla, m_w_br_dil, m_w_br_mem, m_w_o, m_g_post_mix, m_g_pre_ffn, m_w_ffn_up, m_conv_w, m_conv_b, m_w_ffn_down, m_g_post_ffn, v_g_pre_mix, v_w_in, v_b_gate, v_mla_q_norm, v_w_uq, v_mla_kv_norm, v_w_ukv, v_g_mem, v_w_mem_kv, v_w_br_mla, v_w_br_dil, v_w_br_mem, v_w_o, v_g_post_mix, v_g_pre_ffn, v_w_ffn_up, v_conv_w, v_conv_b, v_w_ffn_down, v_g_post_ffn):
    local = dict(g_pre_mix=g_pre_mix, w_in=w_in, b_gate=b_gate, mla_q_norm=mla_q_norm, w_uq=w_uq,
                 mla_kv_norm=mla_kv_norm, w_ukv=w_ukv, g_mem=g_mem, w_mem_kv=w_mem_kv, w_br_mla=w_br_mla,
                 w_br_dil=w_br_dil, w_br_mem=w_br_mem, w_o=w_o, g_post_mix=g_post_mix, g_pre_ffn=g_pre_ffn,
                 w_ffn_up=w_ffn_up, conv_w=conv_w, conv_b=conv_b, w_ffn_down=w_ffn_down, g_post_ffn=g_post_ffn)
    mom_m = dict(g_pre_mix=m_g_pre_mix, w_in=m_w_in, b_gate=m_b_gate, mla_q_norm=m_mla_q_norm, w_uq=m_w_uq,
                 mla_kv_norm=m_mla_kv_norm, w_ukv=m_w_ukv, g_mem=m_g_mem, w_mem_kv=m_w_mem_kv, w_br_mla=m_w_br_mla,
                 w_br_dil=m_w_br_dil, w_br_mem=m_w_br_mem, w_o=m_w_o, g_post_mix=m_g_post_mix,
                 g_pre_ffn=m_g_pre_ffn, w_ffn_up=m_w_ffn_up, conv_w=m_conv_w, conv_b=m_conv_b,
                 w_ffn_down=m_w_ffn_down, g_post_ffn=m_g_post_ffn)
    mom_v = dict(g_pre_mix=v_g_pre_mix, w_in=v_w_in, b_gate=v_b_gate, mla_q_norm=v_mla_q_norm, w_uq=v_w_uq,
                 mla_kv_norm=v_mla_kv_norm, w_ukv=v_w_ukv, g_mem=v_g_mem, w_mem_kv=v_w_mem_kv, w_br_mla=v_w_br_mla,
                 w_br_dil=v_w_br_dil, w_br_mem=v_w_br_mem, w_o=v_w_o, g_post_mix=v_g_post_mix,
                 g_pre_ffn=v_g_pre_ffn, w_ffn_up=v_w_ffn_up, conv_w=v_conv_w, conv_b=v_conv_b,
                 w_ffn_down=v_w_ffn_down, g_post_ffn=v_g_post_ffn)

    w_pack = _pack_local(local)
    gathered = _exchange(w_pack, name="gather_weights", gather=True)
    full = _unpack_gathered(gathered)
    for name, shape, axis in PARAMS:
        if axis is None:
            full[name] = local[name].reshape(1, -1)

    loss, dx, grads = _local_step(x[0], mem[0], positions[0], loss_target[0], full)

    parts = _exchange(_pack_by_destination(grads), name="exchange_grads", gather=False)
    g_pack, d_pack, m_pack, v_pack = _adamw(parts, w_pack, _pack_local(mom_m), _pack_local(mom_v), name="adamw")

    loss = lax.psum(loss, ("x", "y", "c"))
    outs = [loss, dx[None]]
    for buf in (g_pack, d_pack, m_pack, v_pack):
        un = _unpack_local(buf)
        outs.extend(un[name] for name in PARAM_NAMES)
    return tuple(outs)
```

```python
import functools

import numpy as np
import jax
import jax.numpy as jnp
from jax import lax
from jax.experimental import pallas as pl
from jax.experimental.pallas import tpu as pltpu

F32 = jnp.float32
BF16 = jnp.bfloat16

N_DEV = 8
D_MODEL = 1024
RMS_EPS = 1e-6
NEG_INF = -1e30
LANES = 128
SUBLANES = 8
BLOCK = 128

MLA_HEADS = 8
MLA_NOPE = 64
MLA_ROPE = 32
MLA_V = 64
MLA_QK_DIM = MLA_NOPE + MLA_ROPE
MLA_Q_RANK = 384
MLA_KV_RANK = 256
ROPE_THETA = 10000.0
DIL_PAIRS = ((128, 1), (512, 4), (2048, 16))
DIL_HEADS_PER_GROUP = 4
DIL_HEADS = 12
MEM_HEADS = 4
D_FF = 2816
OFF_Q = MLA_Q_RANK
OFF_KV = OFF_Q + MLA_KV_RANK
OFF_KR = OFF_KV + MLA_ROPE
OFF_DIL = OFF_KR + 3 * DIL_HEADS * LANES
OFF_MEMQ = OFF_DIL + MEM_HEADS * LANES
D_IN = OFF_MEMQ + 3 * D_MODEL
N_A = OFF_KV + LANES
N_DM = 3 * DIL_HEADS * LANES + MEM_HEADS * LANES

ADAM_LR = 0.001
ADAM_B1 = 0.9
ADAM_B2 = 0.999
ADAM_EPS = 1e-08
ADAM_WD = 0.01
ADAM_STEP = 10

VMEM_LIMIT = 48 * 1024 * 1024
VMEM_LIMIT_BIG = 58 * 1024 * 1024
PACK_W = 1024

_ALIBI_BASE = np.exp2(-8.0 * np.arange(1, DIL_HEADS + 1) / DIL_HEADS)
DIL_SLOPES = [[float(_ALIBI_BASE[hh * 3 + g]) for hh in range(DIL_HEADS_PER_GROUP)] for g in range(3)]

PARAMS = (
    ("g_pre_mix", (1024,), None), ("w_in", (1024, D_IN), 1), ("b_gate", (3072,), None),
    ("mla_q_norm", (384,), None), ("w_uq", (384, 768), 1), ("mla_kv_norm", (256,), None),
    ("w_ukv", (256, 1024), 1), ("g_mem", (1024,), None), ("w_mem_kv", (1024, 1024), 0),
    ("w_br_mla", (512, 1024), 1), ("w_br_dil", (512, 1024), 1), ("w_br_mem", (512, 1024), 1),
    ("w_o", (1024, 1024), 0), ("g_post_mix", (1024,), None), ("g_pre_ffn", (1024,), None),
    ("w_ffn_up", (1024, 2 * D_FF), 1), ("conv_w", (3, 2 * D_FF), 1), ("conv_b", (2 * D_FF,), None),
    ("w_ffn_down", (D_FF, 1024), 0), ("g_post_ffn", (1024,), None),
)
PARAM_NAMES = tuple(p[0] for p in PARAMS)


def _shard_shape(shape, axis):
    if axis is None:
        return shape
    return tuple(s // N_DEV if a == axis else s for a, s in enumerate(shape))


SHARDED = tuple(p for p in PARAMS if p[2] is not None)
REPLICATED = tuple(p for p in PARAMS if p[2] is None)


def _layout():
    off, table = 0, {}
    for name, shape, _ in REPLICATED:
        table[name] = (off, shape[0])
        off += shape[0]
    rows = -(-off // PACK_W)
    rows = -(-rows // SUBLANES) * SUBLANES
    return table, off, rows


PACK_TABLE, PACK_USED, PACK_ROWS = _layout()


def _pick(n, cap):
    best = None
    for t in range(LANES, min(n, cap) + 1, LANES):
        if n % t == 0:
            best = t
    return best if best is not None else n


def _rows(n, cap):
    best = None
    for t in range(SUBLANES, min(n, cap) + 1, SUBLANES):
        if n % t == 0:
            best = t
    return best if best is not None else n


def _cparams(sem, vmem=VMEM_LIMIT):
    return pltpu.CompilerParams(dimension_semantics=sem, vmem_limit_bytes=vmem)


def _matmul(a, b, *, name, out_dtype=F32, trans_a=False, add=None, tm=1024, tn=1408, tk=640):
    if trans_a:
        kc, m = a.shape
    else:
        m, kc = a.shape
    n = b.shape[1]
    assert b.shape[0] == kc
    tm, tn, tk = _pick(m, tm), _pick(n, tn), _pick(kc, tk)
    nk = kc // tk

    def body(*refs):
        if add is None:
            a_ref, b_ref, o_ref, acc = refs
        else:
            a_ref, b_ref, c_ref, o_ref, acc = refs
        k = pl.program_id(2)

        @pl.when(k == 0)
        def _():
            if add is None:
                acc[...] = jnp.zeros_like(acc)
            else:
                acc[...] = c_ref[...].astype(F32)

        av = a_ref[...].astype(BF16)
        bv = b_ref[...].astype(BF16)
        if trans_a:
            acc[...] += lax.dot_general(av, bv, (((0,), (0,)), ((), ())), preferred_element_type=F32)
        else:
            acc[...] += jnp.dot(av, bv, preferred_element_type=F32)

        @pl.when(k == nk - 1)
        def _():
            o_ref[...] = acc[...].astype(out_dtype)

    if trans_a:
        a_spec = pl.BlockSpec((tk, tm), lambda i, j, k: (k, i))
    else:
        a_spec = pl.BlockSpec((tm, tk), lambda i, j, k: (i, k))
    in_specs = [a_spec, pl.BlockSpec((tk, tn), lambda i, j, k: (k, j))]
    args = [a, b]
    if add is not None:
        in_specs.append(pl.BlockSpec((tm, tn), lambda i, j, k: (i, j)))
        args.append(add)
    return pl.pallas_call(
        body, name=name, grid=(m // tm, n // tn, nk),
        in_specs=in_specs, out_specs=pl.BlockSpec((tm, tn), lambda i, j, k: (i, j)),
        out_shape=jax.ShapeDtypeStruct((m, n), out_dtype),
        scratch_shapes=[pltpu.VMEM((tm, tn), F32)],
        compiler_params=_cparams(("parallel", "parallel", "arbitrary")),
    )(*args)


def _rms_fwd(x, g, *, name, out_dtype, add=None):
    s, n = x.shape
    ts = _rows(s, 512)

    def body(*refs):
        if add is None:
            x_ref, g_ref, o_ref = refs
        else:
            x_ref, g_ref, a_ref, o_ref = refs
        xv = x_ref[...]
        r = lax.rsqrt(jnp.mean(xv * xv, axis=-1, keepdims=True) + RMS_EPS)
        y = xv * r * g_ref[...]
        if add is not None:
            y = a_ref[...] + y
        o_ref[...] = y.astype(out_dtype)

    row = pl.BlockSpec((ts, n), lambda i: (i, 0))
    in_specs = [row, pl.BlockSpec((1, n), lambda i: (0, 0))]
    args = [x, g]
    if add is not None:
        in_specs.append(row)
        args.append(add)
    return pl.pallas_call(
        body, name=name, grid=(s // ts,), in_specs=in_specs, out_specs=row,
        out_shape=jax.ShapeDtypeStruct((s, n), out_dtype),
        compiler_params=_cparams(("parallel",)),
    )(*args)


def _rms_bwd(x, g, dy, *, name, out_dtype, add=None):
    s, n = x.shape
    ts = _rows(s, 512)

    def body(*refs):
        if add is None:
            x_ref, g_ref, dy_ref, dx_ref, dg_ref = refs
        else:
            x_ref, g_ref, dy_ref, a_ref, dx_ref, dg_ref = refs
        i = pl.program_id(0)
        xv = x_ref[...]
        dyv = dy_ref[...].astype(F32)
        r = lax.rsqrt(jnp.mean(xv * xv, axis=-1, keepdims=True) + RMS_EPS)
        nx = xv * r
        gdy = dyv * g_ref[...]
        dx = r * (gdy - nx * jnp.mean(nx * gdy, axis=-1, keepdims=True))
        if add is not None:
            dx = a_ref[...] + dx
        dx_ref[...] = dx.astype(out_dtype)

        @pl.when(i == 0)
        def _():
            dg_ref[...] = jnp.zeros_like(dg_ref)

        dg_ref[...] += jnp.sum(dyv * nx, axis=0, keepdims=True)

    row = pl.BlockSpec((ts, n), lambda i: (i, 0))
    vec = pl.BlockSpec((1, n), lambda i: (0, 0))
    in_specs = [row, vec, row]
    args = [x, g, dy]
    if add is not None:
        in_specs.append(row)
        args.append(add)
    return pl.pallas_call(
        body, name=name, grid=(s // ts,), in_specs=in_specs, out_specs=[row, vec],
        out_shape=[jax.ShapeDtypeStruct((s, n), out_dtype), jax.ShapeDtypeStruct((1, n), F32)],
        compiler_params=_cparams(("arbitrary",)),
    )(*args)


def _loss_fwd(x1, f, target, g, *, name):
    s, n = x1.shape
    ts = _rows(s, 512)

    def body(x_ref, f_ref, t_ref, g_ref, dy_ref, sq_ref):
        i = pl.program_id(0)
        fv = f_ref[...]
        r = lax.rsqrt(jnp.mean(fv * fv, axis=-1, keepdims=True) + RMS_EPS)
        err = x_ref[...] + fv * r * g_ref[...] - t_ref[...]
        dy_ref[...] = err * (1.0 / n)

        @pl.when(i == 0)
        def _():
            sq_ref[...] = jnp.zeros_like(sq_ref)

        sq_ref[...] += jnp.sum(err * err, axis=0, keepdims=True)

    row = pl.BlockSpec((ts, n), lambda i: (i, 0))
    vec = pl.BlockSpec((1, n), lambda i: (0, 0))
    return pl.pallas_call(
        body, name=name, grid=(s // ts,), in_specs=[row, row, row, vec], out_specs=[row, vec],
        out_shape=[jax.ShapeDtypeStruct((s, n), F32), jax.ShapeDtypeStruct((1, n), F32)],
        compiler_params=_cparams(("arbitrary",)),
    )(x1, f, target, g)


def _rope_tables(positions):
    half = MLA_ROPE // 2
    inv_freq = ROPE_THETA ** (-jnp.arange(half, dtype=F32) / half)
    ang = positions.astype(F32)[:, None] * inv_freq[None, :]
    cos, sin = jnp.cos(ang), jnp.sin(ang)
    s = positions.shape[0]
    one = jnp.ones((s, MLA_NOPE), F32)
    zero = jnp.zeros((s, MLA_NOPE), F32)
    pad1 = jnp.ones((s, LANES - MLA_QK_DIM), F32)
    pad0 = jnp.zeros((s, LANES - MLA_QK_DIM), F32)
    zh = jnp.zeros((s, half), F32)
    c_tab = jnp.concatenate([one, cos, cos, pad1], axis=1)
    s1_tab = jnp.concatenate([zero, -sin, zh, pad0], axis=1)
    s2_tab = jnp.concatenate([zero, zh, sin, pad0], axis=1)
    return c_tab, s1_tab, s2_tab


def _rope_fwd(x, tabs, *, name, scale, add=None):
    s, n = x.shape
    nh = n // LANES
    ts = _rows(s, 512)
    half = MLA_ROPE // 2

    def body(*refs):
        if add is None:
            x_ref, c_ref, s1_ref, s2_ref, o_ref = refs
        else:
            x_ref, a_ref, c_ref, s1_ref, s2_ref, o_ref = refs
        c, s1, s2 = c_ref[...], s1_ref[...], s2_ref[...]
        for h in range(nh):
            xh = x_ref[:, h * LANES:(h + 1) * LANES]
            if add is not None:
                xh = xh + a_ref[...]
            y = xh * c + pltpu.roll(xh, LANES - half, 1) * s1 + pltpu.roll(xh, half, 1) * s2
            o_ref[:, h * LANES:(h + 1) * LANES] = (y * scale).astype(BF16)

    row = pl.BlockSpec((ts, n), lambda i: (i, 0))
    tab = pl.BlockSpec((ts, LANES), lambda i: (i, 0))
    in_specs = [row] + ([tab] if add is not None else []) + [tab, tab, tab]
    args = [x] + ([add] if add is not None else []) + list(tabs)
    return pl.pallas_call(
        body, name=name, grid=(s // ts,), in_specs=in_specs, out_specs=row,
        out_shape=jax.ShapeDtypeStruct((s, n), BF16),
        compiler_params=_cparams(("parallel",)),
    )(*args)


def _rope_bwd(dy, tabs, *, name, scale, with_add):
    s, n = dy.shape
    nh = n // LANES
    ts = _rows(s, 512)
    half = MLA_ROPE // 2

    def body(*refs):
        if with_add:
            dy_ref, c_ref, s1_ref, s2_ref, dx_ref, da_ref = refs
        else:
            dy_ref, c_ref, s1_ref, s2_ref, dx_ref = refs
        c, s1, s2 = c_ref[...], s1_ref[...], s2_ref[...]
        tot = None
        for h in range(nh):
            g = dy_ref[:, h * LANES:(h + 1) * LANES].astype(F32)
            dx = (g * c + pltpu.roll(g * s1, half, 1) + pltpu.roll(g * s2, LANES - half, 1)) * scale
            dx_ref[:, h * LANES:(h + 1) * LANES] = dx.astype(BF16)
            tot = dx if tot is None else tot + dx
        if with_add:
            da_ref[...] = tot

    row = pl.BlockSpec((ts, n), lambda i: (i, 0))
    tab = pl.BlockSpec((ts, LANES), lambda i: (i, 0))
    out_specs = [row, tab] if with_add else row
    out_shape = [jax.ShapeDtypeStruct((s, n), BF16)]
    if with_add:
        out_shape.append(jax.ShapeDtypeStruct((s, LANES), F32))
    else:
        out_shape = out_shape[0]
    return pl.pallas_call(
        body, name=name, grid=(s // ts,), in_specs=[row, tab, tab, tab], out_specs=out_specs,
        out_shape=out_shape, compiler_params=_cparams(("parallel",)),
    )(dy, *tabs)


def _scores(q, k, scale, diag):
    s = lax.dot_general(q, k, (((1,), (1,)), ((), ())), preferred_element_type=F32)
    if scale != 1.0:
        s = s * scale
    if diag:
        rows = lax.broadcasted_iota(jnp.int32, s.shape, 0)
        cols = lax.broadcasted_iota(jnp.int32, s.shape, 1)
        s = jnp.where(cols <= rows, s, NEG_INF)
    return s


def _flash_fwd(q, k, v, *, name, heads, qoff, koff, voff, causal, scale, tq, tk):
    s_q, s_kv = q.shape[0], k.shape[0]
    tq, tk = min(tq, s_q), min(tk, s_kv)
    nq, nk = s_q // tq, s_kv // tk
    if causal:
        assert tq == tk and s_q == s_kv

    def body(q_ref, k_ref, v_ref, o_ref, lse_ref, m_s, l_s, acc):
        i, j = pl.program_id(1), pl.program_id(2)

        @pl.when(j == 0)
        def _():
            m_s[...] = jnp.full_like(m_s, NEG_INF)
            l_s[...] = jnp.zeros_like(l_s)
            acc[...] = jnp.zeros_like(acc)

        def step(diag):
            s = _scores(q_ref[...], k_ref[...], scale, diag)
            m_prev = m_s[...]
            m_cur = jnp.maximum(m_prev, jnp.max(s, axis=1, keepdims=True))
            alpha = jnp.exp(m_prev - m_cur)
            p = jnp.exp(s - m_cur[:, :1])
            l_s[...] = alpha * l_s[...] + jnp.sum(p, axis=1, keepdims=True)
            acc[...] = alpha * acc[...] + jnp.dot(p.astype(BF16), v_ref[...], preferred_element_type=F32)
            m_s[...] = m_cur

        def finish():
            o_ref[...] = (acc[...] / l_s[...]).astype(o_ref.dtype)
            lse_ref[...] = m_s[...] + jnp.log(l_s[...])

        if causal:
            @pl.when(j < i)
            def _():
                step(False)

            @pl.when(j == i)
            def _():
                step(True)
                finish()
        else:
            step(False)

            @pl.when(j == nk - 1)
            def _():
                finish()

    def kv_idx(off):
        if causal:
            return lambda h, i, j: (jnp.minimum(j, i), off + h)
        return lambda h, i, j: (j, off + h)

    blk_q = pl.BlockSpec((tq, LANES), lambda h, i, j: (i, qoff + h))
    out_q = pl.BlockSpec((tq, LANES), lambda h, i, j: (i, h))
    return pl.pallas_call(
        body, name=name, grid=(heads, nq, nk),
        in_specs=[blk_q, pl.BlockSpec((tk, LANES), kv_idx(koff)), pl.BlockSpec((tk, LANES), kv_idx(voff))],
        out_specs=[out_q, out_q],
        out_shape=[jax.ShapeDtypeStruct((s_q, heads * LANES), BF16),
                   jax.ShapeDtypeStruct((s_q, heads * LANES), F32)],
        scratch_shapes=[pltpu.VMEM((tq, LANES), F32)] * 3,
        compiler_params=_cparams(("parallel", "parallel", "arbitrary")),
    )(q, k, v)


def _flash_bwd_dq(q, k, v, do, lse, delta, *, name, heads, qoff, koff, voff, causal, scale, tq, tk, out_dtype):
    s_q, s_kv = q.shape[0], k.shape[0]
    tq, tk = min(tq, s_q), min(tk, s_kv)
    nq, nk = s_q // tq, s_kv // tk

    def body(q_ref, k_ref, v_ref, do_ref, lse_ref, dl_ref, dq_ref, acc):
        i, j = pl.program_id(1), pl.program_id(2)

        @pl.when(j == 0)
        def _():
            acc[...] = jnp.zeros_like(acc)

        def step(diag):
            s = _scores(q_ref[...], k_ref[...], scale, diag)
            p = jnp.exp(s - lse_ref[:, :1])
            dp = lax.dot_general(do_ref[...], v_ref[...], (((1,), (1,)), ((), ())), preferred_element_type=F32)
            ds = p * (dp - dl_ref[:, :1])
            acc[...] += jnp.dot(ds.astype(BF16), k_ref[...], preferred_element_type=F32)

        def finish():
            dq_ref[...] = (acc[...] * scale).astype(out_dtype)

        if causal:
            @pl.when(j < i)
            def _():
                step(False)

            @pl.when(j == i)
            def _():
                step(True)
                finish()
        else:
            step(False)

            @pl.when(j == nk - 1)
            def _():
                finish()

    def kv_idx(off):
        if causal:
            return lambda h, i, j: (jnp.minimum(j, i), off + h)
        return lambda h, i, j: (j, off + h)

    blk_q = pl.BlockSpec((tq, LANES), lambda h, i, j: (i, qoff + h))
    blk_h = pl.BlockSpec((tq, LANES), lambda h, i, j: (i, h))
    return pl.pallas_call(
        body, name=name, grid=(heads, nq, nk),
        in_specs=[blk_q, pl.BlockSpec((tk, LANES), kv_idx(koff)), pl.BlockSpec((tk, LANES), kv_idx(voff)),
                  blk_h, blk_h, blk_h],
        out_specs=blk_h,
        out_shape=jax.ShapeDtypeStruct((s_q, heads * LANES), out_dtype),
        scratch_shapes=[pltpu.VMEM((tq, LANES), F32)],
        compiler_params=_cparams(("parallel", "parallel", "arbitrary")),
    )(q, k, v, do, lse, delta)


def _flash_bwd_dkv(q, k, v, do, lse, delta, *, name, heads, qoff, koff, voff, causal, scale, tq, tk,
                   dk_dtype, dv_dtype):
    s_q, s_kv = q.shape[0], k.shape[0]
    tq, tk = min(tq, s_q), min(tk, s_kv)
    nq, nk = s_q // tq, s_kv // tk

    def body(q_ref, k_ref, v_ref, do_ref, lse_ref, dl_ref, dk_ref, dv_ref, dk_acc, dv_acc):
        j, i = pl.program_id(1), pl.program_id(2)

        @pl.when(i == 0)
        def _():
            dk_acc[...] = jnp.zeros_like(dk_acc)
            dv_acc[...] = jnp.zeros_like(dv_acc)

        def step(diag):
            s = _scores(q_ref[...], k_ref[...], scale, diag)
            p = jnp.exp(s - lse_ref[:, :1])
            dov = do_ref[...]
            dp = lax.dot_general(dov, v_ref[...], (((1,), (1,)), ((), ())), preferred_element_type=F32)
            ds = p * (dp - dl_ref[:, :1])
            dv_acc[...] += lax.dot_general(p.astype(BF16), dov, (((0,), (0,)), ((), ())),
                                           preferred_element_type=F32)
            dk_acc[...] += lax.dot_general(ds.astype(BF16), q_ref[...], (((0,), (0,)), ((), ())),
                                           preferred_element_type=F32)

        if causal:
            @pl.when(i > j)
            def _():
                step(False)

            @pl.when(i == j)
            def _():
                step(True)
        else:
            step(False)

        @pl.when(i == nq - 1)
        def _():
            dk_ref[...] = (dk_acc[...] * scale).astype(dk_dtype)
            dv_ref[...] = dv_acc[...].astype(dv_dtype)

    def q_idx(off):
        if causal:
            return lambda h, j, i: (jnp.maximum(i, j), off + h)
        return lambda h, j, i: (i, off + h)

    blk_h = pl.BlockSpec((tq, LANES), q_idx(0))
    out_k = pl.BlockSpec((tk, LANES), lambda h, j, i: (j, h))
    return pl.pallas_call(
        body, name=name, grid=(heads, nk, nq),
        in_specs=[pl.BlockSpec((tq, LANES), q_idx(qoff)),
                  pl.BlockSpec((tk, LANES), lambda h, j, i: (j, koff + h)),
                  pl.BlockSpec((tk, LANES), lambda h, j, i: (j, voff + h)),
                  blk_h, blk_h, blk_h],
        out_specs=[out_k, out_k],
        out_shape=[jax.ShapeDtypeStruct((s_kv, heads * LANES), dk_dtype),
                   jax.ShapeDtypeStruct((s_kv, heads * LANES), dv_dtype)],
        scratch_shapes=[pltpu.VMEM((tk, LANES), F32)] * 2,
        compiler_params=_cparams(("parallel", "parallel", "arbitrary")),
    )(q, k, v, do, lse, delta)


def _row_dot(a, b, *, name):
    s, n = a.shape
    nh = n // LANES
    ts = _rows(s, 512)

    def body(a_ref, b_ref, o_ref):
        for h in range(nh):
            sl = slice(h * LANES, (h + 1) * LANES)
            d = jnp.sum(a_ref[:, sl].astype(F32) * b_ref[:, sl].astype(F32), axis=1, keepdims=True)
            o_ref[:, sl] = jnp.broadcast_to(d, (ts, LANES))

    row = pl.BlockSpec((ts, n), lambda i: (i, 0))
    return pl.pallas_call(
        body, name=name, grid=(s // ts,), in_specs=[row, row], out_specs=row,
        out_shape=jax.ShapeDtypeStruct((s, n), F32), compiler_params=_cparams(("parallel",)),
    )(a, b)


def _band_masks(dilation, slope):
    qi = lax.broadcasted_iota(jnp.int32, (BLOCK, 2 * BLOCK), 0)
    kj = lax.broadcasted_iota(jnp.int32, (BLOCK, 2 * BLOCK), 1)
    dist = qi + BLOCK - kj
    valid = (dist >= 0) & (dist <= BLOCK)
    bias = -slope * (dist * dilation).astype(F32)
    return valid, bias


def _band_fwd(q, k, v, slopes, *, name, dilation, qoff, koff, voff):
    s = q.shape[0]
    sub = s // dilation
    nb = sub // BLOCK
    assert nb * BLOCK == sub
    scale = LANES ** -0.5

    def body(sl_ref, q_ref, k_ref, v_ref, o_ref, lse_ref):
        slope = sl_ref[pl.program_id(0)]
        valid2, bias2 = _band_masks(dilation, slope)
        valid1, bias1 = valid2[:, BLOCK:], bias2[:, BLOCK:]

        def block(start_q, kk, vv, valid, bias):
            qb = q_ref[pl.ds(start_q, BLOCK), :]
            sc = lax.dot_general(qb, kk, (((1,), (1,)), ((), ())), preferred_element_type=F32) * scale
            sc = jnp.where(valid, sc + bias, NEG_INF)
            m = jnp.max(sc, axis=1, keepdims=True)
            e = jnp.exp(sc - m)
            den = jnp.sum(e, axis=1, keepdims=True)
            p = (e / den).astype(BF16)
            o_ref[pl.ds(start_q, BLOCK), :] = jnp.dot(p, vv, preferred_element_type=F32)
            lse_ref[pl.ds(start_q, BLOCK), :] = jnp.broadcast_to(m + jnp.log(den), (BLOCK, LANES))

        block(0, k_ref[0:BLOCK, :], v_ref[0:BLOCK, :], valid1, bias1)

        def loop(jj, carry):
            start_q = pl.multiple_of(jj * BLOCK, BLOCK)
            start_k = pl.multiple_of((jj - 1) * BLOCK, BLOCK)
            block(start_q, k_ref[pl.ds(start_k, 2 * BLOCK), :], v_ref[pl.ds(start_k, 2 * BLOCK), :], valid2, bias2)
            return carry

        lax.fori_loop(1, nb, loop, 0)

    def spec(off):
        return pl.BlockSpec((sub, LANES), lambda h, r: (r, off + h))

    out = pl.BlockSpec((sub, LANES), lambda h, r: (r, h))
    return pl.pallas_call(
        body, name=name, grid=(DIL_HEADS_PER_GROUP, dilation),
        in_specs=[pl.BlockSpec(memory_space=pltpu.SMEM), spec(qoff), spec(koff), spec(voff)],
        out_specs=[out, out],
        out_shape=[jax.ShapeDtypeStruct((s, DIL_HEADS_PER_GROUP * LANES), F32)] * 2,
        compiler_params=_cparams(("parallel", "parallel"), VMEM_LIMIT_BIG),
    )(slopes, q, k, v)


def _band_bwd(q, k, v, do, lse, delta, slopes, *, name, dilation, qoff, koff, voff):
    s = q.shape[0]
    sub = s // dilation
    nb = sub // BLOCK
    scale = LANES ** -0.5

    def body(sl_ref, q_ref, k_ref, v_ref, do_ref, lse_ref, dl_ref, dq_ref, dk_ref, dv_ref):
        slope = sl_ref[pl.program_id(0)]
        valid2, bias2 = _band_masks(dilation, slope)
        valid1, bias1 = valid2[:, BLOCK:], bias2[:, BLOCK:]

        def block(start_q, kk, vv, valid, bias):
            qb = q_ref[pl.ds(start_q, BLOCK), :]
            dob = do_ref[pl.ds(start_q, BLOCK), :]
            sc = lax.dot_general(qb, kk, (((1,), (1,)), ((), ())), preferred_element_type=F32) * scale
            sc = jnp.where(valid, sc + bias, NEG_INF)
            p = jnp.exp(sc - lse_ref[pl.ds(start_q, BLOCK), :][:, :1])
            dp = lax.dot_general(dob, vv, (((1,), (1,)), ((), ())), preferred_element_type=F32)
            ds = (p * (dp - dl_ref[pl.ds(start_q, BLOCK), :][:, :1])).astype(BF16)
            dq = jnp.dot(ds, kk, preferred_element_type=F32) * scale
            dq_ref[pl.ds(start_q, BLOCK), :] = dq.astype(BF16)
            dkk = lax.dot_general(ds, qb, (((0,), (0,)), ((), ())), preferred_element_type=F32) * scale
            dvv = lax.dot_general(p.astype(BF16), dob, (((0,), (0,)), ((), ())), preferred_element_type=F32)
            return dkk, dvv

        carry0 = block(0, k_ref[0:BLOCK, :], v_ref[0:BLOCK, :], valid1, bias1)

        def loop(jj, carry):
            dk_part, dv_part = carry
            start_q = pl.multiple_of(jj * BLOCK, BLOCK)
            start_k = pl.multiple_of((jj - 1) * BLOCK, BLOCK)
            dkk, dvv = block(start_q, k_ref[pl.ds(start_k, 2 * BLOCK), :], v_ref[pl.ds(start_k, 2 * BLOCK), :],
                             valid2, bias2)
            dk_ref[pl.ds(start_k, BLOCK), :] = (dk_part + dkk[:BLOCK]).astype(BF16)
            dv_ref[pl.ds(start_k, BLOCK), :] = (dv_part + dvv[:BLOCK]).astype(BF16)
            return dkk[BLOCK:], dvv[BLOCK:]

        dk_last, dv_last = lax.fori_loop(1, nb, loop, carry0)
        dk_ref[(nb - 1) * BLOCK:nb * BLOCK, :] = dk_last.astype(BF16)
        dv_ref[(nb - 1) * BLOCK:nb * BLOCK, :] = dv_last.astype(BF16)

    def spec(off):
        return pl.BlockSpec((sub, LANES), lambda h, r: (r, off + h))

    out = spec(0)
    return pl.pallas_call(
        body, name=name, grid=(DIL_HEADS_PER_GROUP, dilation),
        in_specs=[pl.BlockSpec(memory_space=pltpu.SMEM), spec(qoff), spec(koff), spec(voff), out, out, out],
        out_specs=[out, out, out],
        out_shape=[jax.ShapeDtypeStruct((s, DIL_HEADS_PER_GROUP * LANES), BF16)] * 3,
        compiler_params=_cparams(("parallel", "parallel"), VMEM_LIMIT_BIG),
    )(slopes, q, k, v, do, lse, delta)


def _mix_fwd(outs, lses, *, name):
    s, n = outs[0].shape
    ts = _rows(s, 512)

    def body(o0, o1, o2, l0, l1, l2, y_ref):
        la, lb, lc = l0[...], l1[...], l2[...]
        m = jnp.maximum(jnp.maximum(la, lb), lc)
        ea, eb, ec = jnp.exp(la - m), jnp.exp(lb - m), jnp.exp(lc - m)
        den = ea + eb + ec
        y = (ea / den) * o0[...] + (eb / den) * o1[...] + (ec / den) * o2[...]
        y_ref[...] = y.astype(BF16)

    row = pl.BlockSpec((ts, n), lambda i: (i, 0))
    return pl.pallas_call(
        body, name=name, grid=(s // ts,), in_specs=[row] * 6, out_specs=row,
        out_shape=jax.ShapeDtypeStruct((s, n), BF16), compiler_params=_cparams(("parallel",)),
    )(*outs, *lses)


def _mix_bwd(dy, outs, lses, *, name):
    s, n = dy.shape
    nh = n // LANES
    ts = _rows(s, 256)

    def body(dy_ref, o0, o1, o2, l0, l1, l2, d0, d1, d2, e0, e1, e2):
        la, lb, lc = l0[...], l1[...], l2[...]
        m = jnp.maximum(jnp.maximum(la, lb), lc)
        ea, eb, ec = jnp.exp(la - m), jnp.exp(lb - m), jnp.exp(lc - m)
        den = ea + eb + ec
        wa, wb, wc = ea / den, eb / den, ec / den
        dyv = dy_ref[...]
        y = wa * o0[...] + wb * o1[...] + wc * o2[...]
        prod = dyv * y
        d0[...] = (wa * dyv).astype(BF16)
        d1[...] = (wb * dyv).astype(BF16)
        d2[...] = (wc * dyv).astype(BF16)
        for h in range(nh):
            sl = slice(h * LANES, (h + 1) * LANES)
            t = jnp.sum(prod[:, sl], axis=1, keepdims=True)
            e0[:, sl] = wa[:, sl] * t
            e1[:, sl] = wb[:, sl] * t
            e2[:, sl] = wc[:, sl] * t

    row = pl.BlockSpec((ts, n), lambda i: (i, 0))
    return pl.pallas_call(
        body, name=name, grid=(s // ts,), in_specs=[row] * 7, out_specs=[row] * 6,
        out_shape=[jax.ShapeDtypeStruct((s, n), BF16)] * 3 + [jax.ShapeDtypeStruct((s, n), F32)] * 3,
        compiler_params=_cparams(("parallel",)),
    )(dy, *outs, *lses)


def _gate_fwd(gp, b_gate, branches, *, name):
    s = gp.shape[0]
    ts = _rows(s, 256)

    def body(gp_ref, b_ref, b0, b1, b2, o_ref):
        tot = None
        for i, br in enumerate((b0, b1, b2)):
            sl = slice(i * D_MODEL, (i + 1) * D_MODEL)
            t = jax.nn.sigmoid(gp_ref[:, sl] + b_ref[:, sl]) * br[...]
            tot = t if tot is None else tot + t
        o_ref[...] = tot.astype(BF16)

    row = pl.BlockSpec((ts, D_MODEL), lambda i: (i, 0))
    return pl.pallas_call(
        body, name=name, grid=(s // ts,),
        in_specs=[pl.BlockSpec((ts, 3 * D_MODEL), lambda i: (i, 0)), pl.BlockSpec((1, 3 * D_MODEL), lambda i: (0, 0)),
                  row, row, row],
        out_specs=row, out_shape=jax.ShapeDtypeStruct((s, D_MODEL), BF16),
        compiler_params=_cparams(("parallel",)),
    )(gp, b_gate, *branches)


def _gate_bwd(dm, gp, b_gate, branches, *, name):
    s = gp.shape[0]
    ts = _rows(s, 256)

    def body(dm_ref, gp_ref, b_ref, b0, b1, b2, d0, d1, d2, dgp_ref, db_ref):
        i = pl.program_id(0)

        @pl.when(i == 0)
        def _():
            db_ref[...] = jnp.zeros_like(db_ref)

        dmv = dm_ref[...]
        for k, (br, dbr) in enumerate(((b0, d0), (b1, d1), (b2, d2))):
            sl = slice(k * D_MODEL, (k + 1) * D_MODEL)
            sg = jax.nn.sigmoid(gp_ref[:, sl] + b_ref[:, sl])
            dbr[...] = (dmv * sg).astype(BF16)
            dg = dmv * br[...] * sg * (1.0 - sg)
            dgp_ref[:, sl] = dg.astype(BF16)
            db_ref[:, sl] += jnp.sum(dg, axis=0, keepdims=True)

    row = pl.BlockSpec((ts, D_MODEL), lambda i: (i, 0))
    wide = pl.BlockSpec((ts, 3 * D_MODEL), lambda i: (i, 0))
    vec = pl.BlockSpec((1, 3 * D_MODEL), lambda i: (0, 0))
    return pl.pallas_call(
        body, name=name, grid=(s // ts,),
        in_specs=[row, wide, vec, row, row, row], out_specs=[row, row, row, wide, vec],
        out_shape=[jax.ShapeDtypeStruct((s, D_MODEL), BF16)] * 3
        + [jax.ShapeDtypeStruct((s, 3 * D_MODEL), BF16), jax.ShapeDtypeStruct((1, 3 * D_MODEL), F32)],
        compiler_params=_cparams(("arbitrary",)),
    )(dm, gp, b_gate, *branches)


CONV_TC = 1408


def _shift_down(x, halo, k):
    rolled = pltpu.roll(x, k, 0)
    r8 = lax.broadcasted_iota(jnp.int32, halo.shape, 0)
    top = jnp.where(r8 < k, pltpu.roll(halo, k, 0), rolled[:SUBLANES])
    return jnp.concatenate([top, rolled[SUBLANES:]], axis=0)


def _shift_up(x, halo, k):
    n = x.shape[0]
    rolled = pltpu.roll(x, n - k, 0)
    r8 = lax.broadcasted_iota(jnp.int32, halo.shape, 0)
    bot = jnp.where(r8 >= SUBLANES - k, pltpu.roll(halo, SUBLANES - k, 0), rolled[n - SUBLANES:])
    return jnp.concatenate([rolled[:n - SUBLANES], bot], axis=0)


def _conv_fwd(u, conv_w, conv_b, *, name):
    s = u.shape[0]
    ts = _rows(s, 256)
    nct = D_FF // CONV_TC
    per8 = ts // SUBLANES

    def body(ug, uv, hg, hv, wg, wv, bg, bv, zg_ref, zv_ref, a_ref):
        first = pl.program_id(1) == 0

        def conv(u_ref, h_ref, w_ref, b_ref):
            x = u_ref[...]
            halo = jnp.where(first, 0.0, h_ref[...])
            z = b_ref[...] + w_ref[0:1, :] * _shift_down(x, halo, 2)
            z = z + w_ref[1:2, :] * _shift_down(x, halo, 1)
            return z + w_ref[2:3, :] * x

        zg = conv(ug, hg, wg, bg)
        zv = conv(uv, hv, wv, bv)
        zg_ref[...] = zg
        zv_ref[...] = zv
        a_ref[...] = (zg * jax.nn.sigmoid(zg) * zv).astype(BF16)

    def col(off):
        return pl.BlockSpec((ts, CONV_TC), lambda c, i: (i, c + off))

    def halo(off):
        return pl.BlockSpec((SUBLANES, CONV_TC), lambda c, i: (jnp.maximum(i * per8 - 1, 0), c + off))

    def wspec(rows, off):
        return pl.BlockSpec((rows, CONV_TC), lambda c, i: (0, c + off))

    zg, zv, a = pl.pallas_call(
        body, name=name, grid=(nct, s // ts),
        in_specs=[col(0), col(nct), halo(0), halo(nct), wspec(3, 0), wspec(3, nct), wspec(1, 0), wspec(1, nct)],
        out_specs=[col(0), col(0), col(0)],
        out_shape=[jax.ShapeDtypeStruct((s, D_FF), F32)] * 2 + [jax.ShapeDtypeStruct((s, D_FF), BF16)],
        compiler_params=_cparams(("parallel", "parallel")),
    )(u, u, u, u, conv_w, conv_w, conv_b, conv_b)
    return zg, zv, a


def _silu_bwd(da, zg, zv, *, name):
    s = da.shape[0]
    ts = _rows(s, 256)
    nct = D_FF // CONV_TC

    def body(da_ref, zg_ref, zv_ref, dz_g, dz_v):
        g = zg_ref[...]
        sg = jax.nn.sigmoid(g)
        dav = da_ref[...]
        dz_g[...] = dav * zv_ref[...] * sg * (1.0 + g * (1.0 - sg))
        dz_v[...] = dav * g * sg

    col = pl.BlockSpec((ts, CONV_TC), lambda c, i: (i, c))
    return pl.pallas_call(
        body, name=name, grid=(nct, s // ts), in_specs=[col, col, col], out_specs=[col, col],
        out_shape=[jax.ShapeDtypeStruct((s, D_FF), F32)] * 2,
        compiler_params=_cparams(("parallel", "parallel")),
    )(da, zg, zv)


def _conv_bwd(dz, u, conv_w, *, name, off):
    s = dz.shape[0]
    ts = _rows(s, 256)
    nct = D_FF // CONV_TC
    per8 = ts // SUBLANES
    nrow = s // ts
    last8 = s // SUBLANES - 1

    def body(dz_ref, nx_ref, u_ref, pv_ref, w_ref, du_ref, acc_ref):
        i = pl.program_id(1)
        dzv = dz_ref[...]
        nxt = jnp.where(i == nrow - 1, 0.0, nx_ref[...])
        du = w_ref[2:3, :] * dzv + w_ref[1:2, :] * _shift_up(dzv, nxt, 1) + w_ref[0:1, :] * _shift_up(dzv, nxt, 2)
        du_ref[...] = du.astype(BF16)

        @pl.when(i == 0)
        def _():
            acc_ref[...] = jnp.zeros_like(acc_ref)

        x = u_ref[...]
        prev = jnp.where(i == 0, 0.0, pv_ref[...])
        acc_ref[0:1, :] += jnp.sum(dzv * _shift_down(x, prev, 2), axis=0, keepdims=True)
        acc_ref[1:2, :] += jnp.sum(dzv * _shift_down(x, prev, 1), axis=0, keepdims=True)
        acc_ref[2:3, :] += jnp.sum(dzv * x, axis=0, keepdims=True)
        acc_ref[3:4, :] += jnp.sum(dzv, axis=0, keepdims=True)

    return pl.pallas_call(
        body, name=name, grid=(nct, nrow),
        in_specs=[pl.BlockSpec((ts, CONV_TC), lambda c, i: (i, c)),
                  pl.BlockSpec((SUBLANES, CONV_TC), lambda c, i: (jnp.minimum((i + 1) * per8, last8), c)),
                  pl.BlockSpec((ts, CONV_TC), lambda c, i: (i, c + off)),
                  pl.BlockSpec((SUBLANES, CONV_TC), lambda c, i: (jnp.maximum(i * per8 - 1, 0), c + off)),
                  pl.BlockSpec((3, CONV_TC), lambda c, i: (0, c + off))],
        out_specs=[pl.BlockSpec((ts, CONV_TC), lambda c, i: (i, c)),
                   pl.BlockSpec((SUBLANES, CONV_TC), lambda c, i: (0, c))],
        out_shape=[jax.ShapeDtypeStruct((s, D_FF), BF16), jax.ShapeDtypeStruct((SUBLANES, D_FF), F32)],
        compiler_params=_cparams(("parallel", "arbitrary")),
    )(dz, dz, u, u, conv_w)


def _peer(k):
    x, y, c = lax.axis_index("x"), lax.axis_index("y"), lax.axis_index("c")
    px = 1 - x if k & 4 else x
    py = 1 - y if k & 2 else y
    pc = 1 - c if k & 1 else c
    return (px, py, pc), 4 * px + 2 * py + pc


def _exchange(bufs, *, name, gather):
    n = len(bufs)
    npeer = N_DEV - 1

    def body(*refs):
        srcs, outs = refs[:n], refs[n:2 * n]
        send_sems, recv_sems, local_sems = refs[2 * n:]
        _, me = _peer(0)
        mine = [src if gather else src.at[me] for src in srcs]
        local = [pltpu.make_async_copy(mine[p], outs[p].at[me], local_sems.at[p]) for p in range(n)]
        for cp in local:
            cp.start()
        sends = []
        for k in range(1, N_DEV):
            dev, idx = _peer(k)
            for p in range(n):
                cp = pltpu.make_async_remote_copy(
                    src_ref=srcs[p] if gather else srcs[p].at[idx], dst_ref=outs[p].at[me],
                    send_sem=send_sems.at[p * npeer + k - 1], recv_sem=recv_sems.at[p * npeer + k - 1],
                    device_id=dev, device_id_type=pl.DeviceIdType.MESH)
                cp.start()
                sends.append(cp)
        for k in range(1, N_DEV):
            dev, idx = _peer(k)
            for p in range(n):
                pltpu.make_async_remote_copy(
                    src_ref=mine[p], dst_ref=outs[p].at[idx],
                    send_sem=send_sems.at[p * npeer + k - 1], recv_sem=recv_sems.at[p * npeer + k - 1],
                    device_id=dev, device_id_type=pl.DeviceIdType.MESH).wait_recv()
        for cp in sends:
            cp.wait_send()
        for cp in local:
            cp.wait()

    any_spec = pl.BlockSpec(memory_space=pl.ANY)
    return pl.pallas_call(
        body, name=name,
        in_specs=[any_spec] * n, out_specs=[any_spec] * n,
        out_shape=[jax.ShapeDtypeStruct((N_DEV,) + b.shape[-2:], b.dtype) for b in bufs],
        scratch_shapes=[pltpu.SemaphoreType.DMA((n * npeer,)), pltpu.SemaphoreType.DMA((n * npeer,)),
                        pltpu.SemaphoreType.DMA((n,))],
    )(*bufs)


def _adamw(parts, w, m, v, *, name):
    rows, width = w.shape
    tr = _rows(rows, max(SUBLANES, (128 * 1024) // width))

    def body(p_ref, w_ref, m_ref, v_ref, g_ref, d_ref, nm_ref, nv_ref):
        g = p_ref[0]
        for k in range(1, N_DEV):
            g = g + p_ref[k]
        mn = ADAM_B1 * m_ref[...] + (1.0 - ADAM_B1) * g
        vn = ADAM_B2 * v_ref[...] + (1.0 - ADAM_B2) * jnp.square(g)
        m_hat = mn / (1.0 - ADAM_B1 ** ADAM_STEP)
        v_hat = vn / (1.0 - ADAM_B2 ** ADAM_STEP)
        g_ref[...] = g
        d_ref[...] = -ADAM_LR * (m_hat / (jnp.sqrt(v_hat) + ADAM_EPS) + ADAM_WD * w_ref[...])
        nm_ref[...] = mn
        nv_ref[...] = vn

    row = pl.BlockSpec((tr, width), lambda i: (i, 0))
    return pl.pallas_call(
        body, name=name, grid=(rows // tr,),
        in_specs=[pl.BlockSpec((N_DEV, tr, width), lambda i: (0, i, 0)), row, row, row],
        out_specs=[row] * 4, out_shape=[jax.ShapeDtypeStruct((rows, width), F32)] * 4,
        compiler_params=_cparams(("parallel",)),
    )(parts, w, m, v)


def _pack_replicated(blocks):
    flat = jnp.concatenate([blocks[name].reshape(-1).astype(F32) for name, _, _ in REPLICATED])
    flat = jnp.pad(flat, (0, PACK_ROWS * PACK_W - PACK_USED))
    return flat.reshape(PACK_ROWS, PACK_W)


def _unpack_replicated(buf):
    flat = buf.reshape(-1)
    return {name: flat[off:off + n].reshape(1, n) for name, (off, n) in PACK_TABLE.items()}


def _join_shards(seg, shape, axis):
    return seg.reshape(shape) if axis == 0 else seg.transpose(1, 0, 2).reshape(shape)


def _split_shards(g, shape, axis):
    r, c = shape
    if axis == 0:
        return g.reshape(N_DEV, r // N_DEV, c)
    return g.reshape(r, N_DEV, c // N_DEV).transpose(1, 0, 2)


def _to_residues(a, d):
    s, c = a.shape
    return a.reshape(s // d, d, c).transpose(1, 0, 2).reshape(s, c)


def _from_residues(a, d):
    s, c = a.shape
    return a.reshape(d, s // d, c).transpose(1, 0, 2).reshape(s, c)


def _pad_heads(w, heads, width, lo, hi):
    r = w.shape[0]
    w = w.reshape(r, heads, width)[:, :, lo:hi]
    w = jnp.pad(w, ((0, 0), (0, 0), (0, LANES - (hi - lo))))
    return w.reshape(r, heads * LANES)


def _local_step(x, mem, positions, target, w):
    s = x.shape[0]
    bf = lambda a: a.astype(BF16)

    w_in = w["w_in"]
    kr_cols = jnp.pad(w_in[:, OFF_KV:OFF_KR], ((0, 0), (MLA_NOPE, LANES - MLA_QK_DIM)))
    w_a = bf(jnp.concatenate([w_in[:, :OFF_KV], kr_cols], axis=1))
    w_dm = bf(w_in[:, OFF_KR:OFF_MEMQ])
    w_g = bf(w_in[:, OFF_MEMQ:])
    w_in_t = jnp.concatenate([w_a, w_dm, w_g], axis=1).T
    wq = bf(_pad_heads(w["w_uq"], MLA_HEADS, MLA_QK_DIM, 0, MLA_QK_DIM))
    wk = bf(_pad_heads(w["w_ukv"], MLA_HEADS, MLA_NOPE + MLA_V, 0, MLA_NOPE))
    wv = bf(_pad_heads(w["w_ukv"], MLA_HEADS, MLA_NOPE + MLA_V, MLA_NOPE, MLA_NOPE + MLA_V))
    w_mkv = bf(w["w_mem_kv"])
    wb_mla = bf(jnp.pad(w["w_br_mla"].reshape(MLA_HEADS, MLA_V, D_MODEL),
                        ((0, 0), (0, LANES - MLA_V), (0, 0))).reshape(MLA_HEADS * LANES, D_MODEL))
    wb_dil, wb_mem, w_o = bf(w["w_br_dil"]), bf(w["w_br_mem"]), bf(w["w_o"])
    w_up, w_down = bf(w["w_ffn_up"]), bf(w["w_ffn_down"])
    tabs = _rope_tables(positions)
    slopes = [jnp.asarray(sl, F32) for sl in DIL_SLOPES]
    mla_scale = MLA_QK_DIM ** -0.5
    mem_scale = LANES ** -0.5
    MQ = 3 * DIL_HEADS

    h = _rms_fwd(x, w["g_pre_mix"], name="rms_pre_mix", out_dtype=BF16)
    p_a = _matmul(h, w_a, name="proj_a")
    p_dm = _matmul(h, w_dm, name="proj_dm", out_dtype=BF16)
    p_g = _matmul(h, w_g, name="proj_gate")
    c_q, c_kv, kr = p_a[:, :OFF_Q], p_a[:, OFF_Q:OFF_KV], p_a[:, OFF_KV:]

    qn = _rms_fwd(c_q, w["mla_q_norm"], name="rms_q", out_dtype=BF16)
    kvn = _rms_fwd(c_kv, w["mla_kv_norm"], name="rms_kv", out_dtype=BF16)
    q_raw = _matmul(qn, wq, name="mla_q_up")
    k_raw = _matmul(kvn, wk, name="mla_k_up")
    v_f = _matmul(kvn, wv, name="mla_v_up", out_dtype=BF16)
    q_f = _rope_fwd(q_raw, tabs, name="rope_q", scale=mla_scale)
    k_f = _rope_fwd(k_raw, tabs, name="rope_k", scale=1.0, add=kr)
    mla = dict(heads=MLA_HEADS, qoff=0, koff=0, voff=0, causal=True, scale=1.0, tq=512, tk=512)
    o_mla, lse_mla = _flash_fwd(q_f, k_f, v_f, name="mla_fwd", **mla)

    dil_in, dil_o, dil_lse = [], [], []
    for g, (_, d) in enumerate(DIL_PAIRS):
        if d == 1:
            arrs, offs = (p_dm, p_dm, p_dm), (4 * g, DIL_HEADS + 4 * g, 2 * DIL_HEADS + 4 * g)
        else:
            arrs = tuple(_to_residues(p_dm[:, (t * DIL_HEADS + 4 * g) * LANES:(t * DIL_HEADS + 4 * g + 4) * LANES], d)
                         for t in range(3))
            offs = (0, 0, 0)
        o_g, lse_g = _band_fwd(*arrs, slopes[g], name=f"dil_fwd_{g}", dilation=d,
                               qoff=offs[0], koff=offs[1], voff=offs[2])
        dil_in.append((arrs, offs))
        dil_o.append(_from_residues(o_g, d))
        dil_lse.append(_from_residues(lse_g, d))
    y_dil = _mix_fwd(dil_o, dil_lse, name="dil_mix")

    memn = _rms_fwd(mem, w["g_mem"], name="rms_mem", out_dtype=BF16)
    kv_m = _matmul(memn, w_mkv, name="mem_kv", out_dtype=BF16)
    memat = dict(heads=MEM_HEADS, qoff=MQ, koff=0, voff=MEM_HEADS, causal=False, scale=mem_scale, tq=512, tk=256)
    o_mem, lse_mem = _flash_fwd(p_dm, kv_m, kv_m, name="mem_fwd", **memat)

    b_mla = _matmul(o_mla, wb_mla, name="br_mla")
    b_dil = _matmul(y_dil, wb_dil, name="br_dil")
    b_mem = _matmul(o_mem, wb_mem, name="br_mem")
    merged = _gate_fwd(p_g, w["b_gate"], (b_mla, b_dil, b_mem), name="gate_fwd")
    z1 = _matmul(merged, w_o, name="out_proj")
    x1 = _rms_fwd(z1, w["g_post_mix"], name="rms_post_mix", out_dtype=F32, add=x)
    h2 = _rms_fwd(x1, w["g_pre_ffn"], name="rms_pre_ffn", out_dtype=BF16)
    u = _matmul(h2, w_up, name="ffn_up")
    zg, zv, act = _conv_fwd(u, w["conv_w"], w["conv_b"], name="conv_fwd")
    f = _matmul(act, w_down, name="ffn_down")
    dy, sq = _loss_fwd(x1, f, target, w["g_post_ffn"], name="loss")
    loss = 0.5 * jnp.sum(sq) / D_MODEL

    grads = {}
    df, grads["g_post_ffn"] = _rms_bwd(f, w["g_post_ffn"], dy, name="rms_post_ffn_bwd", out_dtype=BF16)
    da = _matmul(df, w_down.T, name="ffn_down_dx")
    grads["w_ffn_down"] = _matmul(act, df, name="ffn_down_dw", trans_a=True)
    dzg, dzv = _silu_bwd(da, zg, zv, name="silu_bwd")
    nct = D_FF // CONV_TC
    du_g, cacc_g = _conv_bwd(dzg, u, w["conv_w"], name="conv_bwd_gate", off=0)
    du_v, cacc_v = _conv_bwd(dzv, u, w["conv_w"], name="conv_bwd_val", off=nct)
    grads["conv_w"] = jnp.concatenate([cacc_g[0:3], cacc_v[0:3]], axis=1)
    grads["conv_b"] = jnp.concatenate([cacc_g[3:4], cacc_v[3:4]], axis=1)
    w_up_t = w_up.T
    dh2 = _matmul(du_g, w_up_t[:D_FF], name="ffn_up_dx_gate")
    dh2 = _matmul(du_v, w_up_t[D_FF:], name="ffn_up_dx_val", add=dh2)
    grads["w_ffn_up"] = jnp.concatenate([_matmul(h2, du_g, name="ffn_up_dw_gate", trans_a=True),
                                         _matmul(h2, du_v, name="ffn_up_dw_val", trans_a=True)], axis=1)
    dx1, grads["g_pre_ffn"] = _rms_bwd(x1, w["g_pre_ffn"], dh2, name="rms_pre_ffn_bwd", out_dtype=F32, add=dy)
    dz1, grads["g_post_mix"] = _rms_bwd(z1, w["g_post_mix"], dx1, name="rms_post_mix_bwd", out_dtype=BF16)
    dmerged = _matmul(dz1, w_o.T, name="out_proj_dx")
    grads["w_o"] = _matmul(merged, dz1, name="out_proj_dw", trans_a=True)
    db_mla, db_dil, db_mem, dgp, grads["b_gate"] = _gate_bwd(
        dmerged, p_g, w["b_gate"], (b_mla, b_dil, b_mem), name="gate_bwd")

    do_mla = _matmul(db_mla, wb_mla.T, name="br_mla_dx", out_dtype=BF16)
    g_wb_mla = _matmul(o_mla, db_mla, name="br_mla_dw", trans_a=True)
    grads["w_br_mla"] = g_wb_mla.reshape(MLA_HEADS, LANES, D_MODEL)[:, :MLA_V].reshape(MLA_HEADS * MLA_V, D_MODEL)
    delta_mla = _row_dot(do_mla, o_mla, name="mla_delta")
    dq_f = _flash_bwd_dq(q_f, k_f, v_f, do_mla, lse_mla, delta_mla, name="mla_bwd_dq", out_dtype=F32, **mla)
    dk_f, dv_f = _flash_bwd_dkv(q_f, k_f, v_f, do_mla, lse_mla, delta_mla, name="mla_bwd_dkv",
                                dk_dtype=F32, dv_dtype=BF16, **mla)
    dq_raw = _rope_bwd(dq_f, tabs, name="rope_q_bwd", scale=mla_scale, with_add=False)
    dk_raw, dkr = _rope_bwd(dk_f, tabs, name="rope_k_bwd", scale=1.0, with_add=True)
    dqn = _matmul(dq_raw, wq.T, name="mla_q_up_dx")
    g_wq = _matmul(qn, dq_raw, name="mla_q_up_dw", trans_a=True)
    grads["w_uq"] = g_wq.reshape(MLA_Q_RANK, MLA_HEADS, LANES)[:, :, :MLA_QK_DIM].reshape(MLA_Q_RANK, -1)
    dkvn = _matmul(dk_raw, wk.T, name="mla_k_up_dx")
    dkvn = _matmul(dv_f, wv.T, name="mla_v_up_dx", add=dkvn)
    g_wk = _matmul(kvn, dk_raw, name="mla_k_up_dw", trans_a=True).reshape(MLA_KV_RANK, MLA_HEADS, LANES)
    g_wv = _matmul(kvn, dv_f, name="mla_v_up_dw", trans_a=True).reshape(MLA_KV_RANK, MLA_HEADS, LANES)
    grads["w_ukv"] = jnp.concatenate([g_wk[:, :, :MLA_NOPE], g_wv[:, :, :MLA_V]], axis=2).reshape(MLA_KV_RANK, -1)
    dc_q, grads["mla_q_norm"] = _rms_bwd(c_q, w["mla_q_norm"], dqn, name="rms_q_bwd", out_dtype=BF16)
    dc_kv, grads["mla_kv_norm"] = _rms_bwd(c_kv, w["mla_kv_norm"], dkvn, name="rms_kv_bwd", out_dtype=BF16)

    dy_dil = _matmul(db_dil, wb_dil.T, name="br_dil_dx")
    grads["w_br_dil"] = _matmul(y_dil, db_dil, name="br_dil_dw", trans_a=True)
    mix = _mix_bwd(dy_dil, dil_o, dil_lse, name="dil_mix_bwd")
    d_dil = [[None] * 3 for _ in range(3)]
    for g, (_, d) in enumerate(DIL_PAIRS):
        arrs, offs = dil_in[g]
        do_g, dl_g, lse_g = mix[g], mix[3 + g], dil_lse[g]
        if d != 1:
            do_g, dl_g, lse_g = _to_residues(do_g, d), _to_residues(dl_g, d), _to_residues(lse_g, d)
        dq_g, dk_g, dv_g = _band_bwd(*arrs, do_g, lse_g, dl_g, slopes[g], name=f"dil_bwd_{g}", dilation=d,
                                     qoff=offs[0], koff=offs[1], voff=offs[2])
        for t, a in enumerate((dq_g, dk_g, dv_g)):
            d_dil[t][g] = a if d == 1 else _from_residues(a, d)

    do_mem = _matmul(db_mem, wb_mem.T, name="br_mem_dx", out_dtype=BF16)
    grads["w_br_mem"] = _matmul(o_mem, db_mem, name="br_mem_dw", trans_a=True)
    delta_mem = _row_dot(do_mem, o_mem, name="mem_delta")
    dq_mem = _flash_bwd_dq(p_dm, kv_m, kv_m, do_mem, lse_mem, delta_mem, name="mem_bwd_dq", out_dtype=BF16, **memat)
    dk_mem, dv_mem = _flash_bwd_dkv(p_dm, kv_m, kv_m, do_mem, lse_mem, delta_mem, name="mem_bwd_dkv",
                                    dk_dtype=BF16, dv_dtype=BF16, **memat)
    dkv_m = jnp.concatenate([dk_mem, dv_mem], axis=1)
    dmemn = _matmul(dkv_m, w_mkv.T, name="mem_kv_dx")
    grads["w_mem_kv"] = _matmul(memn, dkv_m, name="mem_kv_dw", trans_a=True)
    _, grads["g_mem"] = _rms_bwd(mem, w["g_mem"], dmemn, name="rms_mem_bwd", out_dtype=BF16)

    dp_all = jnp.concatenate([dc_q, dc_kv, bf(dkr)] + d_dil[0] + d_dil[1] + d_dil[2] + [dq_mem, dgp], axis=1)
    dh = _matmul(dp_all, w_in_t, name="proj_dx")
    g_in = _matmul(h, dp_all, name="proj_dw", trans_a=True)
    grads["w_in"] = jnp.concatenate(
        [g_in[:, :OFF_KV], g_in[:, OFF_KV + MLA_NOPE:OFF_KV + MLA_QK_DIM], g_in[:, N_A:]], axis=1)
    dx, grads["g_pre_mix"] = _rms_bwd(x, w["g_pre_mix"], dh, name="rms_pre_mix_bwd", out_dtype=F32, add=dx1)
    return loss, dx, grads


def kernel(x, mem, positions, g_pre_mix, w_in, b_gate, mla_q_norm, w_uq, mla_kv_norm, w_ukv, g_mem, w_mem_kv, w_br_mla, w_br_dil, w_br_mem, w_o, g_post_mix, g_pre_ffn, w_ffn_up, conv_w, conv_b, w_ffn_down, g_post_ffn, loss_target, m_g_pre_mix, m_w_in, m_b_gate, m_mla_q_norm, m_w_uq, m_mla_kv_norm, m_w_ukv, m_g_mem, m_w_mem_kv, m_w_br_mla, m_w_br_dil, m_w_br_mem, m_w_o, m_g_post_mix, m_g_pre_ffn, m_w_ffn_up, m_conv_w, m_conv_b, m_w_ffn_down, m_g_post_ffn, v_g_pre_mix, v_w_in, v_b_gate, v_mla_q_norm, v_w_uq, v_mla_kv_norm, v_w_ukv, v_g_mem, v_w_mem_kv, v_w_br_mla, v_w_br_dil, v_w_br_mem, v_w_o, v_g_post_mix, v_g_pre_ffn, v_w_ffn_up, v_conv_w, v_conv_b, v_w_ffn_down, v_g_post_ffn):
    local = dict(g_pre_mix=g_pre_mix, w_in=w_in, b_gate=b_gate, mla_q_norm=mla_q_norm, w_uq=w_uq,
                 mla_kv_norm=mla_kv_norm, w_ukv=w_ukv, g_mem=g_mem, w_mem_kv=w_mem_kv, w_br_mla=w_br_mla,
                 w_br_dil=w_br_dil, w_br_mem=w_br_mem, w_o=w_o, g_post_mix=g_post_mix, g_pre_ffn=g_pre_ffn,
                 w_ffn_up=w_ffn_up, conv_w=conv_w, conv_b=conv_b, w_ffn_down=w_ffn_down, g_post_ffn=g_post_ffn)
    mom_m = dict(g_pre_mix=m_g_pre_mix, w_in=m_w_in, b_gate=m_b_gate, mla_q_norm=m_mla_q_norm, w_uq=m_w_uq,
                 mla_kv_norm=m_mla_kv_norm, w_ukv=m_w_ukv, g_mem=m_g_mem, w_mem_kv=m_w_mem_kv, w_br_mla=m_w_br_mla,
                 w_br_dil=m_w_br_dil, w_br_mem=m_w_br_mem, w_o=m_w_o, g_post_mix=m_g_post_mix,
                 g_pre_ffn=m_g_pre_ffn, w_ffn_up=m_w_ffn_up, conv_w=m_conv_w, conv_b=m_conv_b,
                 w_ffn_down=m_w_ffn_down, g_post_ffn=m_g_post_ffn)
    mom_v = dict(g_pre_mix=v_g_pre_mix, w_in=v_w_in, b_gate=v_b_gate, mla_q_norm=v_mla_q_norm, w_uq=v_w_uq,
                 mla_kv_norm=v_mla_kv_norm, w_ukv=v_w_ukv, g_mem=v_g_mem, w_mem_kv=v_w_mem_kv, w_br_mla=v_w_br_mla,
                 w_br_dil=v_w_br_dil, w_br_mem=v_w_br_mem, w_o=v_w_o, g_post_mix=v_g_post_mix,
                 g_pre_ffn=v_g_pre_ffn, w_ffn_up=v_w_ffn_up, conv_w=v_conv_w, conv_b=v_conv_b,
                 w_ffn_down=v_w_ffn_down, g_post_ffn=v_g_post_ffn)

    shards = [local[name][0].astype(F32 if name == "conv_w" else BF16) for name, _, _ in SHARDED]
    gathered = _exchange(shards, name="gather_weights", gather=True)
    full = {name: _join_shards(seg, shape, axis) for (name, shape, axis), seg in zip(SHARDED, gathered)}
    for name, _, _ in REPLICATED:
        full[name] = local[name].reshape(1, -1)

    loss, dx, grads = _local_step(x[0], mem[0], positions[0], loss_target[0], full)

    slabs = [_split_shards(grads[name].astype(F32), shape, axis) for name, shape, axis in SHARDED]
    parts = _exchange(slabs, name="exchange_grads", gather=False)
    rep_parts = _exchange([_pack_replicated(grads)], name="gather_replicated_grads", gather=True)[0]

    results = {}
    for (name, _, _), part in zip(SHARDED, parts):
        res = _adamw(part, local[name][0], mom_m[name][0], mom_v[name][0], name="adamw_" + name)
        results[name] = [r[None] for r in res]
    rep = _adamw(rep_parts, _pack_replicated(local), _pack_replicated(mom_m), _pack_replicated(mom_v),
                 name="adamw_replicated")
    for i, buf in enumerate(rep):
        for name, val in _unpack_replicated(buf).items():
            results.setdefault(name, [None] * 4)[i] = val

    loss = lax.psum(loss, ("x", "y", "c"))
    outs = [loss, dx[None]]
    for i in range(4):
        outs.extend(results[name][i] for name in PARAM_NAMES)
    return tuple(outs)
```

```python
import functools

import numpy as np
import jax
import jax.numpy as jnp
from jax import lax
from jax.experimental import pallas as pl
from jax.experimental.pallas import tpu as pltpu

F32 = jnp.float32
BF16 = jnp.bfloat16

N_DEV = 8
D_MODEL = 1024
RMS_EPS = 1e-6
NEG_INF = -1e30
LANES = 128
SUBLANES = 8
BLOCK = 128

MLA_HEADS = 8
MLA_NOPE = 64
MLA_ROPE = 32
MLA_V = 64
MLA_QK_DIM = MLA_NOPE + MLA_ROPE
MLA_Q_RANK = 384
MLA_KV_RANK = 256
ROPE_THETA = 10000.0
DIL_PAIRS = ((128, 1), (512, 4), (2048, 16))
DIL_HEADS_PER_GROUP = 4
DIL_HEADS = 12
MEM_HEADS = 4
D_FF = 2816
OFF_Q = MLA_Q_RANK
OFF_KV = OFF_Q + MLA_KV_RANK
OFF_KR = OFF_KV + MLA_ROPE
OFF_DIL = OFF_KR + 3 * DIL_HEADS * LANES
OFF_MEMQ = OFF_DIL + MEM_HEADS * LANES
D_IN = OFF_MEMQ + 3 * D_MODEL
N_A = OFF_KV + LANES
N_DM = 3 * DIL_HEADS * LANES + MEM_HEADS * LANES

ADAM_LR = 0.001
ADAM_B1 = 0.9
ADAM_B2 = 0.999
ADAM_EPS = 1e-08
ADAM_WD = 0.01
ADAM_STEP = 10

VMEM_LIMIT = 48 * 1024 * 1024
VMEM_LIMIT_BIG = 58 * 1024 * 1024
PACK_W = 1024

_ALIBI_BASE = np.exp2(-8.0 * np.arange(1, DIL_HEADS + 1) / DIL_HEADS)
DIL_SLOPES = [[float(_ALIBI_BASE[hh * 3 + g]) for hh in range(DIL_HEADS_PER_GROUP)] for g in range(3)]

PARAMS = (
    ("g_pre_mix", (1024,), None), ("w_in", (1024, D_IN), 1), ("b_gate", (3072,), None),
    ("mla_q_norm", (384,), None), ("w_uq", (384, 768), 1), ("mla_kv_norm", (256,), None),
    ("w_ukv", (256, 1024), 1), ("g_mem", (1024,), None), ("w_mem_kv", (1024, 1024), 0),
    ("w_br_mla", (512, 1024), 1), ("w_br_dil", (512, 1024), 1), ("w_br_mem", (512, 1024), 1),
    ("w_o", (1024, 1024), 0), ("g_post_mix", (1024,), None), ("g_pre_ffn", (1024,), None),
    ("w_ffn_up", (1024, 2 * D_FF), 1), ("conv_w", (3, 2 * D_FF), 1), ("conv_b", (2 * D_FF,), None),
    ("w_ffn_down", (D_FF, 1024), 0), ("g_post_ffn", (1024,), None),
)
PARAM_NAMES = tuple(p[0] for p in PARAMS)


def _shard_shape(shape, axis):
    if axis is None:
        return shape
    return tuple(s // N_DEV if a == axis else s for a, s in enumerate(shape))


SHARDED = tuple(p for p in PARAMS if p[2] is not None)
REPLICATED = tuple(p for p in PARAMS if p[2] is None)


def _layout():
    off, table = 0, {}
    for name, shape, _ in REPLICATED:
        table[name] = (off, shape[0])
        off += shape[0]
    rows = -(-off // PACK_W)
    rows = -(-rows // SUBLANES) * SUBLANES
    return table, off, rows


PACK_TABLE, PACK_USED, PACK_ROWS = _layout()


def _pick(n, cap):
    best = None
    for t in range(LANES, min(n, cap) + 1, LANES):
        if n % t == 0:
            best = t
    return best if best is not None else n


def _rows(n, cap, mult=SUBLANES):
    best = None
    for t in range(mult, min(n, cap) + 1, mult):
        if n % t == 0:
            best = t
    return best if best is not None else n


def _cparams(sem, vmem=VMEM_LIMIT):
    return pltpu.CompilerParams(dimension_semantics=sem, vmem_limit_bytes=vmem)


def _matmul(a, b, *, name, out_dtype=F32, trans_a=False, add=None, tm=1024, tn=1408, tk=640):
    if trans_a:
        kc, m = a.shape
    else:
        m, kc = a.shape
    n = b.shape[1]
    assert b.shape[0] == kc
    tm, tn, tk = _pick(m, tm), _pick(n, tn), _pick(kc, tk)
    nk = kc // tk

    def body(*refs):
        if add is None:
            a_ref, b_ref, o_ref, acc = refs
        else:
            a_ref, b_ref, c_ref, o_ref, acc = refs
        k = pl.program_id(2)

        @pl.when(k == 0)
        def _():
            if add is None:
                acc[...] = jnp.zeros_like(acc)
            else:
                acc[...] = c_ref[...].astype(F32)

        av = a_ref[...].astype(BF16)
        bv = b_ref[...].astype(BF16)
        if trans_a:
            acc[...] += lax.dot_general(av, bv, (((0,), (0,)), ((), ())), preferred_element_type=F32)
        else:
            acc[...] += jnp.dot(av, bv, preferred_element_type=F32)

        @pl.when(k == nk - 1)
        def _():
            o_ref[...] = acc[...].astype(out_dtype)

    if trans_a:
        a_spec = pl.BlockSpec((tk, tm), lambda i, j, k: (k, i))
    else:
        a_spec = pl.BlockSpec((tm, tk), lambda i, j, k: (i, k))
    in_specs = [a_spec, pl.BlockSpec((tk, tn), lambda i, j, k: (k, j))]
    args = [a, b]
    if add is not None:
        in_specs.append(pl.BlockSpec((tm, tn), lambda i, j, k: (i, j)))
        args.append(add)
    return pl.pallas_call(
        body, name=name, grid=(m // tm, n // tn, nk),
        in_specs=in_specs, out_specs=pl.BlockSpec((tm, tn), lambda i, j, k: (i, j)),
        out_shape=jax.ShapeDtypeStruct((m, n), out_dtype),
        scratch_shapes=[pltpu.VMEM((tm, tn), F32)],
        compiler_params=_cparams(("parallel", "parallel", "arbitrary")),
    )(*args)


def _rms_fwd(x, g, *, name, out_dtype, add=None):
    s, n = x.shape
    ts = _rows(s, 512)

    def body(*refs):
        if add is None:
            x_ref, g_ref, o_ref = refs
        else:
            x_ref, g_ref, a_ref, o_ref = refs
        xv = x_ref[...]
        r = lax.rsqrt(jnp.mean(xv * xv, axis=-1, keepdims=True) + RMS_EPS)
        y = xv * r * g_ref[...]
        if add is not None:
            y = a_ref[...] + y
        o_ref[...] = y.astype(out_dtype)

    row = pl.BlockSpec((ts, n), lambda i: (i, 0))
    in_specs = [row, pl.BlockSpec((1, n), lambda i: (0, 0))]
    args = [x, g]
    if add is not None:
        in_specs.append(row)
        args.append(add)
    return pl.pallas_call(
        body, name=name, grid=(s // ts,), in_specs=in_specs, out_specs=row,
        out_shape=jax.ShapeDtypeStruct((s, n), out_dtype),
        compiler_params=_cparams(("parallel",)),
    )(*args)


def _rms_bwd(x, g, dy, *, name, out_dtype, add=None):
    s, n = x.shape
    ts = _rows(s, 512)

    def body(*refs):
        if add is None:
            x_ref, g_ref, dy_ref, dx_ref, dg_ref = refs
        else:
            x_ref, g_ref, dy_ref, a_ref, dx_ref, dg_ref = refs
        i = pl.program_id(0)
        xv = x_ref[...]
        dyv = dy_ref[...].astype(F32)
        r = lax.rsqrt(jnp.mean(xv * xv, axis=-1, keepdims=True) + RMS_EPS)
        nx = xv * r
        gdy = dyv * g_ref[...]
        dx = r * (gdy - nx * jnp.mean(nx * gdy, axis=-1, keepdims=True))
        if add is not None:
            dx = a_ref[...] + dx
        dx_ref[...] = dx.astype(out_dtype)

        @pl.when(i == 0)
        def _():
            dg_ref[...] = jnp.zeros_like(dg_ref)

        dg_ref[...] += jnp.sum(dyv * nx, axis=0, keepdims=True)

    row = pl.BlockSpec((ts, n), lambda i: (i, 0))
    vec = pl.BlockSpec((1, n), lambda i: (0, 0))
    in_specs = [row, vec, row]
    args = [x, g, dy]
    if add is not None:
        in_specs.append(row)
        args.append(add)
    return pl.pallas_call(
        body, name=name, grid=(s // ts,), in_specs=in_specs, out_specs=[row, vec],
        out_shape=[jax.ShapeDtypeStruct((s, n), out_dtype), jax.ShapeDtypeStruct((1, n), F32)],
        compiler_params=_cparams(("arbitrary",)),
    )(*args)


def _loss_fwd(x1, f, target, g, *, name):
    s, n = x1.shape
    ts = _rows(s, 512)

    def body(x_ref, f_ref, t_ref, g_ref, dy_ref, sq_ref):
        i = pl.program_id(0)
        fv = f_ref[...]
        r = lax.rsqrt(jnp.mean(fv * fv, axis=-1, keepdims=True) + RMS_EPS)
        err = x_ref[...] + fv * r * g_ref[...] - t_ref[...]
        dy_ref[...] = err * (1.0 / n)

        @pl.when(i == 0)
        def _():
            sq_ref[...] = jnp.zeros_like(sq_ref)

        sq_ref[...] += jnp.sum(err * err, axis=0, keepdims=True)

    row = pl.BlockSpec((ts, n), lambda i: (i, 0))
    vec = pl.BlockSpec((1, n), lambda i: (0, 0))
    return pl.pallas_call(
        body, name=name, grid=(s // ts,), in_specs=[row, row, row, vec], out_specs=[row, vec],
        out_shape=[jax.ShapeDtypeStruct((s, n), F32), jax.ShapeDtypeStruct((1, n), F32)],
        compiler_params=_cparams(("arbitrary",)),
    )(x1, f, target, g)


def _rope_tables(positions):
    half = MLA_ROPE // 2
    inv_freq = ROPE_THETA ** (-jnp.arange(half, dtype=F32) / half)
    ang = positions.astype(F32)[:, None] * inv_freq[None, :]
    cos, sin = jnp.cos(ang), jnp.sin(ang)
    s = positions.shape[0]
    one = jnp.ones((s, MLA_NOPE), F32)
    zero = jnp.zeros((s, MLA_NOPE), F32)
    pad1 = jnp.ones((s, LANES - MLA_QK_DIM), F32)
    pad0 = jnp.zeros((s, LANES - MLA_QK_DIM), F32)
    zh = jnp.zeros((s, half), F32)
    c_tab = jnp.concatenate([one, cos, cos, pad1], axis=1)
    s1_tab = jnp.concatenate([zero, -sin, zh, pad0], axis=1)
    s2_tab = jnp.concatenate([zero, zh, sin, pad0], axis=1)
    return c_tab, s1_tab, s2_tab


def _rope_fwd(x, tabs, *, name, scale, add=None):
    s, n = x.shape
    nh = n // LANES
    ts = _rows(s, 512)
    half = MLA_ROPE // 2

    def body(*refs):
        if add is None:
            x_ref, c_ref, s1_ref, s2_ref, o_ref = refs
        else:
            x_ref, a_ref, c_ref, s1_ref, s2_ref, o_ref = refs
        c, s1, s2 = c_ref[...], s1_ref[...], s2_ref[...]
        for h in range(nh):
            xh = x_ref[:, h * LANES:(h + 1) * LANES]
            if add is not None:
                xh = xh + a_ref[...]
            y = xh * c + pltpu.roll(xh, LANES - half, 1) * s1 + pltpu.roll(xh, half, 1) * s2
            o_ref[:, h * LANES:(h + 1) * LANES] = (y * scale).astype(BF16)

    row = pl.BlockSpec((ts, n), lambda i: (i, 0))
    tab = pl.BlockSpec((ts, LANES), lambda i: (i, 0))
    in_specs = [row] + ([tab] if add is not None else []) + [tab, tab, tab]
    args = [x] + ([add] if add is not None else []) + list(tabs)
    return pl.pallas_call(
        body, name=name, grid=(s // ts,), in_specs=in_specs, out_specs=row,
        out_shape=jax.ShapeDtypeStruct((s, n), BF16),
        compiler_params=_cparams(("parallel",)),
    )(*args)


def _rope_bwd(dy, tabs, *, name, scale, with_add):
    s, n = dy.shape
    nh = n // LANES
    ts = _rows(s, 512)
    half = MLA_ROPE // 2

    def body(*refs):
        if with_add:
            dy_ref, c_ref, s1_ref, s2_ref, dx_ref, da_ref = refs
        else:
            dy_ref, c_ref, s1_ref, s2_ref, dx_ref = refs
        c, s1, s2 = c_ref[...], s1_ref[...], s2_ref[...]
        tot = None
        for h in range(nh):
            g = dy_ref[:, h * LANES:(h + 1) * LANES].astype(F32)
            dx = (g * c + pltpu.roll(g * s1, half, 1) + pltpu.roll(g * s2, LANES - half, 1)) * scale
            dx_ref[:, h * LANES:(h + 1) * LANES] = dx.astype(BF16)
            tot = dx if tot is None else tot + dx
        if with_add:
            da_ref[...] = tot

    row = pl.BlockSpec((ts, n), lambda i: (i, 0))
    tab = pl.BlockSpec((ts, LANES), lambda i: (i, 0))
    out_specs = [row, tab] if with_add else row
    out_shape = [jax.ShapeDtypeStruct((s, n), BF16)]
    if with_add:
        out_shape.append(jax.ShapeDtypeStruct((s, LANES), F32))
    else:
        out_shape = out_shape[0]
    return pl.pallas_call(
        body, name=name, grid=(s // ts,), in_specs=[row, tab, tab, tab], out_specs=out_specs,
        out_shape=out_shape, compiler_params=_cparams(("parallel",)),
    )(dy, *tabs)


def _scores(q, k, scale, diag):
    s = lax.dot_general(q, k, (((1,), (1,)), ((), ())), preferred_element_type=F32)
    if scale != 1.0:
        s = s * scale
    if diag:
        rows = lax.broadcasted_iota(jnp.int32, s.shape, 0)
        cols = lax.broadcasted_iota(jnp.int32, s.shape, 1)
        s = jnp.where(cols <= rows, s, NEG_INF)
    return s


def _flash_fwd(q, k, v, *, name, heads, qoff, koff, voff, causal, scale, tq, tk):
    s_q, s_kv = q.shape[0], k.shape[0]
    tq, tk = min(tq, s_q), min(tk, s_kv)
    nq, nk = s_q // tq, s_kv // tk
    if causal:
        assert tq == tk and s_q == s_kv

    def body(q_ref, k_ref, v_ref, o_ref, lse_ref, m_s, l_s, acc):
        i, j = pl.program_id(1), pl.program_id(2)

        @pl.when(j == 0)
        def _():
            m_s[...] = jnp.full_like(m_s, NEG_INF)
            l_s[...] = jnp.zeros_like(l_s)
            acc[...] = jnp.zeros_like(acc)

        def step(diag):
            s = _scores(q_ref[...], k_ref[...], scale, diag)
            m_prev = m_s[...]
            m_cur = jnp.maximum(m_prev, jnp.max(s, axis=1, keepdims=True))
            alpha = jnp.exp(m_prev - m_cur)
            p = jnp.exp(s - m_cur[:, :1])
            l_s[...] = alpha * l_s[...] + jnp.sum(p, axis=1, keepdims=True)
            acc[...] = alpha * acc[...] + jnp.dot(p.astype(BF16), v_ref[...], preferred_element_type=F32)
            m_s[...] = m_cur

        def finish():
            o_ref[...] = (acc[...] / l_s[...]).astype(o_ref.dtype)
            lse_ref[...] = m_s[...] + jnp.log(l_s[...])

        if causal:
            @pl.when(j < i)
            def _():
                step(False)

            @pl.when(j == i)
            def _():
                step(True)
                finish()
        else:
            step(False)

            @pl.when(j == nk - 1)
            def _():
                finish()

    def kv_idx(off):
        if causal:
            return lambda h, i, j: (jnp.minimum(j, i), off + h)
        return lambda h, i, j: (j, off + h)

    blk_q = pl.BlockSpec((tq, LANES), lambda h, i, j: (i, qoff + h))
    out_q = pl.BlockSpec((tq, LANES), lambda h, i, j: (i, h))
    return pl.pallas_call(
        body, name=name, grid=(heads, nq, nk),
        in_specs=[blk_q, pl.BlockSpec((tk, LANES), kv_idx(koff)), pl.BlockSpec((tk, LANES), kv_idx(voff))],
        out_specs=[out_q, out_q],
        out_shape=[jax.ShapeDtypeStruct((s_q, heads * LANES), BF16),
                   jax.ShapeDtypeStruct((s_q, heads * LANES), F32)],
        scratch_shapes=[pltpu.VMEM((tq, LANES), F32)] * 3,
        compiler_params=_cparams(("parallel", "parallel", "arbitrary")),
    )(q, k, v)


def _flash_bwd_dq(q, k, v, do, lse, delta, *, name, heads, qoff, koff, voff, causal, scale, tq, tk, out_dtype):
    s_q, s_kv = q.shape[0], k.shape[0]
    tq, tk = min(tq, s_q), min(tk, s_kv)
    nq, nk = s_q // tq, s_kv // tk

    def body(q_ref, k_ref, v_ref, do_ref, lse_ref, dl_ref, dq_ref, acc):
        i, j = pl.program_id(1), pl.program_id(2)

        @pl.when(j == 0)
        def _():
            acc[...] = jnp.zeros_like(acc)

        def step(diag):
            s = _scores(q_ref[...], k_ref[...], scale, diag)
            p = jnp.exp(s - lse_ref[:, :1])
            dp = lax.dot_general(do_ref[...], v_ref[...], (((1,), (1,)), ((), ())), preferred_element_type=F32)
            ds = p * (dp - dl_ref[:, :1])
            acc[...] += jnp.dot(ds.astype(BF16), k_ref[...], preferred_element_type=F32)

        def finish():
            dq_ref[...] = (acc[...] * scale).astype(out_dtype)

        if causal:
            @pl.when(j < i)
            def _():
                step(False)

            @pl.when(j == i)
            def _():
                step(True)
                finish()
        else:
            step(False)

            @pl.when(j == nk - 1)
            def _():
                finish()

    def kv_idx(off):
        if causal:
            return lambda h, i, j: (jnp.minimum(j, i), off + h)
        return lambda h, i, j: (j, off + h)

    blk_q = pl.BlockSpec((tq, LANES), lambda h, i, j: (i, qoff + h))
    blk_h = pl.BlockSpec((tq, LANES), lambda h, i, j: (i, h))
    return pl.pallas_call(
        body, name=name, grid=(heads, nq, nk),
        in_specs=[blk_q, pl.BlockSpec((tk, LANES), kv_idx(koff)), pl.BlockSpec((tk, LANES), kv_idx(voff)),
                  blk_h, blk_h, blk_h],
        out_specs=blk_h,
        out_shape=jax.ShapeDtypeStruct((s_q, heads * LANES), out_dtype),
        scratch_shapes=[pltpu.VMEM((tq, LANES), F32)],
        compiler_params=_cparams(("parallel", "parallel", "arbitrary")),
    )(q, k, v, do, lse, delta)


def _flash_bwd_dkv(q, k, v, do, lse, delta, *, name, heads, qoff, koff, voff, causal, scale, tq, tk,
                   dk_dtype, dv_dtype):
    s_q, s_kv = q.shape[0], k.shape[0]
    tq, tk = min(tq, s_q), min(tk, s_kv)
    nq, nk = s_q // tq, s_kv // tk

    def body(q_ref, k_ref, v_ref, do_ref, lse_ref, dl_ref, dk_ref, dv_ref, dk_acc, dv_acc):
        j, i = pl.program_id(1), pl.program_id(2)

        @pl.when(i == 0)
        def _():
            dk_acc[...] = jnp.zeros_like(dk_acc)
            dv_acc[...] = jnp.zeros_like(dv_acc)

        def step(diag):
            s = _scores(q_ref[...], k_ref[...], scale, diag)
            p = jnp.exp(s - lse_ref[:, :1])
            dov = do_ref[...]
            dp = lax.dot_general(dov, v_ref[...], (((1,), (1,)), ((), ())), preferred_element_type=F32)
            ds = p * (dp - dl_ref[:, :1])
            dv_acc[...] += lax.dot_general(p.astype(BF16), dov, (((0,), (0,)), ((), ())),
                                           preferred_element_type=F32)
            dk_acc[...] += lax.dot_general(ds.astype(BF16), q_ref[...], (((0,), (0,)), ((), ())),
                                           preferred_element_type=F32)

        if causal:
            @pl.when(i > j)
            def _():
                step(False)

            @pl.when(i == j)
            def _():
                step(True)
        else:
            step(False)

        @pl.when(i == nq - 1)
        def _():
            dk_ref[...] = (dk_acc[...] * scale).astype(dk_dtype)
            dv_ref[...] = dv_acc[...].astype(dv_dtype)

    def q_idx(off):
        if causal:
            return lambda h, j, i: (jnp.maximum(i, j), off + h)
        return lambda h, j, i: (i, off + h)

    blk_h = pl.BlockSpec((tq, LANES), q_idx(0))
    out_k = pl.BlockSpec((tk, LANES), lambda h, j, i: (j, h))
    return pl.pallas_call(
        body, name=name, grid=(heads, nk, nq),
        in_specs=[pl.BlockSpec((tq, LANES), q_idx(qoff)),
                  pl.BlockSpec((tk, LANES), lambda h, j, i: (j, koff + h)),
                  pl.BlockSpec((tk, LANES), lambda h, j, i: (j, voff + h)),
                  blk_h, blk_h, blk_h],
        out_specs=[out_k, out_k],
        out_shape=[jax.ShapeDtypeStruct((s_kv, heads * LANES), dk_dtype),
                   jax.ShapeDtypeStruct((s_kv, heads * LANES), dv_dtype)],
        scratch_shapes=[pltpu.VMEM((tk, LANES), F32)] * 2,
        compiler_params=_cparams(("parallel", "parallel", "arbitrary")),
    )(q, k, v, do, lse, delta)


def _row_dot(a, b, *, name):
    s, n = a.shape
    nh = n // LANES
    ts = _rows(s, 512)

    def body(a_ref, b_ref, o_ref):
        for h in range(nh):
            sl = slice(h * LANES, (h + 1) * LANES)
            d = jnp.sum(a_ref[:, sl].astype(F32) * b_ref[:, sl].astype(F32), axis=1, keepdims=True)
            o_ref[:, sl] = jnp.broadcast_to(d, (ts, LANES))

    row = pl.BlockSpec((ts, n), lambda i: (i, 0))
    return pl.pallas_call(
        body, name=name, grid=(s // ts,), in_specs=[row, row], out_specs=row,
        out_shape=jax.ShapeDtypeStruct((s, n), F32), compiler_params=_cparams(("parallel",)),
    )(a, b)


CAUSAL_T = 512


def _causal_fwd(q, k, v, *, name, heads):
    s = q.shape[0]
    t = min(CAUSAL_T, s)
    nq = s // t
    wide = 2 * t

    def body(q_ref, k_ref, v_ref, o_ref, lse_ref, m_s, l_s, acc):
        i = pl.program_id(1)
        m_s[...] = jnp.full_like(m_s, NEG_INF)
        l_s[...] = jnp.zeros_like(l_s)
        acc[...] = jnp.zeros_like(acc)
        qv = q_ref[...]

        def step(start, size, diag):
            sc = _scores(qv, k_ref[pl.ds(start, size), :], 1.0, diag)
            m_prev = m_s[...]
            m_cur = jnp.maximum(m_prev, jnp.max(sc, axis=1, keepdims=True))
            alpha = jnp.exp(m_prev - m_cur)
            p = jnp.exp(sc - m_cur[:, :1])
            l_s[...] = alpha * l_s[...] + jnp.sum(p, axis=1, keepdims=True)
            acc[...] = alpha * acc[...] + jnp.dot(p.astype(BF16), v_ref[pl.ds(start, size), :],
                                                  preferred_element_type=F32)
            m_s[...] = m_cur

        def pair(jj, carry):
            step(pl.multiple_of(jj * wide, wide), wide, False)
            return carry

        lax.fori_loop(0, i // 2, pair, 0)

        @pl.when(i % 2 == 1)
        def _():
            step(pl.multiple_of((i - 1) * t, t), t, False)

        step(pl.multiple_of(i * t, t), t, True)
        o_ref[...] = (acc[...] / l_s[...]).astype(BF16)
        lse_ref[...] = m_s[...] + jnp.log(l_s[...])

    blk = pl.BlockSpec((t, LANES), lambda h, i: (i, h))
    full = pl.BlockSpec((s, LANES), lambda h, i: (0, h))
    return pl.pallas_call(
        body, name=name, grid=(heads, nq), in_specs=[blk, full, full], out_specs=[blk, blk],
        out_shape=[jax.ShapeDtypeStruct((s, heads * LANES), BF16), jax.ShapeDtypeStruct((s, heads * LANES), F32)],
        scratch_shapes=[pltpu.VMEM((t, LANES), F32)] * 3,
        compiler_params=_cparams(("parallel", "arbitrary")),
    )(q, k, v)


def _causal_bwd(q, k, v, do, lse, delta, *, name, heads):
    s = q.shape[0]
    t = min(CAUSAL_T, s)
    nt = s // t

    def body(q_ref, k_ref, v_ref, do_ref, lse_ref, dl_ref, dq_ref, dk_ref, dv_ref, dk_acc, dv_acc):
        j = pl.program_id(1)

        @pl.when(j == 0)
        def _():
            dq_ref[...] = jnp.zeros_like(dq_ref)

        dk_acc[...] = jnp.zeros_like(dk_acc)
        dv_acc[...] = jnp.zeros_like(dv_acc)
        kv, vv = k_ref[...], v_ref[...]

        def step(i, diag):
            rows = pl.ds(pl.multiple_of(i * t, t), t)
            qv, dov = q_ref[rows, :], do_ref[rows, :]
            sc = _scores(qv, kv, 1.0, diag)
            p = jnp.exp(sc - lse_ref[rows, :][:, :1])
            dp = lax.dot_general(dov, vv, (((1,), (1,)), ((), ())), preferred_element_type=F32)
            ds = (p * (dp - dl_ref[rows, :][:, :1])).astype(BF16)
            dv_acc[...] += lax.dot_general(p.astype(BF16), dov, (((0,), (0,)), ((), ())),
                                           preferred_element_type=F32)
            dk_acc[...] += lax.dot_general(ds, qv, (((0,), (0,)), ((), ())), preferred_element_type=F32)
            dq_ref[rows, :] += jnp.dot(ds, kv, preferred_element_type=F32)

        step(j, True)

        def loop(i, carry):
            step(i, False)
            return carry

        lax.fori_loop(j + 1, nt, loop, 0)
        dk_ref[...] = dk_acc[...]
        dv_ref[...] = dv_acc[...].astype(BF16)

    blk = pl.BlockSpec((t, LANES), lambda h, j: (j, h))
    full = pl.BlockSpec((s, LANES), lambda h, j: (0, h))
    return pl.pallas_call(
        body, name=name, grid=(heads, nt), in_specs=[full, blk, blk, full, full, full],
        out_specs=[full, blk, blk],
        out_shape=[jax.ShapeDtypeStruct((s, heads * LANES), F32), jax.ShapeDtypeStruct((s, heads * LANES), F32),
                   jax.ShapeDtypeStruct((s, heads * LANES), BF16)],
        scratch_shapes=[pltpu.VMEM((t, LANES), F32)] * 2,
        compiler_params=_cparams(("parallel", "arbitrary")),
    )(q, k, v, do, lse, delta)


def _band_masks(dilation, slope):
    qi = lax.broadcasted_iota(jnp.int32, (BLOCK, 2 * BLOCK), 0)
    kj = lax.broadcasted_iota(jnp.int32, (BLOCK, 2 * BLOCK), 1)
    dist = qi + BLOCK - kj
    valid = (dist >= 0) & (dist <= BLOCK)
    bias = -slope * (dist * dilation).astype(F32)
    return valid, bias


def _band_fwd(q, k, v, slopes, *, name, dilation, qoff, koff, voff):
    s = q.shape[0]
    sub = s // dilation
    nb = sub // BLOCK
    assert nb * BLOCK == sub
    scale = LANES ** -0.5

    def body(sl_ref, q_ref, k_ref, v_ref, o_ref, lse_ref):
        slope = sl_ref[pl.program_id(0)]
        valid2, bias2 = _band_masks(dilation, slope)
        valid1, bias1 = valid2[:, BLOCK:], bias2[:, BLOCK:]

        def block(start_q, kk, vv, valid, bias):
            qb = q_ref[pl.ds(start_q, BLOCK), :]
            sc = lax.dot_general(qb, kk, (((1,), (1,)), ((), ())), preferred_element_type=F32) * scale
            sc = jnp.where(valid, sc + bias, NEG_INF)
            m = jnp.max(sc, axis=1, keepdims=True)
            e = jnp.exp(sc - m)
            den = jnp.sum(e, axis=1, keepdims=True)
            p = (e / den).astype(BF16)
            o_ref[pl.ds(start_q, BLOCK), :] = jnp.dot(p, vv, preferred_element_type=F32)
            lse_ref[pl.ds(start_q, BLOCK), :] = jnp.broadcast_to(m + jnp.log(den), (BLOCK, LANES))

        block(0, k_ref[0:BLOCK, :], v_ref[0:BLOCK, :], valid1, bias1)

        def loop(jj, carry):
            start_q = pl.multiple_of(jj * BLOCK, BLOCK)
            start_k = pl.multiple_of((jj - 1) * BLOCK, BLOCK)
            block(start_q, k_ref[pl.ds(start_k, 2 * BLOCK), :], v_ref[pl.ds(start_k, 2 * BLOCK), :], valid2, bias2)
            return carry

        lax.fori_loop(1, nb, loop, 0)

    def spec(off):
        return pl.BlockSpec((sub, LANES), lambda h, r: (r, off + h))

    out = pl.BlockSpec((sub, LANES), lambda h, r: (r, h))
    return pl.pallas_call(
        body, name=name, grid=(DIL_HEADS_PER_GROUP, dilation),
        in_specs=[pl.BlockSpec(memory_space=pltpu.SMEM), spec(qoff), spec(koff), spec(voff)],
        out_specs=[out, out],
        out_shape=[jax.ShapeDtypeStruct((s, DIL_HEADS_PER_GROUP * LANES), F32)] * 2,
        compiler_params=_cparams(("parallel", "parallel"), VMEM_LIMIT_BIG),
    )(slopes, q, k, v)


def _band_bwd(q, k, v, do, lse, delta, slopes, *, name, dilation, qoff, koff, voff):
    s = q.shape[0]
    sub = s // dilation
    nb = sub // BLOCK
    scale = LANES ** -0.5

    def body(sl_ref, q_ref, k_ref, v_ref, do_ref, lse_ref, dl_ref, dq_ref, dk_ref, dv_ref):
        slope = sl_ref[pl.program_id(0)]
        valid2, bias2 = _band_masks(dilation, slope)
        valid1, bias1 = valid2[:, BLOCK:], bias2[:, BLOCK:]

        def block(start_q, kk, vv, valid, bias):
            qb = q_ref[pl.ds(start_q, BLOCK), :]
            dob = do_ref[pl.ds(start_q, BLOCK), :]
            sc = lax.dot_general(qb, kk, (((1,), (1,)), ((), ())), preferred_element_type=F32) * scale
            sc = jnp.where(valid, sc + bias, NEG_INF)
            p = jnp.exp(sc - lse_ref[pl.ds(start_q, BLOCK), :][:, :1])
            dp = lax.dot_general(dob, vv, (((1,), (1,)), ((), ())), preferred_element_type=F32)
            ds = (p * (dp - dl_ref[pl.ds(start_q, BLOCK), :][:, :1])).astype(BF16)
            dq = jnp.dot(ds, kk, preferred_element_type=F32) * scale
            dq_ref[pl.ds(start_q, BLOCK), :] = dq.astype(BF16)
            dkk = lax.dot_general(ds, qb, (((0,), (0,)), ((), ())), preferred_element_type=F32) * scale
            dvv = lax.dot_general(p.astype(BF16), dob, (((0,), (0,)), ((), ())), preferred_element_type=F32)
            return dkk, dvv

        carry0 = block(0, k_ref[0:BLOCK, :], v_ref[0:BLOCK, :], valid1, bias1)

        def loop(jj, carry):
            dk_part, dv_part = carry
            start_q = pl.multiple_of(jj * BLOCK, BLOCK)
            start_k = pl.multiple_of((jj - 1) * BLOCK, BLOCK)
            dkk, dvv = block(start_q, k_ref[pl.ds(start_k, 2 * BLOCK), :], v_ref[pl.ds(start_k, 2 * BLOCK), :],
                             valid2, bias2)
            dk_ref[pl.ds(start_k, BLOCK), :] = (dk_part + dkk[:BLOCK]).astype(BF16)
            dv_ref[pl.ds(start_k, BLOCK), :] = (dv_part + dvv[:BLOCK]).astype(BF16)
            return dkk[BLOCK:], dvv[BLOCK:]

        dk_last, dv_last = lax.fori_loop(1, nb, loop, carry0)
        dk_ref[(nb - 1) * BLOCK:nb * BLOCK, :] = dk_last.astype(BF16)
        dv_ref[(nb - 1) * BLOCK:nb * BLOCK, :] = dv_last.astype(BF16)

    def spec(off):
        return pl.BlockSpec((sub, LANES), lambda h, r: (r, off + h))

    out = spec(0)
    return pl.pallas_call(
        body, name=name, grid=(DIL_HEADS_PER_GROUP, dilation),
        in_specs=[pl.BlockSpec(memory_space=pltpu.SMEM), spec(qoff), spec(koff), spec(voff), out, out, out],
        out_specs=[out, out, out],
        out_shape=[jax.ShapeDtypeStruct((s, DIL_HEADS_PER_GROUP * LANES), BF16)] * 3,
        compiler_params=_cparams(("parallel", "parallel"), VMEM_LIMIT_BIG),
    )(slopes, q, k, v, do, lse, delta)


def _mix_fwd(outs, lses, *, name):
    s, n = outs[0].shape
    ts = _rows(s, 512)

    def body(o0, o1, o2, l0, l1, l2, y_ref):
        la, lb, lc = l0[...], l1[...], l2[...]
        m = jnp.maximum(jnp.maximum(la, lb), lc)
        ea, eb, ec = jnp.exp(la - m), jnp.exp(lb - m), jnp.exp(lc - m)
        den = ea + eb + ec
        y = (ea / den) * o0[...] + (eb / den) * o1[...] + (ec / den) * o2[...]
        y_ref[...] = y.astype(BF16)

    row = pl.BlockSpec((ts, n), lambda i: (i, 0))
    return pl.pallas_call(
        body, name=name, grid=(s // ts,), in_specs=[row] * 6, out_specs=row,
        out_shape=jax.ShapeDtypeStruct((s, n), BF16), compiler_params=_cparams(("parallel",)),
    )(*outs, *lses)


def _mix_bwd(dy, outs, lses, *, name):
    s, n = dy.shape
    nh = n // LANES
    ts = _rows(s, 256)

    def body(dy_ref, o0, o1, o2, l0, l1, l2, d0, d1, d2, e0, e1, e2):
        la, lb, lc = l0[...], l1[...], l2[...]
        m = jnp.maximum(jnp.maximum(la, lb), lc)
        ea, eb, ec = jnp.exp(la - m), jnp.exp(lb - m), jnp.exp(lc - m)
        den = ea + eb + ec
        wa, wb, wc = ea / den, eb / den, ec / den
        dyv = dy_ref[...]
        y = wa * o0[...] + wb * o1[...] + wc * o2[...]
        prod = dyv * y
        d0[...] = (wa * dyv).astype(BF16)
        d1[...] = (wb * dyv).astype(BF16)
        d2[...] = (wc * dyv).astype(BF16)
        for h in range(nh):
            sl = slice(h * LANES, (h + 1) * LANES)
            t = jnp.sum(prod[:, sl], axis=1, keepdims=True)
            e0[:, sl] = wa[:, sl] * t
            e1[:, sl] = wb[:, sl] * t
            e2[:, sl] = wc[:, sl] * t

    row = pl.BlockSpec((ts, n), lambda i: (i, 0))
    return pl.pallas_call(
        body, name=name, grid=(s // ts,), in_specs=[row] * 7, out_specs=[row] * 6,
        out_shape=[jax.ShapeDtypeStruct((s, n), BF16)] * 3 + [jax.ShapeDtypeStruct((s, n), F32)] * 3,
        compiler_params=_cparams(("parallel",)),
    )(dy, *outs, *lses)


def _gate_fwd(gp, b_gate, branches, *, name):
    s = gp.shape[0]
    ts = _rows(s, 256)

    def body(gp_ref, b_ref, b0, b1, b2, o_ref):
        tot = None
        for i, br in enumerate((b0, b1, b2)):
            sl = slice(i * D_MODEL, (i + 1) * D_MODEL)
            t = jax.nn.sigmoid(gp_ref[:, sl] + b_ref[:, sl]) * br[...]
            tot = t if tot is None else tot + t
        o_ref[...] = tot.astype(BF16)

    row = pl.BlockSpec((ts, D_MODEL), lambda i: (i, 0))
    return pl.pallas_call(
        body, name=name, grid=(s // ts,),
        in_specs=[pl.BlockSpec((ts, 3 * D_MODEL), lambda i: (i, 0)), pl.BlockSpec((1, 3 * D_MODEL), lambda i: (0, 0)),
                  row, row, row],
        out_specs=row, out_shape=jax.ShapeDtypeStruct((s, D_MODEL), BF16),
        compiler_params=_cparams(("parallel",)),
    )(gp, b_gate, *branches)


def _gate_bwd(dm, gp, b_gate, branches, *, name):
    s = gp.shape[0]
    ts = _rows(s, 256)

    def body(dm_ref, gp_ref, b_ref, b0, b1, b2, d0, d1, d2, dgp_ref, db_ref):
        i = pl.program_id(0)

        @pl.when(i == 0)
        def _():
            db_ref[...] = jnp.zeros_like(db_ref)

        dmv = dm_ref[...]
        for k, (br, dbr) in enumerate(((b0, d0), (b1, d1), (b2, d2))):
            sl = slice(k * D_MODEL, (k + 1) * D_MODEL)
            sg = jax.nn.sigmoid(gp_ref[:, sl] + b_ref[:, sl])
            dbr[...] = (dmv * sg).astype(BF16)
            dg = dmv * br[...] * sg * (1.0 - sg)
            dgp_ref[:, sl] = dg.astype(BF16)
            db_ref[:, sl] += jnp.sum(dg, axis=0, keepdims=True)

    row = pl.BlockSpec((ts, D_MODEL), lambda i: (i, 0))
    wide = pl.BlockSpec((ts, 3 * D_MODEL), lambda i: (i, 0))
    vec = pl.BlockSpec((1, 3 * D_MODEL), lambda i: (0, 0))
    return pl.pallas_call(
        body, name=name, grid=(s // ts,),
        in_specs=[row, wide, vec, row, row, row], out_specs=[row, row, row, wide, vec],
        out_shape=[jax.ShapeDtypeStruct((s, D_MODEL), BF16)] * 3
        + [jax.ShapeDtypeStruct((s, 3 * D_MODEL), BF16), jax.ShapeDtypeStruct((1, 3 * D_MODEL), F32)],
        compiler_params=_cparams(("arbitrary",)),
    )(dm, gp, b_gate, *branches)


CONV_TC = 1408


def _shift_down(x, halo, k):
    rolled = pltpu.roll(x, k, 0)
    r8 = lax.broadcasted_iota(jnp.int32, halo.shape, 0)
    top = jnp.where(r8 < k, pltpu.roll(halo, k, 0), rolled[:SUBLANES])
    return jnp.concatenate([top, rolled[SUBLANES:]], axis=0)


def _shift_up(x, halo, k):
    n = x.shape[0]
    rolled = pltpu.roll(x, n - k, 0)
    r8 = lax.broadcasted_iota(jnp.int32, halo.shape, 0)
    bot = jnp.where(r8 >= SUBLANES - k, pltpu.roll(halo, SUBLANES - k, 0), rolled[n - SUBLANES:])
    return jnp.concatenate([rolled[:n - SUBLANES], bot], axis=0)


def _conv_fwd(u, conv_w, conv_b, *, name):
    s = u.shape[0]
    ts = _rows(s, 256)
    nct = D_FF // CONV_TC
    per8 = ts // SUBLANES

    def body(ug, uv, hg, hv, wg, wv, bg, bv, zg_ref, zv_ref, a_ref):
        first = pl.program_id(1) == 0

        def conv(u_ref, h_ref, w_ref, b_ref):
            x = u_ref[...]
            halo = jnp.where(first, 0.0, h_ref[...])
            z = b_ref[...] + w_ref[0:1, :] * _shift_down(x, halo, 2)
            z = z + w_ref[1:2, :] * _shift_down(x, halo, 1)
            return z + w_ref[2:3, :] * x

        zg = conv(ug, hg, wg, bg)
        zv = conv(uv, hv, wv, bv)
        zg_ref[...] = zg
        zv_ref[...] = zv
        a_ref[...] = (zg * jax.nn.sigmoid(zg) * zv).astype(BF16)

    def col(off):
        return pl.BlockSpec((ts, CONV_TC), lambda c, i: (i, c + off))

    def halo(off):
        return pl.BlockSpec((SUBLANES, CONV_TC), lambda c, i: (jnp.maximum(i * per8 - 1, 0), c + off))

    def wspec(rows, off):
        return pl.BlockSpec((rows, CONV_TC), lambda c, i: (0, c + off))

    zg, zv, a = pl.pallas_call(
        body, name=name, grid=(nct, s // ts),
        in_specs=[col(0), col(nct), halo(0), halo(nct), wspec(3, 0), wspec(3, nct), wspec(1, 0), wspec(1, nct)],
        out_specs=[col(0), col(0), col(0)],
        out_shape=[jax.ShapeDtypeStruct((s, D_FF), F32)] * 2 + [jax.ShapeDtypeStruct((s, D_FF), BF16)],
        compiler_params=_cparams(("parallel", "parallel")),
    )(u, u, u, u, conv_w, conv_w, conv_b, conv_b)
    return zg, zv, a


def _silu_bwd(da, zg, zv, *, name):
    s = da.shape[0]
    ts = _rows(s, 256)
    nct = D_FF // CONV_TC

    def body(da_ref, zg_ref, zv_ref, dz_g, dz_v):
        g = zg_ref[...]
        sg = jax.nn.sigmoid(g)
        dav = da_ref[...]
        dz_g[...] = dav * zv_ref[...] * sg * (1.0 + g * (1.0 - sg))
        dz_v[...] = dav * g * sg

    col = pl.BlockSpec((ts, CONV_TC), lambda c, i: (i, c))
    return pl.pallas_call(
        body, name=name, grid=(nct, s // ts), in_specs=[col, col, col], out_specs=[col, col],
        out_shape=[jax.ShapeDtypeStruct((s, D_FF), F32)] * 2,
        compiler_params=_cparams(("parallel", "parallel")),
    )(da, zg, zv)


def _conv_bwd(dz, u, conv_w, *, name, off):
    s = dz.shape[0]
    ts = _rows(s, 256)
    nct = D_FF // CONV_TC
    per8 = ts // SUBLANES
    nrow = s // ts
    last8 = s // SUBLANES - 1

    def body(dz_ref, nx_ref, u_ref, pv_ref, w_ref, du_ref, acc_ref):
        i = pl.program_id(1)
        dzv = dz_ref[...]
        nxt = jnp.where(i == nrow - 1, 0.0, nx_ref[...])
        du = w_ref[2:3, :] * dzv + w_ref[1:2, :] * _shift_up(dzv, nxt, 1) + w_ref[0:1, :] * _shift_up(dzv, nxt, 2)
        du_ref[...] = du.astype(BF16)

        @pl.when(i == 0)
        def _():
            acc_ref[...] = jnp.zeros_like(acc_ref)

        x = u_ref[...]
        prev = jnp.where(i == 0, 0.0, pv_ref[...])
        acc_ref[0:1, :] += jnp.sum(dzv * _shift_down(x, prev, 2), axis=0, keepdims=True)
        acc_ref[1:2, :] += jnp.sum(dzv * _shift_down(x, prev, 1), axis=0, keepdims=True)
        acc_ref[2:3, :] += jnp.sum(dzv * x, axis=0, keepdims=True)
        acc_ref[3:4, :] += jnp.sum(dzv, axis=0, keepdims=True)

    return pl.pallas_call(
        body, name=name, grid=(nct, nrow),
        in_specs=[pl.BlockSpec((ts, CONV_TC), lambda c, i: (i, c)),
                  pl.BlockSpec((SUBLANES, CONV_TC), lambda c, i: (jnp.minimum((i + 1) * per8, last8), c)),
                  pl.BlockSpec((ts, CONV_TC), lambda c, i: (i, c + off)),
                  pl.BlockSpec((SUBLANES, CONV_TC), lambda c, i: (jnp.maximum(i * per8 - 1, 0), c + off)),
                  pl.BlockSpec((3, CONV_TC), lambda c, i: (0, c + off))],
        out_specs=[pl.BlockSpec((ts, CONV_TC), lambda c, i: (i, c)),
                   pl.BlockSpec((SUBLANES, CONV_TC), lambda c, i: (0, c))],
        out_shape=[jax.ShapeDtypeStruct((s, D_FF), BF16), jax.ShapeDtypeStruct((SUBLANES, D_FF), F32)],
        compiler_params=_cparams(("parallel", "arbitrary")),
    )(dz, dz, u, u, conv_w)


def _peer(k):
    x, y, c = lax.axis_index("x"), lax.axis_index("y"), lax.axis_index("c")
    px = 1 - x if k & 4 else x
    py = 1 - y if k & 2 else y
    pc = 1 - c if k & 1 else c
    return (px, py, pc), 4 * px + 2 * py + pc


def _exchange(bufs, *, name, gather):
    n = len(bufs)
    npeer = N_DEV - 1

    def body(*refs):
        srcs, outs = refs[:n], refs[n:2 * n]
        send_sems, recv_sems, local_sems = refs[2 * n:]
        _, me = _peer(0)
        mine = [src if gather else src.at[me] for src in srcs]
        local = [pltpu.make_async_copy(mine[p], outs[p].at[me], local_sems.at[p]) for p in range(n)]
        for cp in local:
            cp.start()
        sends = []
        for k in range(1, N_DEV):
            dev, idx = _peer(k)
            for p in range(n):
                cp = pltpu.make_async_remote_copy(
                    src_ref=srcs[p] if gather else srcs[p].at[idx], dst_ref=outs[p].at[me],
                    send_sem=send_sems.at[p * npeer + k - 1], recv_sem=recv_sems.at[p * npeer + k - 1],
                    device_id=dev, device_id_type=pl.DeviceIdType.MESH)
                cp.start()
                sends.append(cp)
        for k in range(1, N_DEV):
            dev, idx = _peer(k)
            for p in range(n):
                pltpu.make_async_remote_copy(
                    src_ref=mine[p], dst_ref=outs[p].at[idx],
                    send_sem=send_sems.at[p * npeer + k - 1], recv_sem=recv_sems.at[p * npeer + k - 1],
                    device_id=dev, device_id_type=pl.DeviceIdType.MESH).wait_recv()
        for cp in sends:
            cp.wait_send()
        for cp in local:
            cp.wait()

    any_spec = pl.BlockSpec(memory_space=pl.ANY)
    return pl.pallas_call(
        body, name=name,
        in_specs=[any_spec] * n, out_specs=[any_spec] * n,
        out_shape=[jax.ShapeDtypeStruct((N_DEV,) + b.shape[-2:], b.dtype) for b in bufs],
        scratch_shapes=[pltpu.SemaphoreType.DMA((n * npeer,)), pltpu.SemaphoreType.DMA((n * npeer,)),
                        pltpu.SemaphoreType.DMA((n,))],
    )(*bufs)


def _adamw(parts, w, m, v, *, name):
    rows, width = w.shape
    tr = _rows(rows, max(16, (128 * 1024) // width), mult=16)

    def body(p_ref, w_ref, m_ref, v_ref, g_ref, d_ref, nm_ref, nv_ref):
        g = p_ref[0].astype(F32)
        for k in range(1, N_DEV):
            g = g + p_ref[k].astype(F32)
        mn = ADAM_B1 * m_ref[...] + (1.0 - ADAM_B1) * g
        vn = ADAM_B2 * v_ref[...] + (1.0 - ADAM_B2) * jnp.square(g)
        m_hat = mn / (1.0 - ADAM_B1 ** ADAM_STEP)
        v_hat = vn / (1.0 - ADAM_B2 ** ADAM_STEP)
        g_ref[...] = g
        d_ref[...] = -ADAM_LR * (m_hat / (jnp.sqrt(v_hat) + ADAM_EPS) + ADAM_WD * w_ref[...])
        nm_ref[...] = mn
        nv_ref[...] = vn

    row = pl.BlockSpec((tr, width), lambda i: (i, 0))
    return pl.pallas_call(
        body, name=name, grid=(rows // tr,),
        in_specs=[pl.BlockSpec((N_DEV, tr, width), lambda i: (0, i, 0)), row, row, row],
        out_specs=[row] * 4, out_shape=[jax.ShapeDtypeStruct((rows, width), F32)] * 4,
        compiler_params=_cparams(("parallel",)),
    )(parts, w, m, v)


def _pack_replicated(blocks):
    flat = jnp.concatenate([blocks[name].reshape(-1).astype(F32) for name, _, _ in REPLICATED])
    flat = jnp.pad(flat, (0, PACK_ROWS * PACK_W - PACK_USED))
    return flat.reshape(PACK_ROWS, PACK_W)


def _unpack_replicated(buf):
    flat = buf.reshape(-1)
    return {name: flat[off:off + n].reshape(1, n) for name, (off, n) in PACK_TABLE.items()}


def _join_shards(seg, shape, axis):
    return seg.reshape(shape) if axis == 0 else seg.transpose(1, 0, 2).reshape(shape)


def _split_shards(g, shape, axis):
    r, c = shape
    if axis == 0:
        return g.reshape(N_DEV, r // N_DEV, c)
    return g.reshape(r, N_DEV, c // N_DEV).transpose(1, 0, 2)


def _to_residues(a, d):
    s, c = a.shape
    return a.reshape(s // d, d, c).transpose(1, 0, 2).reshape(s, c)


def _from_residues(a, d):
    s, c = a.shape
    return a.reshape(d, s // d, c).transpose(1, 0, 2).reshape(s, c)


def _pad_heads(w, heads, width, lo, hi):
    r = w.shape[0]
    w = w.reshape(r, heads, width)[:, :, lo:hi]
    w = jnp.pad(w, ((0, 0), (0, 0), (0, LANES - (hi - lo))))
    return w.reshape(r, heads * LANES)


def _local_step(x, mem, positions, target, w):
    s = x.shape[0]
    bf = lambda a: a.astype(BF16)

    w_in = w["w_in"]
    kr_cols = jnp.pad(w_in[:, OFF_KV:OFF_KR], ((0, 0), (MLA_NOPE, LANES - MLA_QK_DIM)))
    w_a = bf(jnp.concatenate([w_in[:, :OFF_KV], kr_cols], axis=1))
    w_dm = bf(w_in[:, OFF_KR:OFF_MEMQ])
    w_g = bf(w_in[:, OFF_MEMQ:])
    w_in_t = jnp.concatenate([w_a, w_dm, w_g], axis=1).T
    wq = bf(_pad_heads(w["w_uq"], MLA_HEADS, MLA_QK_DIM, 0, MLA_QK_DIM))
    wk = bf(_pad_heads(w["w_ukv"], MLA_HEADS, MLA_NOPE + MLA_V, 0, MLA_NOPE))
    wv = bf(_pad_heads(w["w_ukv"], MLA_HEADS, MLA_NOPE + MLA_V, MLA_NOPE, MLA_NOPE + MLA_V))
    w_mkv = bf(w["w_mem_kv"])
    wb_mla = bf(jnp.pad(w["w_br_mla"].reshape(MLA_HEADS, MLA_V, D_MODEL),
                        ((0, 0), (0, LANES - MLA_V), (0, 0))).reshape(MLA_HEADS * LANES, D_MODEL))
    wb_dil, wb_mem, w_o = bf(w["w_br_dil"]), bf(w["w_br_mem"]), bf(w["w_o"])
    w_up, w_down = bf(w["w_ffn_up"]), bf(w["w_ffn_down"])
    tabs = _rope_tables(positions)
    slopes = [jnp.asarray(sl, F32) for sl in DIL_SLOPES]
    mla_scale = MLA_QK_DIM ** -0.5
    mem_scale = LANES ** -0.5
    MQ = 3 * DIL_HEADS

    h = _rms_fwd(x, w["g_pre_mix"], name="rms_pre_mix", out_dtype=BF16)
    p_a = _matmul(h, w_a, name="proj_a")
    p_dm = _matmul(h, w_dm, name="proj_dm", out_dtype=BF16)
    p_g = _matmul(h, w_g, name="proj_gate")
    c_q, c_kv, kr = p_a[:, :OFF_Q], p_a[:, OFF_Q:OFF_KV], p_a[:, OFF_KV:]

    qn = _rms_fwd(c_q, w["mla_q_norm"], name="rms_q", out_dtype=BF16)
    kvn = _rms_fwd(c_kv, w["mla_kv_norm"], name="rms_kv", out_dtype=BF16)
    q_raw = _matmul(qn, wq, name="mla_q_up")
    k_raw = _matmul(kvn, wk, name="mla_k_up")
    v_f = _matmul(kvn, wv, name="mla_v_up", out_dtype=BF16)
    q_f = _rope_fwd(q_raw, tabs, name="rope_q", scale=mla_scale)
    k_f = _rope_fwd(k_raw, tabs, name="rope_k", scale=1.0, add=kr)
    o_mla, lse_mla = _causal_fwd(q_f, k_f, v_f, name="mla_fwd", heads=MLA_HEADS)

    dil_in, dil_o, dil_lse = [], [], []
    for g, (_, d) in enumerate(DIL_PAIRS):
        if d == 1:
            arrs, offs = (p_dm, p_dm, p_dm), (4 * g, DIL_HEADS + 4 * g, 2 * DIL_HEADS + 4 * g)
        else:
            arrs = tuple(_to_residues(p_dm[:, (t * DIL_HEADS + 4 * g) * LANES:(t * DIL_HEADS + 4 * g + 4) * LANES], d)
                         for t in range(3))
            offs = (0, 0, 0)
        o_g, lse_g = _band_fwd(*arrs, slopes[g], name=f"dil_fwd_{g}", dilation=d,
                               qoff=offs[0], koff=offs[1], voff=offs[2])
        dil_in.append((arrs, offs))
        dil_o.append(_from_residues(o_g, d))
        dil_lse.append(_from_residues(lse_g, d))
    y_dil = _mix_fwd(dil_o, dil_lse, name="dil_mix")

    memn = _rms_fwd(mem, w["g_mem"], name="rms_mem", out_dtype=BF16)
    kv_m = _matmul(memn, w_mkv, name="mem_kv", out_dtype=BF16)
    memat = dict(heads=MEM_HEADS, qoff=MQ, koff=0, voff=MEM_HEADS, causal=False, scale=mem_scale, tq=512, tk=256)
    o_mem, lse_mem = _flash_fwd(p_dm, kv_m, kv_m, name="mem_fwd", **memat)

    b_mla = _matmul(o_mla, wb_mla, name="br_mla")
    b_dil = _matmul(y_dil, wb_dil, name="br_dil")
    b_mem = _matmul(o_mem, wb_mem, name="br_mem")
    merged = _gate_fwd(p_g, w["b_gate"], (b_mla, b_dil, b_mem), name="gate_fwd")
    z1 = _matmul(merged, w_o, name="out_proj")
    x1 = _rms_fwd(z1, w["g_post_mix"], name="rms_post_mix", out_dtype=F32, add=x)
    h2 = _rms_fwd(x1, w["g_pre_ffn"], name="rms_pre_ffn", out_dtype=BF16)
    u = _matmul(h2, w_up, name="ffn_up")
    zg, zv, act = _conv_fwd(u, w["conv_w"], w["conv_b"], name="conv_fwd")
    f = _matmul(act, w_down, name="ffn_down")
    dy, sq = _loss_fwd(x1, f, target, w["g_post_ffn"], name="loss")
    loss = 0.5 * jnp.sum(sq) / D_MODEL

    grads = {}
    df, grads["g_post_ffn"] = _rms_bwd(f, w["g_post_ffn"], dy, name="rms_post_ffn_bwd", out_dtype=BF16)
    da = _matmul(df, w_down.T, name="ffn_down_dx")
    grads["w_ffn_down"] = _matmul(act, df, name="ffn_down_dw", trans_a=True)
    dzg, dzv = _silu_bwd(da, zg, zv, name="silu_bwd")
    nct = D_FF // CONV_TC
    du_g, cacc_g = _conv_bwd(dzg, u, w["conv_w"], name="conv_bwd_gate", off=0)
    du_v, cacc_v = _conv_bwd(dzv, u, w["conv_w"], name="conv_bwd_val", off=nct)
    grads["conv_w"] = jnp.concatenate([cacc_g[0:3], cacc_v[0:3]], axis=1)
    grads["conv_b"] = jnp.concatenate([cacc_g[3:4], cacc_v[3:4]], axis=1)
    w_up_t = w_up.T
    dh2 = _matmul(du_g, w_up_t[:D_FF], name="ffn_up_dx_gate")
    dh2 = _matmul(du_v, w_up_t[D_FF:], name="ffn_up_dx_val", add=dh2)
    grads["w_ffn_up"] = jnp.concatenate([_matmul(h2, du_g, name="ffn_up_dw_gate", trans_a=True),
                                         _matmul(h2, du_v, name="ffn_up_dw_val", trans_a=True)], axis=1)
    dx1, grads["g_pre_ffn"] = _rms_bwd(x1, w["g_pre_ffn"], dh2, name="rms_pre_ffn_bwd", out_dtype=F32, add=dy)
    dz1, grads["g_post_mix"] = _rms_bwd(z1, w["g_post_mix"], dx1, name="rms_post_mix_bwd", out_dtype=BF16)
    dmerged = _matmul(dz1, w_o.T, name="out_proj_dx")
    grads["w_o"] = _matmul(merged, dz1, name="out_proj_dw", trans_a=True)
    db_mla, db_dil, db_mem, dgp, grads["b_gate"] = _gate_bwd(
        dmerged, p_g, w["b_gate"], (b_mla, b_dil, b_mem), name="gate_bwd")

    do_mla = _matmul(db_mla, wb_mla.T, name="br_mla_dx", out_dtype=BF16)
    g_wb_mla = _matmul(o_mla, db_mla, name="br_mla_dw", trans_a=True)
    grads["w_br_mla"] = g_wb_mla.reshape(MLA_HEADS, LANES, D_MODEL)[:, :MLA_V].reshape(MLA_HEADS * MLA_V, D_MODEL)
    delta_mla = _row_dot(do_mla, o_mla, name="mla_delta")
    dq_f, dk_f, dv_f = _causal_bwd(q_f, k_f, v_f, do_mla, lse_mla, delta_mla, name="mla_bwd", heads=MLA_HEADS)
    dq_raw = _rope_bwd(dq_f, tabs, name="rope_q_bwd", scale=mla_scale, with_add=False)
    dk_raw, dkr = _rope_bwd(dk_f, tabs, name="rope_k_bwd", scale=1.0, with_add=True)
    dqn = _matmul(dq_raw, wq.T, name="mla_q_up_dx")
    g_wq = _matmul(qn, dq_raw, name="mla_q_up_dw", trans_a=True)
    grads["w_uq"] = g_wq.reshape(MLA_Q_RANK, MLA_HEADS, LANES)[:, :, :MLA_QK_DIM].reshape(MLA_Q_RANK, -1)
    dkvn = _matmul(dk_raw, wk.T, name="mla_k_up_dx")
    dkvn = _matmul(dv_f, wv.T, name="mla_v_up_dx", add=dkvn)
    g_wk = _matmul(kvn, dk_raw, name="mla_k_up_dw", trans_a=True).reshape(MLA_KV_RANK, MLA_HEADS, LANES)
    g_wv = _matmul(kvn, dv_f, name="mla_v_up_dw", trans_a=True).reshape(MLA_KV_RANK, MLA_HEADS, LANES)
    grads["w_ukv"] = jnp.concatenate([g_wk[:, :, :MLA_NOPE], g_wv[:, :, :MLA_V]], axis=2).reshape(MLA_KV_RANK, -1)
    dc_q, grads["mla_q_norm"] = _rms_bwd(c_q, w["mla_q_norm"], dqn, name="rms_q_bwd", out_dtype=BF16)
    dc_kv, grads["mla_kv_norm"] = _rms_bwd(c_kv, w["mla_kv_norm"], dkvn, name="rms_kv_bwd", out_dtype=BF16)

    dy_dil = _matmul(db_dil, wb_dil.T, name="br_dil_dx")
    grads["w_br_dil"] = _matmul(y_dil, db_dil, name="br_dil_dw", trans_a=True)
    mix = _mix_bwd(dy_dil, dil_o, dil_lse, name="dil_mix_bwd")
    d_dil = [[None] * 3 for _ in range(3)]
    for g, (_, d) in enumerate(DIL_PAIRS):
        arrs, offs = dil_in[g]
        do_g, dl_g, lse_g = mix[g], mix[3 + g], dil_lse[g]
        if d != 1:
            do_g, dl_g, lse_g = _to_residues(do_g, d), _to_residues(dl_g, d), _to_residues(lse_g, d)
        dq_g, dk_g, dv_g = _band_bwd(*arrs, do_g, lse_g, dl_g, slopes[g], name=f"dil_bwd_{g}", dilation=d,
                                     qoff=offs[0], koff=offs[1], voff=offs[2])
        for t, a in enumerate((dq_g, dk_g, dv_g)):
            d_dil[t][g] = a if d == 1 else _from_residues(a, d)

    do_mem = _matmul(db_mem, wb_mem.T, name="br_mem_dx", out_dtype=BF16)
    grads["w_br_mem"] = _matmul(o_mem, db_mem, name="br_mem_dw", trans_a=True)
    delta_mem = _row_dot(do_mem, o_mem, name="mem_delta")
    dq_mem = _flash_bwd_dq(p_dm, kv_m, kv_m, do_mem, lse_mem, delta_mem, name="mem_bwd_dq", out_dtype=BF16, **memat)
    dk_mem, dv_mem = _flash_bwd_dkv(p_dm, kv_m, kv_m, do_mem, lse_mem, delta_mem, name="mem_bwd_dkv",
                                    dk_dtype=BF16, dv_dtype=BF16, **memat)
    dkv_m = jnp.concatenate([dk_mem, dv_mem], axis=1)
    dmemn = _matmul(dkv_m, w_mkv.T, name="mem_kv_dx")
    grads["w_mem_kv"] = _matmul(memn, dkv_m, name="mem_kv_dw", trans_a=True)
    _, grads["g_mem"] = _rms_bwd(mem, w["g_mem"], dmemn, name="rms_mem_bwd", out_dtype=BF16)

    dp_all = jnp.concatenate([dc_q, dc_kv, bf(dkr)] + d_dil[0] + d_dil[1] + d_dil[2] + [dq_mem, dgp], axis=1)
    dh = _matmul(dp_all, w_in_t, name="proj_dx")
    g_in = _matmul(h, dp_all, name="proj_dw", trans_a=True)
    grads["w_in"] = jnp.concatenate(
        [g_in[:, :OFF_KV], g_in[:, OFF_KV + MLA_NOPE:OFF_KV + MLA_QK_DIM], g_in[:, N_A:]], axis=1)
    dx, grads["g_pre_mix"] = _rms_bwd(x, w["g_pre_mix"], dh, name="rms_pre_mix_bwd", out_dtype=F32, add=dx1)
    return loss, dx, grads


def kernel(x, mem, positions, g_pre_mix, w_in, b_gate, mla_q_norm, w_uq, mla_kv_norm, w_ukv, g_mem, w_mem_kv, w_br_mla, w_br_dil, w_br_mem, w_o, g_post_mix, g_pre_ffn, w_ffn_up, conv_w, conv_b, w_ffn_down, g_post_ffn, loss_target, m_g_pre_mix, m_w_in, m_b_gate, m_mla_q_norm, m_w_uq, m_mla_kv_norm, m_w_ukv, m_g_mem, m_w_mem_kv, m_w_br_mla, m_w_br_dil, m_w_br_mem, m_w_o, m_g_post_mix, m_g_pre_ffn, m_w_ffn_up, m_conv_w, m_conv_b, m_w_ffn_down, m_g_post_ffn, v_g_pre_mix, v_w_in, v_b_gate, v_mla_q_norm, v_w_uq, v_mla_kv_norm, v_w_ukv, v_g_mem, v_w_mem_kv, v_w_br_mla, v_w_br_dil, v_w_br_mem, v_w_o, v_g_post_mix, v_g_pre_ffn, v_w_ffn_up, v_conv_w, v_conv_b, v_w_ffn_down, v_g_post_ffn):
    local = dict(g_pre_mix=g_pre_mix, w_in=w_in, b_gate=b_gate, mla_q_norm=mla_q_norm, w_uq=w_uq,
                 mla_kv_norm=mla_kv_norm, w_ukv=w_ukv, g_mem=g_mem, w_mem_kv=w_mem_kv, w_br_mla=w_br_mla,
                 w_br_dil=w_br_dil, w_br_mem=w_br_mem, w_o=w_o, g_post_mix=g_post_mix, g_pre_ffn=g_pre_ffn,
                 w_ffn_up=w_ffn_up, conv_w=conv_w, conv_b=conv_b, w_ffn_down=w_ffn_down, g_post_ffn=g_post_ffn)
    mom_m = dict(g_pre_mix=m_g_pre_mix, w_in=m_w_in, b_gate=m_b_gate, mla_q_norm=m_mla_q_norm, w_uq=m_w_uq,
                 mla_kv_norm=m_mla_kv_norm, w_ukv=m_w_ukv, g_mem=m_g_mem, w_mem_kv=m_w_mem_kv, w_br_mla=m_w_br_mla,
                 w_br_dil=m_w_br_dil, w_br_mem=m_w_br_mem, w_o=m_w_o, g_post_mix=m_g_post_mix,
                 g_pre_ffn=m_g_pre_ffn, w_ffn_up=m_w_ffn_up, conv_w=m_conv_w, conv_b=m_conv_b,
                 w_ffn_down=m_w_ffn_down, g_post_ffn=m_g_post_ffn)
    mom_v = dict(g_pre_mix=v_g_pre_mix, w_in=v_w_in, b_gate=v_b_gate, mla_q_norm=v_mla_q_norm, w_uq=v_w_uq,
                 mla_kv_norm=v_mla_kv_norm, w_ukv=v_w_ukv, g_mem=v_g_mem, w_mem_kv=v_w_mem_kv, w_br_mla=v_w_br_mla,
                 w_br_dil=v_w_br_dil, w_br_mem=v_w_br_mem, w_o=v_w_o, g_post_mix=v_g_post_mix,
                 g_pre_ffn=v_g_pre_ffn, w_ffn_up=v_w_ffn_up, conv_w=v_conv_w, conv_b=v_conv_b,
                 w_ffn_down=v_w_ffn_down, g_post_ffn=v_g_post_ffn)

    shards = [local[name][0].astype(F32 if name == "conv_w" else BF16) for name, _, _ in SHARDED]
    gathered = _exchange(shards, name="gather_weights", gather=True)
    full = {name: _join_shards(seg, shape, axis) for (name, shape, axis), seg in zip(SHARDED, gathered)}
    for name, _, _ in REPLICATED:
        full[name] = local[name].reshape(1, -1)

    loss, dx, grads = _local_step(x[0], mem[0], positions[0], loss_target[0], full)

    slabs = [_split_shards(grads[name].astype(F32 if name == "conv_w" else BF16), shape, axis)
             for name, shape, axis in SHARDED]
    parts = _exchange(slabs, name="exchange_grads", gather=False)
    rep_parts = _exchange([_pack_replicated(grads)], name="gather_replicated_grads", gather=True)[0]

    results = {}
    for (name, _, _), part in zip(SHARDED, parts):
        res = _adamw(part, local[name][0], mom_m[name][0], mom_v[name][0], name="adamw_" + name)
        results[name] = [r[None] for r in res]
    rep = _adamw(rep_parts, _pack_replicated(local), _pack_replicated(mom_m), _pack_replicated(mom_v),
                 name="adamw_replicated")
    for i, buf in enumerate(rep):
        for name, val in _unpack_replicated(buf).items():
            results.setdefault(name, [None] * 4)[i] = val

    loss = lax.psum(loss, ("x", "y", "c"))
    outs = [loss, dx[None]]
    for i in range(4):
        outs.extend(results[name][i] for name in PARAM_NAMES)
    return tuple(outs)
```

```python
import functools

import numpy as np
import jax
import jax.numpy as jnp
from jax import lax
from jax.experimental import pallas as pl
from jax.experimental.pallas import tpu as pltpu

F32 = jnp.float32
BF16 = jnp.bfloat16

N_DEV = 8
D_MODEL = 1024
RMS_EPS = 1e-6
NEG_INF = -1e30
LANES = 128
SUBLANES = 8
BLOCK = 128

MLA_HEADS = 8
MLA_NOPE = 64
MLA_ROPE = 32
MLA_V = 64
MLA_QK_DIM = MLA_NOPE + MLA_ROPE
MLA_Q_RANK = 384
MLA_KV_RANK = 256
ROPE_THETA = 10000.0
DIL_PAIRS = ((128, 1), (512, 4), (2048, 16))
DIL_HEADS_PER_GROUP = 4
DIL_HEADS = 12
MEM_HEADS = 4
D_FF = 2816
OFF_Q = MLA_Q_RANK
OFF_KV = OFF_Q + MLA_KV_RANK
OFF_KR = OFF_KV + MLA_ROPE
OFF_DIL = OFF_KR + 3 * DIL_HEADS * LANES
OFF_MEMQ = OFF_DIL + MEM_HEADS * LANES
D_IN = OFF_MEMQ + 3 * D_MODEL
N_A = OFF_KV + LANES
N_DM = 3 * DIL_HEADS * LANES + MEM_HEADS * LANES

ADAM_LR = 0.001
ADAM_B1 = 0.9
ADAM_B2 = 0.999
ADAM_EPS = 1e-08
ADAM_WD = 0.01
ADAM_STEP = 10

VMEM_LIMIT = 48 * 1024 * 1024
VMEM_LIMIT_BIG = 58 * 1024 * 1024
PACK_W = 1024

_ALIBI_BASE = np.exp2(-8.0 * np.arange(1, DIL_HEADS + 1) / DIL_HEADS)
DIL_SLOPES = [[float(_ALIBI_BASE[hh * 3 + g]) for hh in range(DIL_HEADS_PER_GROUP)] for g in range(3)]

PARAMS = (
    ("g_pre_mix", (1024,), None), ("w_in", (1024, D_IN), 1), ("b_gate", (3072,), None),
    ("mla_q_norm", (384,), None), ("w_uq", (384, 768), 1), ("mla_kv_norm", (256,), None),
    ("w_ukv", (256, 1024), 1), ("g_mem", (1024,), None), ("w_mem_kv", (1024, 1024), 0),
    ("w_br_mla", (512, 1024), 1), ("w_br_dil", (512, 1024), 1), ("w_br_mem", (512, 1024), 1),
    ("w_o", (1024, 1024), 0), ("g_post_mix", (1024,), None), ("g_pre_ffn", (1024,), None),
    ("w_ffn_up", (1024, 2 * D_FF), 1), ("conv_w", (3, 2 * D_FF), 1), ("conv_b", (2 * D_FF,), None),
    ("w_ffn_down", (D_FF, 1024), 0), ("g_post_ffn", (1024,), None),
)
PARAM_NAMES = tuple(p[0] for p in PARAMS)


def _shard_shape(shape, axis):
    if axis is None:
        return shape
    return tuple(s // N_DEV if a == axis else s for a, s in enumerate(shape))


SHARDED = tuple(p for p in PARAMS if p[2] is not None)
REPLICATED = tuple(p for p in PARAMS if p[2] is None)


def _layout():
    off, table = 0, {}
    for name, shape, _ in REPLICATED:
        table[name] = (off, shape[0])
        off += shape[0]
    rows = -(-off // PACK_W)
    rows = -(-rows // SUBLANES) * SUBLANES
    return table, off, rows


PACK_TABLE, PACK_USED, PACK_ROWS = _layout()


def _pick(n, cap):
    best = None
    for t in range(LANES, min(n, cap) + 1, LANES):
        if n % t == 0:
            best = t
    return best if best is not None else n


def _rows(n, cap, mult=SUBLANES):
    best = None
    for t in range(mult, min(n, cap) + 1, mult):
        if n % t == 0:
            best = t
    return best if best is not None else n


def _cparams(sem, vmem=VMEM_LIMIT):
    return pltpu.CompilerParams(dimension_semantics=sem, vmem_limit_bytes=vmem)


def _matmul(a, b, *, name, out_dtype=F32, trans_a=False, add=None, tm=1024, tn=1408, tk=640):
    if trans_a:
        kc, m = a.shape
    else:
        m, kc = a.shape
    n = b.shape[1]
    assert b.shape[0] == kc
    tm, tn, tk = _pick(m, tm), _pick(n, tn), _pick(kc, tk)
    nk = kc // tk

    def body(*refs):
        if add is None:
            a_ref, b_ref, o_ref, acc = refs
        else:
            a_ref, b_ref, c_ref, o_ref, acc = refs
        k = pl.program_id(2)

        @pl.when(k == 0)
        def _():
            if add is None:
                acc[...] = jnp.zeros_like(acc)
            else:
                acc[...] = c_ref[...].astype(F32)

        av = a_ref[...].astype(BF16)
        bv = b_ref[...].astype(BF16)
        if trans_a:
            acc[...] += lax.dot_general(av, bv, (((0,), (0,)), ((), ())), preferred_element_type=F32)
        else:
            acc[...] += jnp.dot(av, bv, preferred_element_type=F32)

        @pl.when(k == nk - 1)
        def _():
            o_ref[...] = acc[...].astype(out_dtype)

    if trans_a:
        a_spec = pl.BlockSpec((tk, tm), lambda i, j, k: (k, i))
    else:
        a_spec = pl.BlockSpec((tm, tk), lambda i, j, k: (i, k))
    in_specs = [a_spec, pl.BlockSpec((tk, tn), lambda i, j, k: (k, j))]
    args = [a, b]
    if add is not None:
        in_specs.append(pl.BlockSpec((tm, tn), lambda i, j, k: (i, j)))
        args.append(add)
    return pl.pallas_call(
        body, name=name, grid=(m // tm, n // tn, nk),
        in_specs=in_specs, out_specs=pl.BlockSpec((tm, tn), lambda i, j, k: (i, j)),
        out_shape=jax.ShapeDtypeStruct((m, n), out_dtype),
        scratch_shapes=[pltpu.VMEM((tm, tn), F32)],
        compiler_params=_cparams(("parallel", "parallel", "arbitrary")),
    )(*args)


def _rms_fwd(x, g, *, name, out_dtype, add=None):
    s, n = x.shape
    ts = _rows(s, 512)

    def body(*refs):
        if add is None:
            x_ref, g_ref, o_ref = refs
        else:
            x_ref, g_ref, a_ref, o_ref = refs
        xv = x_ref[...]
        r = lax.rsqrt(jnp.mean(xv * xv, axis=-1, keepdims=True) + RMS_EPS)
        y = xv * r * g_ref[...]
        if add is not None:
            y = a_ref[...] + y
        o_ref[...] = y.astype(out_dtype)

    row = pl.BlockSpec((ts, n), lambda i: (i, 0))
    in_specs = [row, pl.BlockSpec((1, n), lambda i: (0, 0))]
    args = [x, g]
    if add is not None:
        in_specs.append(row)
        args.append(add)
    return pl.pallas_call(
        body, name=name, grid=(s // ts,), in_specs=in_specs, out_specs=row,
        out_shape=jax.ShapeDtypeStruct((s, n), out_dtype),
        compiler_params=_cparams(("parallel",)),
    )(*args)


def _rms_bwd(x, g, dy, *, name, out_dtype, add=None):
    s, n = x.shape
    ts = _rows(s, 512)

    def body(*refs):
        if add is None:
            x_ref, g_ref, dy_ref, dx_ref, dg_ref = refs
        else:
            x_ref, g_ref, dy_ref, a_ref, dx_ref, dg_ref = refs
        i = pl.program_id(0)
        xv = x_ref[...]
        dyv = dy_ref[...].astype(F32)
        r = lax.rsqrt(jnp.mean(xv * xv, axis=-1, keepdims=True) + RMS_EPS)
        nx = xv * r
        gdy = dyv * g_ref[...]
        dx = r * (gdy - nx * jnp.mean(nx * gdy, axis=-1, keepdims=True))
        if add is not None:
            dx = a_ref[...] + dx
        dx_ref[...] = dx.astype(out_dtype)

        @pl.when(i == 0)
        def _():
            dg_ref[...] = jnp.zeros_like(dg_ref)

        dg_ref[...] += jnp.sum(dyv * nx, axis=0, keepdims=True)

    row = pl.BlockSpec((ts, n), lambda i: (i, 0))
    vec = pl.BlockSpec((1, n), lambda i: (0, 0))
    in_specs = [row, vec, row]
    args = [x, g, dy]
    if add is not None:
        in_specs.append(row)
        args.append(add)
    return pl.pallas_call(
        body, name=name, grid=(s // ts,), in_specs=in_specs, out_specs=[row, vec],
        out_shape=[jax.ShapeDtypeStruct((s, n), out_dtype), jax.ShapeDtypeStruct((1, n), F32)],
        compiler_params=_cparams(("arbitrary",)),
    )(*args)


def _loss_fwd(x1, f, target, g, *, name):
    s, n = x1.shape
    ts = _rows(s, 512)

    def body(x_ref, f_ref, t_ref, g_ref, dy_ref, sq_ref):
        i = pl.program_id(0)
        fv = f_ref[...]
        r = lax.rsqrt(jnp.mean(fv * fv, axis=-1, keepdims=True) + RMS_EPS)
        err = x_ref[...] + fv * r * g_ref[...] - t_ref[...]
        dy_ref[...] = err * (1.0 / n)

        @pl.when(i == 0)
        def _():
            sq_ref[...] = jnp.zeros_like(sq_ref)

        sq_ref[...] += jnp.sum(err * err, axis=0, keepdims=True)

    row = pl.BlockSpec((ts, n), lambda i: (i, 0))
    vec = pl.BlockSpec((1, n), lambda i: (0, 0))
    return pl.pallas_call(
        body, name=name, grid=(s // ts,), in_specs=[row, row, row, vec], out_specs=[row, vec],
        out_shape=[jax.ShapeDtypeStruct((s, n), F32), jax.ShapeDtypeStruct((1, n), F32)],
        compiler_params=_cparams(("arbitrary",)),
    )(x1, f, target, g)


def _rope_tables(positions):
    half = MLA_ROPE // 2
    inv_freq = ROPE_THETA ** (-jnp.arange(half, dtype=F32) / half)
    ang = positions.astype(F32)[:, None] * inv_freq[None, :]
    cos, sin = jnp.cos(ang), jnp.sin(ang)
    s = positions.shape[0]
    one = jnp.ones((s, MLA_NOPE), F32)
    zero = jnp.zeros((s, MLA_NOPE), F32)
    pad1 = jnp.ones((s, LANES - MLA_QK_DIM), F32)
    pad0 = jnp.zeros((s, LANES - MLA_QK_DIM), F32)
    zh = jnp.zeros((s, half), F32)
    c_tab = jnp.concatenate([one, cos, cos, pad1], axis=1)
    s1_tab = jnp.concatenate([zero, -sin, zh, pad0], axis=1)
    s2_tab = jnp.concatenate([zero, zh, sin, pad0], axis=1)
    return c_tab, s1_tab, s2_tab


def _rope_fwd(x, tabs, *, name, scale, add=None):
    s, n = x.shape
    nh = n // LANES
    ts = _rows(s, 512)
    half = MLA_ROPE // 2

    def body(*refs):
        if add is None:
            x_ref, c_ref, s1_ref, s2_ref, o_ref = refs
        else:
            x_ref, a_ref, c_ref, s1_ref, s2_ref, o_ref = refs
        c, s1, s2 = c_ref[...], s1_ref[...], s2_ref[...]
        for h in range(nh):
            xh = x_ref[:, h * LANES:(h + 1) * LANES]
            if add is not None:
                xh = xh + a_ref[...]
            y = xh * c + pltpu.roll(xh, LANES - half, 1) * s1 + pltpu.roll(xh, half, 1) * s2
            o_ref[:, h * LANES:(h + 1) * LANES] = (y * scale).astype(BF16)

    row = pl.BlockSpec((ts, n), lambda i: (i, 0))
    tab = pl.BlockSpec((ts, LANES), lambda i: (i, 0))
    in_specs = [row] + ([tab] if add is not None else []) + [tab, tab, tab]
    args = [x] + ([add] if add is not None else []) + list(tabs)
    return pl.pallas_call(
        body, name=name, grid=(s // ts,), in_specs=in_specs, out_specs=row,
        out_shape=jax.ShapeDtypeStruct((s, n), BF16),
        compiler_params=_cparams(("parallel",)),
    )(*args)


def _rope_bwd(dy, tabs, *, name, scale, with_add):
    s, n = dy.shape
    nh = n // LANES
    ts = _rows(s, 512)
    half = MLA_ROPE // 2

    def body(*refs):
        if with_add:
            dy_ref, c_ref, s1_ref, s2_ref, dx_ref, da_ref = refs
        else:
            dy_ref, c_ref, s1_ref, s2_ref, dx_ref = refs
        c, s1, s2 = c_ref[...], s1_ref[...], s2_ref[...]
        tot = None
        for h in range(nh):
            g = dy_ref[:, h * LANES:(h + 1) * LANES].astype(F32)
            dx = (g * c + pltpu.roll(g * s1, half, 1) + pltpu.roll(g * s2, LANES - half, 1)) * scale
            dx_ref[:, h * LANES:(h + 1) * LANES] = dx.astype(BF16)
            tot = dx if tot is None else tot + dx
        if with_add:
            da_ref[...] = tot

    row = pl.BlockSpec((ts, n), lambda i: (i, 0))
    tab = pl.BlockSpec((ts, LANES), lambda i: (i, 0))
    out_specs = [row, tab] if with_add else row
    out_shape = [jax.ShapeDtypeStruct((s, n), BF16)]
    if with_add:
        out_shape.append(jax.ShapeDtypeStruct((s, LANES), F32))
    else:
        out_shape = out_shape[0]
    return pl.pallas_call(
        body, name=name, grid=(s // ts,), in_specs=[row, tab, tab, tab], out_specs=out_specs,
        out_shape=out_shape, compiler_params=_cparams(("parallel",)),
    )(dy, *tabs)


def _scores(q, k, scale, diag):
    s = lax.dot_general(q, k, (((1,), (1,)), ((), ())), preferred_element_type=F32)
    if scale != 1.0:
        s = s * scale
    if diag:
        rows = lax.broadcasted_iota(jnp.int32, s.shape, 0)
        cols = lax.broadcasted_iota(jnp.int32, s.shape, 1)
        s = jnp.where(cols <= rows, s, NEG_INF)
    return s


def _flash_fwd(q, k, v, *, name, heads, qoff, koff, voff, causal, scale, tq, tk):
    s_q, s_kv = q.shape[0], k.shape[0]
    tq, tk = min(tq, s_q), min(tk, s_kv)
    nq, nk = s_q // tq, s_kv // tk
    if causal:
        assert tq == tk and s_q == s_kv

    def body(q_ref, k_ref, v_ref, o_ref, lse_ref, m_s, l_s, acc):
        i, j = pl.program_id(1), pl.program_id(2)

        @pl.when(j == 0)
        def _():
            m_s[...] = jnp.full_like(m_s, NEG_INF)
            l_s[...] = jnp.zeros_like(l_s)
            acc[...] = jnp.zeros_like(acc)

        def step(diag):
            s = _scores(q_ref[...], k_ref[...], scale, diag)
            m_prev = m_s[...]
            m_cur = jnp.maximum(m_prev, jnp.max(s, axis=1, keepdims=True))
            alpha = jnp.exp(m_prev - m_cur)
            p = jnp.exp(s - m_cur[:, :1])
            l_s[...] = alpha * l_s[...] + jnp.sum(p, axis=1, keepdims=True)
            acc[...] = alpha * acc[...] + jnp.dot(p.astype(BF16), v_ref[...], preferred_element_type=F32)
            m_s[...] = m_cur

        def finish():
            o_ref[...] = (acc[...] / l_s[...]).astype(o_ref.dtype)
            lse_ref[...] = m_s[...] + jnp.log(l_s[...])

        if causal:
            @pl.when(j < i)
            def _():
                step(False)

            @pl.when(j == i)
            def _():
                step(True)
                finish()
        else:
            step(False)

            @pl.when(j == nk - 1)
            def _():
                finish()

    def kv_idx(off):
        if causal:
            return lambda h, i, j: (jnp.minimum(j, i), off + h)
        return lambda h, i, j: (j, off + h)

    blk_q = pl.BlockSpec((tq, LANES), lambda h, i, j: (i, qoff + h))
    out_q = pl.BlockSpec((tq, LANES), lambda h, i, j: (i, h))
    return pl.pallas_call(
        body, name=name, grid=(heads, nq, nk),
        in_specs=[blk_q, pl.BlockSpec((tk, LANES), kv_idx(koff)), pl.BlockSpec((tk, LANES), kv_idx(voff))],
        out_specs=[out_q, out_q],
        out_shape=[jax.ShapeDtypeStruct((s_q, heads * LANES), BF16),
                   jax.ShapeDtypeStruct((s_q, heads * LANES), F32)],
        scratch_shapes=[pltpu.VMEM((tq, LANES), F32)] * 3,
        compiler_params=_cparams(("parallel", "parallel", "arbitrary")),
    )(q, k, v)


def _flash_bwd_dq(q, k, v, do, lse, delta, *, name, heads, qoff, koff, voff, causal, scale, tq, tk, out_dtype):
    s_q, s_kv = q.shape[0], k.shape[0]
    tq, tk = min(tq, s_q), min(tk, s_kv)
    nq, nk = s_q // tq, s_kv // tk

    def body(q_ref, k_ref, v_ref, do_ref, lse_ref, dl_ref, dq_ref, acc):
        i, j = pl.program_id(1), pl.program_id(2)

        @pl.when(j == 0)
        def _():
            acc[...] = jnp.zeros_like(acc)

        def step(diag):
            s = _scores(q_ref[...], k_ref[...], scale, diag)
            p = jnp.exp(s - lse_ref[:, :1])
            dp = lax.dot_general(do_ref[...], v_ref[...], (((1,), (1,)), ((), ())), preferred_element_type=F32)
            ds = p * (dp - dl_ref[:, :1])
            acc[...] += jnp.dot(ds.astype(BF16), k_ref[...], preferred_element_type=F32)

        def finish():
            dq_ref[...] = (acc[...] * scale).astype(out_dtype)

        if causal:
            @pl.when(j < i)
            def _():
                step(False)

            @pl.when(j == i)
            def _():
                step(True)
                finish()
        else:
            step(False)

            @pl.when(j == nk - 1)
            def _():
                finish()

    def kv_idx(off):
        if causal:
            return lambda h, i, j: (jnp.minimum(j, i), off + h)
        return lambda h, i, j: (j, off + h)

    blk_q = pl.BlockSpec((tq, LANES), lambda h, i, j: (i, qoff + h))
    blk_h = pl.BlockSpec((tq, LANES), lambda h, i, j: (i, h))
    return pl.pallas_call(
        body, name=name, grid=(heads, nq, nk),
        in_specs=[blk_q, pl.BlockSpec((tk, LANES), kv_idx(koff)), pl.BlockSpec((tk, LANES), kv_idx(voff)),
                  blk_h, blk_h, blk_h],
        out_specs=blk_h,
        out_shape=jax.ShapeDtypeStruct((s_q, heads * LANES), out_dtype),
        scratch_shapes=[pltpu.VMEM((tq, LANES), F32)],
        compiler_params=_cparams(("parallel", "parallel", "arbitrary")),
    )(q, k, v, do, lse, delta)


def _flash_bwd_dkv(q, k, v, do, lse, delta, *, name, heads, qoff, koff, voff, causal, scale, tq, tk,
                   dk_dtype, dv_dtype):
    s_q, s_kv = q.shape[0], k.shape[0]
    tq, tk = min(tq, s_q), min(tk, s_kv)
    nq, nk = s_q // tq, s_kv // tk

    def body(q_ref, k_ref, v_ref, do_ref, lse_ref, dl_ref, dk_ref, dv_ref, dk_acc, dv_acc):
        j, i = pl.program_id(1), pl.program_id(2)

        @pl.when(i == 0)
        def _():
            dk_acc[...] = jnp.zeros_like(dk_acc)
            dv_acc[...] = jnp.zeros_like(dv_acc)

        def step(diag):
            s = _scores(q_ref[...], k_ref[...], scale, diag)
            p = jnp.exp(s - lse_ref[:, :1])
            dov = do_ref[...]
            dp = lax.dot_general(dov, v_ref[...], (((1,), (1,)), ((), ())), preferred_element_type=F32)
            ds = p * (dp - dl_ref[:, :1])
            dv_acc[...] += lax.dot_general(p.astype(BF16), dov, (((0,), (0,)), ((), ())),
                                           preferred_element_type=F32)
            dk_acc[...] += lax.dot_general(ds.astype(BF16), q_ref[...], (((0,), (0,)), ((), ())),
                                           preferred_element_type=F32)

        if causal:
            @pl.when(i > j)
            def _():
                step(False)

            @pl.when(i == j)
            def _():
                step(True)
        else:
            step(False)

        @pl.when(i == nq - 1)
        def _():
            dk_ref[...] = (dk_acc[...] * scale).astype(dk_dtype)
            dv_ref[...] = dv_acc[...].astype(dv_dtype)

    def q_idx(off):
        if causal:
            return lambda h, j, i: (jnp.maximum(i, j), off + h)
        return lambda h, j, i: (i, off + h)

    blk_h = pl.BlockSpec((tq, LANES), q_idx(0))
    out_k = pl.BlockSpec((tk, LANES), lambda h, j, i: (j, h))
    return pl.pallas_call(
        body, name=name, grid=(heads, nk, nq),
        in_specs=[pl.BlockSpec((tq, LANES), q_idx(qoff)),
                  pl.BlockSpec((tk, LANES), lambda h, j, i: (j, koff + h)),
                  pl.BlockSpec((tk, LANES), lambda h, j, i: (j, voff + h)),
                  blk_h, blk_h, blk_h],
        out_specs=[out_k, out_k],
        out_shape=[jax.ShapeDtypeStruct((s_kv, heads * LANES), dk_dtype),
                   jax.ShapeDtypeStruct((s_kv, heads * LANES), dv_dtype)],
        scratch_shapes=[pltpu.VMEM((tk, LANES), F32)] * 2,
        compiler_params=_cparams(("parallel", "parallel", "arbitrary")),
    )(q, k, v, do, lse, delta)


def _row_dot(a, b, *, name):
    s, n = a.shape
    nh = n // LANES
    ts = _rows(s, 512)

    def body(a_ref, b_ref, o_ref):
        for h in range(nh):
            sl = slice(h * LANES, (h + 1) * LANES)
            d = jnp.sum(a_ref[:, sl].astype(F32) * b_ref[:, sl].astype(F32), axis=1, keepdims=True)
            o_ref[:, sl] = jnp.broadcast_to(d, (ts, LANES))

    row = pl.BlockSpec((ts, n), lambda i: (i, 0))
    return pl.pallas_call(
        body, name=name, grid=(s // ts,), in_specs=[row, row], out_specs=row,
        out_shape=jax.ShapeDtypeStruct((s, n), F32), compiler_params=_cparams(("parallel",)),
    )(a, b)


CAUSAL_T = 512
LOG2E = 1.4426950408889634
LN2 = 0.6931471805599453


def _causal_fwd(q, k, v, *, name, heads, ones_lane):
    s = q.shape[0]
    t = min(CAUSAL_T, s)
    nq = s // t
    wide = 2 * t

    def body(q_ref, k_ref, v_ref, o_ref, lse_ref, v1, m_s, acc):
        i = pl.program_id(1)

        @pl.when(i == 0)
        def _():
            lane = lax.broadcasted_iota(jnp.int32, v1.shape, 1)
            v1[...] = jnp.where(lane == ones_lane, 1.0, v_ref[...]).astype(BF16)

        m_s[...] = jnp.full_like(m_s, NEG_INF)
        acc[...] = jnp.zeros_like(acc)
        qv = q_ref[...]

        def step(start, size, diag):
            sc = _scores(qv, k_ref[pl.ds(start, size), :], 1.0, diag)
            m_prev = m_s[...]
            m_cur = jnp.maximum(m_prev, jnp.max(sc, axis=1, keepdims=True))
            p = jnp.exp2(sc - m_cur[:, :1]).astype(BF16)
            acc[...] = jnp.exp2(m_prev - m_cur) * acc[...] + jnp.dot(p, v1[pl.ds(start, size), :],
                                                                      preferred_element_type=F32)
            m_s[...] = m_cur

        def pair(jj, carry):
            step(pl.multiple_of(jj * wide, wide), wide, False)
            return carry

        lax.fori_loop(0, i // 2, pair, 0)

        @pl.when(i % 2 == 1)
        def _():
            step(pl.multiple_of((i - 1) * t, t), t, False)

        step(pl.multiple_of(i * t, t), t, True)
        out = acc[...]
        den = out[:, ones_lane:ones_lane + 1]
        lane = lax.broadcasted_iota(jnp.int32, out.shape, 1)
        o_ref[...] = jnp.where(lane == ones_lane, 0.0, out / den).astype(BF16)
        lse_ref[...] = m_s[...] + jnp.log2(den)

    blk = pl.BlockSpec((t, LANES), lambda h, i: (i, h))
    full = pl.BlockSpec((s, LANES), lambda h, i: (0, h))
    return pl.pallas_call(
        body, name=name, grid=(heads, nq), in_specs=[blk, full, full], out_specs=[blk, blk],
        out_shape=[jax.ShapeDtypeStruct((s, heads * LANES), BF16), jax.ShapeDtypeStruct((s, heads * LANES), F32)],
        scratch_shapes=[pltpu.VMEM((s, LANES), BF16), pltpu.VMEM((t, LANES), F32), pltpu.VMEM((t, LANES), F32)],
        compiler_params=_cparams(("parallel", "arbitrary")),
    )(q, k, v)


def _causal_bwd(q, k, v, do, lse, delta, *, name, heads):
    s = q.shape[0]
    t = min(CAUSAL_T, s)
    nt = s // t

    def body(q_ref, k_ref, v_ref, do_ref, lse_ref, dl_ref, dq_ref, dk_ref, dv_ref, dk_acc, dv_acc):
        j = pl.program_id(1)

        @pl.when(j == 0)
        def _():
            dq_ref[...] = jnp.zeros_like(dq_ref)

        dk_acc[...] = jnp.zeros_like(dk_acc)
        dv_acc[...] = jnp.zeros_like(dv_acc)
        kv, vv = k_ref[...], v_ref[...]

        def step(i, diag):
            rows = pl.ds(pl.multiple_of(i * t, t), t)
            qv, dov = q_ref[rows, :], do_ref[rows, :]
            sc = _scores(qv, kv, 1.0, diag)
            p = jnp.exp2(sc - lse_ref[rows, :][:, :1])
            dp = lax.dot_general(dov, vv, (((1,), (1,)), ((), ())), preferred_element_type=F32)
            ds = (p * (dp - dl_ref[rows, :][:, :1])).astype(BF16)
            dv_acc[...] += lax.dot_general(p.astype(BF16), dov, (((0,), (0,)), ((), ())),
                                           preferred_element_type=F32)
            dk_acc[...] += lax.dot_general(ds, qv, (((0,), (0,)), ((), ())), preferred_element_type=F32)
            dq_ref[rows, :] += jnp.dot(ds, kv, preferred_element_type=F32)

        step(j, True)

        def loop(i, carry):
            step(i, False)
            return carry

        lax.fori_loop(j + 1, nt, loop, 0)
        dk_ref[...] = dk_acc[...] * LN2
        dv_ref[...] = dv_acc[...].astype(BF16)

    blk = pl.BlockSpec((t, LANES), lambda h, j: (j, h))
    full = pl.BlockSpec((s, LANES), lambda h, j: (0, h))
    return pl.pallas_call(
        body, name=name, grid=(heads, nt), in_specs=[full, blk, blk, full, full, full],
        out_specs=[full, blk, blk],
        out_shape=[jax.ShapeDtypeStruct((s, heads * LANES), F32), jax.ShapeDtypeStruct((s, heads * LANES), F32),
                   jax.ShapeDtypeStruct((s, heads * LANES), BF16)],
        scratch_shapes=[pltpu.VMEM((t, LANES), F32)] * 2,
        compiler_params=_cparams(("parallel", "arbitrary")),
    )(q, k, v, do, lse, delta)


def _band_masks(dilation, slope):
    qi = lax.broadcasted_iota(jnp.int32, (BLOCK, 2 * BLOCK), 0)
    kj = lax.broadcasted_iota(jnp.int32, (BLOCK, 2 * BLOCK), 1)
    dist = qi + BLOCK - kj
    valid = (dist >= 0) & (dist <= BLOCK)
    bias = -slope * (dist * dilation).astype(F32)
    return valid, bias


BAND_UNROLL = 4


def _aligned(start):
    return start if isinstance(start, int) else pl.multiple_of(start, BLOCK)


def _band_fwd(q, k, v, slopes, *, name, dilation, qoff, koff, voff):
    s = q.shape[0]
    sub = s // dilation
    nb = sub // BLOCK
    assert nb * BLOCK == sub
    unroll = min(BAND_UNROLL, nb)
    assert nb % unroll == 0
    scale = LANES ** -0.5

    def body(sl_ref, q_ref, k_ref, v_ref, o_ref, lse_ref):
        slope = sl_ref[pl.program_id(0)]
        valid2, bias2 = _band_masks(dilation, slope)
        valid1, bias1 = valid2[:, BLOCK:], bias2[:, BLOCK:]

        def block(start_q, kk, vv, valid, bias):
            qb = q_ref[pl.ds(start_q, BLOCK), :]
            sc = lax.dot_general(qb, kk, (((1,), (1,)), ((), ())), preferred_element_type=F32) * scale
            sc = jnp.where(valid, sc + bias, NEG_INF)
            m = jnp.max(sc, axis=1, keepdims=True)
            e = jnp.exp(sc - m)
            den = jnp.sum(e, axis=1, keepdims=True)
            p = (e / den).astype(BF16)
            o_ref[pl.ds(start_q, BLOCK), :] = jnp.dot(p, vv, preferred_element_type=F32)
            lse_ref[pl.ds(start_q, BLOCK), :] = jnp.broadcast_to(m + jnp.log(den), (BLOCK, LANES))

        block(0, k_ref[0:BLOCK, :], v_ref[0:BLOCK, :], valid1, bias1)

        def general(jj):
            start_q, start_k = _aligned(jj * BLOCK), _aligned((jj - 1) * BLOCK)
            block(start_q, k_ref[pl.ds(start_k, 2 * BLOCK), :], v_ref[pl.ds(start_k, 2 * BLOCK), :], valid2, bias2)

        for jj in range(1, unroll):
            general(jj)

        def loop(t, carry):
            for u in range(unroll):
                general(t * unroll + u)
            return carry

        lax.fori_loop(1, nb // unroll, loop, 0)

    def spec(off):
        return pl.BlockSpec((sub, LANES), lambda h, r: (r, off + h))

    out = pl.BlockSpec((sub, LANES), lambda h, r: (r, h))
    return pl.pallas_call(
        body, name=name, grid=(DIL_HEADS_PER_GROUP, dilation),
        in_specs=[pl.BlockSpec(memory_space=pltpu.SMEM), spec(qoff), spec(koff), spec(voff)],
        out_specs=[out, out],
        out_shape=[jax.ShapeDtypeStruct((s, DIL_HEADS_PER_GROUP * LANES), F32)] * 2,
        compiler_params=_cparams(("parallel", "parallel"), VMEM_LIMIT_BIG),
    )(slopes, q, k, v)


def _band_bwd(q, k, v, do, lse, delta, slopes, *, name, dilation, qoff, koff, voff):
    s = q.shape[0]
    sub = s // dilation
    nb = sub // BLOCK
    unroll = min(BAND_UNROLL, nb)
    scale = LANES ** -0.5

    def body(sl_ref, q_ref, k_ref, v_ref, do_ref, lse_ref, dl_ref, dq_ref, dk_ref, dv_ref):
        slope = sl_ref[pl.program_id(0)]
        valid2, bias2 = _band_masks(dilation, slope)
        valid1, bias1 = valid2[:, BLOCK:], bias2[:, BLOCK:]

        def block(start_q, kk, vv, valid, bias):
            qb = q_ref[pl.ds(start_q, BLOCK), :]
            dob = do_ref[pl.ds(start_q, BLOCK), :]
            sc = lax.dot_general(qb, kk, (((1,), (1,)), ((), ())), preferred_element_type=F32) * scale
            sc = jnp.where(valid, sc + bias, NEG_INF)
            p = jnp.exp(sc - lse_ref[pl.ds(start_q, BLOCK), :][:, :1])
            dp = lax.dot_general(dob, vv, (((1,), (1,)), ((), ())), preferred_element_type=F32)
            ds = (p * (dp - dl_ref[pl.ds(start_q, BLOCK), :][:, :1])).astype(BF16)
            dq = jnp.dot(ds, kk, preferred_element_type=F32) * scale
            dq_ref[pl.ds(start_q, BLOCK), :] = dq.astype(BF16)
            dkk = lax.dot_general(ds, qb, (((0,), (0,)), ((), ())), preferred_element_type=F32) * scale
            dvv = lax.dot_general(p.astype(BF16), dob, (((0,), (0,)), ((), ())), preferred_element_type=F32)
            return dkk, dvv

        carry0 = block(0, k_ref[0:BLOCK, :], v_ref[0:BLOCK, :], valid1, bias1)

        def general(jj, carry):
            dk_part, dv_part = carry
            start_q, start_k = _aligned(jj * BLOCK), _aligned((jj - 1) * BLOCK)
            dkk, dvv = block(start_q, k_ref[pl.ds(start_k, 2 * BLOCK), :], v_ref[pl.ds(start_k, 2 * BLOCK), :],
                             valid2, bias2)
            dk_ref[pl.ds(start_k, BLOCK), :] = (dk_part + dkk[:BLOCK]).astype(BF16)
            dv_ref[pl.ds(start_k, BLOCK), :] = (dv_part + dvv[:BLOCK]).astype(BF16)
            return dkk[BLOCK:], dvv[BLOCK:]

        for jj in range(1, unroll):
            carry0 = general(jj, carry0)

        def loop(t, carry):
            for u in range(unroll):
                carry = general(t * unroll + u, carry)
            return carry

        dk_last, dv_last = lax.fori_loop(1, nb // unroll, loop, carry0)
        dk_ref[(nb - 1) * BLOCK:nb * BLOCK, :] = dk_last.astype(BF16)
        dv_ref[(nb - 1) * BLOCK:nb * BLOCK, :] = dv_last.astype(BF16)

    def spec(off):
        return pl.BlockSpec((sub, LANES), lambda h, r: (r, off + h))

    out = spec(0)
    return pl.pallas_call(
        body, name=name, grid=(DIL_HEADS_PER_GROUP, dilation),
        in_specs=[pl.BlockSpec(memory_space=pltpu.SMEM), spec(qoff), spec(koff), spec(voff), out, out, out],
        out_specs=[out, out, out],
        out_shape=[jax.ShapeDtypeStruct((s, DIL_HEADS_PER_GROUP * LANES), BF16)] * 3,
        compiler_params=_cparams(("parallel", "parallel"), VMEM_LIMIT_BIG),
    )(slopes, q, k, v, do, lse, delta)


def _mix_fwd(outs, lses, *, name):
    s, n = outs[0].shape
    ts = _rows(s, 512)

    def body(o0, o1, o2, l0, l1, l2, y_ref):
        la, lb, lc = l0[...], l1[...], l2[...]
        m = jnp.maximum(jnp.maximum(la, lb), lc)
        ea, eb, ec = jnp.exp(la - m), jnp.exp(lb - m), jnp.exp(lc - m)
        den = ea + eb + ec
        y = (ea / den) * o0[...] + (eb / den) * o1[...] + (ec / den) * o2[...]
        y_ref[...] = y.astype(BF16)

    row = pl.BlockSpec((ts, n), lambda i: (i, 0))
    return pl.pallas_call(
        body, name=name, grid=(s // ts,), in_specs=[row] * 6, out_specs=row,
        out_shape=jax.ShapeDtypeStruct((s, n), BF16), compiler_params=_cparams(("parallel",)),
    )(*outs, *lses)


def _mix_bwd(dy, outs, lses, *, name):
    s, n = dy.shape
    nh = n // LANES
    ts = _rows(s, 256)

    def body(dy_ref, o0, o1, o2, l0, l1, l2, d0, d1, d2, e0, e1, e2):
        la, lb, lc = l0[...], l1[...], l2[...]
        m = jnp.maximum(jnp.maximum(la, lb), lc)
        ea, eb, ec = jnp.exp(la - m), jnp.exp(lb - m), jnp.exp(lc - m)
        den = ea + eb + ec
        wa, wb, wc = ea / den, eb / den, ec / den
        dyv = dy_ref[...]
        y = wa * o0[...] + wb * o1[...] + wc * o2[...]
        prod = dyv * y
        d0[...] = (wa * dyv).astype(BF16)
        d1[...] = (wb * dyv).astype(BF16)
        d2[...] = (wc * dyv).astype(BF16)
        for h in range(nh):
            sl = slice(h * LANES, (h + 1) * LANES)
            t = jnp.sum(prod[:, sl], axis=1, keepdims=True)
            e0[:, sl] = wa[:, sl] * t
            e1[:, sl] = wb[:, sl] * t
            e2[:, sl] = wc[:, sl] * t

    row = pl.BlockSpec((ts, n), lambda i: (i, 0))
    return pl.pallas_call(
        body, name=name, grid=(s // ts,), in_specs=[row] * 7, out_specs=[row] * 6,
        out_shape=[jax.ShapeDtypeStruct((s, n), BF16)] * 3 + [jax.ShapeDtypeStruct((s, n), F32)] * 3,
        compiler_params=_cparams(("parallel",)),
    )(dy, *outs, *lses)


def _gate_fwd(gp, b_gate, branches, *, name):
    s = gp.shape[0]
    ts = _rows(s, 256)

    def body(gp_ref, b_ref, b0, b1, b2, o_ref):
        tot = None
        for i, br in enumerate((b0, b1, b2)):
            sl = slice(i * D_MODEL, (i + 1) * D_MODEL)
            t = jax.nn.sigmoid(gp_ref[:, sl] + b_ref[:, sl]) * br[...]
            tot = t if tot is None else tot + t
        o_ref[...] = tot.astype(BF16)

    row = pl.BlockSpec((ts, D_MODEL), lambda i: (i, 0))
    return pl.pallas_call(
        body, name=name, grid=(s // ts,),
        in_specs=[pl.BlockSpec((ts, 3 * D_MODEL), lambda i: (i, 0)), pl.BlockSpec((1, 3 * D_MODEL), lambda i: (0, 0)),
                  row, row, row],
        out_specs=row, out_shape=jax.ShapeDtypeStruct((s, D_MODEL), BF16),
        compiler_params=_cparams(("parallel",)),
    )(gp, b_gate, *branches)


def _gate_bwd(dm, gp, b_gate, branches, *, name):
    s = gp.shape[0]
    ts = _rows(s, 256)

    def body(dm_ref, gp_ref, b_ref, b0, b1, b2, d0, d1, d2, dgp_ref, db_ref):
        i = pl.program_id(0)

        @pl.when(i == 0)
        def _():
            db_ref[...] = jnp.zeros_like(db_ref)

        dmv = dm_ref[...]
        for k, (br, dbr) in enumerate(((b0, d0), (b1, d1), (b2, d2))):
            sl = slice(k * D_MODEL, (k + 1) * D_MODEL)
            sg = jax.nn.sigmoid(gp_ref[:, sl] + b_ref[:, sl])
            dbr[...] = (dmv * sg).astype(BF16)
            dg = dmv * br[...] * sg * (1.0 - sg)
            dgp_ref[:, sl] = dg.astype(BF16)
            db_ref[:, sl] += jnp.sum(dg, axis=0, keepdims=True)

    row = pl.BlockSpec((ts, D_MODEL), lambda i: (i, 0))
    wide = pl.BlockSpec((ts, 3 * D_MODEL), lambda i: (i, 0))
    vec = pl.BlockSpec((1, 3 * D_MODEL), lambda i: (0, 0))
    return pl.pallas_call(
        body, name=name, grid=(s // ts,),
        in_specs=[row, wide, vec, row, row, row], out_specs=[row, row, row, wide, vec],
        out_shape=[jax.ShapeDtypeStruct((s, D_MODEL), BF16)] * 3
        + [jax.ShapeDtypeStruct((s, 3 * D_MODEL), BF16), jax.ShapeDtypeStruct((1, 3 * D_MODEL), F32)],
        compiler_params=_cparams(("arbitrary",)),
    )(dm, gp, b_gate, *branches)


CONV_TC = 1408


def _shift_down(x, halo, k):
    rolled = pltpu.roll(x, k, 0)
    r8 = lax.broadcasted_iota(jnp.int32, halo.shape, 0)
    top = jnp.where(r8 < k, pltpu.roll(halo, k, 0), rolled[:SUBLANES])
    return jnp.concatenate([top, rolled[SUBLANES:]], axis=0)


def _shift_up(x, halo, k):
    n = x.shape[0]
    rolled = pltpu.roll(x, n - k, 0)
    r8 = lax.broadcasted_iota(jnp.int32, halo.shape, 0)
    bot = jnp.where(r8 >= SUBLANES - k, pltpu.roll(halo, SUBLANES - k, 0), rolled[n - SUBLANES:])
    return jnp.concatenate([rolled[:n - SUBLANES], bot], axis=0)


def _conv_fwd(u, conv_w, conv_b, *, name):
    s = u.shape[0]
    ts = _rows(s, 256)
    nct = D_FF // CONV_TC
    per8 = ts // SUBLANES

    def body(ug, uv, hg, hv, wg, wv, bg, bv, zg_ref, zv_ref, a_ref):
        first = pl.program_id(1) == 0

        def conv(u_ref, h_ref, w_ref, b_ref):
            x = u_ref[...]
            halo = jnp.where(first, 0.0, h_ref[...])
            z = b_ref[...] + w_ref[0:1, :] * _shift_down(x, halo, 2)
            z = z + w_ref[1:2, :] * _shift_down(x, halo, 1)
            return z + w_ref[2:3, :] * x

        zg = conv(ug, hg, wg, bg)
        zv = conv(uv, hv, wv, bv)
        zg_ref[...] = zg
        zv_ref[...] = zv
        a_ref[...] = (zg * jax.nn.sigmoid(zg) * zv).astype(BF16)

    def col(off):
        return pl.BlockSpec((ts, CONV_TC), lambda c, i: (i, c + off))

    def halo(off):
        return pl.BlockSpec((SUBLANES, CONV_TC), lambda c, i: (jnp.maximum(i * per8 - 1, 0), c + off))

    def wspec(rows, off):
        return pl.BlockSpec((rows, CONV_TC), lambda c, i: (0, c + off))

    zg, zv, a = pl.pallas_call(
        body, name=name, grid=(nct, s // ts),
        in_specs=[col(0), col(nct), halo(0), halo(nct), wspec(3, 0), wspec(3, nct), wspec(1, 0), wspec(1, nct)],
        out_specs=[col(0), col(0), col(0)],
        out_shape=[jax.ShapeDtypeStruct((s, D_FF), F32)] * 2 + [jax.ShapeDtypeStruct((s, D_FF), BF16)],
        compiler_params=_cparams(("parallel", "parallel")),
    )(u, u, u, u, conv_w, conv_w, conv_b, conv_b)
    return zg, zv, a


def _silu_bwd(da, zg, zv, *, name):
    s = da.shape[0]
    ts = _rows(s, 256)
    nct = D_FF // CONV_TC

    def body(da_ref, zg_ref, zv_ref, dz_g, dz_v):
        g = zg_ref[...]
        sg = jax.nn.sigmoid(g)
        dav = da_ref[...]
        dz_g[...] = dav * zv_ref[...] * sg * (1.0 + g * (1.0 - sg))
        dz_v[...] = dav * g * sg

    col = pl.BlockSpec((ts, CONV_TC), lambda c, i: (i, c))
    return pl.pallas_call(
        body, name=name, grid=(nct, s // ts), in_specs=[col, col, col], out_specs=[col, col],
        out_shape=[jax.ShapeDtypeStruct((s, D_FF), F32)] * 2,
        compiler_params=_cparams(("parallel", "parallel")),
    )(da, zg, zv)


def _conv_bwd(dz, u, conv_w, *, name, off):
    s = dz.shape[0]
    ts = _rows(s, 256)
    nct = D_FF // CONV_TC
    per8 = ts // SUBLANES
    nrow = s // ts
    last8 = s // SUBLANES - 1

    def body(dz_ref, nx_ref, u_ref, pv_ref, w_ref, du_ref, acc_ref):
        i = pl.program_id(1)
        dzv = dz_ref[...]
        nxt = jnp.where(i == nrow - 1, 0.0, nx_ref[...])
        du = w_ref[2:3, :] * dzv + w_ref[1:2, :] * _shift_up(dzv, nxt, 1) + w_ref[0:1, :] * _shift_up(dzv, nxt, 2)
        du_ref[...] = du.astype(BF16)

        @pl.when(i == 0)
        def _():
            acc_ref[...] = jnp.zeros_like(acc_ref)

        x = u_ref[...]
        prev = jnp.where(i == 0, 0.0, pv_ref[...])
        acc_ref[0:1, :] += jnp.sum(dzv * _shift_down(x, prev, 2), axis=0, keepdims=True)
        acc_ref[1:2, :] += jnp.sum(dzv * _shift_down(x, prev, 1), axis=0, keepdims=True)
        acc_ref[2:3, :] += jnp.sum(dzv * x, axis=0, keepdims=True)
        acc_ref[3:4, :] += jnp.sum(dzv, axis=0, keepdims=True)

    return pl.pallas_call(
        body, name=name, grid=(nct, nrow),
        in_specs=[pl.BlockSpec((ts, CONV_TC), lambda c, i: (i, c)),
                  pl.BlockSpec((SUBLANES, CONV_TC), lambda c, i: (jnp.minimum((i + 1) * per8, last8), c)),
                  pl.BlockSpec((ts, CONV_TC), lambda c, i: (i, c + off)),
                  pl.BlockSpec((SUBLANES, CONV_TC), lambda c, i: (jnp.maximum(i * per8 - 1, 0), c + off)),
                  pl.BlockSpec((3, CONV_TC), lambda c, i: (0, c + off))],
        out_specs=[pl.BlockSpec((ts, CONV_TC), lambda c, i: (i, c)),
                   pl.BlockSpec((SUBLANES, CONV_TC), lambda c, i: (0, c))],
        out_shape=[jax.ShapeDtypeStruct((s, D_FF), BF16), jax.ShapeDtypeStruct((SUBLANES, D_FF), F32)],
        compiler_params=_cparams(("parallel", "arbitrary")),
    )(dz, dz, u, u, conv_w)


def _peer(k):
    x, y, c = lax.axis_index("x"), lax.axis_index("y"), lax.axis_index("c")
    px = 1 - x if k & 4 else x
    py = 1 - y if k & 2 else y
    pc = 1 - c if k & 1 else c
    return (px, py, pc), 4 * px + 2 * py + pc


def _exchange(bufs, *, name, gather):
    n = len(bufs)
    npeer = N_DEV - 1

    def body(*refs):
        srcs, outs = refs[:n], refs[n:2 * n]
        send_sems, recv_sems, local_sems = refs[2 * n:]
        _, me = _peer(0)
        mine = [src if gather else src.at[me] for src in srcs]
        local = [pltpu.make_async_copy(mine[p], outs[p].at[me], local_sems.at[p]) for p in range(n)]
        for cp in local:
            cp.start()
        sends = []
        for k in range(1, N_DEV):
            dev, idx = _peer(k)
            for p in range(n):
                cp = pltpu.make_async_remote_copy(
                    src_ref=srcs[p] if gather else srcs[p].at[idx], dst_ref=outs[p].at[me],
                    send_sem=send_sems.at[p * npeer + k - 1], recv_sem=recv_sems.at[p * npeer + k - 1],
                    device_id=dev, device_id_type=pl.DeviceIdType.MESH)
                cp.start()
                sends.append(cp)
        for k in range(1, N_DEV):
            dev, idx = _peer(k)
            for p in range(n):
                pltpu.make_async_remote_copy(
                    src_ref=mine[p], dst_ref=outs[p].at[idx],
                    send_sem=send_sems.at[p * npeer + k - 1], recv_sem=recv_sems.at[p * npeer + k - 1],
                    device_id=dev, device_id_type=pl.DeviceIdType.MESH).wait_recv()
        for cp in sends:
            cp.wait_send()
        for cp in local:
            cp.wait()

    any_spec = pl.BlockSpec(memory_space=pl.ANY)
    return pl.pallas_call(
        body, name=name,
        in_specs=[any_spec] * n, out_specs=[any_spec] * n,
        out_shape=[jax.ShapeDtypeStruct((N_DEV,) + b.shape[-2:], b.dtype) for b in bufs],
        scratch_shapes=[pltpu.SemaphoreType.DMA((n * npeer,)), pltpu.SemaphoreType.DMA((n * npeer,)),
                        pltpu.SemaphoreType.DMA((n,))],
    )(*bufs)


def _adamw(parts, w, m, v, *, name):
    rows, width = w.shape
    tr = _rows(rows, max(16, (128 * 1024) // width), mult=16)

    def body(p_ref, w_ref, m_ref, v_ref, g_ref, d_ref, nm_ref, nv_ref):
        g = p_ref[0].astype(F32)
        for k in range(1, N_DEV):
            g = g + p_ref[k].astype(F32)
        mn = ADAM_B1 * m_ref[...] + (1.0 - ADAM_B1) * g
        vn = ADAM_B2 * v_ref[...] + (1.0 - ADAM_B2) * jnp.square(g)
        m_hat = mn / (1.0 - ADAM_B1 ** ADAM_STEP)
        v_hat = vn / (1.0 - ADAM_B2 ** ADAM_STEP)
        g_ref[...] = g
        d_ref[...] = -ADAM_LR * (m_hat / (jnp.sqrt(v_hat) + ADAM_EPS) + ADAM_WD * w_ref[...])
        nm_ref[...] = mn
        nv_ref[...] = vn

    row = pl.BlockSpec((tr, width), lambda i: (i, 0))
    return pl.pallas_call(
        body, name=name, grid=(rows // tr,),
        in_specs=[pl.BlockSpec((N_DEV, tr, width), lambda i: (0, i, 0)), row, row, row],
        out_specs=[row] * 4, out_shape=[jax.ShapeDtypeStruct((rows, width), F32)] * 4,
        compiler_params=_cparams(("parallel",)),
    )(parts, w, m, v)


def _pack_replicated(blocks):
    flat = jnp.concatenate([blocks[name].reshape(-1).astype(F32) for name, _, _ in REPLICATED])
    flat = jnp.pad(flat, (0, PACK_ROWS * PACK_W - PACK_USED))
    return flat.reshape(PACK_ROWS, PACK_W)


def _unpack_replicated(buf):
    flat = buf.reshape(-1)
    return {name: flat[off:off + n].reshape(1, n) for name, (off, n) in PACK_TABLE.items()}


def _join_shards(seg, shape, axis):
    return seg.reshape(shape) if axis == 0 else seg.transpose(1, 0, 2).reshape(shape)


def _split_shards(g, shape, axis):
    r, c = shape
    if axis == 0:
        return g.reshape(N_DEV, r // N_DEV, c)
    return g.reshape(r, N_DEV, c // N_DEV).transpose(1, 0, 2)


def _to_residues(a, d):
    s, c = a.shape
    return a.reshape(s // d, d, c).transpose(1, 0, 2).reshape(s, c)


def _from_residues(a, d):
    s, c = a.shape
    return a.reshape(d, s // d, c).transpose(1, 0, 2).reshape(s, c)


def _pad_heads(w, heads, width, lo, hi):
    r = w.shape[0]
    w = w.reshape(r, heads, width)[:, :, lo:hi]
    w = jnp.pad(w, ((0, 0), (0, 0), (0, LANES - (hi - lo))))
    return w.reshape(r, heads * LANES)


def _local_step(x, mem, positions, target, w):
    s = x.shape[0]
    bf = lambda a: a.astype(BF16)

    w_in = w["w_in"]
    kr_cols = jnp.pad(w_in[:, OFF_KV:OFF_KR], ((0, 0), (MLA_NOPE, LANES - MLA_QK_DIM)))
    w_a = bf(jnp.concatenate([w_in[:, :OFF_KV], kr_cols], axis=1))
    w_dm = bf(w_in[:, OFF_KR:OFF_MEMQ])
    w_g = bf(w_in[:, OFF_MEMQ:])
    w_in_t = jnp.concatenate([w_a, w_dm, w_g], axis=1).T
    wq = bf(_pad_heads(w["w_uq"], MLA_HEADS, MLA_QK_DIM, 0, MLA_QK_DIM))
    wk = bf(_pad_heads(w["w_ukv"], MLA_HEADS, MLA_NOPE + MLA_V, 0, MLA_NOPE))
    wv = bf(_pad_heads(w["w_ukv"], MLA_HEADS, MLA_NOPE + MLA_V, MLA_NOPE, MLA_NOPE + MLA_V))
    w_mkv = bf(w["w_mem_kv"])
    wb_mla = bf(jnp.pad(w["w_br_mla"].reshape(MLA_HEADS, MLA_V, D_MODEL),
                        ((0, 0), (0, LANES - MLA_V), (0, 0))).reshape(MLA_HEADS * LANES, D_MODEL))
    wb_dil, wb_mem, w_o = bf(w["w_br_dil"]), bf(w["w_br_mem"]), bf(w["w_o"])
    w_up, w_down = bf(w["w_ffn_up"]), bf(w["w_ffn_down"])
    tabs = _rope_tables(positions)
    slopes = [jnp.asarray(sl, F32) for sl in DIL_SLOPES]
    mla_scale = MLA_QK_DIM ** -0.5
    mem_scale = LANES ** -0.5
    MQ = 3 * DIL_HEADS

    h = _rms_fwd(x, w["g_pre_mix"], name="rms_pre_mix", out_dtype=BF16)
    p_a = _matmul(h, w_a, name="proj_a")
    p_dm = _matmul(h, w_dm, name="proj_dm", out_dtype=BF16)
    p_g = _matmul(h, w_g, name="proj_gate")
    c_q, c_kv, kr = p_a[:, :OFF_Q], p_a[:, OFF_Q:OFF_KV], p_a[:, OFF_KV:]

    qn = _rms_fwd(c_q, w["mla_q_norm"], name="rms_q", out_dtype=BF16)
    kvn = _rms_fwd(c_kv, w["mla_kv_norm"], name="rms_kv", out_dtype=BF16)
    q_raw = _matmul(qn, wq, name="mla_q_up")
    k_raw = _matmul(kvn, wk, name="mla_k_up")
    v_f = _matmul(kvn, wv, name="mla_v_up", out_dtype=BF16)
    q_f = _rope_fwd(q_raw, tabs, name="rope_q", scale=mla_scale * LOG2E)
    k_f = _rope_fwd(k_raw, tabs, name="rope_k", scale=1.0, add=kr)
    o_mla, lse_mla = _causal_fwd(q_f, k_f, v_f, name="mla_fwd", heads=MLA_HEADS, ones_lane=MLA_V)

    dil_in, dil_o, dil_lse = [], [], []
    for g, (_, d) in enumerate(DIL_PAIRS):
        if d == 1:
            arrs, offs = (p_dm, p_dm, p_dm), (4 * g, DIL_HEADS + 4 * g, 2 * DIL_HEADS + 4 * g)
        else:
            arrs = tuple(_to_residues(p_dm[:, (t * DIL_HEADS + 4 * g) * LANES:(t * DIL_HEADS + 4 * g + 4) * LANES], d)
                         for t in range(3))
            offs = (0, 0, 0)
        o_g, lse_g = _band_fwd(*arrs, slopes[g], name=f"dil_fwd_{g}", dilation=d,
                               qoff=offs[0], koff=offs[1], voff=offs[2])
        dil_in.append((arrs, offs))
        dil_o.append(_from_residues(o_g, d))
        dil_lse.append(_from_residues(lse_g, d))
    y_dil = _mix_fwd(dil_o, dil_lse, name="dil_mix")

    memn = _rms_fwd(mem, w["g_mem"], name="rms_mem", out_dtype=BF16)
    kv_m = _matmul(memn, w_mkv, name="mem_kv", out_dtype=BF16)
    memat = dict(heads=MEM_HEADS, qoff=MQ, koff=0, voff=MEM_HEADS, causal=False, scale=mem_scale, tq=512, tk=256)
    o_mem, lse_mem = _flash_fwd(p_dm, kv_m, kv_m, name="mem_fwd", **memat)

    b_mla = _matmul(o_mla, wb_mla, name="br_mla")
    b_dil = _matmul(y_dil, wb_dil, name="br_dil")
    b_mem = _matmul(o_mem, wb_mem, name="br_mem")
    merged = _gate_fwd(p_g, w["b_gate"], (b_mla, b_dil, b_mem), name="gate_fwd")
    z1 = _matmul(merged, w_o, name="out_proj")
    x1 = _rms_fwd(z1, w["g_post_mix"], name="rms_post_mix", out_dtype=F32, add=x)
    h2 = _rms_fwd(x1, w["g_pre_ffn"], name="rms_pre_ffn", out_dtype=BF16)
    u = _matmul(h2, w_up, name="ffn_up")
    zg, zv, act = _conv_fwd(u, w["conv_w"], w["conv_b"], name="conv_fwd")
    f = _matmul(act, w_down, name="ffn_down")
    dy, sq = _loss_fwd(x1, f, target, w["g_post_ffn"], name="loss")
    loss = 0.5 * jnp.sum(sq) / D_MODEL

    grads = {}
    df, grads["g_post_ffn"] = _rms_bwd(f, w["g_post_ffn"], dy, name="rms_post_ffn_bwd", out_dtype=BF16)
    da = _matmul(df, w_down.T, name="ffn_down_dx")
    grads["w_ffn_down"] = _matmul(act, df, name="ffn_down_dw", trans_a=True)
    dzg, dzv = _silu_bwd(da, zg, zv, name="silu_bwd")
    nct = D_FF // CONV_TC
    du_g, cacc_g = _conv_bwd(dzg, u, w["conv_w"], name="conv_bwd_gate", off=0)
    du_v, cacc_v = _conv_bwd(dzv, u, w["conv_w"], name="conv_bwd_val", off=nct)
    grads["conv_w"] = jnp.concatenate([cacc_g[0:3], cacc_v[0:3]], axis=1)
    grads["conv_b"] = jnp.concatenate([cacc_g[3:4], cacc_v[3:4]], axis=1)
    w_up_t = w_up.T
    dh2 = _matmul(du_g, w_up_t[:D_FF], name="ffn_up_dx_gate")
    dh2 = _matmul(du_v, w_up_t[D_FF:], name="ffn_up_dx_val", add=dh2)
    grads["w_ffn_up"] = jnp.concatenate([_matmul(h2, du_g, name="ffn_up_dw_gate", trans_a=True),
                                         _matmul(h2, du_v, name="ffn_up_dw_val", trans_a=True)], axis=1)
    dx1, grads["g_pre_ffn"] = _rms_bwd(x1, w["g_pre_ffn"], dh2, name="rms_pre_ffn_bwd", out_dtype=F32, add=dy)
    dz1, grads["g_post_mix"] = _rms_bwd(z1, w["g_post_mix"], dx1, name="rms_post_mix_bwd", out_dtype=BF16)
    dmerged = _matmul(dz1, w_o.T, name="out_proj_dx")
    grads["w_o"] = _matmul(merged, dz1, name="out_proj_dw", trans_a=True)
    db_mla, db_dil, db_mem, dgp, grads["b_gate"] = _gate_bwd(
        dmerged, p_g, w["b_gate"], (b_mla, b_dil, b_mem), name="gate_bwd")

    do_mla = _matmul(db_mla, wb_mla.T, name="br_mla_dx", out_dtype=BF16)
    g_wb_mla = _matmul(o_mla, db_mla, name="br_mla_dw", trans_a=True)
    grads["w_br_mla"] = g_wb_mla.reshape(MLA_HEADS, LANES, D_MODEL)[:, :MLA_V].reshape(MLA_HEADS * MLA_V, D_MODEL)
    delta_mla = _row_dot(do_mla, o_mla, name="mla_delta")
    dq_f, dk_f, dv_f = _causal_bwd(q_f, k_f, v_f, do_mla, lse_mla, delta_mla, name="mla_bwd", heads=MLA_HEADS)
    dq_raw = _rope_bwd(dq_f, tabs, name="rope_q_bwd", scale=mla_scale, with_add=False)
    dk_raw, dkr = _rope_bwd(dk_f, tabs, name="rope_k_bwd", scale=1.0, with_add=True)
    dqn = _matmul(dq_raw, wq.T, name="mla_q_up_dx")
    g_wq = _matmul(qn, dq_raw, name="mla_q_up_dw", trans_a=True)
    grads["w_uq"] = g_wq.reshape(MLA_Q_RANK, MLA_HEADS, LANES)[:, :, :MLA_QK_DIM].reshape(MLA_Q_RANK, -1)
    dkvn = _matmul(dk_raw, wk.T, name="mla_k_up_dx")
    dkvn = _matmul(dv_f, wv.T, name="mla_v_up_dx", add=dkvn)
    g_wk = _matmul(kvn, dk_raw, name="mla_k_up_dw", trans_a=True).reshape(MLA_KV_RANK, MLA_HEADS, LANES)
    g_wv = _matmul(kvn, dv_f, name="mla_v_up_dw", trans_a=True).reshape(MLA_KV_RANK, MLA_HEADS, LANES)
    grads["w_ukv"] = jnp.concatenate([g_wk[:, :, :MLA_NOPE], g_wv[:, :, :MLA_V]], axis=2).reshape(MLA_KV_RANK, -1)
    dc_q, grads["mla_q_norm"] = _rms_bwd(c_q, w["mla_q_norm"], dqn, name="rms_q_bwd", out_dtype=BF16)
    dc_kv, grads["mla_kv_norm"] = _rms_bwd(c_kv, w["mla_kv_norm"], dkvn, name="rms_kv_bwd", out_dtype=BF16)

    dy_dil = _matmul(db_dil, wb_dil.T, name="br_dil_dx")
    grads["w_br_dil"] = _matmul(y_dil, db_dil, name="br_dil_dw", trans_a=True)
    mix = _mix_bwd(dy_dil, dil_o, dil_lse, name="dil_mix_bwd")
    d_dil = [[None] * 3 for _ in range(3)]
    for g, (_, d) in enumerate(DIL_PAIRS):
        arrs, offs = dil_in[g]
        do_g, dl_g, lse_g = mix[g], mix[3 + g], dil_lse[g]
        if d != 1:
            do_g, dl_g, lse_g = _to_residues(do_g, d), _to_residues(dl_g, d), _to_residues(lse_g, d)
        dq_g, dk_g, dv_g = _band_bwd(*arrs, do_g, lse_g, dl_g, slopes[g], name=f"dil_bwd_{g}", dilation=d,
                                     qoff=offs[0], koff=offs[1], voff=offs[2])
        for t, a in enumerate((dq_g, dk_g, dv_g)):
            d_dil[t][g] = a if d == 1 else _from_residues(a, d)

    do_mem = _matmul(db_mem, wb_mem.T, name="br_mem_dx", out_dtype=BF16)
    grads["w_br_mem"] = _matmul(o_mem, db_mem, name="br_mem_dw", trans_a=True)
    delta_mem = _row_dot(do_mem, o_mem, name="mem_delta")
    dq_mem = _flash_bwd_dq(p_dm, kv_m, kv_m, do_mem, lse_mem, delta_mem, name="mem_bwd_dq", out_dtype=BF16, **memat)
    dk_mem, dv_mem = _flash_bwd_dkv(p_dm, kv_m, kv_m, do_mem, lse_mem, delta_mem, name="mem_bwd_dkv",
                                    dk_dtype=BF16, dv_dtype=BF16, **memat)
    dkv_m = jnp.concatenate([dk_mem, dv_mem], axis=1)
    dmemn = _matmul(dkv_m, w_mkv.T, name="mem_kv_dx")
    grads["w_mem_kv"] = _matmul(memn, dkv_m, name="mem_kv_dw", trans_a=True)
    _, grads["g_mem"] = _rms_bwd(mem, w["g_mem"], dmemn, name="rms_mem_bwd", out_dtype=BF16)

    dp_all = jnp.concatenate([dc_q, dc_kv, bf(dkr)] + d_dil[0] + d_dil[1] + d_dil[2] + [dq_mem, dgp], axis=1)
    dh = _matmul(dp_all, w_in_t, name="proj_dx")
    g_in = _matmul(h, dp_all, name="proj_dw", trans_a=True)
    grads["w_in"] = jnp.concatenate(
        [g_in[:, :OFF_KV], g_in[:, OFF_KV + MLA_NOPE:OFF_KV + MLA_QK_DIM], g_in[:, N_A:]], axis=1)
    dx, grads["g_pre_mix"] = _rms_bwd(x, w["g_pre_mix"], dh, name="rms_pre_mix_bwd", out_dtype=F32, add=dx1)
    return loss, dx, grads


def kernel(x, mem, positions, g_pre_mix, w_in, b_gate, mla_q_norm, w_uq, mla_kv_norm, w_ukv, g_mem, w_mem_kv, w_br_mla, w_br_dil, w_br_mem, w_o, g_post_mix, g_pre_ffn, w_ffn_up, conv_w, conv_b, w_ffn_down, g_post_ffn, loss_target, m_g_pre_mix, m_w_in, m_b_gate, m_mla_q_norm, m_w_uq, m_mla_kv_norm, m_w_ukv, m_g_mem, m_w_mem_kv, m_w_br_mla, m_w_br_dil, m_w_br_mem, m_w_o, m_g_post_mix, m_g_pre_ffn, m_w_ffn_up, m_conv_w, m_conv_b, m_w_ffn_down, m_g_post_ffn, v_g_pre_mix, v_w_in, v_b_gate, v_mla_q_norm, v_w_uq, v_mla_kv_norm, v_w_ukv, v_g_mem, v_w_mem_kv, v_w_br_mla, v_w_br_dil, v_w_br_mem, v_w_o, v_g_post_mix, v_g_pre_ffn, v_w_ffn_up, v_conv_w, v_conv_b, v_w_ffn_down, v_g_post_ffn):
    local = dict(g_pre_mix=g_pre_mix, w_in=w_in, b_gate=b_gate, mla_q_norm=mla_q_norm, w_uq=w_uq,
                 mla_kv_norm=mla_kv_norm, w_ukv=w_ukv, g_mem=g_mem, w_mem_kv=w_mem_kv, w_br_mla=w_br_mla,
                 w_br_dil=w_br_dil, w_br_mem=w_br_mem, w_o=w_o, g_post_mix=g_post_mix, g_pre_ffn=g_pre_ffn,
                 w_ffn_up=w_ffn_up, conv_w=conv_w, conv_b=conv_b, w_ffn_down=w_ffn_down, g_post_ffn=g_post_ffn)
    mom_m = dict(g_pre_mix=m_g_pre_mix, w_in=m_w_in, b_gate=m_b_gate, mla_q_norm=m_mla_q_norm, w_uq=m_w_uq,
                 mla_kv_norm=m_mla_kv_norm, w_ukv=m_w_ukv, g_mem=m_g_mem, w_mem_kv=m_w_mem_kv, w_br_mla=m_w_br_mla,
                 w_br_dil=m_w_br_dil, w_br_mem=m_w_br_mem, w_o=m_w_o, g_post_mix=m_g_post_mix,
                 g_pre_ffn=m_g_pre_ffn, w_ffn_up=m_w_ffn_up, conv_w=m_conv_w, conv_b=m_conv_b,
                 w_ffn_down=m_w_ffn_down, g_post_ffn=m_g_post_ffn)
    mom_v = dict(g_pre_mix=v_g_pre_mix, w_in=v_w_in, b_gate=v_b_gate, mla_q_norm=v_mla_q_norm, w_uq=v_w_uq,
                 mla_kv_norm=v_mla_kv_norm, w_ukv=v_w_ukv, g_mem=v_g_mem, w_mem_kv=v_w_mem_kv, w_br_mla=v_w_br_mla,
                 w_br_dil=v_w_br_dil, w_br_mem=v_w_br_mem, w_o=v_w_o, g_post_mix=v_g_post_mix,
                 g_pre_ffn=v_g_pre_ffn, w_ffn_up=v_w_ffn_up, conv_w=v_conv_w, conv_b=v_conv_b,
                 w_ffn_down=v_w_ffn_down, g_post_ffn=v_g_post_ffn)

    shards = [local[name][0].astype(F32 if name == "conv_w" else BF16) for name, _, _ in SHARDED]
    gathered = _exchange(shards, name="gather_weights", gather=True)
    full = {name: _join_shards(seg, shape, axis) for (name, shape, axis), seg in zip(SHARDED, gathered)}
    for name, _, _ in REPLICATED:
        full[name] = local[name].reshape(1, -1)

    loss, dx, grads = _local_step(x[0], mem[0], positions[0], loss_target[0], full)

    slabs = [_split_shards(grads[name].astype(F32 if name == "conv_w" else BF16), shape, axis)
             for name, shape, axis in SHARDED]
    parts = _exchange(slabs, name="exchange_grads", gather=False)
    rep_parts = _exchange([_pack_replicated(grads)], name="gather_replicated_grads", gather=True)[0]

    results = {}
    for (name, _, _), part in zip(SHARDED, parts):
        res = _adamw(part, local[name][0], mom_m[name][0], mom_v[name][0], name="adamw_" + name)
        results[name] = [r[None] for r in res]
    rep = _adamw(rep_parts, _pack_replicated(local), _pack_replicated(mom_m), _pack_replicated(mom_v),
                 name="adamw_replicated")
    for i, buf in enumerate(rep):
        for name, val in _unpack_replicated(buf).items():
            results.setdefault(name, [None] * 4)[i] = val

    loss = lax.psum(loss, ("x", "y", "c"))
    outs = [loss, dx[None]]
    for i in range(4):
        outs.extend(results[name][i] for name in PARAM_NAMES)
    return tuple(outs)
```

```python
import functools

import numpy as np
import jax
import jax.numpy as jnp
from jax import lax
from jax.experimental import pallas as pl
from jax.experimental.pallas import tpu as pltpu

F32 = jnp.float32
BF16 = jnp.bfloat16

N_DEV = 8
D_MODEL = 1024
RMS_EPS = 1e-6
NEG_INF = -1e30
LANES = 128
SUBLANES = 8
BLOCK = 128

MLA_HEADS = 8
MLA_NOPE = 64
MLA_ROPE = 32
MLA_V = 64
MLA_QK_DIM = MLA_NOPE + MLA_ROPE
MLA_Q_RANK = 384
MLA_KV_RANK = 256
ROPE_THETA = 10000.0
DIL_PAIRS = ((128, 1), (512, 4), (2048, 16))
DIL_HEADS_PER_GROUP = 4
DIL_HEADS = 12
MEM_HEADS = 4
D_FF = 2816
OFF_Q = MLA_Q_RANK
OFF_KV = OFF_Q + MLA_KV_RANK
OFF_KR = OFF_KV + MLA_ROPE
OFF_DIL = OFF_KR + 3 * DIL_HEADS * LANES
OFF_MEMQ = OFF_DIL + MEM_HEADS * LANES
D_IN = OFF_MEMQ + 3 * D_MODEL
N_A = OFF_KV + LANES
N_DM = 3 * DIL_HEADS * LANES + MEM_HEADS * LANES

ADAM_LR = 0.001
ADAM_B1 = 0.9
ADAM_B2 = 0.999
ADAM_EPS = 1e-08
ADAM_WD = 0.01
ADAM_STEP = 10

VMEM_LIMIT = 48 * 1024 * 1024
VMEM_LIMIT_BIG = 58 * 1024 * 1024
PACK_W = 1024

_ALIBI_BASE = np.exp2(-8.0 * np.arange(1, DIL_HEADS + 1) / DIL_HEADS)
DIL_SLOPES = [[float(_ALIBI_BASE[hh * 3 + g]) for hh in range(DIL_HEADS_PER_GROUP)] for g in range(3)]

PARAMS = (
    ("g_pre_mix", (1024,), None), ("w_in", (1024, D_IN), 1), ("b_gate", (3072,), None),
    ("mla_q_norm", (384,), None), ("w_uq", (384, 768), 1), ("mla_kv_norm", (256,), None),
    ("w_ukv", (256, 1024), 1), ("g_mem", (1024,), None), ("w_mem_kv", (1024, 1024), 0),
    ("w_br_mla", (512, 1024), 1), ("w_br_dil", (512, 1024), 1), ("w_br_mem", (512, 1024), 1),
    ("w_o", (1024, 1024), 0), ("g_post_mix", (1024,), None), ("g_pre_ffn", (1024,), None),
    ("w_ffn_up", (1024, 2 * D_FF), 1), ("conv_w", (3, 2 * D_FF), 1), ("conv_b", (2 * D_FF,), None),
    ("w_ffn_down", (D_FF, 1024), 0), ("g_post_ffn", (1024,), None),
)
PARAM_NAMES = tuple(p[0] for p in PARAMS)


def _shard_shape(shape, axis):
    if axis is None:
        return shape
    return tuple(s // N_DEV if a == axis else s for a, s in enumerate(shape))


SHARDED = tuple(p for p in PARAMS if p[2] is not None)
REPLICATED = tuple(p for p in PARAMS if p[2] is None)


def _layout():
    off, table = 0, {}
    for name, shape, _ in REPLICATED:
        table[name] = (off, shape[0])
        off += shape[0]
    rows = -(-off // PACK_W)
    rows = -(-rows // SUBLANES) * SUBLANES
    return table, off, rows


PACK_TABLE, PACK_USED, PACK_ROWS = _layout()


def _pick(n, cap):
    best = None
    for t in range(LANES, min(n, cap) + 1, LANES):
        if n % t == 0:
            best = t
    return best if best is not None else n


def _rows(n, cap, mult=SUBLANES):
    best = None
    for t in range(mult, min(n, cap) + 1, mult):
        if n % t == 0:
            best = t
    return best if best is not None else n


def _cparams(sem, vmem=VMEM_LIMIT):
    return pltpu.CompilerParams(dimension_semantics=sem, vmem_limit_bytes=vmem)


def _matmul(a, b, *, name, out_dtype=F32, trans_a=False, add=None, tm=1024, tn=1408, tk=640):
    if trans_a:
        kc, m = a.shape
    else:
        m, kc = a.shape
    n = b.shape[1]
    assert b.shape[0] == kc
    tm, tn, tk = _pick(m, tm), _pick(n, tn), _pick(kc, tk)
    nk = kc // tk

    def body(*refs):
        if add is None:
            a_ref, b_ref, o_ref, acc = refs
        else:
            a_ref, b_ref, c_ref, o_ref, acc = refs
        k = pl.program_id(2)

        @pl.when(k == 0)
        def _():
            if add is None:
                acc[...] = jnp.zeros_like(acc)
            else:
                acc[...] = c_ref[...].astype(F32)

        av = a_ref[...].astype(BF16)
        bv = b_ref[...].astype(BF16)
        if trans_a:
            acc[...] += lax.dot_general(av, bv, (((0,), (0,)), ((), ())), preferred_element_type=F32)
        else:
            acc[...] += jnp.dot(av, bv, preferred_element_type=F32)

        @pl.when(k == nk - 1)
        def _():
            o_ref[...] = acc[...].astype(out_dtype)

    if trans_a:
        a_spec = pl.BlockSpec((tk, tm), lambda i, j, k: (k, i))
    else:
        a_spec = pl.BlockSpec((tm, tk), lambda i, j, k: (i, k))
    in_specs = [a_spec, pl.BlockSpec((tk, tn), lambda i, j, k: (k, j))]
    args = [a, b]
    if add is not None:
        in_specs.append(pl.BlockSpec((tm, tn), lambda i, j, k: (i, j)))
        args.append(add)
    return pl.pallas_call(
        body, name=name, grid=(m // tm, n // tn, nk),
        in_specs=in_specs, out_specs=pl.BlockSpec((tm, tn), lambda i, j, k: (i, j)),
        out_shape=jax.ShapeDtypeStruct((m, n), out_dtype),
        scratch_shapes=[pltpu.VMEM((tm, tn), F32)],
        compiler_params=_cparams(("parallel", "parallel", "arbitrary")),
    )(*args)


def _rms_fwd(x, g, *, name, out_dtype, add=None):
    s, n = x.shape
    ts = _rows(s, 512)

    def body(*refs):
        if add is None:
            x_ref, g_ref, o_ref = refs
        else:
            x_ref, g_ref, a_ref, o_ref = refs
        xv = x_ref[...]
        r = lax.rsqrt(jnp.mean(xv * xv, axis=-1, keepdims=True) + RMS_EPS)
        y = xv * r * g_ref[...]
        if add is not None:
            y = a_ref[...] + y
        o_ref[...] = y.astype(out_dtype)

    row = pl.BlockSpec((ts, n), lambda i: (i, 0))
    in_specs = [row, pl.BlockSpec((1, n), lambda i: (0, 0))]
    args = [x, g]
    if add is not None:
        in_specs.append(row)
        args.append(add)
    return pl.pallas_call(
        body, name=name, grid=(s // ts,), in_specs=in_specs, out_specs=row,
        out_shape=jax.ShapeDtypeStruct((s, n), out_dtype),
        compiler_params=_cparams(("parallel",)),
    )(*args)


def _rms_bwd(x, g, dy, *, name, out_dtype, add=None):
    s, n = x.shape
    ts = _rows(s, 512)

    def body(*refs):
        if add is None:
            x_ref, g_ref, dy_ref, dx_ref, dg_ref = refs
        else:
            x_ref, g_ref, dy_ref, a_ref, dx_ref, dg_ref = refs
        i = pl.program_id(0)
        xv = x_ref[...]
        dyv = dy_ref[...].astype(F32)
        r = lax.rsqrt(jnp.mean(xv * xv, axis=-1, keepdims=True) + RMS_EPS)
        nx = xv * r
        gdy = dyv * g_ref[...]
        dx = r * (gdy - nx * jnp.mean(nx * gdy, axis=-1, keepdims=True))
        if add is not None:
            dx = a_ref[...] + dx
        dx_ref[...] = dx.astype(out_dtype)

        @pl.when(i == 0)
        def _():
            dg_ref[...] = jnp.zeros_like(dg_ref)

        dg_ref[...] += jnp.sum(dyv * nx, axis=0, keepdims=True)

    row = pl.BlockSpec((ts, n), lambda i: (i, 0))
    vec = pl.BlockSpec((1, n), lambda i: (0, 0))
    in_specs = [row, vec, row]
    args = [x, g, dy]
    if add is not None:
        in_specs.append(row)
        args.append(add)
    return pl.pallas_call(
        body, name=name, grid=(s // ts,), in_specs=in_specs, out_specs=[row, vec],
        out_shape=[jax.ShapeDtypeStruct((s, n), out_dtype), jax.ShapeDtypeStruct((1, n), F32)],
        compiler_params=_cparams(("arbitrary",)),
    )(*args)


def _loss_fwd(x1, f, target, g, *, name):
    s, n = x1.shape
    ts = _rows(s, 512)

    def body(x_ref, f_ref, t_ref, g_ref, dy_ref, sq_ref):
        i = pl.program_id(0)
        fv = f_ref[...]
        r = lax.rsqrt(jnp.mean(fv * fv, axis=-1, keepdims=True) + RMS_EPS)
        err = x_ref[...] + fv * r * g_ref[...] - t_ref[...]
        dy_ref[...] = err * (1.0 / n)

        @pl.when(i == 0)
        def _():
            sq_ref[...] = jnp.zeros_like(sq_ref)

        sq_ref[...] += jnp.sum(err * err, axis=0, keepdims=True)

    row = pl.BlockSpec((ts, n), lambda i: (i, 0))
    vec = pl.BlockSpec((1, n), lambda i: (0, 0))
    return pl.pallas_call(
        body, name=name, grid=(s // ts,), in_specs=[row, row, row, vec], out_specs=[row, vec],
        out_shape=[jax.ShapeDtypeStruct((s, n), F32), jax.ShapeDtypeStruct((1, n), F32)],
        compiler_params=_cparams(("arbitrary",)),
    )(x1, f, target, g)


def _rope_tables(positions):
    half = MLA_ROPE // 2
    inv_freq = ROPE_THETA ** (-jnp.arange(half, dtype=F32) / half)
    ang = positions.astype(F32)[:, None] * inv_freq[None, :]
    cos, sin = jnp.cos(ang), jnp.sin(ang)
    s = positions.shape[0]
    one = jnp.ones((s, MLA_NOPE), F32)
    zero = jnp.zeros((s, MLA_NOPE), F32)
    pad1 = jnp.ones((s, LANES - MLA_QK_DIM), F32)
    pad0 = jnp.zeros((s, LANES - MLA_QK_DIM), F32)
    zh = jnp.zeros((s, half), F32)
    c_tab = jnp.concatenate([one, cos, cos, pad1], axis=1)
    s1_tab = jnp.concatenate([zero, -sin, zh, pad0], axis=1)
    s2_tab = jnp.concatenate([zero, zh, sin, pad0], axis=1)
    return c_tab, s1_tab, s2_tab


def _rope_fwd(x, tabs, *, name, scale, add=None):
    s, n = x.shape
    nh = n // LANES
    ts = _rows(s, 512)
    half = MLA_ROPE // 2

    def body(*refs):
        if add is None:
            x_ref, c_ref, s1_ref, s2_ref, o_ref = refs
        else:
            x_ref, a_ref, c_ref, s1_ref, s2_ref, o_ref = refs
        c, s1, s2 = c_ref[...], s1_ref[...], s2_ref[...]
        for h in range(nh):
            xh = x_ref[:, h * LANES:(h + 1) * LANES]
            if add is not None:
                xh = xh + a_ref[...]
            y = xh * c + pltpu.roll(xh, LANES - half, 1) * s1 + pltpu.roll(xh, half, 1) * s2
            o_ref[:, h * LANES:(h + 1) * LANES] = (y * scale).astype(BF16)

    row = pl.BlockSpec((ts, n), lambda i: (i, 0))
    tab = pl.BlockSpec((ts, LANES), lambda i: (i, 0))
    in_specs = [row] + ([tab] if add is not None else []) + [tab, tab, tab]
    args = [x] + ([add] if add is not None else []) + list(tabs)
    return pl.pallas_call(
        body, name=name, grid=(s // ts,), in_specs=in_specs, out_specs=row,
        out_shape=jax.ShapeDtypeStruct((s, n), BF16),
        compiler_params=_cparams(("parallel",)),
    )(*args)


def _rope_bwd(dy, tabs, *, name, scale, with_add):
    s, n = dy.shape
    nh = n // LANES
    ts = _rows(s, 512)
    half = MLA_ROPE // 2

    def body(*refs):
        if with_add:
            dy_ref, c_ref, s1_ref, s2_ref, dx_ref, da_ref = refs
        else:
            dy_ref, c_ref, s1_ref, s2_ref, dx_ref = refs
        c, s1, s2 = c_ref[...], s1_ref[...], s2_ref[...]
        tot = None
        for h in range(nh):
            g = dy_ref[:, h * LANES:(h + 1) * LANES].astype(F32)
            dx = (g * c + pltpu.roll(g * s1, half, 1) + pltpu.roll(g * s2, LANES - half, 1)) * scale
            dx_ref[:, h * LANES:(h + 1) * LANES] = dx.astype(BF16)
            tot = dx if tot is None else tot + dx
        if with_add:
            da_ref[...] = tot

    row = pl.BlockSpec((ts, n), lambda i: (i, 0))
    tab = pl.BlockSpec((ts, LANES), lambda i: (i, 0))
    out_specs = [row, tab] if with_add else row
    out_shape = [jax.ShapeDtypeStruct((s, n), BF16)]
    if with_add:
        out_shape.append(jax.ShapeDtypeStruct((s, LANES), F32))
    else:
        out_shape = out_shape[0]
    return pl.pallas_call(
        body, name=name, grid=(s // ts,), in_specs=[row, tab, tab, tab], out_specs=out_specs,
        out_shape=out_shape, compiler_params=_cparams(("parallel",)),
    )(dy, *tabs)


def _scores(q, k, scale, diag):
    s = lax.dot_general(q, k, (((1,), (1,)), ((), ())), preferred_element_type=F32)
    if scale != 1.0:
        s = s * scale
    if diag:
        rows = lax.broadcasted_iota(jnp.int32, s.shape, 0)
        cols = lax.broadcasted_iota(jnp.int32, s.shape, 1)
        s = jnp.where(cols <= rows, s, NEG_INF)
    return s


def _flash_fwd(q, k, v, *, name, heads, qoff, koff, voff, causal, scale, tq, tk):
    s_q, s_kv = q.shape[0], k.shape[0]
    tq, tk = min(tq, s_q), min(tk, s_kv)
    nq, nk = s_q // tq, s_kv // tk
    if causal:
        assert tq == tk and s_q == s_kv

    def body(q_ref, k_ref, v_ref, o_ref, lse_ref, m_s, l_s, acc):
        i, j = pl.program_id(1), pl.program_id(2)

        @pl.when(j == 0)
        def _():
            m_s[...] = jnp.full_like(m_s, NEG_INF)
            l_s[...] = jnp.zeros_like(l_s)
            acc[...] = jnp.zeros_like(acc)

        def step(diag):
            s = _scores(q_ref[...], k_ref[...], scale, diag)
            m_prev = m_s[...]
            m_cur = jnp.maximum(m_prev, jnp.max(s, axis=1, keepdims=True))
            alpha = jnp.exp(m_prev - m_cur)
            p = jnp.exp(s - m_cur[:, :1])
            l_s[...] = alpha * l_s[...] + jnp.sum(p, axis=1, keepdims=True)
            acc[...] = alpha * acc[...] + jnp.dot(p.astype(BF16), v_ref[...], preferred_element_type=F32)
            m_s[...] = m_cur

        def finish():
            o_ref[...] = (acc[...] / l_s[...]).astype(o_ref.dtype)
            lse_ref[...] = m_s[...] + jnp.log(l_s[...])

        if causal:
            @pl.when(j < i)
            def _():
                step(False)

            @pl.when(j == i)
            def _():
                step(True)
                finish()
        else:
            step(False)

            @pl.when(j == nk - 1)
            def _():
                finish()

    def kv_idx(off):
        if causal:
            return lambda h, i, j: (jnp.minimum(j, i), off + h)
        return lambda h, i, j: (j, off + h)

    blk_q = pl.BlockSpec((tq, LANES), lambda h, i, j: (i, qoff + h))
    out_q = pl.BlockSpec((tq, LANES), lambda h, i, j: (i, h))
    return pl.pallas_call(
        body, name=name, grid=(heads, nq, nk),
        in_specs=[blk_q, pl.BlockSpec((tk, LANES), kv_idx(koff)), pl.BlockSpec((tk, LANES), kv_idx(voff))],
        out_specs=[out_q, out_q],
        out_shape=[jax.ShapeDtypeStruct((s_q, heads * LANES), BF16),
                   jax.ShapeDtypeStruct((s_q, heads * LANES), F32)],
        scratch_shapes=[pltpu.VMEM((tq, LANES), F32)] * 3,
        compiler_params=_cparams(("parallel", "parallel", "arbitrary")),
    )(q, k, v)


def _flash_bwd_dq(q, k, v, do, lse, delta, *, name, heads, qoff, koff, voff, causal, scale, tq, tk, out_dtype):
    s_q, s_kv = q.shape[0], k.shape[0]
    tq, tk = min(tq, s_q), min(tk, s_kv)
    nq, nk = s_q // tq, s_kv // tk

    def body(q_ref, k_ref, v_ref, do_ref, lse_ref, dl_ref, dq_ref, acc):
        i, j = pl.program_id(1), pl.program_id(2)

        @pl.when(j == 0)
        def _():
            acc[...] = jnp.zeros_like(acc)

        def step(diag):
            s = _scores(q_ref[...], k_ref[...], scale, diag)
            p = jnp.exp(s - lse_ref[:, :1])
            dp = lax.dot_general(do_ref[...], v_ref[...], (((1,), (1,)), ((), ())), preferred_element_type=F32)
            ds = p * (dp - dl_ref[:, :1])
            acc[...] += jnp.dot(ds.astype(BF16), k_ref[...], preferred_element_type=F32)

        def finish():
            dq_ref[...] = (acc[...] * scale).astype(out_dtype)

        if causal:
            @pl.when(j < i)
            def _():
                step(False)

            @pl.when(j == i)
            def _():
                step(True)
                finish()
        else:
            step(False)

            @pl.when(j == nk - 1)
            def _():
                finish()

    def kv_idx(off):
        if causal:
            return lambda h, i, j: (jnp.minimum(j, i), off + h)
        return lambda h, i, j: (j, off + h)

    blk_q = pl.BlockSpec((tq, LANES), lambda h, i, j: (i, qoff + h))
    blk_h = pl.BlockSpec((tq, LANES), lambda h, i, j: (i, h))
    return pl.pallas_call(
        body, name=name, grid=(heads, nq, nk),
        in_specs=[blk_q, pl.BlockSpec((tk, LANES), kv_idx(koff)), pl.BlockSpec((tk, LANES), kv_idx(voff)),
                  blk_h, blk_h, blk_h],
        out_specs=blk_h,
        out_shape=jax.ShapeDtypeStruct((s_q, heads * LANES), out_dtype),
        scratch_shapes=[pltpu.VMEM((tq, LANES), F32)],
        compiler_params=_cparams(("parallel", "parallel", "arbitrary")),
    )(q, k, v, do, lse, delta)


def _flash_bwd_dkv(q, k, v, do, lse, delta, *, name, heads, qoff, koff, voff, causal, scale, tq, tk,
                   dk_dtype, dv_dtype):
    s_q, s_kv = q.shape[0], k.shape[0]
    tq, tk = min(tq, s_q), min(tk, s_kv)
    nq, nk = s_q // tq, s_kv // tk

    def body(q_ref, k_ref, v_ref, do_ref, lse_ref, dl_ref, dk_ref, dv_ref, dk_acc, dv_acc):
        j, i = pl.program_id(1), pl.program_id(2)

        @pl.when(i == 0)
        def _():
            dk_acc[...] = jnp.zeros_like(dk_acc)
            dv_acc[...] = jnp.zeros_like(dv_acc)

        def step(diag):
            s = _scores(q_ref[...], k_ref[...], scale, diag)
            p = jnp.exp(s - lse_ref[:, :1])
            dov = do_ref[...]
            dp = lax.dot_general(dov, v_ref[...], (((1,), (1,)), ((), ())), preferred_element_type=F32)
            ds = p * (dp - dl_ref[:, :1])
            dv_acc[...] += lax.dot_general(p.astype(BF16), dov, (((0,), (0,)), ((), ())),
                                           preferred_element_type=F32)
            dk_acc[...] += lax.dot_general(ds.astype(BF16), q_ref[...], (((0,), (0,)), ((), ())),
                                           preferred_element_type=F32)

        if causal:
            @pl.when(i > j)
            def _():
                step(False)

            @pl.when(i == j)
            def _():
                step(True)
        else:
            step(False)

        @pl.when(i == nq - 1)
        def _():
            dk_ref[...] = (dk_acc[...] * scale).astype(dk_dtype)
            dv_ref[...] = dv_acc[...].astype(dv_dtype)

    def q_idx(off):
        if causal:
            return lambda h, j, i: (jnp.maximum(i, j), off + h)
        return lambda h, j, i: (i, off + h)

    blk_h = pl.BlockSpec((tq, LANES), q_idx(0))
    out_k = pl.BlockSpec((tk, LANES), lambda h, j, i: (j, h))
    return pl.pallas_call(
        body, name=name, grid=(heads, nk, nq),
        in_specs=[pl.BlockSpec((tq, LANES), q_idx(qoff)),
                  pl.BlockSpec((tk, LANES), lambda h, j, i: (j, koff + h)),
                  pl.BlockSpec((tk, LANES), lambda h, j, i: (j, voff + h)),
                  blk_h, blk_h, blk_h],
        out_specs=[out_k, out_k],
        out_shape=[jax.ShapeDtypeStruct((s_kv, heads * LANES), dk_dtype),
                   jax.ShapeDtypeStruct((s_kv, heads * LANES), dv_dtype)],
        scratch_shapes=[pltpu.VMEM((tk, LANES), F32)] * 2,
        compiler_params=_cparams(("parallel", "parallel", "arbitrary")),
    )(q, k, v, do, lse, delta)


def _row_dot(a, b, *, name):
    s, n = a.shape
    nh = n // LANES
    ts = _rows(s, 512)

    def body(a_ref, b_ref, o_ref):
        for h in range(nh):
            sl = slice(h * LANES, (h + 1) * LANES)
            d = jnp.sum(a_ref[:, sl].astype(F32) * b_ref[:, sl].astype(F32), axis=1, keepdims=True)
            o_ref[:, sl] = jnp.broadcast_to(d, (ts, LANES))

    row = pl.BlockSpec((ts, n), lambda i: (i, 0))
    return pl.pallas_call(
        body, name=name, grid=(s // ts,), in_specs=[row, row], out_specs=row,
        out_shape=jax.ShapeDtypeStruct((s, n), F32), compiler_params=_cparams(("parallel",)),
    )(a, b)


CAUSAL_T = 512
LOG2E = 1.4426950408889634
LN2 = 0.6931471805599453


def _causal_fwd(q, k, v, *, name, heads, ones_lane):
    s = q.shape[0]
    t = CAUSAL_T
    nq = s // (2 * t)
    assert nq * 2 * t == s

    def body(q_ref, k_ref, v_ref, o_ref, lse_ref, v1, m_s, acc):
        i = pl.program_id(1)

        @pl.when(i == 0)
        def _():
            lane = lax.broadcasted_iota(jnp.int32, v1.shape, 1)
            v1[...] = jnp.where(lane == ones_lane, 1.0, v_ref[...]).astype(BF16)

        m_s[...] = jnp.full_like(m_s, NEG_INF)
        acc[...] = jnp.zeros_like(acc)
        halves = (q_ref[0:t, :], q_ref[t:2 * t, :])

        def raw(c, j):
            rows = pl.ds(pl.multiple_of(j * t, t), t)
            return lax.dot_general(halves[c], k_ref[rows, :], (((1,), (1,)), ((), ())), preferred_element_type=F32)

        def update(c, sc, j):
            m_prev = m_s[c]
            m_cur = jnp.maximum(m_prev, jnp.max(sc, axis=1, keepdims=True))
            p = jnp.exp2(sc - m_cur[:, :1]).astype(BF16)
            acc[c] = jnp.exp2(m_prev - m_cur) * acc[c] + jnp.dot(
                p, v1[pl.ds(pl.multiple_of(j * t, t), t), :], preferred_element_type=F32)
            m_s[c] = m_cur

        def loop(j, carry):
            sa, sb = raw(0, j), raw(1, j)
            update(0, sa, j)
            update(1, sb, j)
            return carry

        lax.fori_loop(0, 2 * i, loop, 0)
        sa, sb = raw(0, 2 * i), raw(1, 2 * i)
        below = (lax.broadcasted_iota(jnp.int32, sa.shape, 1) <= lax.broadcasted_iota(jnp.int32, sa.shape, 0))
        update(0, jnp.where(below, sa, NEG_INF), 2 * i)
        update(1, sb, 2 * i)
        update(1, jnp.where(below, raw(1, 2 * i + 1), NEG_INF), 2 * i + 1)
        lane = lax.broadcasted_iota(jnp.int32, (t, LANES), 1)
        for c in range(2):
            out = acc[c]
            den = out[:, ones_lane:ones_lane + 1]
            o_ref[c * t:(c + 1) * t, :] = jnp.where(lane == ones_lane, 0.0, out / den).astype(BF16)
            lse_ref[c * t:(c + 1) * t, :] = m_s[c] + jnp.log2(den)

    blk = pl.BlockSpec((2 * t, LANES), lambda h, i: (i, h))
    full = pl.BlockSpec((s, LANES), lambda h, i: (0, h))
    return pl.pallas_call(
        body, name=name, grid=(heads, nq), in_specs=[blk, full, full], out_specs=[blk, blk],
        out_shape=[jax.ShapeDtypeStruct((s, heads * LANES), BF16), jax.ShapeDtypeStruct((s, heads * LANES), F32)],
        scratch_shapes=[pltpu.VMEM((s, LANES), BF16), pltpu.VMEM((2, t, LANES), F32),
                        pltpu.VMEM((2, t, LANES), F32)],
        compiler_params=_cparams(("parallel", "arbitrary")),
    )(q, k, v)


def _causal_bwd(q, k, v, do, lse, delta, *, name, heads):
    s = q.shape[0]
    t = min(CAUSAL_T, s)
    nt = s // t

    def body(q_ref, k_ref, v_ref, do_ref, lse_ref, dl_ref, dq_ref, dk_ref, dv_ref, dk_acc, dv_acc):
        j = pl.program_id(1)

        @pl.when(j == 0)
        def _():
            dq_ref[...] = jnp.zeros_like(dq_ref)

        dk_acc[...] = jnp.zeros_like(dk_acc)
        dv_acc[...] = jnp.zeros_like(dv_acc)
        kv, vv = k_ref[...], v_ref[...]

        def step(i, diag):
            rows = pl.ds(pl.multiple_of(i * t, t), t)
            qv, dov = q_ref[rows, :], do_ref[rows, :]
            sc = _scores(qv, kv, 1.0, diag)
            p = jnp.exp2(sc - lse_ref[rows, :][:, :1])
            dp = lax.dot_general(dov, vv, (((1,), (1,)), ((), ())), preferred_element_type=F32)
            ds = (p * (dp - dl_ref[rows, :][:, :1])).astype(BF16)
            dv_acc[...] += lax.dot_general(p.astype(BF16), dov, (((0,), (0,)), ((), ())),
                                           preferred_element_type=F32)
            dk_acc[...] += lax.dot_general(ds, qv, (((0,), (0,)), ((), ())), preferred_element_type=F32)
            dq_ref[rows, :] += jnp.dot(ds, kv, preferred_element_type=F32)

        step(j, True)

        def loop(i, carry):
            step(i, False)
            return carry

        lax.fori_loop(j + 1, nt, loop, 0)
        dk_ref[...] = dk_acc[...] * LN2
        dv_ref[...] = dv_acc[...].astype(BF16)

    blk = pl.BlockSpec((t, LANES), lambda h, j: (j, h))
    full = pl.BlockSpec((s, LANES), lambda h, j: (0, h))
    return pl.pallas_call(
        body, name=name, grid=(heads, nt), in_specs=[full, blk, blk, full, full, full],
        out_specs=[full, blk, blk],
        out_shape=[jax.ShapeDtypeStruct((s, heads * LANES), F32), jax.ShapeDtypeStruct((s, heads * LANES), F32),
                   jax.ShapeDtypeStruct((s, heads * LANES), BF16)],
        scratch_shapes=[pltpu.VMEM((t, LANES), F32)] * 2,
        compiler_params=_cparams(("parallel", "arbitrary")),
    )(q, k, v, do, lse, delta)


def _band_masks(dilation, slope):
    qi = lax.broadcasted_iota(jnp.int32, (BLOCK, 2 * BLOCK), 0)
    kj = lax.broadcasted_iota(jnp.int32, (BLOCK, 2 * BLOCK), 1)
    dist = qi + BLOCK - kj
    valid = (dist >= 0) & (dist <= BLOCK)
    bias = -slope * (dist * dilation).astype(F32)
    return valid, bias


BAND_UNROLL = 4


def _aligned(start):
    return start if isinstance(start, int) else pl.multiple_of(start, BLOCK)


def _band_fwd(q, k, v, slopes, *, name, dilation, qoff, koff, voff):
    s = q.shape[0]
    sub = s // dilation
    nb = sub // BLOCK
    assert nb * BLOCK == sub
    unroll = min(BAND_UNROLL, nb)
    assert nb % unroll == 0
    scale = LANES ** -0.5

    def body(sl_ref, q_ref, k_ref, v_ref, o_ref, lse_ref):
        slope = sl_ref[pl.program_id(0)]
        valid2, bias2 = _band_masks(dilation, slope)
        valid1, bias1 = valid2[:, BLOCK:], bias2[:, BLOCK:]

        def block(start_q, kk, vv, valid, bias):
            qb = q_ref[pl.ds(start_q, BLOCK), :]
            sc = lax.dot_general(qb, kk, (((1,), (1,)), ((), ())), preferred_element_type=F32) * scale
            sc = jnp.where(valid, sc + bias, NEG_INF)
            m = jnp.max(sc, axis=1, keepdims=True)
            e = jnp.exp(sc - m)
            den = jnp.sum(e, axis=1, keepdims=True)
            p = (e / den).astype(BF16)
            o_ref[pl.ds(start_q, BLOCK), :] = jnp.dot(p, vv, preferred_element_type=F32)
            lse_ref[pl.ds(start_q, BLOCK), :] = jnp.broadcast_to(m + jnp.log(den), (BLOCK, LANES))

        block(0, k_ref[0:BLOCK, :], v_ref[0:BLOCK, :], valid1, bias1)

        def general(jj):
            start_q, start_k = _aligned(jj * BLOCK), _aligned((jj - 1) * BLOCK)
            block(start_q, k_ref[pl.ds(start_k, 2 * BLOCK), :], v_ref[pl.ds(start_k, 2 * BLOCK), :], valid2, bias2)

        for jj in range(1, unroll):
            general(jj)

        def loop(t, carry):
            for u in range(unroll):
                general(t * unroll + u)
            return carry

        lax.fori_loop(1, nb // unroll, loop, 0)

    def spec(off):
        return pl.BlockSpec((sub, LANES), lambda h, r: (r, off + h))

    out = pl.BlockSpec((sub, LANES), lambda h, r: (r, h))
    return pl.pallas_call(
        body, name=name, grid=(DIL_HEADS_PER_GROUP, dilation),
        in_specs=[pl.BlockSpec(memory_space=pltpu.SMEM), spec(qoff), spec(koff), spec(voff)],
        out_specs=[out, out],
        out_shape=[jax.ShapeDtypeStruct((s, DIL_HEADS_PER_GROUP * LANES), F32)] * 2,
        compiler_params=_cparams(("parallel", "parallel"), VMEM_LIMIT_BIG),
    )(slopes, q, k, v)


def _band_bwd(q, k, v, do, lse, delta, slopes, *, name, dilation, qoff, koff, voff):
    s = q.shape[0]
    sub = s // dilation
    nb = sub // BLOCK
    unroll = min(BAND_UNROLL, nb)
    scale = LANES ** -0.5

    def body(sl_ref, q_ref, k_ref, v_ref, do_ref, lse_ref, dl_ref, dq_ref, dk_ref, dv_ref):
        slope = sl_ref[pl.program_id(0)]
        valid2, bias2 = _band_masks(dilation, slope)
        valid1, bias1 = valid2[:, BLOCK:], bias2[:, BLOCK:]

        def block(start_q, kk, vv, valid, bias):
            qb = q_ref[pl.ds(start_q, BLOCK), :]
            dob = do_ref[pl.ds(start_q, BLOCK), :]
            sc = lax.dot_general(qb, kk, (((1,), (1,)), ((), ())), preferred_element_type=F32) * scale
            sc = jnp.where(valid, sc + bias, NEG_INF)
            p = jnp.exp(sc - lse_ref[pl.ds(start_q, BLOCK), :][:, :1])
            dp = lax.dot_general(dob, vv, (((1,), (1,)), ((), ())), preferred_element_type=F32)
            ds = (p * (dp - dl_ref[pl.ds(start_q, BLOCK), :][:, :1])).astype(BF16)
            dq = jnp.dot(ds, kk, preferred_element_type=F32) * scale
            dq_ref[pl.ds(start_q, BLOCK), :] = dq.astype(BF16)
            dkk = lax.dot_general(ds, qb, (((0,), (0,)), ((), ())), preferred_element_type=F32) * scale
            dvv = lax.dot_general(p.astype(BF16), dob, (((0,), (0,)), ((), ())), preferred_element_type=F32)
            return dkk, dvv

        carry0 = block(0, k_ref[0:BLOCK, :], v_ref[0:BLOCK, :], valid1, bias1)

        def general(jj, carry):
            dk_part, dv_part = carry
            start_q, start_k = _aligned(jj * BLOCK), _aligned((jj - 1) * BLOCK)
            dkk, dvv = block(start_q, k_ref[pl.ds(start_k, 2 * BLOCK), :], v_ref[pl.ds(start_k, 2 * BLOCK), :],
                             valid2, bias2)
            dk_ref[pl.ds(start_k, BLOCK), :] = (dk_part + dkk[:BLOCK]).astype(BF16)
            dv_ref[pl.ds(start_k, BLOCK), :] = (dv_part + dvv[:BLOCK]).astype(BF16)
            return dkk[BLOCK:], dvv[BLOCK:]

        for jj in range(1, unroll):
            carry0 = general(jj, carry0)

        def loop(t, carry):
            for u in range(unroll):
                carry = general(t * unroll + u, carry)
            return carry

        dk_last, dv_last = lax.fori_loop(1, nb // unroll, loop, carry0)
        dk_ref[(nb - 1) * BLOCK:nb * BLOCK, :] = dk_last.astype(BF16)
        dv_ref[(nb - 1) * BLOCK:nb * BLOCK, :] = dv_last.astype(BF16)

    def spec(off):
        return pl.BlockSpec((sub, LANES), lambda h, r: (r, off + h))

    out = spec(0)
    return pl.pallas_call(
        body, name=name, grid=(DIL_HEADS_PER_GROUP, dilation),
        in_specs=[pl.BlockSpec(memory_space=pltpu.SMEM), spec(qoff), spec(koff), spec(voff), out, out, out],
        out_specs=[out, out, out],
        out_shape=[jax.ShapeDtypeStruct((s, DIL_HEADS_PER_GROUP * LANES), BF16)] * 3,
        compiler_params=_cparams(("parallel", "parallel"), VMEM_LIMIT_BIG),
    )(slopes, q, k, v, do, lse, delta)


def _mix_fwd(outs, lses, *, name):
    s, n = outs[0].shape
    ts = _rows(s, 512)

    def body(o0, o1, o2, l0, l1, l2, y_ref):
        la, lb, lc = l0[...], l1[...], l2[...]
        m = jnp.maximum(jnp.maximum(la, lb), lc)
        ea, eb, ec = jnp.exp(la - m), jnp.exp(lb - m), jnp.exp(lc - m)
        den = ea + eb + ec
        y = (ea / den) * o0[...] + (eb / den) * o1[...] + (ec / den) * o2[...]
        y_ref[...] = y.astype(BF16)

    row = pl.BlockSpec((ts, n), lambda i: (i, 0))
    return pl.pallas_call(
        body, name=name, grid=(s // ts,), in_specs=[row] * 6, out_specs=row,
        out_shape=jax.ShapeDtypeStruct((s, n), BF16), compiler_params=_cparams(("parallel",)),
    )(*outs, *lses)


def _mix_bwd(dy, outs, lses, *, name):
    s, n = dy.shape
    nh = n // LANES
    ts = _rows(s, 256)

    def body(dy_ref, o0, o1, o2, l0, l1, l2, d0, d1, d2, e0, e1, e2):
        la, lb, lc = l0[...], l1[...], l2[...]
        m = jnp.maximum(jnp.maximum(la, lb), lc)
        ea, eb, ec = jnp.exp(la - m), jnp.exp(lb - m), jnp.exp(lc - m)
        den = ea + eb + ec
        wa, wb, wc = ea / den, eb / den, ec / den
        dyv = dy_ref[...]
        y = wa * o0[...] + wb * o1[...] + wc * o2[...]
        prod = dyv * y
        d0[...] = (wa * dyv).astype(BF16)
        d1[...] = (wb * dyv).astype(BF16)
        d2[...] = (wc * dyv).astype(BF16)
        for h in range(nh):
            sl = slice(h * LANES, (h + 1) * LANES)
            t = jnp.sum(prod[:, sl], axis=1, keepdims=True)
            e0[:, sl] = wa[:, sl] * t
            e1[:, sl] = wb[:, sl] * t
            e2[:, sl] = wc[:, sl] * t

    row = pl.BlockSpec((ts, n), lambda i: (i, 0))
    return pl.pallas_call(
        body, name=name, grid=(s // ts,), in_specs=[row] * 7, out_specs=[row] * 6,
        out_shape=[jax.ShapeDtypeStruct((s, n), BF16)] * 3 + [jax.ShapeDtypeStruct((s, n), F32)] * 3,
        compiler_params=_cparams(("parallel",)),
    )(dy, *outs, *lses)


def _gate_fwd(gp, b_gate, branches, *, name):
    s = gp.shape[0]
    ts = _rows(s, 256)

    def body(gp_ref, b_ref, b0, b1, b2, o_ref):
        tot = None
        for i, br in enumerate((b0, b1, b2)):
            sl = slice(i * D_MODEL, (i + 1) * D_MODEL)
            t = jax.nn.sigmoid(gp_ref[:, sl] + b_ref[:, sl]) * br[...]
            tot = t if tot is None else tot + t
        o_ref[...] = tot.astype(BF16)

    row = pl.BlockSpec((ts, D_MODEL), lambda i: (i, 0))
    return pl.pallas_call(
        body, name=name, grid=(s // ts,),
        in_specs=[pl.BlockSpec((ts, 3 * D_MODEL), lambda i: (i, 0)), pl.BlockSpec((1, 3 * D_MODEL), lambda i: (0, 0)),
                  row, row, row],
        out_specs=row, out_shape=jax.ShapeDtypeStruct((s, D_MODEL), BF16),
        compiler_params=_cparams(("parallel",)),
    )(gp, b_gate, *branches)


def _gate_bwd(dm, gp, b_gate, branches, *, name):
    s = gp.shape[0]
    ts = _rows(s, 256)

    def body(dm_ref, gp_ref, b_ref, b0, b1, b2, d0, d1, d2, dgp_ref, db_ref):
        i = pl.program_id(0)

        @pl.when(i == 0)
        def _():
            db_ref[...] = jnp.zeros_like(db_ref)

        dmv = dm_ref[...]
        for k, (br, dbr) in enumerate(((b0, d0), (b1, d1), (b2, d2))):
            sl = slice(k * D_MODEL, (k + 1) * D_MODEL)
            sg = jax.nn.sigmoid(gp_ref[:, sl] + b_ref[:, sl])
            dbr[...] = (dmv * sg).astype(BF16)
            dg = dmv * br[...] * sg * (1.0 - sg)
            dgp_ref[:, sl] = dg.astype(BF16)
            db_ref[:, sl] += jnp.sum(dg, axis=0, keepdims=True)

    row = pl.BlockSpec((ts, D_MODEL), lambda i: (i, 0))
    wide = pl.BlockSpec((ts, 3 * D_MODEL), lambda i: (i, 0))
    vec = pl.BlockSpec((1, 3 * D_MODEL), lambda i: (0, 0))
    return pl.pallas_call(
        body, name=name, grid=(s // ts,),
        in_specs=[row, wide, vec, row, row, row], out_specs=[row, row, row, wide, vec],
        out_shape=[jax.ShapeDtypeStruct((s, D_MODEL), BF16)] * 3
        + [jax.ShapeDtypeStruct((s, 3 * D_MODEL), BF16), jax.ShapeDtypeStruct((1, 3 * D_MODEL), F32)],
        compiler_params=_cparams(("arbitrary",)),
    )(dm, gp, b_gate, *branches)


CONV_TC = 1408


def _shift_down(x, halo, k):
    rolled = pltpu.roll(x, k, 0)
    r8 = lax.broadcasted_iota(jnp.int32, halo.shape, 0)
    top = jnp.where(r8 < k, pltpu.roll(halo, k, 0), rolled[:SUBLANES])
    return jnp.concatenate([top, rolled[SUBLANES:]], axis=0)


def _shift_up(x, halo, k):
    n = x.shape[0]
    rolled = pltpu.roll(x, n - k, 0)
    r8 = lax.broadcasted_iota(jnp.int32, halo.shape, 0)
    bot = jnp.where(r8 >= SUBLANES - k, pltpu.roll(halo, SUBLANES - k, 0), rolled[n - SUBLANES:])
    return jnp.concatenate([rolled[:n - SUBLANES], bot], axis=0)


def _conv_fwd(u, conv_w, conv_b, *, name):
    s = u.shape[0]
    ts = _rows(s, 256)
    nct = D_FF // CONV_TC
    per8 = ts // SUBLANES

    def body(ug, uv, hg, hv, wg, wv, bg, bv, zg_ref, zv_ref, a_ref):
        first = pl.program_id(1) == 0

        def conv(u_ref, h_ref, w_ref, b_ref):
            x = u_ref[...]
            halo = jnp.where(first, 0.0, h_ref[...])
            z = b_ref[...] + w_ref[0:1, :] * _shift_down(x, halo, 2)
            z = z + w_ref[1:2, :] * _shift_down(x, halo, 1)
            return z + w_ref[2:3, :] * x

        zg = conv(ug, hg, wg, bg)
        zv = conv(uv, hv, wv, bv)
        zg_ref[...] = zg
        zv_ref[...] = zv
        a_ref[...] = (zg * jax.nn.sigmoid(zg) * zv).astype(BF16)

    def col(off):
        return pl.BlockSpec((ts, CONV_TC), lambda c, i: (i, c + off))

    def halo(off):
        return pl.BlockSpec((SUBLANES, CONV_TC), lambda c, i: (jnp.maximum(i * per8 - 1, 0), c + off))

    def wspec(rows, off):
        return pl.BlockSpec((rows, CONV_TC), lambda c, i: (0, c + off))

    zg, zv, a = pl.pallas_call(
        body, name=name, grid=(nct, s // ts),
        in_specs=[col(0), col(nct), halo(0), halo(nct), wspec(3, 0), wspec(3, nct), wspec(1, 0), wspec(1, nct)],
        out_specs=[col(0), col(0), col(0)],
        out_shape=[jax.ShapeDtypeStruct((s, D_FF), F32)] * 2 + [jax.ShapeDtypeStruct((s, D_FF), BF16)],
        compiler_params=_cparams(("parallel", "parallel")),
    )(u, u, u, u, conv_w, conv_w, conv_b, conv_b)
    return zg, zv, a


def _conv_bwd(da, zg, zv, u, conv_w, *, name):
    s = da.shape[0]
    ts = _rows(s, 256)
    nct = D_FF // CONV_TC
    per8 = ts // SUBLANES
    nrow = s // ts
    last8 = s // SUBLANES - 1

    def dz_of(dav, g, val):
        sg = jax.nn.sigmoid(g)
        return dav * val * sg * (1.0 + g * (1.0 - sg)), dav * g * sg

    def body(da_ref, zg_ref, zv_ref, da_nx, zg_nx, zv_nx, ug_ref, uv_ref, ug_pv, uv_pv, wg_ref, wv_ref,
             dug_ref, duv_ref, accg_ref, accv_ref):
        i = pl.program_id(1)
        dzg, dzv = dz_of(da_ref[...], zg_ref[...], zv_ref[...])
        da_next = jnp.where(i == nrow - 1, 0.0, da_nx[...])
        nxg, nxv = dz_of(da_next, zg_nx[...], zv_nx[...])

        @pl.when(i == 0)
        def _():
            accg_ref[...] = jnp.zeros_like(accg_ref)
            accv_ref[...] = jnp.zeros_like(accv_ref)

        for dz, nxt, u_ref, pv_ref, w_ref, du_ref, acc_ref in (
                (dzg, nxg, ug_ref, ug_pv, wg_ref, dug_ref, accg_ref),
                (dzv, nxv, uv_ref, uv_pv, wv_ref, duv_ref, accv_ref)):
            du = w_ref[2:3, :] * dz + w_ref[1:2, :] * _shift_up(dz, nxt, 1) + w_ref[0:1, :] * _shift_up(dz, nxt, 2)
            du_ref[...] = du.astype(BF16)
            x = u_ref[...]
            prev = jnp.where(i == 0, 0.0, pv_ref[...])
            acc_ref[0:1, :] += jnp.sum(dz * _shift_down(x, prev, 2), axis=0, keepdims=True)
            acc_ref[1:2, :] += jnp.sum(dz * _shift_down(x, prev, 1), axis=0, keepdims=True)
            acc_ref[2:3, :] += jnp.sum(dz * x, axis=0, keepdims=True)
            acc_ref[3:4, :] += jnp.sum(dz, axis=0, keepdims=True)

    def blk(off):
        return pl.BlockSpec((ts, CONV_TC), lambda c, i: (i, c + off))

    def nxt8(off):
        return pl.BlockSpec((SUBLANES, CONV_TC), lambda c, i: (jnp.minimum((i + 1) * per8, last8), c + off))

    def prv8(off):
        return pl.BlockSpec((SUBLANES, CONV_TC), lambda c, i: (jnp.maximum(i * per8 - 1, 0), c + off))

    def wspec(off):
        return pl.BlockSpec((3, CONV_TC), lambda c, i: (0, c + off))

    acc = pl.BlockSpec((SUBLANES, CONV_TC), lambda c, i: (0, c))
    return pl.pallas_call(
        body, name=name, grid=(nct, nrow),
        in_specs=[blk(0), blk(0), blk(0), nxt8(0), nxt8(0), nxt8(0), blk(0), blk(nct), prv8(0), prv8(nct),
                  wspec(0), wspec(nct)],
        out_specs=[blk(0), blk(0), acc, acc],
        out_shape=[jax.ShapeDtypeStruct((s, D_FF), BF16)] * 2 + [jax.ShapeDtypeStruct((SUBLANES, D_FF), F32)] * 2,
        compiler_params=_cparams(("parallel", "arbitrary")),
    )(da, zg, zv, da, zg, zv, u, u, u, u, conv_w, conv_w)


def _peer(k):
    x, y, c = lax.axis_index("x"), lax.axis_index("y"), lax.axis_index("c")
    px = 1 - x if k & 4 else x
    py = 1 - y if k & 2 else y
    pc = 1 - c if k & 1 else c
    return (px, py, pc), 4 * px + 2 * py + pc


def _exchange(bufs, *, name, gather):
    n = len(bufs)
    npeer = N_DEV - 1

    def body(*refs):
        srcs, outs = refs[:n], refs[n:2 * n]
        send_sems, recv_sems, local_sems = refs[2 * n:]
        _, me = _peer(0)
        mine = [src if gather else src.at[me] for src in srcs]
        local = [pltpu.make_async_copy(mine[p], outs[p].at[me], local_sems.at[p]) for p in range(n)]
        for cp in local:
            cp.start()
        sends = []
        for k in range(1, N_DEV):
            dev, idx = _peer(k)
            for p in range(n):
                cp = pltpu.make_async_remote_copy(
                    src_ref=srcs[p] if gather else srcs[p].at[idx], dst_ref=outs[p].at[me],
                    send_sem=send_sems.at[p * npeer + k - 1], recv_sem=recv_sems.at[p * npeer + k - 1],
                    device_id=dev, device_id_type=pl.DeviceIdType.MESH)
                cp.start()
                sends.append(cp)
        for k in range(1, N_DEV):
            dev, idx = _peer(k)
            for p in range(n):
                pltpu.make_async_remote_copy(
                    src_ref=mine[p], dst_ref=outs[p].at[idx],
                    send_sem=send_sems.at[p * npeer + k - 1], recv_sem=recv_sems.at[p * npeer + k - 1],
                    device_id=dev, device_id_type=pl.DeviceIdType.MESH).wait_recv()
        for cp in sends:
            cp.wait_send()
        for cp in local:
            cp.wait()

    any_spec = pl.BlockSpec(memory_space=pl.ANY)
    return pl.pallas_call(
        body, name=name,
        in_specs=[any_spec] * n, out_specs=[any_spec] * n,
        out_shape=[jax.ShapeDtypeStruct((N_DEV,) + b.shape[-2:], b.dtype) for b in bufs],
        scratch_shapes=[pltpu.SemaphoreType.DMA((n * npeer,)), pltpu.SemaphoreType.DMA((n * npeer,)),
                        pltpu.SemaphoreType.DMA((n,))],
    )(*bufs)


def _adamw(parts, w, m, v, *, name):
    rows, width = w.shape
    tr = _rows(rows, max(16, (128 * 1024) // width), mult=16)

    def body(p_ref, w_ref, m_ref, v_ref, g_ref, d_ref, nm_ref, nv_ref):
        g = p_ref[0].astype(F32)
        for k in range(1, N_DEV):
            g = g + p_ref[k].astype(F32)
        mn = ADAM_B1 * m_ref[...] + (1.0 - ADAM_B1) * g
        vn = ADAM_B2 * v_ref[...] + (1.0 - ADAM_B2) * jnp.square(g)
        m_hat = mn / (1.0 - ADAM_B1 ** ADAM_STEP)
        v_hat = vn / (1.0 - ADAM_B2 ** ADAM_STEP)
        g_ref[...] = g
        d_ref[...] = -ADAM_LR * (m_hat / (jnp.sqrt(v_hat) + ADAM_EPS) + ADAM_WD * w_ref[...])
        nm_ref[...] = mn
        nv_ref[...] = vn

    row = pl.BlockSpec((tr, width), lambda i: (i, 0))
    return pl.pallas_call(
        body, name=name, grid=(rows // tr,),
        in_specs=[pl.BlockSpec((N_DEV, tr, width), lambda i: (0, i, 0)), row, row, row],
        out_specs=[row] * 4, out_shape=[jax.ShapeDtypeStruct((rows, width), F32)] * 4,
        compiler_params=_cparams(("parallel",)),
    )(parts, w, m, v)


def _pack_replicated(blocks):
    flat = jnp.concatenate([blocks[name].reshape(-1).astype(F32) for name, _, _ in REPLICATED])
    flat = jnp.pad(flat, (0, PACK_ROWS * PACK_W - PACK_USED))
    return flat.reshape(PACK_ROWS, PACK_W)


def _unpack_replicated(buf):
    flat = buf.reshape(-1)
    return {name: flat[off:off + n].reshape(1, n) for name, (off, n) in PACK_TABLE.items()}


def _join_shards(seg, shape, axis):
    return seg.reshape(shape) if axis == 0 else seg.transpose(1, 0, 2).reshape(shape)


def _split_shards(g, shape, axis):
    r, c = shape
    if axis == 0:
        return g.reshape(N_DEV, r // N_DEV, c)
    return g.reshape(r, N_DEV, c // N_DEV).transpose(1, 0, 2)


def _to_residues(a, d):
    s, c = a.shape
    return a.reshape(s // d, d, c).transpose(1, 0, 2).reshape(s, c)


def _from_residues(a, d):
    s, c = a.shape
    return a.reshape(d, s // d, c).transpose(1, 0, 2).reshape(s, c)


def _pad_heads(w, heads, width, lo, hi):
    r = w.shape[0]
    w = w.reshape(r, heads, width)[:, :, lo:hi]
    w = jnp.pad(w, ((0, 0), (0, 0), (0, LANES - (hi - lo))))
    return w.reshape(r, heads * LANES)


def _local_step(x, mem, positions, target, w):
    s = x.shape[0]
    bf = lambda a: a.astype(BF16)

    w_in = w["w_in"]
    kr_cols = jnp.pad(w_in[:, OFF_KV:OFF_KR], ((0, 0), (MLA_NOPE, LANES - MLA_QK_DIM)))
    w_a = bf(jnp.concatenate([w_in[:, :OFF_KV], kr_cols], axis=1))
    w_dm = bf(w_in[:, OFF_KR:OFF_MEMQ])
    w_g = bf(w_in[:, OFF_MEMQ:])
    w_in_t = jnp.concatenate([w_a, w_dm, w_g], axis=1).T
    wq = bf(_pad_heads(w["w_uq"], MLA_HEADS, MLA_QK_DIM, 0, MLA_QK_DIM))
    wk = bf(_pad_heads(w["w_ukv"], MLA_HEADS, MLA_NOPE + MLA_V, 0, MLA_NOPE))
    wv = bf(_pad_heads(w["w_ukv"], MLA_HEADS, MLA_NOPE + MLA_V, MLA_NOPE, MLA_NOPE + MLA_V))
    w_mkv = bf(w["w_mem_kv"])
    wb_mla = bf(jnp.pad(w["w_br_mla"].reshape(MLA_HEADS, MLA_V, D_MODEL),
                        ((0, 0), (0, LANES - MLA_V), (0, 0))).reshape(MLA_HEADS * LANES, D_MODEL))
    wb_dil, wb_mem, w_o = bf(w["w_br_dil"]), bf(w["w_br_mem"]), bf(w["w_o"])
    w_up, w_down = bf(w["w_ffn_up"]), bf(w["w_ffn_down"])
    tabs = _rope_tables(positions)
    slopes = [jnp.asarray(sl, F32) for sl in DIL_SLOPES]
    mla_scale = MLA_QK_DIM ** -0.5
    mem_scale = LANES ** -0.5
    MQ = 3 * DIL_HEADS

    h = _rms_fwd(x, w["g_pre_mix"], name="rms_pre_mix", out_dtype=BF16)
    p_a = _matmul(h, w_a, name="proj_a")
    p_dm = _matmul(h, w_dm, name="proj_dm", out_dtype=BF16)
    p_g = _matmul(h, w_g, name="proj_gate")
    c_q, c_kv, kr = p_a[:, :OFF_Q], p_a[:, OFF_Q:OFF_KV], p_a[:, OFF_KV:]

    qn = _rms_fwd(c_q, w["mla_q_norm"], name="rms_q", out_dtype=BF16)
    kvn = _rms_fwd(c_kv, w["mla_kv_norm"], name="rms_kv", out_dtype=BF16)
    q_raw = _matmul(qn, wq, name="mla_q_up")
    k_raw = _matmul(kvn, wk, name="mla_k_up")
    v_f = _matmul(kvn, wv, name="mla_v_up", out_dtype=BF16)
    q_f = _rope_fwd(q_raw, tabs, name="rope_q", scale=mla_scale * LOG2E)
    k_f = _rope_fwd(k_raw, tabs, name="rope_k", scale=1.0, add=kr)
    o_mla, lse_mla = _causal_fwd(q_f, k_f, v_f, name="mla_fwd", heads=MLA_HEADS, ones_lane=MLA_V)

    dil_in, dil_o, dil_lse = [], [], []
    for g, (_, d) in enumerate(DIL_PAIRS):
        if d == 1:
            arrs, offs = (p_dm, p_dm, p_dm), (4 * g, DIL_HEADS + 4 * g, 2 * DIL_HEADS + 4 * g)
        else:
            arrs = tuple(_to_residues(p_dm[:, (t * DIL_HEADS + 4 * g) * LANES:(t * DIL_HEADS + 4 * g + 4) * LANES], d)
                         for t in range(3))
            offs = (0, 0, 0)
        o_g, lse_g = _band_fwd(*arrs, slopes[g], name=f"dil_fwd_{g}", dilation=d,
                               qoff=offs[0], koff=offs[1], voff=offs[2])
        dil_in.append((arrs, offs))
        dil_o.append(_from_residues(o_g, d))
        dil_lse.append(_from_residues(lse_g, d))
    y_dil = _mix_fwd(dil_o, dil_lse, name="dil_mix")

    memn = _rms_fwd(mem, w["g_mem"], name="rms_mem", out_dtype=BF16)
    kv_m = _matmul(memn, w_mkv, name="mem_kv", out_dtype=BF16)
    memat = dict(heads=MEM_HEADS, qoff=MQ, koff=0, voff=MEM_HEADS, causal=False, scale=mem_scale, tq=512, tk=256)
    o_mem, lse_mem = _flash_fwd(p_dm, kv_m, kv_m, name="mem_fwd", **memat)

    b_mla = _matmul(o_mla, wb_mla, name="br_mla")
    b_dil = _matmul(y_dil, wb_dil, name="br_dil")
    b_mem = _matmul(o_mem, wb_mem, name="br_mem")
    merged = _gate_fwd(p_g, w["b_gate"], (b_mla, b_dil, b_mem), name="gate_fwd")
    z1 = _matmul(merged, w_o, name="out_proj")
    x1 = _rms_fwd(z1, w["g_post_mix"], name="rms_post_mix", out_dtype=F32, add=x)
    h2 = _rms_fwd(x1, w["g_pre_ffn"], name="rms_pre_ffn", out_dtype=BF16)
    u = _matmul(h2, w_up, name="ffn_up")
    zg, zv, act = _conv_fwd(u, w["conv_w"], w["conv_b"], name="conv_fwd")
    f = _matmul(act, w_down, name="ffn_down")
    dy, sq = _loss_fwd(x1, f, target, w["g_post_ffn"], name="loss")
    loss = 0.5 * jnp.sum(sq) / D_MODEL

    grads = {}
    df, grads["g_post_ffn"] = _rms_bwd(f, w["g_post_ffn"], dy, name="rms_post_ffn_bwd", out_dtype=BF16)
    da = _matmul(df, w_down.T, name="ffn_down_dx")
    grads["w_ffn_down"] = _matmul(act, df, name="ffn_down_dw", trans_a=True)
    du_g, du_v, cacc_g, cacc_v = _conv_bwd(da, zg, zv, u, w["conv_w"], name="conv_bwd")
    grads["conv_w"] = jnp.concatenate([cacc_g[0:3], cacc_v[0:3]], axis=1)
    grads["conv_b"] = jnp.concatenate([cacc_g[3:4], cacc_v[3:4]], axis=1)
    w_up_t = w_up.T
    dh2 = _matmul(du_g, w_up_t[:D_FF], name="ffn_up_dx_gate")
    dh2 = _matmul(du_v, w_up_t[D_FF:], name="ffn_up_dx_val", add=dh2)
    grads["w_ffn_up"] = jnp.concatenate([_matmul(h2, du_g, name="ffn_up_dw_gate", trans_a=True),
                                         _matmul(h2, du_v, name="ffn_up_dw_val", trans_a=True)], axis=1)
    dx1, grads["g_pre_ffn"] = _rms_bwd(x1, w["g_pre_ffn"], dh2, name="rms_pre_ffn_bwd", out_dtype=F32, add=dy)
    dz1, grads["g_post_mix"] = _rms_bwd(z1, w["g_post_mix"], dx1, name="rms_post_mix_bwd", out_dtype=BF16)
    dmerged = _matmul(dz1, w_o.T, name="out_proj_dx")
    grads["w_o"] = _matmul(merged, dz1, name="out_proj_dw", trans_a=True)
    db_mla, db_dil, db_mem, dgp, grads["b_gate"] = _gate_bwd(
        dmerged, p_g, w["b_gate"], (b_mla, b_dil, b_mem), name="gate_bwd")

    do_mla = _matmul(db_mla, wb_mla.T, name="br_mla_dx", out_dtype=BF16)
    g_wb_mla = _matmul(o_mla, db_mla, name="br_mla_dw", trans_a=True)
    grads["w_br_mla"] = g_wb_mla.reshape(MLA_HEADS, LANES, D_MODEL)[:, :MLA_V].reshape(MLA_HEADS * MLA_V, D_MODEL)
    delta_mla = _row_dot(do_mla, o_mla, name="mla_delta")
    dq_f, dk_f, dv_f = _causal_bwd(q_f, k_f, v_f, do_mla, lse_mla, delta_mla, name="mla_bwd", heads=MLA_HEADS)
    dq_raw = _rope_bwd(dq_f, tabs, name="rope_q_bwd", scale=mla_scale, with_add=False)
    dk_raw, dkr = _rope_bwd(dk_f, tabs, name="rope_k_bwd", scale=1.0, with_add=True)
    dqn = _matmul(dq_raw, wq.T, name="mla_q_up_dx")
    g_wq = _matmul(qn, dq_raw, name="mla_q_up_dw", trans_a=True)
    grads["w_uq"] = g_wq.reshape(MLA_Q_RANK, MLA_HEADS, LANES)[:, :, :MLA_QK_DIM].reshape(MLA_Q_RANK, -1)
    dkvn = _matmul(dk_raw, wk.T, name="mla_k_up_dx")
    dkvn = _matmul(dv_f, wv.T, name="mla_v_up_dx", add=dkvn)
    g_wk = _matmul(kvn, dk_raw, name="mla_k_up_dw", trans_a=True).reshape(MLA_KV_RANK, MLA_HEADS, LANES)
    g_wv = _matmul(kvn, dv_f, name="mla_v_up_dw", trans_a=True).reshape(MLA_KV_RANK, MLA_HEADS, LANES)
    grads["w_ukv"] = jnp.concatenate([g_wk[:, :, :MLA_NOPE], g_wv[:, :, :MLA_V]], axis=2).reshape(MLA_KV_RANK, -1)
    dc_q, grads["mla_q_norm"] = _rms_bwd(c_q, w["mla_q_norm"], dqn, name="rms_q_bwd", out_dtype=BF16)
    dc_kv, grads["mla_kv_norm"] = _rms_bwd(c_kv, w["mla_kv_norm"], dkvn, name="rms_kv_bwd", out_dtype=BF16)

    dy_dil = _matmul(db_dil, wb_dil.T, name="br_dil_dx")
    grads["w_br_dil"] = _matmul(y_dil, db_dil, name="br_dil_dw", trans_a=True)
    mix = _mix_bwd(dy_dil, dil_o, dil_lse, name="dil_mix_bwd")
    d_dil = [[None] * 3 for _ in range(3)]
    for g, (_, d) in enumerate(DIL_PAIRS):
        arrs, offs = dil_in[g]
        do_g, dl_g, lse_g = mix[g], mix[3 + g], dil_lse[g]
        if d != 1:
            do_g, dl_g, lse_g = _to_residues(do_g, d), _to_residues(dl_g, d), _to_residues(lse_g, d)
        dq_g, dk_g, dv_g = _band_bwd(*arrs, do_g, lse_g, dl_g, slopes[g], name=f"dil_bwd_{g}", dilation=d,
                                     qoff=offs[0], koff=offs[1], voff=offs[2])
        for t, a in enumerate((dq_g, dk_g, dv_g)):
            d_dil[t][g] = a if d == 1 else _from_residues(a, d)

    do_mem = _matmul(db_mem, wb_mem.T, name="br_mem_dx", out_dtype=BF16)
    grads["w_br_mem"] = _matmul(o_mem, db_mem, name="br_mem_dw", trans_a=True)
    delta_mem = _row_dot(do_mem, o_mem, name="mem_delta")
    dq_mem = _flash_bwd_dq(p_dm, kv_m, kv_m, do_mem, lse_mem, delta_mem, name="mem_bwd_dq", out_dtype=BF16, **memat)
    dk_mem, dv_mem = _flash_bwd_dkv(p_dm, kv_m, kv_m, do_mem, lse_mem, delta_mem, name="mem_bwd_dkv",
                                    dk_dtype=BF16, dv_dtype=BF16, **memat)
    dkv_m = jnp.concatenate([dk_mem, dv_mem], axis=1)
    dmemn = _matmul(dkv_m, w_mkv.T, name="mem_kv_dx")
    grads["w_mem_kv"] = _matmul(memn, dkv_m, name="mem_kv_dw", trans_a=True)
    _, grads["g_mem"] = _rms_bwd(mem, w["g_mem"], dmemn, name="rms_mem_bwd", out_dtype=BF16)

    dp_all = jnp.concatenate([dc_q, dc_kv, bf(dkr)] + d_dil[0] + d_dil[1] + d_dil[2] + [dq_mem, dgp], axis=1)
    dh = _matmul(dp_all, w_in_t, name="proj_dx")
    g_in = _matmul(h, dp_all, name="proj_dw", trans_a=True)
    grads["w_in"] = jnp.concatenate(
        [g_in[:, :OFF_KV], g_in[:, OFF_KV + MLA_NOPE:OFF_KV + MLA_QK_DIM], g_in[:, N_A:]], axis=1)
    dx, grads["g_pre_mix"] = _rms_bwd(x, w["g_pre_mix"], dh, name="rms_pre_mix_bwd", out_dtype=F32, add=dx1)
    return loss, dx, grads


def kernel(x, mem, positions, g_pre_mix, w_in, b_gate, mla_q_norm, w_uq, mla_kv_norm, w_ukv, g_mem, w_mem_kv, w_br_mla, w_br_dil, w_br_mem, w_o, g_post_mix, g_pre_ffn, w_ffn_up, conv_w, conv_b, w_ffn_down, g_post_ffn, loss_target, m_g_pre_mix, m_w_in, m_b_gate, m_mla_q_norm, m_w_uq, m_mla_kv_norm, m_w_ukv, m_g_mem, m_w_mem_kv, m_w_br_mla, m_w_br_dil, m_w_br_mem, m_w_o, m_g_post_mix, m_g_pre_ffn, m_w_ffn_up, m_conv_w, m_conv_b, m_w_ffn_down, m_g_post_ffn, v_g_pre_mix, v_w_in, v_b_gate, v_mla_q_norm, v_w_uq, v_mla_kv_norm, v_w_ukv, v_g_mem, v_w_mem_kv, v_w_br_mla, v_w_br_dil, v_w_br_mem, v_w_o, v_g_post_mix, v_g_pre_ffn, v_w_ffn_up, v_conv_w, v_conv_b, v_w_ffn_down, v_g_post_ffn):
    local = dict(g_pre_mix=g_pre_mix, w_in=w_in, b_gate=b_gate, mla_q_norm=mla_q_norm, w_uq=w_uq,
                 mla_kv_norm=mla_kv_norm, w_ukv=w_ukv, g_mem=g_mem, w_mem_kv=w_mem_kv, w_br_mla=w_br_mla,
                 w_br_dil=w_br_dil, w_br_mem=w_br_mem, w_o=w_o, g_post_mix=g_post_mix, g_pre_ffn=g_pre_ffn,
                 w_ffn_up=w_ffn_up, conv_w=conv_w, conv_b=conv_b, w_ffn_down=w_ffn_down, g_post_ffn=g_post_ffn)
    mom_m = dict(g_pre_mix=m_g_pre_mix, w_in=m_w_in, b_gate=m_b_gate, mla_q_norm=m_mla_q_norm, w_uq=m_w_uq,
                 mla_kv_norm=m_mla_kv_norm, w_ukv=m_w_ukv, g_mem=m_g_mem, w_mem_kv=m_w_mem_kv, w_br_mla=m_w_br_mla,
                 w_br_dil=m_w_br_dil, w_br_mem=m_w_br_mem, w_o=m_w_o, g_post_mix=m_g_post_mix,
                 g_pre_ffn=m_g_pre_ffn, w_ffn_up=m_w_ffn_up, conv_w=m_conv_w, conv_b=m_conv_b,
                 w_ffn_down=m_w_ffn_down, g_post_ffn=m_g_post_ffn)
    mom_v = dict(g_pre_mix=v_g_pre_mix, w_in=v_w_in, b_gate=v_b_gate, mla_q_norm=v_mla_q_norm, w_uq=v_w_uq,
                 mla_kv_norm=v_mla_kv_norm, w_ukv=v_w_ukv, g_mem=v_g_mem, w_mem_kv=v_w_mem_kv, w_br_mla=v_w_br_mla,
                 w_br_dil=v_w_br_dil, w_br_mem=v_w_br_mem, w_o=v_w_o, g_post_mix=v_g_post_mix,
                 g_pre_ffn=v_g_pre_ffn, w_ffn_up=v_w_ffn_up, conv_w=v_conv_w, conv_b=v_conv_b,
                 w_ffn_down=v_w_ffn_down, g_post_ffn=v_g_post_ffn)

    shards = [local[name][0].astype(F32 if name == "conv_w" else BF16) for name, _, _ in SHARDED]
    gathered = _exchange(shards, name="gather_weights", gather=True)
    full = {name: _join_shards(seg, shape, axis) for (name, shape, axis), seg in zip(SHARDED, gathered)}
    for name, _, _ in REPLICATED:
        full[name] = local[name].reshape(1, -1)

    loss, dx, grads = _local_step(x[0], mem[0], positions[0], loss_target[0], full)

    slabs = [_split_shards(grads[name].astype(F32 if name == "conv_w" else BF16), shape, axis)
             for name, shape, axis in SHARDED]
    parts = _exchange(slabs, name="exchange_grads", gather=False)
    rep_parts = _exchange([_pack_replicated(grads)], name="gather_replicated_grads", gather=True)[0]

    results = {}
    for (name, _, _), part in zip(SHARDED, parts):
        res = _adamw(part, local[name][0], mom_m[name][0], mom_v[name][0], name="adamw_" + name)
        results[name] = [r[None] for r in res]
    rep = _adamw(rep_parts, _pack_replicated(local), _pack_replicated(mom_m), _pack_replicated(mom_v),
                 name="adamw_replicated")
    for i, buf in enumerate(rep):
        for name, val in _unpack_replicated(buf).items():
            results.setdefault(name, [None] * 4)[i] = val

    loss = lax.psum(loss, ("x", "y", "c"))
    outs = [loss, dx[None]]
    for i in range(4):
        outs.extend(results[name][i] for name in PARAM_NAMES)
    return tuple(outs)
```

```python
import functools

import numpy as np
import jax
import jax.numpy as jnp
from jax import lax
from jax.experimental import pallas as pl
from jax.experimental.pallas import tpu as pltpu

F32 = jnp.float32
BF16 = jnp.bfloat16

N_DEV = 8
D_MODEL = 1024
RMS_EPS = 1e-6
NEG_INF = -1e30
LANES = 128
SUBLANES = 8
BLOCK = 128

MLA_HEADS = 8
MLA_NOPE = 64
MLA_ROPE = 32
MLA_V = 64
MLA_QK_DIM = MLA_NOPE + MLA_ROPE
MLA_Q_RANK = 384
MLA_KV_RANK = 256
ROPE_THETA = 10000.0
DIL_PAIRS = ((128, 1), (512, 4), (2048, 16))
DIL_HEADS_PER_GROUP = 4
DIL_HEADS = 12
MEM_HEADS = 4
D_FF = 2816
OFF_Q = MLA_Q_RANK
OFF_KV = OFF_Q + MLA_KV_RANK
OFF_KR = OFF_KV + MLA_ROPE
OFF_DIL = OFF_KR + 3 * DIL_HEADS * LANES
OFF_MEMQ = OFF_DIL + MEM_HEADS * LANES
D_IN = OFF_MEMQ + 3 * D_MODEL
N_A = OFF_KV + LANES
N_DM = 3 * DIL_HEADS * LANES + MEM_HEADS * LANES

ADAM_LR = 0.001
ADAM_B1 = 0.9
ADAM_B2 = 0.999
ADAM_EPS = 1e-08
ADAM_WD = 0.01
ADAM_STEP = 10

VMEM_LIMIT = 48 * 1024 * 1024
VMEM_LIMIT_BIG = 58 * 1024 * 1024
PACK_W = 1024

_ALIBI_BASE = np.exp2(-8.0 * np.arange(1, DIL_HEADS + 1) / DIL_HEADS)
DIL_SLOPES = [[float(_ALIBI_BASE[hh * 3 + g]) for hh in range(DIL_HEADS_PER_GROUP)] for g in range(3)]

PARAMS = (
    ("g_pre_mix", (1024,), None), ("w_in", (1024, D_IN), 1), ("b_gate", (3072,), None),
    ("mla_q_norm", (384,), None), ("w_uq", (384, 768), 1), ("mla_kv_norm", (256,), None),
    ("w_ukv", (256, 1024), 1), ("g_mem", (1024,), None), ("w_mem_kv", (1024, 1024), 0),
    ("w_br_mla", (512, 1024), 1), ("w_br_dil", (512, 1024), 1), ("w_br_mem", (512, 1024), 1),
    ("w_o", (1024, 1024), 0), ("g_post_mix", (1024,), None), ("g_pre_ffn", (1024,), None),
    ("w_ffn_up", (1024, 2 * D_FF), 1), ("conv_w", (3, 2 * D_FF), 1), ("conv_b", (2 * D_FF,), None),
    ("w_ffn_down", (D_FF, 1024), 0), ("g_post_ffn", (1024,), None),
)
PARAM_NAMES = tuple(p[0] for p in PARAMS)


def _shard_shape(shape, axis):
    if axis is None:
        return shape
    return tuple(s // N_DEV if a == axis else s for a, s in enumerate(shape))


SHARDED = tuple(p for p in PARAMS if p[2] is not None)
REPLICATED = tuple(p for p in PARAMS if p[2] is None)


def _layout():
    off, table = 0, {}
    for name, shape, _ in REPLICATED:
        table[name] = (off, shape[0])
        off += shape[0]
    rows = -(-off // PACK_W)
    rows = -(-rows // SUBLANES) * SUBLANES
    return table, off, rows


PACK_TABLE, PACK_USED, PACK_ROWS = _layout()


def _pick(n, cap):
    best = None
    for t in range(LANES, min(n, cap) + 1, LANES):
        if n % t == 0:
            best = t
    return best if best is not None else n


def _rows(n, cap, mult=SUBLANES):
    best = None
    for t in range(mult, min(n, cap) + 1, mult):
        if n % t == 0:
            best = t
    return best if best is not None else n


def _cparams(sem, vmem=VMEM_LIMIT):
    return pltpu.CompilerParams(dimension_semantics=sem, vmem_limit_bytes=vmem)


def _matmul(a, b, *, name, out_dtype=F32, trans_a=False, add=None, tm=1024, tn=1408, tk=640):
    if trans_a:
        kc, m = a.shape
    else:
        m, kc = a.shape
    n = b.shape[1]
    assert b.shape[0] == kc
    tm, tn, tk = _pick(m, tm), _pick(n, tn), _pick(kc, tk)
    nk = kc // tk

    def body(*refs):
        if add is None:
            a_ref, b_ref, o_ref, acc = refs
        else:
            a_ref, b_ref, c_ref, o_ref, acc = refs
        k = pl.program_id(2)

        @pl.when(k == 0)
        def _():
            if add is None:
                acc[...] = jnp.zeros_like(acc)
            else:
                acc[...] = c_ref[...].astype(F32)

        av = a_ref[...].astype(BF16)
        bv = b_ref[...].astype(BF16)
        if trans_a:
            acc[...] += lax.dot_general(av, bv, (((0,), (0,)), ((), ())), preferred_element_type=F32)
        else:
            acc[...] += jnp.dot(av, bv, preferred_element_type=F32)

        @pl.when(k == nk - 1)
        def _():
            o_ref[...] = acc[...].astype(out_dtype)

    if trans_a:
        a_spec = pl.BlockSpec((tk, tm), lambda i, j, k: (k, i))
    else:
        a_spec = pl.BlockSpec((tm, tk), lambda i, j, k: (i, k))
    in_specs = [a_spec, pl.BlockSpec((tk, tn), lambda i, j, k: (k, j))]
    args = [a, b]
    if add is not None:
        in_specs.append(pl.BlockSpec((tm, tn), lambda i, j, k: (i, j)))
        args.append(add)
    return pl.pallas_call(
        body, name=name, grid=(m // tm, n // tn, nk),
        in_specs=in_specs, out_specs=pl.BlockSpec((tm, tn), lambda i, j, k: (i, j)),
        out_shape=jax.ShapeDtypeStruct((m, n), out_dtype),
        scratch_shapes=[pltpu.VMEM((tm, tn), F32)],
        compiler_params=_cparams(("parallel", "parallel", "arbitrary")),
    )(*args)


def _rms_fwd(x, g, *, name, out_dtype, add=None):
    s, n = x.shape
    ts = _rows(s, 512)

    def body(*refs):
        if add is None:
            x_ref, g_ref, o_ref = refs
        else:
            x_ref, g_ref, a_ref, o_ref = refs
        xv = x_ref[...]
        r = lax.rsqrt(jnp.mean(xv * xv, axis=-1, keepdims=True) + RMS_EPS)
        y = xv * r * g_ref[...]
        if add is not None:
            y = a_ref[...] + y
        o_ref[...] = y.astype(out_dtype)

    row = pl.BlockSpec((ts, n), lambda i: (i, 0))
    in_specs = [row, pl.BlockSpec((1, n), lambda i: (0, 0))]
    args = [x, g]
    if add is not None:
        in_specs.append(row)
        args.append(add)
    return pl.pallas_call(
        body, name=name, grid=(s // ts,), in_specs=in_specs, out_specs=row,
        out_shape=jax.ShapeDtypeStruct((s, n), out_dtype),
        compiler_params=_cparams(("parallel",)),
    )(*args)


def _rms_bwd(x, g, dy, *, name, out_dtype, add=None):
    s, n = x.shape
    ts = _rows(s, 512)

    def body(*refs):
        if add is None:
            x_ref, g_ref, dy_ref, dx_ref, dg_ref = refs
        else:
            x_ref, g_ref, dy_ref, a_ref, dx_ref, dg_ref = refs
        i = pl.program_id(0)
        xv = x_ref[...]
        dyv = dy_ref[...].astype(F32)
        r = lax.rsqrt(jnp.mean(xv * xv, axis=-1, keepdims=True) + RMS_EPS)
        nx = xv * r
        gdy = dyv * g_ref[...]
        dx = r * (gdy - nx * jnp.mean(nx * gdy, axis=-1, keepdims=True))
        if add is not None:
            dx = a_ref[...] + dx
        dx_ref[...] = dx.astype(out_dtype)

        @pl.when(i == 0)
        def _():
            dg_ref[...] = jnp.zeros_like(dg_ref)

        dg_ref[...] += jnp.sum(dyv * nx, axis=0, keepdims=True)

    row = pl.BlockSpec((ts, n), lambda i: (i, 0))
    vec = pl.BlockSpec((1, n), lambda i: (0, 0))
    in_specs = [row, vec, row]
    args = [x, g, dy]
    if add is not None:
        in_specs.append(row)
        args.append(add)
    return pl.pallas_call(
        body, name=name, grid=(s // ts,), in_specs=in_specs, out_specs=[row, vec],
        out_shape=[jax.ShapeDtypeStruct((s, n), out_dtype), jax.ShapeDtypeStruct((1, n), F32)],
        compiler_params=_cparams(("arbitrary",)),
    )(*args)


def _loss_fwd(x1, f, target, g, *, name):
    s, n = x1.shape
    ts = _rows(s, 512)

    def body(x_ref, f_ref, t_ref, g_ref, dy_ref, sq_ref):
        i = pl.program_id(0)
        fv = f_ref[...]
        r = lax.rsqrt(jnp.mean(fv * fv, axis=-1, keepdims=True) + RMS_EPS)
        err = x_ref[...] + fv * r * g_ref[...] - t_ref[...]
        dy_ref[...] = err * (1.0 / n)

        @pl.when(i == 0)
        def _():
            sq_ref[...] = jnp.zeros_like(sq_ref)

        sq_ref[...] += jnp.sum(err * err, axis=0, keepdims=True)

    row = pl.BlockSpec((ts, n), lambda i: (i, 0))
    vec = pl.BlockSpec((1, n), lambda i: (0, 0))
    return pl.pallas_call(
        body, name=name, grid=(s // ts,), in_specs=[row, row, row, vec], out_specs=[row, vec],
        out_shape=[jax.ShapeDtypeStruct((s, n), F32), jax.ShapeDtypeStruct((1, n), F32)],
        compiler_params=_cparams(("arbitrary",)),
    )(x1, f, target, g)


def _rope_tables(positions):
    half = MLA_ROPE // 2
    inv_freq = ROPE_THETA ** (-jnp.arange(half, dtype=F32) / half)
    ang = positions.astype(F32)[:, None] * inv_freq[None, :]
    cos, sin = jnp.cos(ang), jnp.sin(ang)
    s = positions.shape[0]
    one = jnp.ones((s, MLA_NOPE), F32)
    zero = jnp.zeros((s, MLA_NOPE), F32)
    pad1 = jnp.ones((s, LANES - MLA_QK_DIM), F32)
    pad0 = jnp.zeros((s, LANES - MLA_QK_DIM), F32)
    zh = jnp.zeros((s, half), F32)
    c_tab = jnp.concatenate([one, cos, cos, pad1], axis=1)
    s1_tab = jnp.concatenate([zero, -sin, zh, pad0], axis=1)
    s2_tab = jnp.concatenate([zero, zh, sin, pad0], axis=1)
    return c_tab, s1_tab, s2_tab


def _rope_fwd(x, tabs, *, name, scale, add=None):
    s, n = x.shape
    nh = n // LANES
    ts = _rows(s, 512)
    half = MLA_ROPE // 2

    def body(*refs):
        if add is None:
            x_ref, c_ref, s1_ref, s2_ref, o_ref = refs
        else:
            x_ref, a_ref, c_ref, s1_ref, s2_ref, o_ref = refs
        c, s1, s2 = c_ref[...], s1_ref[...], s2_ref[...]
        for h in range(nh):
            xh = x_ref[:, h * LANES:(h + 1) * LANES]
            if add is not None:
                xh = xh + a_ref[...]
            y = xh * c + pltpu.roll(xh, LANES - half, 1) * s1 + pltpu.roll(xh, half, 1) * s2
            o_ref[:, h * LANES:(h + 1) * LANES] = (y * scale).astype(BF16)

    row = pl.BlockSpec((ts, n), lambda i: (i, 0))
    tab = pl.BlockSpec((ts, LANES), lambda i: (i, 0))
    in_specs = [row] + ([tab] if add is not None else []) + [tab, tab, tab]
    args = [x] + ([add] if add is not None else []) + list(tabs)
    return pl.pallas_call(
        body, name=name, grid=(s // ts,), in_specs=in_specs, out_specs=row,
        out_shape=jax.ShapeDtypeStruct((s, n), BF16),
        compiler_params=_cparams(("parallel",)),
    )(*args)


def _rope_bwd(dy, tabs, *, name, scale, with_add):
    s, n = dy.shape
    nh = n // LANES
    ts = _rows(s, 512)
    half = MLA_ROPE // 2

    def body(*refs):
        if with_add:
            dy_ref, c_ref, s1_ref, s2_ref, dx_ref, da_ref = refs
        else:
            dy_ref, c_ref, s1_ref, s2_ref, dx_ref = refs
        c, s1, s2 = c_ref[...], s1_ref[...], s2_ref[...]
        tot = None
        for h in range(nh):
            g = dy_ref[:, h * LANES:(h + 1) * LANES].astype(F32)
            dx = (g * c + pltpu.roll(g * s1, half, 1) + pltpu.roll(g * s2, LANES - half, 1)) * scale
            dx_ref[:, h * LANES:(h + 1) * LANES] = dx.astype(BF16)
            tot = dx if tot is None else tot + dx
        if with_add:
            da_ref[...] = tot

    row = pl.BlockSpec((ts, n), lambda i: (i, 0))
    tab = pl.BlockSpec((ts, LANES), lambda i: (i, 0))
    out_specs = [row, tab] if with_add else row
    out_shape = [jax.ShapeDtypeStruct((s, n), BF16)]
    if with_add:
        out_shape.append(jax.ShapeDtypeStruct((s, LANES), F32))
    else:
        out_shape = out_shape[0]
    return pl.pallas_call(
        body, name=name, grid=(s // ts,), in_specs=[row, tab, tab, tab], out_specs=out_specs,
        out_shape=out_shape, compiler_params=_cparams(("parallel",)),
    )(dy, *tabs)


def _scores(q, k, scale, diag):
    s = lax.dot_general(q, k, (((1,), (1,)), ((), ())), preferred_element_type=F32)
    if scale != 1.0:
        s = s * scale
    if diag:
        rows = lax.broadcasted_iota(jnp.int32, s.shape, 0)
        cols = lax.broadcasted_iota(jnp.int32, s.shape, 1)
        s = jnp.where(cols <= rows, s, NEG_INF)
    return s


def _flash_fwd(q, k, v, *, name, heads, qoff, koff, voff, causal, scale, tq, tk):
    s_q, s_kv = q.shape[0], k.shape[0]
    tq, tk = min(tq, s_q), min(tk, s_kv)
    nq, nk = s_q // tq, s_kv // tk
    if causal:
        assert tq == tk and s_q == s_kv

    def body(q_ref, k_ref, v_ref, o_ref, lse_ref, m_s, l_s, acc):
        i, j = pl.program_id(1), pl.program_id(2)

        @pl.when(j == 0)
        def _():
            m_s[...] = jnp.full_like(m_s, NEG_INF)
            l_s[...] = jnp.zeros_like(l_s)
            acc[...] = jnp.zeros_like(acc)

        def step(diag):
            s = _scores(q_ref[...], k_ref[...], scale, diag)
            m_prev = m_s[...]
            m_cur = jnp.maximum(m_prev, jnp.max(s, axis=1, keepdims=True))
            alpha = jnp.exp(m_prev - m_cur)
            p = jnp.exp(s - m_cur[:, :1])
            l_s[...] = alpha * l_s[...] + jnp.sum(p, axis=1, keepdims=True)
            acc[...] = alpha * acc[...] + jnp.dot(p.astype(BF16), v_ref[...], preferred_element_type=F32)
            m_s[...] = m_cur

        def finish():
            o_ref[...] = (acc[...] / l_s[...]).astype(o_ref.dtype)
            lse_ref[...] = m_s[...] + jnp.log(l_s[...])

        if causal:
            @pl.when(j < i)
            def _():
                step(False)

            @pl.when(j == i)
            def _():
                step(True)
                finish()
        else:
            step(False)

            @pl.when(j == nk - 1)
            def _():
                finish()

    def kv_idx(off):
        if causal:
            return lambda h, i, j: (jnp.minimum(j, i), off + h)
        return lambda h, i, j: (j, off + h)

    blk_q = pl.BlockSpec((tq, LANES), lambda h, i, j: (i, qoff + h))
    out_q = pl.BlockSpec((tq, LANES), lambda h, i, j: (i, h))
    return pl.pallas_call(
        body, name=name, grid=(heads, nq, nk),
        in_specs=[blk_q, pl.BlockSpec((tk, LANES), kv_idx(koff)), pl.BlockSpec((tk, LANES), kv_idx(voff))],
        out_specs=[out_q, out_q],
        out_shape=[jax.ShapeDtypeStruct((s_q, heads * LANES), BF16),
                   jax.ShapeDtypeStruct((s_q, heads * LANES), F32)],
        scratch_shapes=[pltpu.VMEM((tq, LANES), F32)] * 3,
        compiler_params=_cparams(("parallel", "parallel", "arbitrary")),
    )(q, k, v)


def _flash_bwd_dq(q, k, v, do, lse, delta, *, name, heads, qoff, koff, voff, causal, scale, tq, tk, out_dtype):
    s_q, s_kv = q.shape[0], k.shape[0]
    tq, tk = min(tq, s_q), min(tk, s_kv)
    nq, nk = s_q // tq, s_kv // tk

    def body(q_ref, k_ref, v_ref, do_ref, lse_ref, dl_ref, dq_ref, acc):
        i, j = pl.program_id(1), pl.program_id(2)

        @pl.when(j == 0)
        def _():
            acc[...] = jnp.zeros_like(acc)

        def step(diag):
            s = _scores(q_ref[...], k_ref[...], scale, diag)
            p = jnp.exp(s - lse_ref[:, :1])
            dp = lax.dot_general(do_ref[...], v_ref[...], (((1,), (1,)), ((), ())), preferred_element_type=F32)
            ds = p * (dp - dl_ref[:, :1])
            acc[...] += jnp.dot(ds.astype(BF16), k_ref[...], preferred_element_type=F32)

        def finish():
            dq_ref[...] = (acc[...] * scale).astype(out_dtype)

        if causal:
            @pl.when(j < i)
            def _():
                step(False)

            @pl.when(j == i)
            def _():
                step(True)
                finish()
        else:
            step(False)

            @pl.when(j == nk - 1)
            def _():
                finish()

    def kv_idx(off):
        if causal:
            return lambda h, i, j: (jnp.minimum(j, i), off + h)
        return lambda h, i, j: (j, off + h)

    blk_q = pl.BlockSpec((tq, LANES), lambda h, i, j: (i, qoff + h))
    blk_h = pl.BlockSpec((tq, LANES), lambda h, i, j: (i, h))
    return pl.pallas_call(
        body, name=name, grid=(heads, nq, nk),
        in_specs=[blk_q, pl.BlockSpec((tk, LANES), kv_idx(koff)), pl.BlockSpec((tk, LANES), kv_idx(voff)),
                  blk_h, blk_h, blk_h],
        out_specs=blk_h,
        out_shape=jax.ShapeDtypeStruct((s_q, heads * LANES), out_dtype),
        scratch_shapes=[pltpu.VMEM((tq, LANES), F32)],
        compiler_params=_cparams(("parallel", "parallel", "arbitrary")),
    )(q, k, v, do, lse, delta)


def _flash_bwd_dkv(q, k, v, do, lse, delta, *, name, heads, qoff, koff, voff, causal, scale, tq, tk,
                   dk_dtype, dv_dtype):
    s_q, s_kv = q.shape[0], k.shape[0]
    tq, tk = min(tq, s_q), min(tk, s_kv)
    nq, nk = s_q // tq, s_kv // tk

    def body(q_ref, k_ref, v_ref, do_ref, lse_ref, dl_ref, dk_ref, dv_ref, dk_acc, dv_acc):
        j, i = pl.program_id(1), pl.program_id(2)

        @pl.when(i == 0)
        def _():
            dk_acc[...] = jnp.zeros_like(dk_acc)
            dv_acc[...] = jnp.zeros_like(dv_acc)

        def step(diag):
            s = _scores(q_ref[...], k_ref[...], scale, diag)
            p = jnp.exp(s - lse_ref[:, :1])
            dov = do_ref[...]
            dp = lax.dot_general(dov, v_ref[...], (((1,), (1,)), ((), ())), preferred_element_type=F32)
            ds = p * (dp - dl_ref[:, :1])
            dv_acc[...] += lax.dot_general(p.astype(BF16), dov, (((0,), (0,)), ((), ())),
                                           preferred_element_type=F32)
            dk_acc[...] += lax.dot_general(ds.astype(BF16), q_ref[...], (((0,), (0,)), ((), ())),
                                           preferred_element_type=F32)

        if causal:
            @pl.when(i > j)
            def _():
                step(False)

            @pl.when(i == j)
            def _():
                step(True)
        else:
            step(False)

        @pl.when(i == nq - 1)
        def _():
            dk_ref[...] = (dk_acc[...] * scale).astype(dk_dtype)
            dv_ref[...] = dv_acc[...].astype(dv_dtype)

    def q_idx(off):
        if causal:
            return lambda h, j, i: (jnp.maximum(i, j), off + h)
        return lambda h, j, i: (i, off + h)

    blk_h = pl.BlockSpec((tq, LANES), q_idx(0))
    out_k = pl.BlockSpec((tk, LANES), lambda h, j, i: (j, h))
    return pl.pallas_call(
        body, name=name, grid=(heads, nk, nq),
        in_specs=[pl.BlockSpec((tq, LANES), q_idx(qoff)),
                  pl.BlockSpec((tk, LANES), lambda h, j, i: (j, koff + h)),
                  pl.BlockSpec((tk, LANES), lambda h, j, i: (j, voff + h)),
                  blk_h, blk_h, blk_h],
        out_specs=[out_k, out_k],
        out_shape=[jax.ShapeDtypeStruct((s_kv, heads * LANES), dk_dtype),
                   jax.ShapeDtypeStruct((s_kv, heads * LANES), dv_dtype)],
        scratch_shapes=[pltpu.VMEM((tk, LANES), F32)] * 2,
        compiler_params=_cparams(("parallel", "parallel", "arbitrary")),
    )(q, k, v, do, lse, delta)


def _row_dot(a, b, *, name):
    s, n = a.shape
    nh = n // LANES
    ts = _rows(s, 512)

    def body(a_ref, b_ref, o_ref):
        for h in range(nh):
            sl = slice(h * LANES, (h + 1) * LANES)
            d = jnp.sum(a_ref[:, sl].astype(F32) * b_ref[:, sl].astype(F32), axis=1, keepdims=True)
            o_ref[:, sl] = jnp.broadcast_to(d, (ts, LANES))

    row = pl.BlockSpec((ts, n), lambda i: (i, 0))
    return pl.pallas_call(
        body, name=name, grid=(s // ts,), in_specs=[row, row], out_specs=row,
        out_shape=jax.ShapeDtypeStruct((s, n), F32), compiler_params=_cparams(("parallel",)),
    )(a, b)


CAUSAL_T = 512
LOG2E = 1.4426950408889634
LN2 = 0.6931471805599453


def _causal_fwd(q, k, v, *, name, heads, ones_lane):
    s = q.shape[0]
    t = CAUSAL_T
    nq = s // (2 * t)
    assert nq * 2 * t == s

    def body(q_ref, k_ref, v_ref, o_ref, lse_ref, v1, m_s, acc):
        i = pl.program_id(1)

        @pl.when(i == 0)
        def _():
            lane = lax.broadcasted_iota(jnp.int32, v1.shape, 1)
            v1[...] = jnp.where(lane == ones_lane, 1.0, v_ref[...]).astype(BF16)

        m_s[...] = jnp.full_like(m_s, NEG_INF)
        acc[...] = jnp.zeros_like(acc)
        halves = (q_ref[0:t, :], q_ref[t:2 * t, :])

        def raw(c, j):
            rows = pl.ds(pl.multiple_of(j * t, t), t)
            return lax.dot_general(halves[c], k_ref[rows, :], (((1,), (1,)), ((), ())), preferred_element_type=F32)

        def update(c, sc, j):
            m_prev = m_s[c]
            m_cur = jnp.maximum(m_prev, jnp.max(sc, axis=1, keepdims=True))
            p = jnp.exp2(sc - m_cur[:, :1]).astype(BF16)
            acc[c] = jnp.exp2(m_prev - m_cur) * acc[c] + jnp.dot(
                p, v1[pl.ds(pl.multiple_of(j * t, t), t), :], preferred_element_type=F32)
            m_s[c] = m_cur

        def loop(j, carry):
            sa, sb = raw(0, j), raw(1, j)
            update(0, sa, j)
            update(1, sb, j)
            return carry

        lax.fori_loop(0, 2 * i, loop, 0)
        sa, sb = raw(0, 2 * i), raw(1, 2 * i)
        below = (lax.broadcasted_iota(jnp.int32, sa.shape, 1) <= lax.broadcasted_iota(jnp.int32, sa.shape, 0))
        update(0, jnp.where(below, sa, NEG_INF), 2 * i)
        update(1, sb, 2 * i)
        update(1, jnp.where(below, raw(1, 2 * i + 1), NEG_INF), 2 * i + 1)
        lane = lax.broadcasted_iota(jnp.int32, (t, LANES), 1)
        for c in range(2):
            out = acc[c]
            den = out[:, ones_lane:ones_lane + 1]
            o_ref[c * t:(c + 1) * t, :] = jnp.where(lane == ones_lane, 0.0, out / den).astype(BF16)
            lse_ref[c * t:(c + 1) * t, :] = m_s[c] + jnp.log2(den)

    blk = pl.BlockSpec((2 * t, LANES), lambda h, i: (i, h))
    full = pl.BlockSpec((s, LANES), lambda h, i: (0, h))
    return pl.pallas_call(
        body, name=name, grid=(heads, nq), in_specs=[blk, full, full], out_specs=[blk, blk],
        out_shape=[jax.ShapeDtypeStruct((s, heads * LANES), BF16), jax.ShapeDtypeStruct((s, heads * LANES), F32)],
        scratch_shapes=[pltpu.VMEM((s, LANES), BF16), pltpu.VMEM((2, t, LANES), F32),
                        pltpu.VMEM((2, t, LANES), F32)],
        compiler_params=_cparams(("parallel", "arbitrary")),
    )(q, k, v)


def _causal_bwd(q, k, v, do, lse, delta, *, name, heads):
    s = q.shape[0]
    t = min(CAUSAL_T, s)
    nt = s // t

    def body(q_ref, k_ref, v_ref, do_ref, lse_ref, dl_ref, dq_ref, dk_ref, dv_ref, dk_acc, dv_acc):
        j = pl.program_id(1)

        @pl.when(j == 0)
        def _():
            dq_ref[...] = jnp.zeros_like(dq_ref)

        dk_acc[...] = jnp.zeros_like(dk_acc)
        dv_acc[...] = jnp.zeros_like(dv_acc)
        kv, vv = k_ref[...], v_ref[...]

        def step(i, diag):
            rows = pl.ds(pl.multiple_of(i * t, t), t)
            qv, dov = q_ref[rows, :], do_ref[rows, :]
            sc = _scores(qv, kv, 1.0, diag)
            p = jnp.exp2(sc - lse_ref[rows, :][:, :1])
            dp = lax.dot_general(dov, vv, (((1,), (1,)), ((), ())), preferred_element_type=F32)
            ds = (p * (dp - dl_ref[rows, :][:, :1])).astype(BF16)
            dv_acc[...] += lax.dot_general(p.astype(BF16), dov, (((0,), (0,)), ((), ())),
                                           preferred_element_type=F32)
            dk_acc[...] += lax.dot_general(ds, qv, (((0,), (0,)), ((), ())), preferred_element_type=F32)
            dq_ref[rows, :] += jnp.dot(ds, kv, preferred_element_type=F32)

        step(j, True)

        def loop(i, carry):
            step(i, False)
            return carry

        lax.fori_loop(j + 1, nt, loop, 0)
        dk_ref[...] = dk_acc[...] * LN2
        dv_ref[...] = dv_acc[...].astype(BF16)

    blk = pl.BlockSpec((t, LANES), lambda h, j: (j, h))
    full = pl.BlockSpec((s, LANES), lambda h, j: (0, h))
    return pl.pallas_call(
        body, name=name, grid=(heads, nt), in_specs=[full, blk, blk, full, full, full],
        out_specs=[full, blk, blk],
        out_shape=[jax.ShapeDtypeStruct((s, heads * LANES), F32), jax.ShapeDtypeStruct((s, heads * LANES), F32),
                   jax.ShapeDtypeStruct((s, heads * LANES), BF16)],
        scratch_shapes=[pltpu.VMEM((t, LANES), F32)] * 2,
        compiler_params=_cparams(("parallel", "arbitrary")),
    )(q, k, v, do, lse, delta)


def _band_masks(dilation, slope):
    qi = lax.broadcasted_iota(jnp.int32, (BLOCK, 2 * BLOCK), 0)
    kj = lax.broadcasted_iota(jnp.int32, (BLOCK, 2 * BLOCK), 1)
    dist = qi + BLOCK - kj
    valid = (dist >= 0) & (dist <= BLOCK)
    bias = -slope * (dist * dilation).astype(F32)
    return valid, bias


BAND_UNROLL = 4


def _aligned(start):
    return start if isinstance(start, int) else pl.multiple_of(start, BLOCK)


def _band_fwd(q, k, v, slopes, *, name, dilation, qoff, koff, voff):
    s = q.shape[0]
    sub = s // dilation
    nb = sub // BLOCK
    assert nb * BLOCK == sub
    unroll = min(BAND_UNROLL, nb)
    assert nb % unroll == 0
    scale = LANES ** -0.5

    def body(sl_ref, q_ref, k_ref, v_ref, o_ref, lse_ref):
        slope = sl_ref[pl.program_id(0)]
        valid2, bias2 = _band_masks(dilation, slope)
        valid1, bias1 = valid2[:, BLOCK:], bias2[:, BLOCK:]

        def block(start_q, kk, vv, valid, bias):
            qb = q_ref[pl.ds(start_q, BLOCK), :]
            sc = lax.dot_general(qb, kk, (((1,), (1,)), ((), ())), preferred_element_type=F32) * scale
            sc = jnp.where(valid, sc + bias, NEG_INF)
            m = jnp.max(sc, axis=1, keepdims=True)
            e = jnp.exp(sc - m)
            den = jnp.sum(e, axis=1, keepdims=True)
            p = (e / den).astype(BF16)
            o_ref[pl.ds(start_q, BLOCK), :] = jnp.dot(p, vv, preferred_element_type=F32)
            lse_ref[pl.ds(start_q, BLOCK), :] = jnp.broadcast_to(m + jnp.log(den), (BLOCK, LANES))

        block(0, k_ref[0:BLOCK, :], v_ref[0:BLOCK, :], valid1, bias1)

        def general(jj):
            start_q, start_k = _aligned(jj * BLOCK), _aligned((jj - 1) * BLOCK)
            block(start_q, k_ref[pl.ds(start_k, 2 * BLOCK), :], v_ref[pl.ds(start_k, 2 * BLOCK), :], valid2, bias2)

        for jj in range(1, unroll):
            general(jj)

        def loop(t, carry):
            for u in range(unroll):
                general(t * unroll + u)
            return carry

        lax.fori_loop(1, nb // unroll, loop, 0)

    def spec(off):
        return pl.BlockSpec((sub, LANES), lambda h, r: (r, off + h))

    out = pl.BlockSpec((sub, LANES), lambda h, r: (r, h))
    return pl.pallas_call(
        body, name=name, grid=(DIL_HEADS_PER_GROUP, dilation),
        in_specs=[pl.BlockSpec(memory_space=pltpu.SMEM), spec(qoff), spec(koff), spec(voff)],
        out_specs=[out, out],
        out_shape=[jax.ShapeDtypeStruct((s, DIL_HEADS_PER_GROUP * LANES), F32)] * 2,
        compiler_params=_cparams(("parallel", "parallel"), VMEM_LIMIT_BIG),
    )(slopes, q, k, v)


def _band_bwd(q, k, v, do, lse, delta, slopes, *, name, dilation, qoff, koff, voff):
    s = q.shape[0]
    sub = s // dilation
    nb = sub // BLOCK
    unroll = min(BAND_UNROLL, nb)
    scale = LANES ** -0.5

    def body(sl_ref, q_ref, k_ref, v_ref, do_ref, lse_ref, dl_ref, dq_ref, dk_ref, dv_ref):
        slope = sl_ref[pl.program_id(0)]
        valid2, bias2 = _band_masks(dilation, slope)
        valid1, bias1 = valid2[:, BLOCK:], bias2[:, BLOCK:]

        def block(start_q, kk, vv, valid, bias):
            qb = q_ref[pl.ds(start_q, BLOCK), :]
            dob = do_ref[pl.ds(start_q, BLOCK), :]
            sc = lax.dot_general(qb, kk, (((1,), (1,)), ((), ())), preferred_element_type=F32) * scale
            sc = jnp.where(valid, sc + bias, NEG_INF)
            p = jnp.exp(sc - lse_ref[pl.ds(start_q, BLOCK), :][:, :1])
            dp = lax.dot_general(dob, vv, (((1,), (1,)), ((), ())), preferred_element_type=F32)
            ds = (p * (dp - dl_ref[pl.ds(start_q, BLOCK), :][:, :1])).astype(BF16)
            dq = jnp.dot(ds, kk, preferred_element_type=F32) * scale
            dq_ref[pl.ds(start_q, BLOCK), :] = dq.astype(BF16)
            dkk = lax.dot_general(ds, qb, (((0,), (0,)), ((), ())), preferred_element_type=F32) * scale
            dvv = lax.dot_general(p.astype(BF16), dob, (((0,), (0,)), ((), ())), preferred_element_type=F32)
            return dkk, dvv

        carry0 = block(0, k_ref[0:BLOCK, :], v_ref[0:BLOCK, :], valid1, bias1)

        def general(jj, carry):
            dk_part, dv_part = carry
            start_q, start_k = _aligned(jj * BLOCK), _aligned((jj - 1) * BLOCK)
            dkk, dvv = block(start_q, k_ref[pl.ds(start_k, 2 * BLOCK), :], v_ref[pl.ds(start_k, 2 * BLOCK), :],
                             valid2, bias2)
            dk_ref[pl.ds(start_k, BLOCK), :] = (dk_part + dkk[:BLOCK]).astype(BF16)
            dv_ref[pl.ds(start_k, BLOCK), :] = (dv_part + dvv[:BLOCK]).astype(BF16)
            return dkk[BLOCK:], dvv[BLOCK:]

        for jj in range(1, unroll):
            carry0 = general(jj, carry0)

        def loop(t, carry):
            for u in range(unroll):
                carry = general(t * unroll + u, carry)
            return carry

        dk_last, dv_last = lax.fori_loop(1, nb // unroll, loop, carry0)
        dk_ref[(nb - 1) * BLOCK:nb * BLOCK, :] = dk_last.astype(BF16)
        dv_ref[(nb - 1) * BLOCK:nb * BLOCK, :] = dv_last.astype(BF16)

    def spec(off):
        return pl.BlockSpec((sub, LANES), lambda h, r: (r, off + h))

    out = spec(0)
    return pl.pallas_call(
        body, name=name, grid=(DIL_HEADS_PER_GROUP, dilation),
        in_specs=[pl.BlockSpec(memory_space=pltpu.SMEM), spec(qoff), spec(koff), spec(voff), out, out, out],
        out_specs=[out, out, out],
        out_shape=[jax.ShapeDtypeStruct((s, DIL_HEADS_PER_GROUP * LANES), BF16)] * 3,
        compiler_params=_cparams(("parallel", "parallel"), VMEM_LIMIT_BIG),
    )(slopes, q, k, v, do, lse, delta)


def _mix_fwd(outs, lses, *, name):
    s, n = outs[0].shape
    ts = _rows(s, 512)

    def body(o0, o1, o2, l0, l1, l2, y_ref):
        la, lb, lc = l0[...], l1[...], l2[...]
        m = jnp.maximum(jnp.maximum(la, lb), lc)
        ea, eb, ec = jnp.exp(la - m), jnp.exp(lb - m), jnp.exp(lc - m)
        den = ea + eb + ec
        y = (ea / den) * o0[...] + (eb / den) * o1[...] + (ec / den) * o2[...]
        y_ref[...] = y.astype(BF16)

    row = pl.BlockSpec((ts, n), lambda i: (i, 0))
    return pl.pallas_call(
        body, name=name, grid=(s // ts,), in_specs=[row] * 6, out_specs=row,
        out_shape=jax.ShapeDtypeStruct((s, n), BF16), compiler_params=_cparams(("parallel",)),
    )(*outs, *lses)


def _mix_bwd(dy, outs, lses, *, name):
    s, n = dy.shape
    nh = n // LANES
    ts = _rows(s, 256)

    def body(dy_ref, o0, o1, o2, l0, l1, l2, d0, d1, d2, e0, e1, e2):
        la, lb, lc = l0[...], l1[...], l2[...]
        m = jnp.maximum(jnp.maximum(la, lb), lc)
        ea, eb, ec = jnp.exp(la - m), jnp.exp(lb - m), jnp.exp(lc - m)
        den = ea + eb + ec
        wa, wb, wc = ea / den, eb / den, ec / den
        dyv = dy_ref[...]
        y = wa * o0[...] + wb * o1[...] + wc * o2[...]
        prod = dyv * y
        d0[...] = (wa * dyv).astype(BF16)
        d1[...] = (wb * dyv).astype(BF16)
        d2[...] = (wc * dyv).astype(BF16)
        for h in range(nh):
            sl = slice(h * LANES, (h + 1) * LANES)
            t = jnp.sum(prod[:, sl], axis=1, keepdims=True)
            e0[:, sl] = wa[:, sl] * t
            e1[:, sl] = wb[:, sl] * t
            e2[:, sl] = wc[:, sl] * t

    row = pl.BlockSpec((ts, n), lambda i: (i, 0))
    return pl.pallas_call(
        body, name=name, grid=(s // ts,), in_specs=[row] * 7, out_specs=[row] * 6,
        out_shape=[jax.ShapeDtypeStruct((s, n), BF16)] * 3 + [jax.ShapeDtypeStruct((s, n), F32)] * 3,
        compiler_params=_cparams(("parallel",)),
    )(dy, *outs, *lses)


def _gate_fwd(gp, b_gate, branches, *, name):
    s = gp.shape[0]
    ts = _rows(s, 256)

    def body(gp_ref, b_ref, b0, b1, b2, o_ref):
        tot = None
        for i, br in enumerate((b0, b1, b2)):
            sl = slice(i * D_MODEL, (i + 1) * D_MODEL)
            t = jax.nn.sigmoid(gp_ref[:, sl] + b_ref[:, sl]) * br[...]
            tot = t if tot is None else tot + t
        o_ref[...] = tot.astype(BF16)

    row = pl.BlockSpec((ts, D_MODEL), lambda i: (i, 0))
    return pl.pallas_call(
        body, name=name, grid=(s // ts,),
        in_specs=[pl.BlockSpec((ts, 3 * D_MODEL), lambda i: (i, 0)), pl.BlockSpec((1, 3 * D_MODEL), lambda i: (0, 0)),
                  row, row, row],
        out_specs=row, out_shape=jax.ShapeDtypeStruct((s, D_MODEL), BF16),
        compiler_params=_cparams(("parallel",)),
    )(gp, b_gate, *branches)


def _gate_bwd(dm, gp, b_gate, branches, *, name):
    s = gp.shape[0]
    ts = _rows(s, 256)

    def body(dm_ref, gp_ref, b_ref, b0, b1, b2, d0, d1, d2, dgp_ref, db_ref):
        i = pl.program_id(0)

        @pl.when(i == 0)
        def _():
            db_ref[...] = jnp.zeros_like(db_ref)

        dmv = dm_ref[...]
        for k, (br, dbr) in enumerate(((b0, d0), (b1, d1), (b2, d2))):
            sl = slice(k * D_MODEL, (k + 1) * D_MODEL)
            sg = jax.nn.sigmoid(gp_ref[:, sl] + b_ref[:, sl])
            dbr[...] = (dmv * sg).astype(BF16)
            dg = dmv * br[...] * sg * (1.0 - sg)
            dgp_ref[:, sl] = dg.astype(BF16)
            db_ref[:, sl] += jnp.sum(dg, axis=0, keepdims=True)

    row = pl.BlockSpec((ts, D_MODEL), lambda i: (i, 0))
    wide = pl.BlockSpec((ts, 3 * D_MODEL), lambda i: (i, 0))
    vec = pl.BlockSpec((1, 3 * D_MODEL), lambda i: (0, 0))
    return pl.pallas_call(
        body, name=name, grid=(s // ts,),
        in_specs=[row, wide, vec, row, row, row], out_specs=[row, row, row, wide, vec],
        out_shape=[jax.ShapeDtypeStruct((s, D_MODEL), BF16)] * 3
        + [jax.ShapeDtypeStruct((s, 3 * D_MODEL), BF16), jax.ShapeDtypeStruct((1, 3 * D_MODEL), F32)],
        compiler_params=_cparams(("arbitrary",)),
    )(dm, gp, b_gate, *branches)


CONV_TC = 1408


def _shift_down(x, halo, k):
    rolled = pltpu.roll(x, k, 0)
    r8 = lax.broadcasted_iota(jnp.int32, halo.shape, 0)
    top = jnp.where(r8 < k, pltpu.roll(halo, k, 0), rolled[:SUBLANES])
    return jnp.concatenate([top, rolled[SUBLANES:]], axis=0)


def _shift_up(x, halo, k):
    n = x.shape[0]
    rolled = pltpu.roll(x, n - k, 0)
    r8 = lax.broadcasted_iota(jnp.int32, halo.shape, 0)
    bot = jnp.where(r8 >= SUBLANES - k, pltpu.roll(halo, SUBLANES - k, 0), rolled[n - SUBLANES:])
    return jnp.concatenate([rolled[:n - SUBLANES], bot], axis=0)


def _conv_fwd(u, conv_w, conv_b, *, name):
    s = u.shape[0]
    ts = _rows(s, 256)
    nct = D_FF // CONV_TC
    per8 = ts // SUBLANES

    def body(ug, uv, hg, hv, wg, wv, bg, bv, zg_ref, zv_ref, a_ref):
        first = pl.program_id(1) == 0

        def conv(u_ref, h_ref, w_ref, b_ref):
            x = u_ref[...]
            halo = jnp.where(first, 0.0, h_ref[...])
            z = b_ref[...] + w_ref[0:1, :] * _shift_down(x, halo, 2)
            z = z + w_ref[1:2, :] * _shift_down(x, halo, 1)
            return z + w_ref[2:3, :] * x

        zg = conv(ug, hg, wg, bg)
        zv = conv(uv, hv, wv, bv)
        zg_ref[...] = zg
        zv_ref[...] = zv
        a_ref[...] = (zg * jax.nn.sigmoid(zg) * zv).astype(BF16)

    def col(off):
        return pl.BlockSpec((ts, CONV_TC), lambda c, i: (i, c + off))

    def halo(off):
        return pl.BlockSpec((SUBLANES, CONV_TC), lambda c, i: (jnp.maximum(i * per8 - 1, 0), c + off))

    def wspec(rows, off):
        return pl.BlockSpec((rows, CONV_TC), lambda c, i: (0, c + off))

    zg, zv, a = pl.pallas_call(
        body, name=name, grid=(nct, s // ts),
        in_specs=[col(0), col(nct), halo(0), halo(nct), wspec(3, 0), wspec(3, nct), wspec(1, 0), wspec(1, nct)],
        out_specs=[col(0), col(0), col(0)],
        out_shape=[jax.ShapeDtypeStruct((s, D_FF), F32)] * 2 + [jax.ShapeDtypeStruct((s, D_FF), BF16)],
        compiler_params=_cparams(("parallel", "parallel")),
    )(u, u, u, u, conv_w, conv_w, conv_b, conv_b)
    return zg, zv, a


def _conv_bwd(da, zg, zv, u, conv_w, *, name):
    s = da.shape[0]
    ts = _rows(s, 256)
    nct = D_FF // CONV_TC
    per8 = ts // SUBLANES
    nrow = s // ts
    last8 = s // SUBLANES - 1

    def dz_of(dav, g, val):
        sg = jax.nn.sigmoid(g)
        return dav * val * sg * (1.0 + g * (1.0 - sg)), dav * g * sg

    def body(da_ref, zg_ref, zv_ref, da_nx, zg_nx, zv_nx, ug_ref, uv_ref, ug_pv, uv_pv, wg_ref, wv_ref,
             dug_ref, duv_ref, accg_ref, accv_ref):
        i = pl.program_id(1)
        dzg, dzv = dz_of(da_ref[...], zg_ref[...], zv_ref[...])
        da_next = jnp.where(i == nrow - 1, 0.0, da_nx[...])
        nxg, nxv = dz_of(da_next, zg_nx[...], zv_nx[...])

        @pl.when(i == 0)
        def _():
            accg_ref[...] = jnp.zeros_like(accg_ref)
            accv_ref[...] = jnp.zeros_like(accv_ref)

        for dz, nxt, u_ref, pv_ref, w_ref, du_ref, acc_ref in (
                (dzg, nxg, ug_ref, ug_pv, wg_ref, dug_ref, accg_ref),
                (dzv, nxv, uv_ref, uv_pv, wv_ref, duv_ref, accv_ref)):
            du = w_ref[2:3, :] * dz + w_ref[1:2, :] * _shift_up(dz, nxt, 1) + w_ref[0:1, :] * _shift_up(dz, nxt, 2)
            du_ref[...] = du.astype(BF16)
            x = u_ref[...]
            prev = jnp.where(i == 0, 0.0, pv_ref[...])
            acc_ref[0:1, :] += jnp.sum(dz * _shift_down(x, prev, 2), axis=0, keepdims=True)
            acc_ref[1:2, :] += jnp.sum(dz * _shift_down(x, prev, 1), axis=0, keepdims=True)
            acc_ref[2:3, :] += jnp.sum(dz * x, axis=0, keepdims=True)
            acc_ref[3:4, :] += jnp.sum(dz, axis=0, keepdims=True)

    def blk(off):
        return pl.BlockSpec((ts, CONV_TC), lambda c, i: (i, c + off))

    def nxt8(off):
        return pl.BlockSpec((SUBLANES, CONV_TC), lambda c, i: (jnp.minimum((i + 1) * per8, last8), c + off))

    def prv8(off):
        return pl.BlockSpec((SUBLANES, CONV_TC), lambda c, i: (jnp.maximum(i * per8 - 1, 0), c + off))

    def wspec(off):
        return pl.BlockSpec((3, CONV_TC), lambda c, i: (0, c + off))

    acc = pl.BlockSpec((SUBLANES, CONV_TC), lambda c, i: (0, c))
    return pl.pallas_call(
        body, name=name, grid=(nct, nrow),
        in_specs=[blk(0), blk(0), blk(0), nxt8(0), nxt8(0), nxt8(0), blk(0), blk(nct), prv8(0), prv8(nct),
                  wspec(0), wspec(nct)],
        out_specs=[blk(0), blk(0), acc, acc],
        out_shape=[jax.ShapeDtypeStruct((s, D_FF), BF16)] * 2 + [jax.ShapeDtypeStruct((SUBLANES, D_FF), F32)] * 2,
        compiler_params=_cparams(("parallel", "arbitrary")),
    )(da, zg, zv, da, zg, zv, u, u, u, u, conv_w, conv_w)


def _peer(k):
    x, y, c = lax.axis_index("x"), lax.axis_index("y"), lax.axis_index("c")
    px = 1 - x if k & 4 else x
    py = 1 - y if k & 2 else y
    pc = 1 - c if k & 1 else c
    return (px, py, pc), 4 * px + 2 * py + pc


def _exchange(bufs, *, name, gather):
    n = len(bufs)
    npeer = N_DEV - 1

    def body(*refs):
        srcs, outs = refs[:n], refs[n:2 * n]
        send_sems, recv_sems, local_sems = refs[2 * n:]
        _, me = _peer(0)
        mine = [src if gather else src.at[me] for src in srcs]
        local = [pltpu.make_async_copy(mine[p], outs[p].at[me], local_sems.at[p]) for p in range(n)]
        for cp in local:
            cp.start()
        sends = []
        for k in range(1, N_DEV):
            dev, idx = _peer(k)
            for p in range(n):
                cp = pltpu.make_async_remote_copy(
                    src_ref=srcs[p] if gather else srcs[p].at[idx], dst_ref=outs[p].at[me],
                    send_sem=send_sems.at[p * npeer + k - 1], recv_sem=recv_sems.at[p * npeer + k - 1],
                    device_id=dev, device_id_type=pl.DeviceIdType.MESH)
                cp.start()
                sends.append(cp)
        for k in range(1, N_DEV):
            dev, idx = _peer(k)
            for p in range(n):
                pltpu.make_async_remote_copy(
                    src_ref=mine[p], dst_ref=outs[p].at[idx],
                    send_sem=send_sems.at[p * npeer + k - 1], recv_sem=recv_sems.at[p * npeer + k - 1],
                    device_id=dev, device_id_type=pl.DeviceIdType.MESH).wait_recv()
        for cp in sends:
            cp.wait_send()
        for cp in local:
            cp.wait()

    any_spec = pl.BlockSpec(memory_space=pl.ANY)
    return pl.pallas_call(
        body, name=name,
        in_specs=[any_spec] * n, out_specs=[any_spec] * n,
        out_shape=[jax.ShapeDtypeStruct((N_DEV,) + b.shape[-2:], b.dtype) for b in bufs],
        scratch_shapes=[pltpu.SemaphoreType.DMA((n * npeer,)), pltpu.SemaphoreType.DMA((n * npeer,)),
                        pltpu.SemaphoreType.DMA((n,))],
    )(*bufs)


_HBM = pl.BlockSpec(memory_space=pltpu.HBM)
_SEM = pl.BlockSpec(memory_space=pltpu.SEMAPHORE)
_EFFECT = pltpu.SideEffectType.DATAFLOW_SIDE_EFFECTING


def _split_copy(srcs, lands, send_sems, recv_sems, gather, k, p):
    _, me = _peer(0)
    dev, idx = _peer(k)
    sem = p * (N_DEV - 1) + k - 1
    return pltpu.make_async_remote_copy(
        src_ref=srcs[p] if gather else srcs[p].at[idx], dst_ref=lands[p].at[me],
        send_sem=send_sems.at[sem], recv_sem=recv_sems.at[sem],
        device_id=dev, device_id_type=pl.DeviceIdType.MESH)


def _exchange_start(bufs, after, *, name, gather):
    n = len(bufs)
    nsem = n * (N_DEV - 1)

    def body(*refs):
        srcs, lands = refs[:n], refs[n:2 * n]
        send_sems, recv_sems = refs[2 * n + 1], refs[2 * n + 2]
        token = refs[-1]
        for k in range(1, N_DEV):
            for p in range(n):
                _split_copy(srcs, lands, send_sems, recv_sems, gather, k, p).start()
        token[...] = jnp.zeros_like(token)

    hbm = lambda a: pltpu.with_memory_space_constraint(a, pltpu.HBM)
    lands = [lax.empty((N_DEV,) + b.shape[-2:], b.dtype) for b in bufs]
    mem = [pltpu.HBM(b.shape, b.dtype) for b in bufs] + [pltpu.HBM(l.shape, l.dtype) for l in lands]
    outs = pl.pallas_call(
        body, name=name,
        in_specs=[_HBM] * (2 * n) + [pl.BlockSpec(memory_space=pl.ANY)],
        out_specs=[_SEM, _SEM] + [_HBM] * (2 * n) + [pl.BlockSpec(memory_space=pltpu.VMEM)],
        out_shape=[pltpu.SemaphoreType.DMA((nsem,)), pltpu.SemaphoreType.DMA((nsem,))] + mem
        + [jax.ShapeDtypeStruct((SUBLANES, LANES), F32)],
        input_output_aliases={p: 2 + p for p in range(2 * n)},
        compiler_params=pltpu.CompilerParams(has_side_effects=_EFFECT),
    )(*[hbm(b) for b in bufs], *[hbm(l) for l in lands], after)
    return (outs[0], outs[1], outs[2:2 + n], outs[2 + n:2 + 2 * n]), outs[-1]


def _exchange_wait(handle, after, *, name, gather):
    send_sems, recv_sems, srcs, lands = handle
    n = len(srcs)

    def body(*refs):
        src_refs, land_refs = refs[:n], refs[n:2 * n]
        send_ref, recv_ref = refs[2 * n], refs[2 * n + 1]
        for k in range(1, N_DEV):
            for p in range(n):
                cp = _split_copy(src_refs, land_refs, send_ref, recv_ref, gather, k, p)
                cp.wait_send()
                cp.wait_recv()

    mem = [pltpu.HBM(b.shape, b.dtype) for b in srcs] + [pltpu.HBM(l.shape, l.dtype) for l in lands]
    outs = pl.pallas_call(
        body, name=name,
        in_specs=[_HBM] * (2 * n) + [_SEM, _SEM, pl.BlockSpec(memory_space=pl.ANY)],
        out_specs=[_HBM] * (2 * n), out_shape=mem,
        input_output_aliases={p: p for p in range(2 * n)},
        compiler_params=pltpu.CompilerParams(has_side_effects=_EFFECT),
    )(*srcs, *lands, send_sems, recv_sems, after)
    return outs[n:]


def _with_own(landed, own, me):
    return lax.dynamic_update_slice(landed, own[None], (me, 0, 0))


def _adamw(parts, w, m, v, *, name):
    rows, width = w.shape
    tr = _rows(rows, max(16, (128 * 1024) // width), mult=16)

    def body(p_ref, w_ref, m_ref, v_ref, g_ref, d_ref, nm_ref, nv_ref):
        g = p_ref[0].astype(F32)
        for k in range(1, N_DEV):
            g = g + p_ref[k].astype(F32)
        mn = ADAM_B1 * m_ref[...] + (1.0 - ADAM_B1) * g
        vn = ADAM_B2 * v_ref[...] + (1.0 - ADAM_B2) * jnp.square(g)
        m_hat = mn / (1.0 - ADAM_B1 ** ADAM_STEP)
        v_hat = vn / (1.0 - ADAM_B2 ** ADAM_STEP)
        g_ref[...] = g
        d_ref[...] = -ADAM_LR * (m_hat / (jnp.sqrt(v_hat) + ADAM_EPS) + ADAM_WD * w_ref[...])
        nm_ref[...] = mn
        nv_ref[...] = vn

    row = pl.BlockSpec((tr, width), lambda i: (i, 0))
    return pl.pallas_call(
        body, name=name, grid=(rows // tr,),
        in_specs=[pl.BlockSpec((N_DEV, tr, width), lambda i: (0, i, 0)), row, row, row],
        out_specs=[row] * 4, out_shape=[jax.ShapeDtypeStruct((rows, width), F32)] * 4,
        compiler_params=_cparams(("parallel",)),
    )(parts, w, m, v)


def _pack_replicated(blocks):
    flat = jnp.concatenate([blocks[name].reshape(-1).astype(F32) for name, _, _ in REPLICATED])
    flat = jnp.pad(flat, (0, PACK_ROWS * PACK_W - PACK_USED))
    return flat.reshape(PACK_ROWS, PACK_W)


def _unpack_replicated(buf):
    flat = buf.reshape(-1)
    return {name: flat[off:off + n].reshape(1, n) for name, (off, n) in PACK_TABLE.items()}


def _join_shards(seg, shape, axis):
    return seg.reshape(shape) if axis == 0 else seg.transpose(1, 0, 2).reshape(shape)


def _split_shards(g, shape, axis):
    r, c = shape
    if axis == 0:
        return g.reshape(N_DEV, r // N_DEV, c)
    return g.reshape(r, N_DEV, c // N_DEV).transpose(1, 0, 2)


def _to_residues(a, d):
    s, c = a.shape
    return a.reshape(s // d, d, c).transpose(1, 0, 2).reshape(s, c)


def _from_residues(a, d):
    s, c = a.shape
    return a.reshape(d, s // d, c).transpose(1, 0, 2).reshape(s, c)


def _pad_heads(w, heads, width, lo, hi):
    r = w.shape[0]
    w = w.reshape(r, heads, width)[:, :, lo:hi]
    w = jnp.pad(w, ((0, 0), (0, 0), (0, LANES - (hi - lo))))
    return w.reshape(r, heads * LANES)


class _NoOverlap:
    start_token = None

    def late_weights(self, w, after):
        return w

    def early_grads(self, names, grads):
        return None


def _after(vec, token):
    return vec if token is None else vec + token[0:1, 0:1]


def _local_step(x, mem, positions, target, w, hooks=_NoOverlap()):
    s = x.shape[0]
    bf = lambda a: a.astype(BF16)

    w_in = w["w_in"]
    kr_cols = jnp.pad(w_in[:, OFF_KV:OFF_KR], ((0, 0), (MLA_NOPE, LANES - MLA_QK_DIM)))
    w_a = bf(jnp.concatenate([w_in[:, :OFF_KV], kr_cols], axis=1))
    w_dm = bf(w_in[:, OFF_KR:OFF_MEMQ])
    w_g = bf(w_in[:, OFF_MEMQ:])
    w_in_t = jnp.concatenate([w_a, w_dm, w_g], axis=1).T
    tabs = _rope_tables(positions)

    h = _rms_fwd(x, _after(w["g_pre_mix"], hooks.start_token), name="rms_pre_mix", out_dtype=BF16)
    p_a = _matmul(h, w_a, name="proj_a")
    p_dm = _matmul(h, w_dm, name="proj_dm", out_dtype=BF16)
    p_g = _matmul(h, w_g, name="proj_gate")
    c_q, c_kv, kr = p_a[:, :OFF_Q], p_a[:, OFF_Q:OFF_KV], p_a[:, OFF_KV:]

    w = hooks.late_weights(w, p_g)
    wq = bf(_pad_heads(w["w_uq"], MLA_HEADS, MLA_QK_DIM, 0, MLA_QK_DIM))
    wk = bf(_pad_heads(w["w_ukv"], MLA_HEADS, MLA_NOPE + MLA_V, 0, MLA_NOPE))
    wv = bf(_pad_heads(w["w_ukv"], MLA_HEADS, MLA_NOPE + MLA_V, MLA_NOPE, MLA_NOPE + MLA_V))
    w_mkv = bf(w["w_mem_kv"])
    wb_mla = bf(jnp.pad(w["w_br_mla"].reshape(MLA_HEADS, MLA_V, D_MODEL),
                        ((0, 0), (0, LANES - MLA_V), (0, 0))).reshape(MLA_HEADS * LANES, D_MODEL))
    wb_dil, wb_mem, w_o = bf(w["w_br_dil"]), bf(w["w_br_mem"]), bf(w["w_o"])
    w_up, w_down = bf(w["w_ffn_up"]), bf(w["w_ffn_down"])
    slopes = [jnp.asarray(sl, F32) for sl in DIL_SLOPES]
    mla_scale = MLA_QK_DIM ** -0.5
    mem_scale = LANES ** -0.5
    MQ = 3 * DIL_HEADS

    qn = _rms_fwd(c_q, w["mla_q_norm"], name="rms_q", out_dtype=BF16)
    kvn = _rms_fwd(c_kv, w["mla_kv_norm"], name="rms_kv", out_dtype=BF16)
    q_raw = _matmul(qn, wq, name="mla_q_up")
    k_raw = _matmul(kvn, wk, name="mla_k_up")
    v_f = _matmul(kvn, wv, name="mla_v_up", out_dtype=BF16)
    q_f = _rope_fwd(q_raw, tabs, name="rope_q", scale=mla_scale * LOG2E)
    k_f = _rope_fwd(k_raw, tabs, name="rope_k", scale=1.0, add=kr)
    o_mla, lse_mla = _causal_fwd(q_f, k_f, v_f, name="mla_fwd", heads=MLA_HEADS, ones_lane=MLA_V)

    dil_in, dil_o, dil_lse = [], [], []
    for g, (_, d) in enumerate(DIL_PAIRS):
        if d == 1:
            arrs, offs = (p_dm, p_dm, p_dm), (4 * g, DIL_HEADS + 4 * g, 2 * DIL_HEADS + 4 * g)
        else:
            arrs = tuple(_to_residues(p_dm[:, (t * DIL_HEADS + 4 * g) * LANES:(t * DIL_HEADS + 4 * g + 4) * LANES], d)
                         for t in range(3))
            offs = (0, 0, 0)
        o_g, lse_g = _band_fwd(*arrs, slopes[g], name=f"dil_fwd_{g}", dilation=d,
                               qoff=offs[0], koff=offs[1], voff=offs[2])
        dil_in.append((arrs, offs))
        dil_o.append(_from_residues(o_g, d))
        dil_lse.append(_from_residues(lse_g, d))
    y_dil = _mix_fwd(dil_o, dil_lse, name="dil_mix")

    memn = _rms_fwd(mem, w["g_mem"], name="rms_mem", out_dtype=BF16)
    kv_m = _matmul(memn, w_mkv, name="mem_kv", out_dtype=BF16)
    memat = dict(heads=MEM_HEADS, qoff=MQ, koff=0, voff=MEM_HEADS, causal=False, scale=mem_scale, tq=512, tk=256)
    o_mem, lse_mem = _flash_fwd(p_dm, kv_m, kv_m, name="mem_fwd", **memat)

    b_mla = _matmul(o_mla, wb_mla, name="br_mla")
    b_dil = _matmul(y_dil, wb_dil, name="br_dil")
    b_mem = _matmul(o_mem, wb_mem, name="br_mem")
    merged = _gate_fwd(p_g, w["b_gate"], (b_mla, b_dil, b_mem), name="gate_fwd")
    z1 = _matmul(merged, w_o, name="out_proj")
    x1 = _rms_fwd(z1, w["g_post_mix"], name="rms_post_mix", out_dtype=F32, add=x)
    h2 = _rms_fwd(x1, w["g_pre_ffn"], name="rms_pre_ffn", out_dtype=BF16)
    u = _matmul(h2, w_up, name="ffn_up")
    zg, zv, act = _conv_fwd(u, w["conv_w"], w["conv_b"], name="conv_fwd")
    f = _matmul(act, w_down, name="ffn_down")
    dy, sq = _loss_fwd(x1, f, target, w["g_post_ffn"], name="loss")
    loss = 0.5 * jnp.sum(sq) / D_MODEL

    grads = {}
    df, grads["g_post_ffn"] = _rms_bwd(f, w["g_post_ffn"], dy, name="rms_post_ffn_bwd", out_dtype=BF16)
    da = _matmul(df, w_down.T, name="ffn_down_dx")
    grads["w_ffn_down"] = _matmul(act, df, name="ffn_down_dw", trans_a=True)
    du_g, du_v, cacc_g, cacc_v = _conv_bwd(da, zg, zv, u, w["conv_w"], name="conv_bwd")
    grads["conv_w"] = jnp.concatenate([cacc_g[0:3], cacc_v[0:3]], axis=1)
    grads["conv_b"] = jnp.concatenate([cacc_g[3:4], cacc_v[3:4]], axis=1)
    w_up_t = w_up.T
    dh2 = _matmul(du_g, w_up_t[:D_FF], name="ffn_up_dx_gate")
    dh2 = _matmul(du_v, w_up_t[D_FF:], name="ffn_up_dx_val", add=dh2)
    grads["w_ffn_up"] = jnp.concatenate([_matmul(h2, du_g, name="ffn_up_dw_gate", trans_a=True),
                                         _matmul(h2, du_v, name="ffn_up_dw_val", trans_a=True)], axis=1)
    tok = hooks.early_grads(("w_ffn_down", "w_ffn_up", "conv_w"), grads)
    dx1, grads["g_pre_ffn"] = _rms_bwd(x1, _after(w["g_pre_ffn"], tok), dh2, name="rms_pre_ffn_bwd",
                                       out_dtype=F32, add=dy)
    dz1, grads["g_post_mix"] = _rms_bwd(z1, w["g_post_mix"], dx1, name="rms_post_mix_bwd", out_dtype=BF16)
    dmerged = _matmul(dz1, w_o.T, name="out_proj_dx")
    grads["w_o"] = _matmul(merged, dz1, name="out_proj_dw", trans_a=True)
    db_mla, db_dil, db_mem, dgp, grads["b_gate"] = _gate_bwd(
        dmerged, p_g, w["b_gate"], (b_mla, b_dil, b_mem), name="gate_bwd")

    do_mla = _matmul(db_mla, wb_mla.T, name="br_mla_dx", out_dtype=BF16)
    g_wb_mla = _matmul(o_mla, db_mla, name="br_mla_dw", trans_a=True)
    grads["w_br_mla"] = g_wb_mla.reshape(MLA_HEADS, LANES, D_MODEL)[:, :MLA_V].reshape(MLA_HEADS * MLA_V, D_MODEL)
    delta_mla = _row_dot(do_mla, o_mla, name="mla_delta")
    dq_f, dk_f, dv_f = _causal_bwd(q_f, k_f, v_f, do_mla, lse_mla, delta_mla, name="mla_bwd", heads=MLA_HEADS)
    dq_raw = _rope_bwd(dq_f, tabs, name="rope_q_bwd", scale=mla_scale, with_add=False)
    dk_raw, dkr = _rope_bwd(dk_f, tabs, name="rope_k_bwd", scale=1.0, with_add=True)
    dqn = _matmul(dq_raw, wq.T, name="mla_q_up_dx")
    g_wq = _matmul(qn, dq_raw, name="mla_q_up_dw", trans_a=True)
    grads["w_uq"] = g_wq.reshape(MLA_Q_RANK, MLA_HEADS, LANES)[:, :, :MLA_QK_DIM].reshape(MLA_Q_RANK, -1)
    dkvn = _matmul(dk_raw, wk.T, name="mla_k_up_dx")
    dkvn = _matmul(dv_f, wv.T, name="mla_v_up_dx", add=dkvn)
    g_wk = _matmul(kvn, dk_raw, name="mla_k_up_dw", trans_a=True).reshape(MLA_KV_RANK, MLA_HEADS, LANES)
    g_wv = _matmul(kvn, dv_f, name="mla_v_up_dw", trans_a=True).reshape(MLA_KV_RANK, MLA_HEADS, LANES)
    grads["w_ukv"] = jnp.concatenate([g_wk[:, :, :MLA_NOPE], g_wv[:, :, :MLA_V]], axis=2).reshape(MLA_KV_RANK, -1)
    dc_q, grads["mla_q_norm"] = _rms_bwd(c_q, w["mla_q_norm"], dqn, name="rms_q_bwd", out_dtype=BF16)
    dc_kv, grads["mla_kv_norm"] = _rms_bwd(c_kv, w["mla_kv_norm"], dkvn, name="rms_kv_bwd", out_dtype=BF16)

    dy_dil = _matmul(db_dil, wb_dil.T, name="br_dil_dx")
    grads["w_br_dil"] = _matmul(y_dil, db_dil, name="br_dil_dw", trans_a=True)
    mix = _mix_bwd(dy_dil, dil_o, dil_lse, name="dil_mix_bwd")
    d_dil = [[None] * 3 for _ in range(3)]
    for g, (_, d) in enumerate(DIL_PAIRS):
        arrs, offs = dil_in[g]
        do_g, dl_g, lse_g = mix[g], mix[3 + g], dil_lse[g]
        if d != 1:
            do_g, dl_g, lse_g = _to_residues(do_g, d), _to_residues(dl_g, d), _to_residues(lse_g, d)
        dq_g, dk_g, dv_g = _band_bwd(*arrs, do_g, lse_g, dl_g, slopes[g], name=f"dil_bwd_{g}", dilation=d,
                                     qoff=offs[0], koff=offs[1], voff=offs[2])
        for t, a in enumerate((dq_g, dk_g, dv_g)):
            d_dil[t][g] = a if d == 1 else _from_residues(a, d)

    do_mem = _matmul(db_mem, wb_mem.T, name="br_mem_dx", out_dtype=BF16)
    grads["w_br_mem"] = _matmul(o_mem, db_mem, name="br_mem_dw", trans_a=True)
    delta_mem = _row_dot(do_mem, o_mem, name="mem_delta")
    dq_mem = _flash_bwd_dq(p_dm, kv_m, kv_m, do_mem, lse_mem, delta_mem, name="mem_bwd_dq", out_dtype=BF16, **memat)
    dk_mem, dv_mem = _flash_bwd_dkv(p_dm, kv_m, kv_m, do_mem, lse_mem, delta_mem, name="mem_bwd_dkv",
                                    dk_dtype=BF16, dv_dtype=BF16, **memat)
    dkv_m = jnp.concatenate([dk_mem, dv_mem], axis=1)
    dmemn = _matmul(dkv_m, w_mkv.T, name="mem_kv_dx")
    grads["w_mem_kv"] = _matmul(memn, dkv_m, name="mem_kv_dw", trans_a=True)
    _, grads["g_mem"] = _rms_bwd(mem, w["g_mem"], dmemn, name="rms_mem_bwd", out_dtype=BF16)

    tok = hooks.early_grads(("w_o", "w_br_mla", "w_br_dil", "w_br_mem", "w_uq", "w_ukv", "w_mem_kv"), grads)
    if tok is not None:
        dkr = dkr + tok[0:1, 0:1]
    dp_all = jnp.concatenate([dc_q, dc_kv, bf(dkr)] + d_dil[0] + d_dil[1] + d_dil[2] + [dq_mem, dgp], axis=1)
    dh = _matmul(dp_all, w_in_t, name="proj_dx")
    g_in = _matmul(h, dp_all, name="proj_dw", trans_a=True)
    grads["w_in"] = jnp.concatenate(
        [g_in[:, :OFF_KV], g_in[:, OFF_KV + MLA_NOPE:OFF_KV + MLA_QK_DIM], g_in[:, N_A:]], axis=1)
    dx, grads["g_pre_mix"] = _rms_bwd(x, w["g_pre_mix"], dh, name="rms_pre_mix_bwd", out_dtype=F32, add=dx1)
    return loss, dx, grads


def kernel(x, mem, positions, g_pre_mix, w_in, b_gate, mla_q_norm, w_uq, mla_kv_norm, w_ukv, g_mem, w_mem_kv, w_br_mla, w_br_dil, w_br_mem, w_o, g_post_mix, g_pre_ffn, w_ffn_up, conv_w, conv_b, w_ffn_down, g_post_ffn, loss_target, m_g_pre_mix, m_w_in, m_b_gate, m_mla_q_norm, m_w_uq, m_mla_kv_norm, m_w_ukv, m_g_mem, m_w_mem_kv, m_w_br_mla, m_w_br_dil, m_w_br_mem, m_w_o, m_g_post_mix, m_g_pre_ffn, m_w_ffn_up, m_conv_w, m_conv_b, m_w_ffn_down, m_g_post_ffn, v_g_pre_mix, v_w_in, v_b_gate, v_mla_q_norm, v_w_uq, v_mla_kv_norm, v_w_ukv, v_g_mem, v_w_mem_kv, v_w_br_mla, v_w_br_dil, v_w_br_mem, v_w_o, v_g_post_mix, v_g_pre_ffn, v_w_ffn_up, v_conv_w, v_conv_b, v_w_ffn_down, v_g_post_ffn):
    local = dict(g_pre_mix=g_pre_mix, w_in=w_in, b_gate=b_gate, mla_q_norm=mla_q_norm, w_uq=w_uq,
                 mla_kv_norm=mla_kv_norm, w_ukv=w_ukv, g_mem=g_mem, w_mem_kv=w_mem_kv, w_br_mla=w_br_mla,
                 w_br_dil=w_br_dil, w_br_mem=w_br_mem, w_o=w_o, g_post_mix=g_post_mix, g_pre_ffn=g_pre_ffn,
                 w_ffn_up=w_ffn_up, conv_w=conv_w, conv_b=conv_b, w_ffn_down=w_ffn_down, g_post_ffn=g_post_ffn)
    mom_m = dict(g_pre_mix=m_g_pre_mix, w_in=m_w_in, b_gate=m_b_gate, mla_q_norm=m_mla_q_norm, w_uq=m_w_uq,
                 mla_kv_norm=m_mla_kv_norm, w_ukv=m_w_ukv, g_mem=m_g_mem, w_mem_kv=m_w_mem_kv, w_br_mla=m_w_br_mla,
                 w_br_dil=m_w_br_dil, w_br_mem=m_w_br_mem, w_o=m_w_o, g_post_mix=m_g_post_mix,
                 g_pre_ffn=m_g_pre_ffn, w_ffn_up=m_w_ffn_up, conv_w=m_conv_w, conv_b=m_conv_b,
                 w_ffn_down=m_w_ffn_down, g_post_ffn=m_g_post_ffn)
    mom_v = dict(g_pre_mix=v_g_pre_mix, w_in=v_w_in, b_gate=v_b_gate, mla_q_norm=v_mla_q_norm, w_uq=v_w_uq,
                 mla_kv_norm=v_mla_kv_norm, w_ukv=v_w_ukv, g_mem=v_g_mem, w_mem_kv=v_w_mem_kv, w_br_mla=v_w_br_mla,
                 w_br_dil=v_w_br_dil, w_br_mem=v_w_br_mem, w_o=v_w_o, g_post_mix=v_g_post_mix,
                 g_pre_ffn=v_g_pre_ffn, w_ffn_up=v_w_ffn_up, conv_w=v_conv_w, conv_b=v_conv_b,
                 w_ffn_down=v_w_ffn_down, g_post_ffn=v_g_post_ffn)

    me = 4 * lax.axis_index("x") + 2 * lax.axis_index("y") + lax.axis_index("c")
    spec = {name: (shape, axis) for name, shape, axis in SHARDED}
    wire = lambda name: F32 if name == "conv_w" else BF16
    shard = {name: local[name][0].astype(wire(name)) for name in spec}
    slab = lambda name, grads: _split_shards(grads[name].astype(wire(name)), *spec[name])

    w_in_all = _exchange([shard["w_in"]], name="gather_w_in", gather=True)[0]
    late = tuple(name for name in spec if name != "w_in")
    late_handle, late_token = _exchange_start([shard[n] for n in late], w_in_all, name="gather_rest_start",
                                              gather=True)
    full = {"w_in": _join_shards(w_in_all, *spec["w_in"])}
    for name, _, _ in REPLICATED:
        full[name] = local[name].reshape(1, -1)

    pending = []

    class Overlap:
        start_token = late_token

        def late_weights(self, w, after):
            landed = _exchange_wait(late_handle, after, name="gather_rest_wait", gather=True)
            w = dict(w)
            for name, buf in zip(late, landed):
                w[name] = _join_shards(_with_own(buf, shard[name], me), *spec[name])
            return w

        def early_grads(self, names, grads):
            slabs = [slab(name, grads) for name in names]
            handle, token = _exchange_start(slabs, slabs[0], name="grads_start_" + names[0], gather=False)
            pending.append((names, slabs, handle))
            return token

    loss, dx, grads = _local_step(x[0], mem[0], positions[0], loss_target[0], full, Overlap())

    parts = {}
    for names, slabs, handle in pending:
        landed = _exchange_wait(handle, dx, name="grads_wait_" + names[0], gather=False)
        for name, own, buf in zip(names, slabs, landed):
            parts[name] = _with_own(buf, lax.dynamic_index_in_dim(own, me, 0, keepdims=False), me)
    parts["w_in"] = _exchange([slab("w_in", grads)], name="exchange_grad_w_in", gather=False)[0]
    rep_parts = _exchange([_pack_replicated(grads)], name="gather_replicated_grads", gather=True)[0]

    results = {}
    for name in spec:
        res = _adamw(parts[name], local[name][0], mom_m[name][0], mom_v[name][0], name="adamw_" + name)
        results[name] = [r[None] for r in res]
    rep = _adamw(rep_parts, _pack_replicated(local), _pack_replicated(mom_m), _pack_replicated(mom_v),
                 name="adamw_replicated")
    for i, buf in enumerate(rep):
        for name, val in _unpack_replicated(buf).items():
            results.setdefault(name, [None] * 4)[i] = val

    loss = lax.psum(loss, ("x", "y", "c"))
    outs = [loss, dx[None]]
    for i in range(4):
        outs.extend(results[name][i] for name in PARAM_NAMES)
    return tuple(outs)
```

```python
import functools

import numpy as np
import jax
import jax.numpy as jnp
from jax import lax
from jax.experimental import pallas as pl
from jax.experimental.pallas import tpu as pltpu

F32 = jnp.float32
BF16 = jnp.bfloat16

N_DEV = 8
D_MODEL = 1024
RMS_EPS = 1e-6
NEG_INF = -1e30
LANES = 128
SUBLANES = 8
BLOCK = 128

MLA_HEADS = 8
MLA_NOPE = 64
MLA_ROPE = 32
MLA_V = 64
MLA_QK_DIM = MLA_NOPE + MLA_ROPE
MLA_Q_RANK = 384
MLA_KV_RANK = 256
ROPE_THETA = 10000.0
DIL_PAIRS = ((128, 1), (512, 4), (2048, 16))
DIL_HEADS_PER_GROUP = 4
DIL_HEADS = 12
MEM_HEADS = 4
D_FF = 2816
OFF_Q = MLA_Q_RANK
OFF_KV = OFF_Q + MLA_KV_RANK
OFF_KR = OFF_KV + MLA_ROPE
OFF_DIL = OFF_KR + 3 * DIL_HEADS * LANES
OFF_MEMQ = OFF_DIL + MEM_HEADS * LANES
D_IN = OFF_MEMQ + 3 * D_MODEL
N_A = OFF_KV + LANES
N_DM = 3 * DIL_HEADS * LANES + MEM_HEADS * LANES

ADAM_LR = 0.001
ADAM_B1 = 0.9
ADAM_B2 = 0.999
ADAM_EPS = 1e-08
ADAM_WD = 0.01
ADAM_STEP = 10

VMEM_LIMIT = 48 * 1024 * 1024
VMEM_LIMIT_BIG = 58 * 1024 * 1024
PACK_W = 1024

_ALIBI_BASE = np.exp2(-8.0 * np.arange(1, DIL_HEADS + 1) / DIL_HEADS)
DIL_SLOPES = [[float(_ALIBI_BASE[hh * 3 + g]) for hh in range(DIL_HEADS_PER_GROUP)] for g in range(3)]

PARAMS = (
    ("g_pre_mix", (1024,), None), ("w_in", (1024, D_IN), 1), ("b_gate", (3072,), None),
    ("mla_q_norm", (384,), None), ("w_uq", (384, 768), 1), ("mla_kv_norm", (256,), None),
    ("w_ukv", (256, 1024), 1), ("g_mem", (1024,), None), ("w_mem_kv", (1024, 1024), 0),
    ("w_br_mla", (512, 1024), 1), ("w_br_dil", (512, 1024), 1), ("w_br_mem", (512, 1024), 1),
    ("w_o", (1024, 1024), 0), ("g_post_mix", (1024,), None), ("g_pre_ffn", (1024,), None),
    ("w_ffn_up", (1024, 2 * D_FF), 1), ("conv_w", (3, 2 * D_FF), 1), ("conv_b", (2 * D_FF,), None),
    ("w_ffn_down", (D_FF, 1024), 0), ("g_post_ffn", (1024,), None),
)
PARAM_NAMES = tuple(p[0] for p in PARAMS)


def _shard_shape(shape, axis):
    if axis is None:
        return shape
    return tuple(s // N_DEV if a == axis else s for a, s in enumerate(shape))


SHARDED = tuple(p for p in PARAMS if p[2] is not None)
REPLICATED = tuple(p for p in PARAMS if p[2] is None)


def _layout():
    off, table = 0, {}
    for name, shape, _ in REPLICATED:
        table[name] = (off, shape[0])
        off += shape[0]
    rows = -(-off // PACK_W)
    rows = -(-rows // SUBLANES) * SUBLANES
    return table, off, rows


PACK_TABLE, PACK_USED, PACK_ROWS = _layout()


def _pick(n, cap):
    best = None
    for t in range(LANES, min(n, cap) + 1, LANES):
        if n % t == 0:
            best = t
    return best if best is not None else n


def _rows(n, cap, mult=SUBLANES):
    best = None
    for t in range(mult, min(n, cap) + 1, mult):
        if n % t == 0:
            best = t
    return best if best is not None else n


def _cparams(sem, vmem=VMEM_LIMIT):
    return pltpu.CompilerParams(dimension_semantics=sem, vmem_limit_bytes=vmem)


def _matmul(a, b, *, name, out_dtype=F32, trans_a=False, add=None, after=None, tm=1024, tn=1408, tk=640):
    if trans_a:
        kc, m = a.shape
    else:
        m, kc = a.shape
    n = b.shape[1]
    assert b.shape[0] == kc
    tm, tn, tk = _pick(m, tm), _pick(n, tn), _pick(kc, tk)
    nk = kc // tk

    def body(*refs):
        a_ref, b_ref = refs[:2]
        c_ref = refs[2] if add is not None else None
        o_ref, acc = refs[-2:]
        k = pl.program_id(2)

        @pl.when(k == 0)
        def _():
            if add is None:
                acc[...] = jnp.zeros_like(acc)
            else:
                acc[...] = c_ref[...].astype(F32)

        av = a_ref[...].astype(BF16)
        bv = b_ref[...].astype(BF16)
        if trans_a:
            acc[...] += lax.dot_general(av, bv, (((0,), (0,)), ((), ())), preferred_element_type=F32)
        else:
            acc[...] += jnp.dot(av, bv, preferred_element_type=F32)

        @pl.when(k == nk - 1)
        def _():
            o_ref[...] = acc[...].astype(out_dtype)

    if trans_a:
        a_spec = pl.BlockSpec((tk, tm), lambda i, j, k: (k, i))
    else:
        a_spec = pl.BlockSpec((tm, tk), lambda i, j, k: (i, k))
    in_specs = [a_spec, pl.BlockSpec((tk, tn), lambda i, j, k: (k, j))]
    args = [a, b]
    if add is not None:
        in_specs.append(pl.BlockSpec((tm, tn), lambda i, j, k: (i, j)))
        args.append(add)
    if after is not None:
        in_specs.append(pl.BlockSpec(memory_space=pl.ANY))
        args.append(after)
    return pl.pallas_call(
        body, name=name, grid=(m // tm, n // tn, nk),
        in_specs=in_specs, out_specs=pl.BlockSpec((tm, tn), lambda i, j, k: (i, j)),
        out_shape=jax.ShapeDtypeStruct((m, n), out_dtype),
        scratch_shapes=[pltpu.VMEM((tm, tn), F32)],
        compiler_params=_cparams(("parallel", "parallel", "arbitrary")),
    )(*args)


def _rms_fwd(x, g, *, name, out_dtype, add=None):
    s, n = x.shape
    ts = _rows(s, 512)

    def body(*refs):
        if add is None:
            x_ref, g_ref, o_ref = refs
        else:
            x_ref, g_ref, a_ref, o_ref = refs
        xv = x_ref[...]
        r = lax.rsqrt(jnp.mean(xv * xv, axis=-1, keepdims=True) + RMS_EPS)
        y = xv * r * g_ref[...]
        if add is not None:
            y = a_ref[...] + y
        o_ref[...] = y.astype(out_dtype)

    row = pl.BlockSpec((ts, n), lambda i: (i, 0))
    in_specs = [row, pl.BlockSpec((1, n), lambda i: (0, 0))]
    args = [x, g]
    if add is not None:
        in_specs.append(row)
        args.append(add)
    return pl.pallas_call(
        body, name=name, grid=(s // ts,), in_specs=in_specs, out_specs=row,
        out_shape=jax.ShapeDtypeStruct((s, n), out_dtype),
        compiler_params=_cparams(("parallel",)),
    )(*args)


def _rms_bwd(x, g, dy, *, name, out_dtype, add=None):
    s, n = x.shape
    ts = _rows(s, 512)

    def body(*refs):
        if add is None:
            x_ref, g_ref, dy_ref, dx_ref, dg_ref = refs
        else:
            x_ref, g_ref, dy_ref, a_ref, dx_ref, dg_ref = refs
        i = pl.program_id(0)
        xv = x_ref[...]
        dyv = dy_ref[...].astype(F32)
        r = lax.rsqrt(jnp.mean(xv * xv, axis=-1, keepdims=True) + RMS_EPS)
        nx = xv * r
        gdy = dyv * g_ref[...]
        dx = r * (gdy - nx * jnp.mean(nx * gdy, axis=-1, keepdims=True))
        if add is not None:
            dx = a_ref[...] + dx
        dx_ref[...] = dx.astype(out_dtype)

        @pl.when(i == 0)
        def _():
            dg_ref[...] = jnp.zeros_like(dg_ref)

        dg_ref[...] += jnp.sum(dyv * nx, axis=0, keepdims=True)

    row = pl.BlockSpec((ts, n), lambda i: (i, 0))
    vec = pl.BlockSpec((1, n), lambda i: (0, 0))
    in_specs = [row, vec, row]
    args = [x, g, dy]
    if add is not None:
        in_specs.append(row)
        args.append(add)
    return pl.pallas_call(
        body, name=name, grid=(s // ts,), in_specs=in_specs, out_specs=[row, vec],
        out_shape=[jax.ShapeDtypeStruct((s, n), out_dtype), jax.ShapeDtypeStruct((1, n), F32)],
        compiler_params=_cparams(("arbitrary",)),
    )(*args)


def _loss_fwd(x1, f, target, g, *, name):
    s, n = x1.shape
    ts = _rows(s, 512)

    def body(x_ref, f_ref, t_ref, g_ref, dy_ref, sq_ref):
        i = pl.program_id(0)
        fv = f_ref[...]
        r = lax.rsqrt(jnp.mean(fv * fv, axis=-1, keepdims=True) + RMS_EPS)
        err = x_ref[...] + fv * r * g_ref[...] - t_ref[...]
        dy_ref[...] = err * (1.0 / n)

        @pl.when(i == 0)
        def _():
            sq_ref[...] = jnp.zeros_like(sq_ref)

        sq_ref[...] += jnp.sum(err * err, axis=0, keepdims=True)

    row = pl.BlockSpec((ts, n), lambda i: (i, 0))
    vec = pl.BlockSpec((1, n), lambda i: (0, 0))
    return pl.pallas_call(
        body, name=name, grid=(s // ts,), in_specs=[row, row, row, vec], out_specs=[row, vec],
        out_shape=[jax.ShapeDtypeStruct((s, n), F32), jax.ShapeDtypeStruct((1, n), F32)],
        compiler_params=_cparams(("arbitrary",)),
    )(x1, f, target, g)


def _rope_tables(positions):
    half = MLA_ROPE // 2
    inv_freq = ROPE_THETA ** (-jnp.arange(half, dtype=F32) / half)
    ang = positions.astype(F32)[:, None] * inv_freq[None, :]
    cos, sin = jnp.cos(ang), jnp.sin(ang)
    s = positions.shape[0]
    one = jnp.ones((s, MLA_NOPE), F32)
    zero = jnp.zeros((s, MLA_NOPE), F32)
    pad1 = jnp.ones((s, LANES - MLA_QK_DIM), F32)
    pad0 = jnp.zeros((s, LANES - MLA_QK_DIM), F32)
    zh = jnp.zeros((s, half), F32)
    c_tab = jnp.concatenate([one, cos, cos, pad1], axis=1)
    s1_tab = jnp.concatenate([zero, -sin, zh, pad0], axis=1)
    s2_tab = jnp.concatenate([zero, zh, sin, pad0], axis=1)
    return c_tab, s1_tab, s2_tab


def _rope_fwd(x, tabs, *, name, scale, add=None):
    s, n = x.shape
    nh = n // LANES
    ts = _rows(s, 512)
    half = MLA_ROPE // 2

    def body(*refs):
        if add is None:
            x_ref, c_ref, s1_ref, s2_ref, o_ref = refs
        else:
            x_ref, a_ref, c_ref, s1_ref, s2_ref, o_ref = refs
        c, s1, s2 = c_ref[...], s1_ref[...], s2_ref[...]
        for h in range(nh):
            xh = x_ref[:, h * LANES:(h + 1) * LANES]
            if add is not None:
                xh = xh + a_ref[...]
            y = xh * c + pltpu.roll(xh, LANES - half, 1) * s1 + pltpu.roll(xh, half, 1) * s2
            o_ref[:, h * LANES:(h + 1) * LANES] = (y * scale).astype(BF16)

    row = pl.BlockSpec((ts, n), lambda i: (i, 0))
    tab = pl.BlockSpec((ts, LANES), lambda i: (i, 0))
    in_specs = [row] + ([tab] if add is not None else []) + [tab, tab, tab]
    args = [x] + ([add] if add is not None else []) + list(tabs)
    return pl.pallas_call(
        body, name=name, grid=(s // ts,), in_specs=in_specs, out_specs=row,
        out_shape=jax.ShapeDtypeStruct((s, n), BF16),
        compiler_params=_cparams(("parallel",)),
    )(*args)


def _rope_bwd(dy, tabs, *, name, scale, with_add):
    s, n = dy.shape
    nh = n // LANES
    ts = _rows(s, 512)
    half = MLA_ROPE // 2

    def body(*refs):
        if with_add:
            dy_ref, c_ref, s1_ref, s2_ref, dx_ref, da_ref = refs
        else:
            dy_ref, c_ref, s1_ref, s2_ref, dx_ref = refs
        c, s1, s2 = c_ref[...], s1_ref[...], s2_ref[...]
        tot = None
        for h in range(nh):
            g = dy_ref[:, h * LANES:(h + 1) * LANES].astype(F32)
            dx = (g * c + pltpu.roll(g * s1, half, 1) + pltpu.roll(g * s2, LANES - half, 1)) * scale
            dx_ref[:, h * LANES:(h + 1) * LANES] = dx.astype(BF16)
            tot = dx if tot is None else tot + dx
        if with_add:
            da_ref[...] = tot

    row = pl.BlockSpec((ts, n), lambda i: (i, 0))
    tab = pl.BlockSpec((ts, LANES), lambda i: (i, 0))
    out_specs = [row, tab] if with_add else row
    out_shape = [jax.ShapeDtypeStruct((s, n), BF16)]
    if with_add:
        out_shape.append(jax.ShapeDtypeStruct((s, LANES), F32))
    else:
        out_shape = out_shape[0]
    return pl.pallas_call(
        body, name=name, grid=(s // ts,), in_specs=[row, tab, tab, tab], out_specs=out_specs,
        out_shape=out_shape, compiler_params=_cparams(("parallel",)),
    )(dy, *tabs)


def _scores(q, k, scale, diag):
    s = lax.dot_general(q, k, (((1,), (1,)), ((), ())), preferred_element_type=F32)
    if scale != 1.0:
        s = s * scale
    if diag:
        rows = lax.broadcasted_iota(jnp.int32, s.shape, 0)
        cols = lax.broadcasted_iota(jnp.int32, s.shape, 1)
        s = jnp.where(cols <= rows, s, NEG_INF)
    return s


def _flash_fwd(q, k, v, *, name, heads, qoff, koff, voff, causal, scale, tq, tk):
    s_q, s_kv = q.shape[0], k.shape[0]
    tq, tk = min(tq, s_q), min(tk, s_kv)
    nq, nk = s_q // tq, s_kv // tk
    if causal:
        assert tq == tk and s_q == s_kv

    def body(q_ref, k_ref, v_ref, o_ref, lse_ref, m_s, l_s, acc):
        i, j = pl.program_id(1), pl.program_id(2)

        @pl.when(j == 0)
        def _():
            m_s[...] = jnp.full_like(m_s, NEG_INF)
            l_s[...] = jnp.zeros_like(l_s)
            acc[...] = jnp.zeros_like(acc)

        def step(diag):
            s = _scores(q_ref[...], k_ref[...], scale, diag)
            m_prev = m_s[...]
            m_cur = jnp.maximum(m_prev, jnp.max(s, axis=1, keepdims=True))
            alpha = jnp.exp(m_prev - m_cur)
            p = jnp.exp(s - m_cur[:, :1])
            l_s[...] = alpha * l_s[...] + jnp.sum(p, axis=1, keepdims=True)
            acc[...] = alpha * acc[...] + jnp.dot(p.astype(BF16), v_ref[...], preferred_element_type=F32)
            m_s[...] = m_cur

        def finish():
            o_ref[...] = (acc[...] / l_s[...]).astype(o_ref.dtype)
            lse_ref[...] = m_s[...] + jnp.log(l_s[...])

        if causal:
            @pl.when(j < i)
            def _():
                step(False)

            @pl.when(j == i)
            def _():
                step(True)
                finish()
        else:
            step(False)

            @pl.when(j == nk - 1)
            def _():
                finish()

    def kv_idx(off):
        if causal:
            return lambda h, i, j: (jnp.minimum(j, i), off + h)
        return lambda h, i, j: (j, off + h)

    blk_q = pl.BlockSpec((tq, LANES), lambda h, i, j: (i, qoff + h))
    out_q = pl.BlockSpec((tq, LANES), lambda h, i, j: (i, h))
    return pl.pallas_call(
        body, name=name, grid=(heads, nq, nk),
        in_specs=[blk_q, pl.BlockSpec((tk, LANES), kv_idx(koff)), pl.BlockSpec((tk, LANES), kv_idx(voff))],
        out_specs=[out_q, out_q],
        out_shape=[jax.ShapeDtypeStruct((s_q, heads * LANES), BF16),
                   jax.ShapeDtypeStruct((s_q, heads * LANES), F32)],
        scratch_shapes=[pltpu.VMEM((tq, LANES), F32)] * 3,
        compiler_params=_cparams(("parallel", "parallel", "arbitrary")),
    )(q, k, v)


def _flash_bwd_dq(q, k, v, do, lse, delta, *, name, heads, qoff, koff, voff, causal, scale, tq, tk, out_dtype):
    s_q, s_kv = q.shape[0], k.shape[0]
    tq, tk = min(tq, s_q), min(tk, s_kv)
    nq, nk = s_q // tq, s_kv // tk

    def body(q_ref, k_ref, v_ref, do_ref, lse_ref, dl_ref, dq_ref, acc):
        i, j = pl.program_id(1), pl.program_id(2)

        @pl.when(j == 0)
        def _():
            acc[...] = jnp.zeros_like(acc)

        def step(diag):
            s = _scores(q_ref[...], k_ref[...], scale, diag)
            p = jnp.exp(s - lse_ref[:, :1])
            dp = lax.dot_general(do_ref[...], v_ref[...], (((1,), (1,)), ((), ())), preferred_element_type=F32)
            ds = p * (dp - dl_ref[:, :1])
            acc[...] += jnp.dot(ds.astype(BF16), k_ref[...], preferred_element_type=F32)

        def finish():
            dq_ref[...] = (acc[...] * scale).astype(out_dtype)

        if causal:
            @pl.when(j < i)
            def _():
                step(False)

            @pl.when(j == i)
            def _():
                step(True)
                finish()
        else:
            step(False)

            @pl.when(j == nk - 1)
            def _():
                finish()

    def kv_idx(off):
        if causal:
            return lambda h, i, j: (jnp.minimum(j, i), off + h)
        return lambda h, i, j: (j, off + h)

    blk_q = pl.BlockSpec((tq, LANES), lambda h, i, j: (i, qoff + h))
    blk_h = pl.BlockSpec((tq, LANES), lambda h, i, j: (i, h))
    return pl.pallas_call(
        body, name=name, grid=(heads, nq, nk),
        in_specs=[blk_q, pl.BlockSpec((tk, LANES), kv_idx(koff)), pl.BlockSpec((tk, LANES), kv_idx(voff)),
                  blk_h, blk_h, blk_h],
        out_specs=blk_h,
        out_shape=jax.ShapeDtypeStruct((s_q, heads * LANES), out_dtype),
        scratch_shapes=[pltpu.VMEM((tq, LANES), F32)],
        compiler_params=_cparams(("parallel", "parallel", "arbitrary")),
    )(q, k, v, do, lse, delta)


def _flash_bwd_dkv(q, k, v, do, lse, delta, *, name, heads, qoff, koff, voff, causal, scale, tq, tk,
                   dk_dtype, dv_dtype):
    s_q, s_kv = q.shape[0], k.shape[0]
    tq, tk = min(tq, s_q), min(tk, s_kv)
    nq, nk = s_q // tq, s_kv // tk

    def body(q_ref, k_ref, v_ref, do_ref, lse_ref, dl_ref, dk_ref, dv_ref, dk_acc, dv_acc):
        j, i = pl.program_id(1), pl.program_id(2)

        @pl.when(i == 0)
        def _():
            dk_acc[...] = jnp.zeros_like(dk_acc)
            dv_acc[...] = jnp.zeros_like(dv_acc)

        def step(diag):
            s = _scores(q_ref[...], k_ref[...], scale, diag)
            p = jnp.exp(s - lse_ref[:, :1])
            dov = do_ref[...]
            dp = lax.dot_general(dov, v_ref[...], (((1,), (1,)), ((), ())), preferred_element_type=F32)
            ds = p * (dp - dl_ref[:, :1])
            dv_acc[...] += lax.dot_general(p.astype(BF16), dov, (((0,), (0,)), ((), ())),
                                           preferred_element_type=F32)
            dk_acc[...] += lax.dot_general(ds.astype(BF16), q_ref[...], (((0,), (0,)), ((), ())),
                                           preferred_element_type=F32)

        if causal:
            @pl.when(i > j)
            def _():
                step(False)

            @pl.when(i == j)
            def _():
                step(True)
        else:
            step(False)

        @pl.when(i == nq - 1)
        def _():
            dk_ref[...] = (dk_acc[...] * scale).astype(dk_dtype)
            dv_ref[...] = dv_acc[...].astype(dv_dtype)

    def q_idx(off):
        if causal:
            return lambda h, j, i: (jnp.maximum(i, j), off + h)
        return lambda h, j, i: (i, off + h)

    blk_h = pl.BlockSpec((tq, LANES), q_idx(0))
    out_k = pl.BlockSpec((tk, LANES), lambda h, j, i: (j, h))
    return pl.pallas_call(
        body, name=name, grid=(heads, nk, nq),
        in_specs=[pl.BlockSpec((tq, LANES), q_idx(qoff)),
                  pl.BlockSpec((tk, LANES), lambda h, j, i: (j, koff + h)),
                  pl.BlockSpec((tk, LANES), lambda h, j, i: (j, voff + h)),
                  blk_h, blk_h, blk_h],
        out_specs=[out_k, out_k],
        out_shape=[jax.ShapeDtypeStruct((s_kv, heads * LANES), dk_dtype),
                   jax.ShapeDtypeStruct((s_kv, heads * LANES), dv_dtype)],
        scratch_shapes=[pltpu.VMEM((tk, LANES), F32)] * 2,
        compiler_params=_cparams(("parallel", "parallel", "arbitrary")),
    )(q, k, v, do, lse, delta)


def _row_dot(a, b, *, name):
    s, n = a.shape
    nh = n // LANES
    ts = _rows(s, 512)

    def body(a_ref, b_ref, o_ref):
        for h in range(nh):
            sl = slice(h * LANES, (h + 1) * LANES)
            d = jnp.sum(a_ref[:, sl].astype(F32) * b_ref[:, sl].astype(F32), axis=1, keepdims=True)
            o_ref[:, sl] = jnp.broadcast_to(d, (ts, LANES))

    row = pl.BlockSpec((ts, n), lambda i: (i, 0))
    return pl.pallas_call(
        body, name=name, grid=(s // ts,), in_specs=[row, row], out_specs=row,
        out_shape=jax.ShapeDtypeStruct((s, n), F32), compiler_params=_cparams(("parallel",)),
    )(a, b)


CAUSAL_T = 512
LOG2E = 1.4426950408889634
LN2 = 0.6931471805599453


def _causal_fwd(q, k, v, *, name, heads, ones_lane):
    s = q.shape[0]
    t = CAUSAL_T
    nq = s // (2 * t)
    assert nq * 2 * t == s

    def body(q_ref, k_ref, v_ref, o_ref, lse_ref, v1, m_s, acc):
        i = pl.program_id(1)

        @pl.when(i == 0)
        def _():
            lane = lax.broadcasted_iota(jnp.int32, v1.shape, 1)
            v1[...] = jnp.where(lane == ones_lane, 1.0, v_ref[...]).astype(BF16)

        m_s[...] = jnp.full_like(m_s, NEG_INF)
        acc[...] = jnp.zeros_like(acc)
        halves = (q_ref[0:t, :], q_ref[t:2 * t, :])

        def raw(c, j):
            rows = pl.ds(pl.multiple_of(j * t, t), t)
            return lax.dot_general(halves[c], k_ref[rows, :], (((1,), (1,)), ((), ())), preferred_element_type=F32)

        def update(c, sc, j):
            m_prev = m_s[c]
            m_cur = jnp.maximum(m_prev, jnp.max(sc, axis=1, keepdims=True))
            p = jnp.exp2(sc - m_cur[:, :1]).astype(BF16)
            acc[c] = jnp.exp2(m_prev - m_cur) * acc[c] + jnp.dot(
                p, v1[pl.ds(pl.multiple_of(j * t, t), t), :], preferred_element_type=F32)
            m_s[c] = m_cur

        def loop(j, carry):
            sa, sb = raw(0, j), raw(1, j)
            update(0, sa, j)
            update(1, sb, j)
            return carry

        lax.fori_loop(0, 2 * i, loop, 0)
        sa, sb = raw(0, 2 * i), raw(1, 2 * i)
        below = (lax.broadcasted_iota(jnp.int32, sa.shape, 1) <= lax.broadcasted_iota(jnp.int32, sa.shape, 0))
        update(0, jnp.where(below, sa, NEG_INF), 2 * i)
        update(1, sb, 2 * i)
        update(1, jnp.where(below, raw(1, 2 * i + 1), NEG_INF), 2 * i + 1)
        lane = lax.broadcasted_iota(jnp.int32, (t, LANES), 1)
        for c in range(2):
            out = acc[c]
            den = out[:, ones_lane:ones_lane + 1]
            o_ref[c * t:(c + 1) * t, :] = jnp.where(lane == ones_lane, 0.0, out / den).astype(BF16)
            lse_ref[c * t:(c + 1) * t, :] = m_s[c] + jnp.log2(den)

    blk = pl.BlockSpec((2 * t, LANES), lambda h, i: (i, h))
    full = pl.BlockSpec((s, LANES), lambda h, i: (0, h))
    return pl.pallas_call(
        body, name=name, grid=(heads, nq), in_specs=[blk, full, full], out_specs=[blk, blk],
        out_shape=[jax.ShapeDtypeStruct((s, heads * LANES), BF16), jax.ShapeDtypeStruct((s, heads * LANES), F32)],
        scratch_shapes=[pltpu.VMEM((s, LANES), BF16), pltpu.VMEM((2, t, LANES), F32),
                        pltpu.VMEM((2, t, LANES), F32)],
        compiler_params=_cparams(("parallel", "arbitrary")),
    )(q, k, v)


def _causal_bwd(q, k, v, do, lse, delta, *, name, heads):
    s = q.shape[0]
    t = min(CAUSAL_T, s)
    nt = s // t

    def body(q_ref, k_ref, v_ref, do_ref, lse_ref, dl_ref, dq_ref, dk_ref, dv_ref, dk_acc, dv_acc):
        j = pl.program_id(1)

        @pl.when(j == 0)
        def _():
            dq_ref[...] = jnp.zeros_like(dq_ref)

        dk_acc[...] = jnp.zeros_like(dk_acc)
        dv_acc[...] = jnp.zeros_like(dv_acc)
        kv, vv = k_ref[...], v_ref[...]

        def step(i, diag, size=t):
            rows = pl.ds(pl.multiple_of(i * t, t), size)
            qv, dov = q_ref[rows, :], do_ref[rows, :]
            sc = _scores(qv, kv, 1.0, diag)
            p = jnp.exp2(sc - lse_ref[rows, :][:, :1])
            dp = lax.dot_general(dov, vv, (((1,), (1,)), ((), ())), preferred_element_type=F32)
            ds = (p * (dp - dl_ref[rows, :][:, :1])).astype(BF16)
            dv_acc[...] += lax.dot_general(p.astype(BF16), dov, (((0,), (0,)), ((), ())),
                                           preferred_element_type=F32)
            dk_acc[...] += lax.dot_general(ds, qv, (((0,), (0,)), ((), ())), preferred_element_type=F32)
            dq_ref[rows, :] += jnp.dot(ds, kv, preferred_element_type=F32)

        step(j, True)
        odd = (nt - 1 - j) % 2

        @pl.when(odd == 1)
        def _():
            step(j + 1, False)

        def loop(n, carry):
            step(j + 1 + odd + 2 * n, False, 2 * t)
            return carry

        lax.fori_loop(0, (nt - 1 - j) // 2, loop, 0)
        dk_ref[...] = dk_acc[...] * LN2
        dv_ref[...] = dv_acc[...].astype(BF16)

    blk = pl.BlockSpec((t, LANES), lambda h, j: (j, h))
    full = pl.BlockSpec((s, LANES), lambda h, j: (0, h))
    return pl.pallas_call(
        body, name=name, grid=(heads, nt), in_specs=[full, blk, blk, full, full, full],
        out_specs=[full, blk, blk],
        out_shape=[jax.ShapeDtypeStruct((s, heads * LANES), F32), jax.ShapeDtypeStruct((s, heads * LANES), F32),
                   jax.ShapeDtypeStruct((s, heads * LANES), BF16)],
        scratch_shapes=[pltpu.VMEM((t, LANES), F32)] * 2,
        compiler_params=_cparams(("parallel", "arbitrary")),
    )(q, k, v, do, lse, delta)


def _band_masks(dilation, slope):
    qi = lax.broadcasted_iota(jnp.int32, (BLOCK, 2 * BLOCK), 0)
    kj = lax.broadcasted_iota(jnp.int32, (BLOCK, 2 * BLOCK), 1)
    dist = qi + BLOCK - kj
    valid = (dist >= 0) & (dist <= BLOCK)
    bias = -slope * (dist * dilation).astype(F32)
    return valid, bias


BAND_UNROLL = 4


def _aligned(start):
    return start if isinstance(start, int) else pl.multiple_of(start, BLOCK)


def _band_fwd(q, k, v, slopes, *, name, dilation, qoff, koff, voff):
    s = q.shape[0]
    sub = s // dilation
    nb = sub // BLOCK
    assert nb * BLOCK == sub
    unroll = min(BAND_UNROLL, nb)
    assert nb % unroll == 0
    scale = LANES ** -0.5

    def body(sl_ref, q_ref, k_ref, v_ref, o_ref, lse_ref):
        slope = sl_ref[pl.program_id(0)]
        valid2, bias2 = _band_masks(dilation, slope)
        valid1, bias1 = valid2[:, BLOCK:], bias2[:, BLOCK:]

        def block(start_q, kk, vv, valid, bias):
            qb = q_ref[pl.ds(start_q, BLOCK), :]
            sc = lax.dot_general(qb, kk, (((1,), (1,)), ((), ())), preferred_element_type=F32) * scale
            sc = jnp.where(valid, sc + bias, NEG_INF)
            m = jnp.max(sc, axis=1, keepdims=True)
            e = jnp.exp(sc - m)
            den = jnp.sum(e, axis=1, keepdims=True)
            p = (e / den).astype(BF16)
            o_ref[pl.ds(start_q, BLOCK), :] = jnp.dot(p, vv, preferred_element_type=F32)
            lse_ref[pl.ds(start_q, BLOCK), :] = jnp.broadcast_to(m + jnp.log(den), (BLOCK, LANES))

        block(0, k_ref[0:BLOCK, :], v_ref[0:BLOCK, :], valid1, bias1)

        def general(jj):
            start_q, start_k = _aligned(jj * BLOCK), _aligned((jj - 1) * BLOCK)
            block(start_q, k_ref[pl.ds(start_k, 2 * BLOCK), :], v_ref[pl.ds(start_k, 2 * BLOCK), :], valid2, bias2)

        for jj in range(1, unroll):
            general(jj)

        def loop(t, carry):
            for u in range(unroll):
                general(t * unroll + u)
            return carry

        lax.fori_loop(1, nb // unroll, loop, 0)

    def spec(off):
        return pl.BlockSpec((sub, LANES), lambda h, r: (r, off + h))

    out = pl.BlockSpec((sub, LANES), lambda h, r: (r, h))
    return pl.pallas_call(
        body, name=name, grid=(DIL_HEADS_PER_GROUP, dilation),
        in_specs=[pl.BlockSpec(memory_space=pltpu.SMEM), spec(qoff), spec(koff), spec(voff)],
        out_specs=[out, out],
        out_shape=[jax.ShapeDtypeStruct((s, DIL_HEADS_PER_GROUP * LANES), F32)] * 2,
        compiler_params=_cparams(("parallel", "parallel"), VMEM_LIMIT_BIG),
    )(slopes, q, k, v)


def _band_bwd(q, k, v, do, lse, delta, slopes, *, name, dilation, qoff, koff, voff):
    s = q.shape[0]
    sub = s // dilation
    nb = sub // BLOCK
    unroll = min(BAND_UNROLL, nb)
    scale = LANES ** -0.5

    def body(sl_ref, q_ref, k_ref, v_ref, do_ref, lse_ref, dl_ref, dq_ref, dk_ref, dv_ref):
        slope = sl_ref[pl.program_id(0)]
        valid2, bias2 = _band_masks(dilation, slope)
        valid1, bias1 = valid2[:, BLOCK:], bias2[:, BLOCK:]

        def block(start_q, kk, vv, valid, bias):
            qb = q_ref[pl.ds(start_q, BLOCK), :]
            dob = do_ref[pl.ds(start_q, BLOCK), :]
            sc = lax.dot_general(qb, kk, (((1,), (1,)), ((), ())), preferred_element_type=F32) * scale
            sc = jnp.where(valid, sc + bias, NEG_INF)
            p = jnp.exp(sc - lse_ref[pl.ds(start_q, BLOCK), :][:, :1])
            dp = lax.dot_general(dob, vv, (((1,), (1,)), ((), ())), preferred_element_type=F32)
            ds = (p * (dp - dl_ref[pl.ds(start_q, BLOCK), :][:, :1])).astype(BF16)
            dq = jnp.dot(ds, kk, preferred_element_type=F32) * scale
            dq_ref[pl.ds(start_q, BLOCK), :] = dq.astype(BF16)
            dkk = lax.dot_general(ds, qb, (((0,), (0,)), ((), ())), preferred_element_type=F32) * scale
            dvv = lax.dot_general(p.astype(BF16), dob, (((0,), (0,)), ((), ())), preferred_element_type=F32)
            return dkk, dvv

        carry0 = block(0, k_ref[0:BLOCK, :], v_ref[0:BLOCK, :], valid1, bias1)

        def general(jj, carry):
            dk_part, dv_part = carry
            start_q, start_k = _aligned(jj * BLOCK), _aligned((jj - 1) * BLOCK)
            dkk, dvv = block(start_q, k_ref[pl.ds(start_k, 2 * BLOCK), :], v_ref[pl.ds(start_k, 2 * BLOCK), :],
                             valid2, bias2)
            dk_ref[pl.ds(start_k, BLOCK), :] = (dk_part + dkk[:BLOCK]).astype(BF16)
            dv_ref[pl.ds(start_k, BLOCK), :] = (dv_part + dvv[:BLOCK]).astype(BF16)
            return dkk[BLOCK:], dvv[BLOCK:]

        for jj in range(1, unroll):
            carry0 = general(jj, carry0)

        def loop(t, carry):
            for u in range(unroll):
                carry = general(t * unroll + u, carry)
            return carry

        dk_last, dv_last = lax.fori_loop(1, nb // unroll, loop, carry0)
        dk_ref[(nb - 1) * BLOCK:nb * BLOCK, :] = dk_last.astype(BF16)
        dv_ref[(nb - 1) * BLOCK:nb * BLOCK, :] = dv_last.astype(BF16)

    def spec(off):
        return pl.BlockSpec((sub, LANES), lambda h, r: (r, off + h))

    out = spec(0)
    return pl.pallas_call(
        body, name=name, grid=(DIL_HEADS_PER_GROUP, dilation),
        in_specs=[pl.BlockSpec(memory_space=pltpu.SMEM), spec(qoff), spec(koff), spec(voff), out, out, out],
        out_specs=[out, out, out],
        out_shape=[jax.ShapeDtypeStruct((s, DIL_HEADS_PER_GROUP * LANES), BF16)] * 3,
        compiler_params=_cparams(("parallel", "parallel"), VMEM_LIMIT_BIG),
    )(slopes, q, k, v, do, lse, delta)


def _mix_fwd(outs, lses, *, name):
    s, n = outs[0].shape
    ts = _rows(s, 512)

    def body(o0, o1, o2, l0, l1, l2, y_ref):
        la, lb, lc = l0[...], l1[...], l2[...]
        m = jnp.maximum(jnp.maximum(la, lb), lc)
        ea, eb, ec = jnp.exp(la - m), jnp.exp(lb - m), jnp.exp(lc - m)
        den = ea + eb + ec
        y = (ea / den) * o0[...] + (eb / den) * o1[...] + (ec / den) * o2[...]
        y_ref[...] = y.astype(BF16)

    row = pl.BlockSpec((ts, n), lambda i: (i, 0))
    return pl.pallas_call(
        body, name=name, grid=(s // ts,), in_specs=[row] * 6, out_specs=row,
        out_shape=jax.ShapeDtypeStruct((s, n), BF16), compiler_params=_cparams(("parallel",)),
    )(*outs, *lses)


def _mix_bwd(dy, outs, lses, *, name):
    s, n = dy.shape
    nh = n // LANES
    ts = _rows(s, 256)

    def body(dy_ref, o0, o1, o2, l0, l1, l2, d0, d1, d2, e0, e1, e2):
        la, lb, lc = l0[...], l1[...], l2[...]
        m = jnp.maximum(jnp.maximum(la, lb), lc)
        ea, eb, ec = jnp.exp(la - m), jnp.exp(lb - m), jnp.exp(lc - m)
        den = ea + eb + ec
        wa, wb, wc = ea / den, eb / den, ec / den
        dyv = dy_ref[...]
        y = wa * o0[...] + wb * o1[...] + wc * o2[...]
        prod = dyv * y
        d0[...] = (wa * dyv).astype(BF16)
        d1[...] = (wb * dyv).astype(BF16)
        d2[...] = (wc * dyv).astype(BF16)
        for h in range(nh):
            sl = slice(h * LANES, (h + 1) * LANES)
            t = jnp.sum(prod[:, sl], axis=1, keepdims=True)
            e0[:, sl] = wa[:, sl] * t
            e1[:, sl] = wb[:, sl] * t
            e2[:, sl] = wc[:, sl] * t

    row = pl.BlockSpec((ts, n), lambda i: (i, 0))
    return pl.pallas_call(
        body, name=name, grid=(s // ts,), in_specs=[row] * 7, out_specs=[row] * 6,
        out_shape=[jax.ShapeDtypeStruct((s, n), BF16)] * 3 + [jax.ShapeDtypeStruct((s, n), F32)] * 3,
        compiler_params=_cparams(("parallel",)),
    )(dy, *outs, *lses)


def _gate_fwd(gp, b_gate, branches, *, name):
    s = gp.shape[0]
    ts = _rows(s, 256)

    def body(gp_ref, b_ref, b0, b1, b2, o_ref):
        tot = None
        for i, br in enumerate((b0, b1, b2)):
            sl = slice(i * D_MODEL, (i + 1) * D_MODEL)
            t = jax.nn.sigmoid(gp_ref[:, sl] + b_ref[:, sl]) * br[...]
            tot = t if tot is None else tot + t
        o_ref[...] = tot.astype(BF16)

    row = pl.BlockSpec((ts, D_MODEL), lambda i: (i, 0))
    return pl.pallas_call(
        body, name=name, grid=(s // ts,),
        in_specs=[pl.BlockSpec((ts, 3 * D_MODEL), lambda i: (i, 0)), pl.BlockSpec((1, 3 * D_MODEL), lambda i: (0, 0)),
                  row, row, row],
        out_specs=row, out_shape=jax.ShapeDtypeStruct((s, D_MODEL), BF16),
        compiler_params=_cparams(("parallel",)),
    )(gp, b_gate, *branches)


def _gate_bwd(dm, gp, b_gate, branches, *, name):
    s = gp.shape[0]
    ts = _rows(s, 256)

    def body(dm_ref, gp_ref, b_ref, b0, b1, b2, d0, d1, d2, dgp_ref, db_ref):
        i = pl.program_id(0)

        @pl.when(i == 0)
        def _():
            db_ref[...] = jnp.zeros_like(db_ref)

        dmv = dm_ref[...]
        for k, (br, dbr) in enumerate(((b0, d0), (b1, d1), (b2, d2))):
            sl = slice(k * D_MODEL, (k + 1) * D_MODEL)
            sg = jax.nn.sigmoid(gp_ref[:, sl] + b_ref[:, sl])
            dbr[...] = (dmv * sg).astype(BF16)
            dg = dmv * br[...] * sg * (1.0 - sg)
            dgp_ref[:, sl] = dg.astype(BF16)
            db_ref[:, sl] += jnp.sum(dg, axis=0, keepdims=True)

    row = pl.BlockSpec((ts, D_MODEL), lambda i: (i, 0))
    wide = pl.BlockSpec((ts, 3 * D_MODEL), lambda i: (i, 0))
    vec = pl.BlockSpec((1, 3 * D_MODEL), lambda i: (0, 0))
    return pl.pallas_call(
        body, name=name, grid=(s // ts,),
        in_specs=[row, wide, vec, row, row, row], out_specs=[row, row, row, wide, vec],
        out_shape=[jax.ShapeDtypeStruct((s, D_MODEL), BF16)] * 3
        + [jax.ShapeDtypeStruct((s, 3 * D_MODEL), BF16), jax.ShapeDtypeStruct((1, 3 * D_MODEL), F32)],
        compiler_params=_cparams(("arbitrary",)),
    )(dm, gp, b_gate, *branches)


CONV_TC = 1408


def _shift_down(x, halo, k):
    rolled = pltpu.roll(x, k, 0)
    r8 = lax.broadcasted_iota(jnp.int32, halo.shape, 0)
    top = jnp.where(r8 < k, pltpu.roll(halo, k, 0), rolled[:SUBLANES])
    return jnp.concatenate([top, rolled[SUBLANES:]], axis=0)


def _shift_up(x, halo, k):
    n = x.shape[0]
    rolled = pltpu.roll(x, n - k, 0)
    r8 = lax.broadcasted_iota(jnp.int32, halo.shape, 0)
    bot = jnp.where(r8 >= SUBLANES - k, pltpu.roll(halo, SUBLANES - k, 0), rolled[n - SUBLANES:])
    return jnp.concatenate([rolled[:n - SUBLANES], bot], axis=0)


def _conv_fwd(u, conv_w, conv_b, *, name):
    s = u.shape[0]
    ts = _rows(s, 256)
    nct = D_FF // CONV_TC
    per8 = ts // SUBLANES

    def body(ug, uv, hg, hv, wg, wv, bg, bv, zg_ref, zv_ref, a_ref):
        first = pl.program_id(1) == 0

        def conv(u_ref, h_ref, w_ref, b_ref):
            x = u_ref[...]
            halo = jnp.where(first, 0.0, h_ref[...])
            z = b_ref[...] + w_ref[0:1, :] * _shift_down(x, halo, 2)
            z = z + w_ref[1:2, :] * _shift_down(x, halo, 1)
            return z + w_ref[2:3, :] * x

        zg = conv(ug, hg, wg, bg)
        zv = conv(uv, hv, wv, bv)
        zg_ref[...] = zg
        zv_ref[...] = zv
        a_ref[...] = (zg * jax.nn.sigmoid(zg) * zv).astype(BF16)

    def col(off):
        return pl.BlockSpec((ts, CONV_TC), lambda c, i: (i, c + off))

    def halo(off):
        return pl.BlockSpec((SUBLANES, CONV_TC), lambda c, i: (jnp.maximum(i * per8 - 1, 0), c + off))

    def wspec(rows, off):
        return pl.BlockSpec((rows, CONV_TC), lambda c, i: (0, c + off))

    zg, zv, a = pl.pallas_call(
        body, name=name, grid=(nct, s // ts),
        in_specs=[col(0), col(nct), halo(0), halo(nct), wspec(3, 0), wspec(3, nct), wspec(1, 0), wspec(1, nct)],
        out_specs=[col(0), col(0), col(0)],
        out_shape=[jax.ShapeDtypeStruct((s, D_FF), F32)] * 2 + [jax.ShapeDtypeStruct((s, D_FF), BF16)],
        compiler_params=_cparams(("parallel", "parallel")),
    )(u, u, u, u, conv_w, conv_w, conv_b, conv_b)
    return zg, zv, a


def _conv_bwd(da, zg, zv, u, conv_w, *, name):
    s = da.shape[0]
    ts = _rows(s, 256)
    nct = D_FF // CONV_TC
    per8 = ts // SUBLANES
    nrow = s // ts
    last8 = s // SUBLANES - 1

    def dz_of(dav, g, val):
        sg = jax.nn.sigmoid(g)
        return dav * val * sg * (1.0 + g * (1.0 - sg)), dav * g * sg

    def body(da_ref, zg_ref, zv_ref, da_nx, zg_nx, zv_nx, ug_ref, uv_ref, ug_pv, uv_pv, wg_ref, wv_ref,
             dug_ref, duv_ref, accg_ref, accv_ref):
        i = pl.program_id(1)
        dzg, dzv = dz_of(da_ref[...], zg_ref[...], zv_ref[...])
        da_next = jnp.where(i == nrow - 1, 0.0, da_nx[...])
        nxg, nxv = dz_of(da_next, zg_nx[...], zv_nx[...])

        @pl.when(i == 0)
        def _():
            accg_ref[...] = jnp.zeros_like(accg_ref)
            accv_ref[...] = jnp.zeros_like(accv_ref)

        for dz, nxt, u_ref, pv_ref, w_ref, du_ref, acc_ref in (
                (dzg, nxg, ug_ref, ug_pv, wg_ref, dug_ref, accg_ref),
                (dzv, nxv, uv_ref, uv_pv, wv_ref, duv_ref, accv_ref)):
            du = w_ref[2:3, :] * dz + w_ref[1:2, :] * _shift_up(dz, nxt, 1) + w_ref[0:1, :] * _shift_up(dz, nxt, 2)
            du_ref[...] = du.astype(BF16)
            x = u_ref[...]
            prev = jnp.where(i == 0, 0.0, pv_ref[...])
            acc_ref[0:1, :] += jnp.sum(dz * _shift_down(x, prev, 2), axis=0, keepdims=True)
            acc_ref[1:2, :] += jnp.sum(dz * _shift_down(x, prev, 1), axis=0, keepdims=True)
            acc_ref[2:3, :] += jnp.sum(dz * x, axis=0, keepdims=True)
            acc_ref[3:4, :] += jnp.sum(dz, axis=0, keepdims=True)

    def blk(off):
        return pl.BlockSpec((ts, CONV_TC), lambda c, i: (i, c + off))

    def nxt8(off):
        return pl.BlockSpec((SUBLANES, CONV_TC), lambda c, i: (jnp.minimum((i + 1) * per8, last8), c + off))

    def prv8(off):
        return pl.BlockSpec((SUBLANES, CONV_TC), lambda c, i: (jnp.maximum(i * per8 - 1, 0), c + off))

    def wspec(off):
        return pl.BlockSpec((3, CONV_TC), lambda c, i: (0, c + off))

    acc = pl.BlockSpec((SUBLANES, CONV_TC), lambda c, i: (0, c))
    return pl.pallas_call(
        body, name=name, grid=(nct, nrow),
        in_specs=[blk(0), blk(0), blk(0), nxt8(0), nxt8(0), nxt8(0), blk(0), blk(nct), prv8(0), prv8(nct),
                  wspec(0), wspec(nct)],
        out_specs=[blk(0), blk(0), acc, acc],
        out_shape=[jax.ShapeDtypeStruct((s, D_FF), BF16)] * 2 + [jax.ShapeDtypeStruct((SUBLANES, D_FF), F32)] * 2,
        compiler_params=_cparams(("parallel", "arbitrary")),
    )(da, zg, zv, da, zg, zv, u, u, u, u, conv_w, conv_w)


def _peer(k):
    x, y, c = lax.axis_index("x"), lax.axis_index("y"), lax.axis_index("c")
    px = 1 - x if k & 4 else x
    py = 1 - y if k & 2 else y
    pc = 1 - c if k & 1 else c
    return (px, py, pc), 4 * px + 2 * py + pc


def _exchange(bufs, *, name, gather):
    n = len(bufs)
    npeer = N_DEV - 1

    def body(*refs):
        srcs, outs = refs[:n], refs[n:2 * n]
        send_sems, recv_sems, local_sems = refs[2 * n:]
        _, me = _peer(0)
        mine = [src if gather else src.at[me] for src in srcs]
        local = [pltpu.make_async_copy(mine[p], outs[p].at[me], local_sems.at[p]) for p in range(n)]
        for cp in local:
            cp.start()
        sends = []
        for k in range(1, N_DEV):
            dev, idx = _peer(k)
            for p in range(n):
                cp = pltpu.make_async_remote_copy(
                    src_ref=srcs[p] if gather else srcs[p].at[idx], dst_ref=outs[p].at[me],
                    send_sem=send_sems.at[p * npeer + k - 1], recv_sem=recv_sems.at[p * npeer + k - 1],
                    device_id=dev, device_id_type=pl.DeviceIdType.MESH)
                cp.start()
                sends.append(cp)
        for k in range(1, N_DEV):
            dev, idx = _peer(k)
            for p in range(n):
                pltpu.make_async_remote_copy(
                    src_ref=mine[p], dst_ref=outs[p].at[idx],
                    send_sem=send_sems.at[p * npeer + k - 1], recv_sem=recv_sems.at[p * npeer + k - 1],
                    device_id=dev, device_id_type=pl.DeviceIdType.MESH).wait_recv()
        for cp in sends:
            cp.wait_send()
        for cp in local:
            cp.wait()

    any_spec = pl.BlockSpec(memory_space=pl.ANY)
    return pl.pallas_call(
        body, name=name,
        in_specs=[any_spec] * n, out_specs=[any_spec] * n,
        out_shape=[jax.ShapeDtypeStruct((N_DEV,) + b.shape[-2:], b.dtype) for b in bufs],
        scratch_shapes=[pltpu.SemaphoreType.DMA((n * npeer,)), pltpu.SemaphoreType.DMA((n * npeer,)),
                        pltpu.SemaphoreType.DMA((n,))],
    )(*bufs)


_HBM = pl.BlockSpec(memory_space=pltpu.HBM)
_SEM = pl.BlockSpec(memory_space=pltpu.SEMAPHORE)
_EFFECT = pltpu.SideEffectType.DATAFLOW_SIDE_EFFECTING


def _split_copy(srcs, lands, send_sems, recv_sems, gather, k, p):
    _, me = _peer(0)
    dev, idx = _peer(k)
    sem = p * (N_DEV - 1) + k - 1
    return pltpu.make_async_remote_copy(
        src_ref=srcs[p] if gather else srcs[p].at[idx], dst_ref=lands[p].at[me],
        send_sem=send_sems.at[sem], recv_sem=recv_sems.at[sem],
        device_id=dev, device_id_type=pl.DeviceIdType.MESH)


def _exchange_start(bufs, after, *, name, gather):
    n = len(bufs)
    nsem = n * (N_DEV - 1)

    def body(*refs):
        srcs, lands = refs[:n], refs[n:2 * n]
        send_sems, recv_sems = refs[2 * n + 1], refs[2 * n + 2]
        token = refs[-1]
        for k in range(1, N_DEV):
            for p in range(n):
                _split_copy(srcs, lands, send_sems, recv_sems, gather, k, p).start()
        token[...] = jnp.zeros_like(token)

    hbm = lambda a: pltpu.with_memory_space_constraint(a, pltpu.HBM)
    lands = [lax.empty((N_DEV,) + b.shape[-2:], b.dtype) for b in bufs]
    mem = [pltpu.HBM(b.shape, b.dtype) for b in bufs] + [pltpu.HBM(l.shape, l.dtype) for l in lands]
    outs = pl.pallas_call(
        body, name=name,
        in_specs=[_HBM] * (2 * n) + [pl.BlockSpec(memory_space=pl.ANY)],
        out_specs=[_SEM, _SEM] + [_HBM] * (2 * n) + [pl.BlockSpec(memory_space=pltpu.VMEM)],
        out_shape=[pltpu.SemaphoreType.DMA((nsem,)), pltpu.SemaphoreType.DMA((nsem,))] + mem
        + [jax.ShapeDtypeStruct((SUBLANES, LANES), F32)],
        input_output_aliases={p: 2 + p for p in range(2 * n)},
        compiler_params=pltpu.CompilerParams(has_side_effects=_EFFECT),
    )(*[hbm(b) for b in bufs], *[hbm(l) for l in lands], after)
    return (outs[0], outs[1], outs[2:2 + n], outs[2 + n:2 + 2 * n]), outs[-1]


def _exchange_wait(handle, after, *, name, gather):
    send_sems, recv_sems, srcs, lands = handle
    n = len(srcs)

    def body(*refs):
        src_refs, land_refs = refs[:n], refs[n:2 * n]
        send_ref, recv_ref = refs[2 * n], refs[2 * n + 1]
        for k in range(1, N_DEV):
            for p in range(n):
                cp = _split_copy(src_refs, land_refs, send_ref, recv_ref, gather, k, p)
                cp.wait_send()
                cp.wait_recv()

    mem = [pltpu.HBM(b.shape, b.dtype) for b in srcs] + [pltpu.HBM(l.shape, l.dtype) for l in lands]
    outs = pl.pallas_call(
        body, name=name,
        in_specs=[_HBM] * (2 * n) + [_SEM, _SEM, pl.BlockSpec(memory_space=pl.ANY)],
        out_specs=[_HBM] * (2 * n), out_shape=mem,
        input_output_aliases={p: p for p in range(2 * n)},
        compiler_params=pltpu.CompilerParams(has_side_effects=_EFFECT),
    )(*srcs, *lands, send_sems, recv_sems, after)
    return outs[n:]


def _with_own(landed, own, me):
    return lax.dynamic_update_slice(landed, own[None], (me, 0, 0))


def _adamw(parts, w, m, v, *, name):
    rows, width = w.shape
    tr = _rows(rows, max(16, (128 * 1024) // width), mult=16)

    def body(p_ref, w_ref, m_ref, v_ref, g_ref, d_ref, nm_ref, nv_ref):
        g = p_ref[0].astype(F32)
        for k in range(1, N_DEV):
            g = g + p_ref[k].astype(F32)
        mn = ADAM_B1 * m_ref[...] + (1.0 - ADAM_B1) * g
        vn = ADAM_B2 * v_ref[...] + (1.0 - ADAM_B2) * jnp.square(g)
        m_hat = mn / (1.0 - ADAM_B1 ** ADAM_STEP)
        v_hat = vn / (1.0 - ADAM_B2 ** ADAM_STEP)
        g_ref[...] = g
        d_ref[...] = -ADAM_LR * (m_hat / (jnp.sqrt(v_hat) + ADAM_EPS) + ADAM_WD * w_ref[...])
        nm_ref[...] = mn
        nv_ref[...] = vn

    row = pl.BlockSpec((tr, width), lambda i: (i, 0))
    return pl.pallas_call(
        body, name=name, grid=(rows // tr,),
        in_specs=[pl.BlockSpec((N_DEV, tr, width), lambda i: (0, i, 0)), row, row, row],
        out_specs=[row] * 4, out_shape=[jax.ShapeDtypeStruct((rows, width), F32)] * 4,
        compiler_params=_cparams(("parallel",)),
    )(parts, w, m, v)


def _pack_replicated(blocks):
    flat = jnp.concatenate([blocks[name].reshape(-1).astype(F32) for name, _, _ in REPLICATED])
    flat = jnp.pad(flat, (0, PACK_ROWS * PACK_W - PACK_USED))
    return flat.reshape(PACK_ROWS, PACK_W)


def _unpack_replicated(buf):
    flat = buf.reshape(-1)
    return {name: flat[off:off + n].reshape(1, n) for name, (off, n) in PACK_TABLE.items()}


def _join_shards(seg, shape, axis):
    return seg.reshape(shape) if axis == 0 else seg.transpose(1, 0, 2).reshape(shape)


def _split_shards(g, shape, axis):
    r, c = shape
    if axis == 0:
        return g.reshape(N_DEV, r // N_DEV, c)
    return g.reshape(r, N_DEV, c // N_DEV).transpose(1, 0, 2)


def _to_residues(a, d):
    s, c = a.shape
    return a.reshape(s // d, d, c).transpose(1, 0, 2).reshape(s, c)


def _from_residues(a, d):
    s, c = a.shape
    return a.reshape(d, s // d, c).transpose(1, 0, 2).reshape(s, c)


def _pad_heads(w, heads, width, lo, hi):
    r = w.shape[0]
    w = w.reshape(r, heads, width)[:, :, lo:hi]
    w = jnp.pad(w, ((0, 0), (0, 0), (0, LANES - (hi - lo))))
    return w.reshape(r, heads * LANES)


class _NoOverlap:
    start_token = None

    def late_weights(self, w, after):
        return w

    def early_grads(self, names, grads):
        return None


def _after(vec, token):
    return vec if token is None else vec + token[0:1, 0:1]


def _local_step(x, mem, positions, target, w, hooks=_NoOverlap()):
    s = x.shape[0]
    bf = lambda a: a.astype(BF16)

    w_in = w["w_in"]
    kr_cols = jnp.pad(w_in[:, OFF_KV:OFF_KR], ((0, 0), (MLA_NOPE, LANES - MLA_QK_DIM)))
    w_a = bf(jnp.concatenate([w_in[:, :OFF_KV], kr_cols], axis=1))
    w_dm = bf(w_in[:, OFF_KR:OFF_MEMQ])
    w_g = bf(w_in[:, OFF_MEMQ:])
    w_in_t = jnp.concatenate([w_a, w_dm, w_g], axis=1).T
    tabs = _rope_tables(positions)

    h = _rms_fwd(x, _after(w["g_pre_mix"], hooks.start_token), name="rms_pre_mix", out_dtype=BF16)
    p_a = _matmul(h, w_a, name="proj_a")
    p_dm = _matmul(h, w_dm, name="proj_dm", out_dtype=BF16)
    p_g = _matmul(h, w_g, name="proj_gate")
    c_q, c_kv, kr = p_a[:, :OFF_Q], p_a[:, OFF_Q:OFF_KV], p_a[:, OFF_KV:]

    w = hooks.late_weights(w, p_g)
    wq = bf(_pad_heads(w["w_uq"], MLA_HEADS, MLA_QK_DIM, 0, MLA_QK_DIM))
    wk = bf(_pad_heads(w["w_ukv"], MLA_HEADS, MLA_NOPE + MLA_V, 0, MLA_NOPE))
    wv = bf(_pad_heads(w["w_ukv"], MLA_HEADS, MLA_NOPE + MLA_V, MLA_NOPE, MLA_NOPE + MLA_V))
    w_mkv = bf(w["w_mem_kv"])
    wb_mla = bf(jnp.pad(w["w_br_mla"].reshape(MLA_HEADS, MLA_V, D_MODEL),
                        ((0, 0), (0, LANES - MLA_V), (0, 0))).reshape(MLA_HEADS * LANES, D_MODEL))
    wb_dil, wb_mem, w_o = bf(w["w_br_dil"]), bf(w["w_br_mem"]), bf(w["w_o"])
    w_up, w_down = bf(w["w_ffn_up"]), bf(w["w_ffn_down"])
    slopes = [jnp.asarray(sl, F32) for sl in DIL_SLOPES]
    mla_scale = MLA_QK_DIM ** -0.5
    mem_scale = LANES ** -0.5
    MQ = 3 * DIL_HEADS

    qn = _rms_fwd(c_q, w["mla_q_norm"], name="rms_q", out_dtype=BF16)
    kvn = _rms_fwd(c_kv, w["mla_kv_norm"], name="rms_kv", out_dtype=BF16)
    q_raw = _matmul(qn, wq, name="mla_q_up")
    k_raw = _matmul(kvn, wk, name="mla_k_up")
    v_f = _matmul(kvn, wv, name="mla_v_up", out_dtype=BF16)
    q_f = _rope_fwd(q_raw, tabs, name="rope_q", scale=mla_scale * LOG2E)
    k_f = _rope_fwd(k_raw, tabs, name="rope_k", scale=1.0, add=kr)
    o_mla, lse_mla = _causal_fwd(q_f, k_f, v_f, name="mla_fwd", heads=MLA_HEADS, ones_lane=MLA_V)

    dil_in, dil_o, dil_lse = [], [], []
    for g, (_, d) in enumerate(DIL_PAIRS):
        if d == 1:
            arrs, offs = (p_dm, p_dm, p_dm), (4 * g, DIL_HEADS + 4 * g, 2 * DIL_HEADS + 4 * g)
        else:
            arrs = tuple(_to_residues(p_dm[:, (t * DIL_HEADS + 4 * g) * LANES:(t * DIL_HEADS + 4 * g + 4) * LANES], d)
                         for t in range(3))
            offs = (0, 0, 0)
        o_g, lse_g = _band_fwd(*arrs, slopes[g], name=f"dil_fwd_{g}", dilation=d,
                               qoff=offs[0], koff=offs[1], voff=offs[2])
        dil_in.append((arrs, offs))
        dil_o.append(_from_residues(o_g, d))
        dil_lse.append(_from_residues(lse_g, d))
    y_dil = _mix_fwd(dil_o, dil_lse, name="dil_mix")

    memn = _rms_fwd(mem, w["g_mem"], name="rms_mem", out_dtype=BF16)
    kv_m = _matmul(memn, w_mkv, name="mem_kv", out_dtype=BF16)
    memat = dict(heads=MEM_HEADS, qoff=MQ, koff=0, voff=MEM_HEADS, causal=False, scale=mem_scale, tq=512, tk=256)
    o_mem, lse_mem = _flash_fwd(p_dm, kv_m, kv_m, name="mem_fwd", **memat)

    b_mla = _matmul(o_mla, wb_mla, name="br_mla")
    b_dil = _matmul(y_dil, wb_dil, name="br_dil")
    b_mem = _matmul(o_mem, wb_mem, name="br_mem")
    merged = _gate_fwd(p_g, w["b_gate"], (b_mla, b_dil, b_mem), name="gate_fwd")
    z1 = _matmul(merged, w_o, name="out_proj")
    x1 = _rms_fwd(z1, w["g_post_mix"], name="rms_post_mix", out_dtype=F32, add=x)
    h2 = _rms_fwd(x1, w["g_pre_ffn"], name="rms_pre_ffn", out_dtype=BF16)
    u = _matmul(h2, w_up, name="ffn_up")
    zg, zv, act = _conv_fwd(u, w["conv_w"], w["conv_b"], name="conv_fwd")
    f = _matmul(act, w_down, name="ffn_down")
    dy, sq = _loss_fwd(x1, f, target, w["g_post_ffn"], name="loss")
    loss = 0.5 * jnp.sum(sq) / D_MODEL

    grads = {}
    df, grads["g_post_ffn"] = _rms_bwd(f, w["g_post_ffn"], dy, name="rms_post_ffn_bwd", out_dtype=BF16)
    da = _matmul(df, w_down.T, name="ffn_down_dx")
    grads["w_ffn_down"] = _matmul(act, df, name="ffn_down_dw", trans_a=True)
    du_g, du_v, cacc_g, cacc_v = _conv_bwd(da, zg, zv, u, w["conv_w"], name="conv_bwd")
    grads["conv_w"] = jnp.concatenate([cacc_g[0:3], cacc_v[0:3]], axis=1)
    grads["conv_b"] = jnp.concatenate([cacc_g[3:4], cacc_v[3:4]], axis=1)
    w_up_t = w_up.T
    dh2 = _matmul(du_g, w_up_t[:D_FF], name="ffn_up_dx_gate")
    dh2 = _matmul(du_v, w_up_t[D_FF:], name="ffn_up_dx_val", add=dh2)
    grads["w_ffn_up"] = jnp.concatenate([_matmul(h2, du_g, name="ffn_up_dw_gate", trans_a=True),
                                         _matmul(h2, du_v, name="ffn_up_dw_val", trans_a=True)], axis=1)
    tok = hooks.early_grads(("w_ffn_down", "w_ffn_up", "conv_w"), grads)
    dx1, grads["g_pre_ffn"] = _rms_bwd(x1, _after(w["g_pre_ffn"], tok), dh2, name="rms_pre_ffn_bwd",
                                       out_dtype=F32, add=dy)
    dz1, grads["g_post_mix"] = _rms_bwd(z1, w["g_post_mix"], dx1, name="rms_post_mix_bwd", out_dtype=BF16)
    dmerged = _matmul(dz1, w_o.T, name="out_proj_dx")
    grads["w_o"] = _matmul(merged, dz1, name="out_proj_dw", trans_a=True)
    db_mla, db_dil, db_mem, dgp, grads["b_gate"] = _gate_bwd(
        dmerged, p_g, w["b_gate"], (b_mla, b_dil, b_mem), name="gate_bwd")

    do_mla = _matmul(db_mla, wb_mla.T, name="br_mla_dx", out_dtype=BF16)
    g_wb_mla = _matmul(o_mla, db_mla, name="br_mla_dw", trans_a=True)
    grads["w_br_mla"] = g_wb_mla.reshape(MLA_HEADS, LANES, D_MODEL)[:, :MLA_V].reshape(MLA_HEADS * MLA_V, D_MODEL)
    delta_mla = _row_dot(do_mla, o_mla, name="mla_delta")
    dq_f, dk_f, dv_f = _causal_bwd(q_f, k_f, v_f, do_mla, lse_mla, delta_mla, name="mla_bwd", heads=MLA_HEADS)
    dq_raw = _rope_bwd(dq_f, tabs, name="rope_q_bwd", scale=mla_scale, with_add=False)
    dk_raw, dkr = _rope_bwd(dk_f, tabs, name="rope_k_bwd", scale=1.0, with_add=True)
    dqn = _matmul(dq_raw, wq.T, name="mla_q_up_dx")
    g_wq = _matmul(qn, dq_raw, name="mla_q_up_dw", trans_a=True)
    grads["w_uq"] = g_wq.reshape(MLA_Q_RANK, MLA_HEADS, LANES)[:, :, :MLA_QK_DIM].reshape(MLA_Q_RANK, -1)
    dkvn = _matmul(dk_raw, wk.T, name="mla_k_up_dx")
    dkvn = _matmul(dv_f, wv.T, name="mla_v_up_dx", add=dkvn)
    g_wk = _matmul(kvn, dk_raw, name="mla_k_up_dw", trans_a=True).reshape(MLA_KV_RANK, MLA_HEADS, LANES)
    g_wv = _matmul(kvn, dv_f, name="mla_v_up_dw", trans_a=True).reshape(MLA_KV_RANK, MLA_HEADS, LANES)
    grads["w_ukv"] = jnp.concatenate([g_wk[:, :, :MLA_NOPE], g_wv[:, :, :MLA_V]], axis=2).reshape(MLA_KV_RANK, -1)
    dc_q, grads["mla_q_norm"] = _rms_bwd(c_q, w["mla_q_norm"], dqn, name="rms_q_bwd", out_dtype=BF16)
    dc_kv, grads["mla_kv_norm"] = _rms_bwd(c_kv, w["mla_kv_norm"], dkvn, name="rms_kv_bwd", out_dtype=BF16)

    dy_dil = _matmul(db_dil, wb_dil.T, name="br_dil_dx")
    grads["w_br_dil"] = _matmul(y_dil, db_dil, name="br_dil_dw", trans_a=True)
    mix = _mix_bwd(dy_dil, dil_o, dil_lse, name="dil_mix_bwd")
    d_dil = [[None] * 3 for _ in range(3)]
    for g, (_, d) in enumerate(DIL_PAIRS):
        arrs, offs = dil_in[g]
        do_g, dl_g, lse_g = mix[g], mix[3 + g], dil_lse[g]
        if d != 1:
            do_g, dl_g, lse_g = _to_residues(do_g, d), _to_residues(dl_g, d), _to_residues(lse_g, d)
        dq_g, dk_g, dv_g = _band_bwd(*arrs, do_g, lse_g, dl_g, slopes[g], name=f"dil_bwd_{g}", dilation=d,
                                     qoff=offs[0], koff=offs[1], voff=offs[2])
        for t, a in enumerate((dq_g, dk_g, dv_g)):
            d_dil[t][g] = a if d == 1 else _from_residues(a, d)

    do_mem = _matmul(db_mem, wb_mem.T, name="br_mem_dx", out_dtype=BF16)
    grads["w_br_mem"] = _matmul(o_mem, db_mem, name="br_mem_dw", trans_a=True)
    delta_mem = _row_dot(do_mem, o_mem, name="mem_delta")
    dq_mem = _flash_bwd_dq(p_dm, kv_m, kv_m, do_mem, lse_mem, delta_mem, name="mem_bwd_dq", out_dtype=BF16, **memat)
    dk_mem, dv_mem = _flash_bwd_dkv(p_dm, kv_m, kv_m, do_mem, lse_mem, delta_mem, name="mem_bwd_dkv",
                                    dk_dtype=BF16, dv_dtype=BF16, **memat)
    dkv_m = jnp.concatenate([dk_mem, dv_mem], axis=1)
    dmemn = _matmul(dkv_m, w_mkv.T, name="mem_kv_dx")
    grads["w_mem_kv"] = _matmul(memn, dkv_m, name="mem_kv_dw", trans_a=True)
    _, grads["g_mem"] = _rms_bwd(mem, w["g_mem"], dmemn, name="rms_mem_bwd", out_dtype=BF16)

    tok = hooks.early_grads(("w_o", "w_br_mla", "w_br_dil", "w_br_mem", "w_uq", "w_ukv", "w_mem_kv"), grads)
    if tok is not None:
        dkr = dkr + tok[0:1, 0:1]
    dp_all = jnp.concatenate([dc_q, dc_kv, bf(dkr)] + d_dil[0] + d_dil[1] + d_dil[2] + [dq_mem, dgp], axis=1)
    g_in = _matmul(h, dp_all, name="proj_dw", trans_a=True)
    grads["w_in"] = jnp.concatenate(
        [g_in[:, :OFF_KV], g_in[:, OFF_KV + MLA_NOPE:OFF_KV + MLA_QK_DIM], g_in[:, N_A:]], axis=1)
    tok = hooks.early_grads(("w_in",), grads)
    dh = _matmul(dp_all, w_in_t, name="proj_dx", after=tok)
    dx, grads["g_pre_mix"] = _rms_bwd(x, w["g_pre_mix"], dh, name="rms_pre_mix_bwd", out_dtype=F32, add=dx1)
    return loss, dx, grads


def kernel(x, mem, positions, g_pre_mix, w_in, b_gate, mla_q_norm, w_uq, mla_kv_norm, w_ukv, g_mem, w_mem_kv, w_br_mla, w_br_dil, w_br_mem, w_o, g_post_mix, g_pre_ffn, w_ffn_up, conv_w, conv_b, w_ffn_down, g_post_ffn, loss_target, m_g_pre_mix, m_w_in, m_b_gate, m_mla_q_norm, m_w_uq, m_mla_kv_norm, m_w_ukv, m_g_mem, m_w_mem_kv, m_w_br_mla, m_w_br_dil, m_w_br_mem, m_w_o, m_g_post_mix, m_g_pre_ffn, m_w_ffn_up, m_conv_w, m_conv_b, m_w_ffn_down, m_g_post_ffn, v_g_pre_mix, v_w_in, v_b_gate, v_mla_q_norm, v_w_uq, v_mla_kv_norm, v_w_ukv, v_g_mem, v_w_mem_kv, v_w_br_mla, v_w_br_dil, v_w_br_mem, v_w_o, v_g_post_mix, v_g_pre_ffn, v_w_ffn_up, v_conv_w, v_conv_b, v_w_ffn_down, v_g_post_ffn):
    local = dict(g_pre_mix=g_pre_mix, w_in=w_in, b_gate=b_gate, mla_q_norm=mla_q_norm, w_uq=w_uq,
                 mla_kv_norm=mla_kv_norm, w_ukv=w_ukv, g_mem=g_mem, w_mem_kv=w_mem_kv, w_br_mla=w_br_mla,
                 w_br_dil=w_br_dil, w_br_mem=w_br_mem, w_o=w_o, g_post_mix=g_post_mix, g_pre_ffn=g_pre_ffn,
                 w_ffn_up=w_ffn_up, conv_w=conv_w, conv_b=conv_b, w_ffn_down=w_ffn_down, g_post_ffn=g_post_ffn)
    mom_m = dict(g_pre_mix=m_g_pre_mix, w_in=m_w_in, b_gate=m_b_gate, mla_q_norm=m_mla_q_norm, w_uq=m_w_uq,
                 mla_kv_norm=m_mla_kv_norm, w_ukv=m_w_ukv, g_mem=m_g_mem, w_mem_kv=m_w_mem_kv, w_br_mla=m_w_br_mla,
                 w_br_dil=m_w_br_dil, w_br_mem=m_w_br_mem, w_o=m_w_o, g_post_mix=m_g_post_mix,
                 g_pre_ffn=m_g_pre_ffn, w_ffn_up=m_w_ffn_up, conv_w=m_conv_w, conv_b=m_conv_b,
                 w_ffn_down=m_w_ffn_down, g_post_ffn=m_g_post_ffn)
    mom_v = dict(g_pre_mix=v_g_pre_mix, w_in=v_w_in, b_gate=v_b_gate, mla_q_norm=v_mla_q_norm, w_uq=v_w_uq,
                 mla_kv_norm=v_mla_kv_norm, w_ukv=v_w_ukv, g_mem=v_g_mem, w_mem_kv=v_w_mem_kv, w_br_mla=v_w_br_mla,
                 w_br_dil=v_w_br_dil, w_br_mem=v_w_br_mem, w_o=v_w_o, g_post_mix=v_g_post_mix,
                 g_pre_ffn=v_g_pre_ffn, w_ffn_up=v_w_ffn_up, conv_w=v_conv_w, conv_b=v_conv_b,
                 w_ffn_down=v_w_ffn_down, g_post_ffn=v_g_post_ffn)

    me = 4 * lax.axis_index("x") + 2 * lax.axis_index("y") + lax.axis_index("c")
    spec = {name: (shape, axis) for name, shape, axis in SHARDED}
    wire = lambda name: F32 if name == "conv_w" else BF16
    shard = {name: local[name][0].astype(wire(name)) for name in spec}
    slab = lambda name, grads: _split_shards(grads[name].astype(wire(name)), *spec[name])

    w_in_all = _exchange([shard["w_in"]], name="gather_w_in", gather=True)[0]
    late = tuple(name for name in spec if name != "w_in")
    late_handle, late_token = _exchange_start([shard[n] for n in late], w_in_all, name="gather_rest_start",
                                              gather=True)
    full = {"w_in": _join_shards(w_in_all, *spec["w_in"])}
    for name, _, _ in REPLICATED:
        full[name] = local[name].reshape(1, -1)

    pending = []

    class Overlap:
        start_token = late_token

        def late_weights(self, w, after):
            landed = _exchange_wait(late_handle, after, name="gather_rest_wait", gather=True)
            w = dict(w)
            for name, buf in zip(late, landed):
                w[name] = _join_shards(_with_own(buf, shard[name], me), *spec[name])
            return w

        def early_grads(self, names, grads):
            slabs = [slab(name, grads) for name in names]
            handle, token = _exchange_start(slabs, slabs[0], name="grads_start_" + names[0], gather=False)
            pending.append((names, slabs, handle))
            return token

    loss, dx, grads = _local_step(x[0], mem[0], positions[0], loss_target[0], full, Overlap())

    parts = {}
    for names, slabs, handle in pending:
        landed = _exchange_wait(handle, dx, name="grads_wait_" + names[0], gather=False)
        for name, own, buf in zip(names, slabs, landed):
            parts[name] = _with_own(buf, lax.dynamic_index_in_dim(own, me, 0, keepdims=False), me)
    rep_parts = _exchange([_pack_replicated(grads)], name="gather_replicated_grads", gather=True)[0]

    results = {}
    for name in spec:
        res = _adamw(parts[name], local[name][0], mom_m[name][0], mom_v[name][0], name="adamw_" + name)
        results[name] = [r[None] for r in res]
    rep = _adamw(rep_parts, _pack_replicated(local), _pack_replicated(mom_m), _pack_replicated(mom_v),
                 name="adamw_replicated")
    for i, buf in enumerate(rep):
        for name, val in _unpack_replicated(buf).items():
            results.setdefault(name, [None] * 4)[i] = val

    loss = lax.psum(loss, ("x", "y", "c"))
    outs = [loss, dx[None]]
    for i in range(4):
        outs.extend(results[name][i] for name in PARAM_NAMES)
    return tuple(outs)
```

```python
import functools

import numpy as np
import jax
import jax.numpy as jnp
from jax import lax
from jax.experimental import pallas as pl
from jax.experimental.pallas import tpu as pltpu

F32 = jnp.float32
BF16 = jnp.bfloat16

N_DEV = 8
D_MODEL = 1024
RMS_EPS = 1e-6
NEG_INF = -1e30
LANES = 128
SUBLANES = 8
BLOCK = 128

MLA_HEADS = 8
MLA_NOPE = 64
MLA_ROPE = 32
MLA_V = 64
MLA_QK_DIM = MLA_NOPE + MLA_ROPE
MLA_Q_RANK = 384
MLA_KV_RANK = 256
ROPE_THETA = 10000.0
DIL_PAIRS = ((128, 1), (512, 4), (2048, 16))
DIL_HEADS_PER_GROUP = 4
DIL_HEADS = 12
MEM_HEADS = 4
D_FF = 2816
OFF_Q = MLA_Q_RANK
OFF_KV = OFF_Q + MLA_KV_RANK
OFF_KR = OFF_KV + MLA_ROPE
OFF_DIL = OFF_KR + 3 * DIL_HEADS * LANES
OFF_MEMQ = OFF_DIL + MEM_HEADS * LANES
D_IN = OFF_MEMQ + 3 * D_MODEL
N_A = OFF_KV + LANES
N_DM = 3 * DIL_HEADS * LANES + MEM_HEADS * LANES

ADAM_LR = 0.001
ADAM_B1 = 0.9
ADAM_B2 = 0.999
ADAM_EPS = 1e-08
ADAM_WD = 0.01
ADAM_STEP = 10

VMEM_LIMIT = 48 * 1024 * 1024
VMEM_LIMIT_BIG = 58 * 1024 * 1024
PACK_W = 1024

_ALIBI_BASE = np.exp2(-8.0 * np.arange(1, DIL_HEADS + 1) / DIL_HEADS)
DIL_SLOPES = [[float(_ALIBI_BASE[hh * 3 + g]) for hh in range(DIL_HEADS_PER_GROUP)] for g in range(3)]

PARAMS = (
    ("g_pre_mix", (1024,), None), ("w_in", (1024, D_IN), 1), ("b_gate", (3072,), None),
    ("mla_q_norm", (384,), None), ("w_uq", (384, 768), 1), ("mla_kv_norm", (256,), None),
    ("w_ukv", (256, 1024), 1), ("g_mem", (1024,), None), ("w_mem_kv", (1024, 1024), 0),
    ("w_br_mla", (512, 1024), 1), ("w_br_dil", (512, 1024), 1), ("w_br_mem", (512, 1024), 1),
    ("w_o", (1024, 1024), 0), ("g_post_mix", (1024,), None), ("g_pre_ffn", (1024,), None),
    ("w_ffn_up", (1024, 2 * D_FF), 1), ("conv_w", (3, 2 * D_FF), 1), ("conv_b", (2 * D_FF,), None),
    ("w_ffn_down", (D_FF, 1024), 0), ("g_post_ffn", (1024,), None),
)
PARAM_NAMES = tuple(p[0] for p in PARAMS)


def _shard_shape(shape, axis):
    if axis is None:
        return shape
    return tuple(s // N_DEV if a == axis else s for a, s in enumerate(shape))


SHARDED = tuple(p for p in PARAMS if p[2] is not None)
REPLICATED = tuple(p for p in PARAMS if p[2] is None)


def _layout():
    off, table = 0, {}
    for name, shape, _ in REPLICATED:
        table[name] = (off, shape[0])
        off += shape[0]
    rows = -(-off // PACK_W)
    rows = -(-rows // SUBLANES) * SUBLANES
    return table, off, rows


PACK_TABLE, PACK_USED, PACK_ROWS = _layout()


def _pick(n, cap):
    best = None
    for t in range(LANES, min(n, cap) + 1, LANES):
        if n % t == 0:
            best = t
    return best if best is not None else n


def _rows(n, cap, mult=SUBLANES):
    best = None
    for t in range(mult, min(n, cap) + 1, mult):
        if n % t == 0:
            best = t
    return best if best is not None else n


def _cparams(sem, vmem=VMEM_LIMIT):
    return pltpu.CompilerParams(dimension_semantics=sem, vmem_limit_bytes=vmem)


def _matmul(a, b, *, name, out_dtype=F32, trans_a=False, add=None, after=None, tm=1024, tn=1408, tk=640):
    if trans_a:
        kc, m = a.shape
    else:
        m, kc = a.shape
    n = b.shape[1]
    assert b.shape[0] == kc
    tm, tn, tk = _pick(m, tm), _pick(n, tn), _pick(kc, tk)
    nk = kc // tk

    def body(*refs):
        a_ref, b_ref = refs[:2]
        c_ref = refs[2] if add is not None else None
        o_ref, acc = refs[-2:]
        k = pl.program_id(2)

        @pl.when(k == 0)
        def _():
            if add is None:
                acc[...] = jnp.zeros_like(acc)
            else:
                acc[...] = c_ref[...].astype(F32)

        av = a_ref[...].astype(BF16)
        bv = b_ref[...].astype(BF16)
        if trans_a:
            acc[...] += lax.dot_general(av, bv, (((0,), (0,)), ((), ())), preferred_element_type=F32)
        else:
            acc[...] += jnp.dot(av, bv, preferred_element_type=F32)

        @pl.when(k == nk - 1)
        def _():
            o_ref[...] = acc[...].astype(out_dtype)

    if trans_a:
        a_spec = pl.BlockSpec((tk, tm), lambda i, j, k: (k, i))
    else:
        a_spec = pl.BlockSpec((tm, tk), lambda i, j, k: (i, k))
    in_specs = [a_spec, pl.BlockSpec((tk, tn), lambda i, j, k: (k, j))]
    args = [a, b]
    if add is not None:
        in_specs.append(pl.BlockSpec((tm, tn), lambda i, j, k: (i, j)))
        args.append(add)
    if after is not None:
        in_specs.append(pl.BlockSpec(memory_space=pl.ANY))
        args.append(after)
    return pl.pallas_call(
        body, name=name, grid=(m // tm, n // tn, nk),
        in_specs=in_specs, out_specs=pl.BlockSpec((tm, tn), lambda i, j, k: (i, j)),
        out_shape=jax.ShapeDtypeStruct((m, n), out_dtype),
        scratch_shapes=[pltpu.VMEM((tm, tn), F32)],
        compiler_params=_cparams(("parallel", "parallel", "arbitrary")),
    )(*args)


def _rms_fwd(x, g, *, name, out_dtype, add=None):
    s, n = x.shape
    ts = _rows(s, 512)

    def body(*refs):
        if add is None:
            x_ref, g_ref, o_ref = refs
        else:
            x_ref, g_ref, a_ref, o_ref = refs
        xv = x_ref[...]
        r = lax.rsqrt(jnp.mean(xv * xv, axis=-1, keepdims=True) + RMS_EPS)
        y = xv * r * g_ref[...]
        if add is not None:
            y = a_ref[...] + y
        o_ref[...] = y.astype(out_dtype)

    row = pl.BlockSpec((ts, n), lambda i: (i, 0))
    in_specs = [row, pl.BlockSpec((1, n), lambda i: (0, 0))]
    args = [x, g]
    if add is not None:
        in_specs.append(row)
        args.append(add)
    return pl.pallas_call(
        body, name=name, grid=(s // ts,), in_specs=in_specs, out_specs=row,
        out_shape=jax.ShapeDtypeStruct((s, n), out_dtype),
        compiler_params=_cparams(("parallel",)),
    )(*args)


def _rms_bwd(x, g, dy, *, name, out_dtype, add=None):
    s, n = x.shape
    ts = _rows(s, 512)

    def body(*refs):
        if add is None:
            x_ref, g_ref, dy_ref, dx_ref, dg_ref = refs
        else:
            x_ref, g_ref, dy_ref, a_ref, dx_ref, dg_ref = refs
        i = pl.program_id(0)
        xv = x_ref[...]
        dyv = dy_ref[...].astype(F32)
        r = lax.rsqrt(jnp.mean(xv * xv, axis=-1, keepdims=True) + RMS_EPS)
        nx = xv * r
        gdy = dyv * g_ref[...]
        dx = r * (gdy - nx * jnp.mean(nx * gdy, axis=-1, keepdims=True))
        if add is not None:
            dx = a_ref[...] + dx
        dx_ref[...] = dx.astype(out_dtype)

        @pl.when(i == 0)
        def _():
            dg_ref[...] = jnp.zeros_like(dg_ref)

        dg_ref[...] += jnp.sum(dyv * nx, axis=0, keepdims=True)

    row = pl.BlockSpec((ts, n), lambda i: (i, 0))
    vec = pl.BlockSpec((1, n), lambda i: (0, 0))
    in_specs = [row, vec, row]
    args = [x, g, dy]
    if add is not None:
        in_specs.append(row)
        args.append(add)
    return pl.pallas_call(
        body, name=name, grid=(s // ts,), in_specs=in_specs, out_specs=[row, vec],
        out_shape=[jax.ShapeDtypeStruct((s, n), out_dtype), jax.ShapeDtypeStruct((1, n), F32)],
        compiler_params=_cparams(("arbitrary",)),
    )(*args)


def _loss_step(x1, f, target, g, *, name):
    s, n = x1.shape
    ts = _rows(s, 512)

    def body(x_ref, f_ref, t_ref, g_ref, dy_ref, df_ref, dg_ref, sq_ref):
        i = pl.program_id(0)
        fv = f_ref[...]
        gv = g_ref[...]
        r = lax.rsqrt(jnp.mean(fv * fv, axis=-1, keepdims=True) + RMS_EPS)
        nx = fv * r
        err = x_ref[...] + nx * gv - t_ref[...]
        dy = err * (1.0 / n)
        dy_ref[...] = dy
        gdy = dy * gv
        df_ref[...] = (r * (gdy - nx * jnp.mean(nx * gdy, axis=-1, keepdims=True))).astype(BF16)

        @pl.when(i == 0)
        def _():
            sq_ref[...] = jnp.zeros_like(sq_ref)
            dg_ref[...] = jnp.zeros_like(dg_ref)

        sq_ref[...] += jnp.sum(err * err, axis=0, keepdims=True)
        dg_ref[...] += jnp.sum(dy * nx, axis=0, keepdims=True)

    row = pl.BlockSpec((ts, n), lambda i: (i, 0))
    vec = pl.BlockSpec((1, n), lambda i: (0, 0))
    return pl.pallas_call(
        body, name=name, grid=(s // ts,), in_specs=[row, row, row, vec], out_specs=[row, row, vec, vec],
        out_shape=[jax.ShapeDtypeStruct((s, n), F32), jax.ShapeDtypeStruct((s, n), BF16),
                   jax.ShapeDtypeStruct((1, n), F32), jax.ShapeDtypeStruct((1, n), F32)],
        compiler_params=_cparams(("arbitrary",)),
    )(x1, f, target, g)


def _rope_tables(positions):
    half = MLA_ROPE // 2
    inv_freq = ROPE_THETA ** (-jnp.arange(half, dtype=F32) / half)
    ang = positions.astype(F32)[:, None] * inv_freq[None, :]
    cos, sin = jnp.cos(ang), jnp.sin(ang)
    s = positions.shape[0]
    one = jnp.ones((s, MLA_NOPE), F32)
    zero = jnp.zeros((s, MLA_NOPE), F32)
    pad1 = jnp.ones((s, LANES - MLA_QK_DIM), F32)
    pad0 = jnp.zeros((s, LANES - MLA_QK_DIM), F32)
    zh = jnp.zeros((s, half), F32)
    c_tab = jnp.concatenate([one, cos, cos, pad1], axis=1)
    s1_tab = jnp.concatenate([zero, -sin, zh, pad0], axis=1)
    s2_tab = jnp.concatenate([zero, zh, sin, pad0], axis=1)
    return c_tab, s1_tab, s2_tab


def _rope_fwd(x, tabs, *, name, scale, add=None):
    s, n = x.shape
    nh = n // LANES
    ts = _rows(s, 512)
    half = MLA_ROPE // 2

    def body(*refs):
        if add is None:
            x_ref, c_ref, s1_ref, s2_ref, o_ref = refs
        else:
            x_ref, a_ref, c_ref, s1_ref, s2_ref, o_ref = refs
        c, s1, s2 = c_ref[...], s1_ref[...], s2_ref[...]
        for h in range(nh):
            xh = x_ref[:, h * LANES:(h + 1) * LANES]
            if add is not None:
                xh = xh + a_ref[...]
            y = xh * c + pltpu.roll(xh, LANES - half, 1) * s1 + pltpu.roll(xh, half, 1) * s2
            o_ref[:, h * LANES:(h + 1) * LANES] = (y * scale).astype(BF16)

    row = pl.BlockSpec((ts, n), lambda i: (i, 0))
    tab = pl.BlockSpec((ts, LANES), lambda i: (i, 0))
    in_specs = [row] + ([tab] if add is not None else []) + [tab, tab, tab]
    args = [x] + ([add] if add is not None else []) + list(tabs)
    return pl.pallas_call(
        body, name=name, grid=(s // ts,), in_specs=in_specs, out_specs=row,
        out_shape=jax.ShapeDtypeStruct((s, n), BF16),
        compiler_params=_cparams(("parallel",)),
    )(*args)


def _rope_bwd(dy, tabs, *, name, scale, with_add):
    s, n = dy.shape
    nh = n // LANES
    ts = _rows(s, 512)
    half = MLA_ROPE // 2

    def body(*refs):
        if with_add:
            dy_ref, c_ref, s1_ref, s2_ref, dx_ref, da_ref = refs
        else:
            dy_ref, c_ref, s1_ref, s2_ref, dx_ref = refs
        c, s1, s2 = c_ref[...], s1_ref[...], s2_ref[...]
        tot = None
        for h in range(nh):
            g = dy_ref[:, h * LANES:(h + 1) * LANES].astype(F32)
            dx = (g * c + pltpu.roll(g * s1, half, 1) + pltpu.roll(g * s2, LANES - half, 1)) * scale
            dx_ref[:, h * LANES:(h + 1) * LANES] = dx.astype(BF16)
            tot = dx if tot is None else tot + dx
        if with_add:
            da_ref[...] = tot

    row = pl.BlockSpec((ts, n), lambda i: (i, 0))
    tab = pl.BlockSpec((ts, LANES), lambda i: (i, 0))
    out_specs = [row, tab] if with_add else row
    out_shape = [jax.ShapeDtypeStruct((s, n), BF16)]
    if with_add:
        out_shape.append(jax.ShapeDtypeStruct((s, LANES), F32))
    else:
        out_shape = out_shape[0]
    return pl.pallas_call(
        body, name=name, grid=(s // ts,), in_specs=[row, tab, tab, tab], out_specs=out_specs,
        out_shape=out_shape, compiler_params=_cparams(("parallel",)),
    )(dy, *tabs)


def _scores(q, k, scale, diag):
    s = lax.dot_general(q, k, (((1,), (1,)), ((), ())), preferred_element_type=F32)
    if scale != 1.0:
        s = s * scale
    if diag:
        rows = lax.broadcasted_iota(jnp.int32, s.shape, 0)
        cols = lax.broadcasted_iota(jnp.int32, s.shape, 1)
        s = jnp.where(cols <= rows, s, NEG_INF)
    return s


def _flash_fwd(q, k, v, *, name, heads, qoff, koff, voff, causal, scale, tq, tk):
    s_q, s_kv = q.shape[0], k.shape[0]
    tq, tk = min(tq, s_q), min(tk, s_kv)
    nq, nk = s_q // tq, s_kv // tk
    if causal:
        assert tq == tk and s_q == s_kv

    def body(q_ref, k_ref, v_ref, o_ref, lse_ref, m_s, l_s, acc):
        i, j = pl.program_id(1), pl.program_id(2)

        @pl.when(j == 0)
        def _():
            m_s[...] = jnp.full_like(m_s, NEG_INF)
            l_s[...] = jnp.zeros_like(l_s)
            acc[...] = jnp.zeros_like(acc)

        def step(diag):
            s = _scores(q_ref[...], k_ref[...], scale, diag)
            m_prev = m_s[...]
            m_cur = jnp.maximum(m_prev, jnp.max(s, axis=1, keepdims=True))
            alpha = jnp.exp(m_prev - m_cur)
            p = jnp.exp(s - m_cur[:, :1])
            l_s[...] = alpha * l_s[...] + jnp.sum(p, axis=1, keepdims=True)
            acc[...] = alpha * acc[...] + jnp.dot(p.astype(BF16), v_ref[...], preferred_element_type=F32)
            m_s[...] = m_cur

        def finish():
            o_ref[...] = (acc[...] / l_s[...]).astype(o_ref.dtype)
            lse_ref[...] = m_s[...] + jnp.log(l_s[...])

        if causal:
            @pl.when(j < i)
            def _():
                step(False)

            @pl.when(j == i)
            def _():
                step(True)
                finish()
        else:
            step(False)

            @pl.when(j == nk - 1)
            def _():
                finish()

    def kv_idx(off):
        if causal:
            return lambda h, i, j: (jnp.minimum(j, i), off + h)
        return lambda h, i, j: (j, off + h)

    blk_q = pl.BlockSpec((tq, LANES), lambda h, i, j: (i, qoff + h))
    out_q = pl.BlockSpec((tq, LANES), lambda h, i, j: (i, h))
    return pl.pallas_call(
        body, name=name, grid=(heads, nq, nk),
        in_specs=[blk_q, pl.BlockSpec((tk, LANES), kv_idx(koff)), pl.BlockSpec((tk, LANES), kv_idx(voff))],
        out_specs=[out_q, out_q],
        out_shape=[jax.ShapeDtypeStruct((s_q, heads * LANES), BF16),
                   jax.ShapeDtypeStruct((s_q, heads * LANES), F32)],
        scratch_shapes=[pltpu.VMEM((tq, LANES), F32)] * 3,
        compiler_params=_cparams(("parallel", "parallel", "arbitrary")),
    )(q, k, v)


def _flash_bwd_dq(q, k, v, do, lse, delta, *, name, heads, qoff, koff, voff, causal, scale, tq, tk, out_dtype):
    s_q, s_kv = q.shape[0], k.shape[0]
    tq, tk = min(tq, s_q), min(tk, s_kv)
    nq, nk = s_q // tq, s_kv // tk

    def body(q_ref, k_ref, v_ref, do_ref, lse_ref, dl_ref, dq_ref, acc):
        i, j = pl.program_id(1), pl.program_id(2)

        @pl.when(j == 0)
        def _():
            acc[...] = jnp.zeros_like(acc)

        def step(diag):
            s = _scores(q_ref[...], k_ref[...], scale, diag)
            p = jnp.exp(s - lse_ref[:, :1])
            dp = lax.dot_general(do_ref[...], v_ref[...], (((1,), (1,)), ((), ())), preferred_element_type=F32)
            ds = p * (dp - dl_ref[:, :1])
            acc[...] += jnp.dot(ds.astype(BF16), k_ref[...], preferred_element_type=F32)

        def finish():
            dq_ref[...] = (acc[...] * scale).astype(out_dtype)

        if causal:
            @pl.when(j < i)
            def _():
                step(False)

            @pl.when(j == i)
            def _():
                step(True)
                finish()
        else:
            step(False)

            @pl.when(j == nk - 1)
            def _():
                finish()

    def kv_idx(off):
        if causal:
            return lambda h, i, j: (jnp.minimum(j, i), off + h)
        return lambda h, i, j: (j, off + h)

    blk_q = pl.BlockSpec((tq, LANES), lambda h, i, j: (i, qoff + h))
    blk_h = pl.BlockSpec((tq, LANES), lambda h, i, j: (i, h))
    return pl.pallas_call(
        body, name=name, grid=(heads, nq, nk),
        in_specs=[blk_q, pl.BlockSpec((tk, LANES), kv_idx(koff)), pl.BlockSpec((tk, LANES), kv_idx(voff)),
                  blk_h, blk_h, blk_h],
        out_specs=blk_h,
        out_shape=jax.ShapeDtypeStruct((s_q, heads * LANES), out_dtype),
        scratch_shapes=[pltpu.VMEM((tq, LANES), F32)],
        compiler_params=_cparams(("parallel", "parallel", "arbitrary")),
    )(q, k, v, do, lse, delta)


def _flash_bwd_dkv(q, k, v, do, lse, delta, *, name, heads, qoff, koff, voff, causal, scale, tq, tk,
                   dk_dtype, dv_dtype):
    s_q, s_kv = q.shape[0], k.shape[0]
    tq, tk = min(tq, s_q), min(tk, s_kv)
    nq, nk = s_q // tq, s_kv // tk

    def body(q_ref, k_ref, v_ref, do_ref, lse_ref, dl_ref, dk_ref, dv_ref, dk_acc, dv_acc):
        j, i = pl.program_id(1), pl.program_id(2)

        @pl.when(i == 0)
        def _():
            dk_acc[...] = jnp.zeros_like(dk_acc)
            dv_acc[...] = jnp.zeros_like(dv_acc)

        def step(diag):
            s = _scores(q_ref[...], k_ref[...], scale, diag)
            p = jnp.exp(s - lse_ref[:, :1])
            dov = do_ref[...]
            dp = lax.dot_general(dov, v_ref[...], (((1,), (1,)), ((), ())), preferred_element_type=F32)
            ds = p * (dp - dl_ref[:, :1])
            dv_acc[...] += lax.dot_general(p.astype(BF16), dov, (((0,), (0,)), ((), ())),
                                           preferred_element_type=F32)
            dk_acc[...] += lax.dot_general(ds.astype(BF16), q_ref[...], (((0,), (0,)), ((), ())),
                                           preferred_element_type=F32)

        if causal:
            @pl.when(i > j)
            def _():
                step(False)

            @pl.when(i == j)
            def _():
                step(True)
        else:
            step(False)

        @pl.when(i == nq - 1)
        def _():
            dk_ref[...] = (dk_acc[...] * scale).astype(dk_dtype)
            dv_ref[...] = dv_acc[...].astype(dv_dtype)

    def q_idx(off):
        if causal:
            return lambda h, j, i: (jnp.maximum(i, j), off + h)
        return lambda h, j, i: (i, off + h)

    blk_h = pl.BlockSpec((tq, LANES), q_idx(0))
    out_k = pl.BlockSpec((tk, LANES), lambda h, j, i: (j, h))
    return pl.pallas_call(
        body, name=name, grid=(heads, nk, nq),
        in_specs=[pl.BlockSpec((tq, LANES), q_idx(qoff)),
                  pl.BlockSpec((tk, LANES), lambda h, j, i: (j, koff + h)),
                  pl.BlockSpec((tk, LANES), lambda h, j, i: (j, voff + h)),
                  blk_h, blk_h, blk_h],
        out_specs=[out_k, out_k],
        out_shape=[jax.ShapeDtypeStruct((s_kv, heads * LANES), dk_dtype),
                   jax.ShapeDtypeStruct((s_kv, heads * LANES), dv_dtype)],
        scratch_shapes=[pltpu.VMEM((tk, LANES), F32)] * 2,
        compiler_params=_cparams(("parallel", "parallel", "arbitrary")),
    )(q, k, v, do, lse, delta)


def _row_dot(a, b, *, name):
    s, n = a.shape
    nh = n // LANES
    ts = _rows(s, 512)

    def body(a_ref, b_ref, o_ref):
        for h in range(nh):
            sl = slice(h * LANES, (h + 1) * LANES)
            d = jnp.sum(a_ref[:, sl].astype(F32) * b_ref[:, sl].astype(F32), axis=1, keepdims=True)
            o_ref[:, sl] = jnp.broadcast_to(d, (ts, LANES))

    row = pl.BlockSpec((ts, n), lambda i: (i, 0))
    return pl.pallas_call(
        body, name=name, grid=(s // ts,), in_specs=[row, row], out_specs=row,
        out_shape=jax.ShapeDtypeStruct((s, n), F32), compiler_params=_cparams(("parallel",)),
    )(a, b)


CAUSAL_T = 512
LOG2E = 1.4426950408889634
LN2 = 0.6931471805599453


def _causal_fwd(q, k, v, *, name, heads, ones_lane):
    s = q.shape[0]
    t = CAUSAL_T
    nq = s // (2 * t)
    assert nq * 2 * t == s

    def body(q_ref, k_ref, v_ref, o_ref, lse_ref, v1, m_s, acc):
        i = pl.program_id(1)

        @pl.when(i == 0)
        def _():
            lane = lax.broadcasted_iota(jnp.int32, v1.shape, 1)
            v1[...] = jnp.where(lane == ones_lane, 1.0, v_ref[...]).astype(BF16)

        m_s[...] = jnp.full_like(m_s, NEG_INF)
        acc[...] = jnp.zeros_like(acc)
        halves = (q_ref[0:t, :], q_ref[t:2 * t, :])

        def raw(c, j):
            rows = pl.ds(pl.multiple_of(j * t, t), t)
            return lax.dot_general(halves[c], k_ref[rows, :], (((1,), (1,)), ((), ())), preferred_element_type=F32)

        def update(c, sc, j):
            m_prev = m_s[c]
            m_cur = jnp.maximum(m_prev, jnp.max(sc, axis=1, keepdims=True))
            p = jnp.exp2(sc - m_cur[:, :1]).astype(BF16)
            acc[c] = jnp.exp2(m_prev - m_cur) * acc[c] + jnp.dot(
                p, v1[pl.ds(pl.multiple_of(j * t, t), t), :], preferred_element_type=F32)
            m_s[c] = m_cur

        def loop(j, carry):
            sa, sb = raw(0, j), raw(1, j)
            update(0, sa, j)
            update(1, sb, j)
            return carry

        lax.fori_loop(0, 2 * i, loop, 0)
        sa, sb = raw(0, 2 * i), raw(1, 2 * i)
        below = (lax.broadcasted_iota(jnp.int32, sa.shape, 1) <= lax.broadcasted_iota(jnp.int32, sa.shape, 0))
        update(0, jnp.where(below, sa, NEG_INF), 2 * i)
        update(1, sb, 2 * i)
        update(1, jnp.where(below, raw(1, 2 * i + 1), NEG_INF), 2 * i + 1)
        lane = lax.broadcasted_iota(jnp.int32, (t, LANES), 1)
        for c in range(2):
            out = acc[c]
            den = out[:, ones_lane:ones_lane + 1]
            o_ref[c * t:(c + 1) * t, :] = jnp.where(lane == ones_lane, 0.0, out / den).astype(BF16)
            lse_ref[c * t:(c + 1) * t, :] = m_s[c] + jnp.log2(den)

    blk = pl.BlockSpec((2 * t, LANES), lambda h, i: (i, h))
    full = pl.BlockSpec((s, LANES), lambda h, i: (0, h))
    return pl.pallas_call(
        body, name=name, grid=(heads, nq), in_specs=[blk, full, full], out_specs=[blk, blk],
        out_shape=[jax.ShapeDtypeStruct((s, heads * LANES), BF16), jax.ShapeDtypeStruct((s, heads * LANES), F32)],
        scratch_shapes=[pltpu.VMEM((s, LANES), BF16), pltpu.VMEM((2, t, LANES), F32),
                        pltpu.VMEM((2, t, LANES), F32)],
        compiler_params=_cparams(("parallel", "arbitrary")),
    )(q, k, v)


def _causal_bwd(q, k, v, do, lse, delta, *, name, heads):
    s = q.shape[0]
    t = min(CAUSAL_T, s)
    nt = s // t

    def body(q_ref, k_ref, v_ref, do_ref, lse_ref, dl_ref, dq_ref, dk_ref, dv_ref, dk_acc, dv_acc):
        j = pl.program_id(1)

        @pl.when(j == 0)
        def _():
            dq_ref[...] = jnp.zeros_like(dq_ref)

        dk_acc[...] = jnp.zeros_like(dk_acc)
        dv_acc[...] = jnp.zeros_like(dv_acc)
        kv, vv = k_ref[...], v_ref[...]

        def step(i, diag, size=t):
            rows = pl.ds(pl.multiple_of(i * t, t), size)
            qv, dov = q_ref[rows, :], do_ref[rows, :]
            sc = _scores(qv, kv, 1.0, diag)
            p = jnp.exp2(sc - lse_ref[rows, :][:, :1])
            dp = lax.dot_general(dov, vv, (((1,), (1,)), ((), ())), preferred_element_type=F32)
            ds = (p * (dp - dl_ref[rows, :][:, :1])).astype(BF16)
            dv_acc[...] += lax.dot_general(p.astype(BF16), dov, (((0,), (0,)), ((), ())),
                                           preferred_element_type=F32)
            dk_acc[...] += lax.dot_general(ds, qv, (((0,), (0,)), ((), ())), preferred_element_type=F32)
            dq_ref[rows, :] += jnp.dot(ds, kv, preferred_element_type=F32)

        step(j, True)
        odd = (nt - 1 - j) % 2

        @pl.when(odd == 1)
        def _():
            step(j + 1, False)

        def loop(n, carry):
            step(j + 1 + odd + 2 * n, False, 2 * t)
            return carry

        lax.fori_loop(0, (nt - 1 - j) // 2, loop, 0)
        dk_ref[...] = dk_acc[...] * LN2
        dv_ref[...] = dv_acc[...].astype(BF16)

    blk = pl.BlockSpec((t, LANES), lambda h, j: (j, h))
    full = pl.BlockSpec((s, LANES), lambda h, j: (0, h))
    return pl.pallas_call(
        body, name=name, grid=(heads, nt), in_specs=[full, blk, blk, full, full, full],
        out_specs=[full, blk, blk],
        out_shape=[jax.ShapeDtypeStruct((s, heads * LANES), F32), jax.ShapeDtypeStruct((s, heads * LANES), F32),
                   jax.ShapeDtypeStruct((s, heads * LANES), BF16)],
        scratch_shapes=[pltpu.VMEM((t, LANES), F32)] * 2,
        compiler_params=_cparams(("parallel", "arbitrary")),
    )(q, k, v, do, lse, delta)


def _band_masks(dilation, slope):
    qi = lax.broadcasted_iota(jnp.int32, (BLOCK, 2 * BLOCK), 0)
    kj = lax.broadcasted_iota(jnp.int32, (BLOCK, 2 * BLOCK), 1)
    dist = qi + BLOCK - kj
    valid = (dist >= 0) & (dist <= BLOCK)
    bias = -slope * (dist * dilation).astype(F32)
    return valid, bias


BAND_UNROLL = 4


def _aligned(start):
    return start if isinstance(start, int) else pl.multiple_of(start, BLOCK)


def _band_fwd(q, k, v, slopes, *, name, dilation, qoff, koff, voff):
    s = q.shape[0]
    sub = s // dilation
    nb = sub // BLOCK
    assert nb * BLOCK == sub
    unroll = min(BAND_UNROLL, nb)
    assert nb % unroll == 0
    scale = LANES ** -0.5

    def body(sl_ref, q_ref, k_ref, v_ref, o_ref, lse_ref):
        slope = sl_ref[pl.program_id(0)]
        valid2, bias2 = _band_masks(dilation, slope)
        valid1, bias1 = valid2[:, BLOCK:], bias2[:, BLOCK:]

        def block(start_q, kk, vv, valid, bias):
            qb = q_ref[pl.ds(start_q, BLOCK), :]
            sc = lax.dot_general(qb, kk, (((1,), (1,)), ((), ())), preferred_element_type=F32) * scale
            sc = jnp.where(valid, sc + bias, NEG_INF)
            m = jnp.max(sc, axis=1, keepdims=True)
            e = jnp.exp(sc - m)
            den = jnp.sum(e, axis=1, keepdims=True)
            p = (e / den).astype(BF16)
            o_ref[pl.ds(start_q, BLOCK), :] = jnp.dot(p, vv, preferred_element_type=F32)
            lse_ref[pl.ds(start_q, BLOCK), :] = jnp.broadcast_to(m + jnp.log(den), (BLOCK, LANES))

        block(0, k_ref[0:BLOCK, :], v_ref[0:BLOCK, :], valid1, bias1)

        def general(jj):
            start_q, start_k = _aligned(jj * BLOCK), _aligned((jj - 1) * BLOCK)
            block(start_q, k_ref[pl.ds(start_k, 2 * BLOCK), :], v_ref[pl.ds(start_k, 2 * BLOCK), :], valid2, bias2)

        for jj in range(1, unroll):
            general(jj)

        def loop(t, carry):
            for u in range(unroll):
                general(t * unroll + u)
            return carry

        lax.fori_loop(1, nb // unroll, loop, 0)

    def spec(off):
        return pl.BlockSpec((sub, LANES), lambda h, r: (r, off + h))

    out = pl.BlockSpec((sub, LANES), lambda h, r: (r, h))
    return pl.pallas_call(
        body, name=name, grid=(DIL_HEADS_PER_GROUP, dilation),
        in_specs=[pl.BlockSpec(memory_space=pltpu.SMEM), spec(qoff), spec(koff), spec(voff)],
        out_specs=[out, out],
        out_shape=[jax.ShapeDtypeStruct((s, DIL_HEADS_PER_GROUP * LANES), F32)] * 2,
        compiler_params=_cparams(("parallel", "parallel"), VMEM_LIMIT_BIG),
    )(slopes, q, k, v)


def _band_bwd(q, k, v, do, lse, delta, slopes, *, name, dilation, qoff, koff, voff):
    s = q.shape[0]
    sub = s // dilation
    nb = sub // BLOCK
    unroll = min(BAND_UNROLL, nb)
    scale = LANES ** -0.5

    def body(sl_ref, q_ref, k_ref, v_ref, do_ref, lse_ref, dl_ref, dq_ref, dk_ref, dv_ref):
        slope = sl_ref[pl.program_id(0)]
        valid2, bias2 = _band_masks(dilation, slope)
        valid1, bias1 = valid2[:, BLOCK:], bias2[:, BLOCK:]

        def block(start_q, kk, vv, valid, bias):
            qb = q_ref[pl.ds(start_q, BLOCK), :]
            dob = do_ref[pl.ds(start_q, BLOCK), :]
            sc = lax.dot_general(qb, kk, (((1,), (1,)), ((), ())), preferred_element_type=F32) * scale
            sc = jnp.where(valid, sc + bias, NEG_INF)
            p = jnp.exp(sc - lse_ref[pl.ds(start_q, BLOCK), :][:, :1])
            dp = lax.dot_general(dob, vv, (((1,), (1,)), ((), ())), preferred_element_type=F32)
            ds = (p * (dp - dl_ref[pl.ds(start_q, BLOCK), :][:, :1])).astype(BF16)
            dq = jnp.dot(ds, kk, preferred_element_type=F32) * scale
            dq_ref[pl.ds(start_q, BLOCK), :] = dq.astype(BF16)
            dkk = lax.dot_general(ds, qb, (((0,), (0,)), ((), ())), preferred_element_type=F32) * scale
            dvv = lax.dot_general(p.astype(BF16), dob, (((0,), (0,)), ((), ())), preferred_element_type=F32)
            return dkk, dvv

        carry0 = block(0, k_ref[0:BLOCK, :], v_ref[0:BLOCK, :], valid1, bias1)

        def general(jj, carry):
            dk_part, dv_part = carry
            start_q, start_k = _aligned(jj * BLOCK), _aligned((jj - 1) * BLOCK)
            dkk, dvv = block(start_q, k_ref[pl.ds(start_k, 2 * BLOCK), :], v_ref[pl.ds(start_k, 2 * BLOCK), :],
                             valid2, bias2)
            dk_ref[pl.ds(start_k, BLOCK), :] = (dk_part + dkk[:BLOCK]).astype(BF16)
            dv_ref[pl.ds(start_k, BLOCK), :] = (dv_part + dvv[:BLOCK]).astype(BF16)
            return dkk[BLOCK:], dvv[BLOCK:]

        for jj in range(1, unroll):
            carry0 = general(jj, carry0)

        def loop(t, carry):
            for u in range(unroll):
                carry = general(t * unroll + u, carry)
            return carry

        dk_last, dv_last = lax.fori_loop(1, nb // unroll, loop, carry0)
        dk_ref[(nb - 1) * BLOCK:nb * BLOCK, :] = dk_last.astype(BF16)
        dv_ref[(nb - 1) * BLOCK:nb * BLOCK, :] = dv_last.astype(BF16)

    def spec(off):
        return pl.BlockSpec((sub, LANES), lambda h, r: (r, off + h))

    out = spec(0)
    return pl.pallas_call(
        body, name=name, grid=(DIL_HEADS_PER_GROUP, dilation),
        in_specs=[pl.BlockSpec(memory_space=pltpu.SMEM), spec(qoff), spec(koff), spec(voff), out, out, out],
        out_specs=[out, out, out],
        out_shape=[jax.ShapeDtypeStruct((s, DIL_HEADS_PER_GROUP * LANES), BF16)] * 3,
        compiler_params=_cparams(("parallel", "parallel"), VMEM_LIMIT_BIG),
    )(slopes, q, k, v, do, lse, delta)


def _mix_fwd(outs, lses, *, name):
    s, n = outs[0].shape
    ts = _rows(s, 512)

    def body(o0, o1, o2, l0, l1, l2, y_ref):
        la, lb, lc = l0[...], l1[...], l2[...]
        m = jnp.maximum(jnp.maximum(la, lb), lc)
        ea, eb, ec = jnp.exp(la - m), jnp.exp(lb - m), jnp.exp(lc - m)
        den = ea + eb + ec
        y = (ea / den) * o0[...] + (eb / den) * o1[...] + (ec / den) * o2[...]
        y_ref[...] = y.astype(BF16)

    row = pl.BlockSpec((ts, n), lambda i: (i, 0))
    return pl.pallas_call(
        body, name=name, grid=(s // ts,), in_specs=[row] * 6, out_specs=row,
        out_shape=jax.ShapeDtypeStruct((s, n), BF16), compiler_params=_cparams(("parallel",)),
    )(*outs, *lses)


def _mix_bwd(dy, outs, lses, *, name):
    s, n = dy.shape
    nh = n // LANES
    ts = _rows(s, 256)

    def body(dy_ref, o0, o1, o2, l0, l1, l2, d0, d1, d2, e0, e1, e2):
        la, lb, lc = l0[...], l1[...], l2[...]
        m = jnp.maximum(jnp.maximum(la, lb), lc)
        ea, eb, ec = jnp.exp(la - m), jnp.exp(lb - m), jnp.exp(lc - m)
        den = ea + eb + ec
        wa, wb, wc = ea / den, eb / den, ec / den
        dyv = dy_ref[...]
        y = wa * o0[...] + wb * o1[...] + wc * o2[...]
        prod = dyv * y
        d0[...] = (wa * dyv).astype(BF16)
        d1[...] = (wb * dyv).astype(BF16)
        d2[...] = (wc * dyv).astype(BF16)
        for h in range(nh):
            sl = slice(h * LANES, (h + 1) * LANES)
            t = jnp.sum(prod[:, sl], axis=1, keepdims=True)
            e0[:, sl] = wa[:, sl] * t
            e1[:, sl] = wb[:, sl] * t
            e2[:, sl] = wc[:, sl] * t

    row = pl.BlockSpec((ts, n), lambda i: (i, 0))
    return pl.pallas_call(
        body, name=name, grid=(s // ts,), in_specs=[row] * 7, out_specs=[row] * 6,
        out_shape=[jax.ShapeDtypeStruct((s, n), BF16)] * 3 + [jax.ShapeDtypeStruct((s, n), F32)] * 3,
        compiler_params=_cparams(("parallel",)),
    )(dy, *outs, *lses)


def _gate_fwd(gp, b_gate, branches, *, name):
    s = gp.shape[0]
    ts = _rows(s, 256)

    def body(gp_ref, b_ref, b0, b1, b2, o_ref):
        tot = None
        for i, br in enumerate((b0, b1, b2)):
            sl = slice(i * D_MODEL, (i + 1) * D_MODEL)
            t = jax.nn.sigmoid(gp_ref[:, sl] + b_ref[:, sl]) * br[...]
            tot = t if tot is None else tot + t
        o_ref[...] = tot.astype(BF16)

    row = pl.BlockSpec((ts, D_MODEL), lambda i: (i, 0))
    return pl.pallas_call(
        body, name=name, grid=(s // ts,),
        in_specs=[pl.BlockSpec((ts, 3 * D_MODEL), lambda i: (i, 0)), pl.BlockSpec((1, 3 * D_MODEL), lambda i: (0, 0)),
                  row, row, row],
        out_specs=row, out_shape=jax.ShapeDtypeStruct((s, D_MODEL), BF16),
        compiler_params=_cparams(("parallel",)),
    )(gp, b_gate, *branches)


def _gate_bwd(dm, gp, b_gate, branches, *, name):
    s = gp.shape[0]
    ts = _rows(s, 256)

    def body(dm_ref, gp_ref, b_ref, b0, b1, b2, d0, d1, d2, dgp_ref, db_ref):
        i = pl.program_id(0)

        @pl.when(i == 0)
        def _():
            db_ref[...] = jnp.zeros_like(db_ref)

        dmv = dm_ref[...]
        for k, (br, dbr) in enumerate(((b0, d0), (b1, d1), (b2, d2))):
            sl = slice(k * D_MODEL, (k + 1) * D_MODEL)
            sg = jax.nn.sigmoid(gp_ref[:, sl] + b_ref[:, sl])
            dbr[...] = (dmv * sg).astype(BF16)
            dg = dmv * br[...] * sg * (1.0 - sg)
            dgp_ref[:, sl] = dg.astype(BF16)
            db_ref[:, sl] += jnp.sum(dg, axis=0, keepdims=True)

    row = pl.BlockSpec((ts, D_MODEL), lambda i: (i, 0))
    wide = pl.BlockSpec((ts, 3 * D_MODEL), lambda i: (i, 0))
    vec = pl.BlockSpec((1, 3 * D_MODEL), lambda i: (0, 0))
    return pl.pallas_call(
        body, name=name, grid=(s // ts,),
        in_specs=[row, wide, vec, row, row, row], out_specs=[row, row, row, wide, vec],
        out_shape=[jax.ShapeDtypeStruct((s, D_MODEL), BF16)] * 3
        + [jax.ShapeDtypeStruct((s, 3 * D_MODEL), BF16), jax.ShapeDtypeStruct((1, 3 * D_MODEL), F32)],
        compiler_params=_cparams(("arbitrary",)),
    )(dm, gp, b_gate, *branches)


CONV_TC = 1408


def _shift_down(x, halo, k):
    rolled = pltpu.roll(x, k, 0)
    r8 = lax.broadcasted_iota(jnp.int32, halo.shape, 0)
    top = jnp.where(r8 < k, pltpu.roll(halo, k, 0), rolled[:SUBLANES])
    return jnp.concatenate([top, rolled[SUBLANES:]], axis=0)


def _shift_up(x, halo, k):
    n = x.shape[0]
    rolled = pltpu.roll(x, n - k, 0)
    r8 = lax.broadcasted_iota(jnp.int32, halo.shape, 0)
    bot = jnp.where(r8 >= SUBLANES - k, pltpu.roll(halo, SUBLANES - k, 0), rolled[n - SUBLANES:])
    return jnp.concatenate([rolled[:n - SUBLANES], bot], axis=0)


def _conv_fwd(u, conv_w, conv_b, *, name):
    s = u.shape[0]
    ts = _rows(s, 256)
    nct = D_FF // CONV_TC
    per8 = ts // SUBLANES

    def body(ug, uv, hg, hv, wg, wv, bg, bv, zg_ref, zv_ref, a_ref):
        first = pl.program_id(1) == 0

        def conv(u_ref, h_ref, w_ref, b_ref):
            x = u_ref[...]
            halo = jnp.where(first, 0.0, h_ref[...])
            z = b_ref[...] + w_ref[0:1, :] * _shift_down(x, halo, 2)
            z = z + w_ref[1:2, :] * _shift_down(x, halo, 1)
            return z + w_ref[2:3, :] * x

        zg = conv(ug, hg, wg, bg)
        zv = conv(uv, hv, wv, bv)
        zg_ref[...] = zg
        zv_ref[...] = zv
        a_ref[...] = (zg * jax.nn.sigmoid(zg) * zv).astype(BF16)

    def col(off):
        return pl.BlockSpec((ts, CONV_TC), lambda c, i: (i, c + off))

    def halo(off):
        return pl.BlockSpec((SUBLANES, CONV_TC), lambda c, i: (jnp.maximum(i * per8 - 1, 0), c + off))

    def wspec(rows, off):
        return pl.BlockSpec((rows, CONV_TC), lambda c, i: (0, c + off))

    zg, zv, a = pl.pallas_call(
        body, name=name, grid=(nct, s // ts),
        in_specs=[col(0), col(nct), halo(0), halo(nct), wspec(3, 0), wspec(3, nct), wspec(1, 0), wspec(1, nct)],
        out_specs=[col(0), col(0), col(0)],
        out_shape=[jax.ShapeDtypeStruct((s, D_FF), F32)] * 2 + [jax.ShapeDtypeStruct((s, D_FF), BF16)],
        compiler_params=_cparams(("parallel", "parallel")),
    )(u, u, u, u, conv_w, conv_w, conv_b, conv_b)
    return zg, zv, a


def _conv_bwd(da, zg, zv, u, conv_w, *, name):
    s = da.shape[0]
    ts = _rows(s, 256)
    nct = D_FF // CONV_TC
    per8 = ts // SUBLANES
    nrow = s // ts
    last8 = s // SUBLANES - 1

    def dz_of(dav, g, val):
        sg = jax.nn.sigmoid(g)
        return dav * val * sg * (1.0 + g * (1.0 - sg)), dav * g * sg

    def body(da_ref, zg_ref, zv_ref, da_nx, zg_nx, zv_nx, ug_ref, uv_ref, ug_pv, uv_pv, wg_ref, wv_ref,
             dug_ref, duv_ref, accg_ref, accv_ref):
        i = pl.program_id(1)
        dzg, dzv = dz_of(da_ref[...], zg_ref[...], zv_ref[...])
        da_next = jnp.where(i == nrow - 1, 0.0, da_nx[...])
        nxg, nxv = dz_of(da_next, zg_nx[...], zv_nx[...])

        @pl.when(i == 0)
        def _():
            accg_ref[...] = jnp.zeros_like(accg_ref)
            accv_ref[...] = jnp.zeros_like(accv_ref)

        for dz, nxt, u_ref, pv_ref, w_ref, du_ref, acc_ref in (
                (dzg, nxg, ug_ref, ug_pv, wg_ref, dug_ref, accg_ref),
                (dzv, nxv, uv_ref, uv_pv, wv_ref, duv_ref, accv_ref)):
            du = w_ref[2:3, :] * dz + w_ref[1:2, :] * _shift_up(dz, nxt, 1) + w_ref[0:1, :] * _shift_up(dz, nxt, 2)
            du_ref[...] = du.astype(BF16)
            x = u_ref[...]
            prev = jnp.where(i == 0, 0.0, pv_ref[...])
            acc_ref[0:1, :] += jnp.sum(dz * _shift_down(x, prev, 2), axis=0, keepdims=True)
            acc_ref[1:2, :] += jnp.sum(dz * _shift_down(x, prev, 1), axis=0, keepdims=True)
            acc_ref[2:3, :] += jnp.sum(dz * x, axis=0, keepdims=True)
            acc_ref[3:4, :] += jnp.sum(dz, axis=0, keepdims=True)

    def blk(off):
        return pl.BlockSpec((ts, CONV_TC), lambda c, i: (i, c + off))

    def nxt8(off):
        return pl.BlockSpec((SUBLANES, CONV_TC), lambda c, i: (jnp.minimum((i + 1) * per8, last8), c + off))

    def prv8(off):
        return pl.BlockSpec((SUBLANES, CONV_TC), lambda c, i: (jnp.maximum(i * per8 - 1, 0), c + off))

    def wspec(off):
        return pl.BlockSpec((3, CONV_TC), lambda c, i: (0, c + off))

    acc = pl.BlockSpec((SUBLANES, CONV_TC), lambda c, i: (0, c))
    return pl.pallas_call(
        body, name=name, grid=(nct, nrow),
        in_specs=[blk(0), blk(0), blk(0), nxt8(0), nxt8(0), nxt8(0), blk(0), blk(nct), prv8(0), prv8(nct),
                  wspec(0), wspec(nct)],
        out_specs=[blk(0), blk(0), acc, acc],
        out_shape=[jax.ShapeDtypeStruct((s, D_FF), BF16)] * 2 + [jax.ShapeDtypeStruct((SUBLANES, D_FF), F32)] * 2,
        compiler_params=_cparams(("parallel", "arbitrary")),
    )(da, zg, zv, da, zg, zv, u, u, u, u, conv_w, conv_w)


def _peer(k):
    x, y, c = lax.axis_index("x"), lax.axis_index("y"), lax.axis_index("c")
    px = 1 - x if k & 4 else x
    py = 1 - y if k & 2 else y
    pc = 1 - c if k & 1 else c
    return (px, py, pc), 4 * px + 2 * py + pc


def _exchange(bufs, *, name, gather):
    n = len(bufs)
    npeer = N_DEV - 1

    def body(*refs):
        srcs, outs = refs[:n], refs[n:2 * n]
        send_sems, recv_sems, local_sems = refs[2 * n:]
        _, me = _peer(0)
        mine = [src if gather else src.at[me] for src in srcs]
        local = [pltpu.make_async_copy(mine[p], outs[p].at[me], local_sems.at[p]) for p in range(n)]
        for cp in local:
            cp.start()
        sends = []
        for k in range(1, N_DEV):
            dev, idx = _peer(k)
            for p in range(n):
                cp = pltpu.make_async_remote_copy(
                    src_ref=srcs[p] if gather else srcs[p].at[idx], dst_ref=outs[p].at[me],
                    send_sem=send_sems.at[p * npeer + k - 1], recv_sem=recv_sems.at[p * npeer + k - 1],
                    device_id=dev, device_id_type=pl.DeviceIdType.MESH)
                cp.start()
                sends.append(cp)
        for k in range(1, N_DEV):
            dev, idx = _peer(k)
            for p in range(n):
                pltpu.make_async_remote_copy(
                    src_ref=mine[p], dst_ref=outs[p].at[idx],
                    send_sem=send_sems.at[p * npeer + k - 1], recv_sem=recv_sems.at[p * npeer + k - 1],
                    device_id=dev, device_id_type=pl.DeviceIdType.MESH).wait_recv()
        for cp in sends:
            cp.wait_send()
        for cp in local:
            cp.wait()

    any_spec = pl.BlockSpec(memory_space=pl.ANY)
    return pl.pallas_call(
        body, name=name,
        in_specs=[any_spec] * n, out_specs=[any_spec] * n,
        out_shape=[jax.ShapeDtypeStruct((N_DEV,) + b.shape[-2:], b.dtype) for b in bufs],
        scratch_shapes=[pltpu.SemaphoreType.DMA((n * npeer,)), pltpu.SemaphoreType.DMA((n * npeer,)),
                        pltpu.SemaphoreType.DMA((n,))],
    )(*bufs)


_HBM = pl.BlockSpec(memory_space=pltpu.HBM)
_SEM = pl.BlockSpec(memory_space=pltpu.SEMAPHORE)
_EFFECT = pltpu.SideEffectType.DATAFLOW_SIDE_EFFECTING


def _split_copy(srcs, lands, send_sems, recv_sems, gather, k, p):
    _, me = _peer(0)
    dev, idx = _peer(k)
    sem = p * (N_DEV - 1) + k - 1
    return pltpu.make_async_remote_copy(
        src_ref=srcs[p] if gather else srcs[p].at[idx], dst_ref=lands[p].at[me],
        send_sem=send_sems.at[sem], recv_sem=recv_sems.at[sem],
        device_id=dev, device_id_type=pl.DeviceIdType.MESH)


def _exchange_start(bufs, after, *, name, gather):
    n = len(bufs)
    nsem = n * (N_DEV - 1)

    def body(*refs):
        srcs, lands = refs[:n], refs[n:2 * n]
        send_sems, recv_sems = refs[2 * n + 1], refs[2 * n + 2]
        token = refs[-1]
        for k in range(1, N_DEV):
            for p in range(n):
                _split_copy(srcs, lands, send_sems, recv_sems, gather, k, p).start()
        token[...] = jnp.zeros_like(token)

    hbm = lambda a: pltpu.with_memory_space_constraint(a, pltpu.HBM)
    lands = [lax.empty((N_DEV,) + b.shape[-2:], b.dtype) for b in bufs]
    mem = [pltpu.HBM(b.shape, b.dtype) for b in bufs] + [pltpu.HBM(l.shape, l.dtype) for l in lands]
    outs = pl.pallas_call(
        body, name=name,
        in_specs=[_HBM] * (2 * n) + [pl.BlockSpec(memory_space=pl.ANY)],
        out_specs=[_SEM, _SEM] + [_HBM] * (2 * n) + [pl.BlockSpec(memory_space=pltpu.VMEM)],
        out_shape=[pltpu.SemaphoreType.DMA((nsem,)), pltpu.SemaphoreType.DMA((nsem,))] + mem
        + [jax.ShapeDtypeStruct((SUBLANES, LANES), F32)],
        input_output_aliases={p: 2 + p for p in range(2 * n)},
        compiler_params=pltpu.CompilerParams(has_side_effects=_EFFECT),
    )(*[hbm(b) for b in bufs], *[hbm(l) for l in lands], after)
    return (outs[0], outs[1], outs[2:2 + n], outs[2 + n:2 + 2 * n]), outs[-1]


def _exchange_wait(handle, after, *, name, gather):
    send_sems, recv_sems, srcs, lands = handle
    n = len(srcs)

    def body(*refs):
        src_refs, land_refs = refs[:n], refs[n:2 * n]
        send_ref, recv_ref = refs[2 * n], refs[2 * n + 1]
        for k in range(1, N_DEV):
            for p in range(n):
                cp = _split_copy(src_refs, land_refs, send_ref, recv_ref, gather, k, p)
                cp.wait_send()
                cp.wait_recv()

    mem = [pltpu.HBM(b.shape, b.dtype) for b in srcs] + [pltpu.HBM(l.shape, l.dtype) for l in lands]
    outs = pl.pallas_call(
        body, name=name,
        in_specs=[_HBM] * (2 * n) + [_SEM, _SEM, pl.BlockSpec(memory_space=pl.ANY)],
        out_specs=[_HBM] * (2 * n), out_shape=mem,
        input_output_aliases={p: p for p in range(2 * n)},
        compiler_params=pltpu.CompilerParams(has_side_effects=_EFFECT),
    )(*srcs, *lands, send_sems, recv_sems, after)
    return outs[n:]


def _with_own(landed, own, me):
    return lax.dynamic_update_slice(landed, own[None], (me, 0, 0))


def _adamw(parts, w, m, v, *, name):
    rows, width = w.shape
    tr = _rows(rows, max(16, (128 * 1024) // width), mult=16)

    def body(p_ref, w_ref, m_ref, v_ref, g_ref, d_ref, nm_ref, nv_ref):
        g = p_ref[0].astype(F32)
        for k in range(1, N_DEV):
            g = g + p_ref[k].astype(F32)
        mn = ADAM_B1 * m_ref[...] + (1.0 - ADAM_B1) * g
        vn = ADAM_B2 * v_ref[...] + (1.0 - ADAM_B2) * jnp.square(g)
        m_hat = mn / (1.0 - ADAM_B1 ** ADAM_STEP)
        v_hat = vn / (1.0 - ADAM_B2 ** ADAM_STEP)
        g_ref[...] = g
        d_ref[...] = -ADAM_LR * (m_hat / (jnp.sqrt(v_hat) + ADAM_EPS) + ADAM_WD * w_ref[...])
        nm_ref[...] = mn
        nv_ref[...] = vn

    row = pl.BlockSpec((tr, width), lambda i: (i, 0))
    return pl.pallas_call(
        body, name=name, grid=(rows // tr,),
        in_specs=[pl.BlockSpec((N_DEV, tr, width), lambda i: (0, i, 0)), row, row, row],
        out_specs=[row] * 4, out_shape=[jax.ShapeDtypeStruct((rows, width), F32)] * 4,
        compiler_params=_cparams(("parallel",)),
    )(parts, w, m, v)


def _pack_replicated(blocks):
    flat = jnp.concatenate([blocks[name].reshape(-1).astype(F32) for name, _, _ in REPLICATED])
    flat = jnp.pad(flat, (0, PACK_ROWS * PACK_W - PACK_USED))
    return flat.reshape(PACK_ROWS, PACK_W)


def _unpack_replicated(buf):
    flat = buf.reshape(-1)
    return {name: flat[off:off + n].reshape(1, n) for name, (off, n) in PACK_TABLE.items()}


def _join_shards(seg, shape, axis):
    return seg.reshape(shape) if axis == 0 else seg.transpose(1, 0, 2).reshape(shape)


def _split_shards(g, shape, axis):
    r, c = shape
    if axis == 0:
        return g.reshape(N_DEV, r // N_DEV, c)
    return g.reshape(r, N_DEV, c // N_DEV).transpose(1, 0, 2)


def _to_residues(a, d):
    s, c = a.shape
    return a.reshape(s // d, d, c).transpose(1, 0, 2).reshape(s, c)


def _from_residues(a, d):
    s, c = a.shape
    return a.reshape(d, s // d, c).transpose(1, 0, 2).reshape(s, c)


def _pad_heads(w, heads, width, lo, hi):
    r = w.shape[0]
    w = w.reshape(r, heads, width)[:, :, lo:hi]
    w = jnp.pad(w, ((0, 0), (0, 0), (0, LANES - (hi - lo))))
    return w.reshape(r, heads * LANES)


class _NoOverlap:
    start_token = None

    def late_weights(self, w, after):
        return w

    def early_grads(self, names, grads):
        return None


def _after(vec, token):
    return vec if token is None else vec + token[0:1, 0:1]


def _local_step(x, mem, positions, target, w, hooks=_NoOverlap()):
    s = x.shape[0]
    bf = lambda a: a.astype(BF16)

    w_in = w["w_in"]
    kr_cols = jnp.pad(w_in[:, OFF_KV:OFF_KR], ((0, 0), (MLA_NOPE, LANES - MLA_QK_DIM)))
    w_a = bf(jnp.concatenate([w_in[:, :OFF_KV], kr_cols], axis=1))
    w_dm = bf(w_in[:, OFF_KR:OFF_MEMQ])
    w_g = bf(w_in[:, OFF_MEMQ:])
    w_in_t = jnp.concatenate([w_a, w_dm, w_g], axis=1).T
    tabs = _rope_tables(positions)

    h = _rms_fwd(x, _after(w["g_pre_mix"], hooks.start_token), name="rms_pre_mix", out_dtype=BF16)
    p_a = _matmul(h, w_a, name="proj_a")
    p_dm = _matmul(h, w_dm, name="proj_dm", out_dtype=BF16)
    p_g = _matmul(h, w_g, name="proj_gate")
    c_q, c_kv, kr = p_a[:, :OFF_Q], p_a[:, OFF_Q:OFF_KV], p_a[:, OFF_KV:]

    w = hooks.late_weights(w, p_g)
    wq = bf(_pad_heads(w["w_uq"], MLA_HEADS, MLA_QK_DIM, 0, MLA_QK_DIM))
    wk = bf(_pad_heads(w["w_ukv"], MLA_HEADS, MLA_NOPE + MLA_V, 0, MLA_NOPE))
    wv = bf(_pad_heads(w["w_ukv"], MLA_HEADS, MLA_NOPE + MLA_V, MLA_NOPE, MLA_NOPE + MLA_V))
    w_mkv = bf(w["w_mem_kv"])
    wb_mla = bf(jnp.pad(w["w_br_mla"].reshape(MLA_HEADS, MLA_V, D_MODEL),
                        ((0, 0), (0, LANES - MLA_V), (0, 0))).reshape(MLA_HEADS * LANES, D_MODEL))
    wb_dil, wb_mem, w_o = bf(w["w_br_dil"]), bf(w["w_br_mem"]), bf(w["w_o"])
    w_up, w_down = bf(w["w_ffn_up"]), bf(w["w_ffn_down"])
    slopes = [jnp.asarray(sl, F32) for sl in DIL_SLOPES]
    mla_scale = MLA_QK_DIM ** -0.5
    mem_scale = LANES ** -0.5
    MQ = 3 * DIL_HEADS

    qn = _rms_fwd(c_q, w["mla_q_norm"], name="rms_q", out_dtype=BF16)
    kvn = _rms_fwd(c_kv, w["mla_kv_norm"], name="rms_kv", out_dtype=BF16)
    q_raw = _matmul(qn, wq, name="mla_q_up")
    k_raw = _matmul(kvn, wk, name="mla_k_up")
    v_f = _matmul(kvn, wv, name="mla_v_up", out_dtype=BF16)
    q_f = _rope_fwd(q_raw, tabs, name="rope_q", scale=mla_scale * LOG2E)
    k_f = _rope_fwd(k_raw, tabs, name="rope_k", scale=1.0, add=kr)
    o_mla, lse_mla = _causal_fwd(q_f, k_f, v_f, name="mla_fwd", heads=MLA_HEADS, ones_lane=MLA_V)

    dil_in, dil_o, dil_lse = [], [], []
    for g, (_, d) in enumerate(DIL_PAIRS):
        if d == 1:
            arrs, offs = (p_dm, p_dm, p_dm), (4 * g, DIL_HEADS + 4 * g, 2 * DIL_HEADS + 4 * g)
        else:
            arrs = tuple(_to_residues(p_dm[:, (t * DIL_HEADS + 4 * g) * LANES:(t * DIL_HEADS + 4 * g + 4) * LANES], d)
                         for t in range(3))
            offs = (0, 0, 0)
        o_g, lse_g = _band_fwd(*arrs, slopes[g], name=f"dil_fwd_{g}", dilation=d,
                               qoff=offs[0], koff=offs[1], voff=offs[2])
        dil_in.append((arrs, offs))
        dil_o.append(_from_residues(o_g, d))
        dil_lse.append(_from_residues(lse_g, d))
    y_dil = _mix_fwd(dil_o, dil_lse, name="dil_mix")

    memn = _rms_fwd(mem, w["g_mem"], name="rms_mem", out_dtype=BF16)
    kv_m = _matmul(memn, w_mkv, name="mem_kv", out_dtype=BF16)
    memat = dict(heads=MEM_HEADS, qoff=MQ, koff=0, voff=MEM_HEADS, causal=False, scale=mem_scale, tq=512, tk=256)
    o_mem, lse_mem = _flash_fwd(p_dm, kv_m, kv_m, name="mem_fwd", **memat)

    b_mla = _matmul(o_mla, wb_mla, name="br_mla")
    b_dil = _matmul(y_dil, wb_dil, name="br_dil")
    b_mem = _matmul(o_mem, wb_mem, name="br_mem")
    merged = _gate_fwd(p_g, w["b_gate"], (b_mla, b_dil, b_mem), name="gate_fwd")
    z1 = _matmul(merged, w_o, name="out_proj")
    x1 = _rms_fwd(z1, w["g_post_mix"], name="rms_post_mix", out_dtype=F32, add=x)
    h2 = _rms_fwd(x1, w["g_pre_ffn"], name="rms_pre_ffn", out_dtype=BF16)
    u = _matmul(h2, w_up, name="ffn_up")
    zg, zv, act = _conv_fwd(u, w["conv_w"], w["conv_b"], name="conv_fwd")
    f = _matmul(act, w_down, name="ffn_down")
    dy, df, g_post_ffn_grad, sq = _loss_step(x1, f, target, w["g_post_ffn"], name="loss")
    loss = 0.5 * jnp.sum(sq) / D_MODEL

    grads = {}
    grads["g_post_ffn"] = g_post_ffn_grad
    da = _matmul(df, w_down.T, name="ffn_down_dx")
    grads["w_ffn_down"] = _matmul(act, df, name="ffn_down_dw", trans_a=True)
    du_g, du_v, cacc_g, cacc_v = _conv_bwd(da, zg, zv, u, w["conv_w"], name="conv_bwd")
    grads["conv_w"] = jnp.concatenate([cacc_g[0:3], cacc_v[0:3]], axis=1)
    grads["conv_b"] = jnp.concatenate([cacc_g[3:4], cacc_v[3:4]], axis=1)
    w_up_t = w_up.T
    dh2 = _matmul(du_g, w_up_t[:D_FF], name="ffn_up_dx_gate")
    dh2 = _matmul(du_v, w_up_t[D_FF:], name="ffn_up_dx_val", add=dh2)
    grads["w_ffn_up"] = jnp.concatenate([_matmul(h2, du_g, name="ffn_up_dw_gate", trans_a=True),
                                         _matmul(h2, du_v, name="ffn_up_dw_val", trans_a=True)], axis=1)
    tok = hooks.early_grads(("w_ffn_down", "w_ffn_up", "conv_w"), grads)
    dx1, grads["g_pre_ffn"] = _rms_bwd(x1, _after(w["g_pre_ffn"], tok), dh2, name="rms_pre_ffn_bwd",
                                       out_dtype=F32, add=dy)
    dz1, grads["g_post_mix"] = _rms_bwd(z1, w["g_post_mix"], dx1, name="rms_post_mix_bwd", out_dtype=BF16)
    dmerged = _matmul(dz1, w_o.T, name="out_proj_dx")
    grads["w_o"] = _matmul(merged, dz1, name="out_proj_dw", trans_a=True)
    db_mla, db_dil, db_mem, dgp, grads["b_gate"] = _gate_bwd(
        dmerged, p_g, w["b_gate"], (b_mla, b_dil, b_mem), name="gate_bwd")

    do_mla = _matmul(db_mla, wb_mla.T, name="br_mla_dx", out_dtype=BF16)
    g_wb_mla = _matmul(o_mla, db_mla, name="br_mla_dw", trans_a=True)
    grads["w_br_mla"] = g_wb_mla.reshape(MLA_HEADS, LANES, D_MODEL)[:, :MLA_V].reshape(MLA_HEADS * MLA_V, D_MODEL)
    delta_mla = _row_dot(do_mla, o_mla, name="mla_delta")
    dq_f, dk_f, dv_f = _causal_bwd(q_f, k_f, v_f, do_mla, lse_mla, delta_mla, name="mla_bwd", heads=MLA_HEADS)
    dq_raw = _rope_bwd(dq_f, tabs, name="rope_q_bwd", scale=mla_scale, with_add=False)
    dk_raw, dkr = _rope_bwd(dk_f, tabs, name="rope_k_bwd", scale=1.0, with_add=True)
    dqn = _matmul(dq_raw, wq.T, name="mla_q_up_dx")
    g_wq = _matmul(qn, dq_raw, name="mla_q_up_dw", trans_a=True)
    grads["w_uq"] = g_wq.reshape(MLA_Q_RANK, MLA_HEADS, LANES)[:, :, :MLA_QK_DIM].reshape(MLA_Q_RANK, -1)
    dkvn = _matmul(dk_raw, wk.T, name="mla_k_up_dx")
    dkvn = _matmul(dv_f, wv.T, name="mla_v_up_dx", add=dkvn)
    g_wk = _matmul(kvn, dk_raw, name="mla_k_up_dw", trans_a=True).reshape(MLA_KV_RANK, MLA_HEADS, LANES)
    g_wv = _matmul(kvn, dv_f, name="mla_v_up_dw", trans_a=True).reshape(MLA_KV_RANK, MLA_HEADS, LANES)
    grads["w_ukv"] = jnp.concatenate([g_wk[:, :, :MLA_NOPE], g_wv[:, :, :MLA_V]], axis=2).reshape(MLA_KV_RANK, -1)
    dc_q, grads["mla_q_norm"] = _rms_bwd(c_q, w["mla_q_norm"], dqn, name="rms_q_bwd", out_dtype=BF16)
    dc_kv, grads["mla_kv_norm"] = _rms_bwd(c_kv, w["mla_kv_norm"], dkvn, name="rms_kv_bwd", out_dtype=BF16)

    dy_dil = _matmul(db_dil, wb_dil.T, name="br_dil_dx")
    grads["w_br_dil"] = _matmul(y_dil, db_dil, name="br_dil_dw", trans_a=True)
    mix = _mix_bwd(dy_dil, dil_o, dil_lse, name="dil_mix_bwd")
    d_dil = [[None] * 3 for _ in range(3)]
    for g, (_, d) in enumerate(DIL_PAIRS):
        arrs, offs = dil_in[g]
        do_g, dl_g, lse_g = mix[g], mix[3 + g], dil_lse[g]
        if d != 1:
            do_g, dl_g, lse_g = _to_residues(do_g, d), _to_residues(dl_g, d), _to_residues(lse_g, d)
        dq_g, dk_g, dv_g = _band_bwd(*arrs, do_g, lse_g, dl_g, slopes[g], name=f"dil_bwd_{g}", dilation=d,
                                     qoff=offs[0], koff=offs[1], voff=offs[2])
        for t, a in enumerate((dq_g, dk_g, dv_g)):
            d_dil[t][g] = a if d == 1 else _from_residues(a, d)

    do_mem = _matmul(db_mem, wb_mem.T, name="br_mem_dx", out_dtype=BF16)
    grads["w_br_mem"] = _matmul(o_mem, db_mem, name="br_mem_dw", trans_a=True)
    delta_mem = _row_dot(do_mem, o_mem, name="mem_delta")
    dq_mem = _flash_bwd_dq(p_dm, kv_m, kv_m, do_mem, lse_mem, delta_mem, name="mem_bwd_dq", out_dtype=BF16, **memat)
    dk_mem, dv_mem = _flash_bwd_dkv(p_dm, kv_m, kv_m, do_mem, lse_mem, delta_mem, name="mem_bwd_dkv",
                                    dk_dtype=BF16, dv_dtype=BF16, **memat)
    dkv_m = jnp.concatenate([dk_mem, dv_mem], axis=1)
    dmemn = _matmul(dkv_m, w_mkv.T, name="mem_kv_dx")
    grads["w_mem_kv"] = _matmul(memn, dkv_m, name="mem_kv_dw", trans_a=True)
    _, grads["g_mem"] = _rms_bwd(mem, w["g_mem"], dmemn, name="rms_mem_bwd", out_dtype=BF16)

    tok = hooks.early_grads(("w_o", "w_br_mla", "w_br_dil", "w_br_mem", "w_uq", "w_ukv", "w_mem_kv"), grads)
    if tok is not None:
        dkr = dkr + tok[0:1, 0:1]
    dp_all = jnp.concatenate([dc_q, dc_kv, bf(dkr)] + d_dil[0] + d_dil[1] + d_dil[2] + [dq_mem, dgp], axis=1)
    g_in = _matmul(h, dp_all, name="proj_dw", trans_a=True)
    grads["w_in"] = jnp.concatenate(
        [g_in[:, :OFF_KV], g_in[:, OFF_KV + MLA_NOPE:OFF_KV + MLA_QK_DIM], g_in[:, N_A:]], axis=1)
    tok = hooks.early_grads(("w_in",), grads)
    dh = _matmul(dp_all, w_in_t, name="proj_dx", after=tok)
    dx, grads["g_pre_mix"] = _rms_bwd(x, w["g_pre_mix"], dh, name="rms_pre_mix_bwd", out_dtype=F32, add=dx1)
    return loss, dx, grads


def kernel(x, mem, positions, g_pre_mix, w_in, b_gate, mla_q_norm, w_uq, mla_kv_norm, w_ukv, g_mem, w_mem_kv, w_br_mla, w_br_dil, w_br_mem, w_o, g_post_mix, g_pre_ffn, w_ffn_up, conv_w, conv_b, w_ffn_down, g_post_ffn, loss_target, m_g_pre_mix, m_w_in, m_b_gate, m_mla_q_norm, m_w_uq, m_mla_kv_norm, m_w_ukv, m_g_mem, m_w_mem_kv, m_w_br_mla, m_w_br_dil, m_w_br_mem, m_w_o, m_g_post_mix, m_g_pre_ffn, m_w_ffn_up, m_conv_w, m_conv_b, m_w_ffn_down, m_g_post_ffn, v_g_pre_mix, v_w_in, v_b_gate, v_mla_q_norm, v_w_uq, v_mla_kv_norm, v_w_ukv, v_g_mem, v_w_mem_kv, v_w_br_mla, v_w_br_dil, v_w_br_mem, v_w_o, v_g_post_mix, v_g_pre_ffn, v_w_ffn_up, v_conv_w, v_conv_b, v_w_ffn_down, v_g_post_ffn):
    local = dict(g_pre_mix=g_pre_mix, w_in=w_in, b_gate=b_gate, mla_q_norm=mla_q_norm, w_uq=w_uq,
                 mla_kv_norm=mla_kv_norm, w_ukv=w_ukv, g_mem=g_mem, w_mem_kv=w_mem_kv, w_br_mla=w_br_mla,
                 w_br_dil=w_br_dil, w_br_mem=w_br_mem, w_o=w_o, g_post_mix=g_post_mix, g_pre_ffn=g_pre_ffn,
                 w_ffn_up=w_ffn_up, conv_w=conv_w, conv_b=conv_b, w_ffn_down=w_ffn_down, g_post_ffn=g_post_ffn)
    mom_m = dict(g_pre_mix=m_g_pre_mix, w_in=m_w_in, b_gate=m_b_gate, mla_q_norm=m_mla_q_norm, w_uq=m_w_uq,
                 mla_kv_norm=m_mla_kv_norm, w_ukv=m_w_ukv, g_mem=m_g_mem, w_mem_kv=m_w_mem_kv, w_br_mla=m_w_br_mla,
                 w_br_dil=m_w_br_dil, w_br_mem=m_w_br_mem, w_o=m_w_o, g_post_mix=m_g_post_mix,
                 g_pre_ffn=m_g_pre_ffn, w_ffn_up=m_w_ffn_up, conv_w=m_conv_w, conv_b=m_conv_b,
                 w_ffn_down=m_w_ffn_down, g_post_ffn=m_g_post_ffn)
    mom_v = dict(g_pre_mix=v_g_pre_mix, w_in=v_w_in, b_gate=v_b_gate, mla_q_norm=v_mla_q_norm, w_uq=v_w_uq,
                 mla_kv_norm=v_mla_kv_norm, w_ukv=v_w_ukv, g_mem=v_g_mem, w_mem_kv=v_w_mem_kv, w_br_mla=v_w_br_mla,
                 w_br_dil=v_w_br_dil, w_br_mem=v_w_br_mem, w_o=v_w_o, g_post_mix=v_g_post_mix,
                 g_pre_ffn=v_g_pre_ffn, w_ffn_up=v_w_ffn_up, conv_w=v_conv_w, conv_b=v_conv_b,
                 w_ffn_down=v_w_ffn_down, g_post_ffn=v_g_post_ffn)

    me = 4 * lax.axis_index("x") + 2 * lax.axis_index("y") + lax.axis_index("c")
    spec = {name: (shape, axis) for name, shape, axis in SHARDED}
    wire = lambda name: F32 if name == "conv_w" else BF16
    shard = {name: local[name][0].astype(wire(name)) for name in spec}
    slab = lambda name, grads: _split_shards(grads[name].astype(wire(name)), *spec[name])

    w_in_all = _exchange([shard["w_in"]], name="gather_w_in", gather=True)[0]
    late = tuple(name for name in spec if name != "w_in")
    late_handle, late_token = _exchange_start([shard[n] for n in late], w_in_all, name="gather_rest_start",
                                              gather=True)
    full = {"w_in": _join_shards(w_in_all, *spec["w_in"])}
    for name, _, _ in REPLICATED:
        full[name] = local[name].reshape(1, -1)

    pending = []

    class Overlap:
        start_token = late_token

        def late_weights(self, w, after):
            landed = _exchange_wait(late_handle, after, name="gather_rest_wait", gather=True)
            w = dict(w)
            for name, buf in zip(late, landed):
                w[name] = _join_shards(_with_own(buf, shard[name], me), *spec[name])
            return w

        def early_grads(self, names, grads):
            slabs = [slab(name, grads) for name in names]
            handle, token = _exchange_start(slabs, slabs[0], name="grads_start_" + names[0], gather=False)
            pending.append((names, slabs, handle))
            return token

    loss, dx, grads = _local_step(x[0], mem[0], positions[0], loss_target[0], full, Overlap())

    parts = {}
    for names, slabs, handle in pending:
        landed = _exchange_wait(handle, dx, name="grads_wait_" + names[0], gather=False)
        for name, own, buf in zip(names, slabs, landed):
            parts[name] = _with_own(buf, lax.dynamic_index_in_dim(own, me, 0, keepdims=False), me)
    rep_parts = _exchange([_pack_replicated(grads)], name="gather_replicated_grads", gather=True)[0]

    results = {}
    for name in spec:
        res = _adamw(parts[name], local[name][0], mom_m[name][0], mom_v[name][0], name="adamw_" + name)
        results[name] = [r[None] for r in res]
    rep = _adamw(rep_parts, _pack_replicated(local), _pack_replicated(mom_m), _pack_replicated(mom_v),
                 name="adamw_replicated")
    for i, buf in enumerate(rep):
        for name, val in _unpack_replicated(buf).items():
            results.setdefault(name, [None] * 4)[i] = val

    loss = lax.psum(loss, ("x", "y", "c"))
    outs = [loss, dx[None]]
    for i in range(4):
        outs.extend(results[name][i] for name in PARAM_NAMES)
    return tuple(outs)
```

```python
import functools

import numpy as np
import jax
import jax.numpy as jnp
from jax import lax
from jax.experimental import pallas as pl
from jax.experimental.pallas import tpu as pltpu

F32 = jnp.float32
BF16 = jnp.bfloat16

N_DEV = 8
D_MODEL = 1024
RMS_EPS = 1e-6
NEG_INF = -1e30
LANES = 128
SUBLANES = 8
BLOCK = 128

MLA_HEADS = 8
MLA_NOPE = 64
MLA_ROPE = 32
MLA_V = 64
MLA_QK_DIM = MLA_NOPE + MLA_ROPE
MLA_Q_RANK = 384
MLA_KV_RANK = 256
ROPE_THETA = 10000.0
DIL_PAIRS = ((128, 1), (512, 4), (2048, 16))
DIL_HEADS_PER_GROUP = 4
DIL_HEADS = 12
MEM_HEADS = 4
D_FF = 2816
OFF_Q = MLA_Q_RANK
OFF_KV = OFF_Q + MLA_KV_RANK
OFF_KR = OFF_KV + MLA_ROPE
OFF_DIL = OFF_KR + 3 * DIL_HEADS * LANES
OFF_MEMQ = OFF_DIL + MEM_HEADS * LANES
D_IN = OFF_MEMQ + 3 * D_MODEL
N_A = OFF_KV + LANES
N_DM = 3 * DIL_HEADS * LANES + MEM_HEADS * LANES

ADAM_LR = 0.001
ADAM_B1 = 0.9
ADAM_B2 = 0.999
ADAM_EPS = 1e-08
ADAM_WD = 0.01
ADAM_STEP = 10

VMEM_LIMIT = 48 * 1024 * 1024
VMEM_LIMIT_BIG = 58 * 1024 * 1024
PACK_W = 1024

_ALIBI_BASE = np.exp2(-8.0 * np.arange(1, DIL_HEADS + 1) / DIL_HEADS)
DIL_SLOPES = [[float(_ALIBI_BASE[hh * 3 + g]) for hh in range(DIL_HEADS_PER_GROUP)] for g in range(3)]

PARAMS = (
    ("g_pre_mix", (1024,), None), ("w_in", (1024, D_IN), 1), ("b_gate", (3072,), None),
    ("mla_q_norm", (384,), None), ("w_uq", (384, 768), 1), ("mla_kv_norm", (256,), None),
    ("w_ukv", (256, 1024), 1), ("g_mem", (1024,), None), ("w_mem_kv", (1024, 1024), 0),
    ("w_br_mla", (512, 1024), 1), ("w_br_dil", (512, 1024), 1), ("w_br_mem", (512, 1024), 1),
    ("w_o", (1024, 1024), 0), ("g_post_mix", (1024,), None), ("g_pre_ffn", (1024,), None),
    ("w_ffn_up", (1024, 2 * D_FF), 1), ("conv_w", (3, 2 * D_FF), 1), ("conv_b", (2 * D_FF,), None),
    ("w_ffn_down", (D_FF, 1024), 0), ("g_post_ffn", (1024,), None),
)
PARAM_NAMES = tuple(p[0] for p in PARAMS)


def _shard_shape(shape, axis):
    if axis is None:
        return shape
    return tuple(s // N_DEV if a == axis else s for a, s in enumerate(shape))


SHARDED = tuple(p for p in PARAMS if p[2] is not None)
REPLICATED = tuple(p for p in PARAMS if p[2] is None)


def _layout():
    off, table = 0, {}
    for name, shape, _ in REPLICATED:
        table[name] = (off, shape[0])
        off += shape[0]
    rows = -(-off // PACK_W)
    rows = -(-rows // SUBLANES) * SUBLANES
    return table, off, rows


PACK_TABLE, PACK_USED, PACK_ROWS = _layout()


def _pick(n, cap):
    best = None
    for t in range(LANES, min(n, cap) + 1, LANES):
        if n % t == 0:
            best = t
    return best if best is not None else n


def _rows(n, cap, mult=SUBLANES):
    best = None
    for t in range(mult, min(n, cap) + 1, mult):
        if n % t == 0:
            best = t
    return best if best is not None else n


def _cparams(sem, vmem=VMEM_LIMIT):
    return pltpu.CompilerParams(dimension_semantics=sem, vmem_limit_bytes=vmem)


def _matmul(a, b, *, name, out_dtype=F32, trans_a=False, add=None, after=None, tm=1024, tn=1408, tk=640):
    if trans_a:
        kc, m = a.shape
    else:
        m, kc = a.shape
    n = b.shape[1]
    assert b.shape[0] == kc
    tm, tn, tk = _pick(m, tm), _pick(n, tn), _pick(kc, tk)
    nk = kc // tk

    def body(*refs):
        a_ref, b_ref = refs[:2]
        c_ref = refs[2] if add is not None else None
        o_ref, acc = refs[-2:]
        k = pl.program_id(2)

        @pl.when(k == 0)
        def _():
            if add is None:
                acc[...] = jnp.zeros_like(acc)
            else:
                acc[...] = c_ref[...].astype(F32)

        av = a_ref[...].astype(BF16)
        bv = b_ref[...].astype(BF16)
        if trans_a:
            acc[...] += lax.dot_general(av, bv, (((0,), (0,)), ((), ())), preferred_element_type=F32)
        else:
            acc[...] += jnp.dot(av, bv, preferred_element_type=F32)

        @pl.when(k == nk - 1)
        def _():
            o_ref[...] = acc[...].astype(out_dtype)

    if trans_a:
        a_spec = pl.BlockSpec((tk, tm), lambda i, j, k: (k, i))
    else:
        a_spec = pl.BlockSpec((tm, tk), lambda i, j, k: (i, k))
    in_specs = [a_spec, pl.BlockSpec((tk, tn), lambda i, j, k: (k, j))]
    args = [a, b]
    if add is not None:
        in_specs.append(pl.BlockSpec((tm, tn), lambda i, j, k: (i, j)))
        args.append(add)
    if after is not None:
        in_specs.append(pl.BlockSpec(memory_space=pl.ANY))
        args.append(after)
    return pl.pallas_call(
        body, name=name, grid=(m // tm, n // tn, nk),
        in_specs=in_specs, out_specs=pl.BlockSpec((tm, tn), lambda i, j, k: (i, j)),
        out_shape=jax.ShapeDtypeStruct((m, n), out_dtype),
        scratch_shapes=[pltpu.VMEM((tm, tn), F32)],
        compiler_params=_cparams(("parallel", "parallel", "arbitrary")),
    )(*args)


def _rms_fwd(x, g, *, name, out_dtype, add=None):
    s, n = x.shape
    ts = _rows(s, 512)

    def body(*refs):
        if add is None:
            x_ref, g_ref, o_ref = refs
        else:
            x_ref, g_ref, a_ref, o_ref = refs
        xv = x_ref[...]
        r = lax.rsqrt(jnp.mean(xv * xv, axis=-1, keepdims=True) + RMS_EPS)
        y = xv * r * g_ref[...]
        if add is not None:
            y = a_ref[...] + y
        o_ref[...] = y.astype(out_dtype)

    row = pl.BlockSpec((ts, n), lambda i: (i, 0))
    in_specs = [row, pl.BlockSpec((1, n), lambda i: (0, 0))]
    args = [x, g]
    if add is not None:
        in_specs.append(row)
        args.append(add)
    return pl.pallas_call(
        body, name=name, grid=(s // ts,), in_specs=in_specs, out_specs=row,
        out_shape=jax.ShapeDtypeStruct((s, n), out_dtype),
        compiler_params=_cparams(("parallel",)),
    )(*args)


def _rms_bwd(x, g, dy, *, name, out_dtype, add=None):
    s, n = x.shape
    ts = _rows(s, 512)

    def body(*refs):
        if add is None:
            x_ref, g_ref, dy_ref, dx_ref, dg_ref = refs
        else:
            x_ref, g_ref, dy_ref, a_ref, dx_ref, dg_ref = refs
        i = pl.program_id(0)
        xv = x_ref[...]
        dyv = dy_ref[...].astype(F32)
        r = lax.rsqrt(jnp.mean(xv * xv, axis=-1, keepdims=True) + RMS_EPS)
        nx = xv * r
        gdy = dyv * g_ref[...]
        dx = r * (gdy - nx * jnp.mean(nx * gdy, axis=-1, keepdims=True))
        if add is not None:
            dx = a_ref[...] + dx
        dx_ref[...] = dx.astype(out_dtype)

        @pl.when(i == 0)
        def _():
            dg_ref[...] = jnp.zeros_like(dg_ref)

        dg_ref[...] += jnp.sum(dyv * nx, axis=0, keepdims=True)

    row = pl.BlockSpec((ts, n), lambda i: (i, 0))
    vec = pl.BlockSpec((1, n), lambda i: (0, 0))
    in_specs = [row, vec, row]
    args = [x, g, dy]
    if add is not None:
        in_specs.append(row)
        args.append(add)
    return pl.pallas_call(
        body, name=name, grid=(s // ts,), in_specs=in_specs, out_specs=[row, vec],
        out_shape=[jax.ShapeDtypeStruct((s, n), out_dtype), jax.ShapeDtypeStruct((1, n), F32)],
        compiler_params=_cparams(("arbitrary",)),
    )(*args)


def _loss_step(x1, f, target, g, *, name):
    s, n = x1.shape
    ts = _rows(s, 512)

    def body(x_ref, f_ref, t_ref, g_ref, dy_ref, df_ref, dg_ref, sq_ref):
        i = pl.program_id(0)
        fv = f_ref[...]
        gv = g_ref[...]
        r = lax.rsqrt(jnp.mean(fv * fv, axis=-1, keepdims=True) + RMS_EPS)
        nx = fv * r
        err = x_ref[...] + nx * gv - t_ref[...]
        dy = err * (1.0 / n)
        dy_ref[...] = dy
        gdy = dy * gv
        df_ref[...] = (r * (gdy - nx * jnp.mean(nx * gdy, axis=-1, keepdims=True))).astype(BF16)

        @pl.when(i == 0)
        def _():
            sq_ref[...] = jnp.zeros_like(sq_ref)
            dg_ref[...] = jnp.zeros_like(dg_ref)

        sq_ref[...] += jnp.sum(err * err, axis=0, keepdims=True)
        dg_ref[...] += jnp.sum(dy * nx, axis=0, keepdims=True)

    row = pl.BlockSpec((ts, n), lambda i: (i, 0))
    vec = pl.BlockSpec((1, n), lambda i: (0, 0))
    return pl.pallas_call(
        body, name=name, grid=(s // ts,), in_specs=[row, row, row, vec], out_specs=[row, row, vec, vec],
        out_shape=[jax.ShapeDtypeStruct((s, n), F32), jax.ShapeDtypeStruct((s, n), BF16),
                   jax.ShapeDtypeStruct((1, n), F32), jax.ShapeDtypeStruct((1, n), F32)],
        compiler_params=_cparams(("arbitrary",)),
    )(x1, f, target, g)


def _rope_tables(positions):
    half = MLA_ROPE // 2
    inv_freq = ROPE_THETA ** (-jnp.arange(half, dtype=F32) / half)
    ang = positions.astype(F32)[:, None] * inv_freq[None, :]
    cos, sin = jnp.cos(ang), jnp.sin(ang)
    s = positions.shape[0]
    one = jnp.ones((s, MLA_NOPE), F32)
    zero = jnp.zeros((s, MLA_NOPE), F32)
    pad1 = jnp.ones((s, LANES - MLA_QK_DIM), F32)
    pad0 = jnp.zeros((s, LANES - MLA_QK_DIM), F32)
    zh = jnp.zeros((s, half), F32)
    c_tab = jnp.concatenate([one, cos, cos, pad1], axis=1)
    s1_tab = jnp.concatenate([zero, -sin, zh, pad0], axis=1)
    s2_tab = jnp.concatenate([zero, zh, sin, pad0], axis=1)
    return c_tab, s1_tab, s2_tab


def _rope_fwd(x, tabs, *, name, scale, add=None):
    s, n = x.shape
    nh = n // LANES
    ts = _rows(s, 512)
    half = MLA_ROPE // 2

    def body(*refs):
        if add is None:
            x_ref, c_ref, s1_ref, s2_ref, o_ref = refs
        else:
            x_ref, a_ref, c_ref, s1_ref, s2_ref, o_ref = refs
        c, s1, s2 = c_ref[...], s1_ref[...], s2_ref[...]
        for h in range(nh):
            xh = x_ref[:, h * LANES:(h + 1) * LANES]
            if add is not None:
                xh = xh + a_ref[...]
            y = xh * c + pltpu.roll(xh, LANES - half, 1) * s1 + pltpu.roll(xh, half, 1) * s2
            o_ref[:, h * LANES:(h + 1) * LANES] = (y * scale).astype(BF16)

    row = pl.BlockSpec((ts, n), lambda i: (i, 0))
    tab = pl.BlockSpec((ts, LANES), lambda i: (i, 0))
    in_specs = [row] + ([tab] if add is not None else []) + [tab, tab, tab]
    args = [x] + ([add] if add is not None else []) + list(tabs)
    return pl.pallas_call(
        body, name=name, grid=(s // ts,), in_specs=in_specs, out_specs=row,
        out_shape=jax.ShapeDtypeStruct((s, n), BF16),
        compiler_params=_cparams(("parallel",)),
    )(*args)


def _rope_bwd(dy, tabs, *, name, scale, with_add):
    s, n = dy.shape
    nh = n // LANES
    ts = _rows(s, 512)
    half = MLA_ROPE // 2

    def body(*refs):
        if with_add:
            dy_ref, c_ref, s1_ref, s2_ref, dx_ref, da_ref = refs
        else:
            dy_ref, c_ref, s1_ref, s2_ref, dx_ref = refs
        c, s1, s2 = c_ref[...], s1_ref[...], s2_ref[...]
        tot = None
        for h in range(nh):
            g = dy_ref[:, h * LANES:(h + 1) * LANES].astype(F32)
            dx = (g * c + pltpu.roll(g * s1, half, 1) + pltpu.roll(g * s2, LANES - half, 1)) * scale
            dx_ref[:, h * LANES:(h + 1) * LANES] = dx.astype(BF16)
            tot = dx if tot is None else tot + dx
        if with_add:
            da_ref[...] = tot

    row = pl.BlockSpec((ts, n), lambda i: (i, 0))
    tab = pl.BlockSpec((ts, LANES), lambda i: (i, 0))
    out_specs = [row, tab] if with_add else row
    out_shape = [jax.ShapeDtypeStruct((s, n), BF16)]
    if with_add:
        out_shape.append(jax.ShapeDtypeStruct((s, LANES), F32))
    else:
        out_shape = out_shape[0]
    return pl.pallas_call(
        body, name=name, grid=(s // ts,), in_specs=[row, tab, tab, tab], out_specs=out_specs,
        out_shape=out_shape, compiler_params=_cparams(("parallel",)),
    )(dy, *tabs)


def _scores(q, k, scale, diag):
    s = lax.dot_general(q, k, (((1,), (1,)), ((), ())), preferred_element_type=F32)
    if scale != 1.0:
        s = s * scale
    if diag:
        rows = lax.broadcasted_iota(jnp.int32, s.shape, 0)
        cols = lax.broadcasted_iota(jnp.int32, s.shape, 1)
        s = jnp.where(cols <= rows, s, NEG_INF)
    return s


def _flash_fwd(q, k, v, *, name, heads, qoff, koff, voff, causal, scale, tq, tk):
    s_q, s_kv = q.shape[0], k.shape[0]
    tq, tk = min(tq, s_q), min(tk, s_kv)
    nq, nk = s_q // tq, s_kv // tk
    if causal:
        assert tq == tk and s_q == s_kv

    def body(q_ref, k_ref, v_ref, o_ref, lse_ref, m_s, l_s, acc):
        i, j = pl.program_id(1), pl.program_id(2)

        @pl.when(j == 0)
        def _():
            m_s[...] = jnp.full_like(m_s, NEG_INF)
            l_s[...] = jnp.zeros_like(l_s)
            acc[...] = jnp.zeros_like(acc)

        def step(diag):
            s = _scores(q_ref[...], k_ref[...], scale, diag)
            m_prev = m_s[...]
            m_cur = jnp.maximum(m_prev, jnp.max(s, axis=1, keepdims=True))
            alpha = jnp.exp(m_prev - m_cur)
            p = jnp.exp(s - m_cur[:, :1])
            l_s[...] = alpha * l_s[...] + jnp.sum(p, axis=1, keepdims=True)
            acc[...] = alpha * acc[...] + jnp.dot(p.astype(BF16), v_ref[...], preferred_element_type=F32)
            m_s[...] = m_cur

        def finish():
            o_ref[...] = (acc[...] / l_s[...]).astype(o_ref.dtype)
            lse_ref[...] = m_s[...] + jnp.log(l_s[...])

        if causal:
            @pl.when(j < i)
            def _():
                step(False)

            @pl.when(j == i)
            def _():
                step(True)
                finish()
        else:
            step(False)

            @pl.when(j == nk - 1)
            def _():
                finish()

    def kv_idx(off):
        if causal:
            return lambda h, i, j: (jnp.minimum(j, i), off + h)
        return lambda h, i, j: (j, off + h)

    blk_q = pl.BlockSpec((tq, LANES), lambda h, i, j: (i, qoff + h))
    out_q = pl.BlockSpec((tq, LANES), lambda h, i, j: (i, h))
    return pl.pallas_call(
        body, name=name, grid=(heads, nq, nk),
        in_specs=[blk_q, pl.BlockSpec((tk, LANES), kv_idx(koff)), pl.BlockSpec((tk, LANES), kv_idx(voff))],
        out_specs=[out_q, out_q],
        out_shape=[jax.ShapeDtypeStruct((s_q, heads * LANES), BF16),
                   jax.ShapeDtypeStruct((s_q, heads * LANES), F32)],
        scratch_shapes=[pltpu.VMEM((tq, LANES), F32)] * 3,
        compiler_params=_cparams(("parallel", "parallel", "arbitrary")),
    )(q, k, v)


def _flash_bwd_dq(q, k, v, do, lse, delta, *, name, heads, qoff, koff, voff, causal, scale, tq, tk, out_dtype):
    s_q, s_kv = q.shape[0], k.shape[0]
    tq, tk = min(tq, s_q), min(tk, s_kv)
    nq, nk = s_q // tq, s_kv // tk

    def body(q_ref, k_ref, v_ref, do_ref, lse_ref, dl_ref, dq_ref, acc):
        i, j = pl.program_id(1), pl.program_id(2)

        @pl.when(j == 0)
        def _():
            acc[...] = jnp.zeros_like(acc)

        def step(diag):
            s = _scores(q_ref[...], k_ref[...], scale, diag)
            p = jnp.exp(s - lse_ref[:, :1])
            dp = lax.dot_general(do_ref[...], v_ref[...], (((1,), (1,)), ((), ())), preferred_element_type=F32)
            ds = p * (dp - dl_ref[:, :1])
            acc[...] += jnp.dot(ds.astype(BF16), k_ref[...], preferred_element_type=F32)

        def finish():
            dq_ref[...] = (acc[...] * scale).astype(out_dtype)

        if causal:
            @pl.when(j < i)
            def _():
                step(False)

            @pl.when(j == i)
            def _():
                step(True)
                finish()
        else:
            step(False)

            @pl.when(j == nk - 1)
            def _():
                finish()

    def kv_idx(off):
        if causal:
            return lambda h, i, j: (jnp.minimum(j, i), off + h)
        return lambda h, i, j: (j, off + h)

    blk_q = pl.BlockSpec((tq, LANES), lambda h, i, j: (i, qoff + h))
    blk_h = pl.BlockSpec((tq, LANES), lambda h, i, j: (i, h))
    return pl.pallas_call(
        body, name=name, grid=(heads, nq, nk),
        in_specs=[blk_q, pl.BlockSpec((tk, LANES), kv_idx(koff)), pl.BlockSpec((tk, LANES), kv_idx(voff)),
                  blk_h, blk_h, blk_h],
        out_specs=blk_h,
        out_shape=jax.ShapeDtypeStruct((s_q, heads * LANES), out_dtype),
        scratch_shapes=[pltpu.VMEM((tq, LANES), F32)],
        compiler_params=_cparams(("parallel", "parallel", "arbitrary")),
    )(q, k, v, do, lse, delta)


def _flash_bwd_dkv(q, k, v, do, lse, delta, *, name, heads, qoff, koff, voff, causal, scale, tq, tk,
                   dk_dtype, dv_dtype):
    s_q, s_kv = q.shape[0], k.shape[0]
    tq, tk = min(tq, s_q), min(tk, s_kv)
    nq, nk = s_q // tq, s_kv // tk

    def body(q_ref, k_ref, v_ref, do_ref, lse_ref, dl_ref, dk_ref, dv_ref, dk_acc, dv_acc):
        j, i = pl.program_id(1), pl.program_id(2)

        @pl.when(i == 0)
        def _():
            dk_acc[...] = jnp.zeros_like(dk_acc)
            dv_acc[...] = jnp.zeros_like(dv_acc)

        def step(diag):
            s = _scores(q_ref[...], k_ref[...], scale, diag)
            p = jnp.exp(s - lse_ref[:, :1])
            dov = do_ref[...]
            dp = lax.dot_general(dov, v_ref[...], (((1,), (1,)), ((), ())), preferred_element_type=F32)
            ds = p * (dp - dl_ref[:, :1])
            dv_acc[...] += lax.dot_general(p.astype(BF16), dov, (((0,), (0,)), ((), ())),
                                           preferred_element_type=F32)
            dk_acc[...] += lax.dot_general(ds.astype(BF16), q_ref[...], (((0,), (0,)), ((), ())),
                                           preferred_element_type=F32)

        if causal:
            @pl.when(i > j)
            def _():
                step(False)

            @pl.when(i == j)
            def _():
                step(True)
        else:
            step(False)

        @pl.when(i == nq - 1)
        def _():
            dk_ref[...] = (dk_acc[...] * scale).astype(dk_dtype)
            dv_ref[...] = dv_acc[...].astype(dv_dtype)

    def q_idx(off):
        if causal:
            return lambda h, j, i: (jnp.maximum(i, j), off + h)
        return lambda h, j, i: (i, off + h)

    blk_h = pl.BlockSpec((tq, LANES), q_idx(0))
    out_k = pl.BlockSpec((tk, LANES), lambda h, j, i: (j, h))
    return pl.pallas_call(
        body, name=name, grid=(heads, nk, nq),
        in_specs=[pl.BlockSpec((tq, LANES), q_idx(qoff)),
                  pl.BlockSpec((tk, LANES), lambda h, j, i: (j, koff + h)),
                  pl.BlockSpec((tk, LANES), lambda h, j, i: (j, voff + h)),
                  blk_h, blk_h, blk_h],
        out_specs=[out_k, out_k],
        out_shape=[jax.ShapeDtypeStruct((s_kv, heads * LANES), dk_dtype),
                   jax.ShapeDtypeStruct((s_kv, heads * LANES), dv_dtype)],
        scratch_shapes=[pltpu.VMEM((tk, LANES), F32)] * 2,
        compiler_params=_cparams(("parallel", "parallel", "arbitrary")),
    )(q, k, v, do, lse, delta)


def _row_dot(a, b, *, name):
    s, n = a.shape
    nh = n // LANES
    ts = _rows(s, 512)

    def body(a_ref, b_ref, o_ref):
        for h in range(nh):
            sl = slice(h * LANES, (h + 1) * LANES)
            d = jnp.sum(a_ref[:, sl].astype(F32) * b_ref[:, sl].astype(F32), axis=1, keepdims=True)
            o_ref[:, sl] = jnp.broadcast_to(d, (ts, LANES))

    row = pl.BlockSpec((ts, n), lambda i: (i, 0))
    return pl.pallas_call(
        body, name=name, grid=(s // ts,), in_specs=[row, row], out_specs=row,
        out_shape=jax.ShapeDtypeStruct((s, n), F32), compiler_params=_cparams(("parallel",)),
    )(a, b)


CAUSAL_T = 512
LOG2E = 1.4426950408889634
LN2 = 0.6931471805599453


def _causal_fwd(q, k, v, *, name, heads, ones_lane):
    s = q.shape[0]
    t = CAUSAL_T
    nq = s // (2 * t)
    assert nq * 2 * t == s

    def body(q_ref, k_ref, v_ref, o_ref, lse_ref, v1, m_s, acc):
        i = pl.program_id(1)

        @pl.when(i == 0)
        def _():
            lane = lax.broadcasted_iota(jnp.int32, v1.shape, 1)
            v1[...] = jnp.where(lane == ones_lane, 1.0, v_ref[...]).astype(BF16)

        m_s[...] = jnp.full_like(m_s, NEG_INF)
        acc[...] = jnp.zeros_like(acc)
        halves = (q_ref[0:t, :], q_ref[t:2 * t, :])

        def raw(c, j):
            rows = pl.ds(pl.multiple_of(j * t, t), t)
            return lax.dot_general(halves[c], k_ref[rows, :], (((1,), (1,)), ((), ())), preferred_element_type=F32)

        def update(c, sc, j):
            m_prev = m_s[c]
            m_cur = jnp.maximum(m_prev, jnp.max(sc, axis=1, keepdims=True))
            p = jnp.exp2(sc - m_cur[:, :1]).astype(BF16)
            acc[c] = jnp.exp2(m_prev - m_cur) * acc[c] + jnp.dot(
                p, v1[pl.ds(pl.multiple_of(j * t, t), t), :], preferred_element_type=F32)
            m_s[c] = m_cur

        def loop(j, carry):
            sa, sb = raw(0, j), raw(1, j)
            update(0, sa, j)
            update(1, sb, j)
            return carry

        lax.fori_loop(0, 2 * i, loop, 0)
        sa, sb = raw(0, 2 * i), raw(1, 2 * i)
        below = (lax.broadcasted_iota(jnp.int32, sa.shape, 1) <= lax.broadcasted_iota(jnp.int32, sa.shape, 0))
        update(0, jnp.where(below, sa, NEG_INF), 2 * i)
        update(1, sb, 2 * i)
        update(1, jnp.where(below, raw(1, 2 * i + 1), NEG_INF), 2 * i + 1)
        lane = lax.broadcasted_iota(jnp.int32, (t, LANES), 1)
        for c in range(2):
            out = acc[c]
            den = out[:, ones_lane:ones_lane + 1]
            o_ref[c * t:(c + 1) * t, :] = jnp.where(lane == ones_lane, 0.0, out / den).astype(BF16)
            lse_ref[c * t:(c + 1) * t, :] = m_s[c] + jnp.log2(den)

    blk = pl.BlockSpec((2 * t, LANES), lambda h, i: (i, h))
    full = pl.BlockSpec((s, LANES), lambda h, i: (0, h))
    return pl.pallas_call(
        body, name=name, grid=(heads, nq), in_specs=[blk, full, full], out_specs=[blk, blk],
        out_shape=[jax.ShapeDtypeStruct((s, heads * LANES), BF16), jax.ShapeDtypeStruct((s, heads * LANES), F32)],
        scratch_shapes=[pltpu.VMEM((s, LANES), BF16), pltpu.VMEM((2, t, LANES), F32),
                        pltpu.VMEM((2, t, LANES), F32)],
        compiler_params=_cparams(("parallel", "arbitrary")),
    )(q, k, v)


def _causal_bwd(q, k, v, do, lse, delta, *, name, heads):
    s = q.shape[0]
    t = min(CAUSAL_T, s)
    nt = s // t

    def body(q_ref, k_ref, v_ref, do_ref, lse_ref, dl_ref, dq_ref, dk_ref, dv_ref, dk_acc, dv_acc):
        j = pl.program_id(1)

        @pl.when(j == 0)
        def _():
            dq_ref[...] = jnp.zeros_like(dq_ref)

        dk_acc[...] = jnp.zeros_like(dk_acc)
        dv_acc[...] = jnp.zeros_like(dv_acc)
        kv, vv = k_ref[...], v_ref[...]

        def step(i, diag, size=t):
            rows = pl.ds(pl.multiple_of(i * t, t), size)
            qv, dov = q_ref[rows, :], do_ref[rows, :]
            sc = _scores(qv, kv, 1.0, diag)
            p = jnp.exp2(sc - lse_ref[rows, :][:, :1])
            dp = lax.dot_general(dov, vv, (((1,), (1,)), ((), ())), preferred_element_type=F32)
            ds = (p * (dp - dl_ref[rows, :][:, :1])).astype(BF16)
            dv_acc[...] += lax.dot_general(p.astype(BF16), dov, (((0,), (0,)), ((), ())),
                                           preferred_element_type=F32)
            dk_acc[...] += lax.dot_general(ds, qv, (((0,), (0,)), ((), ())), preferred_element_type=F32)
            dq_ref[rows, :] += jnp.dot(ds, kv, preferred_element_type=F32)

        step(j, True)
        odd = (nt - 1 - j) % 2

        @pl.when(odd == 1)
        def _():
            step(j + 1, False)

        def loop(n, carry):
            step(j + 1 + odd + 2 * n, False, 2 * t)
            return carry

        lax.fori_loop(0, (nt - 1 - j) // 2, loop, 0)
        dk_ref[...] = dk_acc[...] * LN2
        dv_ref[...] = dv_acc[...].astype(BF16)

    blk = pl.BlockSpec((t, LANES), lambda h, j: (j, h))
    full = pl.BlockSpec((s, LANES), lambda h, j: (0, h))
    return pl.pallas_call(
        body, name=name, grid=(heads, nt), in_specs=[full, blk, blk, full, full, full],
        out_specs=[full, blk, blk],
        out_shape=[jax.ShapeDtypeStruct((s, heads * LANES), F32), jax.ShapeDtypeStruct((s, heads * LANES), F32),
                   jax.ShapeDtypeStruct((s, heads * LANES), BF16)],
        scratch_shapes=[pltpu.VMEM((t, LANES), F32)] * 2,
        compiler_params=_cparams(("parallel", "arbitrary")),
    )(q, k, v, do, lse, delta)


def _band_masks(dilation, slope):
    qi = lax.broadcasted_iota(jnp.int32, (BLOCK, 2 * BLOCK), 0)
    kj = lax.broadcasted_iota(jnp.int32, (BLOCK, 2 * BLOCK), 1)
    dist = qi + BLOCK - kj
    valid = (dist >= 0) & (dist <= BLOCK)
    bias = -slope * (dist * dilation).astype(F32)
    return valid, bias


BAND_UNROLL = 4


def _aligned(start):
    return start if isinstance(start, int) else pl.multiple_of(start, BLOCK)


def _band_fwd(q, k, v, slopes, *, name, dilation, qoff, koff, voff):
    s = q.shape[0]
    sub = s // dilation
    nb = sub // BLOCK
    assert nb * BLOCK == sub
    unroll = min(BAND_UNROLL, nb)
    assert nb % unroll == 0
    scale = LANES ** -0.5

    def body(sl_ref, q_ref, k_ref, v_ref, o_ref, lse_ref):
        slope = sl_ref[pl.program_id(0)]
        valid2, bias2 = _band_masks(dilation, slope)
        valid1, bias1 = valid2[:, BLOCK:], bias2[:, BLOCK:]

        def block(start_q, kk, vv, valid, bias):
            qb = q_ref[pl.ds(start_q, BLOCK), :]
            sc = lax.dot_general(qb, kk, (((1,), (1,)), ((), ())), preferred_element_type=F32) * scale
            sc = jnp.where(valid, sc + bias, NEG_INF)
            m = jnp.max(sc, axis=1, keepdims=True)
            e = jnp.exp(sc - m)
            den = jnp.sum(e, axis=1, keepdims=True)
            p = (e / den).astype(BF16)
            o_ref[pl.ds(start_q, BLOCK), :] = jnp.dot(p, vv, preferred_element_type=F32)
            lse_ref[pl.ds(start_q, BLOCK), :] = jnp.broadcast_to(m + jnp.log(den), (BLOCK, LANES))

        block(0, k_ref[0:BLOCK, :], v_ref[0:BLOCK, :], valid1, bias1)

        def general(jj):
            start_q, start_k = _aligned(jj * BLOCK), _aligned((jj - 1) * BLOCK)
            block(start_q, k_ref[pl.ds(start_k, 2 * BLOCK), :], v_ref[pl.ds(start_k, 2 * BLOCK), :], valid2, bias2)

        for jj in range(1, unroll):
            general(jj)

        def loop(t, carry):
            for u in range(unroll):
                general(t * unroll + u)
            return carry

        lax.fori_loop(1, nb // unroll, loop, 0)

    def spec(off):
        return pl.BlockSpec((sub, LANES), lambda h, r: (r, off + h))

    out = pl.BlockSpec((sub, LANES), lambda h, r: (r, h))
    return pl.pallas_call(
        body, name=name, grid=(DIL_HEADS_PER_GROUP, dilation),
        in_specs=[pl.BlockSpec(memory_space=pltpu.SMEM), spec(qoff), spec(koff), spec(voff)],
        out_specs=[out, out],
        out_shape=[jax.ShapeDtypeStruct((s, DIL_HEADS_PER_GROUP * LANES), F32)] * 2,
        compiler_params=_cparams(("parallel", "parallel"), VMEM_LIMIT_BIG),
    )(slopes, q, k, v)


def _band_bwd(q, k, v, do, lse, delta, slopes, *, name, dilation, qoff, koff, voff):
    s = q.shape[0]
    sub = s // dilation
    nb = sub // BLOCK
    unroll = min(BAND_UNROLL, nb)
    scale = LANES ** -0.5

    def body(sl_ref, q_ref, k_ref, v_ref, do_ref, lse_ref, dl_ref, dq_ref, dk_ref, dv_ref):
        slope = sl_ref[pl.program_id(0)]
        valid2, bias2 = _band_masks(dilation, slope)
        valid1, bias1 = valid2[:, BLOCK:], bias2[:, BLOCK:]

        def block(start_q, kk, vv, valid, bias):
            qb = q_ref[pl.ds(start_q, BLOCK), :]
            dob = do_ref[pl.ds(start_q, BLOCK), :]
            sc = lax.dot_general(qb, kk, (((1,), (1,)), ((), ())), preferred_element_type=F32) * scale
            sc = jnp.where(valid, sc + bias, NEG_INF)
            p = jnp.exp(sc - lse_ref[pl.ds(start_q, BLOCK), :][:, :1])
            dp = lax.dot_general(dob, vv, (((1,), (1,)), ((), ())), preferred_element_type=F32)
            ds = (p * (dp - dl_ref[pl.ds(start_q, BLOCK), :][:, :1])).astype(BF16)
            dq = jnp.dot(ds, kk, preferred_element_type=F32) * scale
            dq_ref[pl.ds(start_q, BLOCK), :] = dq.astype(BF16)
            dkk = lax.dot_general(ds, qb, (((0,), (0,)), ((), ())), preferred_element_type=F32) * scale
            dvv = lax.dot_general(p.astype(BF16), dob, (((0,), (0,)), ((), ())), preferred_element_type=F32)
            return dkk, dvv

        carry0 = block(0, k_ref[0:BLOCK, :], v_ref[0:BLOCK, :], valid1, bias1)

        def general(jj, carry):
            dk_part, dv_part = carry
            start_q, start_k = _aligned(jj * BLOCK), _aligned((jj - 1) * BLOCK)
            dkk, dvv = block(start_q, k_ref[pl.ds(start_k, 2 * BLOCK), :], v_ref[pl.ds(start_k, 2 * BLOCK), :],
                             valid2, bias2)
            dk_ref[pl.ds(start_k, BLOCK), :] = (dk_part + dkk[:BLOCK]).astype(BF16)
            dv_ref[pl.ds(start_k, BLOCK), :] = (dv_part + dvv[:BLOCK]).astype(BF16)
            return dkk[BLOCK:], dvv[BLOCK:]

        for jj in range(1, unroll):
            carry0 = general(jj, carry0)

        def loop(t, carry):
            for u in range(unroll):
                carry = general(t * unroll + u, carry)
            return carry

        dk_last, dv_last = lax.fori_loop(1, nb // unroll, loop, carry0)
        dk_ref[(nb - 1) * BLOCK:nb * BLOCK, :] = dk_last.astype(BF16)
        dv_ref[(nb - 1) * BLOCK:nb * BLOCK, :] = dv_last.astype(BF16)

    def spec(off):
        return pl.BlockSpec((sub, LANES), lambda h, r: (r, off + h))

    out = spec(0)
    return pl.pallas_call(
        body, name=name, grid=(DIL_HEADS_PER_GROUP, dilation),
        in_specs=[pl.BlockSpec(memory_space=pltpu.SMEM), spec(qoff), spec(koff), spec(voff), out, out, out],
        out_specs=[out, out, out],
        out_shape=[jax.ShapeDtypeStruct((s, DIL_HEADS_PER_GROUP * LANES), BF16)] * 3,
        compiler_params=_cparams(("parallel", "parallel"), VMEM_LIMIT_BIG),
    )(slopes, q, k, v, do, lse, delta)


def _mix_fwd(outs, lses, *, name):
    s, n = outs[0].shape
    ts = _rows(s, 512)

    def body(o0, o1, o2, l0, l1, l2, y_ref):
        la, lb, lc = l0[...], l1[...], l2[...]
        m = jnp.maximum(jnp.maximum(la, lb), lc)
        ea, eb, ec = jnp.exp(la - m), jnp.exp(lb - m), jnp.exp(lc - m)
        den = ea + eb + ec
        y = (ea / den) * o0[...] + (eb / den) * o1[...] + (ec / den) * o2[...]
        y_ref[...] = y.astype(BF16)

    row = pl.BlockSpec((ts, n), lambda i: (i, 0))
    return pl.pallas_call(
        body, name=name, grid=(s // ts,), in_specs=[row] * 6, out_specs=row,
        out_shape=jax.ShapeDtypeStruct((s, n), BF16), compiler_params=_cparams(("parallel",)),
    )(*outs, *lses)


def _mix_bwd(dy, outs, lses, *, name):
    s, n = dy.shape
    nh = n // LANES
    ts = _rows(s, 256)

    def body(dy_ref, o0, o1, o2, l0, l1, l2, d0, d1, d2, e0, e1, e2):
        la, lb, lc = l0[...], l1[...], l2[...]
        m = jnp.maximum(jnp.maximum(la, lb), lc)
        ea, eb, ec = jnp.exp(la - m), jnp.exp(lb - m), jnp.exp(lc - m)
        den = ea + eb + ec
        wa, wb, wc = ea / den, eb / den, ec / den
        dyv = dy_ref[...]
        y = wa * o0[...] + wb * o1[...] + wc * o2[...]
        prod = dyv * y
        d0[...] = (wa * dyv).astype(BF16)
        d1[...] = (wb * dyv).astype(BF16)
        d2[...] = (wc * dyv).astype(BF16)
        for h in range(nh):
            sl = slice(h * LANES, (h + 1) * LANES)
            t = jnp.sum(prod[:, sl], axis=1, keepdims=True)
            e0[:, sl] = wa[:, sl] * t
            e1[:, sl] = wb[:, sl] * t
            e2[:, sl] = wc[:, sl] * t

    row = pl.BlockSpec((ts, n), lambda i: (i, 0))
    return pl.pallas_call(
        body, name=name, grid=(s // ts,), in_specs=[row] * 7, out_specs=[row] * 6,
        out_shape=[jax.ShapeDtypeStruct((s, n), BF16)] * 3 + [jax.ShapeDtypeStruct((s, n), F32)] * 3,
        compiler_params=_cparams(("parallel",)),
    )(dy, *outs, *lses)


def _gate_fwd(gp, b_gate, branches, *, name):
    s = gp.shape[0]
    ts = _rows(s, 256)

    def body(gp_ref, b_ref, b0, b1, b2, o_ref):
        tot = None
        for i, br in enumerate((b0, b1, b2)):
            sl = slice(i * D_MODEL, (i + 1) * D_MODEL)
            t = jax.nn.sigmoid(gp_ref[:, sl] + b_ref[:, sl]) * br[...]
            tot = t if tot is None else tot + t
        o_ref[...] = tot.astype(BF16)

    row = pl.BlockSpec((ts, D_MODEL), lambda i: (i, 0))
    return pl.pallas_call(
        body, name=name, grid=(s // ts,),
        in_specs=[pl.BlockSpec((ts, 3 * D_MODEL), lambda i: (i, 0)), pl.BlockSpec((1, 3 * D_MODEL), lambda i: (0, 0)),
                  row, row, row],
        out_specs=row, out_shape=jax.ShapeDtypeStruct((s, D_MODEL), BF16),
        compiler_params=_cparams(("parallel",)),
    )(gp, b_gate, *branches)


def _gate_bwd(dm, gp, b_gate, branches, *, name):
    s = gp.shape[0]
    ts = _rows(s, 256)

    def body(dm_ref, gp_ref, b_ref, b0, b1, b2, d0, d1, d2, dgp_ref, db_ref):
        i = pl.program_id(0)

        @pl.when(i == 0)
        def _():
            db_ref[...] = jnp.zeros_like(db_ref)

        dmv = dm_ref[...]
        for k, (br, dbr) in enumerate(((b0, d0), (b1, d1), (b2, d2))):
            sl = slice(k * D_MODEL, (k + 1) * D_MODEL)
            sg = jax.nn.sigmoid(gp_ref[:, sl] + b_ref[:, sl])
            dbr[...] = (dmv * sg).astype(BF16)
            dg = dmv * br[...] * sg * (1.0 - sg)
            dgp_ref[:, sl] = dg.astype(BF16)
            db_ref[:, sl] += jnp.sum(dg, axis=0, keepdims=True)

    row = pl.BlockSpec((ts, D_MODEL), lambda i: (i, 0))
    wide = pl.BlockSpec((ts, 3 * D_MODEL), lambda i: (i, 0))
    vec = pl.BlockSpec((1, 3 * D_MODEL), lambda i: (0, 0))
    return pl.pallas_call(
        body, name=name, grid=(s // ts,),
        in_specs=[row, wide, vec, row, row, row], out_specs=[row, row, row, wide, vec],
        out_shape=[jax.ShapeDtypeStruct((s, D_MODEL), BF16)] * 3
        + [jax.ShapeDtypeStruct((s, 3 * D_MODEL), BF16), jax.ShapeDtypeStruct((1, 3 * D_MODEL), F32)],
        compiler_params=_cparams(("arbitrary",)),
    )(dm, gp, b_gate, *branches)


CONV_TC = 1408


def _shift_down(x, halo, k):
    rolled = pltpu.roll(x, k, 0)
    r8 = lax.broadcasted_iota(jnp.int32, halo.shape, 0)
    top = jnp.where(r8 < k, pltpu.roll(halo, k, 0), rolled[:SUBLANES])
    return jnp.concatenate([top, rolled[SUBLANES:]], axis=0)


def _shift_up(x, halo, k):
    n = x.shape[0]
    rolled = pltpu.roll(x, n - k, 0)
    r8 = lax.broadcasted_iota(jnp.int32, halo.shape, 0)
    bot = jnp.where(r8 >= SUBLANES - k, pltpu.roll(halo, SUBLANES - k, 0), rolled[n - SUBLANES:])
    return jnp.concatenate([rolled[:n - SUBLANES], bot], axis=0)


def _conv_fwd(u, conv_w, conv_b, *, name):
    s = u.shape[0]
    ts = _rows(s, 256)
    nct = D_FF // CONV_TC
    per8 = ts // SUBLANES

    def body(ug, uv, hg, hv, wg, wv, bg, bv, zg_ref, zv_ref, a_ref):
        first = pl.program_id(1) == 0

        def conv(u_ref, h_ref, w_ref, b_ref):
            x = u_ref[...]
            halo = jnp.where(first, 0.0, h_ref[...])
            z = b_ref[...] + w_ref[0:1, :] * _shift_down(x, halo, 2)
            z = z + w_ref[1:2, :] * _shift_down(x, halo, 1)
            return z + w_ref[2:3, :] * x

        zg = conv(ug, hg, wg, bg)
        zv = conv(uv, hv, wv, bv)
        zg_ref[...] = zg.astype(BF16)
        zv_ref[...] = zv.astype(BF16)
        a_ref[...] = (zg * jax.nn.sigmoid(zg) * zv).astype(BF16)

    def col(off):
        return pl.BlockSpec((ts, CONV_TC), lambda c, i: (i, c + off))

    def halo(off):
        return pl.BlockSpec((SUBLANES, CONV_TC), lambda c, i: (jnp.maximum(i * per8 - 1, 0), c + off))

    def wspec(rows, off):
        return pl.BlockSpec((rows, CONV_TC), lambda c, i: (0, c + off))

    zg, zv, a = pl.pallas_call(
        body, name=name, grid=(nct, s // ts),
        in_specs=[col(0), col(nct), halo(0), halo(nct), wspec(3, 0), wspec(3, nct), wspec(1, 0), wspec(1, nct)],
        out_specs=[col(0), col(0), col(0)],
        out_shape=[jax.ShapeDtypeStruct((s, D_FF), BF16)] * 3,
        compiler_params=_cparams(("parallel", "parallel")),
    )(u, u, u, u, conv_w, conv_w, conv_b, conv_b)
    return zg, zv, a


def _conv_bwd(da, zg, zv, u, conv_w, *, name):
    s = da.shape[0]
    ts = _rows(s, 256)
    nct = D_FF // CONV_TC
    per8 = ts // SUBLANES
    nrow = s // ts
    last8 = s // SUBLANES - 1

    def dz_of(dav, g, val):
        sg = jax.nn.sigmoid(g)
        return dav * val * sg * (1.0 + g * (1.0 - sg)), dav * g * sg

    def body(da_ref, zg_ref, zv_ref, da_nx, zg_nx, zv_nx, ug_ref, uv_ref, ug_pv, uv_pv, wg_ref, wv_ref,
             dug_ref, duv_ref, accg_ref, accv_ref):
        i = pl.program_id(1)
        dzg, dzv = dz_of(da_ref[...], zg_ref[...].astype(F32), zv_ref[...].astype(F32))
        da_next = jnp.where(i == nrow - 1, 0.0, da_nx[...])
        nxg, nxv = dz_of(da_next, zg_nx[0:SUBLANES, :].astype(F32), zv_nx[0:SUBLANES, :].astype(F32))

        @pl.when(i == 0)
        def _():
            accg_ref[...] = jnp.zeros_like(accg_ref)
            accv_ref[...] = jnp.zeros_like(accv_ref)

        for dz, nxt, u_ref, pv_ref, w_ref, du_ref, acc_ref in (
                (dzg, nxg, ug_ref, ug_pv, wg_ref, dug_ref, accg_ref),
                (dzv, nxv, uv_ref, uv_pv, wv_ref, duv_ref, accv_ref)):
            du = w_ref[2:3, :] * dz + w_ref[1:2, :] * _shift_up(dz, nxt, 1) + w_ref[0:1, :] * _shift_up(dz, nxt, 2)
            du_ref[...] = du.astype(BF16)
            x = u_ref[...]
            prev = jnp.where(i == 0, 0.0, pv_ref[...])
            acc_ref[0:1, :] += jnp.sum(dz * _shift_down(x, prev, 2), axis=0, keepdims=True)
            acc_ref[1:2, :] += jnp.sum(dz * _shift_down(x, prev, 1), axis=0, keepdims=True)
            acc_ref[2:3, :] += jnp.sum(dz * x, axis=0, keepdims=True)
            acc_ref[3:4, :] += jnp.sum(dz, axis=0, keepdims=True)

    def blk(off):
        return pl.BlockSpec((ts, CONV_TC), lambda c, i: (i, c + off))

    def nxt8(off):
        return pl.BlockSpec((SUBLANES, CONV_TC), lambda c, i: (jnp.minimum((i + 1) * per8, last8), c + off))

    def prv8(off):
        return pl.BlockSpec((SUBLANES, CONV_TC), lambda c, i: (jnp.maximum(i * per8 - 1, 0), c + off))

    nxt16 = pl.BlockSpec((2 * SUBLANES, CONV_TC),
                         lambda c, i: (jnp.minimum((i + 1) * (per8 // 2), last8 // 2), c))

    def wspec(off):
        return pl.BlockSpec((3, CONV_TC), lambda c, i: (0, c + off))

    acc = pl.BlockSpec((SUBLANES, CONV_TC), lambda c, i: (0, c))
    return pl.pallas_call(
        body, name=name, grid=(nct, nrow),
        in_specs=[blk(0), blk(0), blk(0), nxt8(0), nxt16, nxt16, blk(0), blk(nct), prv8(0), prv8(nct),
                  wspec(0), wspec(nct)],
        out_specs=[blk(0), blk(0), acc, acc],
        out_shape=[jax.ShapeDtypeStruct((s, D_FF), BF16)] * 2 + [jax.ShapeDtypeStruct((SUBLANES, D_FF), F32)] * 2,
        compiler_params=_cparams(("parallel", "arbitrary")),
    )(da, zg, zv, da, zg, zv, u, u, u, u, conv_w, conv_w)


def _peer(k):
    x, y, c = lax.axis_index("x"), lax.axis_index("y"), lax.axis_index("c")
    px = 1 - x if k & 4 else x
    py = 1 - y if k & 2 else y
    pc = 1 - c if k & 1 else c
    return (px, py, pc), 4 * px + 2 * py + pc


def _exchange(bufs, *, name, gather):
    n = len(bufs)
    npeer = N_DEV - 1

    def body(*refs):
        srcs, outs = refs[:n], refs[n:2 * n]
        send_sems, recv_sems, local_sems = refs[2 * n:]
        _, me = _peer(0)
        mine = [src if gather else src.at[me] for src in srcs]
        local = [pltpu.make_async_copy(mine[p], outs[p].at[me], local_sems.at[p]) for p in range(n)]
        for cp in local:
            cp.start()
        sends = []
        for k in range(1, N_DEV):
            dev, idx = _peer(k)
            for p in range(n):
                cp = pltpu.make_async_remote_copy(
                    src_ref=srcs[p] if gather else srcs[p].at[idx], dst_ref=outs[p].at[me],
                    send_sem=send_sems.at[p * npeer + k - 1], recv_sem=recv_sems.at[p * npeer + k - 1],
                    device_id=dev, device_id_type=pl.DeviceIdType.MESH)
                cp.start()
                sends.append(cp)
        for k in range(1, N_DEV):
            dev, idx = _peer(k)
            for p in range(n):
                pltpu.make_async_remote_copy(
                    src_ref=mine[p], dst_ref=outs[p].at[idx],
                    send_sem=send_sems.at[p * npeer + k - 1], recv_sem=recv_sems.at[p * npeer + k - 1],
                    device_id=dev, device_id_type=pl.DeviceIdType.MESH).wait_recv()
        for cp in sends:
            cp.wait_send()
        for cp in local:
            cp.wait()

    any_spec = pl.BlockSpec(memory_space=pl.ANY)
    return pl.pallas_call(
        body, name=name,
        in_specs=[any_spec] * n, out_specs=[any_spec] * n,
        out_shape=[jax.ShapeDtypeStruct((N_DEV,) + b.shape[-2:], b.dtype) for b in bufs],
        scratch_shapes=[pltpu.SemaphoreType.DMA((n * npeer,)), pltpu.SemaphoreType.DMA((n * npeer,)),
                        pltpu.SemaphoreType.DMA((n,))],
    )(*bufs)


_HBM = pl.BlockSpec(memory_space=pltpu.HBM)
_SEM = pl.BlockSpec(memory_space=pltpu.SEMAPHORE)
_EFFECT = pltpu.SideEffectType.DATAFLOW_SIDE_EFFECTING


def _split_copy(srcs, lands, send_sems, recv_sems, gather, k, p):
    _, me = _peer(0)
    dev, idx = _peer(k)
    sem = p * (N_DEV - 1) + k - 1
    return pltpu.make_async_remote_copy(
        src_ref=srcs[p] if gather else srcs[p].at[idx], dst_ref=lands[p].at[me],
        send_sem=send_sems.at[sem], recv_sem=recv_sems.at[sem],
        device_id=dev, device_id_type=pl.DeviceIdType.MESH)


def _exchange_start(bufs, after, *, name, gather):
    n = len(bufs)
    nsem = n * (N_DEV - 1)

    def body(*refs):
        srcs, lands = refs[:n], refs[n:2 * n]
        send_sems, recv_sems = refs[2 * n + 1], refs[2 * n + 2]
        token = refs[-1]
        for k in range(1, N_DEV):
            for p in range(n):
                _split_copy(srcs, lands, send_sems, recv_sems, gather, k, p).start()
        token[...] = jnp.zeros_like(token)

    hbm = lambda a: pltpu.with_memory_space_constraint(a, pltpu.HBM)
    lands = [lax.empty((N_DEV,) + b.shape[-2:], b.dtype) for b in bufs]
    mem = [pltpu.HBM(b.shape, b.dtype) for b in bufs] + [pltpu.HBM(l.shape, l.dtype) for l in lands]
    outs = pl.pallas_call(
        body, name=name,
        in_specs=[_HBM] * (2 * n) + [pl.BlockSpec(memory_space=pl.ANY)],
        out_specs=[_SEM, _SEM] + [_HBM] * (2 * n) + [pl.BlockSpec(memory_space=pltpu.VMEM)],
        out_shape=[pltpu.SemaphoreType.DMA((nsem,)), pltpu.SemaphoreType.DMA((nsem,))] + mem
        + [jax.ShapeDtypeStruct((SUBLANES, LANES), F32)],
        input_output_aliases={p: 2 + p for p in range(2 * n)},
        compiler_params=pltpu.CompilerParams(has_side_effects=_EFFECT),
    )(*[hbm(b) for b in bufs], *[hbm(l) for l in lands], after)
    return (outs[0], outs[1], outs[2:2 + n], outs[2 + n:2 + 2 * n]), outs[-1]


def _exchange_wait(handle, after, *, name, gather):
    send_sems, recv_sems, srcs, lands = handle
    n = len(srcs)

    def body(*refs):
        src_refs, land_refs = refs[:n], refs[n:2 * n]
        send_ref, recv_ref = refs[2 * n], refs[2 * n + 1]
        for k in range(1, N_DEV):
            for p in range(n):
                cp = _split_copy(src_refs, land_refs, send_ref, recv_ref, gather, k, p)
                cp.wait_send()
                cp.wait_recv()

    mem = [pltpu.HBM(b.shape, b.dtype) for b in srcs] + [pltpu.HBM(l.shape, l.dtype) for l in lands]
    outs = pl.pallas_call(
        body, name=name,
        in_specs=[_HBM] * (2 * n) + [_SEM, _SEM, pl.BlockSpec(memory_space=pl.ANY)],
        out_specs=[_HBM] * (2 * n), out_shape=mem,
        input_output_aliases={p: p for p in range(2 * n)},
        compiler_params=pltpu.CompilerParams(has_side_effects=_EFFECT),
    )(*srcs, *lands, send_sems, recv_sems, after)
    return outs[n:]


def _with_own(landed, own, me):
    return lax.dynamic_update_slice(landed, own[None], (me, 0, 0))


def _adamw(parts, w, m, v, *, name):
    rows, width = w.shape
    tr = _rows(rows, max(16, (128 * 1024) // width), mult=16)

    def body(p_ref, w_ref, m_ref, v_ref, g_ref, d_ref, nm_ref, nv_ref):
        g = p_ref[0].astype(F32)
        for k in range(1, N_DEV):
            g = g + p_ref[k].astype(F32)
        mn = ADAM_B1 * m_ref[...] + (1.0 - ADAM_B1) * g
        vn = ADAM_B2 * v_ref[...] + (1.0 - ADAM_B2) * jnp.square(g)
        m_hat = mn / (1.0 - ADAM_B1 ** ADAM_STEP)
        v_hat = vn / (1.0 - ADAM_B2 ** ADAM_STEP)
        g_ref[...] = g
        d_ref[...] = -ADAM_LR * (m_hat / (jnp.sqrt(v_hat) + ADAM_EPS) + ADAM_WD * w_ref[...])
        nm_ref[...] = mn
        nv_ref[...] = vn

    row = pl.BlockSpec((tr, width), lambda i: (i, 0))
    return pl.pallas_call(
        body, name=name, grid=(rows // tr,),
        in_specs=[pl.BlockSpec((N_DEV, tr, width), lambda i: (0, i, 0)), row, row, row],
        out_specs=[row] * 4, out_shape=[jax.ShapeDtypeStruct((rows, width), F32)] * 4,
        compiler_params=_cparams(("parallel",)),
    )(parts, w, m, v)


def _pack_replicated(blocks):
    flat = jnp.concatenate([blocks[name].reshape(-1).astype(F32) for name, _, _ in REPLICATED])
    flat = jnp.pad(flat, (0, PACK_ROWS * PACK_W - PACK_USED))
    return flat.reshape(PACK_ROWS, PACK_W)


def _unpack_replicated(buf):
    flat = buf.reshape(-1)
    return {name: flat[off:off + n].reshape(1, n) for name, (off, n) in PACK_TABLE.items()}


def _join_shards(seg, shape, axis):
    return seg.reshape(shape) if axis == 0 else seg.transpose(1, 0, 2).reshape(shape)


def _split_shards(g, shape, axis):
    r, c = shape
    if axis == 0:
        return g.reshape(N_DEV, r // N_DEV, c)
    return g.reshape(r, N_DEV, c // N_DEV).transpose(1, 0, 2)


def _to_residues(a, d):
    s, c = a.shape
    return a.reshape(s // d, d, c).transpose(1, 0, 2).reshape(s, c)


def _from_residues(a, d):
    s, c = a.shape
    return a.reshape(d, s // d, c).transpose(1, 0, 2).reshape(s, c)


def _pad_heads(w, heads, width, lo, hi):
    r = w.shape[0]
    w = w.reshape(r, heads, width)[:, :, lo:hi]
    w = jnp.pad(w, ((0, 0), (0, 0), (0, LANES - (hi - lo))))
    return w.reshape(r, heads * LANES)


class _NoOverlap:
    start_token = None

    def late_weights(self, w, after):
        return w

    def early_grads(self, names, grads):
        return None


def _after(vec, token):
    return vec if token is None else vec + token[0:1, 0:1]


def _local_step(x, mem, positions, target, w, hooks=_NoOverlap()):
    s = x.shape[0]
    bf = lambda a: a.astype(BF16)

    w_in = w["w_in"]
    kr_cols = jnp.pad(w_in[:, OFF_KV:OFF_KR], ((0, 0), (MLA_NOPE, LANES - MLA_QK_DIM)))
    w_a = bf(jnp.concatenate([w_in[:, :OFF_KV], kr_cols], axis=1))
    w_dm = bf(w_in[:, OFF_KR:OFF_MEMQ])
    w_g = bf(w_in[:, OFF_MEMQ:])
    w_in_t = jnp.concatenate([w_a, w_dm, w_g], axis=1).T
    tabs = _rope_tables(positions)

    h = _rms_fwd(x, _after(w["g_pre_mix"], hooks.start_token), name="rms_pre_mix", out_dtype=BF16)
    p_a = _matmul(h, w_a, name="proj_a")
    p_dm = _matmul(h, w_dm, name="proj_dm", out_dtype=BF16)
    p_g = _matmul(h, w_g, name="proj_gate")
    c_q, c_kv, kr = p_a[:, :OFF_Q], p_a[:, OFF_Q:OFF_KV], p_a[:, OFF_KV:]

    w = hooks.late_weights(w, p_g)
    wq = bf(_pad_heads(w["w_uq"], MLA_HEADS, MLA_QK_DIM, 0, MLA_QK_DIM))
    wk = bf(_pad_heads(w["w_ukv"], MLA_HEADS, MLA_NOPE + MLA_V, 0, MLA_NOPE))
    wv = bf(_pad_heads(w["w_ukv"], MLA_HEADS, MLA_NOPE + MLA_V, MLA_NOPE, MLA_NOPE + MLA_V))
    w_mkv = bf(w["w_mem_kv"])
    wb_mla = bf(jnp.pad(w["w_br_mla"].reshape(MLA_HEADS, MLA_V, D_MODEL),
                        ((0, 0), (0, LANES - MLA_V), (0, 0))).reshape(MLA_HEADS * LANES, D_MODEL))
    wb_dil, wb_mem, w_o = bf(w["w_br_dil"]), bf(w["w_br_mem"]), bf(w["w_o"])
    w_up, w_down = bf(w["w_ffn_up"]), bf(w["w_ffn_down"])
    slopes = [jnp.asarray(sl, F32) for sl in DIL_SLOPES]
    mla_scale = MLA_QK_DIM ** -0.5
    mem_scale = LANES ** -0.5
    MQ = 3 * DIL_HEADS

    qn = _rms_fwd(c_q, w["mla_q_norm"], name="rms_q", out_dtype=BF16)
    kvn = _rms_fwd(c_kv, w["mla_kv_norm"], name="rms_kv", out_dtype=BF16)
    q_raw = _matmul(qn, wq, name="mla_q_up")
    k_raw = _matmul(kvn, wk, name="mla_k_up")
    v_f = _matmul(kvn, wv, name="mla_v_up", out_dtype=BF16)
    q_f = _rope_fwd(q_raw, tabs, name="rope_q", scale=mla_scale * LOG2E)
    k_f = _rope_fwd(k_raw, tabs, name="rope_k", scale=1.0, add=kr)
    o_mla, lse_mla = _causal_fwd(q_f, k_f, v_f, name="mla_fwd", heads=MLA_HEADS, ones_lane=MLA_V)

    dil_in, dil_o, dil_lse = [], [], []
    for g, (_, d) in enumerate(DIL_PAIRS):
        if d == 1:
            arrs, offs = (p_dm, p_dm, p_dm), (4 * g, DIL_HEADS + 4 * g, 2 * DIL_HEADS + 4 * g)
        else:
            arrs = tuple(_to_residues(p_dm[:, (t * DIL_HEADS + 4 * g) * LANES:(t * DIL_HEADS + 4 * g + 4) * LANES], d)
                         for t in range(3))
            offs = (0, 0, 0)
        o_g, lse_g = _band_fwd(*arrs, slopes[g], name=f"dil_fwd_{g}", dilation=d,
                               qoff=offs[0], koff=offs[1], voff=offs[2])
        dil_in.append((arrs, offs))
        dil_o.append(_from_residues(o_g, d))
        dil_lse.append(_from_residues(lse_g, d))
    y_dil = _mix_fwd(dil_o, dil_lse, name="dil_mix")

    memn = _rms_fwd(mem, w["g_mem"], name="rms_mem", out_dtype=BF16)
    kv_m = _matmul(memn, w_mkv, name="mem_kv", out_dtype=BF16)
    memat = dict(heads=MEM_HEADS, qoff=MQ, koff=0, voff=MEM_HEADS, causal=False, scale=mem_scale, tq=512, tk=256)
    o_mem, lse_mem = _flash_fwd(p_dm, kv_m, kv_m, name="mem_fwd", **memat)

    b_mla = _matmul(o_mla, wb_mla, name="br_mla")
    b_dil = _matmul(y_dil, wb_dil, name="br_dil")
    b_mem = _matmul(o_mem, wb_mem, name="br_mem")
    merged = _gate_fwd(p_g, w["b_gate"], (b_mla, b_dil, b_mem), name="gate_fwd")
    z1 = _matmul(merged, w_o, name="out_proj")
    x1 = _rms_fwd(z1, w["g_post_mix"], name="rms_post_mix", out_dtype=F32, add=x)
    h2 = _rms_fwd(x1, w["g_pre_ffn"], name="rms_pre_ffn", out_dtype=BF16)
    u = _matmul(h2, w_up, name="ffn_up")
    zg, zv, act = _conv_fwd(u, w["conv_w"], w["conv_b"], name="conv_fwd")
    f = _matmul(act, w_down, name="ffn_down")
    dy, df, g_post_ffn_grad, sq = _loss_step(x1, f, target, w["g_post_ffn"], name="loss")
    loss = 0.5 * jnp.sum(sq) / D_MODEL

    grads = {}
    grads["g_post_ffn"] = g_post_ffn_grad
    da = _matmul(df, w_down.T, name="ffn_down_dx")
    grads["w_ffn_down"] = _matmul(act, df, name="ffn_down_dw", trans_a=True)
    du_g, du_v, cacc_g, cacc_v = _conv_bwd(da, zg, zv, u, w["conv_w"], name="conv_bwd")
    grads["conv_w"] = jnp.concatenate([cacc_g[0:3], cacc_v[0:3]], axis=1)
    grads["conv_b"] = jnp.concatenate([cacc_g[3:4], cacc_v[3:4]], axis=1)
    w_up_t = w_up.T
    dh2 = _matmul(du_g, w_up_t[:D_FF], name="ffn_up_dx_gate")
    dh2 = _matmul(du_v, w_up_t[D_FF:], name="ffn_up_dx_val", add=dh2)
    grads["w_ffn_up"] = jnp.concatenate([_matmul(h2, du_g, name="ffn_up_dw_gate", trans_a=True),
                                         _matmul(h2, du_v, name="ffn_up_dw_val", trans_a=True)], axis=1)
    tok = hooks.early_grads(("w_ffn_down", "w_ffn_up", "conv_w"), grads)
    dx1, grads["g_pre_ffn"] = _rms_bwd(x1, _after(w["g_pre_ffn"], tok), dh2, name="rms_pre_ffn_bwd",
                                       out_dtype=F32, add=dy)
    dz1, grads["g_post_mix"] = _rms_bwd(z1, w["g_post_mix"], dx1, name="rms_post_mix_bwd", out_dtype=BF16)
    dmerged = _matmul(dz1, w_o.T, name="out_proj_dx")
    grads["w_o"] = _matmul(merged, dz1, name="out_proj_dw", trans_a=True)
    db_mla, db_dil, db_mem, dgp, grads["b_gate"] = _gate_bwd(
        dmerged, p_g, w["b_gate"], (b_mla, b_dil, b_mem), name="gate_bwd")

    do_mla = _matmul(db_mla, wb_mla.T, name="br_mla_dx", out_dtype=BF16)
    g_wb_mla = _matmul(o_mla, db_mla, name="br_mla_dw", trans_a=True)
    grads["w_br_mla"] = g_wb_mla.reshape(MLA_HEADS, LANES, D_MODEL)[:, :MLA_V].reshape(MLA_HEADS * MLA_V, D_MODEL)
    delta_mla = _row_dot(do_mla, o_mla, name="mla_delta")
    dq_f, dk_f, dv_f = _causal_bwd(q_f, k_f, v_f, do_mla, lse_mla, delta_mla, name="mla_bwd", heads=MLA_HEADS)
    dq_raw = _rope_bwd(dq_f, tabs, name="rope_q_bwd", scale=mla_scale, with_add=False)
    dk_raw, dkr = _rope_bwd(dk_f, tabs, name="rope_k_bwd", scale=1.0, with_add=True)
    dqn = _matmul(dq_raw, wq.T, name="mla_q_up_dx")
    g_wq = _matmul(qn, dq_raw, name="mla_q_up_dw", trans_a=True)
    grads["w_uq"] = g_wq.reshape(MLA_Q_RANK, MLA_HEADS, LANES)[:, :, :MLA_QK_DIM].reshape(MLA_Q_RANK, -1)
    dkvn = _matmul(dk_raw, wk.T, name="mla_k_up_dx")
    dkvn = _matmul(dv_f, wv.T, name="mla_v_up_dx", add=dkvn)
    g_wk = _matmul(kvn, dk_raw, name="mla_k_up_dw", trans_a=True).reshape(MLA_KV_RANK, MLA_HEADS, LANES)
    g_wv = _matmul(kvn, dv_f, name="mla_v_up_dw", trans_a=True).reshape(MLA_KV_RANK, MLA_HEADS, LANES)
    grads["w_ukv"] = jnp.concatenate([g_wk[:, :, :MLA_NOPE], g_wv[:, :, :MLA_V]], axis=2).reshape(MLA_KV_RANK, -1)
    dc_q, grads["mla_q_norm"] = _rms_bwd(c_q, w["mla_q_norm"], dqn, name="rms_q_bwd", out_dtype=BF16)
    dc_kv, grads["mla_kv_norm"] = _rms_bwd(c_kv, w["mla_kv_norm"], dkvn, name="rms_kv_bwd", out_dtype=BF16)

    dy_dil = _matmul(db_dil, wb_dil.T, name="br_dil_dx")
    grads["w_br_dil"] = _matmul(y_dil, db_dil, name="br_dil_dw", trans_a=True)
    mix = _mix_bwd(dy_dil, dil_o, dil_lse, name="dil_mix_bwd")
    d_dil = [[None] * 3 for _ in range(3)]
    for g, (_, d) in enumerate(DIL_PAIRS):
        arrs, offs = dil_in[g]
        do_g, dl_g, lse_g = mix[g], mix[3 + g], dil_lse[g]
        if d != 1:
            do_g, dl_g, lse_g = _to_residues(do_g, d), _to_residues(dl_g, d), _to_residues(lse_g, d)
        dq_g, dk_g, dv_g = _band_bwd(*arrs, do_g, lse_g, dl_g, slopes[g], name=f"dil_bwd_{g}", dilation=d,
                                     qoff=offs[0], koff=offs[1], voff=offs[2])
        for t, a in enumerate((dq_g, dk_g, dv_g)):
            d_dil[t][g] = a if d == 1 else _from_residues(a, d)

    do_mem = _matmul(db_mem, wb_mem.T, name="br_mem_dx", out_dtype=BF16)
    grads["w_br_mem"] = _matmul(o_mem, db_mem, name="br_mem_dw", trans_a=True)
    delta_mem = _row_dot(do_mem, o_mem, name="mem_delta")
    dq_mem = _flash_bwd_dq(p_dm, kv_m, kv_m, do_mem, lse_mem, delta_mem, name="mem_bwd_dq", out_dtype=BF16, **memat)
    dk_mem, dv_mem = _flash_bwd_dkv(p_dm, kv_m, kv_m, do_mem, lse_mem, delta_mem, name="mem_bwd_dkv",
                                    dk_dtype=BF16, dv_dtype=BF16, **memat)
    dkv_m = jnp.concatenate([dk_mem, dv_mem], axis=1)
    dmemn = _matmul(dkv_m, w_mkv.T, name="mem_kv_dx")
    grads["w_mem_kv"] = _matmul(memn, dkv_m, name="mem_kv_dw", trans_a=True)
    _, grads["g_mem"] = _rms_bwd(mem, w["g_mem"], dmemn, name="rms_mem_bwd", out_dtype=BF16)

    tok = hooks.early_grads(("w_o", "w_br_mla", "w_br_dil", "w_br_mem", "w_uq", "w_ukv", "w_mem_kv"), grads)
    if tok is not None:
        dkr = dkr + tok[0:1, 0:1]
    dp_all = jnp.concatenate([dc_q, dc_kv, bf(dkr)] + d_dil[0] + d_dil[1] + d_dil[2] + [dq_mem, dgp], axis=1)
    g_in = _matmul(h, dp_all, name="proj_dw", trans_a=True)
    grads["w_in"] = jnp.concatenate(
        [g_in[:, :OFF_KV], g_in[:, OFF_KV + MLA_NOPE:OFF_KV + MLA_QK_DIM], g_in[:, N_A:]], axis=1)
    tok = hooks.early_grads(("w_in",), grads)
    dh = _matmul(dp_all, w_in_t, name="proj_dx", after=tok)
    dx, grads["g_pre_mix"] = _rms_bwd(x, w["g_pre_mix"], dh, name="rms_pre_mix_bwd", out_dtype=F32, add=dx1)
    return loss, dx, grads


def kernel(x, mem, positions, g_pre_mix, w_in, b_gate, mla_q_norm, w_uq, mla_kv_norm, w_ukv, g_mem, w_mem_kv, w_br_mla, w_br_dil, w_br_mem, w_o, g_post_mix, g_pre_ffn, w_ffn_up, conv_w, conv_b, w_ffn_down, g_post_ffn, loss_target, m_g_pre_mix, m_w_in, m_b_gate, m_mla_q_norm, m_w_uq, m_mla_kv_norm, m_w_ukv, m_g_mem, m_w_mem_kv, m_w_br_mla, m_w_br_dil, m_w_br_mem, m_w_o, m_g_post_mix, m_g_pre_ffn, m_w_ffn_up, m_conv_w, m_conv_b, m_w_ffn_down, m_g_post_ffn, v_g_pre_mix, v_w_in, v_b_gate, v_mla_q_norm, v_w_uq, v_mla_kv_norm, v_w_ukv, v_g_mem, v_w_mem_kv, v_w_br_mla, v_w_br_dil, v_w_br_mem, v_w_o, v_g_post_mix, v_g_pre_ffn, v_w_ffn_up, v_conv_w, v_conv_b, v_w_ffn_down, v_g_post_ffn):
    local = dict(g_pre_mix=g_pre_mix, w_in=w_in, b_gate=b_gate, mla_q_norm=mla_q_norm, w_uq=w_uq,
                 mla_kv_norm=mla_kv_norm, w_ukv=w_ukv, g_mem=g_mem, w_mem_kv=w_mem_kv, w_br_mla=w_br_mla,
                 w_br_dil=w_br_dil, w_br_mem=w_br_mem, w_o=w_o, g_post_mix=g_post_mix, g_pre_ffn=g_pre_ffn,
                 w_ffn_up=w_ffn_up, conv_w=conv_w, conv_b=conv_b, w_ffn_down=w_ffn_down, g_post_ffn=g_post_ffn)
    mom_m = dict(g_pre_mix=m_g_pre_mix, w_in=m_w_in, b_gate=m_b_gate, mla_q_norm=m_mla_q_norm, w_uq=m_w_uq,
                 mla_kv_norm=m_mla_kv_norm, w_ukv=m_w_ukv, g_mem=m_g_mem, w_mem_kv=m_w_mem_kv, w_br_mla=m_w_br_mla,
                 w_br_dil=m_w_br_dil, w_br_mem=m_w_br_mem, w_o=m_w_o, g_post_mix=m_g_post_mix,
                 g_pre_ffn=m_g_pre_ffn, w_ffn_up=m_w_ffn_up, conv_w=m_conv_w, conv_b=m_conv_b,
                 w_ffn_down=m_w_ffn_down, g_post_ffn=m_g_post_ffn)
    mom_v = dict(g_pre_mix=v_g_pre_mix, w_in=v_w_in, b_gate=v_b_gate, mla_q_norm=v_mla_q_norm, w_uq=v_w_uq,
                 mla_kv_norm=v_mla_kv_norm, w_ukv=v_w_ukv, g_mem=v_g_mem, w_mem_kv=v_w_mem_kv, w_br_mla=v_w_br_mla,
                 w_br_dil=v_w_br_dil, w_br_mem=v_w_br_mem, w_o=v_w_o, g_post_mix=v_g_post_mix,
                 g_pre_ffn=v_g_pre_ffn, w_ffn_up=v_w_ffn_up, conv_w=v_conv_w, conv_b=v_conv_b,
                 w_ffn_down=v_w_ffn_down, g_post_ffn=v_g_post_ffn)

    me = 4 * lax.axis_index("x") + 2 * lax.axis_index("y") + lax.axis_index("c")
    spec = {name: (shape, axis) for name, shape, axis in SHARDED}
    wire = lambda name: F32 if name == "conv_w" else BF16
    shard = {name: local[name][0].astype(wire(name)) for name in spec}
    slab = lambda name, grads: _split_shards(grads[name].astype(wire(name)), *spec[name])

    w_in_all = _exchange([shard["w_in"]], name="gather_w_in", gather=True)[0]
    late = tuple(name for name in spec if name != "w_in")
    late_handle, late_token = _exchange_start([shard[n] for n in late], w_in_all, name="gather_rest_start",
                                              gather=True)
    full = {"w_in": _join_shards(w_in_all, *spec["w_in"])}
    for name, _, _ in REPLICATED:
        full[name] = local[name].reshape(1, -1)

    pending = []

    class Overlap:
        start_token = late_token

        def late_weights(self, w, after):
            landed = _exchange_wait(late_handle, after, name="gather_rest_wait", gather=True)
            w = dict(w)
            for name, buf in zip(late, landed):
                w[name] = _join_shards(_with_own(buf, shard[name], me), *spec[name])
            return w

        def early_grads(self, names, grads):
            slabs = [slab(name, grads) for name in names]
            handle, token = _exchange_start(slabs, slabs[0], name="grads_start_" + names[0], gather=False)
            pending.append((names, slabs, handle))
            return token

    loss, dx, grads = _local_step(x[0], mem[0], positions[0], loss_target[0], full, Overlap())

    parts = {}
    for names, slabs, handle in pending:
        landed = _exchange_wait(handle, dx, name="grads_wait_" + names[0], gather=False)
        for name, own, buf in zip(names, slabs, landed):
            parts[name] = _with_own(buf, lax.dynamic_index_in_dim(own, me, 0, keepdims=False), me)
    rep_parts = _exchange([_pack_replicated(grads)], name="gather_replicated_grads", gather=True)[0]

    results = {}
    for name in spec:
        res = _adamw(parts[name], local[name][0], mom_m[name][0], mom_v[name][0], name="adamw_" + name)
        results[name] = [r[None] for r in res]
    rep = _adamw(rep_parts, _pack_replicated(local), _pack_replicated(mom_m), _pack_replicated(mom_v),
                 name="adamw_replicated")
    for i, buf in enumerate(rep):
        for name, val in _unpack_replicated(buf).items():
            results.setdefault(name, [None] * 4)[i] = val

    loss = lax.psum(loss, ("x", "y", "c"))
    outs = [loss, dx[None]]
    for i in range(4):
        outs.extend(results[name][i] for name in PARAM_NAMES)
    return tuple(outs)
```

```python
import functools

import numpy as np
import jax
import jax.numpy as jnp
from jax import lax
from jax.experimental import pallas as pl
from jax.experimental.pallas import tpu as pltpu

F32 = jnp.float32
BF16 = jnp.bfloat16

N_DEV = 8
D_MODEL = 1024
RMS_EPS = 1e-6
NEG_INF = -1e30
LANES = 128
SUBLANES = 8
BLOCK = 128

MLA_HEADS = 8
MLA_NOPE = 64
MLA_ROPE = 32
MLA_V = 64
MLA_QK_DIM = MLA_NOPE + MLA_ROPE
MLA_Q_RANK = 384
MLA_KV_RANK = 256
ROPE_THETA = 10000.0
DIL_PAIRS = ((128, 1), (512, 4), (2048, 16))
DIL_HEADS_PER_GROUP = 4
DIL_HEADS = 12
MEM_HEADS = 4
D_FF = 2816
OFF_Q = MLA_Q_RANK
OFF_KV = OFF_Q + MLA_KV_RANK
OFF_KR = OFF_KV + MLA_ROPE
OFF_DIL = OFF_KR + 3 * DIL_HEADS * LANES
OFF_MEMQ = OFF_DIL + MEM_HEADS * LANES
D_IN = OFF_MEMQ + 3 * D_MODEL
N_A = OFF_KV + LANES
N_DM = 3 * DIL_HEADS * LANES + MEM_HEADS * LANES

ADAM_LR = 0.001
ADAM_B1 = 0.9
ADAM_B2 = 0.999
ADAM_EPS = 1e-08
ADAM_WD = 0.01
ADAM_STEP = 10

VMEM_LIMIT = 48 * 1024 * 1024
VMEM_LIMIT_BIG = 58 * 1024 * 1024
PACK_W = 1024

_ALIBI_BASE = np.exp2(-8.0 * np.arange(1, DIL_HEADS + 1) / DIL_HEADS)
DIL_SLOPES = [[float(_ALIBI_BASE[hh * 3 + g]) for hh in range(DIL_HEADS_PER_GROUP)] for g in range(3)]

PARAMS = (
    ("g_pre_mix", (1024,), None), ("w_in", (1024, D_IN), 1), ("b_gate", (3072,), None),
    ("mla_q_norm", (384,), None), ("w_uq", (384, 768), 1), ("mla_kv_norm", (256,), None),
    ("w_ukv", (256, 1024), 1), ("g_mem", (1024,), None), ("w_mem_kv", (1024, 1024), 0),
    ("w_br_mla", (512, 1024), 1), ("w_br_dil", (512, 1024), 1), ("w_br_mem", (512, 1024), 1),
    ("w_o", (1024, 1024), 0), ("g_post_mix", (1024,), None), ("g_pre_ffn", (1024,), None),
    ("w_ffn_up", (1024, 2 * D_FF), 1), ("conv_w", (3, 2 * D_FF), 1), ("conv_b", (2 * D_FF,), None),
    ("w_ffn_down", (D_FF, 1024), 0), ("g_post_ffn", (1024,), None),
)
PARAM_NAMES = tuple(p[0] for p in PARAMS)


def _shard_shape(shape, axis):
    if axis is None:
        return shape
    return tuple(s // N_DEV if a == axis else s for a, s in enumerate(shape))


SHARDED = tuple(p for p in PARAMS if p[2] is not None)
REPLICATED = tuple(p for p in PARAMS if p[2] is None)


def _layout():
    off, table = 0, {}
    for name, shape, _ in REPLICATED:
        table[name] = (off, shape[0])
        off += shape[0]
    rows = -(-off // PACK_W)
    rows = -(-rows // SUBLANES) * SUBLANES
    return table, off, rows


PACK_TABLE, PACK_USED, PACK_ROWS = _layout()


def _pick(n, cap):
    best = None
    for t in range(LANES, min(n, cap) + 1, LANES):
        if n % t == 0:
            best = t
    return best if best is not None else n


def _rows(n, cap, mult=SUBLANES):
    best = None
    for t in range(mult, min(n, cap) + 1, mult):
        if n % t == 0:
            best = t
    return best if best is not None else n


def _cparams(sem, vmem=VMEM_LIMIT):
    return pltpu.CompilerParams(dimension_semantics=sem, vmem_limit_bytes=vmem)


def _matmul(a, b, *, name, out_dtype=F32, trans_a=False, add=None, after=None, tm=1024, tn=1408, tk=640):
    if trans_a:
        kc, m = a.shape
    else:
        m, kc = a.shape
    n = b.shape[1]
    assert b.shape[0] == kc
    tm, tn, tk = _pick(m, tm), _pick(n, tn), _pick(kc, tk)
    nk = kc // tk

    def body(*refs):
        a_ref, b_ref = refs[:2]
        c_ref = refs[2] if add is not None else None
        o_ref, acc = refs[-2:]
        k = pl.program_id(2)

        @pl.when(k == 0)
        def _():
            if add is None:
                acc[...] = jnp.zeros_like(acc)
            else:
                acc[...] = c_ref[...].astype(F32)

        av = a_ref[...].astype(BF16)
        bv = b_ref[...].astype(BF16)
        if trans_a:
            acc[...] += lax.dot_general(av, bv, (((0,), (0,)), ((), ())), preferred_element_type=F32)
        else:
            acc[...] += jnp.dot(av, bv, preferred_element_type=F32)

        @pl.when(k == nk - 1)
        def _():
            o_ref[...] = acc[...].astype(out_dtype)

    if trans_a:
        a_spec = pl.BlockSpec((tk, tm), lambda i, j, k: (k, i))
    else:
        a_spec = pl.BlockSpec((tm, tk), lambda i, j, k: (i, k))
    in_specs = [a_spec, pl.BlockSpec((tk, tn), lambda i, j, k: (k, j))]
    args = [a, b]
    if add is not None:
        in_specs.append(pl.BlockSpec((tm, tn), lambda i, j, k: (i, j)))
        args.append(add)
    if after is not None:
        in_specs.append(pl.BlockSpec(memory_space=pl.ANY))
        args.append(after)
    return pl.pallas_call(
        body, name=name, grid=(m // tm, n // tn, nk),
        in_specs=in_specs, out_specs=pl.BlockSpec((tm, tn), lambda i, j, k: (i, j)),
        out_shape=jax.ShapeDtypeStruct((m, n), out_dtype),
        scratch_shapes=[pltpu.VMEM((tm, tn), F32)],
        compiler_params=_cparams(("parallel", "parallel", "arbitrary")),
    )(*args)


def _rms_fwd(x, g, *, name, out_dtype, add=None):
    s, n = x.shape
    ts = _rows(s, 512)

    def body(*refs):
        if add is None:
            x_ref, g_ref, o_ref = refs
        else:
            x_ref, g_ref, a_ref, o_ref = refs
        xv = x_ref[...]
        r = lax.rsqrt(jnp.mean(xv * xv, axis=-1, keepdims=True) + RMS_EPS)
        y = xv * r * g_ref[...]
        if add is not None:
            y = a_ref[...] + y
        o_ref[...] = y.astype(out_dtype)

    row = pl.BlockSpec((ts, n), lambda i: (i, 0))
    in_specs = [row, pl.BlockSpec((1, n), lambda i: (0, 0))]
    args = [x, g]
    if add is not None:
        in_specs.append(row)
        args.append(add)
    return pl.pallas_call(
        body, name=name, grid=(s // ts,), in_specs=in_specs, out_specs=row,
        out_shape=jax.ShapeDtypeStruct((s, n), out_dtype),
        compiler_params=_cparams(("parallel",)),
    )(*args)


def _rms_bwd(x, g, dy, *, name, out_dtype, add=None):
    s, n = x.shape
    ts = _rows(s, 512)

    def body(*refs):
        if add is None:
            x_ref, g_ref, dy_ref, dx_ref, dg_ref = refs
        else:
            x_ref, g_ref, dy_ref, a_ref, dx_ref, dg_ref = refs
        i = pl.program_id(0)
        xv = x_ref[...]
        dyv = dy_ref[...].astype(F32)
        r = lax.rsqrt(jnp.mean(xv * xv, axis=-1, keepdims=True) + RMS_EPS)
        nx = xv * r
        gdy = dyv * g_ref[...]
        dx = r * (gdy - nx * jnp.mean(nx * gdy, axis=-1, keepdims=True))
        if add is not None:
            dx = a_ref[...] + dx
        dx_ref[...] = dx.astype(out_dtype)

        @pl.when(i == 0)
        def _():
            dg_ref[...] = jnp.zeros_like(dg_ref)

        dg_ref[...] += jnp.sum(dyv * nx, axis=0, keepdims=True)

    row = pl.BlockSpec((ts, n), lambda i: (i, 0))
    vec = pl.BlockSpec((1, n), lambda i: (0, 0))
    in_specs = [row, vec, row]
    args = [x, g, dy]
    if add is not None:
        in_specs.append(row)
        args.append(add)
    return pl.pallas_call(
        body, name=name, grid=(s // ts,), in_specs=in_specs, out_specs=[row, vec],
        out_shape=[jax.ShapeDtypeStruct((s, n), out_dtype), jax.ShapeDtypeStruct((1, n), F32)],
        compiler_params=_cparams(("arbitrary",)),
    )(*args)


def _loss_step(x1, f, target, g, *, name):
    s, n = x1.shape
    ts = _rows(s, 512)

    def body(x_ref, f_ref, t_ref, g_ref, dy_ref, df_ref, dg_ref, sq_ref):
        i = pl.program_id(0)
        fv = f_ref[...]
        gv = g_ref[...]
        r = lax.rsqrt(jnp.mean(fv * fv, axis=-1, keepdims=True) + RMS_EPS)
        nx = fv * r
        err = x_ref[...] + nx * gv - t_ref[...]
        dy = err * (1.0 / n)
        dy_ref[...] = dy
        gdy = dy * gv
        df_ref[...] = (r * (gdy - nx * jnp.mean(nx * gdy, axis=-1, keepdims=True))).astype(BF16)

        @pl.when(i == 0)
        def _():
            sq_ref[...] = jnp.zeros_like(sq_ref)
            dg_ref[...] = jnp.zeros_like(dg_ref)

        sq_ref[...] += jnp.sum(err * err, axis=0, keepdims=True)
        dg_ref[...] += jnp.sum(dy * nx, axis=0, keepdims=True)

    row = pl.BlockSpec((ts, n), lambda i: (i, 0))
    vec = pl.BlockSpec((1, n), lambda i: (0, 0))
    return pl.pallas_call(
        body, name=name, grid=(s // ts,), in_specs=[row, row, row, vec], out_specs=[row, row, vec, vec],
        out_shape=[jax.ShapeDtypeStruct((s, n), F32), jax.ShapeDtypeStruct((s, n), BF16),
                   jax.ShapeDtypeStruct((1, n), F32), jax.ShapeDtypeStruct((1, n), F32)],
        compiler_params=_cparams(("arbitrary",)),
    )(x1, f, target, g)


def _rope_tables(positions):
    half = MLA_ROPE // 2
    inv_freq = ROPE_THETA ** (-jnp.arange(half, dtype=F32) / half)
    ang = positions.astype(F32)[:, None] * inv_freq[None, :]
    cos, sin = jnp.cos(ang), jnp.sin(ang)
    s = positions.shape[0]
    one = jnp.ones((s, MLA_NOPE), F32)
    zero = jnp.zeros((s, MLA_NOPE), F32)
    pad1 = jnp.ones((s, LANES - MLA_QK_DIM), F32)
    pad0 = jnp.zeros((s, LANES - MLA_QK_DIM), F32)
    zh = jnp.zeros((s, half), F32)
    c_tab = jnp.concatenate([one, cos, cos, pad1], axis=1)
    s1_tab = jnp.concatenate([zero, -sin, zh, pad0], axis=1)
    s2_tab = jnp.concatenate([zero, zh, sin, pad0], axis=1)
    return c_tab, s1_tab, s2_tab


def _rope_fwd(x, tabs, *, name, scale, add=None):
    s, n = x.shape
    nh = n // LANES
    ts = _rows(s, 512)
    half = MLA_ROPE // 2

    def body(*refs):
        if add is None:
            x_ref, c_ref, s1_ref, s2_ref, o_ref = refs
        else:
            x_ref, a_ref, c_ref, s1_ref, s2_ref, o_ref = refs
        c, s1, s2 = c_ref[...], s1_ref[...], s2_ref[...]
        for h in range(nh):
            xh = x_ref[:, h * LANES:(h + 1) * LANES]
            if add is not None:
                xh = xh + a_ref[...]
            y = xh * c + pltpu.roll(xh, LANES - half, 1) * s1 + pltpu.roll(xh, half, 1) * s2
            o_ref[:, h * LANES:(h + 1) * LANES] = (y * scale).astype(BF16)

    row = pl.BlockSpec((ts, n), lambda i: (i, 0))
    tab = pl.BlockSpec((ts, LANES), lambda i: (i, 0))
    in_specs = [row] + ([tab] if add is not None else []) + [tab, tab, tab]
    args = [x] + ([add] if add is not None else []) + list(tabs)
    return pl.pallas_call(
        body, name=name, grid=(s // ts,), in_specs=in_specs, out_specs=row,
        out_shape=jax.ShapeDtypeStruct((s, n), BF16),
        compiler_params=_cparams(("parallel",)),
    )(*args)


def _rope_bwd(dy, tabs, *, name, scale, with_add):
    s, n = dy.shape
    nh = n // LANES
    ts = _rows(s, 512)
    half = MLA_ROPE // 2

    def body(*refs):
        if with_add:
            dy_ref, c_ref, s1_ref, s2_ref, dx_ref, da_ref = refs
        else:
            dy_ref, c_ref, s1_ref, s2_ref, dx_ref = refs
        c, s1, s2 = c_ref[...], s1_ref[...], s2_ref[...]
        tot = None
        for h in range(nh):
            g = dy_ref[:, h * LANES:(h + 1) * LANES].astype(F32)
            dx = (g * c + pltpu.roll(g * s1, half, 1) + pltpu.roll(g * s2, LANES - half, 1)) * scale
            dx_ref[:, h * LANES:(h + 1) * LANES] = dx.astype(BF16)
            tot = dx if tot is None else tot + dx
        if with_add:
            da_ref[...] = tot

    row = pl.BlockSpec((ts, n), lambda i: (i, 0))
    tab = pl.BlockSpec((ts, LANES), lambda i: (i, 0))
    out_specs = [row, tab] if with_add else row
    out_shape = [jax.ShapeDtypeStruct((s, n), BF16)]
    if with_add:
        out_shape.append(jax.ShapeDtypeStruct((s, LANES), F32))
    else:
        out_shape = out_shape[0]
    return pl.pallas_call(
        body, name=name, grid=(s // ts,), in_specs=[row, tab, tab, tab], out_specs=out_specs,
        out_shape=out_shape, compiler_params=_cparams(("parallel",)),
    )(dy, *tabs)


def _scores(q, k, scale, diag):
    s = lax.dot_general(q, k, (((1,), (1,)), ((), ())), preferred_element_type=F32)
    if scale != 1.0:
        s = s * scale
    if diag:
        rows = lax.broadcasted_iota(jnp.int32, s.shape, 0)
        cols = lax.broadcasted_iota(jnp.int32, s.shape, 1)
        s = jnp.where(cols <= rows, s, NEG_INF)
    return s


def _flash_fwd(q, k, v, *, name, heads, qoff, koff, voff, causal, scale, tq, tk):
    s_q, s_kv = q.shape[0], k.shape[0]
    tq, tk = min(tq, s_q), min(tk, s_kv)
    nq, nk = s_q // tq, s_kv // tk
    if causal:
        assert tq == tk and s_q == s_kv

    def body(q_ref, k_ref, v_ref, o_ref, lse_ref, m_s, l_s, acc):
        i, j = pl.program_id(1), pl.program_id(2)

        @pl.when(j == 0)
        def _():
            m_s[...] = jnp.full_like(m_s, NEG_INF)
            l_s[...] = jnp.zeros_like(l_s)
            acc[...] = jnp.zeros_like(acc)

        def step(diag):
            s = _scores(q_ref[...], k_ref[...], scale, diag)
            m_prev = m_s[...]
            m_cur = jnp.maximum(m_prev, jnp.max(s, axis=1, keepdims=True))
            alpha = jnp.exp(m_prev - m_cur)
            p = jnp.exp(s - m_cur[:, :1])
            l_s[...] = alpha * l_s[...] + jnp.sum(p, axis=1, keepdims=True)
            acc[...] = alpha * acc[...] + jnp.dot(p.astype(BF16), v_ref[...], preferred_element_type=F32)
            m_s[...] = m_cur

        def finish():
            o_ref[...] = (acc[...] / l_s[...]).astype(o_ref.dtype)
            lse_ref[...] = m_s[...] + jnp.log(l_s[...])

        if causal:
            @pl.when(j < i)
            def _():
                step(False)

            @pl.when(j == i)
            def _():
                step(True)
                finish()
        else:
            step(False)

            @pl.when(j == nk - 1)
            def _():
                finish()

    def kv_idx(off):
        if causal:
            return lambda h, i, j: (jnp.minimum(j, i), off + h)
        return lambda h, i, j: (j, off + h)

    blk_q = pl.BlockSpec((tq, LANES), lambda h, i, j: (i, qoff + h))
    out_q = pl.BlockSpec((tq, LANES), lambda h, i, j: (i, h))
    return pl.pallas_call(
        body, name=name, grid=(heads, nq, nk),
        in_specs=[blk_q, pl.BlockSpec((tk, LANES), kv_idx(koff)), pl.BlockSpec((tk, LANES), kv_idx(voff))],
        out_specs=[out_q, out_q],
        out_shape=[jax.ShapeDtypeStruct((s_q, heads * LANES), BF16),
                   jax.ShapeDtypeStruct((s_q, heads * LANES), F32)],
        scratch_shapes=[pltpu.VMEM((tq, LANES), F32)] * 3,
        compiler_params=_cparams(("parallel", "parallel", "arbitrary")),
    )(q, k, v)


def _flash_bwd_dq(q, k, v, do, lse, delta, *, name, heads, qoff, koff, voff, causal, scale, tq, tk, out_dtype):
    s_q, s_kv = q.shape[0], k.shape[0]
    tq, tk = min(tq, s_q), min(tk, s_kv)
    nq, nk = s_q // tq, s_kv // tk

    def body(q_ref, k_ref, v_ref, do_ref, lse_ref, dl_ref, dq_ref, acc):
        i, j = pl.program_id(1), pl.program_id(2)

        @pl.when(j == 0)
        def _():
            acc[...] = jnp.zeros_like(acc)

        def step(diag):
            s = _scores(q_ref[...], k_ref[...], scale, diag)
            p = jnp.exp(s - lse_ref[:, :1])
            dp = lax.dot_general(do_ref[...], v_ref[...], (((1,), (1,)), ((), ())), preferred_element_type=F32)
            ds = p * (dp - dl_ref[:, :1])
            acc[...] += jnp.dot(ds.astype(BF16), k_ref[...], preferred_element_type=F32)

        def finish():
            dq_ref[...] = (acc[...] * scale).astype(out_dtype)

        if causal:
            @pl.when(j < i)
            def _():
                step(False)

            @pl.when(j == i)
            def _():
                step(True)
                finish()
        else:
            step(False)

            @pl.when(j == nk - 1)
            def _():
                finish()

    def kv_idx(off):
        if causal:
            return lambda h, i, j: (jnp.minimum(j, i), off + h)
        return lambda h, i, j: (j, off + h)

    blk_q = pl.BlockSpec((tq, LANES), lambda h, i, j: (i, qoff + h))
    blk_h = pl.BlockSpec((tq, LANES), lambda h, i, j: (i, h))
    return pl.pallas_call(
        body, name=name, grid=(heads, nq, nk),
        in_specs=[blk_q, pl.BlockSpec((tk, LANES), kv_idx(koff)), pl.BlockSpec((tk, LANES), kv_idx(voff)),
                  blk_h, blk_h, blk_h],
        out_specs=blk_h,
        out_shape=jax.ShapeDtypeStruct((s_q, heads * LANES), out_dtype),
        scratch_shapes=[pltpu.VMEM((tq, LANES), F32)],
        compiler_params=_cparams(("parallel", "parallel", "arbitrary")),
    )(q, k, v, do, lse, delta)


def _flash_bwd_dkv(q, k, v, do, lse, delta, *, name, heads, qoff, koff, voff, causal, scale, tq, tk,
                   dk_dtype, dv_dtype):
    s_q, s_kv = q.shape[0], k.shape[0]
    tq, tk = min(tq, s_q), min(tk, s_kv)
    nq, nk = s_q // tq, s_kv // tk

    def body(q_ref, k_ref, v_ref, do_ref, lse_ref, dl_ref, dk_ref, dv_ref, dk_acc, dv_acc):
        j, i = pl.program_id(1), pl.program_id(2)

        @pl.when(i == 0)
        def _():
            dk_acc[...] = jnp.zeros_like(dk_acc)
            dv_acc[...] = jnp.zeros_like(dv_acc)

        def step(diag):
            s = _scores(q_ref[...], k_ref[...], scale, diag)
            p = jnp.exp(s - lse_ref[:, :1])
            dov = do_ref[...]
            dp = lax.dot_general(dov, v_ref[...], (((1,), (1,)), ((), ())), preferred_element_type=F32)
            ds = p * (dp - dl_ref[:, :1])
            dv_acc[...] += lax.dot_general(p.astype(BF16), dov, (((0,), (0,)), ((), ())),
                                           preferred_element_type=F32)
            dk_acc[...] += lax.dot_general(ds.astype(BF16), q_ref[...], (((0,), (0,)), ((), ())),
                                           preferred_element_type=F32)

        if causal:
            @pl.when(i > j)
            def _():
                step(False)

            @pl.when(i == j)
            def _():
                step(True)
        else:
            step(False)

        @pl.when(i == nq - 1)
        def _():
            dk_ref[...] = (dk_acc[...] * scale).astype(dk_dtype)
            dv_ref[...] = dv_acc[...].astype(dv_dtype)

    def q_idx(off):
        if causal:
            return lambda h, j, i: (jnp.maximum(i, j), off + h)
        return lambda h, j, i: (i, off + h)

    blk_h = pl.BlockSpec((tq, LANES), q_idx(0))
    out_k = pl.BlockSpec((tk, LANES), lambda h, j, i: (j, h))
    return pl.pallas_call(
        body, name=name, grid=(heads, nk, nq),
        in_specs=[pl.BlockSpec((tq, LANES), q_idx(qoff)),
                  pl.BlockSpec((tk, LANES), lambda h, j, i: (j, koff + h)),
                  pl.BlockSpec((tk, LANES), lambda h, j, i: (j, voff + h)),
                  blk_h, blk_h, blk_h],
        out_specs=[out_k, out_k],
        out_shape=[jax.ShapeDtypeStruct((s_kv, heads * LANES), dk_dtype),
                   jax.ShapeDtypeStruct((s_kv, heads * LANES), dv_dtype)],
        scratch_shapes=[pltpu.VMEM((tk, LANES), F32)] * 2,
        compiler_params=_cparams(("parallel", "parallel", "arbitrary")),
    )(q, k, v, do, lse, delta)


def _row_dot(a, b, *, name):
    s, n = a.shape
    nh = n // LANES
    ts = _rows(s, 512)

    def body(a_ref, b_ref, o_ref):
        for h in range(nh):
            sl = slice(h * LANES, (h + 1) * LANES)
            d = jnp.sum(a_ref[:, sl].astype(F32) * b_ref[:, sl].astype(F32), axis=1, keepdims=True)
            o_ref[:, sl] = jnp.broadcast_to(d, (ts, LANES))

    row = pl.BlockSpec((ts, n), lambda i: (i, 0))
    return pl.pallas_call(
        body, name=name, grid=(s // ts,), in_specs=[row, row], out_specs=row,
        out_shape=jax.ShapeDtypeStruct((s, n), F32), compiler_params=_cparams(("parallel",)),
    )(a, b)


CAUSAL_T = 512
LOG2E = 1.4426950408889634
LN2 = 0.6931471805599453


def _causal_fwd(q, k, v, *, name, heads, ones_lane):
    s = q.shape[0]
    t = CAUSAL_T
    nq = s // (2 * t)
    assert nq * 2 * t == s

    def body(q_ref, k_ref, v_ref, o_ref, lse_ref, v1, m_s, acc):
        i = pl.program_id(1)

        @pl.when(i == 0)
        def _():
            lane = lax.broadcasted_iota(jnp.int32, v1.shape, 1)
            v1[...] = jnp.where(lane == ones_lane, 1.0, v_ref[...]).astype(BF16)

        m_s[...] = jnp.full_like(m_s, NEG_INF)
        acc[...] = jnp.zeros_like(acc)
        halves = (q_ref[0:t, :], q_ref[t:2 * t, :])

        def raw(c, j):
            rows = pl.ds(pl.multiple_of(j * t, t), t)
            return lax.dot_general(halves[c], k_ref[rows, :], (((1,), (1,)), ((), ())), preferred_element_type=F32)

        def update(c, sc, j):
            m_prev = m_s[c]
            m_cur = jnp.maximum(m_prev, jnp.max(sc, axis=1, keepdims=True))
            p = jnp.exp2(sc - m_cur[:, :1]).astype(BF16)
            acc[c] = jnp.exp2(m_prev - m_cur) * acc[c] + jnp.dot(
                p, v1[pl.ds(pl.multiple_of(j * t, t), t), :], preferred_element_type=F32)
            m_s[c] = m_cur

        def loop(j, carry):
            sa, sb = raw(0, j), raw(1, j)
            update(0, sa, j)
            update(1, sb, j)
            return carry

        lax.fori_loop(0, 2 * i, loop, 0)
        sa, sb = raw(0, 2 * i), raw(1, 2 * i)
        below = (lax.broadcasted_iota(jnp.int32, sa.shape, 1) <= lax.broadcasted_iota(jnp.int32, sa.shape, 0))
        update(0, jnp.where(below, sa, NEG_INF), 2 * i)
        update(1, sb, 2 * i)
        update(1, jnp.where(below, raw(1, 2 * i + 1), NEG_INF), 2 * i + 1)
        lane = lax.broadcasted_iota(jnp.int32, (t, LANES), 1)
        for c in range(2):
            out = acc[c]
            den = out[:, ones_lane:ones_lane + 1]
            o_ref[c * t:(c + 1) * t, :] = jnp.where(lane == ones_lane, 0.0, out / den).astype(BF16)
            lse_ref[c * t:(c + 1) * t, :] = m_s[c] + jnp.log2(den)

    blk = pl.BlockSpec((2 * t, LANES), lambda h, i: (i, h))
    full = pl.BlockSpec((s, LANES), lambda h, i: (0, h))
    return pl.pallas_call(
        body, name=name, grid=(heads, nq), in_specs=[blk, full, full], out_specs=[blk, blk],
        out_shape=[jax.ShapeDtypeStruct((s, heads * LANES), BF16), jax.ShapeDtypeStruct((s, heads * LANES), F32)],
        scratch_shapes=[pltpu.VMEM((s, LANES), BF16), pltpu.VMEM((2, t, LANES), F32),
                        pltpu.VMEM((2, t, LANES), F32)],
        compiler_params=_cparams(("parallel", "arbitrary")),
    )(q, k, v)


def _causal_bwd(q, k, v, do, lse, delta, *, name, heads):
    s = q.shape[0]
    t = min(CAUSAL_T, s)
    nt = s // t

    def body(q_ref, k_ref, v_ref, do_ref, lse_ref, dl_ref, dq_ref, dk_ref, dv_ref, dk_acc, dv_acc):
        j = pl.program_id(1)

        @pl.when(j == 0)
        def _():
            dq_ref[...] = jnp.zeros_like(dq_ref)

        dk_acc[...] = jnp.zeros_like(dk_acc)
        dv_acc[...] = jnp.zeros_like(dv_acc)
        kv, vv = k_ref[...], v_ref[...]

        def step(i, diag, size=t):
            rows = pl.ds(pl.multiple_of(i * t, t), size)
            qv, dov = q_ref[rows, :], do_ref[rows, :]
            sc = _scores(qv, kv, 1.0, diag)
            p = jnp.exp2(sc - lse_ref[rows, :][:, :1])
            dp = lax.dot_general(dov, vv, (((1,), (1,)), ((), ())), preferred_element_type=F32)
            ds = (p * (dp - dl_ref[rows, :][:, :1])).astype(BF16)
            dv_acc[...] += lax.dot_general(p.astype(BF16), dov, (((0,), (0,)), ((), ())),
                                           preferred_element_type=F32)
            dk_acc[...] += lax.dot_general(ds, qv, (((0,), (0,)), ((), ())), preferred_element_type=F32)
            dq_ref[rows, :] += jnp.dot(ds, kv, preferred_element_type=F32)

        step(j, True)
        odd = (nt - 1 - j) % 2

        @pl.when(odd == 1)
        def _():
            step(j + 1, False)

        def loop(n, carry):
            step(j + 1 + odd + 2 * n, False, 2 * t)
            return carry

        lax.fori_loop(0, (nt - 1 - j) // 2, loop, 0)
        dk_ref[...] = dk_acc[...] * LN2
        dv_ref[...] = dv_acc[...].astype(BF16)

    blk = pl.BlockSpec((t, LANES), lambda h, j: (j, h))
    full = pl.BlockSpec((s, LANES), lambda h, j: (0, h))
    return pl.pallas_call(
        body, name=name, grid=(heads, nt), in_specs=[full, blk, blk, full, full, full],
        out_specs=[full, blk, blk],
        out_shape=[jax.ShapeDtypeStruct((s, heads * LANES), F32), jax.ShapeDtypeStruct((s, heads * LANES), F32),
                   jax.ShapeDtypeStruct((s, heads * LANES), BF16)],
        scratch_shapes=[pltpu.VMEM((t, LANES), F32)] * 2,
        compiler_params=_cparams(("parallel", "arbitrary")),
    )(q, k, v, do, lse, delta)


def _band_masks(dilation, slope):
    qi = lax.broadcasted_iota(jnp.int32, (BLOCK, 2 * BLOCK), 0)
    kj = lax.broadcasted_iota(jnp.int32, (BLOCK, 2 * BLOCK), 1)
    dist = qi + BLOCK - kj
    valid = (dist >= 0) & (dist <= BLOCK)
    bias = -slope * (dist * dilation).astype(F32)
    return valid, bias


BAND_UNROLL = 4


def _aligned(start):
    return start if isinstance(start, int) else pl.multiple_of(start, BLOCK)


def _band_fwd(q, k, v, slopes, *, name, dilation, qoff, koff, voff):
    s = q.shape[0]
    sub = s // dilation
    nb = sub // BLOCK
    assert nb * BLOCK == sub
    unroll = min(BAND_UNROLL, nb)
    assert nb % unroll == 0
    scale = LANES ** -0.5

    def body(sl_ref, q_ref, k_ref, v_ref, o_ref, lse_ref):
        slope = sl_ref[pl.program_id(0)]
        valid2, bias2 = _band_masks(dilation, slope)
        valid1, bias1 = valid2[:, BLOCK:], bias2[:, BLOCK:]

        def block(start_q, kk, vv, valid, bias):
            qb = q_ref[pl.ds(start_q, BLOCK), :]
            sc = lax.dot_general(qb, kk, (((1,), (1,)), ((), ())), preferred_element_type=F32) * scale
            sc = jnp.where(valid, sc + bias, NEG_INF)
            m = jnp.max(sc, axis=1, keepdims=True)
            e = jnp.exp(sc - m)
            den = jnp.sum(e, axis=1, keepdims=True)
            p = (e / den).astype(BF16)
            o_ref[pl.ds(start_q, BLOCK), :] = jnp.dot(p, vv, preferred_element_type=F32)
            lse_ref[pl.ds(start_q, BLOCK), :] = jnp.broadcast_to(m + jnp.log(den), (BLOCK, LANES))

        block(0, k_ref[0:BLOCK, :], v_ref[0:BLOCK, :], valid1, bias1)

        def general(jj):
            start_q, start_k = _aligned(jj * BLOCK), _aligned((jj - 1) * BLOCK)
            block(start_q, k_ref[pl.ds(start_k, 2 * BLOCK), :], v_ref[pl.ds(start_k, 2 * BLOCK), :], valid2, bias2)

        for jj in range(1, unroll):
            general(jj)

        def loop(t, carry):
            for u in range(unroll):
                general(t * unroll + u)
            return carry

        lax.fori_loop(1, nb // unroll, loop, 0)

    def spec(off):
        return pl.BlockSpec((sub, LANES), lambda h, r: (r, off + h))

    out = pl.BlockSpec((sub, LANES), lambda h, r: (r, h))
    return pl.pallas_call(
        body, name=name, grid=(DIL_HEADS_PER_GROUP, dilation),
        in_specs=[pl.BlockSpec(memory_space=pltpu.SMEM), spec(qoff), spec(koff), spec(voff)],
        out_specs=[out, out],
        out_shape=[jax.ShapeDtypeStruct((s, DIL_HEADS_PER_GROUP * LANES), F32)] * 2,
        compiler_params=_cparams(("parallel", "parallel"), VMEM_LIMIT_BIG),
    )(slopes, q, k, v)


def _band_bwd(q, k, v, do, lse, delta, slopes, *, name, dilation, qoff, koff, voff):
    s = q.shape[0]
    sub = s // dilation
    nb = sub // BLOCK
    unroll = min(BAND_UNROLL, nb)
    scale = LANES ** -0.5

    def body(sl_ref, q_ref, k_ref, v_ref, do_ref, lse_ref, dl_ref, dq_ref, dk_ref, dv_ref):
        slope = sl_ref[pl.program_id(0)]
        valid2, bias2 = _band_masks(dilation, slope)
        valid1, bias1 = valid2[:, BLOCK:], bias2[:, BLOCK:]

        def block(start_q, kk, vv, valid, bias):
            qb = q_ref[pl.ds(start_q, BLOCK), :]
            dob = do_ref[pl.ds(start_q, BLOCK), :]
            sc = lax.dot_general(qb, kk, (((1,), (1,)), ((), ())), preferred_element_type=F32) * scale
            sc = jnp.where(valid, sc + bias, NEG_INF)
            p = jnp.exp(sc - lse_ref[pl.ds(start_q, BLOCK), :][:, :1])
            dp = lax.dot_general(dob, vv, (((1,), (1,)), ((), ())), preferred_element_type=F32)
            ds = (p * (dp - dl_ref[pl.ds(start_q, BLOCK), :][:, :1])).astype(BF16)
            dq = jnp.dot(ds, kk, preferred_element_type=F32) * scale
            dq_ref[pl.ds(start_q, BLOCK), :] = dq.astype(BF16)
            dkk = lax.dot_general(ds, qb, (((0,), (0,)), ((), ())), preferred_element_type=F32) * scale
            dvv = lax.dot_general(p.astype(BF16), dob, (((0,), (0,)), ((), ())), preferred_element_type=F32)
            return dkk, dvv

        carry0 = block(0, k_ref[0:BLOCK, :], v_ref[0:BLOCK, :], valid1, bias1)

        def general(jj, carry):
            dk_part, dv_part = carry
            start_q, start_k = _aligned(jj * BLOCK), _aligned((jj - 1) * BLOCK)
            dkk, dvv = block(start_q, k_ref[pl.ds(start_k, 2 * BLOCK), :], v_ref[pl.ds(start_k, 2 * BLOCK), :],
                             valid2, bias2)
            dk_ref[pl.ds(start_k, BLOCK), :] = (dk_part + dkk[:BLOCK]).astype(BF16)
            dv_ref[pl.ds(start_k, BLOCK), :] = (dv_part + dvv[:BLOCK]).astype(BF16)
            return dkk[BLOCK:], dvv[BLOCK:]

        for jj in range(1, unroll):
            carry0 = general(jj, carry0)

        def loop(t, carry):
            for u in range(unroll):
                carry = general(t * unroll + u, carry)
            return carry

        dk_last, dv_last = lax.fori_loop(1, nb // unroll, loop, carry0)
        dk_ref[(nb - 1) * BLOCK:nb * BLOCK, :] = dk_last.astype(BF16)
        dv_ref[(nb - 1) * BLOCK:nb * BLOCK, :] = dv_last.astype(BF16)

    def spec(off):
        return pl.BlockSpec((sub, LANES), lambda h, r: (r, off + h))

    out = spec(0)
    return pl.pallas_call(
        body, name=name, grid=(DIL_HEADS_PER_GROUP, dilation),
        in_specs=[pl.BlockSpec(memory_space=pltpu.SMEM), spec(qoff), spec(koff), spec(voff), out, out, out],
        out_specs=[out, out, out],
        out_shape=[jax.ShapeDtypeStruct((s, DIL_HEADS_PER_GROUP * LANES), BF16)] * 3,
        compiler_params=_cparams(("parallel", "parallel"), VMEM_LIMIT_BIG),
    )(slopes, q, k, v, do, lse, delta)


def _mix_fwd(outs, lses, *, name):
    s, n = outs[0].shape
    ts = _rows(s, 512)

    def body(o0, o1, o2, l0, l1, l2, y_ref):
        la, lb, lc = l0[...], l1[...], l2[...]
        m = jnp.maximum(jnp.maximum(la, lb), lc)
        ea, eb, ec = jnp.exp(la - m), jnp.exp(lb - m), jnp.exp(lc - m)
        den = ea + eb + ec
        y = (ea / den) * o0[...] + (eb / den) * o1[...] + (ec / den) * o2[...]
        y_ref[...] = y.astype(BF16)

    row = pl.BlockSpec((ts, n), lambda i: (i, 0))
    return pl.pallas_call(
        body, name=name, grid=(s // ts,), in_specs=[row] * 6, out_specs=row,
        out_shape=jax.ShapeDtypeStruct((s, n), BF16), compiler_params=_cparams(("parallel",)),
    )(*outs, *lses)


def _mix_bwd(dy, outs, lses, *, name):
    s, n = dy.shape
    nh = n // LANES
    ts = _rows(s, 256)

    def body(dy_ref, o0, o1, o2, l0, l1, l2, d0, d1, d2, e0, e1, e2):
        la, lb, lc = l0[...], l1[...], l2[...]
        m = jnp.maximum(jnp.maximum(la, lb), lc)
        ea, eb, ec = jnp.exp(la - m), jnp.exp(lb - m), jnp.exp(lc - m)
        den = ea + eb + ec
        wa, wb, wc = ea / den, eb / den, ec / den
        dyv = dy_ref[...]
        y = wa * o0[...] + wb * o1[...] + wc * o2[...]
        prod = dyv * y
        d0[...] = (wa * dyv).astype(BF16)
        d1[...] = (wb * dyv).astype(BF16)
        d2[...] = (wc * dyv).astype(BF16)
        for h in range(nh):
            sl = slice(h * LANES, (h + 1) * LANES)
            t = jnp.sum(prod[:, sl], axis=1, keepdims=True)
            e0[:, sl] = wa[:, sl] * t
            e1[:, sl] = wb[:, sl] * t
            e2[:, sl] = wc[:, sl] * t

    row = pl.BlockSpec((ts, n), lambda i: (i, 0))
    return pl.pallas_call(
        body, name=name, grid=(s // ts,), in_specs=[row] * 7, out_specs=[row] * 6,
        out_shape=[jax.ShapeDtypeStruct((s, n), BF16)] * 3 + [jax.ShapeDtypeStruct((s, n), F32)] * 3,
        compiler_params=_cparams(("parallel",)),
    )(dy, *outs, *lses)


def _gate_fwd(gp, b_gate, branches, *, name):
    s = gp.shape[0]
    ts = _rows(s, 256)

    def body(gp_ref, b_ref, b0, b1, b2, o_ref):
        tot = None
        for i, br in enumerate((b0, b1, b2)):
            sl = slice(i * D_MODEL, (i + 1) * D_MODEL)
            t = jax.nn.sigmoid(gp_ref[:, sl] + b_ref[:, sl]) * br[...]
            tot = t if tot is None else tot + t
        o_ref[...] = tot.astype(BF16)

    row = pl.BlockSpec((ts, D_MODEL), lambda i: (i, 0))
    return pl.pallas_call(
        body, name=name, grid=(s // ts,),
        in_specs=[pl.BlockSpec((ts, 3 * D_MODEL), lambda i: (i, 0)), pl.BlockSpec((1, 3 * D_MODEL), lambda i: (0, 0)),
                  row, row, row],
        out_specs=row, out_shape=jax.ShapeDtypeStruct((s, D_MODEL), BF16),
        compiler_params=_cparams(("parallel",)),
    )(gp, b_gate, *branches)


def _gate_bwd(dm, gp, b_gate, branches, *, name):
    s = gp.shape[0]
    ts = _rows(s, 256)

    def body(dm_ref, gp_ref, b_ref, b0, b1, b2, d0, d1, d2, dgp_ref, db_ref):
        i = pl.program_id(0)

        @pl.when(i == 0)
        def _():
            db_ref[...] = jnp.zeros_like(db_ref)

        dmv = dm_ref[...]
        for k, (br, dbr) in enumerate(((b0, d0), (b1, d1), (b2, d2))):
            sl = slice(k * D_MODEL, (k + 1) * D_MODEL)
            sg = jax.nn.sigmoid(gp_ref[:, sl] + b_ref[:, sl])
            dbr[...] = (dmv * sg).astype(BF16)
            dg = dmv * br[...] * sg * (1.0 - sg)
            dgp_ref[:, sl] = dg.astype(BF16)
            db_ref[:, sl] += jnp.sum(dg, axis=0, keepdims=True)

    row = pl.BlockSpec((ts, D_MODEL), lambda i: (i, 0))
    wide = pl.BlockSpec((ts, 3 * D_MODEL), lambda i: (i, 0))
    vec = pl.BlockSpec((1, 3 * D_MODEL), lambda i: (0, 0))
    return pl.pallas_call(
        body, name=name, grid=(s // ts,),
        in_specs=[row, wide, vec, row, row, row], out_specs=[row, row, row, wide, vec],
        out_shape=[jax.ShapeDtypeStruct((s, D_MODEL), BF16)] * 3
        + [jax.ShapeDtypeStruct((s, 3 * D_MODEL), BF16), jax.ShapeDtypeStruct((1, 3 * D_MODEL), F32)],
        compiler_params=_cparams(("arbitrary",)),
    )(dm, gp, b_gate, *branches)


CONV_TC = 1408


def _shift_down(x, halo, k):
    rolled = pltpu.roll(x, k, 0)
    r8 = lax.broadcasted_iota(jnp.int32, halo.shape, 0)
    top = jnp.where(r8 < k, pltpu.roll(halo, k, 0), rolled[:SUBLANES])
    return jnp.concatenate([top, rolled[SUBLANES:]], axis=0)


def _shift_up(x, halo, k):
    n = x.shape[0]
    rolled = pltpu.roll(x, n - k, 0)
    r8 = lax.broadcasted_iota(jnp.int32, halo.shape, 0)
    bot = jnp.where(r8 >= SUBLANES - k, pltpu.roll(halo, SUBLANES - k, 0), rolled[n - SUBLANES:])
    return jnp.concatenate([rolled[:n - SUBLANES], bot], axis=0)


def _conv_fwd(u, conv_w, conv_b, *, name):
    s = u.shape[0]
    ts = _rows(s, 256)
    nct = D_FF // CONV_TC
    per8 = ts // SUBLANES

    def body(ug, uv, hg, hv, wg, wv, bg, bv, zg_ref, zv_ref, a_ref):
        first = pl.program_id(1) == 0

        def conv(u_ref, h_ref, w_ref, b_ref):
            x = u_ref[...]
            halo = jnp.where(first, 0.0, h_ref[...])
            z = b_ref[...] + w_ref[0:1, :] * _shift_down(x, halo, 2)
            z = z + w_ref[1:2, :] * _shift_down(x, halo, 1)
            return z + w_ref[2:3, :] * x

        zg = conv(ug, hg, wg, bg)
        zv = conv(uv, hv, wv, bv)
        zg_ref[...] = zg
        zv_ref[...] = zv
        a_ref[...] = (zg * jax.nn.sigmoid(zg) * zv).astype(BF16)

    def col(off):
        return pl.BlockSpec((ts, CONV_TC), lambda c, i: (i, c + off))

    def halo(off):
        return pl.BlockSpec((SUBLANES, CONV_TC), lambda c, i: (jnp.maximum(i * per8 - 1, 0), c + off))

    def wspec(rows, off):
        return pl.BlockSpec((rows, CONV_TC), lambda c, i: (0, c + off))

    zg, zv, a = pl.pallas_call(
        body, name=name, grid=(nct, s // ts),
        in_specs=[col(0), col(nct), halo(0), halo(nct), wspec(3, 0), wspec(3, nct), wspec(1, 0), wspec(1, nct)],
        out_specs=[col(0), col(0), col(0)],
        out_shape=[jax.ShapeDtypeStruct((s, D_FF), F32)] * 2 + [jax.ShapeDtypeStruct((s, D_FF), BF16)],
        compiler_params=_cparams(("parallel", "parallel")),
    )(u, u, u, u, conv_w, conv_w, conv_b, conv_b)
    return zg, zv, a


def _conv_bwd(da, zg, zv, u, conv_w, *, name):
    s = da.shape[0]
    ts = _rows(s, 256)
    nct = D_FF // CONV_TC
    per8 = ts // SUBLANES
    nrow = s // ts
    last8 = s // SUBLANES - 1

    def dz_of(dav, g, val):
        sg = jax.nn.sigmoid(g)
        return dav * val * sg * (1.0 + g * (1.0 - sg)), dav * g * sg

    def body(da_ref, zg_ref, zv_ref, da_nx, zg_nx, zv_nx, ug_ref, uv_ref, ug_pv, uv_pv, wg_ref, wv_ref,
             dug_ref, duv_ref, accg_ref, accv_ref):
        i = pl.program_id(1)
        dzg, dzv = dz_of(da_ref[...], zg_ref[...], zv_ref[...])
        da_next = jnp.where(i == nrow - 1, 0.0, da_nx[...])
        nxg, nxv = dz_of(da_next, zg_nx[...], zv_nx[...])

        @pl.when(i == 0)
        def _():
            accg_ref[...] = jnp.zeros_like(accg_ref)
            accv_ref[...] = jnp.zeros_like(accv_ref)

        for dz, nxt, u_ref, pv_ref, w_ref, du_ref, acc_ref in (
                (dzg, nxg, ug_ref, ug_pv, wg_ref, dug_ref, accg_ref),
                (dzv, nxv, uv_ref, uv_pv, wv_ref, duv_ref, accv_ref)):
            du = w_ref[2:3, :] * dz + w_ref[1:2, :] * _shift_up(dz, nxt, 1) + w_ref[0:1, :] * _shift_up(dz, nxt, 2)
            du_ref[...] = du.astype(BF16)
            x = u_ref[...]
            prev = jnp.where(i == 0, 0.0, pv_ref[...])
            acc_ref[0:1, :] += jnp.sum(dz * _shift_down(x, prev, 2), axis=0, keepdims=True)
            acc_ref[1:2, :] += jnp.sum(dz * _shift_down(x, prev, 1), axis=0, keepdims=True)
            acc_ref[2:3, :] += jnp.sum(dz * x, axis=0, keepdims=True)
            acc_ref[3:4, :] += jnp.sum(dz, axis=0, keepdims=True)

    def blk(off):
        return pl.BlockSpec((ts, CONV_TC), lambda c, i: (i, c + off))

    def nxt8(off):
        return pl.BlockSpec((SUBLANES, CONV_TC), lambda c, i: (jnp.minimum((i + 1) * per8, last8), c + off))

    def prv8(off):
        return pl.BlockSpec((SUBLANES, CONV_TC), lambda c, i: (jnp.maximum(i * per8 - 1, 0), c + off))

    def wspec(off):
        return pl.BlockSpec((3, CONV_TC), lambda c, i: (0, c + off))

    acc = pl.BlockSpec((SUBLANES, CONV_TC), lambda c, i: (0, c))
    return pl.pallas_call(
        body, name=name, grid=(nct, nrow),
        in_specs=[blk(0), blk(0), blk(0), nxt8(0), nxt8(0), nxt8(0), blk(0), blk(nct), prv8(0), prv8(nct),
                  wspec(0), wspec(nct)],
        out_specs=[blk(0), blk(0), acc, acc],
        out_shape=[jax.ShapeDtypeStruct((s, D_FF), BF16)] * 2 + [jax.ShapeDtypeStruct((SUBLANES, D_FF), F32)] * 2,
        compiler_params=_cparams(("parallel", "arbitrary")),
    )(da, zg, zv, da, zg, zv, u, u, u, u, conv_w, conv_w)


def _peer(k):
    x, y, c = lax.axis_index("x"), lax.axis_index("y"), lax.axis_index("c")
    px = 1 - x if k & 4 else x
    py = 1 - y if k & 2 else y
    pc = 1 - c if k & 1 else c
    return (px, py, pc), 4 * px + 2 * py + pc


def _exchange(bufs, *, name, gather):
    n = len(bufs)
    npeer = N_DEV - 1

    def body(*refs):
        srcs, outs = refs[:n], refs[n:2 * n]
        send_sems, recv_sems, local_sems = refs[2 * n:]
        _, me = _peer(0)
        mine = [src if gather else src.at[me] for src in srcs]
        local = [pltpu.make_async_copy(mine[p], outs[p].at[me], local_sems.at[p]) for p in range(n)]
        for cp in local:
            cp.start()
        sends = []
        for k in range(1, N_DEV):
            dev, idx = _peer(k)
            for p in range(n):
                cp = pltpu.make_async_remote_copy(
                    src_ref=srcs[p] if gather else srcs[p].at[idx], dst_ref=outs[p].at[me],
                    send_sem=send_sems.at[p * npeer + k - 1], recv_sem=recv_sems.at[p * npeer + k - 1],
                    device_id=dev, device_id_type=pl.DeviceIdType.MESH)
                cp.start()
                sends.append(cp)
        for k in range(1, N_DEV):
            dev, idx = _peer(k)
            for p in range(n):
                pltpu.make_async_remote_copy(
                    src_ref=mine[p], dst_ref=outs[p].at[idx],
                    send_sem=send_sems.at[p * npeer + k - 1], recv_sem=recv_sems.at[p * npeer + k - 1],
                    device_id=dev, device_id_type=pl.DeviceIdType.MESH).wait_recv()
        for cp in sends:
            cp.wait_send()
        for cp in local:
            cp.wait()

    any_spec = pl.BlockSpec(memory_space=pl.ANY)
    return pl.pallas_call(
        body, name=name,
        in_specs=[any_spec] * n, out_specs=[any_spec] * n,
        out_shape=[jax.ShapeDtypeStruct((N_DEV,) + b.shape[-2:], b.dtype) for b in bufs],
        scratch_shapes=[pltpu.SemaphoreType.DMA((n * npeer,)), pltpu.SemaphoreType.DMA((n * npeer,)),
                        pltpu.SemaphoreType.DMA((n,))],
    )(*bufs)


def _gather_two_level(buf, *, name):
    def body(src, out, send_sems, recv_sems, local_sem):
        x, y, c = lax.axis_index("x"), lax.axis_index("y"), lax.axis_index("c")
        me, sibling = (x, y, c), (x, y, 1 - c)
        chips = [(1 - x, y), (x, 1 - y), (1 - x, 1 - y)]

        def slab(px, py, pc):
            return out.at[4 * px + 2 * py + pc]

        def copy(k, block, to, from_src=False):
            return pltpu.make_async_remote_copy(
                src_ref=src if from_src else slab(*block), dst_ref=slab(*block),
                send_sem=send_sems.at[k], recv_sem=recv_sems.at[k],
                device_id=to, device_id_type=pl.DeviceIdType.MESH)

        mine = pltpu.make_async_copy(src, slab(*me), local_sem)
        mine.start()
        first = [copy(0, me, sibling, True)] + [copy(1 + j, me, (*chip, c), True) for j, chip in enumerate(chips)]
        for cp in first:
            cp.start()
        passed = [copy(4 + j, (*chip, c), sibling) for j, chip in enumerate(chips)]
        for j, chip in enumerate(chips):
            copy(1 + j, (*chip, c), me).wait_recv()
            passed[j].start()
        copy(0, sibling, me).wait_recv()
        for j, chip in enumerate(chips):
            copy(4 + j, (*chip, 1 - c), me).wait_recv()
        for cp in first + passed:
            cp.wait_send()
        mine.wait()

    any_spec = pl.BlockSpec(memory_space=pl.ANY)
    return pl.pallas_call(
        body, name=name, in_specs=[any_spec], out_specs=any_spec,
        out_shape=jax.ShapeDtypeStruct((N_DEV,) + buf.shape, buf.dtype),
        scratch_shapes=[pltpu.SemaphoreType.DMA((N_DEV - 1,)), pltpu.SemaphoreType.DMA((N_DEV - 1,)),
                        pltpu.SemaphoreType.DMA],
    )(buf)


_HBM = pl.BlockSpec(memory_space=pltpu.HBM)
_SEM = pl.BlockSpec(memory_space=pltpu.SEMAPHORE)
_EFFECT = pltpu.SideEffectType.DATAFLOW_SIDE_EFFECTING


def _split_copy(srcs, lands, send_sems, recv_sems, gather, k, p):
    _, me = _peer(0)
    dev, idx = _peer(k)
    sem = p * (N_DEV - 1) + k - 1
    return pltpu.make_async_remote_copy(
        src_ref=srcs[p] if gather else srcs[p].at[idx], dst_ref=lands[p].at[me],
        send_sem=send_sems.at[sem], recv_sem=recv_sems.at[sem],
        device_id=dev, device_id_type=pl.DeviceIdType.MESH)


def _exchange_start(bufs, after, *, name, gather):
    n = len(bufs)
    nsem = n * (N_DEV - 1)

    def body(*refs):
        srcs, lands = refs[:n], refs[n:2 * n]
        send_sems, recv_sems = refs[2 * n + 1], refs[2 * n + 2]
        token = refs[-1]
        for k in range(1, N_DEV):
            for p in range(n):
                _split_copy(srcs, lands, send_sems, recv_sems, gather, k, p).start()
        token[...] = jnp.zeros_like(token)

    hbm = lambda a: pltpu.with_memory_space_constraint(a, pltpu.HBM)
    lands = [lax.empty((N_DEV,) + b.shape[-2:], b.dtype) for b in bufs]
    mem = [pltpu.HBM(b.shape, b.dtype) for b in bufs] + [pltpu.HBM(l.shape, l.dtype) for l in lands]
    outs = pl.pallas_call(
        body, name=name,
        in_specs=[_HBM] * (2 * n) + [pl.BlockSpec(memory_space=pl.ANY)],
        out_specs=[_SEM, _SEM] + [_HBM] * (2 * n) + [pl.BlockSpec(memory_space=pltpu.VMEM)],
        out_shape=[pltpu.SemaphoreType.DMA((nsem,)), pltpu.SemaphoreType.DMA((nsem,))] + mem
        + [jax.ShapeDtypeStruct((SUBLANES, LANES), F32)],
        input_output_aliases={p: 2 + p for p in range(2 * n)},
        compiler_params=pltpu.CompilerParams(has_side_effects=_EFFECT),
    )(*[hbm(b) for b in bufs], *[hbm(l) for l in lands], after)
    return (outs[0], outs[1], outs[2:2 + n], outs[2 + n:2 + 2 * n]), outs[-1]


def _exchange_wait(handle, after, *, name, gather):
    send_sems, recv_sems, srcs, lands = handle
    n = len(srcs)

    def body(*refs):
        src_refs, land_refs = refs[:n], refs[n:2 * n]
        send_ref, recv_ref = refs[2 * n], refs[2 * n + 1]
        for k in range(1, N_DEV):
            for p in range(n):
                cp = _split_copy(src_refs, land_refs, send_ref, recv_ref, gather, k, p)
                cp.wait_send()
                cp.wait_recv()

    mem = [pltpu.HBM(b.shape, b.dtype) for b in srcs] + [pltpu.HBM(l.shape, l.dtype) for l in lands]
    outs = pl.pallas_call(
        body, name=name,
        in_specs=[_HBM] * (2 * n) + [_SEM, _SEM, pl.BlockSpec(memory_space=pl.ANY)],
        out_specs=[_HBM] * (2 * n), out_shape=mem,
        input_output_aliases={p: p for p in range(2 * n)},
        compiler_params=pltpu.CompilerParams(has_side_effects=_EFFECT),
    )(*srcs, *lands, send_sems, recv_sems, after)
    return outs[n:]


def _with_own(landed, own, me):
    return lax.dynamic_update_slice(landed, own[None], (me, 0, 0))


def _adamw(parts, w, m, v, *, name):
    rows, width = w.shape
    tr = _rows(rows, max(16, (128 * 1024) // width), mult=16)

    def body(p_ref, w_ref, m_ref, v_ref, g_ref, d_ref, nm_ref, nv_ref):
        g = p_ref[0].astype(F32)
        for k in range(1, N_DEV):
            g = g + p_ref[k].astype(F32)
        mn = ADAM_B1 * m_ref[...] + (1.0 - ADAM_B1) * g
        vn = ADAM_B2 * v_ref[...] + (1.0 - ADAM_B2) * jnp.square(g)
        m_hat = mn / (1.0 - ADAM_B1 ** ADAM_STEP)
        v_hat = vn / (1.0 - ADAM_B2 ** ADAM_STEP)
        g_ref[...] = g
        d_ref[...] = -ADAM_LR * (m_hat / (jnp.sqrt(v_hat) + ADAM_EPS) + ADAM_WD * w_ref[...])
        nm_ref[...] = mn
        nv_ref[...] = vn

    row = pl.BlockSpec((tr, width), lambda i: (i, 0))
    return pl.pallas_call(
        body, name=name, grid=(rows // tr,),
        in_specs=[pl.BlockSpec((N_DEV, tr, width), lambda i: (0, i, 0)), row, row, row],
        out_specs=[row] * 4, out_shape=[jax.ShapeDtypeStruct((rows, width), F32)] * 4,
        compiler_params=_cparams(("parallel",)),
    )(parts, w, m, v)


def _pack_replicated(blocks):
    flat = jnp.concatenate([blocks[name].reshape(-1).astype(F32) for name, _, _ in REPLICATED])
    flat = jnp.pad(flat, (0, PACK_ROWS * PACK_W - PACK_USED))
    return flat.reshape(PACK_ROWS, PACK_W)


def _unpack_replicated(buf):
    flat = buf.reshape(-1)
    return {name: flat[off:off + n].reshape(1, n) for name, (off, n) in PACK_TABLE.items()}


def _join_shards(seg, shape, axis):
    return seg.reshape(shape) if axis == 0 else seg.transpose(1, 0, 2).reshape(shape)


def _split_shards(g, shape, axis):
    r, c = shape
    if axis == 0:
        return g.reshape(N_DEV, r // N_DEV, c)
    return g.reshape(r, N_DEV, c // N_DEV).transpose(1, 0, 2)


def _to_residues(a, d):
    s, c = a.shape
    return a.reshape(s // d, d, c).transpose(1, 0, 2).reshape(s, c)


def _from_residues(a, d):
    s, c = a.shape
    return a.reshape(d, s // d, c).transpose(1, 0, 2).reshape(s, c)


def _pad_heads(w, heads, width, lo, hi):
    r = w.shape[0]
    w = w.reshape(r, heads, width)[:, :, lo:hi]
    w = jnp.pad(w, ((0, 0), (0, 0), (0, LANES - (hi - lo))))
    return w.reshape(r, heads * LANES)


class _NoOverlap:
    start_token = None

    def late_weights(self, w, after):
        return w

    def early_grads(self, names, grads):
        return None


def _after(vec, token):
    return vec if token is None else vec + token[0:1, 0:1]


def _local_step(x, mem, positions, target, w, hooks=_NoOverlap()):
    s = x.shape[0]
    bf = lambda a: a.astype(BF16)

    w_in = w["w_in"]
    kr_cols = jnp.pad(w_in[:, OFF_KV:OFF_KR], ((0, 0), (MLA_NOPE, LANES - MLA_QK_DIM)))
    w_a = bf(jnp.concatenate([w_in[:, :OFF_KV], kr_cols], axis=1))
    w_dm = bf(w_in[:, OFF_KR:OFF_MEMQ])
    w_g = bf(w_in[:, OFF_MEMQ:])
    w_in_t = jnp.concatenate([w_a, w_dm, w_g], axis=1).T
    tabs = _rope_tables(positions)

    h = _rms_fwd(x, _after(w["g_pre_mix"], hooks.start_token), name="rms_pre_mix", out_dtype=BF16)
    p_a = _matmul(h, w_a, name="proj_a")
    p_dm = _matmul(h, w_dm, name="proj_dm", out_dtype=BF16)
    p_g = _matmul(h, w_g, name="proj_gate")
    c_q, c_kv, kr = p_a[:, :OFF_Q], p_a[:, OFF_Q:OFF_KV], p_a[:, OFF_KV:]

    w = hooks.late_weights(w, p_g)
    wq = bf(_pad_heads(w["w_uq"], MLA_HEADS, MLA_QK_DIM, 0, MLA_QK_DIM))
    wk = bf(_pad_heads(w["w_ukv"], MLA_HEADS, MLA_NOPE + MLA_V, 0, MLA_NOPE))
    wv = bf(_pad_heads(w["w_ukv"], MLA_HEADS, MLA_NOPE + MLA_V, MLA_NOPE, MLA_NOPE + MLA_V))
    w_mkv = bf(w["w_mem_kv"])
    wb_mla = bf(jnp.pad(w["w_br_mla"].reshape(MLA_HEADS, MLA_V, D_MODEL),
                        ((0, 0), (0, LANES - MLA_V), (0, 0))).reshape(MLA_HEADS * LANES, D_MODEL))
    wb_dil, wb_mem, w_o = bf(w["w_br_dil"]), bf(w["w_br_mem"]), bf(w["w_o"])
    w_up, w_down = bf(w["w_ffn_up"]), bf(w["w_ffn_down"])
    slopes = [jnp.asarray(sl, F32) for sl in DIL_SLOPES]
    mla_scale = MLA_QK_DIM ** -0.5
    mem_scale = LANES ** -0.5
    MQ = 3 * DIL_HEADS

    qn = _rms_fwd(c_q, w["mla_q_norm"], name="rms_q", out_dtype=BF16)
    kvn = _rms_fwd(c_kv, w["mla_kv_norm"], name="rms_kv", out_dtype=BF16)
    q_raw = _matmul(qn, wq, name="mla_q_up")
    k_raw = _matmul(kvn, wk, name="mla_k_up")
    v_f = _matmul(kvn, wv, name="mla_v_up", out_dtype=BF16)
    q_f = _rope_fwd(q_raw, tabs, name="rope_q", scale=mla_scale * LOG2E)
    k_f = _rope_fwd(k_raw, tabs, name="rope_k", scale=1.0, add=kr)
    o_mla, lse_mla = _causal_fwd(q_f, k_f, v_f, name="mla_fwd", heads=MLA_HEADS, ones_lane=MLA_V)

    dil_in, dil_o, dil_lse = [], [], []
    for g, (_, d) in enumerate(DIL_PAIRS):
        if d == 1:
            arrs, offs = (p_dm, p_dm, p_dm), (4 * g, DIL_HEADS + 4 * g, 2 * DIL_HEADS + 4 * g)
        else:
            arrs = tuple(_to_residues(p_dm[:, (t * DIL_HEADS + 4 * g) * LANES:(t * DIL_HEADS + 4 * g + 4) * LANES], d)
                         for t in range(3))
            offs = (0, 0, 0)
        o_g, lse_g = _band_fwd(*arrs, slopes[g], name=f"dil_fwd_{g}", dilation=d,
                               qoff=offs[0], koff=offs[1], voff=offs[2])
        dil_in.append((arrs, offs))
        dil_o.append(_from_residues(o_g, d))
        dil_lse.append(_from_residues(lse_g, d))
    y_dil = _mix_fwd(dil_o, dil_lse, name="dil_mix")

    memn = _rms_fwd(mem, w["g_mem"], name="rms_mem", out_dtype=BF16)
    kv_m = _matmul(memn, w_mkv, name="mem_kv", out_dtype=BF16)
    memat = dict(heads=MEM_HEADS, qoff=MQ, koff=0, voff=MEM_HEADS, causal=False, scale=mem_scale, tq=512, tk=256)
    o_mem, lse_mem = _flash_fwd(p_dm, kv_m, kv_m, name="mem_fwd", **memat)

    b_mla = _matmul(o_mla, wb_mla, name="br_mla")
    b_dil = _matmul(y_dil, wb_dil, name="br_dil")
    b_mem = _matmul(o_mem, wb_mem, name="br_mem")
    merged = _gate_fwd(p_g, w["b_gate"], (b_mla, b_dil, b_mem), name="gate_fwd")
    z1 = _matmul(merged, w_o, name="out_proj")
    x1 = _rms_fwd(z1, w["g_post_mix"], name="rms_post_mix", out_dtype=F32, add=x)
    h2 = _rms_fwd(x1, w["g_pre_ffn"], name="rms_pre_ffn", out_dtype=BF16)
    u = _matmul(h2, w_up, name="ffn_up")
    zg, zv, act = _conv_fwd(u, w["conv_w"], w["conv_b"], name="conv_fwd")
    f = _matmul(act, w_down, name="ffn_down")
    dy, df, g_post_ffn_grad, sq = _loss_step(x1, f, target, w["g_post_ffn"], name="loss")
    loss = 0.5 * jnp.sum(sq) / D_MODEL

    grads = {}
    grads["g_post_ffn"] = g_post_ffn_grad
    da = _matmul(df, w_down.T, name="ffn_down_dx")
    grads["w_ffn_down"] = _matmul(act, df, name="ffn_down_dw", trans_a=True)
    du_g, du_v, cacc_g, cacc_v = _conv_bwd(da, zg, zv, u, w["conv_w"], name="conv_bwd")
    grads["conv_w"] = jnp.concatenate([cacc_g[0:3], cacc_v[0:3]], axis=1)
    grads["conv_b"] = jnp.concatenate([cacc_g[3:4], cacc_v[3:4]], axis=1)
    w_up_t = w_up.T
    dh2 = _matmul(du_g, w_up_t[:D_FF], name="ffn_up_dx_gate")
    dh2 = _matmul(du_v, w_up_t[D_FF:], name="ffn_up_dx_val", add=dh2)
    grads["w_ffn_up"] = jnp.concatenate([_matmul(h2, du_g, name="ffn_up_dw_gate", trans_a=True),
                                         _matmul(h2, du_v, name="ffn_up_dw_val", trans_a=True)], axis=1)
    tok = hooks.early_grads(("w_ffn_down", "w_ffn_up", "conv_w"), grads)
    dx1, grads["g_pre_ffn"] = _rms_bwd(x1, _after(w["g_pre_ffn"], tok), dh2, name="rms_pre_ffn_bwd",
                                       out_dtype=F32, add=dy)
    dz1, grads["g_post_mix"] = _rms_bwd(z1, w["g_post_mix"], dx1, name="rms_post_mix_bwd", out_dtype=BF16)
    dmerged = _matmul(dz1, w_o.T, name="out_proj_dx")
    grads["w_o"] = _matmul(merged, dz1, name="out_proj_dw", trans_a=True)
    db_mla, db_dil, db_mem, dgp, grads["b_gate"] = _gate_bwd(
        dmerged, p_g, w["b_gate"], (b_mla, b_dil, b_mem), name="gate_bwd")

    do_mla = _matmul(db_mla, wb_mla.T, name="br_mla_dx", out_dtype=BF16)
    g_wb_mla = _matmul(o_mla, db_mla, name="br_mla_dw", trans_a=True)
    grads["w_br_mla"] = g_wb_mla.reshape(MLA_HEADS, LANES, D_MODEL)[:, :MLA_V].reshape(MLA_HEADS * MLA_V, D_MODEL)
    delta_mla = _row_dot(do_mla, o_mla, name="mla_delta")
    dq_f, dk_f, dv_f = _causal_bwd(q_f, k_f, v_f, do_mla, lse_mla, delta_mla, name="mla_bwd", heads=MLA_HEADS)
    dq_raw = _rope_bwd(dq_f, tabs, name="rope_q_bwd", scale=mla_scale, with_add=False)
    dk_raw, dkr = _rope_bwd(dk_f, tabs, name="rope_k_bwd", scale=1.0, with_add=True)
    dqn = _matmul(dq_raw, wq.T, name="mla_q_up_dx")
    g_wq = _matmul(qn, dq_raw, name="mla_q_up_dw", trans_a=True)
    grads["w_uq"] = g_wq.reshape(MLA_Q_RANK, MLA_HEADS, LANES)[:, :, :MLA_QK_DIM].reshape(MLA_Q_RANK, -1)
    dkvn = _matmul(dk_raw, wk.T, name="mla_k_up_dx")
    dkvn = _matmul(dv_f, wv.T, name="mla_v_up_dx", add=dkvn)
    g_wk = _matmul(kvn, dk_raw, name="mla_k_up_dw", trans_a=True).reshape(MLA_KV_RANK, MLA_HEADS, LANES)
    g_wv = _matmul(kvn, dv_f, name="mla_v_up_dw", trans_a=True).reshape(MLA_KV_RANK, MLA_HEADS, LANES)
    grads["w_ukv"] = jnp.concatenate([g_wk[:, :, :MLA_NOPE], g_wv[:, :, :MLA_V]], axis=2).reshape(MLA_KV_RANK, -1)
    dc_q, grads["mla_q_norm"] = _rms_bwd(c_q, w["mla_q_norm"], dqn, name="rms_q_bwd", out_dtype=BF16)
    dc_kv, grads["mla_kv_norm"] = _rms_bwd(c_kv, w["mla_kv_norm"], dkvn, name="rms_kv_bwd", out_dtype=BF16)

    dy_dil = _matmul(db_dil, wb_dil.T, name="br_dil_dx")
    grads["w_br_dil"] = _matmul(y_dil, db_dil, name="br_dil_dw", trans_a=True)
    mix = _mix_bwd(dy_dil, dil_o, dil_lse, name="dil_mix_bwd")
    d_dil = [[None] * 3 for _ in range(3)]
    for g, (_, d) in enumerate(DIL_PAIRS):
        arrs, offs = dil_in[g]
        do_g, dl_g, lse_g = mix[g], mix[3 + g], dil_lse[g]
        if d != 1:
            do_g, dl_g, lse_g = _to_residues(do_g, d), _to_residues(dl_g, d), _to_residues(lse_g, d)
        dq_g, dk_g, dv_g = _band_bwd(*arrs, do_g, lse_g, dl_g, slopes[g], name=f"dil_bwd_{g}", dilation=d,
                                     qoff=offs[0], koff=offs[1], voff=offs[2])
        for t, a in enumerate((dq_g, dk_g, dv_g)):
            d_dil[t][g] = a if d == 1 else _from_residues(a, d)

    do_mem = _matmul(db_mem, wb_mem.T, name="br_mem_dx", out_dtype=BF16)
    grads["w_br_mem"] = _matmul(o_mem, db_mem, name="br_mem_dw", trans_a=True)
    delta_mem = _row_dot(do_mem, o_mem, name="mem_delta")
    dq_mem = _flash_bwd_dq(p_dm, kv_m, kv_m, do_mem, lse_mem, delta_mem, name="mem_bwd_dq", out_dtype=BF16, **memat)
    dk_mem, dv_mem = _flash_bwd_dkv(p_dm, kv_m, kv_m, do_mem, lse_mem, delta_mem, name="mem_bwd_dkv",
                                    dk_dtype=BF16, dv_dtype=BF16, **memat)
    dkv_m = jnp.concatenate([dk_mem, dv_mem], axis=1)
    dmemn = _matmul(dkv_m, w_mkv.T, name="mem_kv_dx")
    grads["w_mem_kv"] = _matmul(memn, dkv_m, name="mem_kv_dw", trans_a=True)
    _, grads["g_mem"] = _rms_bwd(mem, w["g_mem"], dmemn, name="rms_mem_bwd", out_dtype=BF16)

    tok = hooks.early_grads(("w_o", "w_br_mla", "w_br_dil", "w_br_mem", "w_uq", "w_ukv", "w_mem_kv"), grads)
    if tok is not None:
        dkr = dkr + tok[0:1, 0:1]
    dp_all = jnp.concatenate([dc_q, dc_kv, bf(dkr)] + d_dil[0] + d_dil[1] + d_dil[2] + [dq_mem, dgp], axis=1)
    g_in = _matmul(h, dp_all, name="proj_dw", trans_a=True)
    grads["w_in"] = jnp.concatenate(
        [g_in[:, :OFF_KV], g_in[:, OFF_KV + MLA_NOPE:OFF_KV + MLA_QK_DIM], g_in[:, N_A:]], axis=1)
    tok = hooks.early_grads(("w_in",), grads)
    dh = _matmul(dp_all, w_in_t, name="proj_dx", after=tok)
    dx, grads["g_pre_mix"] = _rms_bwd(x, w["g_pre_mix"], dh, name="rms_pre_mix_bwd", out_dtype=F32, add=dx1)
    return loss, dx, grads


def kernel(x, mem, positions, g_pre_mix, w_in, b_gate, mla_q_norm, w_uq, mla_kv_norm, w_ukv, g_mem, w_mem_kv, w_br_mla, w_br_dil, w_br_mem, w_o, g_post_mix, g_pre_ffn, w_ffn_up, conv_w, conv_b, w_ffn_down, g_post_ffn, loss_target, m_g_pre_mix, m_w_in, m_b_gate, m_mla_q_norm, m_w_uq, m_mla_kv_norm, m_w_ukv, m_g_mem, m_w_mem_kv, m_w_br_mla, m_w_br_dil, m_w_br_mem, m_w_o, m_g_post_mix, m_g_pre_ffn, m_w_ffn_up, m_conv_w, m_conv_b, m_w_ffn_down, m_g_post_ffn, v_g_pre_mix, v_w_in, v_b_gate, v_mla_q_norm, v_w_uq, v_mla_kv_norm, v_w_ukv, v_g_mem, v_w_mem_kv, v_w_br_mla, v_w_br_dil, v_w_br_mem, v_w_o, v_g_post_mix, v_g_pre_ffn, v_w_ffn_up, v_conv_w, v_conv_b, v_w_ffn_down, v_g_post_ffn):
    local = dict(g_pre_mix=g_pre_mix, w_in=w_in, b_gate=b_gate, mla_q_norm=mla_q_norm, w_uq=w_uq,
                 mla_kv_norm=mla_kv_norm, w_ukv=w_ukv, g_mem=g_mem, w_mem_kv=w_mem_kv, w_br_mla=w_br_mla,
                 w_br_dil=w_br_dil, w_br_mem=w_br_mem, w_o=w_o, g_post_mix=g_post_mix, g_pre_ffn=g_pre_ffn,
                 w_ffn_up=w_ffn_up, conv_w=conv_w, conv_b=conv_b, w_ffn_down=w_ffn_down, g_post_ffn=g_post_ffn)
    mom_m = dict(g_pre_mix=m_g_pre_mix, w_in=m_w_in, b_gate=m_b_gate, mla_q_norm=m_mla_q_norm, w_uq=m_w_uq,
                 mla_kv_norm=m_mla_kv_norm, w_ukv=m_w_ukv, g_mem=m_g_mem, w_mem_kv=m_w_mem_kv, w_br_mla=m_w_br_mla,
                 w_br_dil=m_w_br_dil, w_br_mem=m_w_br_mem, w_o=m_w_o, g_post_mix=m_g_post_mix,
                 g_pre_ffn=m_g_pre_ffn, w_ffn_up=m_w_ffn_up, conv_w=m_conv_w, conv_b=m_conv_b,
                 w_ffn_down=m_w_ffn_down, g_post_ffn=m_g_post_ffn)
    mom_v = dict(g_pre_mix=v_g_pre_mix, w_in=v_w_in, b_gate=v_b_gate, mla_q_norm=v_mla_q_norm, w_uq=v_w_uq,
                 mla_kv_norm=v_mla_kv_norm, w_ukv=v_w_ukv, g_mem=v_g_mem, w_mem_kv=v_w_mem_kv, w_br_mla=v_w_br_mla,
                 w_br_dil=v_w_br_dil, w_br_mem=v_w_br_mem, w_o=v_w_o, g_post_mix=v_g_post_mix,
                 g_pre_ffn=v_g_pre_ffn, w_ffn_up=v_w_ffn_up, conv_w=v_conv_w, conv_b=v_conv_b,
                 w_ffn_down=v_w_ffn_down, g_post_ffn=v_g_post_ffn)

    me = 4 * lax.axis_index("x") + 2 * lax.axis_index("y") + lax.axis_index("c")
    spec = {name: (shape, axis) for name, shape, axis in SHARDED}
    wire = lambda name: F32 if name == "conv_w" else BF16
    shard = {name: local[name][0].astype(wire(name)) for name in spec}
    slab = lambda name, grads: _split_shards(grads[name].astype(wire(name)), *spec[name])

    w_in_all = _gather_two_level(shard["w_in"], name="gather_w_in")
    late = tuple(name for name in spec if name != "w_in")
    late_handle, late_token = _exchange_start([shard[n] for n in late], w_in_all, name="gather_rest_start",
                                              gather=True)
    full = {"w_in": _join_shards(w_in_all, *spec["w_in"])}
    for name, _, _ in REPLICATED:
        full[name] = local[name].reshape(1, -1)

    pending = []

    class Overlap:
        start_token = late_token

        def late_weights(self, w, after):
            landed = _exchange_wait(late_handle, after, name="gather_rest_wait", gather=True)
            w = dict(w)
            for name, buf in zip(late, landed):
                w[name] = _join_shards(_with_own(buf, shard[name], me), *spec[name])
            return w

        def early_grads(self, names, grads):
            slabs = [slab(name, grads) for name in names]
            handle, token = _exchange_start(slabs, slabs[0], name="grads_start_" + names[0], gather=False)
            pending.append((names, slabs, handle))
            return token

    loss, dx, grads = _local_step(x[0], mem[0], positions[0], loss_target[0], full, Overlap())

    parts = {}
    for names, slabs, handle in pending:
        landed = _exchange_wait(handle, dx, name="grads_wait_" + names[0], gather=False)
        for name, own, buf in zip(names, slabs, landed):
            parts[name] = _with_own(buf, lax.dynamic_index_in_dim(own, me, 0, keepdims=False), me)
    rep_parts = _exchange([_pack_replicated(grads)], name="gather_replicated_grads", gather=True)[0]

    results = {}
    for name in spec:
        res = _adamw(parts[name], local[name][0], mom_m[name][0], mom_v[name][0], name="adamw_" + name)
        results[name] = [r[None] for r in res]
    rep = _adamw(rep_parts, _pack_replicated(local), _pack_replicated(mom_m), _pack_replicated(mom_v),
                 name="adamw_replicated")
    for i, buf in enumerate(rep):
        for name, val in _unpack_replicated(buf).items():
            results.setdefault(name, [None] * 4)[i] = val

    loss = lax.psum(loss, ("x", "y", "c"))
    outs = [loss, dx[None]]
    for i in range(4):
        outs.extend(results[name][i] for name in PARAM_NAMES)
    return tuple(outs)
```

```python
import functools

import numpy as np
import jax
import jax.numpy as jnp
from jax import lax
from jax.experimental import pallas as pl
from jax.experimental.pallas import tpu as pltpu

F32 = jnp.float32
BF16 = jnp.bfloat16

N_DEV = 8
D_MODEL = 1024
RMS_EPS = 1e-6
NEG_INF = -1e30
LANES = 128
SUBLANES = 8
BLOCK = 128

MLA_HEADS = 8
MLA_NOPE = 64
MLA_ROPE = 32
MLA_V = 64
MLA_QK_DIM = MLA_NOPE + MLA_ROPE
MLA_Q_RANK = 384
MLA_KV_RANK = 256
ROPE_THETA = 10000.0
DIL_PAIRS = ((128, 1), (512, 4), (2048, 16))
DIL_HEADS_PER_GROUP = 4
DIL_HEADS = 12
MEM_HEADS = 4
D_FF = 2816
OFF_Q = MLA_Q_RANK
OFF_KV = OFF_Q + MLA_KV_RANK
OFF_KR = OFF_KV + MLA_ROPE
OFF_DIL = OFF_KR + 3 * DIL_HEADS * LANES
OFF_MEMQ = OFF_DIL + MEM_HEADS * LANES
D_IN = OFF_MEMQ + 3 * D_MODEL
N_A = OFF_KV + LANES
N_DM = 3 * DIL_HEADS * LANES + MEM_HEADS * LANES

ADAM_LR = 0.001
ADAM_B1 = 0.9
ADAM_B2 = 0.999
ADAM_EPS = 1e-08
ADAM_WD = 0.01
ADAM_STEP = 10

VMEM_LIMIT = 48 * 1024 * 1024
VMEM_LIMIT_BIG = 58 * 1024 * 1024
PACK_W = 1024

_ALIBI_BASE = np.exp2(-8.0 * np.arange(1, DIL_HEADS + 1) / DIL_HEADS)
DIL_SLOPES = [[float(_ALIBI_BASE[hh * 3 + g]) for hh in range(DIL_HEADS_PER_GROUP)] for g in range(3)]

PARAMS = (
    ("g_pre_mix", (1024,), None), ("w_in", (1024, D_IN), 1), ("b_gate", (3072,), None),
    ("mla_q_norm", (384,), None), ("w_uq", (384, 768), 1), ("mla_kv_norm", (256,), None),
    ("w_ukv", (256, 1024), 1), ("g_mem", (1024,), None), ("w_mem_kv", (1024, 1024), 0),
    ("w_br_mla", (512, 1024), 1), ("w_br_dil", (512, 1024), 1), ("w_br_mem", (512, 1024), 1),
    ("w_o", (1024, 1024), 0), ("g_post_mix", (1024,), None), ("g_pre_ffn", (1024,), None),
    ("w_ffn_up", (1024, 2 * D_FF), 1), ("conv_w", (3, 2 * D_FF), 1), ("conv_b", (2 * D_FF,), None),
    ("w_ffn_down", (D_FF, 1024), 0), ("g_post_ffn", (1024,), None),
)
PARAM_NAMES = tuple(p[0] for p in PARAMS)


def _shard_shape(shape, axis):
    if axis is None:
        return shape
    return tuple(s // N_DEV if a == axis else s for a, s in enumerate(shape))


SHARDED = tuple(p for p in PARAMS if p[2] is not None)
REPLICATED = tuple(p for p in PARAMS if p[2] is None)


def _layout():
    off, table = 0, {}
    for name, shape, _ in REPLICATED:
        table[name] = (off, shape[0])
        off += shape[0]
    rows = -(-off // PACK_W)
    rows = -(-rows // SUBLANES) * SUBLANES
    return table, off, rows


PACK_TABLE, PACK_USED, PACK_ROWS = _layout()


def _pick(n, cap):
    best = None
    for t in range(LANES, min(n, cap) + 1, LANES):
        if n % t == 0:
            best = t
    return best if best is not None else n


def _rows(n, cap, mult=SUBLANES):
    best = None
    for t in range(mult, min(n, cap) + 1, mult):
        if n % t == 0:
            best = t
    return best if best is not None else n


def _cparams(sem, vmem=VMEM_LIMIT):
    return pltpu.CompilerParams(dimension_semantics=sem, vmem_limit_bytes=vmem)


def _matmul(a, b, *, name, out_dtype=F32, trans_a=False, add=None, after=None, tm=1024, tn=1408, tk=640):
    if trans_a:
        kc, m = a.shape
    else:
        m, kc = a.shape
    n = b.shape[1]
    assert b.shape[0] == kc
    tm, tn, tk = _pick(m, tm), _pick(n, tn), _pick(kc, tk)
    nk = kc // tk

    def body(*refs):
        a_ref, b_ref = refs[:2]
        c_ref = refs[2] if add is not None else None
        o_ref, acc = refs[-2:]
        k = pl.program_id(2)

        @pl.when(k == 0)
        def _():
            if add is None:
                acc[...] = jnp.zeros_like(acc)
            else:
                acc[...] = c_ref[...].astype(F32)

        av = a_ref[...].astype(BF16)
        bv = b_ref[...].astype(BF16)
        if trans_a:
            acc[...] += lax.dot_general(av, bv, (((0,), (0,)), ((), ())), preferred_element_type=F32)
        else:
            acc[...] += jnp.dot(av, bv, preferred_element_type=F32)

        @pl.when(k == nk - 1)
        def _():
            o_ref[...] = acc[...].astype(out_dtype)

    if trans_a:
        a_spec = pl.BlockSpec((tk, tm), lambda i, j, k: (k, i))
    else:
        a_spec = pl.BlockSpec((tm, tk), lambda i, j, k: (i, k))
    in_specs = [a_spec, pl.BlockSpec((tk, tn), lambda i, j, k: (k, j))]
    args = [a, b]
    if add is not None:
        in_specs.append(pl.BlockSpec((tm, tn), lambda i, j, k: (i, j)))
        args.append(add)
    if after is not None:
        in_specs.append(pl.BlockSpec(memory_space=pl.ANY))
        args.append(after)
    return pl.pallas_call(
        body, name=name, grid=(m // tm, n // tn, nk),
        in_specs=in_specs, out_specs=pl.BlockSpec((tm, tn), lambda i, j, k: (i, j)),
        out_shape=jax.ShapeDtypeStruct((m, n), out_dtype),
        scratch_shapes=[pltpu.VMEM((tm, tn), F32)],
        compiler_params=_cparams(("parallel", "parallel", "arbitrary")),
    )(*args)


def _rms_fwd(x, g, *, name, out_dtype, add=None):
    s, n = x.shape
    ts = _rows(s, 512)

    def body(*refs):
        if add is None:
            x_ref, g_ref, o_ref = refs
        else:
            x_ref, g_ref, a_ref, o_ref = refs
        xv = x_ref[...]
        r = lax.rsqrt(jnp.mean(xv * xv, axis=-1, keepdims=True) + RMS_EPS)
        y = xv * r * g_ref[...]
        if add is not None:
            y = a_ref[...] + y
        o_ref[...] = y.astype(out_dtype)

    row = pl.BlockSpec((ts, n), lambda i: (i, 0))
    in_specs = [row, pl.BlockSpec((1, n), lambda i: (0, 0))]
    args = [x, g]
    if add is not None:
        in_specs.append(row)
        args.append(add)
    return pl.pallas_call(
        body, name=name, grid=(s // ts,), in_specs=in_specs, out_specs=row,
        out_shape=jax.ShapeDtypeStruct((s, n), out_dtype),
        compiler_params=_cparams(("parallel",)),
    )(*args)


def _rms_bwd(x, g, dy, *, name, out_dtype, add=None):
    s, n = x.shape
    ts = _rows(s, 512)

    def body(*refs):
        if add is None:
            x_ref, g_ref, dy_ref, dx_ref, dg_ref = refs
        else:
            x_ref, g_ref, dy_ref, a_ref, dx_ref, dg_ref = refs
        i = pl.program_id(0)
        xv = x_ref[...]
        dyv = dy_ref[...].astype(F32)
        r = lax.rsqrt(jnp.mean(xv * xv, axis=-1, keepdims=True) + RMS_EPS)
        nx = xv * r
        gdy = dyv * g_ref[...]
        dx = r * (gdy - nx * jnp.mean(nx * gdy, axis=-1, keepdims=True))
        if add is not None:
            dx = a_ref[...] + dx
        dx_ref[...] = dx.astype(out_dtype)

        @pl.when(i == 0)
        def _():
            dg_ref[...] = jnp.zeros_like(dg_ref)

        dg_ref[...] += jnp.sum(dyv * nx, axis=0, keepdims=True)

    row = pl.BlockSpec((ts, n), lambda i: (i, 0))
    vec = pl.BlockSpec((1, n), lambda i: (0, 0))
    in_specs = [row, vec, row]
    args = [x, g, dy]
    if add is not None:
        in_specs.append(row)
        args.append(add)
    return pl.pallas_call(
        body, name=name, grid=(s // ts,), in_specs=in_specs, out_specs=[row, vec],
        out_shape=[jax.ShapeDtypeStruct((s, n), out_dtype), jax.ShapeDtypeStruct((1, n), F32)],
        compiler_params=_cparams(("arbitrary",)),
    )(*args)


def _loss_step(x1, f, target, g, *, name):
    s, n = x1.shape
    ts = _rows(s, 512)

    def body(x_ref, f_ref, t_ref, g_ref, dy_ref, df_ref, dg_ref, sq_ref):
        i = pl.program_id(0)
        fv = f_ref[...]
        gv = g_ref[...]
        r = lax.rsqrt(jnp.mean(fv * fv, axis=-1, keepdims=True) + RMS_EPS)
        nx = fv * r
        err = x_ref[...] + nx * gv - t_ref[...]
        dy = err * (1.0 / n)
        dy_ref[...] = dy
        gdy = dy * gv
        df_ref[...] = (r * (gdy - nx * jnp.mean(nx * gdy, axis=-1, keepdims=True))).astype(BF16)

        @pl.when(i == 0)
        def _():
            sq_ref[...] = jnp.zeros_like(sq_ref)
            dg_ref[...] = jnp.zeros_like(dg_ref)

        sq_ref[...] += jnp.sum(err * err, axis=0, keepdims=True)
        dg_ref[...] += jnp.sum(dy * nx, axis=0, keepdims=True)

    row = pl.BlockSpec((ts, n), lambda i: (i, 0))
    vec = pl.BlockSpec((1, n), lambda i: (0, 0))
    return pl.pallas_call(
        body, name=name, grid=(s // ts,), in_specs=[row, row, row, vec], out_specs=[row, row, vec, vec],
        out_shape=[jax.ShapeDtypeStruct((s, n), F32), jax.ShapeDtypeStruct((s, n), BF16),
                   jax.ShapeDtypeStruct((1, n), F32), jax.ShapeDtypeStruct((1, n), F32)],
        compiler_params=_cparams(("arbitrary",)),
    )(x1, f, target, g)


def _rope_tables(positions):
    half = MLA_ROPE // 2
    inv_freq = ROPE_THETA ** (-jnp.arange(half, dtype=F32) / half)
    ang = positions.astype(F32)[:, None] * inv_freq[None, :]
    cos, sin = jnp.cos(ang), jnp.sin(ang)
    s = positions.shape[0]
    one = jnp.ones((s, MLA_NOPE), F32)
    zero = jnp.zeros((s, MLA_NOPE), F32)
    pad1 = jnp.ones((s, LANES - MLA_QK_DIM), F32)
    pad0 = jnp.zeros((s, LANES - MLA_QK_DIM), F32)
    zh = jnp.zeros((s, half), F32)
    c_tab = jnp.concatenate([one, cos, cos, pad1], axis=1)
    s1_tab = jnp.concatenate([zero, -sin, zh, pad0], axis=1)
    s2_tab = jnp.concatenate([zero, zh, sin, pad0], axis=1)
    return c_tab, s1_tab, s2_tab


def _rope_fwd(x, tabs, *, name, scale, add=None):
    s, n = x.shape
    nh = n // LANES
    ts = _rows(s, 512)
    half = MLA_ROPE // 2

    def body(*refs):
        if add is None:
            x_ref, c_ref, s1_ref, s2_ref, o_ref = refs
        else:
            x_ref, a_ref, c_ref, s1_ref, s2_ref, o_ref = refs
        c, s1, s2 = c_ref[...], s1_ref[...], s2_ref[...]
        for h in range(nh):
            xh = x_ref[:, h * LANES:(h + 1) * LANES]
            if add is not None:
                xh = xh + a_ref[...]
            y = xh * c + pltpu.roll(xh, LANES - half, 1) * s1 + pltpu.roll(xh, half, 1) * s2
            o_ref[:, h * LANES:(h + 1) * LANES] = (y * scale).astype(BF16)

    row = pl.BlockSpec((ts, n), lambda i: (i, 0))
    tab = pl.BlockSpec((ts, LANES), lambda i: (i, 0))
    in_specs = [row] + ([tab] if add is not None else []) + [tab, tab, tab]
    args = [x] + ([add] if add is not None else []) + list(tabs)
    return pl.pallas_call(
        body, name=name, grid=(s // ts,), in_specs=in_specs, out_specs=row,
        out_shape=jax.ShapeDtypeStruct((s, n), BF16),
        compiler_params=_cparams(("parallel",)),
    )(*args)


def _rope_bwd(dy, tabs, *, name, scale, with_add):
    s, n = dy.shape
    nh = n // LANES
    ts = _rows(s, 512)
    half = MLA_ROPE // 2

    def body(*refs):
        if with_add:
            dy_ref, c_ref, s1_ref, s2_ref, dx_ref, da_ref = refs
        else:
            dy_ref, c_ref, s1_ref, s2_ref, dx_ref = refs
        c, s1, s2 = c_ref[...], s1_ref[...], s2_ref[...]
        tot = None
        for h in range(nh):
            g = dy_ref[:, h * LANES:(h + 1) * LANES].astype(F32)
            dx = (g * c + pltpu.roll(g * s1, half, 1) + pltpu.roll(g * s2, LANES - half, 1)) * scale
            dx_ref[:, h * LANES:(h + 1) * LANES] = dx.astype(BF16)
            tot = dx if tot is None else tot + dx
        if with_add:
            da_ref[...] = tot

    row = pl.BlockSpec((ts, n), lambda i: (i, 0))
    tab = pl.BlockSpec((ts, LANES), lambda i: (i, 0))
    out_specs = [row, tab] if with_add else row
    out_shape = [jax.ShapeDtypeStruct((s, n), BF16)]
    if with_add:
        out_shape.append(jax.ShapeDtypeStruct((s, LANES), F32))
    else:
        out_shape = out_shape[0]
    return pl.pallas_call(
        body, name=name, grid=(s // ts,), in_specs=[row, tab, tab, tab], out_specs=out_specs,
        out_shape=out_shape, compiler_params=_cparams(("parallel",)),
    )(dy, *tabs)


def _scores(q, k, scale, diag):
    s = lax.dot_general(q, k, (((1,), (1,)), ((), ())), preferred_element_type=F32)
    if scale != 1.0:
        s = s * scale
    if diag:
        rows = lax.broadcasted_iota(jnp.int32, s.shape, 0)
        cols = lax.broadcasted_iota(jnp.int32, s.shape, 1)
        s = jnp.where(cols <= rows, s, NEG_INF)
    return s


def _flash_fwd(q, k, v, *, name, heads, qoff, koff, voff, causal, scale, tq, tk):
    s_q, s_kv = q.shape[0], k.shape[0]
    tq, tk = min(tq, s_q), min(tk, s_kv)
    nq, nk = s_q // tq, s_kv // tk
    if causal:
        assert tq == tk and s_q == s_kv

    def body(q_ref, k_ref, v_ref, o_ref, lse_ref, m_s, l_s, acc):
        i, j = pl.program_id(1), pl.program_id(2)

        @pl.when(j == 0)
        def _():
            m_s[...] = jnp.full_like(m_s, NEG_INF)
            l_s[...] = jnp.zeros_like(l_s)
            acc[...] = jnp.zeros_like(acc)

        def step(diag):
            s = _scores(q_ref[...], k_ref[...], scale, diag)
            m_prev = m_s[...]
            m_cur = jnp.maximum(m_prev, jnp.max(s, axis=1, keepdims=True))
            alpha = jnp.exp(m_prev - m_cur)
            p = jnp.exp(s - m_cur[:, :1])
            l_s[...] = alpha * l_s[...] + jnp.sum(p, axis=1, keepdims=True)
            acc[...] = alpha * acc[...] + jnp.dot(p.astype(BF16), v_ref[...], preferred_element_type=F32)
            m_s[...] = m_cur

        def finish():
            o_ref[...] = (acc[...] / l_s[...]).astype(o_ref.dtype)
            lse_ref[...] = m_s[...] + jnp.log(l_s[...])

        if causal:
            @pl.when(j < i)
            def _():
                step(False)

            @pl.when(j == i)
            def _():
                step(True)
                finish()
        else:
            step(False)

            @pl.when(j == nk - 1)
            def _():
                finish()

    def kv_idx(off):
        if causal:
            return lambda h, i, j: (jnp.minimum(j, i), off + h)
        return lambda h, i, j: (j, off + h)

    blk_q = pl.BlockSpec((tq, LANES), lambda h, i, j: (i, qoff + h))
    out_q = pl.BlockSpec((tq, LANES), lambda h, i, j: (i, h))
    return pl.pallas_call(
        body, name=name, grid=(heads, nq, nk),
        in_specs=[blk_q, pl.BlockSpec((tk, LANES), kv_idx(koff)), pl.BlockSpec((tk, LANES), kv_idx(voff))],
        out_specs=[out_q, out_q],
        out_shape=[jax.ShapeDtypeStruct((s_q, heads * LANES), BF16),
                   jax.ShapeDtypeStruct((s_q, heads * LANES), F32)],
        scratch_shapes=[pltpu.VMEM((tq, LANES), F32)] * 3,
        compiler_params=_cparams(("parallel", "parallel", "arbitrary")),
    )(q, k, v)


def _flash_bwd_dq(q, k, v, do, lse, delta, *, name, heads, qoff, koff, voff, causal, scale, tq, tk, out_dtype):
    s_q, s_kv = q.shape[0], k.shape[0]
    tq, tk = min(tq, s_q), min(tk, s_kv)
    nq, nk = s_q // tq, s_kv // tk

    def body(q_ref, k_ref, v_ref, do_ref, lse_ref, dl_ref, dq_ref, acc):
        i, j = pl.program_id(1), pl.program_id(2)

        @pl.when(j == 0)
        def _():
            acc[...] = jnp.zeros_like(acc)

        def step(diag):
            s = _scores(q_ref[...], k_ref[...], scale, diag)
            p = jnp.exp(s - lse_ref[:, :1])
            dp = lax.dot_general(do_ref[...], v_ref[...], (((1,), (1,)), ((), ())), preferred_element_type=F32)
            ds = p * (dp - dl_ref[:, :1])
            acc[...] += jnp.dot(ds.astype(BF16), k_ref[...], preferred_element_type=F32)

        def finish():
            dq_ref[...] = (acc[...] * scale).astype(out_dtype)

        if causal:
            @pl.when(j < i)
            def _():
                step(False)

            @pl.when(j == i)
            def _():
                step(True)
                finish()
        else:
            step(False)

            @pl.when(j == nk - 1)
            def _():
                finish()

    def kv_idx(off):
        if causal:
            return lambda h, i, j: (jnp.minimum(j, i), off + h)
        return lambda h, i, j: (j, off + h)

    blk_q = pl.BlockSpec((tq, LANES), lambda h, i, j: (i, qoff + h))
    blk_h = pl.BlockSpec((tq, LANES), lambda h, i, j: (i, h))
    return pl.pallas_call(
        body, name=name, grid=(heads, nq, nk),
        in_specs=[blk_q, pl.BlockSpec((tk, LANES), kv_idx(koff)), pl.BlockSpec((tk, LANES), kv_idx(voff)),
                  blk_h, blk_h, blk_h],
        out_specs=blk_h,
        out_shape=jax.ShapeDtypeStruct((s_q, heads * LANES), out_dtype),
        scratch_shapes=[pltpu.VMEM((tq, LANES), F32)],
        compiler_params=_cparams(("parallel", "parallel", "arbitrary")),
    )(q, k, v, do, lse, delta)


def _flash_bwd_dkv(q, k, v, do, lse, delta, *, name, heads, qoff, koff, voff, causal, scale, tq, tk,
                   dk_dtype, dv_dtype):
    s_q, s_kv = q.shape[0], k.shape[0]
    tq, tk = min(tq, s_q), min(tk, s_kv)
    nq, nk = s_q // tq, s_kv // tk

    def body(q_ref, k_ref, v_ref, do_ref, lse_ref, dl_ref, dk_ref, dv_ref, dk_acc, dv_acc):
        j, i = pl.program_id(1), pl.program_id(2)

        @pl.when(i == 0)
        def _():
            dk_acc[...] = jnp.zeros_like(dk_acc)
            dv_acc[...] = jnp.zeros_like(dv_acc)

        def step(diag):
            s = _scores(q_ref[...], k_ref[...], scale, diag)
            p = jnp.exp(s - lse_ref[:, :1])
            dov = do_ref[...]
            dp = lax.dot_general(dov, v_ref[...], (((1,), (1,)), ((), ())), preferred_element_type=F32)
            ds = p * (dp - dl_ref[:, :1])
            dv_acc[...] += lax.dot_general(p.astype(BF16), dov, (((0,), (0,)), ((), ())),
                                           preferred_element_type=F32)
            dk_acc[...] += lax.dot_general(ds.astype(BF16), q_ref[...], (((0,), (0,)), ((), ())),
                                           preferred_element_type=F32)

        if causal:
            @pl.when(i > j)
            def _():
                step(False)

            @pl.when(i == j)
            def _():
                step(True)
        else:
            step(False)

        @pl.when(i == nq - 1)
        def _():
            dk_ref[...] = (dk_acc[...] * scale).astype(dk_dtype)
            dv_ref[...] = dv_acc[...].astype(dv_dtype)

    def q_idx(off):
        if causal:
            return lambda h, j, i: (jnp.maximum(i, j), off + h)
        return lambda h, j, i: (i, off + h)

    blk_h = pl.BlockSpec((tq, LANES), q_idx(0))
    out_k = pl.BlockSpec((tk, LANES), lambda h, j, i: (j, h))
    return pl.pallas_call(
        body, name=name, grid=(heads, nk, nq),
        in_specs=[pl.BlockSpec((tq, LANES), q_idx(qoff)),
                  pl.BlockSpec((tk, LANES), lambda h, j, i: (j, koff + h)),
                  pl.BlockSpec((tk, LANES), lambda h, j, i: (j, voff + h)),
                  blk_h, blk_h, blk_h],
        out_specs=[out_k, out_k],
        out_shape=[jax.ShapeDtypeStruct((s_kv, heads * LANES), dk_dtype),
                   jax.ShapeDtypeStruct((s_kv, heads * LANES), dv_dtype)],
        scratch_shapes=[pltpu.VMEM((tk, LANES), F32)] * 2,
        compiler_params=_cparams(("parallel", "parallel", "arbitrary")),
    )(q, k, v, do, lse, delta)


def _row_dot(a, b, *, name):
    s, n = a.shape
    nh = n // LANES
    ts = _rows(s, 512)

    def body(a_ref, b_ref, o_ref):
        for h in range(nh):
            sl = slice(h * LANES, (h + 1) * LANES)
            d = jnp.sum(a_ref[:, sl].astype(F32) * b_ref[:, sl].astype(F32), axis=1, keepdims=True)
            o_ref[:, sl] = jnp.broadcast_to(d, (ts, LANES))

    row = pl.BlockSpec((ts, n), lambda i: (i, 0))
    return pl.pallas_call(
        body, name=name, grid=(s // ts,), in_specs=[row, row], out_specs=row,
        out_shape=jax.ShapeDtypeStruct((s, n), F32), compiler_params=_cparams(("parallel",)),
    )(a, b)


CAUSAL_T = 512
LOG2E = 1.4426950408889634
LN2 = 0.6931471805599453


def _causal_fwd(q, k, v, *, name, heads, ones_lane):
    s = q.shape[0]
    t = CAUSAL_T
    nq = s // (2 * t)
    assert nq * 2 * t == s

    def body(q_ref, k_ref, v_ref, o_ref, lse_ref, v1, m_s, acc):
        i = pl.program_id(1)

        @pl.when(i == 0)
        def _():
            lane = lax.broadcasted_iota(jnp.int32, v1.shape, 1)
            v1[...] = jnp.where(lane == ones_lane, 1.0, v_ref[...]).astype(BF16)

        m_s[...] = jnp.full_like(m_s, NEG_INF)
        acc[...] = jnp.zeros_like(acc)
        halves = (q_ref[0:t, :], q_ref[t:2 * t, :])

        def raw(c, j):
            rows = pl.ds(pl.multiple_of(j * t, t), t)
            return lax.dot_general(halves[c], k_ref[rows, :], (((1,), (1,)), ((), ())), preferred_element_type=F32)

        def update(c, sc, j):
            m_prev = m_s[c]
            m_cur = jnp.maximum(m_prev, jnp.max(sc, axis=1, keepdims=True))
            p = jnp.exp2(sc - m_cur[:, :1]).astype(BF16)
            acc[c] = jnp.exp2(m_prev - m_cur) * acc[c] + jnp.dot(
                p, v1[pl.ds(pl.multiple_of(j * t, t), t), :], preferred_element_type=F32)
            m_s[c] = m_cur

        def loop(j, carry):
            sa, sb = raw(0, j), raw(1, j)
            update(0, sa, j)
            update(1, sb, j)
            return carry

        lax.fori_loop(0, 2 * i, loop, 0)
        sa, sb = raw(0, 2 * i), raw(1, 2 * i)
        below = (lax.broadcasted_iota(jnp.int32, sa.shape, 1) <= lax.broadcasted_iota(jnp.int32, sa.shape, 0))
        update(0, jnp.where(below, sa, NEG_INF), 2 * i)
        update(1, sb, 2 * i)
        update(1, jnp.where(below, raw(1, 2 * i + 1), NEG_INF), 2 * i + 1)
        lane = lax.broadcasted_iota(jnp.int32, (t, LANES), 1)
        for c in range(2):
            out = acc[c]
            den = out[:, ones_lane:ones_lane + 1]
            o_ref[c * t:(c + 1) * t, :] = jnp.where(lane == ones_lane, 0.0, out / den).astype(BF16)
            lse_ref[c * t:(c + 1) * t, :] = m_s[c] + jnp.log2(den)

    blk = pl.BlockSpec((2 * t, LANES), lambda h, i: (i, h))
    full = pl.BlockSpec((s, LANES), lambda h, i: (0, h))
    return pl.pallas_call(
        body, name=name, grid=(heads, nq), in_specs=[blk, full, full], out_specs=[blk, blk],
        out_shape=[jax.ShapeDtypeStruct((s, heads * LANES), BF16), jax.ShapeDtypeStruct((s, heads * LANES), F32)],
        scratch_shapes=[pltpu.VMEM((s, LANES), BF16), pltpu.VMEM((2, t, LANES), F32),
                        pltpu.VMEM((2, t, LANES), F32)],
        compiler_params=_cparams(("parallel", "arbitrary")),
    )(q, k, v)


def _causal_bwd(q, k, v, do, lse, delta, *, name, heads):
    s = q.shape[0]
    t = min(CAUSAL_T, s)
    nt = s // t

    def body(q_ref, k_ref, v_ref, do_ref, lse_ref, dl_ref, dq_ref, dk_ref, dv_ref, dk_acc, dv_acc):
        j = pl.program_id(1)

        @pl.when(j == 0)
        def _():
            dq_ref[...] = jnp.zeros_like(dq_ref)

        dk_acc[...] = jnp.zeros_like(dk_acc)
        dv_acc[...] = jnp.zeros_like(dv_acc)
        kv, vv = k_ref[...], v_ref[...]

        def step(i, diag, size=t):
            rows = pl.ds(pl.multiple_of(i * t, t), size)
            qv, dov = q_ref[rows, :], do_ref[rows, :]
            sc = _scores(qv, kv, 1.0, diag)
            p = jnp.exp2(sc - lse_ref[rows, :][:, :1])
            dp = lax.dot_general(dov, vv, (((1,), (1,)), ((), ())), preferred_element_type=F32)
            ds = (p * (dp - dl_ref[rows, :][:, :1])).astype(BF16)
            dv_acc[...] += lax.dot_general(p.astype(BF16), dov, (((0,), (0,)), ((), ())),
                                           preferred_element_type=F32)
            dk_acc[...] += lax.dot_general(ds, qv, (((0,), (0,)), ((), ())), preferred_element_type=F32)
            dq_ref[rows, :] += jnp.dot(ds, kv, preferred_element_type=F32)

        step(j, True)
        odd = (nt - 1 - j) % 2

        @pl.when(odd == 1)
        def _():
            step(j + 1, False)

        def loop(n, carry):
            step(j + 1 + odd + 2 * n, False, 2 * t)
            return carry

        lax.fori_loop(0, (nt - 1 - j) // 2, loop, 0)
        dk_ref[...] = dk_acc[...] * LN2
        dv_ref[...] = dv_acc[...].astype(BF16)

    blk = pl.BlockSpec((t, LANES), lambda h, j: (j, h))
    full = pl.BlockSpec((s, LANES), lambda h, j: (0, h))
    return pl.pallas_call(
        body, name=name, grid=(heads, nt), in_specs=[full, blk, blk, full, full, full],
        out_specs=[full, blk, blk],
        out_shape=[jax.ShapeDtypeStruct((s, heads * LANES), F32), jax.ShapeDtypeStruct((s, heads * LANES), F32),
                   jax.ShapeDtypeStruct((s, heads * LANES), BF16)],
        scratch_shapes=[pltpu.VMEM((t, LANES), F32)] * 2,
        compiler_params=_cparams(("parallel", "arbitrary")),
    )(q, k, v, do, lse, delta)


def _band_masks(dilation, slope):
    qi = lax.broadcasted_iota(jnp.int32, (BLOCK, 2 * BLOCK), 0)
    kj = lax.broadcasted_iota(jnp.int32, (BLOCK, 2 * BLOCK), 1)
    dist = qi + BLOCK - kj
    valid = (dist >= 0) & (dist <= BLOCK)
    bias = -slope * (dist * dilation).astype(F32)
    return valid, bias


BAND_UNROLL = 8


def _aligned(start):
    return start if isinstance(start, int) else pl.multiple_of(start, BLOCK)


def _band_fwd(q, k, v, slopes, *, name, dilation, qoff, koff, voff):
    s = q.shape[0]
    sub = s // dilation
    nb = sub // BLOCK
    assert nb * BLOCK == sub
    unroll = min(BAND_UNROLL, nb)
    assert nb % unroll == 0
    scale = LANES ** -0.5

    def body(sl_ref, q_ref, k_ref, v_ref, o_ref, lse_ref):
        slope = sl_ref[pl.program_id(0)]
        valid2, bias2 = _band_masks(dilation, slope)
        valid1, bias1 = valid2[:, BLOCK:], bias2[:, BLOCK:]

        def block(start_q, kk, vv, valid, bias):
            qb = q_ref[pl.ds(start_q, BLOCK), :]
            sc = lax.dot_general(qb, kk, (((1,), (1,)), ((), ())), preferred_element_type=F32) * scale
            sc = jnp.where(valid, sc + bias, NEG_INF)
            m = jnp.max(sc, axis=1, keepdims=True)
            e = jnp.exp(sc - m)
            den = jnp.sum(e, axis=1, keepdims=True)
            p = (e / den).astype(BF16)
            o_ref[pl.ds(start_q, BLOCK), :] = jnp.dot(p, vv, preferred_element_type=F32)
            lse_ref[pl.ds(start_q, BLOCK), :] = jnp.broadcast_to(m + jnp.log(den), (BLOCK, LANES))

        block(0, k_ref[0:BLOCK, :], v_ref[0:BLOCK, :], valid1, bias1)

        def general(jj):
            start_q, start_k = _aligned(jj * BLOCK), _aligned((jj - 1) * BLOCK)
            block(start_q, k_ref[pl.ds(start_k, 2 * BLOCK), :], v_ref[pl.ds(start_k, 2 * BLOCK), :], valid2, bias2)

        for jj in range(1, unroll):
            general(jj)

        def loop(t, carry):
            for u in range(unroll):
                general(t * unroll + u)
            return carry

        lax.fori_loop(1, nb // unroll, loop, 0)

    def spec(off):
        return pl.BlockSpec((sub, LANES), lambda h, r: (r, off + h))

    out = pl.BlockSpec((sub, LANES), lambda h, r: (r, h))
    return pl.pallas_call(
        body, name=name, grid=(DIL_HEADS_PER_GROUP, dilation),
        in_specs=[pl.BlockSpec(memory_space=pltpu.SMEM), spec(qoff), spec(koff), spec(voff)],
        out_specs=[out, out],
        out_shape=[jax.ShapeDtypeStruct((s, DIL_HEADS_PER_GROUP * LANES), F32)] * 2,
        compiler_params=_cparams(("parallel", "parallel"), VMEM_LIMIT_BIG),
    )(slopes, q, k, v)


def _band_bwd(q, k, v, do, lse, delta, slopes, *, name, dilation, qoff, koff, voff):
    s = q.shape[0]
    sub = s // dilation
    nb = sub // BLOCK
    unroll = min(BAND_UNROLL, nb)
    scale = LANES ** -0.5

    def body(sl_ref, q_ref, k_ref, v_ref, do_ref, lse_ref, dl_ref, dq_ref, dk_ref, dv_ref):
        slope = sl_ref[pl.program_id(0)]
        valid2, bias2 = _band_masks(dilation, slope)
        valid1, bias1 = valid2[:, BLOCK:], bias2[:, BLOCK:]

        def block(start_q, kk, vv, valid, bias):
            qb = q_ref[pl.ds(start_q, BLOCK), :]
            dob = do_ref[pl.ds(start_q, BLOCK), :]
            sc = lax.dot_general(qb, kk, (((1,), (1,)), ((), ())), preferred_element_type=F32) * scale
            sc = jnp.where(valid, sc + bias, NEG_INF)
            p = jnp.exp(sc - lse_ref[pl.ds(start_q, BLOCK), :][:, :1])
            dp = lax.dot_general(dob, vv, (((1,), (1,)), ((), ())), preferred_element_type=F32)
            ds = (p * (dp - dl_ref[pl.ds(start_q, BLOCK), :][:, :1])).astype(BF16)
            dq = jnp.dot(ds, kk, preferred_element_type=F32) * scale
            dq_ref[pl.ds(start_q, BLOCK), :] = dq.astype(BF16)
            dkk = lax.dot_general(ds, qb, (((0,), (0,)), ((), ())), preferred_element_type=F32) * scale
            dvv = lax.dot_general(p.astype(BF16), dob, (((0,), (0,)), ((), ())), preferred_element_type=F32)
            return dkk, dvv

        carry0 = block(0, k_ref[0:BLOCK, :], v_ref[0:BLOCK, :], valid1, bias1)

        def general(jj, carry):
            dk_part, dv_part = carry
            start_q, start_k = _aligned(jj * BLOCK), _aligned((jj - 1) * BLOCK)
            dkk, dvv = block(start_q, k_ref[pl.ds(start_k, 2 * BLOCK), :], v_ref[pl.ds(start_k, 2 * BLOCK), :],
                             valid2, bias2)
            dk_ref[pl.ds(start_k, BLOCK), :] = (dk_part + dkk[:BLOCK]).astype(BF16)
            dv_ref[pl.ds(start_k, BLOCK), :] = (dv_part + dvv[:BLOCK]).astype(BF16)
            return dkk[BLOCK:], dvv[BLOCK:]

        for jj in range(1, unroll):
            carry0 = general(jj, carry0)

        def loop(t, carry):
            for u in range(unroll):
                carry = general(t * unroll + u, carry)
            return carry

        dk_last, dv_last = lax.fori_loop(1, nb // unroll, loop, carry0)
        dk_ref[(nb - 1) * BLOCK:nb * BLOCK, :] = dk_last.astype(BF16)
        dv_ref[(nb - 1) * BLOCK:nb * BLOCK, :] = dv_last.astype(BF16)

    def spec(off):
        return pl.BlockSpec((sub, LANES), lambda h, r: (r, off + h))

    out = spec(0)
    return pl.pallas_call(
        body, name=name, grid=(DIL_HEADS_PER_GROUP, dilation),
        in_specs=[pl.BlockSpec(memory_space=pltpu.SMEM), spec(qoff), spec(koff), spec(voff), out, out, out],
        out_specs=[out, out, out],
        out_shape=[jax.ShapeDtypeStruct((s, DIL_HEADS_PER_GROUP * LANES), BF16)] * 3,
        compiler_params=_cparams(("parallel", "parallel"), VMEM_LIMIT_BIG),
    )(slopes, q, k, v, do, lse, delta)


def _mix_fwd(outs, lses, *, name):
    s, n = outs[0].shape
    ts = _rows(s, 512)

    def body(o0, o1, o2, l0, l1, l2, y_ref):
        la, lb, lc = l0[...], l1[...], l2[...]
        m = jnp.maximum(jnp.maximum(la, lb), lc)
        ea, eb, ec = jnp.exp(la - m), jnp.exp(lb - m), jnp.exp(lc - m)
        den = ea + eb + ec
        y = (ea / den) * o0[...] + (eb / den) * o1[...] + (ec / den) * o2[...]
        y_ref[...] = y.astype(BF16)

    row = pl.BlockSpec((ts, n), lambda i: (i, 0))
    return pl.pallas_call(
        body, name=name, grid=(s // ts,), in_specs=[row] * 6, out_specs=row,
        out_shape=jax.ShapeDtypeStruct((s, n), BF16), compiler_params=_cparams(("parallel",)),
    )(*outs, *lses)


def _mix_bwd(dy, outs, lses, *, name):
    s, n = dy.shape
    nh = n // LANES
    ts = _rows(s, 256)

    def body(dy_ref, o0, o1, o2, l0, l1, l2, d0, d1, d2, e0, e1, e2):
        la, lb, lc = l0[...], l1[...], l2[...]
        m = jnp.maximum(jnp.maximum(la, lb), lc)
        ea, eb, ec = jnp.exp(la - m), jnp.exp(lb - m), jnp.exp(lc - m)
        den = ea + eb + ec
        wa, wb, wc = ea / den, eb / den, ec / den
        dyv = dy_ref[...]
        y = wa * o0[...] + wb * o1[...] + wc * o2[...]
        prod = dyv * y
        d0[...] = (wa * dyv).astype(BF16)
        d1[...] = (wb * dyv).astype(BF16)
        d2[...] = (wc * dyv).astype(BF16)
        for h in range(nh):
            sl = slice(h * LANES, (h + 1) * LANES)
            t = jnp.sum(prod[:, sl], axis=1, keepdims=True)
            e0[:, sl] = wa[:, sl] * t
            e1[:, sl] = wb[:, sl] * t
            e2[:, sl] = wc[:, sl] * t

    row = pl.BlockSpec((ts, n), lambda i: (i, 0))
    return pl.pallas_call(
        body, name=name, grid=(s // ts,), in_specs=[row] * 7, out_specs=[row] * 6,
        out_shape=[jax.ShapeDtypeStruct((s, n), BF16)] * 3 + [jax.ShapeDtypeStruct((s, n), F32)] * 3,
        compiler_params=_cparams(("parallel",)),
    )(dy, *outs, *lses)


def _gate_fwd(gp, b_gate, branches, *, name):
    s = gp.shape[0]
    ts = _rows(s, 256)

    def body(gp_ref, b_ref, b0, b1, b2, o_ref):
        tot = None
        for i, br in enumerate((b0, b1, b2)):
            sl = slice(i * D_MODEL, (i + 1) * D_MODEL)
            t = jax.nn.sigmoid(gp_ref[:, sl] + b_ref[:, sl]) * br[...]
            tot = t if tot is None else tot + t
        o_ref[...] = tot.astype(BF16)

    row = pl.BlockSpec((ts, D_MODEL), lambda i: (i, 0))
    return pl.pallas_call(
        body, name=name, grid=(s // ts,),
        in_specs=[pl.BlockSpec((ts, 3 * D_MODEL), lambda i: (i, 0)), pl.BlockSpec((1, 3 * D_MODEL), lambda i: (0, 0)),
                  row, row, row],
        out_specs=row, out_shape=jax.ShapeDtypeStruct((s, D_MODEL), BF16),
        compiler_params=_cparams(("parallel",)),
    )(gp, b_gate, *branches)


def _gate_bwd(dm, gp, b_gate, branches, *, name):
    s = gp.shape[0]
    ts = _rows(s, 256)

    def body(dm_ref, gp_ref, b_ref, b0, b1, b2, d0, d1, d2, dgp_ref, db_ref):
        i = pl.program_id(0)

        @pl.when(i == 0)
        def _():
            db_ref[...] = jnp.zeros_like(db_ref)

        dmv = dm_ref[...]
        for k, (br, dbr) in enumerate(((b0, d0), (b1, d1), (b2, d2))):
            sl = slice(k * D_MODEL, (k + 1) * D_MODEL)
            sg = jax.nn.sigmoid(gp_ref[:, sl] + b_ref[:, sl])
            dbr[...] = (dmv * sg).astype(BF16)
            dg = dmv * br[...] * sg * (1.0 - sg)
            dgp_ref[:, sl] = dg.astype(BF16)
            db_ref[:, sl] += jnp.sum(dg, axis=0, keepdims=True)

    row = pl.BlockSpec((ts, D_MODEL), lambda i: (i, 0))
    wide = pl.BlockSpec((ts, 3 * D_MODEL), lambda i: (i, 0))
    vec = pl.BlockSpec((1, 3 * D_MODEL), lambda i: (0, 0))
    return pl.pallas_call(
        body, name=name, grid=(s // ts,),
        in_specs=[row, wide, vec, row, row, row], out_specs=[row, row, row, wide, vec],
        out_shape=[jax.ShapeDtypeStruct((s, D_MODEL), BF16)] * 3
        + [jax.ShapeDtypeStruct((s, 3 * D_MODEL), BF16), jax.ShapeDtypeStruct((1, 3 * D_MODEL), F32)],
        compiler_params=_cparams(("arbitrary",)),
    )(dm, gp, b_gate, *branches)


CONV_TC = 1408


def _shift_down(x, halo, k):
    rolled = pltpu.roll(x, k, 0)
    r8 = lax.broadcasted_iota(jnp.int32, halo.shape, 0)
    top = jnp.where(r8 < k, pltpu.roll(halo, k, 0), rolled[:SUBLANES])
    return jnp.concatenate([top, rolled[SUBLANES:]], axis=0)


def _shift_up(x, halo, k):
    n = x.shape[0]
    rolled = pltpu.roll(x, n - k, 0)
    r8 = lax.broadcasted_iota(jnp.int32, halo.shape, 0)
    bot = jnp.where(r8 >= SUBLANES - k, pltpu.roll(halo, SUBLANES - k, 0), rolled[n - SUBLANES:])
    return jnp.concatenate([rolled[:n - SUBLANES], bot], axis=0)


def _conv_fwd(u, conv_w, conv_b, *, name):
    s = u.shape[0]
    ts = _rows(s, 256)
    nct = D_FF // CONV_TC
    per8 = ts // SUBLANES

    def body(ug, uv, hg, hv, wg, wv, bg, bv, zg_ref, zv_ref, a_ref):
        first = pl.program_id(1) == 0

        def conv(u_ref, h_ref, w_ref, b_ref):
            x = u_ref[...]
            halo = jnp.where(first, 0.0, h_ref[...])
            z = b_ref[...] + w_ref[0:1, :] * _shift_down(x, halo, 2)
            z = z + w_ref[1:2, :] * _shift_down(x, halo, 1)
            return z + w_ref[2:3, :] * x

        zg = conv(ug, hg, wg, bg)
        zv = conv(uv, hv, wv, bv)
        zg_ref[...] = zg
        zv_ref[...] = zv
        a_ref[...] = (zg * jax.nn.sigmoid(zg) * zv).astype(BF16)

    def col(off):
        return pl.BlockSpec((ts, CONV_TC), lambda c, i: (i, c + off))

    def halo(off):
        return pl.BlockSpec((SUBLANES, CONV_TC), lambda c, i: (jnp.maximum(i * per8 - 1, 0), c + off))

    def wspec(rows, off):
        return pl.BlockSpec((rows, CONV_TC), lambda c, i: (0, c + off))

    zg, zv, a = pl.pallas_call(
        body, name=name, grid=(nct, s // ts),
        in_specs=[col(0), col(nct), halo(0), halo(nct), wspec(3, 0), wspec(3, nct), wspec(1, 0), wspec(1, nct)],
        out_specs=[col(0), col(0), col(0)],
        out_shape=[jax.ShapeDtypeStruct((s, D_FF), F32)] * 2 + [jax.ShapeDtypeStruct((s, D_FF), BF16)],
        compiler_params=_cparams(("parallel", "parallel")),
    )(u, u, u, u, conv_w, conv_w, conv_b, conv_b)
    return zg, zv, a


def _conv_bwd(da, zg, zv, u, conv_w, *, name):
    s = da.shape[0]
    ts = _rows(s, 256)
    nct = D_FF // CONV_TC
    per8 = ts // SUBLANES
    nrow = s // ts
    last8 = s // SUBLANES - 1

    def dz_of(dav, g, val):
        sg = jax.nn.sigmoid(g)
        return dav * val * sg * (1.0 + g * (1.0 - sg)), dav * g * sg

    def body(da_ref, zg_ref, zv_ref, da_nx, zg_nx, zv_nx, ug_ref, uv_ref, ug_pv, uv_pv, wg_ref, wv_ref,
             dug_ref, duv_ref, accg_ref, accv_ref):
        i = pl.program_id(1)
        dzg, dzv = dz_of(da_ref[...], zg_ref[...], zv_ref[...])
        da_next = jnp.where(i == nrow - 1, 0.0, da_nx[...])
        nxg, nxv = dz_of(da_next, zg_nx[...], zv_nx[...])

        @pl.when(i == 0)
        def _():
            accg_ref[...] = jnp.zeros_like(accg_ref)
            accv_ref[...] = jnp.zeros_like(accv_ref)

        for dz, nxt, u_ref, pv_ref, w_ref, du_ref, acc_ref in (
                (dzg, nxg, ug_ref, ug_pv, wg_ref, dug_ref, accg_ref),
                (dzv, nxv, uv_ref, uv_pv, wv_ref, duv_ref, accv_ref)):
            du = w_ref[2:3, :] * dz + w_ref[1:2, :] * _shift_up(dz, nxt, 1) + w_ref[0:1, :] * _shift_up(dz, nxt, 2)
            du_ref[...] = du.astype(BF16)
            x = u_ref[...]
            prev = jnp.where(i == 0, 0.0, pv_ref[...])
            acc_ref[0:1, :] += jnp.sum(dz * _shift_down(x, prev, 2), axis=0, keepdims=True)
            acc_ref[1:2, :] += jnp.sum(dz * _shift_down(x, prev, 1), axis=0, keepdims=True)
            acc_ref[2:3, :] += jnp.sum(dz * x, axis=0, keepdims=True)
            acc_ref[3:4, :] += jnp.sum(dz, axis=0, keepdims=True)

    def blk(off):
        return pl.BlockSpec((ts, CONV_TC), lambda c, i: (i, c + off))

    def nxt8(off):
        return pl.BlockSpec((SUBLANES, CONV_TC), lambda c, i: (jnp.minimum((i + 1) * per8, last8), c + off))

    def prv8(off):
        return pl.BlockSpec((SUBLANES, CONV_TC), lambda c, i: (jnp.maximum(i * per8 - 1, 0), c + off))

    def wspec(off):
        return pl.BlockSpec((3, CONV_TC), lambda c, i: (0, c + off))

    acc = pl.BlockSpec((SUBLANES, CONV_TC), lambda c, i: (0, c))
    return pl.pallas_call(
        body, name=name, grid=(nct, nrow),
        in_specs=[blk(0), blk(0), blk(0), nxt8(0), nxt8(0), nxt8(0), blk(0), blk(nct), prv8(0), prv8(nct),
                  wspec(0), wspec(nct)],
        out_specs=[blk(0), blk(0), acc, acc],
        out_shape=[jax.ShapeDtypeStruct((s, D_FF), BF16)] * 2 + [jax.ShapeDtypeStruct((SUBLANES, D_FF), F32)] * 2,
        compiler_params=_cparams(("parallel", "arbitrary")),
    )(da, zg, zv, da, zg, zv, u, u, u, u, conv_w, conv_w)


def _peer(k):
    x, y, c = lax.axis_index("x"), lax.axis_index("y"), lax.axis_index("c")
    px = 1 - x if k & 4 else x
    py = 1 - y if k & 2 else y
    pc = 1 - c if k & 1 else c
    return (px, py, pc), 4 * px + 2 * py + pc


def _exchange(bufs, *, name, gather):
    n = len(bufs)
    npeer = N_DEV - 1

    def body(*refs):
        srcs, outs = refs[:n], refs[n:2 * n]
        send_sems, recv_sems, local_sems = refs[2 * n:]
        _, me = _peer(0)
        mine = [src if gather else src.at[me] for src in srcs]
        local = [pltpu.make_async_copy(mine[p], outs[p].at[me], local_sems.at[p]) for p in range(n)]
        for cp in local:
            cp.start()
        sends = []
        for k in range(1, N_DEV):
            dev, idx = _peer(k)
            for p in range(n):
                cp = pltpu.make_async_remote_copy(
                    src_ref=srcs[p] if gather else srcs[p].at[idx], dst_ref=outs[p].at[me],
                    send_sem=send_sems.at[p * npeer + k - 1], recv_sem=recv_sems.at[p * npeer + k - 1],
                    device_id=dev, device_id_type=pl.DeviceIdType.MESH)
                cp.start()
                sends.append(cp)
        for k in range(1, N_DEV):
            dev, idx = _peer(k)
            for p in range(n):
                pltpu.make_async_remote_copy(
                    src_ref=mine[p], dst_ref=outs[p].at[idx],
                    send_sem=send_sems.at[p * npeer + k - 1], recv_sem=recv_sems.at[p * npeer + k - 1],
                    device_id=dev, device_id_type=pl.DeviceIdType.MESH).wait_recv()
        for cp in sends:
            cp.wait_send()
        for cp in local:
            cp.wait()

    any_spec = pl.BlockSpec(memory_space=pl.ANY)
    return pl.pallas_call(
        body, name=name,
        in_specs=[any_spec] * n, out_specs=[any_spec] * n,
        out_shape=[jax.ShapeDtypeStruct((N_DEV,) + b.shape[-2:], b.dtype) for b in bufs],
        scratch_shapes=[pltpu.SemaphoreType.DMA((n * npeer,)), pltpu.SemaphoreType.DMA((n * npeer,)),
                        pltpu.SemaphoreType.DMA((n,))],
    )(*bufs)


def _gather_two_level(buf, *, name):
    def body(src, out, send_sems, recv_sems, local_sem):
        x, y, c = lax.axis_index("x"), lax.axis_index("y"), lax.axis_index("c")
        me, sibling = (x, y, c), (x, y, 1 - c)
        chips = [(1 - x, y), (x, 1 - y), (1 - x, 1 - y)]

        def slab(px, py, pc):
            return out.at[4 * px + 2 * py + pc]

        def copy(k, block, to, from_src=False):
            return pltpu.make_async_remote_copy(
                src_ref=src if from_src else slab(*block), dst_ref=slab(*block),
                send_sem=send_sems.at[k], recv_sem=recv_sems.at[k],
                device_id=to, device_id_type=pl.DeviceIdType.MESH)

        mine = pltpu.make_async_copy(src, slab(*me), local_sem)
        mine.start()
        first = [copy(0, me, sibling, True)] + [copy(1 + j, me, (*chip, c), True) for j, chip in enumerate(chips)]
        for cp in first:
            cp.start()
        passed = [copy(4 + j, (*chip, c), sibling) for j, chip in enumerate(chips)]
        for j, chip in enumerate(chips):
            copy(1 + j, (*chip, c), me).wait_recv()
            passed[j].start()
        copy(0, sibling, me).wait_recv()
        for j, chip in enumerate(chips):
            copy(4 + j, (*chip, 1 - c), me).wait_recv()
        for cp in first + passed:
            cp.wait_send()
        mine.wait()

    any_spec = pl.BlockSpec(memory_space=pl.ANY)
    return pl.pallas_call(
        body, name=name, in_specs=[any_spec], out_specs=any_spec,
        out_shape=jax.ShapeDtypeStruct((N_DEV,) + buf.shape, buf.dtype),
        scratch_shapes=[pltpu.SemaphoreType.DMA((N_DEV - 1,)), pltpu.SemaphoreType.DMA((N_DEV - 1,)),
                        pltpu.SemaphoreType.DMA],
    )(buf)


_HBM = pl.BlockSpec(memory_space=pltpu.HBM)
_SEM = pl.BlockSpec(memory_space=pltpu.SEMAPHORE)
_EFFECT = pltpu.SideEffectType.DATAFLOW_SIDE_EFFECTING


def _split_copy(srcs, lands, send_sems, recv_sems, gather, k, p):
    _, me = _peer(0)
    dev, idx = _peer(k)
    sem = p * (N_DEV - 1) + k - 1
    return pltpu.make_async_remote_copy(
        src_ref=srcs[p] if gather else srcs[p].at[idx], dst_ref=lands[p].at[me],
        send_sem=send_sems.at[sem], recv_sem=recv_sems.at[sem],
        device_id=dev, device_id_type=pl.DeviceIdType.MESH)


def _exchange_start(bufs, after, *, name, gather):
    n = len(bufs)
    nsem = n * (N_DEV - 1)

    def body(*refs):
        srcs, lands = refs[:n], refs[n:2 * n]
        send_sems, recv_sems = refs[2 * n + 1], refs[2 * n + 2]
        token = refs[-1]
        for k in range(1, N_DEV):
            for p in range(n):
                _split_copy(srcs, lands, send_sems, recv_sems, gather, k, p).start()
        token[...] = jnp.zeros_like(token)

    hbm = lambda a: pltpu.with_memory_space_constraint(a, pltpu.HBM)
    lands = [lax.empty((N_DEV,) + b.shape[-2:], b.dtype) for b in bufs]
    mem = [pltpu.HBM(b.shape, b.dtype) for b in bufs] + [pltpu.HBM(l.shape, l.dtype) for l in lands]
    outs = pl.pallas_call(
        body, name=name,
        in_specs=[_HBM] * (2 * n) + [pl.BlockSpec(memory_space=pl.ANY)],
        out_specs=[_SEM, _SEM] + [_HBM] * (2 * n) + [pl.BlockSpec(memory_space=pltpu.VMEM)],
        out_shape=[pltpu.SemaphoreType.DMA((nsem,)), pltpu.SemaphoreType.DMA((nsem,))] + mem
        + [jax.ShapeDtypeStruct((SUBLANES, LANES), F32)],
        input_output_aliases={p: 2 + p for p in range(2 * n)},
        compiler_params=pltpu.CompilerParams(has_side_effects=_EFFECT),
    )(*[hbm(b) for b in bufs], *[hbm(l) for l in lands], after)
    return (outs[0], outs[1], outs[2:2 + n], outs[2 + n:2 + 2 * n]), outs[-1]


def _exchange_wait(handle, after, *, name, gather):
    send_sems, recv_sems, srcs, lands = handle
    n = len(srcs)

    def body(*refs):
        src_refs, land_refs = refs[:n], refs[n:2 * n]
        send_ref, recv_ref = refs[2 * n], refs[2 * n + 1]
        for k in range(1, N_DEV):
            for p in range(n):
                cp = _split_copy(src_refs, land_refs, send_ref, recv_ref, gather, k, p)
                cp.wait_send()
                cp.wait_recv()

    mem = [pltpu.HBM(b.shape, b.dtype) for b in srcs] + [pltpu.HBM(l.shape, l.dtype) for l in lands]
    outs = pl.pallas_call(
        body, name=name,
        in_specs=[_HBM] * (2 * n) + [_SEM, _SEM, pl.BlockSpec(memory_space=pl.ANY)],
        out_specs=[_HBM] * (2 * n), out_shape=mem,
        input_output_aliases={p: p for p in range(2 * n)},
        compiler_params=pltpu.CompilerParams(has_side_effects=_EFFECT),
    )(*srcs, *lands, send_sems, recv_sems, after)
    return outs[n:]


def _with_own(landed, own, me):
    return lax.dynamic_update_slice(landed, own[None], (me, 0, 0))


def _adamw(parts, w, m, v, *, name):
    rows, width = w.shape
    tr = _rows(rows, max(16, (128 * 1024) // width), mult=16)

    def body(p_ref, w_ref, m_ref, v_ref, g_ref, d_ref, nm_ref, nv_ref):
        g = p_ref[0].astype(F32)
        for k in range(1, N_DEV):
            g = g + p_ref[k].astype(F32)
        mn = ADAM_B1 * m_ref[...] + (1.0 - ADAM_B1) * g
        vn = ADAM_B2 * v_ref[...] + (1.0 - ADAM_B2) * jnp.square(g)
        m_hat = mn / (1.0 - ADAM_B1 ** ADAM_STEP)
        v_hat = vn / (1.0 - ADAM_B2 ** ADAM_STEP)
        g_ref[...] = g
        d_ref[...] = -ADAM_LR * (m_hat / (jnp.sqrt(v_hat) + ADAM_EPS) + ADAM_WD * w_ref[...])
        nm_ref[...] = mn
        nv_ref[...] = vn

    row = pl.BlockSpec((tr, width), lambda i: (i, 0))
    return pl.pallas_call(
        body, name=name, grid=(rows // tr,),
        in_specs=[pl.BlockSpec((N_DEV, tr, width), lambda i: (0, i, 0)), row, row, row],
        out_specs=[row] * 4, out_shape=[jax.ShapeDtypeStruct((rows, width), F32)] * 4,
        compiler_params=_cparams(("parallel",)),
    )(parts, w, m, v)


def _pack_replicated(blocks):
    flat = jnp.concatenate([blocks[name].reshape(-1).astype(F32) for name, _, _ in REPLICATED])
    flat = jnp.pad(flat, (0, PACK_ROWS * PACK_W - PACK_USED))
    return flat.reshape(PACK_ROWS, PACK_W)


def _unpack_replicated(buf):
    flat = buf.reshape(-1)
    return {name: flat[off:off + n].reshape(1, n) for name, (off, n) in PACK_TABLE.items()}


def _join_shards(seg, shape, axis):
    return seg.reshape(shape) if axis == 0 else seg.transpose(1, 0, 2).reshape(shape)


def _split_shards(g, shape, axis):
    r, c = shape
    if axis == 0:
        return g.reshape(N_DEV, r // N_DEV, c)
    return g.reshape(r, N_DEV, c // N_DEV).transpose(1, 0, 2)


def _to_residues(a, d):
    s, c = a.shape
    return a.reshape(s // d, d, c).transpose(1, 0, 2).reshape(s, c)


def _from_residues(a, d):
    s, c = a.shape
    return a.reshape(d, s // d, c).transpose(1, 0, 2).reshape(s, c)


def _pad_heads(w, heads, width, lo, hi):
    r = w.shape[0]
    w = w.reshape(r, heads, width)[:, :, lo:hi]
    w = jnp.pad(w, ((0, 0), (0, 0), (0, LANES - (hi - lo))))
    return w.reshape(r, heads * LANES)


class _NoOverlap:
    start_token = None

    def late_weights(self, w, after):
        return w

    def early_grads(self, names, grads):
        return None


def _after(vec, token):
    return vec if token is None else vec + token[0:1, 0:1]


def _local_step(x, mem, positions, target, w, hooks=_NoOverlap()):
    s = x.shape[0]
    bf = lambda a: a.astype(BF16)

    w_in = w["w_in"]
    kr_cols = jnp.pad(w_in[:, OFF_KV:OFF_KR], ((0, 0), (MLA_NOPE, LANES - MLA_QK_DIM)))
    w_a = bf(jnp.concatenate([w_in[:, :OFF_KV], kr_cols], axis=1))
    w_dm = bf(w_in[:, OFF_KR:OFF_MEMQ])
    w_g = bf(w_in[:, OFF_MEMQ:])
    w_in_t = jnp.concatenate([w_a, w_dm, w_g], axis=1).T
    tabs = _rope_tables(positions)

    h = _rms_fwd(x, _after(w["g_pre_mix"], hooks.start_token), name="rms_pre_mix", out_dtype=BF16)
    p_a = _matmul(h, w_a, name="proj_a")
    p_dm = _matmul(h, w_dm, name="proj_dm", out_dtype=BF16)
    p_g = _matmul(h, w_g, name="proj_gate")
    c_q, c_kv, kr = p_a[:, :OFF_Q], p_a[:, OFF_Q:OFF_KV], p_a[:, OFF_KV:]

    w = hooks.late_weights(w, p_g)
    wq = bf(_pad_heads(w["w_uq"], MLA_HEADS, MLA_QK_DIM, 0, MLA_QK_DIM))
    wk = bf(_pad_heads(w["w_ukv"], MLA_HEADS, MLA_NOPE + MLA_V, 0, MLA_NOPE))
    wv = bf(_pad_heads(w["w_ukv"], MLA_HEADS, MLA_NOPE + MLA_V, MLA_NOPE, MLA_NOPE + MLA_V))
    w_mkv = bf(w["w_mem_kv"])
    wb_mla = bf(jnp.pad(w["w_br_mla"].reshape(MLA_HEADS, MLA_V, D_MODEL),
                        ((0, 0), (0, LANES - MLA_V), (0, 0))).reshape(MLA_HEADS * LANES, D_MODEL))
    wb_dil, wb_mem, w_o = bf(w["w_br_dil"]), bf(w["w_br_mem"]), bf(w["w_o"])
    w_up, w_down = bf(w["w_ffn_up"]), bf(w["w_ffn_down"])
    slopes = [jnp.asarray(sl, F32) for sl in DIL_SLOPES]
    mla_scale = MLA_QK_DIM ** -0.5
    mem_scale = LANES ** -0.5
    MQ = 3 * DIL_HEADS

    qn = _rms_fwd(c_q, w["mla_q_norm"], name="rms_q", out_dtype=BF16)
    kvn = _rms_fwd(c_kv, w["mla_kv_norm"], name="rms_kv", out_dtype=BF16)
    q_raw = _matmul(qn, wq, name="mla_q_up")
    k_raw = _matmul(kvn, wk, name="mla_k_up")
    v_f = _matmul(kvn, wv, name="mla_v_up", out_dtype=BF16)
    q_f = _rope_fwd(q_raw, tabs, name="rope_q", scale=mla_scale * LOG2E)
    k_f = _rope_fwd(k_raw, tabs, name="rope_k", scale=1.0, add=kr)
    o_mla, lse_mla = _causal_fwd(q_f, k_f, v_f, name="mla_fwd", heads=MLA_HEADS, ones_lane=MLA_V)

    dil_in, dil_o, dil_lse = [], [], []
    for g, (_, d) in enumerate(DIL_PAIRS):
        if d == 1:
            arrs, offs = (p_dm, p_dm, p_dm), (4 * g, DIL_HEADS + 4 * g, 2 * DIL_HEADS + 4 * g)
        else:
            arrs = tuple(_to_residues(p_dm[:, (t * DIL_HEADS + 4 * g) * LANES:(t * DIL_HEADS + 4 * g + 4) * LANES], d)
                         for t in range(3))
            offs = (0, 0, 0)
        o_g, lse_g = _band_fwd(*arrs, slopes[g], name=f"dil_fwd_{g}", dilation=d,
                               qoff=offs[0], koff=offs[1], voff=offs[2])
        dil_in.append((arrs, offs))
        dil_o.append(_from_residues(o_g, d))
        dil_lse.append(_from_residues(lse_g, d))
    y_dil = _mix_fwd(dil_o, dil_lse, name="dil_mix")

    memn = _rms_fwd(mem, w["g_mem"], name="rms_mem", out_dtype=BF16)
    kv_m = _matmul(memn, w_mkv, name="mem_kv", out_dtype=BF16)
    memat = dict(heads=MEM_HEADS, qoff=MQ, koff=0, voff=MEM_HEADS, causal=False, scale=mem_scale, tq=512, tk=256)
    o_mem, lse_mem = _flash_fwd(p_dm, kv_m, kv_m, name="mem_fwd", **memat)

    b_mla = _matmul(o_mla, wb_mla, name="br_mla")
    b_dil = _matmul(y_dil, wb_dil, name="br_dil")
    b_mem = _matmul(o_mem, wb_mem, name="br_mem")
    merged = _gate_fwd(p_g, w["b_gate"], (b_mla, b_dil, b_mem), name="gate_fwd")
    z1 = _matmul(merged, w_o, name="out_proj")
    x1 = _rms_fwd(z1, w["g_post_mix"], name="rms_post_mix", out_dtype=F32, add=x)
    h2 = _rms_fwd(x1, w["g_pre_ffn"], name="rms_pre_ffn", out_dtype=BF16)
    u = _matmul(h2, w_up, name="ffn_up")
    zg, zv, act = _conv_fwd(u, w["conv_w"], w["conv_b"], name="conv_fwd")
    f = _matmul(act, w_down, name="ffn_down")
    dy, df, g_post_ffn_grad, sq = _loss_step(x1, f, target, w["g_post_ffn"], name="loss")
    loss = 0.5 * jnp.sum(sq) / D_MODEL

    grads = {}
    grads["g_post_ffn"] = g_post_ffn_grad
    da = _matmul(df, w_down.T, name="ffn_down_dx")
    grads["w_ffn_down"] = _matmul(act, df, name="ffn_down_dw", trans_a=True)
    du_g, du_v, cacc_g, cacc_v = _conv_bwd(da, zg, zv, u, w["conv_w"], name="conv_bwd")
    grads["conv_w"] = jnp.concatenate([cacc_g[0:3], cacc_v[0:3]], axis=1)
    grads["conv_b"] = jnp.concatenate([cacc_g[3:4], cacc_v[3:4]], axis=1)
    w_up_t = w_up.T
    dh2 = _matmul(du_g, w_up_t[:D_FF], name="ffn_up_dx_gate")
    dh2 = _matmul(du_v, w_up_t[D_FF:], name="ffn_up_dx_val", add=dh2)
    grads["w_ffn_up"] = jnp.concatenate([_matmul(h2, du_g, name="ffn_up_dw_gate", trans_a=True),
                                         _matmul(h2, du_v, name="ffn_up_dw_val", trans_a=True)], axis=1)
    tok = hooks.early_grads(("w_ffn_down", "w_ffn_up", "conv_w"), grads)
    dx1, grads["g_pre_ffn"] = _rms_bwd(x1, _after(w["g_pre_ffn"], tok), dh2, name="rms_pre_ffn_bwd",
                                       out_dtype=F32, add=dy)
    dz1, grads["g_post_mix"] = _rms_bwd(z1, w["g_post_mix"], dx1, name="rms_post_mix_bwd", out_dtype=BF16)
    dmerged = _matmul(dz1, w_o.T, name="out_proj_dx")
    grads["w_o"] = _matmul(merged, dz1, name="out_proj_dw", trans_a=True)
    db_mla, db_dil, db_mem, dgp, grads["b_gate"] = _gate_bwd(
        dmerged, p_g, w["b_gate"], (b_mla, b_dil, b_mem), name="gate_bwd")

    do_mla = _matmul(db_mla, wb_mla.T, name="br_mla_dx", out_dtype=BF16)
    g_wb_mla = _matmul(o_mla, db_mla, name="br_mla_dw", trans_a=True)
    grads["w_br_mla"] = g_wb_mla.reshape(MLA_HEADS, LANES, D_MODEL)[:, :MLA_V].reshape(MLA_HEADS * MLA_V, D_MODEL)
    delta_mla = _row_dot(do_mla, o_mla, name="mla_delta")
    dq_f, dk_f, dv_f = _causal_bwd(q_f, k_f, v_f, do_mla, lse_mla, delta_mla, name="mla_bwd", heads=MLA_HEADS)
    dq_raw = _rope_bwd(dq_f, tabs, name="rope_q_bwd", scale=mla_scale, with_add=False)
    dk_raw, dkr = _rope_bwd(dk_f, tabs, name="rope_k_bwd", scale=1.0, with_add=True)
    dqn = _matmul(dq_raw, wq.T, name="mla_q_up_dx")
    g_wq = _matmul(qn, dq_raw, name="mla_q_up_dw", trans_a=True)
    grads["w_uq"] = g_wq.reshape(MLA_Q_RANK, MLA_HEADS, LANES)[:, :, :MLA_QK_DIM].reshape(MLA_Q_RANK, -1)
    dkvn = _matmul(dk_raw, wk.T, name="mla_k_up_dx")
    dkvn = _matmul(dv_f, wv.T, name="mla_v_up_dx", add=dkvn)
    g_wk = _matmul(kvn, dk_raw, name="mla_k_up_dw", trans_a=True).reshape(MLA_KV_RANK, MLA_HEADS, LANES)
    g_wv = _matmul(kvn, dv_f, name="mla_v_up_dw", trans_a=True).reshape(MLA_KV_RANK, MLA_HEADS, LANES)
    grads["w_ukv"] = jnp.concatenate([g_wk[:, :, :MLA_NOPE], g_wv[:, :, :MLA_V]], axis=2).reshape(MLA_KV_RANK, -1)
    dc_q, grads["mla_q_norm"] = _rms_bwd(c_q, w["mla_q_norm"], dqn, name="rms_q_bwd", out_dtype=BF16)
    dc_kv, grads["mla_kv_norm"] = _rms_bwd(c_kv, w["mla_kv_norm"], dkvn, name="rms_kv_bwd", out_dtype=BF16)

    dy_dil = _matmul(db_dil, wb_dil.T, name="br_dil_dx")
    grads["w_br_dil"] = _matmul(y_dil, db_dil, name="br_dil_dw", trans_a=True)
    mix = _mix_bwd(dy_dil, dil_o, dil_lse, name="dil_mix_bwd")
    d_dil = [[None] * 3 for _ in range(3)]
    for g, (_, d) in enumerate(DIL_PAIRS):
        arrs, offs = dil_in[g]
        do_g, dl_g, lse_g = mix[g], mix[3 + g], dil_lse[g]
        if d != 1:
            do_g, dl_g, lse_g = _to_residues(do_g, d), _to_residues(dl_g, d), _to_residues(lse_g, d)
        dq_g, dk_g, dv_g = _band_bwd(*arrs, do_g, lse_g, dl_g, slopes[g], name=f"dil_bwd_{g}", dilation=d,
                                     qoff=offs[0], koff=offs[1], voff=offs[2])
        for t, a in enumerate((dq_g, dk_g, dv_g)):
            d_dil[t][g] = a if d == 1 else _from_residues(a, d)

    do_mem = _matmul(db_mem, wb_mem.T, name="br_mem_dx", out_dtype=BF16)
    grads["w_br_mem"] = _matmul(o_mem, db_mem, name="br_mem_dw", trans_a=True)
    delta_mem = _row_dot(do_mem, o_mem, name="mem_delta")
    dq_mem = _flash_bwd_dq(p_dm, kv_m, kv_m, do_mem, lse_mem, delta_mem, name="mem_bwd_dq", out_dtype=BF16, **memat)
    dk_mem, dv_mem = _flash_bwd_dkv(p_dm, kv_m, kv_m, do_mem, lse_mem, delta_mem, name="mem_bwd_dkv",
                                    dk_dtype=BF16, dv_dtype=BF16, **memat)
    dkv_m = jnp.concatenate([dk_mem, dv_mem], axis=1)
    dmemn = _matmul(dkv_m, w_mkv.T, name="mem_kv_dx")
    grads["w_mem_kv"] = _matmul(memn, dkv_m, name="mem_kv_dw", trans_a=True)
    _, grads["g_mem"] = _rms_bwd(mem, w["g_mem"], dmemn, name="rms_mem_bwd", out_dtype=BF16)

    tok = hooks.early_grads(("w_o", "w_br_mla", "w_br_dil", "w_br_mem", "w_uq", "w_ukv", "w_mem_kv"), grads)
    if tok is not None:
        dkr = dkr + tok[0:1, 0:1]
    dp_all = jnp.concatenate([dc_q, dc_kv, bf(dkr)] + d_dil[0] + d_dil[1] + d_dil[2] + [dq_mem, dgp], axis=1)
    g_in = _matmul(h, dp_all, name="proj_dw", trans_a=True)
    grads["w_in"] = jnp.concatenate(
        [g_in[:, :OFF_KV], g_in[:, OFF_KV + MLA_NOPE:OFF_KV + MLA_QK_DIM], g_in[:, N_A:]], axis=1)
    tok = hooks.early_grads(("w_in",), grads)
    dh = _matmul(dp_all, w_in_t, name="proj_dx", after=tok)
    dx, grads["g_pre_mix"] = _rms_bwd(x, w["g_pre_mix"], dh, name="rms_pre_mix_bwd", out_dtype=F32, add=dx1)
    return loss, dx, grads


def kernel(x, mem, positions, g_pre_mix, w_in, b_gate, mla_q_norm, w_uq, mla_kv_norm, w_ukv, g_mem, w_mem_kv, w_br_mla, w_br_dil, w_br_mem, w_o, g_post_mix, g_pre_ffn, w_ffn_up, conv_w, conv_b, w_ffn_down, g_post_ffn, loss_target, m_g_pre_mix, m_w_in, m_b_gate, m_mla_q_norm, m_w_uq, m_mla_kv_norm, m_w_ukv, m_g_mem, m_w_mem_kv, m_w_br_mla, m_w_br_dil, m_w_br_mem, m_w_o, m_g_post_mix, m_g_pre_ffn, m_w_ffn_up, m_conv_w, m_conv_b, m_w_ffn_down, m_g_post_ffn, v_g_pre_mix, v_w_in, v_b_gate, v_mla_q_norm, v_w_uq, v_mla_kv_norm, v_w_ukv, v_g_mem, v_w_mem_kv, v_w_br_mla, v_w_br_dil, v_w_br_mem, v_w_o, v_g_post_mix, v_g_pre_ffn, v_w_ffn_up, v_conv_w, v_conv_b, v_w_ffn_down, v_g_post_ffn):
    local = dict(g_pre_mix=g_pre_mix, w_in=w_in, b_gate=b_gate, mla_q_norm=mla_q_norm, w_uq=w_uq,
                 mla_kv_norm=mla_kv_norm, w_ukv=w_ukv, g_mem=g_mem, w_mem_kv=w_mem_kv, w_br_mla=w_br_mla,
                 w_br_dil=w_br_dil, w_br_mem=w_br_mem, w_o=w_o, g_post_mix=g_post_mix, g_pre_ffn=g_pre_ffn,
                 w_ffn_up=w_ffn_up, conv_w=conv_w, conv_b=conv_b, w_ffn_down=w_ffn_down, g_post_ffn=g_post_ffn)
    mom_m = dict(g_pre_mix=m_g_pre_mix, w_in=m_w_in, b_gate=m_b_gate, mla_q_norm=m_mla_q_norm, w_uq=m_w_uq,
                 mla_kv_norm=m_mla_kv_norm, w_ukv=m_w_ukv, g_mem=m_g_mem, w_mem_kv=m_w_mem_kv, w_br_mla=m_w_br_mla,
                 w_br_dil=m_w_br_dil, w_br_mem=m_w_br_mem, w_o=m_w_o, g_post_mix=m_g_post_mix,
                 g_pre_ffn=m_g_pre_ffn, w_ffn_up=m_w_ffn_up, conv_w=m_conv_w, conv_b=m_conv_b,
                 w_ffn_down=m_w_ffn_down, g_post_ffn=m_g_post_ffn)
    mom_v = dict(g_pre_mix=v_g_pre_mix, w_in=v_w_in, b_gate=v_b_gate, mla_q_norm=v_mla_q_norm, w_uq=v_w_uq,
                 mla_kv_norm=v_mla_kv_norm, w_ukv=v_w_ukv, g_mem=v_g_mem, w_mem_kv=v_w_mem_kv, w_br_mla=v_w_br_mla,
                 w_br_dil=v_w_br_dil, w_br_mem=v_w_br_mem, w_o=v_w_o, g_post_mix=v_g_post_mix,
                 g_pre_ffn=v_g_pre_ffn, w_ffn_up=v_w_ffn_up, conv_w=v_conv_w, conv_b=v_conv_b,
                 w_ffn_down=v_w_ffn_down, g_post_ffn=v_g_post_ffn)

    me = 4 * lax.axis_index("x") + 2 * lax.axis_index("y") + lax.axis_index("c")
    spec = {name: (shape, axis) for name, shape, axis in SHARDED}
    wire = lambda name: F32 if name == "conv_w" else BF16
    shard = {name: local[name][0].astype(wire(name)) for name in spec}
    slab = lambda name, grads: _split_shards(grads[name].astype(wire(name)), *spec[name])

    w_in_all = _gather_two_level(shard["w_in"], name="gather_w_in")
    late = tuple(name for name in spec if name != "w_in")
    late_handle, late_token = _exchange_start([shard[n] for n in late], w_in_all, name="gather_rest_start",
                                              gather=True)
    full = {"w_in": _join_shards(w_in_all, *spec["w_in"])}
    for name, _, _ in REPLICATED:
        full[name] = local[name].reshape(1, -1)

    pending = []

    class Overlap:
        start_token = late_token

        def late_weights(self, w, after):
            landed = _exchange_wait(late_handle, after, name="gather_rest_wait", gather=True)
            w = dict(w)
            for name, buf in zip(late, landed):
                w[name] = _join_shards(_with_own(buf, shard[name], me), *spec[name])
            return w

        def early_grads(self, names, grads):
            slabs = [slab(name, grads) for name in names]
            handle, token = _exchange_start(slabs, slabs[0], name="grads_start_" + names[0], gather=False)
            pending.append((names, slabs, handle))
            return token

    loss, dx, grads = _local_step(x[0], mem[0], positions[0], loss_target[0], full, Overlap())

    parts = {}
    for names, slabs, handle in pending:
        landed = _exchange_wait(handle, dx, name="grads_wait_" + names[0], gather=False)
        for name, own, buf in zip(names, slabs, landed):
            parts[name] = _with_own(buf, lax.dynamic_index_in_dim(own, me, 0, keepdims=False), me)
    rep_parts = _exchange([_pack_replicated(grads)], name="gather_replicated_grads", gather=True)[0]

    results = {}
    for name in spec:
        res = _adamw(parts[name], local[name][0], mom_m[name][0], mom_v[name][0], name="adamw_" + name)
        results[name] = [r[None] for r in res]
    rep = _adamw(rep_parts, _pack_replicated(local), _pack_replicated(mom_m), _pack_replicated(mom_v),
                 name="adamw_replicated")
    for i, buf in enumerate(rep):
        for name, val in _unpack_replicated(buf).items():
            results.setdefault(name, [None] * 4)[i] = val

    loss = lax.psum(loss, ("x", "y", "c"))
    outs = [loss, dx[None]]
    for i in range(4):
        outs.extend(results[name][i] for name in PARAM_NAMES)
    return tuple(outs)
```

```python
import functools

import numpy as np
import jax
import jax.numpy as jnp
from jax import lax
from jax.experimental import pallas as pl
from jax.experimental.pallas import tpu as pltpu

F32 = jnp.float32
BF16 = jnp.bfloat16

N_DEV = 8
D_MODEL = 1024
RMS_EPS = 1e-6
NEG_INF = -1e30
LANES = 128
SUBLANES = 8
BLOCK = 128

MLA_HEADS = 8
MLA_NOPE = 64
MLA_ROPE = 32
MLA_V = 64
MLA_QK_DIM = MLA_NOPE + MLA_ROPE
MLA_Q_RANK = 384
MLA_KV_RANK = 256
ROPE_THETA = 10000.0
DIL_PAIRS = ((128, 1), (512, 4), (2048, 16))
DIL_HEADS_PER_GROUP = 4
DIL_HEADS = 12
MEM_HEADS = 4
D_FF = 2816
OFF_Q = MLA_Q_RANK
OFF_KV = OFF_Q + MLA_KV_RANK
OFF_KR = OFF_KV + MLA_ROPE
OFF_DIL = OFF_KR + 3 * DIL_HEADS * LANES
OFF_MEMQ = OFF_DIL + MEM_HEADS * LANES
D_IN = OFF_MEMQ + 3 * D_MODEL
N_A = OFF_KV + LANES
N_DM = 3 * DIL_HEADS * LANES + MEM_HEADS * LANES

ADAM_LR = 0.001
ADAM_B1 = 0.9
ADAM_B2 = 0.999
ADAM_EPS = 1e-08
ADAM_WD = 0.01
ADAM_STEP = 10

VMEM_LIMIT = 48 * 1024 * 1024
VMEM_LIMIT_BIG = 58 * 1024 * 1024
PACK_W = 1024

_ALIBI_BASE = np.exp2(-8.0 * np.arange(1, DIL_HEADS + 1) / DIL_HEADS)
DIL_SLOPES = [[float(_ALIBI_BASE[hh * 3 + g]) for hh in range(DIL_HEADS_PER_GROUP)] for g in range(3)]

PARAMS = (
    ("g_pre_mix", (1024,), None), ("w_in", (1024, D_IN), 1), ("b_gate", (3072,), None),
    ("mla_q_norm", (384,), None), ("w_uq", (384, 768), 1), ("mla_kv_norm", (256,), None),
    ("w_ukv", (256, 1024), 1), ("g_mem", (1024,), None), ("w_mem_kv", (1024, 1024), 0),
    ("w_br_mla", (512, 1024), 1), ("w_br_dil", (512, 1024), 1), ("w_br_mem", (512, 1024), 1),
    ("w_o", (1024, 1024), 0), ("g_post_mix", (1024,), None), ("g_pre_ffn", (1024,), None),
    ("w_ffn_up", (1024, 2 * D_FF), 1), ("conv_w", (3, 2 * D_FF), 1), ("conv_b", (2 * D_FF,), None),
    ("w_ffn_down", (D_FF, 1024), 0), ("g_post_ffn", (1024,), None),
)
PARAM_NAMES = tuple(p[0] for p in PARAMS)


def _shard_shape(shape, axis):
    if axis is None:
        return shape
    return tuple(s // N_DEV if a == axis else s for a, s in enumerate(shape))


SHARDED = tuple(p for p in PARAMS if p[2] is not None)
REPLICATED = tuple(p for p in PARAMS if p[2] is None)


def _layout():
    off, table = 0, {}
    for name, shape, _ in REPLICATED:
        table[name] = (off, shape[0])
        off += shape[0]
    rows = -(-off // PACK_W)
    rows = -(-rows // SUBLANES) * SUBLANES
    return table, off, rows


PACK_TABLE, PACK_USED, PACK_ROWS = _layout()


def _pick(n, cap):
    best = None
    for t in range(LANES, min(n, cap) + 1, LANES):
        if n % t == 0:
            best = t
    return best if best is not None else n


def _rows(n, cap, mult=SUBLANES):
    best = None
    for t in range(mult, min(n, cap) + 1, mult):
        if n % t == 0:
            best = t
    return best if best is not None else n


def _cparams(sem, vmem=VMEM_LIMIT):
    return pltpu.CompilerParams(dimension_semantics=sem, vmem_limit_bytes=vmem)


def _matmul(a, b, *, name, out_dtype=F32, trans_a=False, add=None, after=None, tm=1024, tn=1408, tk=640):
    if trans_a:
        kc, m = a.shape
    else:
        m, kc = a.shape
    n = b.shape[1]
    assert b.shape[0] == kc
    tm, tn, tk = _pick(m, tm), _pick(n, tn), _pick(kc, tk)
    nk = kc // tk

    def body(*refs):
        a_ref, b_ref = refs[:2]
        c_ref = refs[2] if add is not None else None
        o_ref, acc = refs[-2:]
        k = pl.program_id(2)

        @pl.when(k == 0)
        def _():
            if add is None:
                acc[...] = jnp.zeros_like(acc)
            else:
                acc[...] = c_ref[...].astype(F32)

        av = a_ref[...].astype(BF16)
        bv = b_ref[...].astype(BF16)
        if trans_a:
            acc[...] += lax.dot_general(av, bv, (((0,), (0,)), ((), ())), preferred_element_type=F32)
        else:
            acc[...] += jnp.dot(av, bv, preferred_element_type=F32)

        @pl.when(k == nk - 1)
        def _():
            o_ref[...] = acc[...].astype(out_dtype)

    if trans_a:
        a_spec = pl.BlockSpec((tk, tm), lambda i, j, k: (k, i))
    else:
        a_spec = pl.BlockSpec((tm, tk), lambda i, j, k: (i, k))
    in_specs = [a_spec, pl.BlockSpec((tk, tn), lambda i, j, k: (k, j))]
    args = [a, b]
    if add is not None:
        in_specs.append(pl.BlockSpec((tm, tn), lambda i, j, k: (i, j)))
        args.append(add)
    if after is not None:
        in_specs.append(pl.BlockSpec(memory_space=pl.ANY))
        args.append(after)
    return pl.pallas_call(
        body, name=name, grid=(m // tm, n // tn, nk),
        in_specs=in_specs, out_specs=pl.BlockSpec((tm, tn), lambda i, j, k: (i, j)),
        out_shape=jax.ShapeDtypeStruct((m, n), out_dtype),
        scratch_shapes=[pltpu.VMEM((tm, tn), F32)],
        compiler_params=_cparams(("parallel", "parallel", "arbitrary")),
    )(*args)


def _rms_fwd(x, g, *, name, out_dtype, add=None):
    s, n = x.shape
    ts = _rows(s, 512)

    def body(*refs):
        if add is None:
            x_ref, g_ref, o_ref = refs
        else:
            x_ref, g_ref, a_ref, o_ref = refs
        xv = x_ref[...]
        r = lax.rsqrt(jnp.mean(xv * xv, axis=-1, keepdims=True) + RMS_EPS)
        y = xv * r * g_ref[...]
        if add is not None:
            y = a_ref[...] + y
        o_ref[...] = y.astype(out_dtype)

    row = pl.BlockSpec((ts, n), lambda i: (i, 0))
    in_specs = [row, pl.BlockSpec((1, n), lambda i: (0, 0))]
    args = [x, g]
    if add is not None:
        in_specs.append(row)
        args.append(add)
    return pl.pallas_call(
        body, name=name, grid=(s // ts,), in_specs=in_specs, out_specs=row,
        out_shape=jax.ShapeDtypeStruct((s, n), out_dtype),
        compiler_params=_cparams(("parallel",)),
    )(*args)


def _rms_bwd(x, g, dy, *, name, out_dtype, add=None):
    s, n = x.shape
    ts = _rows(s, 512)

    def body(*refs):
        if add is None:
            x_ref, g_ref, dy_ref, dx_ref, dg_ref = refs
        else:
            x_ref, g_ref, dy_ref, a_ref, dx_ref, dg_ref = refs
        i = pl.program_id(0)
        xv = x_ref[...]
        dyv = dy_ref[...].astype(F32)
        r = lax.rsqrt(jnp.mean(xv * xv, axis=-1, keepdims=True) + RMS_EPS)
        nx = xv * r
        gdy = dyv * g_ref[...]
        dx = r * (gdy - nx * jnp.mean(nx * gdy, axis=-1, keepdims=True))
        if add is not None:
            dx = a_ref[...] + dx
        dx_ref[...] = dx.astype(out_dtype)

        @pl.when(i == 0)
        def _():
            dg_ref[...] = jnp.zeros_like(dg_ref)

        dg_ref[...] += jnp.sum(dyv * nx, axis=0, keepdims=True)

    row = pl.BlockSpec((ts, n), lambda i: (i, 0))
    vec = pl.BlockSpec((1, n), lambda i: (0, 0))
    in_specs = [row, vec, row]
    args = [x, g, dy]
    if add is not None:
        in_specs.append(row)
        args.append(add)
    return pl.pallas_call(
        body, name=name, grid=(s // ts,), in_specs=in_specs, out_specs=[row, vec],
        out_shape=[jax.ShapeDtypeStruct((s, n), out_dtype), jax.ShapeDtypeStruct((1, n), F32)],
        compiler_params=_cparams(("arbitrary",)),
    )(*args)


def _loss_step(x1, f, target, g, *, name):
    s, n = x1.shape
    ts = _rows(s, 512)

    def body(x_ref, f_ref, t_ref, g_ref, dy_ref, df_ref, dg_ref, sq_ref):
        i = pl.program_id(0)
        fv = f_ref[...]
        gv = g_ref[...]
        r = lax.rsqrt(jnp.mean(fv * fv, axis=-1, keepdims=True) + RMS_EPS)
        nx = fv * r
        err = x_ref[...] + nx * gv - t_ref[...]
        dy = err * (1.0 / n)
        dy_ref[...] = dy
        gdy = dy * gv
        df_ref[...] = (r * (gdy - nx * jnp.mean(nx * gdy, axis=-1, keepdims=True))).astype(BF16)

        @pl.when(i == 0)
        def _():
            sq_ref[...] = jnp.zeros_like(sq_ref)
            dg_ref[...] = jnp.zeros_like(dg_ref)

        sq_ref[...] += jnp.sum(err * err, axis=0, keepdims=True)
        dg_ref[...] += jnp.sum(dy * nx, axis=0, keepdims=True)

    row = pl.BlockSpec((ts, n), lambda i: (i, 0))
    vec = pl.BlockSpec((1, n), lambda i: (0, 0))
    return pl.pallas_call(
        body, name=name, grid=(s // ts,), in_specs=[row, row, row, vec], out_specs=[row, row, vec, vec],
        out_shape=[jax.ShapeDtypeStruct((s, n), F32), jax.ShapeDtypeStruct((s, n), BF16),
                   jax.ShapeDtypeStruct((1, n), F32), jax.ShapeDtypeStruct((1, n), F32)],
        compiler_params=_cparams(("arbitrary",)),
    )(x1, f, target, g)


def _rope_tables(positions):
    half = MLA_ROPE // 2
    inv_freq = ROPE_THETA ** (-jnp.arange(half, dtype=F32) / half)
    ang = positions.astype(F32)[:, None] * inv_freq[None, :]
    cos, sin = jnp.cos(ang), jnp.sin(ang)
    s = positions.shape[0]
    one = jnp.ones((s, MLA_NOPE), F32)
    zero = jnp.zeros((s, MLA_NOPE), F32)
    pad1 = jnp.ones((s, LANES - MLA_QK_DIM), F32)
    pad0 = jnp.zeros((s, LANES - MLA_QK_DIM), F32)
    zh = jnp.zeros((s, half), F32)
    c_tab = jnp.concatenate([one, cos, cos, pad1], axis=1)
    s1_tab = jnp.concatenate([zero, -sin, zh, pad0], axis=1)
    s2_tab = jnp.concatenate([zero, zh, sin, pad0], axis=1)
    return c_tab, s1_tab, s2_tab


def _rope_fwd(x, tabs, *, name, scale, add=None):
    s, n = x.shape
    nh = n // LANES
    ts = _rows(s, 512)
    half = MLA_ROPE // 2

    def body(*refs):
        if add is None:
            x_ref, c_ref, s1_ref, s2_ref, o_ref = refs
        else:
            x_ref, a_ref, c_ref, s1_ref, s2_ref, o_ref = refs
        c, s1, s2 = c_ref[...], s1_ref[...], s2_ref[...]
        for h in range(nh):
            xh = x_ref[:, h * LANES:(h + 1) * LANES]
            if add is not None:
                xh = xh + a_ref[...]
            y = xh * c + pltpu.roll(xh, LANES - half, 1) * s1 + pltpu.roll(xh, half, 1) * s2
            o_ref[:, h * LANES:(h + 1) * LANES] = (y * scale).astype(BF16)

    row = pl.BlockSpec((ts, n), lambda i: (i, 0))
    tab = pl.BlockSpec((ts, LANES), lambda i: (i, 0))
    in_specs = [row] + ([tab] if add is not None else []) + [tab, tab, tab]
    args = [x] + ([add] if add is not None else []) + list(tabs)
    return pl.pallas_call(
        body, name=name, grid=(s // ts,), in_specs=in_specs, out_specs=row,
        out_shape=jax.ShapeDtypeStruct((s, n), BF16),
        compiler_params=_cparams(("parallel",)),
    )(*args)


def _rope_bwd(dy, tabs, *, name, scale, with_add):
    s, n = dy.shape
    nh = n // LANES
    ts = _rows(s, 512)
    half = MLA_ROPE // 2

    def body(*refs):
        if with_add:
            dy_ref, c_ref, s1_ref, s2_ref, dx_ref, da_ref = refs
        else:
            dy_ref, c_ref, s1_ref, s2_ref, dx_ref = refs
        c, s1, s2 = c_ref[...], s1_ref[...], s2_ref[...]
        tot = None
        for h in range(nh):
            g = dy_ref[:, h * LANES:(h + 1) * LANES].astype(F32)
            dx = (g * c + pltpu.roll(g * s1, half, 1) + pltpu.roll(g * s2, LANES - half, 1)) * scale
            dx_ref[:, h * LANES:(h + 1) * LANES] = dx.astype(BF16)
            tot = dx if tot is None else tot + dx
        if with_add:
            da_ref[...] = tot

    row = pl.BlockSpec((ts, n), lambda i: (i, 0))
    tab = pl.BlockSpec((ts, LANES), lambda i: (i, 0))
    out_specs = [row, tab] if with_add else row
    out_shape = [jax.ShapeDtypeStruct((s, n), BF16)]
    if with_add:
        out_shape.append(jax.ShapeDtypeStruct((s, LANES), F32))
    else:
        out_shape = out_shape[0]
    return pl.pallas_call(
        body, name=name, grid=(s // ts,), in_specs=[row, tab, tab, tab], out_specs=out_specs,
        out_shape=out_shape, compiler_params=_cparams(("parallel",)),
    )(dy, *tabs)


def _scores(q, k, scale, diag):
    s = lax.dot_general(q, k, (((1,), (1,)), ((), ())), preferred_element_type=F32)
    if scale != 1.0:
        s = s * scale
    if diag:
        rows = lax.broadcasted_iota(jnp.int32, s.shape, 0)
        cols = lax.broadcasted_iota(jnp.int32, s.shape, 1)
        s = jnp.where(cols <= rows, s, NEG_INF)
    return s


def _flash_fwd(q, k, v, *, name, heads, qoff, koff, voff, causal, scale, tq, tk):
    s_q, s_kv = q.shape[0], k.shape[0]
    tq, tk = min(tq, s_q), min(tk, s_kv)
    nq, nk = s_q // tq, s_kv // tk
    if causal:
        assert tq == tk and s_q == s_kv

    def body(q_ref, k_ref, v_ref, o_ref, lse_ref, m_s, l_s, acc):
        i, j = pl.program_id(1), pl.program_id(2)

        @pl.when(j == 0)
        def _():
            m_s[...] = jnp.full_like(m_s, NEG_INF)
            l_s[...] = jnp.zeros_like(l_s)
            acc[...] = jnp.zeros_like(acc)

        def step(diag):
            s = _scores(q_ref[...], k_ref[...], scale, diag)
            m_prev = m_s[...]
            m_cur = jnp.maximum(m_prev, jnp.max(s, axis=1, keepdims=True))
            alpha = jnp.exp(m_prev - m_cur)
            p = jnp.exp(s - m_cur[:, :1])
            l_s[...] = alpha * l_s[...] + jnp.sum(p, axis=1, keepdims=True)
            acc[...] = alpha * acc[...] + jnp.dot(p.astype(BF16), v_ref[...], preferred_element_type=F32)
            m_s[...] = m_cur

        def finish():
            o_ref[...] = (acc[...] / l_s[...]).astype(o_ref.dtype)
            lse_ref[...] = m_s[...] + jnp.log(l_s[...])

        if causal:
            @pl.when(j < i)
            def _():
                step(False)

            @pl.when(j == i)
            def _():
                step(True)
                finish()
        else:
            step(False)

            @pl.when(j == nk - 1)
            def _():
                finish()

    def kv_idx(off):
        if causal:
            return lambda h, i, j: (jnp.minimum(j, i), off + h)
        return lambda h, i, j: (j, off + h)

    blk_q = pl.BlockSpec((tq, LANES), lambda h, i, j: (i, qoff + h))
    out_q = pl.BlockSpec((tq, LANES), lambda h, i, j: (i, h))
    return pl.pallas_call(
        body, name=name, grid=(heads, nq, nk),
        in_specs=[blk_q, pl.BlockSpec((tk, LANES), kv_idx(koff)), pl.BlockSpec((tk, LANES), kv_idx(voff))],
        out_specs=[out_q, out_q],
        out_shape=[jax.ShapeDtypeStruct((s_q, heads * LANES), BF16),
                   jax.ShapeDtypeStruct((s_q, heads * LANES), F32)],
        scratch_shapes=[pltpu.VMEM((tq, LANES), F32)] * 3,
        compiler_params=_cparams(("parallel", "parallel", "arbitrary")),
    )(q, k, v)


def _flash_bwd_dq(q, k, v, do, lse, delta, *, name, heads, qoff, koff, voff, causal, scale, tq, tk, out_dtype):
    s_q, s_kv = q.shape[0], k.shape[0]
    tq, tk = min(tq, s_q), min(tk, s_kv)
    nq, nk = s_q // tq, s_kv // tk

    def body(q_ref, k_ref, v_ref, do_ref, lse_ref, dl_ref, dq_ref, acc):
        i, j = pl.program_id(1), pl.program_id(2)

        @pl.when(j == 0)
        def _():
            acc[...] = jnp.zeros_like(acc)

        def step(diag):
            s = _scores(q_ref[...], k_ref[...], scale, diag)
            p = jnp.exp(s - lse_ref[:, :1])
            dp = lax.dot_general(do_ref[...], v_ref[...], (((1,), (1,)), ((), ())), preferred_element_type=F32)
            ds = p * (dp - dl_ref[:, :1])
            acc[...] += jnp.dot(ds.astype(BF16), k_ref[...], preferred_element_type=F32)

        def finish():
            dq_ref[...] = (acc[...] * scale).astype(out_dtype)

        if causal:
            @pl.when(j < i)
            def _():
                step(False)

            @pl.when(j == i)
            def _():
                step(True)
                finish()
        else:
            step(False)

            @pl.when(j == nk - 1)
            def _():
                finish()

    def kv_idx(off):
        if causal:
            return lambda h, i, j: (jnp.minimum(j, i), off + h)
        return lambda h, i, j: (j, off + h)

    blk_q = pl.BlockSpec((tq, LANES), lambda h, i, j: (i, qoff + h))
    blk_h = pl.BlockSpec((tq, LANES), lambda h, i, j: (i, h))
    return pl.pallas_call(
        body, name=name, grid=(heads, nq, nk),
        in_specs=[blk_q, pl.BlockSpec((tk, LANES), kv_idx(koff)), pl.BlockSpec((tk, LANES), kv_idx(voff)),
                  blk_h, blk_h, blk_h],
        out_specs=blk_h,
        out_shape=jax.ShapeDtypeStruct((s_q, heads * LANES), out_dtype),
        scratch_shapes=[pltpu.VMEM((tq, LANES), F32)],
        compiler_params=_cparams(("parallel", "parallel", "arbitrary")),
    )(q, k, v, do, lse, delta)


def _flash_bwd_dkv(q, k, v, do, lse, delta, *, name, heads, qoff, koff, voff, causal, scale, tq, tk,
                   dk_dtype, dv_dtype):
    s_q, s_kv = q.shape[0], k.shape[0]
    tq, tk = min(tq, s_q), min(tk, s_kv)
    nq, nk = s_q // tq, s_kv // tk

    def body(q_ref, k_ref, v_ref, do_ref, lse_ref, dl_ref, dk_ref, dv_ref, dk_acc, dv_acc):
        j, i = pl.program_id(1), pl.program_id(2)

        @pl.when(i == 0)
        def _():
            dk_acc[...] = jnp.zeros_like(dk_acc)
            dv_acc[...] = jnp.zeros_like(dv_acc)

        def step(diag):
            s = _scores(q_ref[...], k_ref[...], scale, diag)
            p = jnp.exp(s - lse_ref[:, :1])
            dov = do_ref[...]
            dp = lax.dot_general(dov, v_ref[...], (((1,), (1,)), ((), ())), preferred_element_type=F32)
            ds = p * (dp - dl_ref[:, :1])
            dv_acc[...] += lax.dot_general(p.astype(BF16), dov, (((0,), (0,)), ((), ())),
                                           preferred_element_type=F32)
            dk_acc[...] += lax.dot_general(ds.astype(BF16), q_ref[...], (((0,), (0,)), ((), ())),
                                           preferred_element_type=F32)

        if causal:
            @pl.when(i > j)
            def _():
                step(False)

            @pl.when(i == j)
            def _():
                step(True)
        else:
            step(False)

        @pl.when(i == nq - 1)
        def _():
            dk_ref[...] = (dk_acc[...] * scale).astype(dk_dtype)
            dv_ref[...] = dv_acc[...].astype(dv_dtype)

    def q_idx(off):
        if causal:
            return lambda h, j, i: (jnp.maximum(i, j), off + h)
        return lambda h, j, i: (i, off + h)

    blk_h = pl.BlockSpec((tq, LANES), q_idx(0))
    out_k = pl.BlockSpec((tk, LANES), lambda h, j, i: (j, h))
    return pl.pallas_call(
        body, name=name, grid=(heads, nk, nq),
        in_specs=[pl.BlockSpec((tq, LANES), q_idx(qoff)),
                  pl.BlockSpec((tk, LANES), lambda h, j, i: (j, koff + h)),
                  pl.BlockSpec((tk, LANES), lambda h, j, i: (j, voff + h)),
                  blk_h, blk_h, blk_h],
        out_specs=[out_k, out_k],
        out_shape=[jax.ShapeDtypeStruct((s_kv, heads * LANES), dk_dtype),
                   jax.ShapeDtypeStruct((s_kv, heads * LANES), dv_dtype)],
        scratch_shapes=[pltpu.VMEM((tk, LANES), F32)] * 2,
        compiler_params=_cparams(("parallel", "parallel", "arbitrary")),
    )(q, k, v, do, lse, delta)


def _row_dot(a, b, *, name):
    s, n = a.shape
    nh = n // LANES
    ts = _rows(s, 512)

    def body(a_ref, b_ref, o_ref):
        for h in range(nh):
            sl = slice(h * LANES, (h + 1) * LANES)
            d = jnp.sum(a_ref[:, sl].astype(F32) * b_ref[:, sl].astype(F32), axis=1, keepdims=True)
            o_ref[:, sl] = jnp.broadcast_to(d, (ts, LANES))

    row = pl.BlockSpec((ts, n), lambda i: (i, 0))
    return pl.pallas_call(
        body, name=name, grid=(s // ts,), in_specs=[row, row], out_specs=row,
        out_shape=jax.ShapeDtypeStruct((s, n), F32), compiler_params=_cparams(("parallel",)),
    )(a, b)


CAUSAL_T = 512
LOG2E = 1.4426950408889634
LN2 = 0.6931471805599453


def _causal_fwd(q, k, v, *, name, heads, ones_lane):
    s = q.shape[0]
    t = CAUSAL_T
    nq = s // (2 * t)
    assert nq * 2 * t == s

    def body(q_ref, k_ref, v_ref, o_ref, lse_ref, v1, m_s, acc):
        i = pl.program_id(1)

        @pl.when(i == 0)
        def _():
            lane = lax.broadcasted_iota(jnp.int32, v1.shape, 1)
            v1[...] = jnp.where(lane == ones_lane, 1.0, v_ref[...]).astype(BF16)

        m_s[...] = jnp.full_like(m_s, NEG_INF)
        acc[...] = jnp.zeros_like(acc)
        halves = (q_ref[0:t, :], q_ref[t:2 * t, :])

        def raw(c, j):
            rows = pl.ds(pl.multiple_of(j * t, t), t)
            return lax.dot_general(halves[c], k_ref[rows, :], (((1,), (1,)), ((), ())), preferred_element_type=F32)

        def update(c, sc, j):
            m_prev = m_s[c]
            m_cur = jnp.maximum(m_prev, jnp.max(sc, axis=1, keepdims=True))
            p = jnp.exp2(sc - m_cur[:, :1]).astype(BF16)
            acc[c] = jnp.exp2(m_prev - m_cur) * acc[c] + jnp.dot(
                p, v1[pl.ds(pl.multiple_of(j * t, t), t), :], preferred_element_type=F32)
            m_s[c] = m_cur

        def loop(j, carry):
            sa, sb = raw(0, j), raw(1, j)
            update(0, sa, j)
            update(1, sb, j)
            return carry

        lax.fori_loop(0, 2 * i, loop, 0)
        sa, sb = raw(0, 2 * i), raw(1, 2 * i)
        below = (lax.broadcasted_iota(jnp.int32, sa.shape, 1) <= lax.broadcasted_iota(jnp.int32, sa.shape, 0))
        update(0, jnp.where(below, sa, NEG_INF), 2 * i)
        update(1, sb, 2 * i)
        update(1, jnp.where(below, raw(1, 2 * i + 1), NEG_INF), 2 * i + 1)
        lane = lax.broadcasted_iota(jnp.int32, (t, LANES), 1)
        for c in range(2):
            out = acc[c]
            den = out[:, ones_lane:ones_lane + 1]
            o_ref[c * t:(c + 1) * t, :] = jnp.where(lane == ones_lane, 0.0, out / den).astype(BF16)
            lse_ref[c * t:(c + 1) * t, :] = m_s[c] + jnp.log2(den)

    blk = pl.BlockSpec((2 * t, LANES), lambda h, i: (i, h))
    full = pl.BlockSpec((s, LANES), lambda h, i: (0, h))
    return pl.pallas_call(
        body, name=name, grid=(heads, nq), in_specs=[blk, full, full], out_specs=[blk, blk],
        out_shape=[jax.ShapeDtypeStruct((s, heads * LANES), BF16), jax.ShapeDtypeStruct((s, heads * LANES), F32)],
        scratch_shapes=[pltpu.VMEM((s, LANES), BF16), pltpu.VMEM((2, t, LANES), F32),
                        pltpu.VMEM((2, t, LANES), F32)],
        compiler_params=_cparams(("parallel", "arbitrary")),
    )(q, k, v)


def _causal_bwd(q, k, v, do, lse, delta, *, name, heads):
    s = q.shape[0]
    t = min(CAUSAL_T, s)
    nt = s // t

    def body(q_ref, k_ref, v_ref, do_ref, lse_ref, dl_ref, dq_ref, dk_ref, dv_ref, dk_acc, dv_acc):
        j = pl.program_id(1)

        @pl.when(j == 0)
        def _():
            dq_ref[...] = jnp.zeros_like(dq_ref)

        dk_acc[...] = jnp.zeros_like(dk_acc)
        dv_acc[...] = jnp.zeros_like(dv_acc)
        kv, vv = k_ref[...], v_ref[...]

        def step(i, diag, size=t):
            rows = pl.ds(pl.multiple_of(i * t, t), size)
            qv, dov = q_ref[rows, :], do_ref[rows, :]
            sc = _scores(qv, kv, 1.0, diag)
            p = jnp.exp2(sc - lse_ref[rows, :][:, :1])
            dp = lax.dot_general(dov, vv, (((1,), (1,)), ((), ())), preferred_element_type=F32)
            ds = (p * (dp - dl_ref[rows, :][:, :1])).astype(BF16)
            dv_acc[...] += lax.dot_general(p.astype(BF16), dov, (((0,), (0,)), ((), ())),
                                           preferred_element_type=F32)
            dk_acc[...] += lax.dot_general(ds, qv, (((0,), (0,)), ((), ())), preferred_element_type=F32)
            dq_ref[rows, :] += jnp.dot(ds, kv, preferred_element_type=F32)

        step(j, True)
        odd = (nt - 1 - j) % 2

        @pl.when(odd == 1)
        def _():
            step(j + 1, False)

        def loop(n, carry):
            step(j + 1 + odd + 2 * n, False, 2 * t)
            return carry

        lax.fori_loop(0, (nt - 1 - j) // 2, loop, 0)
        dk_ref[...] = dk_acc[...] * LN2
        dv_ref[...] = dv_acc[...].astype(BF16)

    blk = pl.BlockSpec((t, LANES), lambda h, j: (j, h))
    full = pl.BlockSpec((s, LANES), lambda h, j: (0, h))
    return pl.pallas_call(
        body, name=name, grid=(heads, nt), in_specs=[full, blk, blk, full, full, full],
        out_specs=[full, blk, blk],
        out_shape=[jax.ShapeDtypeStruct((s, heads * LANES), F32), jax.ShapeDtypeStruct((s, heads * LANES), F32),
                   jax.ShapeDtypeStruct((s, heads * LANES), BF16)],
        scratch_shapes=[pltpu.VMEM((t, LANES), F32)] * 2,
        compiler_params=_cparams(("parallel", "arbitrary")),
    )(q, k, v, do, lse, delta)


def _band_masks(dilation, slope):
    qi = lax.broadcasted_iota(jnp.int32, (BLOCK, 2 * BLOCK), 0)
    kj = lax.broadcasted_iota(jnp.int32, (BLOCK, 2 * BLOCK), 1)
    dist = qi + BLOCK - kj
    valid = (dist >= 0) & (dist <= BLOCK)
    bias = -slope * (dist * dilation).astype(F32)
    return valid, bias


BAND_UNROLL = 16


def _aligned(start):
    return start if isinstance(start, int) else pl.multiple_of(start, BLOCK)


def _band_fwd(q, k, v, slopes, *, name, dilation, qoff, koff, voff):
    s = q.shape[0]
    sub = s // dilation
    nb = sub // BLOCK
    assert nb * BLOCK == sub
    unroll = min(BAND_UNROLL, nb)
    assert nb % unroll == 0
    scale = LANES ** -0.5

    def body(sl_ref, q_ref, k_ref, v_ref, o_ref, lse_ref):
        slope = sl_ref[pl.program_id(0)]
        valid2, bias2 = _band_masks(dilation, slope)
        valid1, bias1 = valid2[:, BLOCK:], bias2[:, BLOCK:]

        def block(start_q, kk, vv, valid, bias):
            qb = q_ref[pl.ds(start_q, BLOCK), :]
            sc = lax.dot_general(qb, kk, (((1,), (1,)), ((), ())), preferred_element_type=F32) * scale
            sc = jnp.where(valid, sc + bias, NEG_INF)
            m = jnp.max(sc, axis=1, keepdims=True)
            e = jnp.exp(sc - m)
            den = jnp.sum(e, axis=1, keepdims=True)
            p = (e / den).astype(BF16)
            o_ref[pl.ds(start_q, BLOCK), :] = jnp.dot(p, vv, preferred_element_type=F32)
            lse_ref[pl.ds(start_q, BLOCK), :] = jnp.broadcast_to(m + jnp.log(den), (BLOCK, LANES))

        block(0, k_ref[0:BLOCK, :], v_ref[0:BLOCK, :], valid1, bias1)

        def general(jj):
            start_q, start_k = _aligned(jj * BLOCK), _aligned((jj - 1) * BLOCK)
            block(start_q, k_ref[pl.ds(start_k, 2 * BLOCK), :], v_ref[pl.ds(start_k, 2 * BLOCK), :], valid2, bias2)

        for jj in range(1, unroll):
            general(jj)

        def loop(t, carry):
            for u in range(unroll):
                general(t * unroll + u)
            return carry

        lax.fori_loop(1, nb // unroll, loop, 0)

    def spec(off):
        return pl.BlockSpec((sub, LANES), lambda h, r: (r, off + h))

    out = pl.BlockSpec((sub, LANES), lambda h, r: (r, h))
    return pl.pallas_call(
        body, name=name, grid=(DIL_HEADS_PER_GROUP, dilation),
        in_specs=[pl.BlockSpec(memory_space=pltpu.SMEM), spec(qoff), spec(koff), spec(voff)],
        out_specs=[out, out],
        out_shape=[jax.ShapeDtypeStruct((s, DIL_HEADS_PER_GROUP * LANES), F32)] * 2,
        compiler_params=_cparams(("parallel", "parallel"), VMEM_LIMIT_BIG),
    )(slopes, q, k, v)


def _band_bwd(q, k, v, do, lse, delta, slopes, *, name, dilation, qoff, koff, voff):
    s = q.shape[0]
    sub = s // dilation
    nb = sub // BLOCK
    unroll = min(BAND_UNROLL, nb)
    scale = LANES ** -0.5

    def body(sl_ref, q_ref, k_ref, v_ref, do_ref, lse_ref, dl_ref, dq_ref, dk_ref, dv_ref):
        slope = sl_ref[pl.program_id(0)]
        valid2, bias2 = _band_masks(dilation, slope)
        valid1, bias1 = valid2[:, BLOCK:], bias2[:, BLOCK:]

        def block(start_q, kk, vv, valid, bias):
            qb = q_ref[pl.ds(start_q, BLOCK), :]
            dob = do_ref[pl.ds(start_q, BLOCK), :]
            sc = lax.dot_general(qb, kk, (((1,), (1,)), ((), ())), preferred_element_type=F32) * scale
            sc = jnp.where(valid, sc + bias, NEG_INF)
            p = jnp.exp(sc - lse_ref[pl.ds(start_q, BLOCK), :][:, :1])
            dp = lax.dot_general(dob, vv, (((1,), (1,)), ((), ())), preferred_element_type=F32)
            ds = (p * (dp - dl_ref[pl.ds(start_q, BLOCK), :][:, :1])).astype(BF16)
            dq = jnp.dot(ds, kk, preferred_element_type=F32) * scale
            dq_ref[pl.ds(start_q, BLOCK), :] = dq.astype(BF16)
            dkk = lax.dot_general(ds, qb, (((0,), (0,)), ((), ())), preferred_element_type=F32) * scale
            dvv = lax.dot_general(p.astype(BF16), dob, (((0,), (0,)), ((), ())), preferred_element_type=F32)
            return dkk, dvv

        carry0 = block(0, k_ref[0:BLOCK, :], v_ref[0:BLOCK, :], valid1, bias1)

        def general(jj, carry):
            dk_part, dv_part = carry
            start_q, start_k = _aligned(jj * BLOCK), _aligned((jj - 1) * BLOCK)
            dkk, dvv = block(start_q, k_ref[pl.ds(start_k, 2 * BLOCK), :], v_ref[pl.ds(start_k, 2 * BLOCK), :],
                             valid2, bias2)
            dk_ref[pl.ds(start_k, BLOCK), :] = (dk_part + dkk[:BLOCK]).astype(BF16)
            dv_ref[pl.ds(start_k, BLOCK), :] = (dv_part + dvv[:BLOCK]).astype(BF16)
            return dkk[BLOCK:], dvv[BLOCK:]

        for jj in range(1, unroll):
            carry0 = general(jj, carry0)

        def loop(t, carry):
            for u in range(unroll):
                carry = general(t * unroll + u, carry)
            return carry

        dk_last, dv_last = lax.fori_loop(1, nb // unroll, loop, carry0)
        dk_ref[(nb - 1) * BLOCK:nb * BLOCK, :] = dk_last.astype(BF16)
        dv_ref[(nb - 1) * BLOCK:nb * BLOCK, :] = dv_last.astype(BF16)

    def spec(off):
        return pl.BlockSpec((sub, LANES), lambda h, r: (r, off + h))

    out = spec(0)
    return pl.pallas_call(
        body, name=name, grid=(DIL_HEADS_PER_GROUP, dilation),
        in_specs=[pl.BlockSpec(memory_space=pltpu.SMEM), spec(qoff), spec(koff), spec(voff), out, out, out],
        out_specs=[out, out, out],
        out_shape=[jax.ShapeDtypeStruct((s, DIL_HEADS_PER_GROUP * LANES), BF16)] * 3,
        compiler_params=_cparams(("parallel", "parallel"), VMEM_LIMIT_BIG),
    )(slopes, q, k, v, do, lse, delta)


def _mix_fwd(outs, lses, *, name):
    s, n = outs[0].shape
    ts = _rows(s, 512)

    def body(o0, o1, o2, l0, l1, l2, y_ref):
        la, lb, lc = l0[...], l1[...], l2[...]
        m = jnp.maximum(jnp.maximum(la, lb), lc)
        ea, eb, ec = jnp.exp(la - m), jnp.exp(lb - m), jnp.exp(lc - m)
        den = ea + eb + ec
        y = (ea / den) * o0[...] + (eb / den) * o1[...] + (ec / den) * o2[...]
        y_ref[...] = y.astype(BF16)

    row = pl.BlockSpec((ts, n), lambda i: (i, 0))
    return pl.pallas_call(
        body, name=name, grid=(s // ts,), in_specs=[row] * 6, out_specs=row,
        out_shape=jax.ShapeDtypeStruct((s, n), BF16), compiler_params=_cparams(("parallel",)),
    )(*outs, *lses)


def _mix_bwd(dy, outs, lses, *, name):
    s, n = dy.shape
    nh = n // LANES
    ts = _rows(s, 256)

    def body(dy_ref, o0, o1, o2, l0, l1, l2, d0, d1, d2, e0, e1, e2):
        la, lb, lc = l0[...], l1[...], l2[...]
        m = jnp.maximum(jnp.maximum(la, lb), lc)
        ea, eb, ec = jnp.exp(la - m), jnp.exp(lb - m), jnp.exp(lc - m)
        den = ea + eb + ec
        wa, wb, wc = ea / den, eb / den, ec / den
        dyv = dy_ref[...]
        y = wa * o0[...] + wb * o1[...] + wc * o2[...]
        prod = dyv * y
        d0[...] = (wa * dyv).astype(BF16)
        d1[...] = (wb * dyv).astype(BF16)
        d2[...] = (wc * dyv).astype(BF16)
        for h in range(nh):
            sl = slice(h * LANES, (h + 1) * LANES)
            t = jnp.sum(prod[:, sl], axis=1, keepdims=True)
            e0[:, sl] = wa[:, sl] * t
            e1[:, sl] = wb[:, sl] * t
            e2[:, sl] = wc[:, sl] * t

    row = pl.BlockSpec((ts, n), lambda i: (i, 0))
    return pl.pallas_call(
        body, name=name, grid=(s // ts,), in_specs=[row] * 7, out_specs=[row] * 6,
        out_shape=[jax.ShapeDtypeStruct((s, n), BF16)] * 3 + [jax.ShapeDtypeStruct((s, n), F32)] * 3,
        compiler_params=_cparams(("parallel",)),
    )(dy, *outs, *lses)


def _gate_fwd(gp, b_gate, branches, *, name):
    s = gp.shape[0]
    ts = _rows(s, 256)

    def body(gp_ref, b_ref, b0, b1, b2, o_ref):
        tot = None
        for i, br in enumerate((b0, b1, b2)):
            sl = slice(i * D_MODEL, (i + 1) * D_MODEL)
            t = jax.nn.sigmoid(gp_ref[:, sl] + b_ref[:, sl]) * br[...]
            tot = t if tot is None else tot + t
        o_ref[...] = tot.astype(BF16)

    row = pl.BlockSpec((ts, D_MODEL), lambda i: (i, 0))
    return pl.pallas_call(
        body, name=name, grid=(s // ts,),
        in_specs=[pl.BlockSpec((ts, 3 * D_MODEL), lambda i: (i, 0)), pl.BlockSpec((1, 3 * D_MODEL), lambda i: (0, 0)),
                  row, row, row],
        out_specs=row, out_shape=jax.ShapeDtypeStruct((s, D_MODEL), BF16),
        compiler_params=_cparams(("parallel",)),
    )(gp, b_gate, *branches)


def _gate_bwd(dm, gp, b_gate, branches, *, name):
    s = gp.shape[0]
    ts = _rows(s, 256)

    def body(dm_ref, gp_ref, b_ref, b0, b1, b2, d0, d1, d2, dgp_ref, db_ref):
        i = pl.program_id(0)

        @pl.when(i == 0)
        def _():
            db_ref[...] = jnp.zeros_like(db_ref)

        dmv = dm_ref[...]
        for k, (br, dbr) in enumerate(((b0, d0), (b1, d1), (b2, d2))):
            sl = slice(k * D_MODEL, (k + 1) * D_MODEL)
            sg = jax.nn.sigmoid(gp_ref[:, sl] + b_ref[:, sl])
            dbr[...] = (dmv * sg).astype(BF16)
            dg = dmv * br[...] * sg * (1.0 - sg)
            dgp_ref[:, sl] = dg.astype(BF16)
            db_ref[:, sl] += jnp.sum(dg, axis=0, keepdims=True)

    row = pl.BlockSpec((ts, D_MODEL), lambda i: (i, 0))
    wide = pl.BlockSpec((ts, 3 * D_MODEL), lambda i: (i, 0))
    vec = pl.BlockSpec((1, 3 * D_MODEL), lambda i: (0, 0))
    return pl.pallas_call(
        body, name=name, grid=(s // ts,),
        in_specs=[row, wide, vec, row, row, row], out_specs=[row, row, row, wide, vec],
        out_shape=[jax.ShapeDtypeStruct((s, D_MODEL), BF16)] * 3
        + [jax.ShapeDtypeStruct((s, 3 * D_MODEL), BF16), jax.ShapeDtypeStruct((1, 3 * D_MODEL), F32)],
        compiler_params=_cparams(("arbitrary",)),
    )(dm, gp, b_gate, *branches)


CONV_TC = 1408


def _shift_down(x, halo, k):
    rolled = pltpu.roll(x, k, 0)
    r8 = lax.broadcasted_iota(jnp.int32, halo.shape, 0)
    top = jnp.where(r8 < k, pltpu.roll(halo, k, 0), rolled[:SUBLANES])
    return jnp.concatenate([top, rolled[SUBLANES:]], axis=0)


def _shift_up(x, halo, k):
    n = x.shape[0]
    rolled = pltpu.roll(x, n - k, 0)
    r8 = lax.broadcasted_iota(jnp.int32, halo.shape, 0)
    bot = jnp.where(r8 >= SUBLANES - k, pltpu.roll(halo, SUBLANES - k, 0), rolled[n - SUBLANES:])
    return jnp.concatenate([rolled[:n - SUBLANES], bot], axis=0)


def _conv_fwd(u, conv_w, conv_b, *, name):
    s = u.shape[0]
    ts = _rows(s, 256)
    nct = D_FF // CONV_TC
    per8 = ts // SUBLANES

    def body(ug, uv, hg, hv, wg, wv, bg, bv, zg_ref, zv_ref, a_ref):
        first = pl.program_id(1) == 0

        def conv(u_ref, h_ref, w_ref, b_ref):
            x = u_ref[...]
            halo = jnp.where(first, 0.0, h_ref[...])
            z = b_ref[...] + w_ref[0:1, :] * _shift_down(x, halo, 2)
            z = z + w_ref[1:2, :] * _shift_down(x, halo, 1)
            return z + w_ref[2:3, :] * x

        zg = conv(ug, hg, wg, bg)
        zv = conv(uv, hv, wv, bv)
        zg_ref[...] = zg
        zv_ref[...] = zv
        a_ref[...] = (zg * jax.nn.sigmoid(zg) * zv).astype(BF16)

    def col(off):
        return pl.BlockSpec((ts, CONV_TC), lambda c, i: (i, c + off))

    def halo(off):
        return pl.BlockSpec((SUBLANES, CONV_TC), lambda c, i: (jnp.maximum(i * per8 - 1, 0), c + off))

    def wspec(rows, off):
        return pl.BlockSpec((rows, CONV_TC), lambda c, i: (0, c + off))

    zg, zv, a = pl.pallas_call(
        body, name=name, grid=(nct, s // ts),
        in_specs=[col(0), col(nct), halo(0), halo(nct), wspec(3, 0), wspec(3, nct), wspec(1, 0), wspec(1, nct)],
        out_specs=[col(0), col(0), col(0)],
        out_shape=[jax.ShapeDtypeStruct((s, D_FF), F32)] * 2 + [jax.ShapeDtypeStruct((s, D_FF), BF16)],
        compiler_params=_cparams(("parallel", "parallel")),
    )(u, u, u, u, conv_w, conv_w, conv_b, conv_b)
    return zg, zv, a


def _conv_bwd(da, zg, zv, u, conv_w, *, name):
    s = da.shape[0]
    ts = _rows(s, 256)
    nct = D_FF // CONV_TC
    per8 = ts // SUBLANES
    nrow = s // ts
    last8 = s // SUBLANES - 1

    def dz_of(dav, g, val):
        sg = jax.nn.sigmoid(g)
        return dav * val * sg * (1.0 + g * (1.0 - sg)), dav * g * sg

    def body(da_ref, zg_ref, zv_ref, da_nx, zg_nx, zv_nx, ug_ref, uv_ref, ug_pv, uv_pv, wg_ref, wv_ref,
             dug_ref, duv_ref, accg_ref, accv_ref):
        i = pl.program_id(1)
        dzg, dzv = dz_of(da_ref[...], zg_ref[...], zv_ref[...])
        da_next = jnp.where(i == nrow - 1, 0.0, da_nx[...])
        nxg, nxv = dz_of(da_next, zg_nx[...], zv_nx[...])

        @pl.when(i == 0)
        def _():
            accg_ref[...] = jnp.zeros_like(accg_ref)
            accv_ref[...] = jnp.zeros_like(accv_ref)

        for dz, nxt, u_ref, pv_ref, w_ref, du_ref, acc_ref in (
                (dzg, nxg, ug_ref, ug_pv, wg_ref, dug_ref, accg_ref),
                (dzv, nxv, uv_ref, uv_pv, wv_ref, duv_ref, accv_ref)):
            du = w_ref[2:3, :] * dz + w_ref[1:2, :] * _shift_up(dz, nxt, 1) + w_ref[0:1, :] * _shift_up(dz, nxt, 2)
            du_ref[...] = du.astype(BF16)
            x = u_ref[...]
            prev = jnp.where(i == 0, 0.0, pv_ref[...])
            acc_ref[0:1, :] += jnp.sum(dz * _shift_down(x, prev, 2), axis=0, keepdims=True)
            acc_ref[1:2, :] += jnp.sum(dz * _shift_down(x, prev, 1), axis=0, keepdims=True)
            acc_ref[2:3, :] += jnp.sum(dz * x, axis=0, keepdims=True)
            acc_ref[3:4, :] += jnp.sum(dz, axis=0, keepdims=True)

    def blk(off):
        return pl.BlockSpec((ts, CONV_TC), lambda c, i: (i, c + off))

    def nxt8(off):
        return pl.BlockSpec((SUBLANES, CONV_TC), lambda c, i: (jnp.minimum((i + 1) * per8, last8), c + off))

    def prv8(off):
        return pl.BlockSpec((SUBLANES, CONV_TC), lambda c, i: (jnp.maximum(i * per8 - 1, 0), c + off))

    def wspec(off):
        return pl.BlockSpec((3, CONV_TC), lambda c, i: (0, c + off))

    acc = pl.BlockSpec((SUBLANES, CONV_TC), lambda c, i: (0, c))
    return pl.pallas_call(
        body, name=name, grid=(nct, nrow),
        in_specs=[blk(0), blk(0), blk(0), nxt8(0), nxt8(0), nxt8(0), blk(0), blk(nct), prv8(0), prv8(nct),
                  wspec(0), wspec(nct)],
        out_specs=[blk(0), blk(0), acc, acc],
        out_shape=[jax.ShapeDtypeStruct((s, D_FF), BF16)] * 2 + [jax.ShapeDtypeStruct((SUBLANES, D_FF), F32)] * 2,
        compiler_params=_cparams(("parallel", "arbitrary")),
    )(da, zg, zv, da, zg, zv, u, u, u, u, conv_w, conv_w)


def _peer(k):
    x, y, c = lax.axis_index("x"), lax.axis_index("y"), lax.axis_index("c")
    px = 1 - x if k & 4 else x
    py = 1 - y if k & 2 else y
    pc = 1 - c if k & 1 else c
    return (px, py, pc), 4 * px + 2 * py + pc


def _exchange(bufs, *, name, gather):
    n = len(bufs)
    npeer = N_DEV - 1

    def body(*refs):
        srcs, outs = refs[:n], refs[n:2 * n]
        send_sems, recv_sems, local_sems = refs[2 * n:]
        _, me = _peer(0)
        mine = [src if gather else src.at[me] for src in srcs]
        local = [pltpu.make_async_copy(mine[p], outs[p].at[me], local_sems.at[p]) for p in range(n)]
        for cp in local:
            cp.start()
        sends = []
        for k in range(1, N_DEV):
            dev, idx = _peer(k)
            for p in range(n):
                cp = pltpu.make_async_remote_copy(
                    src_ref=srcs[p] if gather else srcs[p].at[idx], dst_ref=outs[p].at[me],
                    send_sem=send_sems.at[p * npeer + k - 1], recv_sem=recv_sems.at[p * npeer + k - 1],
                    device_id=dev, device_id_type=pl.DeviceIdType.MESH)
                cp.start()
                sends.append(cp)
        for k in range(1, N_DEV):
            dev, idx = _peer(k)
            for p in range(n):
                pltpu.make_async_remote_copy(
                    src_ref=mine[p], dst_ref=outs[p].at[idx],
                    send_sem=send_sems.at[p * npeer + k - 1], recv_sem=recv_sems.at[p * npeer + k - 1],
                    device_id=dev, device_id_type=pl.DeviceIdType.MESH).wait_recv()
        for cp in sends:
            cp.wait_send()
        for cp in local:
            cp.wait()

    any_spec = pl.BlockSpec(memory_space=pl.ANY)
    return pl.pallas_call(
        body, name=name,
        in_specs=[any_spec] * n, out_specs=[any_spec] * n,
        out_shape=[jax.ShapeDtypeStruct((N_DEV,) + b.shape[-2:], b.dtype) for b in bufs],
        scratch_shapes=[pltpu.SemaphoreType.DMA((n * npeer,)), pltpu.SemaphoreType.DMA((n * npeer,)),
                        pltpu.SemaphoreType.DMA((n,))],
    )(*bufs)


def _gather_two_level(buf, *, name):
    def body(src, out, send_sems, recv_sems, local_sem):
        x, y, c = lax.axis_index("x"), lax.axis_index("y"), lax.axis_index("c")
        me, sibling = (x, y, c), (x, y, 1 - c)
        chips = [(1 - x, y), (x, 1 - y), (1 - x, 1 - y)]

        def slab(px, py, pc):
            return out.at[4 * px + 2 * py + pc]

        def copy(k, block, to, from_src=False):
            return pltpu.make_async_remote_copy(
                src_ref=src if from_src else slab(*block), dst_ref=slab(*block),
                send_sem=send_sems.at[k], recv_sem=recv_sems.at[k],
                device_id=to, device_id_type=pl.DeviceIdType.MESH)

        mine = pltpu.make_async_copy(src, slab(*me), local_sem)
        mine.start()
        first = [copy(0, me, sibling, True)] + [copy(1 + j, me, (*chip, c), True) for j, chip in enumerate(chips)]
        for cp in first:
            cp.start()
        passed = [copy(4 + j, (*chip, c), sibling) for j, chip in enumerate(chips)]
        for j, chip in enumerate(chips):
            copy(1 + j, (*chip, c), me).wait_recv()
            passed[j].start()
        copy(0, sibling, me).wait_recv()
        for j, chip in enumerate(chips):
            copy(4 + j, (*chip, 1 - c), me).wait_recv()
        for cp in first + passed:
            cp.wait_send()
        mine.wait()

    any_spec = pl.BlockSpec(memory_space=pl.ANY)
    return pl.pallas_call(
        body, name=name, in_specs=[any_spec], out_specs=any_spec,
        out_shape=jax.ShapeDtypeStruct((N_DEV,) + buf.shape, buf.dtype),
        scratch_shapes=[pltpu.SemaphoreType.DMA((N_DEV - 1,)), pltpu.SemaphoreType.DMA((N_DEV - 1,)),
                        pltpu.SemaphoreType.DMA],
    )(buf)


_HBM = pl.BlockSpec(memory_space=pltpu.HBM)
_SEM = pl.BlockSpec(memory_space=pltpu.SEMAPHORE)
_EFFECT = pltpu.SideEffectType.DATAFLOW_SIDE_EFFECTING


def _split_copy(srcs, lands, send_sems, recv_sems, gather, k, p):
    _, me = _peer(0)
    dev, idx = _peer(k)
    sem = p * (N_DEV - 1) + k - 1
    return pltpu.make_async_remote_copy(
        src_ref=srcs[p] if gather else srcs[p].at[idx], dst_ref=lands[p].at[me],
        send_sem=send_sems.at[sem], recv_sem=recv_sems.at[sem],
        device_id=dev, device_id_type=pl.DeviceIdType.MESH)


def _exchange_start(bufs, after, *, name, gather):
    n = len(bufs)
    nsem = n * (N_DEV - 1)

    def body(*refs):
        srcs, lands = refs[:n], refs[n:2 * n]
        send_sems, recv_sems = refs[2 * n + 1], refs[2 * n + 2]
        token = refs[-1]
        for k in range(1, N_DEV):
            for p in range(n):
                _split_copy(srcs, lands, send_sems, recv_sems, gather, k, p).start()
        token[...] = jnp.zeros_like(token)

    hbm = lambda a: pltpu.with_memory_space_constraint(a, pltpu.HBM)
    lands = [lax.empty((N_DEV,) + b.shape[-2:], b.dtype) for b in bufs]
    mem = [pltpu.HBM(b.shape, b.dtype) for b in bufs] + [pltpu.HBM(l.shape, l.dtype) for l in lands]
    outs = pl.pallas_call(
        body, name=name,
        in_specs=[_HBM] * (2 * n) + [pl.BlockSpec(memory_space=pl.ANY)],
        out_specs=[_SEM, _SEM] + [_HBM] * (2 * n) + [pl.BlockSpec(memory_space=pltpu.VMEM)],
        out_shape=[pltpu.SemaphoreType.DMA((nsem,)), pltpu.SemaphoreType.DMA((nsem,))] + mem
        + [jax.ShapeDtypeStruct((SUBLANES, LANES), F32)],
        input_output_aliases={p: 2 + p for p in range(2 * n)},
        compiler_params=pltpu.CompilerParams(has_side_effects=_EFFECT),
    )(*[hbm(b) for b in bufs], *[hbm(l) for l in lands], after)
    return (outs[0], outs[1], outs[2:2 + n], outs[2 + n:2 + 2 * n]), outs[-1]


def _exchange_wait(handle, after, *, name, gather):
    send_sems, recv_sems, srcs, lands = handle
    n = len(srcs)

    def body(*refs):
        src_refs, land_refs = refs[:n], refs[n:2 * n]
        send_ref, recv_ref = refs[2 * n], refs[2 * n + 1]
        for k in range(1, N_DEV):
            for p in range(n):
                cp = _split_copy(src_refs, land_refs, send_ref, recv_ref, gather, k, p)
                cp.wait_send()
                cp.wait_recv()

    mem = [pltpu.HBM(b.shape, b.dtype) for b in srcs] + [pltpu.HBM(l.shape, l.dtype) for l in lands]
    outs = pl.pallas_call(
        body, name=name,
        in_specs=[_HBM] * (2 * n) + [_SEM, _SEM, pl.BlockSpec(memory_space=pl.ANY)],
        out_specs=[_HBM] * (2 * n), out_shape=mem,
        input_output_aliases={p: p for p in range(2 * n)},
        compiler_params=pltpu.CompilerParams(has_side_effects=_EFFECT),
    )(*srcs, *lands, send_sems, recv_sems, after)
    return outs[n:]


def _with_own(landed, own, me):
    return lax.dynamic_update_slice(landed, own[None], (me, 0, 0))


def _adamw(parts, w, m, v, *, name):
    rows, width = w.shape
    tr = _rows(rows, max(16, (128 * 1024) // width), mult=16)

    def body(p_ref, w_ref, m_ref, v_ref, g_ref, d_ref, nm_ref, nv_ref):
        g = p_ref[0].astype(F32)
        for k in range(1, N_DEV):
            g = g + p_ref[k].astype(F32)
        mn = ADAM_B1 * m_ref[...] + (1.0 - ADAM_B1) * g
        vn = ADAM_B2 * v_ref[...] + (1.0 - ADAM_B2) * jnp.square(g)
        m_hat = mn / (1.0 - ADAM_B1 ** ADAM_STEP)
        v_hat = vn / (1.0 - ADAM_B2 ** ADAM_STEP)
        g_ref[...] = g
        d_ref[...] = -ADAM_LR * (m_hat / (jnp.sqrt(v_hat) + ADAM_EPS) + ADAM_WD * w_ref[...])
        nm_ref[...] = mn
        nv_ref[...] = vn

    row = pl.BlockSpec((tr, width), lambda i: (i, 0))
    return pl.pallas_call(
        body, name=name, grid=(rows // tr,),
        in_specs=[pl.BlockSpec((N_DEV, tr, width), lambda i: (0, i, 0)), row, row, row],
        out_specs=[row] * 4, out_shape=[jax.ShapeDtypeStruct((rows, width), F32)] * 4,
        compiler_params=_cparams(("parallel",)),
    )(parts, w, m, v)


def _pack_replicated(blocks):
    flat = jnp.concatenate([blocks[name].reshape(-1).astype(F32) for name, _, _ in REPLICATED])
    flat = jnp.pad(flat, (0, PACK_ROWS * PACK_W - PACK_USED))
    return flat.reshape(PACK_ROWS, PACK_W)


def _unpack_replicated(buf):
    flat = buf.reshape(-1)
    return {name: flat[off:off + n].reshape(1, n) for name, (off, n) in PACK_TABLE.items()}


def _join_shards(seg, shape, axis):
    return seg.reshape(shape) if axis == 0 else seg.transpose(1, 0, 2).reshape(shape)


def _split_shards(g, shape, axis):
    r, c = shape
    if axis == 0:
        return g.reshape(N_DEV, r // N_DEV, c)
    return g.reshape(r, N_DEV, c // N_DEV).transpose(1, 0, 2)


def _to_residues(a, d):
    s, c = a.shape
    return a.reshape(s // d, d, c).transpose(1, 0, 2).reshape(s, c)


def _from_residues(a, d):
    s, c = a.shape
    return a.reshape(d, s // d, c).transpose(1, 0, 2).reshape(s, c)


def _pad_heads(w, heads, width, lo, hi):
    r = w.shape[0]
    w = w.reshape(r, heads, width)[:, :, lo:hi]
    w = jnp.pad(w, ((0, 0), (0, 0), (0, LANES - (hi - lo))))
    return w.reshape(r, heads * LANES)


class _NoOverlap:
    start_token = None

    def late_weights(self, w, after):
        return w

    def early_grads(self, names, grads):
        return None


def _after(vec, token):
    return vec if token is None else vec + token[0:1, 0:1]


def _local_step(x, mem, positions, target, w, hooks=_NoOverlap()):
    s = x.shape[0]
    bf = lambda a: a.astype(BF16)

    w_in = w["w_in"]
    kr_cols = jnp.pad(w_in[:, OFF_KV:OFF_KR], ((0, 0), (MLA_NOPE, LANES - MLA_QK_DIM)))
    w_a = bf(jnp.concatenate([w_in[:, :OFF_KV], kr_cols], axis=1))
    w_dm = bf(w_in[:, OFF_KR:OFF_MEMQ])
    w_g = bf(w_in[:, OFF_MEMQ:])
    w_in_t = jnp.concatenate([w_a, w_dm, w_g], axis=1).T
    tabs = _rope_tables(positions)

    h = _rms_fwd(x, _after(w["g_pre_mix"], hooks.start_token), name="rms_pre_mix", out_dtype=BF16)
    p_a = _matmul(h, w_a, name="proj_a")
    p_dm = _matmul(h, w_dm, name="proj_dm", out_dtype=BF16)
    p_g = _matmul(h, w_g, name="proj_gate")
    c_q, c_kv, kr = p_a[:, :OFF_Q], p_a[:, OFF_Q:OFF_KV], p_a[:, OFF_KV:]

    w = hooks.late_weights(w, p_g)
    wq = bf(_pad_heads(w["w_uq"], MLA_HEADS, MLA_QK_DIM, 0, MLA_QK_DIM))
    wk = bf(_pad_heads(w["w_ukv"], MLA_HEADS, MLA_NOPE + MLA_V, 0, MLA_NOPE))
    wv = bf(_pad_heads(w["w_ukv"], MLA_HEADS, MLA_NOPE + MLA_V, MLA_NOPE, MLA_NOPE + MLA_V))
    w_mkv = bf(w["w_mem_kv"])
    wb_mla = bf(jnp.pad(w["w_br_mla"].reshape(MLA_HEADS, MLA_V, D_MODEL),
                        ((0, 0), (0, LANES - MLA_V), (0, 0))).reshape(MLA_HEADS * LANES, D_MODEL))
    wb_dil, wb_mem, w_o = bf(w["w_br_dil"]), bf(w["w_br_mem"]), bf(w["w_o"])
    w_up, w_down = bf(w["w_ffn_up"]), bf(w["w_ffn_down"])
    slopes = [jnp.asarray(sl, F32) for sl in DIL_SLOPES]
    mla_scale = MLA_QK_DIM ** -0.5
    mem_scale = LANES ** -0.5
    MQ = 3 * DIL_HEADS

    qn = _rms_fwd(c_q, w["mla_q_norm"], name="rms_q", out_dtype=BF16)
    kvn = _rms_fwd(c_kv, w["mla_kv_norm"], name="rms_kv", out_dtype=BF16)
    q_raw = _matmul(qn, wq, name="mla_q_up")
    k_raw = _matmul(kvn, wk, name="mla_k_up")
    v_f = _matmul(kvn, wv, name="mla_v_up", out_dtype=BF16)
    q_f = _rope_fwd(q_raw, tabs, name="rope_q", scale=mla_scale * LOG2E)
    k_f = _rope_fwd(k_raw, tabs, name="rope_k", scale=1.0, add=kr)
    o_mla, lse_mla = _causal_fwd(q_f, k_f, v_f, name="mla_fwd", heads=MLA_HEADS, ones_lane=MLA_V)

    dil_in, dil_o, dil_lse = [], [], []
    for g, (_, d) in enumerate(DIL_PAIRS):
        if d == 1:
            arrs, offs = (p_dm, p_dm, p_dm), (4 * g, DIL_HEADS + 4 * g, 2 * DIL_HEADS + 4 * g)
        else:
            arrs = tuple(_to_residues(p_dm[:, (t * DIL_HEADS + 4 * g) * LANES:(t * DIL_HEADS + 4 * g + 4) * LANES], d)
                         for t in range(3))
            offs = (0, 0, 0)
        o_g, lse_g = _band_fwd(*arrs, slopes[g], name=f"dil_fwd_{g}", dilation=d,
                               qoff=offs[0], koff=offs[1], voff=offs[2])
        dil_in.append((arrs, offs))
        dil_o.append(_from_residues(o_g, d))
        dil_lse.append(_from_residues(lse_g, d))
    y_dil = _mix_fwd(dil_o, dil_lse, name="dil_mix")

    memn = _rms_fwd(mem, w["g_mem"], name="rms_mem", out_dtype=BF16)
    kv_m = _matmul(memn, w_mkv, name="mem_kv", out_dtype=BF16)
    memat = dict(heads=MEM_HEADS, qoff=MQ, koff=0, voff=MEM_HEADS, causal=False, scale=mem_scale, tq=512, tk=256)
    o_mem, lse_mem = _flash_fwd(p_dm, kv_m, kv_m, name="mem_fwd", **memat)

    b_mla = _matmul(o_mla, wb_mla, name="br_mla")
    b_dil = _matmul(y_dil, wb_dil, name="br_dil")
    b_mem = _matmul(o_mem, wb_mem, name="br_mem")
    merged = _gate_fwd(p_g, w["b_gate"], (b_mla, b_dil, b_mem), name="gate_fwd")
    z1 = _matmul(merged, w_o, name="out_proj")
    x1 = _rms_fwd(z1, w["g_post_mix"], name="rms_post_mix", out_dtype=F32, add=x)
    h2 = _rms_fwd(x1, w["g_pre_ffn"], name="rms_pre_ffn", out_dtype=BF16)
    u = _matmul(h2, w_up, name="ffn_up")
    zg, zv, act = _conv_fwd(u, w["conv_w"], w["conv_b"], name="conv_fwd")
    f = _matmul(act, w_down, name="ffn_down")
    dy, df, g_post_ffn_grad, sq = _loss_step(x1, f, target, w["g_post_ffn"], name="loss")
    loss = 0.5 * jnp.sum(sq) / D_MODEL

    grads = {}
    grads["g_post_ffn"] = g_post_ffn_grad
    da = _matmul(df, w_down.T, name="ffn_down_dx")
    grads["w_ffn_down"] = _matmul(act, df, name="ffn_down_dw", trans_a=True)
    du_g, du_v, cacc_g, cacc_v = _conv_bwd(da, zg, zv, u, w["conv_w"], name="conv_bwd")
    grads["conv_w"] = jnp.concatenate([cacc_g[0:3], cacc_v[0:3]], axis=1)
    grads["conv_b"] = jnp.concatenate([cacc_g[3:4], cacc_v[3:4]], axis=1)
    w_up_t = w_up.T
    dh2 = _matmul(du_g, w_up_t[:D_FF], name="ffn_up_dx_gate")
    dh2 = _matmul(du_v, w_up_t[D_FF:], name="ffn_up_dx_val", add=dh2)
    grads["w_ffn_up"] = jnp.concatenate([_matmul(h2, du_g, name="ffn_up_dw_gate", trans_a=True),
                                         _matmul(h2, du_v, name="ffn_up_dw_val", trans_a=True)], axis=1)
    tok = hooks.early_grads(("w_ffn_down", "w_ffn_up", "conv_w"), grads)
    dx1, grads["g_pre_ffn"] = _rms_bwd(x1, _after(w["g_pre_ffn"], tok), dh2, name="rms_pre_ffn_bwd",
                                       out_dtype=F32, add=dy)
    dz1, grads["g_post_mix"] = _rms_bwd(z1, w["g_post_mix"], dx1, name="rms_post_mix_bwd", out_dtype=BF16)
    dmerged = _matmul(dz1, w_o.T, name="out_proj_dx")
    grads["w_o"] = _matmul(merged, dz1, name="out_proj_dw", trans_a=True)
    db_mla, db_dil, db_mem, dgp, grads["b_gate"] = _gate_bwd(
        dmerged, p_g, w["b_gate"], (b_mla, b_dil, b_mem), name="gate_bwd")

    do_mla = _matmul(db_mla, wb_mla.T, name="br_mla_dx", out_dtype=BF16)
    g_wb_mla = _matmul(o_mla, db_mla, name="br_mla_dw", trans_a=True)
    grads["w_br_mla"] = g_wb_mla.reshape(MLA_HEADS, LANES, D_MODEL)[:, :MLA_V].reshape(MLA_HEADS * MLA_V, D_MODEL)
    delta_mla = _row_dot(do_mla, o_mla, name="mla_delta")
    dq_f, dk_f, dv_f = _causal_bwd(q_f, k_f, v_f, do_mla, lse_mla, delta_mla, name="mla_bwd", heads=MLA_HEADS)
    dq_raw = _rope_bwd(dq_f, tabs, name="rope_q_bwd", scale=mla_scale, with_add=False)
    dk_raw, dkr = _rope_bwd(dk_f, tabs, name="rope_k_bwd", scale=1.0, with_add=True)
    dqn = _matmul(dq_raw, wq.T, name="mla_q_up_dx")
    g_wq = _matmul(qn, dq_raw, name="mla_q_up_dw", trans_a=True)
    grads["w_uq"] = g_wq.reshape(MLA_Q_RANK, MLA_HEADS, LANES)[:, :, :MLA_QK_DIM].reshape(MLA_Q_RANK, -1)
    dkvn = _matmul(dk_raw, wk.T, name="mla_k_up_dx")
    dkvn = _matmul(dv_f, wv.T, name="mla_v_up_dx", add=dkvn)
    g_wk = _matmul(kvn, dk_raw, name="mla_k_up_dw", trans_a=True).reshape(MLA_KV_RANK, MLA_HEADS, LANES)
    g_wv = _matmul(kvn, dv_f, name="mla_v_up_dw", trans_a=True).reshape(MLA_KV_RANK, MLA_HEADS, LANES)
    grads["w_ukv"] = jnp.concatenate([g_wk[:, :, :MLA_NOPE], g_wv[:, :, :MLA_V]], axis=2).reshape(MLA_KV_RANK, -1)
    dc_q, grads["mla_q_norm"] = _rms_bwd(c_q, w["mla_q_norm"], dqn, name="rms_q_bwd", out_dtype=BF16)
    dc_kv, grads["mla_kv_norm"] = _rms_bwd(c_kv, w["mla_kv_norm"], dkvn, name="rms_kv_bwd", out_dtype=BF16)

    dy_dil = _matmul(db_dil, wb_dil.T, name="br_dil_dx")
    grads["w_br_dil"] = _matmul(y_dil, db_dil, name="br_dil_dw", trans_a=True)
    mix = _mix_bwd(dy_dil, dil_o, dil_lse, name="dil_mix_bwd")
    d_dil = [[None] * 3 for _ in range(3)]
    for g, (_, d) in enumerate(DIL_PAIRS):
        arrs, offs = dil_in[g]
        do_g, dl_g, lse_g = mix[g], mix[3 + g], dil_lse[g]
        if d != 1:
            do_g, dl_g, lse_g = _to_residues(do_g, d), _to_residues(dl_g, d), _to_residues(lse_g, d)
        dq_g, dk_g, dv_g = _band_bwd(*arrs, do_g, lse_g, dl_g, slopes[g], name=f"dil_bwd_{g}", dilation=d,
                                     qoff=offs[0], koff=offs[1], voff=offs[2])
        for t, a in enumerate((dq_g, dk_g, dv_g)):
            d_dil[t][g] = a if d == 1 else _from_residues(a, d)

    do_mem = _matmul(db_mem, wb_mem.T, name="br_mem_dx", out_dtype=BF16)
    grads["w_br_mem"] = _matmul(o_mem, db_mem, name="br_mem_dw", trans_a=True)
    delta_mem = _row_dot(do_mem, o_mem, name="mem_delta")
    dq_mem = _flash_bwd_dq(p_dm, kv_m, kv_m, do_mem, lse_mem, delta_mem, name="mem_bwd_dq", out_dtype=BF16, **memat)
    dk_mem, dv_mem = _flash_bwd_dkv(p_dm, kv_m, kv_m, do_mem, lse_mem, delta_mem, name="mem_bwd_dkv",
                                    dk_dtype=BF16, dv_dtype=BF16, **memat)
    dkv_m = jnp.concatenate([dk_mem, dv_mem], axis=1)
    dmemn = _matmul(dkv_m, w_mkv.T, name="mem_kv_dx")
    grads["w_mem_kv"] = _matmul(memn, dkv_m, name="mem_kv_dw", trans_a=True)
    _, grads["g_mem"] = _rms_bwd(mem, w["g_mem"], dmemn, name="rms_mem_bwd", out_dtype=BF16)

    tok = hooks.early_grads(("w_o", "w_br_mla", "w_br_dil", "w_br_mem", "w_uq", "w_ukv", "w_mem_kv"), grads)
    if tok is not None:
        dkr = dkr + tok[0:1, 0:1]
    dp_all = jnp.concatenate([dc_q, dc_kv, bf(dkr)] + d_dil[0] + d_dil[1] + d_dil[2] + [dq_mem, dgp], axis=1)
    g_in = _matmul(h, dp_all, name="proj_dw", trans_a=True)
    grads["w_in"] = jnp.concatenate(
        [g_in[:, :OFF_KV], g_in[:, OFF_KV + MLA_NOPE:OFF_KV + MLA_QK_DIM], g_in[:, N_A:]], axis=1)
    tok = hooks.early_grads(("w_in",), grads)
    dh = _matmul(dp_all, w_in_t, name="proj_dx", after=tok)
    dx, grads["g_pre_mix"] = _rms_bwd(x, w["g_pre_mix"], dh, name="rms_pre_mix_bwd", out_dtype=F32, add=dx1)
    return loss, dx, grads


def kernel(x, mem, positions, g_pre_mix, w_in, b_gate, mla_q_norm, w_uq, mla_kv_norm, w_ukv, g_mem, w_mem_kv, w_br_mla, w_br_dil, w_br_mem, w_o, g_post_mix, g_pre_ffn, w_ffn_up, conv_w, conv_b, w_ffn_down, g_post_ffn, loss_target, m_g_pre_mix, m_w_in, m_b_gate, m_mla_q_norm, m_w_uq, m_mla_kv_norm, m_w_ukv, m_g_mem, m_w_mem_kv, m_w_br_mla, m_w_br_dil, m_w_br_mem, m_w_o, m_g_post_mix, m_g_pre_ffn, m_w_ffn_up, m_conv_w, m_conv_b, m_w_ffn_down, m_g_post_ffn, v_g_pre_mix, v_w_in, v_b_gate, v_mla_q_norm, v_w_uq, v_mla_kv_norm, v_w_ukv, v_g_mem, v_w_mem_kv, v_w_br_mla, v_w_br_dil, v_w_br_mem, v_w_o, v_g_post_mix, v_g_pre_ffn, v_w_ffn_up, v_conv_w, v_conv_b, v_w_ffn_down, v_g_post_ffn):
    local = dict(g_pre_mix=g_pre_mix, w_in=w_in, b_gate=b_gate, mla_q_norm=mla_q_norm, w_uq=w_uq,
                 mla_kv_norm=mla_kv_norm, w_ukv=w_ukv, g_mem=g_mem, w_mem_kv=w_mem_kv, w_br_mla=w_br_mla,
                 w_br_dil=w_br_dil, w_br_mem=w_br_mem, w_o=w_o, g_post_mix=g_post_mix, g_pre_ffn=g_pre_ffn,
                 w_ffn_up=w_ffn_up, conv_w=conv_w, conv_b=conv_b, w_ffn_down=w_ffn_down, g_post_ffn=g_post_ffn)
    mom_m = dict(g_pre_mix=m_g_pre_mix, w_in=m_w_in, b_gate=m_b_gate, mla_q_norm=m_mla_q_norm, w_uq=m_w_uq,
                 mla_kv_norm=m_mla_kv_norm, w_ukv=m_w_ukv, g_mem=m_g_mem, w_mem_kv=m_w_mem_kv, w_br_mla=m_w_br_mla,
                 w_br_dil=m_w_br_dil, w_br_mem=m_w_br_mem, w_o=m_w_o, g_post_mix=m_g_post_mix,
                 g_pre_ffn=m_g_pre_ffn, w_ffn_up=m_w_ffn_up, conv_w=m_conv_w, conv_b=m_conv_b,
                 w_ffn_down=m_w_ffn_down, g_post_ffn=m_g_post_ffn)
    mom_v = dict(g_pre_mix=v_g_pre_mix, w_in=v_w_in, b_gate=v_b_gate, mla_q_norm=v_mla_q_norm, w_uq=v_w_uq,
                 mla_kv_norm=v_mla_kv_norm, w_ukv=v_w_ukv, g_mem=v_g_mem, w_mem_kv=v_w_mem_kv, w_br_mla=v_w_br_mla,
                 w_br_dil=v_w_br_dil, w_br_mem=v_w_br_mem, w_o=v_w_o, g_post_mix=v_g_post_mix,
                 g_pre_ffn=v_g_pre_ffn, w_ffn_up=v_w_ffn_up, conv_w=v_conv_w, conv_b=v_conv_b,
                 w_ffn_down=v_w_ffn_down, g_post_ffn=v_g_post_ffn)

    me = 4 * lax.axis_index("x") + 2 * lax.axis_index("y") + lax.axis_index("c")
    spec = {name: (shape, axis) for name, shape, axis in SHARDED}
    wire = lambda name: F32 if name == "conv_w" else BF16
    shard = {name: local[name][0].astype(wire(name)) for name in spec}
    slab = lambda name, grads: _split_shards(grads[name].astype(wire(name)), *spec[name])

    w_in_all = _gather_two_level(shard["w_in"], name="gather_w_in")
    late = tuple(name for name in spec if name != "w_in")
    late_handle, late_token = _exchange_start([shard[n] for n in late], w_in_all, name="gather_rest_start",
                                              gather=True)
    full = {"w_in": _join_shards(w_in_all, *spec["w_in"])}
    for name, _, _ in REPLICATED:
        full[name] = local[name].reshape(1, -1)

    pending = []

    class Overlap:
        start_token = late_token

        def late_weights(self, w, after):
            landed = _exchange_wait(late_handle, after, name="gather_rest_wait", gather=True)
            w = dict(w)
            for name, buf in zip(late, landed):
                w[name] = _join_shards(_with_own(buf, shard[name], me), *spec[name])
            return w

        def early_grads(self, names, grads):
            slabs = [slab(name, grads) for name in names]
            handle, token = _exchange_start(slabs, slabs[0], name="grads_start_" + names[0], gather=False)
            pending.append((names, slabs, handle))
            return token

    loss, dx, grads = _local_step(x[0], mem[0], positions[0], loss_target[0], full, Overlap())

    parts = {}
    for names, slabs, handle in pending:
        landed = _exchange_wait(handle, dx, name="grads_wait_" + names[0], gather=False)
        for name, own, buf in zip(names, slabs, landed):
            parts[name] = _with_own(buf, lax.dynamic_index_in_dim(own, me, 0, keepdims=False), me)
    rep_parts = _exchange([_pack_replicated(grads)], name="gather_replicated_grads", gather=True)[0]

    results = {}
    for name in spec:
        res = _adamw(parts[name], local[name][0], mom_m[name][0], mom_v[name][0], name="adamw_" + name)
        results[name] = [r[None] for r in res]
    rep = _adamw(rep_parts, _pack_replicated(local), _pack_replicated(mom_m), _pack_replicated(mom_v),
                 name="adamw_replicated")
    for i, buf in enumerate(rep):
        for name, val in _unpack_replicated(buf).items():
            results.setdefault(name, [None] * 4)[i] = val

    loss = lax.psum(loss, ("x", "y", "c"))
    outs = [loss, dx[None]]
    for i in range(4):
        outs.extend(results[name][i] for name in PARAM_NAMES)
    return tuple(outs)
```
